```python
import jax, jax.numpy as jnp
from jax import lax
import numpy as np

D_MODEL = 2048
BATCH = 8
SEQ = 2048
DEPTH = 1

GRID_W = 64
CTX_LEN = 256
HEAD_DIM = 128
N_Q_HEADS = 16
N_KV_HEADS = 4
Q_GROUP = N_Q_HEADS // N_KV_HEADS
ATTN_WIDTH = N_Q_HEADS * HEAD_DIM
KV_WIDTH = N_KV_HEADS * HEAD_DIM
D_RNN = D_MODEL
N_RNN_BLOCKS = 16
RNN_BLOCK = D_RNN // N_RNN_BLOCKS
CONV_WIDTH = 4
CONV_PAD_LO = 1
CONV_PAD_HI = 2
LRU_C = 8.0
D_FF = 4 * D_MODEL
N_BRANCH = 2
Q_BLOCK = 128
ROPE_THETA = 10000.0
NORM_EPS = 1e-6
N_MOD = 6
N_IN = ATTN_WIDTH + 2 * KV_WIDTH + 2 * D_RNN + N_BRANCH * D_MODEL
IN_SPLITS = (ATTN_WIDTH,
             ATTN_WIDTH + KV_WIDTH,
             ATTN_WIDTH + 2 * KV_WIDTH,
             ATTN_WIDTH + 2 * KV_WIDTH + D_RNN,
             ATTN_WIDTH + 2 * KV_WIDTH + 2 * D_RNN)

kernel_name = "hybrid_gqa_rglru_parallel_dit_block"


def rms_norm(x, g):
    xf = x.astype(jnp.float32)
    y = xf * lax.rsqrt(jnp.mean(xf * xf, axis=-1, keepdims=True) + NORM_EPS)
    return (y * g.astype(jnp.float32)).astype(x.dtype)


def modulate(h, shift, scale):
    return h * (1 + scale) + shift


def rope_tables(row_idx, col_idx):
    n_freq = HEAD_DIM // 4
    inv_freq = ROPE_THETA ** (-jnp.arange(n_freq, dtype=jnp.float32) / n_freq)
    ang = jnp.concatenate([row_idx.astype(jnp.float32)[:, None] * inv_freq,
                           col_idx.astype(jnp.float32)[:, None] * inv_freq], axis=-1)
    return jnp.cos(ang), jnp.sin(ang)


def apply_rope(x, cos, sin):
    xf = x.astype(jnp.float32).reshape(*x.shape[:-1], HEAD_DIM // 2, 2)
    x1, x2 = xf[..., 0], xf[..., 1]
    cs, sn = cos[None, :, None, :], sin[None, :, None, :]
    out = jnp.stack([x1 * cs - x2 * sn, x1 * sn + x2 * cs], axis=-1).reshape(x.shape)
    return out.astype(x.dtype)


def gqa_softmax(q_blk, k, v):
    s = jnp.einsum('bqhgd,bkhd->bhgqk', q_blk, k).astype(jnp.float32) * (HEAD_DIM ** -0.5)
    p = jax.nn.softmax(s, axis=-1)
    return jnp.einsum('bhgqk,bkhd->bqhgd', p.astype(v.dtype), v)


def latent_attention(q, k_lat, v_lat, k_ctx, v_ctx):
    b, s = q.shape[:2]
    k_all = jnp.concatenate([k_ctx, k_lat], axis=1)
    v_all = jnp.concatenate([v_ctx, v_lat], axis=1)
    nb = s // Q_BLOCK
    qb = q.reshape(b, nb, Q_BLOCK, N_KV_HEADS, Q_GROUP, HEAD_DIM).transpose(1, 0, 2, 3, 4, 5)
    o = lax.map(lambda q_blk: gqa_softmax(q_blk, k_all, v_all), qb)
    return o.transpose(1, 0, 2, 3, 4, 5).reshape(b, s, ATTN_WIDTH)


def context_attention(q_c, k_c, v_c):
    b, l = q_c.shape[:2]
    o = gqa_softmax(q_c.reshape(b, l, N_KV_HEADS, Q_GROUP, HEAD_DIM), k_c, v_c)
    return o.reshape(b, l, ATTN_WIDTH)


def depthwise_conv(x, w, bias):
    s = x.shape[1]
    xp = jnp.pad(x, ((0, 0), (CONV_PAD_LO, CONV_PAD_HI), (0, 0)))
    y = bias
    for k in range(CONV_WIDTH):
        y = y + xp[:, k:k + s] * w[k]
    return y


def rglru_coeffs(x, w_r, b_r, w_i, b_i, lam):
    b, s, _ = x.shape
    xb = x.reshape(b, s, N_RNN_BLOCKS, RNN_BLOCK)
    r = jax.nn.sigmoid(jnp.einsum('bshi,hij->bshj', xb, w_r).reshape(b, s, D_RNN) + b_r)
    i = jax.nn.sigmoid(jnp.einsum('bshi,hij->bshj', xb, w_i).reshape(b, s, D_RNN) + b_i)
    log_a = -LRU_C * r.astype(jnp.float32) * jax.nn.softplus(-lam.astype(jnp.float32))
    a = jnp.exp(log_a)
    mult = jnp.sqrt(-jnp.expm1(2.0 * log_a))
    return a, mult * (i * x).astype(jnp.float32)


def linear_scan(a, bx, reverse):
    def combine(e1, e2):
        a1, b1 = e1
        a2, b2 = e2
        return a1 * a2, a2 * b1 + b2
    _, h = lax.associative_scan(combine, (a, bx), axis=1, reverse=reverse)
    return h


def rglru_direction(x_ctx, x_lat, w_r, b_r, w_i, b_i, lam, reverse):
    a_c, bx_c = rglru_coeffs(x_ctx, w_r, b_r, w_i, b_i, lam)
    h_c = linear_scan(a_c, bx_c, reverse)
    idx = 0 if reverse else -1
    h0 = h_c[:, idx]
    a_l, bx_l = rglru_coeffs(x_lat, w_r, b_r, w_i, b_i, lam)
    first = -1 if reverse else 0
    bx_l = bx_l.at[:, first].add(a_l[:, first] * h0)
    h_l = linear_scan(a_l, bx_l, reverse)
    return h_c, h_l


def merge_branches(attn_o, rnn_h, xg, gl, w_o_attn, w_o_rnn, w_out):
    y_attn = attn_o @ w_o_attn
    y_rnn = (rnn_h * jax.nn.gelu(xg)) @ w_o_rnn
    g_attn, g_rnn = jnp.split(jax.nn.sigmoid(gl), N_BRANCH, axis=-1)
    return (g_attn * y_attn + g_rnn * y_rnn) @ w_out


def mixer(h_lat, h_ctx, cos, sin, w_in, q_gain, k_gain, conv_w, conv_b, w_rg, b_rg, w_ig, b_ig,
          lam, w_o_attn, w_o_rnn, w_out, update_ctx):
    b, s, _ = h_lat.shape
    l = h_ctx.shape[1]
    q, k, v, xr, xg, gl = jnp.split(h_lat @ w_in, IN_SPLITS, axis=-1)
    qc, kc, vc, xrc, xgc, glc = jnp.split(h_ctx @ w_in, IN_SPLITS, axis=-1)

    q = apply_rope(rms_norm(q.reshape(b, s, N_Q_HEADS, HEAD_DIM), q_gain), cos, sin)
    k = apply_rope(rms_norm(k.reshape(b, s, N_KV_HEADS, HEAD_DIM), k_gain), cos, sin)
    v = v.reshape(b, s, N_KV_HEADS, HEAD_DIM)
    kc = rms_norm(kc.reshape(b, l, N_KV_HEADS, HEAD_DIM), k_gain)
    vc = vc.reshape(b, l, N_KV_HEADS, HEAD_DIM)
    attn_lat = latent_attention(q, k, v, kc, vc)

    xr_lat = depthwise_conv(xr, conv_w, conv_b)
    xr_ctx = depthwise_conv(xrc, conv_w, conv_b)
    hc_f, hl_f = rglru_direction(xr_ctx, xr_lat, w_rg[0], b_rg[0], w_ig[0], b_ig[0], lam[0], False)
    hc_b, hl_b = rglru_direction(xr_ctx, xr_lat, w_rg[1], b_rg[1], w_ig[1], b_ig[1], lam[1], True)
    rnn_lat = (hl_f + hl_b).astype(h_lat.dtype)

    out_lat = merge_branches(attn_lat, rnn_lat, xg, gl, w_o_attn, w_o_rnn, w_out)
    out_ctx = None
    if update_ctx:
        qc = rms_norm(qc.reshape(b, l, N_Q_HEADS, HEAD_DIM), q_gain)
        attn_ctx = context_attention(qc, kc, vc)
        rnn_ctx = (hc_f + hc_b).astype(h_ctx.dtype)
        out_ctx = merge_branches(attn_ctx, rnn_ctx, xgc, glc, w_o_attn, w_o_rnn, w_out)
    return out_lat, out_ctx


def sq_relu_mlp(h, w_up, w_down):
    return jnp.square(jax.nn.relu(h @ w_up)) @ w_down


def _fwd_setup_inputs(seed: int = 0) -> dict:
    key = jax.random.key(seed)
    ks = jax.random.split(key, 24)
    f32 = jnp.float32
    nrm = lambda k, shape, scale: jax.random.normal(k, shape, f32) * scale
    a0 = jax.random.uniform(ks[17], (DEPTH, 2, D_RNN), f32, 0.9, 0.999)
    sig = a0 ** (1.0 / LRU_C)
    lru_lambda = jnp.log(sig) - jnp.log1p(-sig)
    return {
        "x": nrm(ks[0], (BATCH, SEQ, D_MODEL), 1.0),
        "c": nrm(ks[1], (BATCH, D_MODEL), 1.0),
        "ctx": nrm(ks[2], (BATCH, CTX_LEN, D_MODEL), 1.0),
        "c_ctx": nrm(ks[3], (D_MODEL,), 1.0),
        "w_mod": nrm(ks[4], (DEPTH, D_MODEL, N_MOD * D_MODEL), 0.5 * D_MODEL ** -0.5),
        "b_mod": nrm(ks[5], (DEPTH, N_MOD * D_MODEL), 0.01),
        "g_mix": 1.0 + nrm(ks[6], (DEPTH, D_MODEL), 0.1),
        "g_mlp": 1.0 + nrm(ks[7], (DEPTH, D_MODEL), 0.1),
        "w_in": nrm(ks[8], (DEPTH, D_MODEL, N_IN), D_MODEL ** -0.5),
        "q_gain": 1.0 + nrm(ks[9], (DEPTH, HEAD_DIM), 0.1),
        "k_gain": 1.0 + nrm(ks[10], (DEPTH, HEAD_DIM), 0.1),
        "conv_w": nrm(ks[11], (DEPTH, CONV_WIDTH, D_RNN), CONV_WIDTH ** -0.5),
        "conv_b": nrm(ks[12], (DEPTH, D_RNN), 0.01),
        "w_rg": nrm(ks[13], (DEPTH, 2, N_RNN_BLOCKS, RNN_BLOCK, RNN_BLOCK), RNN_BLOCK ** -0.5),
        "b_rg": nrm(ks[14], (DEPTH, 2, D_RNN), 0.01),
        "w_ig": nrm(ks[15], (DEPTH, 2, N_RNN_BLOCKS, RNN_BLOCK, RNN_BLOCK), RNN_BLOCK ** -0.5),
        "b_ig": nrm(ks[16], (DEPTH, 2, D_RNN), 0.01),
        "lru_lambda": lru_lambda,
        "w_o_attn": nrm(ks[18], (DEPTH, ATTN_WIDTH, D_MODEL), ATTN_WIDTH ** -0.5),
        "w_o_rnn": nrm(ks[19], (DEPTH, D_RNN, D_MODEL), D_RNN ** -0.5),
        "w_out": nrm(ks[20], (DEPTH, D_MODEL, D_MODEL), D_MODEL ** -0.5),
        "w_up": nrm(ks[21], (DEPTH, D_MODEL, D_FF), D_MODEL ** -0.5),
        "w_down": nrm(ks[22], (DEPTH, D_FF, D_MODEL), D_FF ** -0.5),
        "g_final": 1.0 + nrm(ks[23], (D_MODEL,), 0.1),
    }


def _fwd_reference(x, c, ctx, c_ctx, w_mod, b_mod, g_mix, g_mlp, w_in, q_gain, k_gain, conv_w, conv_b,
              w_rg, b_rg, w_ig, b_ig, lru_lambda, w_o_attn, w_o_rnn, w_out, w_up, w_down, g_final):
    n_lat = x.shape[1]
    rows = n_lat // GRID_W
    row_idx = jnp.repeat(jnp.arange(rows), GRID_W)
    col_idx = jnp.tile(jnp.arange(GRID_W), rows)
    cos, sin = rope_tables(row_idx, col_idx)

    for layer in range(DEPTH):
        update_ctx = layer < DEPTH - 1
        mod_lat = (jax.nn.silu(c) @ w_mod[layer] + b_mod[layer])[:, None, :]
        mod_ctx = jax.nn.silu(c_ctx) @ w_mod[layer] + b_mod[layer]
        sh_a, sc_a, ga_a, sh_f, sc_f, ga_f = jnp.split(mod_lat, N_MOD, axis=-1)
        csh_a, csc_a, cga_a, csh_f, csc_f, cga_f = jnp.split(mod_ctx, N_MOD, axis=-1)

        h_lat = modulate(rms_norm(x, g_mix[layer]), sh_a, sc_a)
        h_ctx = modulate(rms_norm(ctx, g_mix[layer]), csh_a, csc_a)
        mix_lat, mix_ctx = mixer(h_lat, h_ctx, cos, sin, w_in[layer], q_gain[layer], k_gain[layer],
                                 conv_w[layer], conv_b[layer], w_rg[layer], b_rg[layer],
                                 w_ig[layer], b_ig[layer], lru_lambda[layer], w_o_attn[layer],
                                 w_o_rnn[layer], w_out[layer], update_ctx)
        x = x + ga_a * mix_lat
        x = x + ga_f * sq_relu_mlp(modulate(rms_norm(x, g_mlp[layer]), sh_f, sc_f),
                                   w_up[layer], w_down[layer])
        if update_ctx:
            ctx = ctx + cga_a * mix_ctx
            ctx = ctx + cga_f * sq_relu_mlp(modulate(rms_norm(ctx, g_mlp[layer]), csh_f, csc_f),
                                            w_up[layer], w_down[layer])
    return rms_norm(x, g_final)


import jax as _jax
import jax.numpy as _jnp

TWIN_FORMAT = 'train_step'
FWD_PARAMS = ['x', 'c', 'ctx', 'c_ctx', 'w_mod', 'b_mod', 'g_mix', 'g_mlp', 'w_in', 'q_gain', 'k_gain', 'conv_w', 'conv_b', 'w_rg', 'b_rg', 'w_ig', 'b_ig', 'lru_lambda', 'w_o_attn', 'w_o_rnn', 'w_out', 'w_up', 'w_down', 'g_final']
TWIN_WEIGHTS = ['c_ctx', 'w_mod', 'b_mod', 'g_mix', 'g_mlp', 'w_in', 'q_gain', 'k_gain', 'conv_w', 'conv_b', 'w_rg', 'b_rg', 'w_ig', 'b_ig', 'lru_lambda', 'w_o_attn', 'w_o_rnn', 'w_out', 'w_up', 'w_down', 'g_final']
TWIN_DIFF_INPUT = 'x'
TWIN_INPUTS = ['x', 'c', 'ctx', 'c_ctx', 'w_mod', 'b_mod', 'g_mix', 'g_mlp', 'w_in', 'q_gain', 'k_gain', 'conv_w', 'conv_b', 'w_rg', 'b_rg', 'w_ig', 'b_ig', 'lru_lambda', 'w_o_attn', 'w_o_rnn', 'w_out', 'w_up', 'w_down', 'g_final', 'loss_target', 'm_c_ctx', 'm_w_mod', 'm_b_mod', 'm_g_mix', 'm_g_mlp', 'm_w_in', 'm_q_gain', 'm_k_gain', 'm_conv_w', 'm_conv_b', 'm_w_rg', 'm_b_rg', 'm_w_ig', 'm_b_ig', 'm_lru_lambda', 'm_w_o_attn', 'm_w_o_rnn', 'm_w_out', 'm_w_up', 'm_w_down', 'm_g_final', 'v_c_ctx', 'v_w_mod', 'v_b_mod', 'v_g_mix', 'v_g_mlp', 'v_w_in', 'v_q_gain', 'v_k_gain', 'v_conv_w', 'v_conv_b', 'v_w_rg', 'v_b_rg', 'v_w_ig', 'v_b_ig', 'v_lru_lambda', 'v_w_o_attn', 'v_w_o_rnn', 'v_w_out', 'v_w_up', 'v_w_down', 'v_g_final']
TWIN_OUTPUTS = ['loss', 'grad_x', 'grad_c_ctx', 'grad_w_mod', 'grad_b_mod', 'grad_g_mix', 'grad_g_mlp', 'grad_w_in', 'grad_q_gain', 'grad_k_gain', 'grad_conv_w', 'grad_conv_b', 'grad_w_rg', 'grad_b_rg', 'grad_w_ig', 'grad_b_ig', 'grad_lru_lambda', 'grad_w_o_attn', 'grad_w_o_rnn', 'grad_w_out', 'grad_w_up', 'grad_w_down', 'grad_g_final', 'delta_c_ctx', 'delta_w_mod', 'delta_b_mod', 'delta_g_mix', 'delta_g_mlp', 'delta_w_in', 'delta_q_gain', 'delta_k_gain', 'delta_conv_w', 'delta_conv_b', 'delta_w_rg', 'delta_b_rg', 'delta_w_ig', 'delta_b_ig', 'delta_lru_lambda', 'delta_w_o_attn', 'delta_w_o_rnn', 'delta_w_out', 'delta_w_up', 'delta_w_down', 'delta_g_final', 'new_m_c_ctx', 'new_m_w_mod', 'new_m_b_mod', 'new_m_g_mix', 'new_m_g_mlp', 'new_m_w_in', 'new_m_q_gain', 'new_m_k_gain', 'new_m_conv_w', 'new_m_conv_b', 'new_m_w_rg', 'new_m_b_rg', 'new_m_w_ig', 'new_m_b_ig', 'new_m_lru_lambda', 'new_m_w_o_attn', 'new_m_w_o_rnn', 'new_m_w_out', 'new_m_w_up', 'new_m_w_down', 'new_m_g_final', 'new_v_c_ctx', 'new_v_w_mod', 'new_v_b_mod', 'new_v_g_mix', 'new_v_g_mlp', 'new_v_w_in', 'new_v_q_gain', 'new_v_k_gain', 'new_v_conv_w', 'new_v_conv_b', 'new_v_w_rg', 'new_v_b_rg', 'new_v_w_ig', 'new_v_b_ig', 'new_v_lru_lambda', 'new_v_w_o_attn', 'new_v_w_o_rnn', 'new_v_w_out', 'new_v_w_up', 'new_v_w_down', 'new_v_g_final']
TWIN_LEAF_KINDS = {'loss': 'loss', 'grad_x': 'grad_x', 'grad_c_ctx': 'grad_w', 'grad_w_mod': 'grad_w', 'grad_b_mod': 'grad_w', 'grad_g_mix': 'grad_w', 'grad_g_mlp': 'grad_w', 'grad_w_in': 'grad_w', 'grad_q_gain': 'grad_w', 'grad_k_gain': 'grad_w', 'grad_conv_w': 'grad_w', 'grad_conv_b': 'grad_w', 'grad_w_rg': 'grad_w', 'grad_b_rg': 'grad_w', 'grad_w_ig': 'grad_w', 'grad_b_ig': 'grad_w', 'grad_lru_lambda': 'grad_w', 'grad_w_o_attn': 'grad_w', 'grad_w_o_rnn': 'grad_w', 'grad_w_out': 'grad_w', 'grad_w_up': 'grad_w', 'grad_w_down': 'grad_w', 'grad_g_final': 'grad_w', 'delta_c_ctx': 'delta_w', 'delta_w_mod': 'delta_w', 'delta_b_mod': 'delta_w', 'delta_g_mix': 'delta_w', 'delta_g_mlp': 'delta_w', 'delta_w_in': 'delta_w', 'delta_q_gain': 'delta_w', 'delta_k_gain': 'delta_w', 'delta_conv_w': 'delta_w', 'delta_conv_b': 'delta_w', 'delta_w_rg': 'delta_w', 'delta_b_rg': 'delta_w', 'delta_w_ig': 'delta_w', 'delta_b_ig': 'delta_w', 'delta_lru_lambda': 'delta_w', 'delta_w_o_attn': 'delta_w', 'delta_w_o_rnn': 'delta_w', 'delta_w_out': 'delta_w', 'delta_w_up': 'delta_w', 'delta_w_down': 'delta_w', 'delta_g_final': 'delta_w', 'new_m_c_ctx': 'new_m', 'new_m_w_mod': 'new_m', 'new_m_b_mod': 'new_m', 'new_m_g_mix': 'new_m', 'new_m_g_mlp': 'new_m', 'new_m_w_in': 'new_m', 'new_m_q_gain': 'new_m', 'new_m_k_gain': 'new_m', 'new_m_conv_w': 'new_m', 'new_m_conv_b': 'new_m', 'new_m_w_rg': 'new_m', 'new_m_b_rg': 'new_m', 'new_m_w_ig': 'new_m', 'new_m_b_ig': 'new_m', 'new_m_lru_lambda': 'new_m', 'new_m_w_o_attn': 'new_m', 'new_m_w_o_rnn': 'new_m', 'new_m_w_out': 'new_m', 'new_m_w_up': 'new_m', 'new_m_w_down': 'new_m', 'new_m_g_final': 'new_m', 'new_v_c_ctx': 'new_v', 'new_v_w_mod': 'new_v', 'new_v_b_mod': 'new_v', 'new_v_g_mix': 'new_v', 'new_v_g_mlp': 'new_v', 'new_v_w_in': 'new_v', 'new_v_q_gain': 'new_v', 'new_v_k_gain': 'new_v', 'new_v_conv_w': 'new_v', 'new_v_conv_b': 'new_v', 'new_v_w_rg': 'new_v', 'new_v_b_rg': 'new_v', 'new_v_w_ig': 'new_v', 'new_v_b_ig': 'new_v', 'new_v_lru_lambda': 'new_v', 'new_v_w_o_attn': 'new_v', 'new_v_w_o_rnn': 'new_v', 'new_v_w_out': 'new_v', 'new_v_w_up': 'new_v', 'new_v_w_down': 'new_v', 'new_v_g_final': 'new_v'}


def _forward(args):
    return _fwd_reference(*[args[k] for k in FWD_PARAMS])


def _output_shape():
    out = _jax.eval_shape(lambda: _forward(_fwd_setup_inputs(0)))
    return out.shape, out.dtype

N_MICROBATCH = 1
ADAM_LR = 0.001
ADAM_B1 = 0.9
ADAM_B2 = 0.999
ADAM_EPS = 1e-08
ADAM_WD = 0.01
ADAM_STEP = 10
PER_EXAMPLE_BATCH_AXIS = {'x': 0, 'c': 0, 'ctx': 0, 'loss_target': 0}
SHARED_INPUTS = []
_WEIGHT_DTYPES = {'c_ctx': _jnp.float32, 'w_mod': _jnp.float32, 'b_mod': _jnp.float32, 'g_mix': _jnp.float32, 'g_mlp': _jnp.float32, 'w_in': _jnp.float32, 'q_gain': _jnp.float32, 'k_gain': _jnp.float32, 'conv_w': _jnp.float32, 'conv_b': _jnp.float32, 'w_rg': _jnp.float32, 'b_rg': _jnp.float32, 'w_ig': _jnp.float32, 'b_ig': _jnp.float32, 'lru_lambda': _jnp.float32, 'w_o_attn': _jnp.float32, 'w_o_rnn': _jnp.float32, 'w_out': _jnp.float32, 'w_up': _jnp.float32, 'w_down': _jnp.float32, 'g_final': _jnp.float32}
MOMENT_SCALE = {'c_ctx': 7.148467e-03, 'w_mod': 1.205099e-01, 'b_mod': 2.223597e-01, 'g_mix': 2.756403e-02, 'g_mlp': 2.936125e-02, 'w_in': 2.135493e-02, 'q_gain': 3.302639e-03, 'k_gain': 3.391340e-03, 'conv_w': 4.374025e-02, 'conv_b': 1.327092e-01, 'w_rg': 2.138608e-03, 'b_rg': 3.270382e-03, 'w_ig': 4.299325e-03, 'b_ig': 8.320268e-03, 'lru_lambda': 8.606897e-03, 'w_o_attn': 8.306167e-03, 'w_o_rnn': 3.982971e-02, 'w_out': 3.995275e-02, 'w_up': 1.597246e-02, 'w_down': 4.891719e-02, 'g_final': 8.098548e+00}


def _to_microbatches(a, axis):
    t = _jnp.moveaxis(a, axis, 0)
    t = t.reshape((N_MICROBATCH, t.shape[0] // N_MICROBATCH) + t.shape[1:])
    return _jnp.moveaxis(t, 1, axis + 1)


def setup_inputs(seed: int = 0) -> dict:
    inp = _fwd_setup_inputs(seed)
    key = _jax.random.fold_in(_jax.random.key(seed), 7919)
    shape, _ = _output_shape()
    out = dict(inp)
    out["loss_target"] = _jax.random.normal(_jax.random.fold_in(key, 0), shape, _jnp.float32)
    for i, name in enumerate(TWIN_WEIGHTS):
        w = inp[name].astype(_jnp.float32)
        if MOMENT_SCALE is None:
            s = _jnp.sqrt(_jnp.mean(_jnp.square(w)) + 1e-30)
        else:
            s = MOMENT_SCALE[name]
        km, kv = _jax.random.split(_jax.random.fold_in(key, i + 1))
        out[name] = w
        out["m_" + name] = s * _jax.random.normal(km, w.shape, _jnp.float32)
        out["v_" + name] = (s * s) * _jax.random.uniform(kv, w.shape, _jnp.float32, 0.5, 1.5)
    if N_MICROBATCH > 1:
        for name, axis in PER_EXAMPLE_BATCH_AXIS.items():
            out[name] = _to_microbatches(out[name], axis)
    return {'x': out['x'], 'c': out['c'], 'ctx': out['ctx'], 'c_ctx': out['c_ctx'], 'w_mod': out['w_mod'], 'b_mod': out['b_mod'], 'g_mix': out['g_mix'], 'g_mlp': out['g_mlp'], 'w_in': out['w_in'], 'q_gain': out['q_gain'], 'k_gain': out['k_gain'], 'conv_w': out['conv_w'], 'conv_b': out['conv_b'], 'w_rg': out['w_rg'], 'b_rg': out['b_rg'], 'w_ig': out['w_ig'], 'b_ig': out['b_ig'], 'lru_lambda': out['lru_lambda'], 'w_o_attn': out['w_o_attn'], 'w_o_rnn': out['w_o_rnn'], 'w_out': out['w_out'], 'w_up': out['w_up'], 'w_down': out['w_down'], 'g_final': out['g_final'], 'loss_target': out['loss_target'], 'm_c_ctx': out['m_c_ctx'], 'm_w_mod': out['m_w_mod'], 'm_b_mod': out['m_b_mod'], 'm_g_mix': out['m_g_mix'], 'm_g_mlp': out['m_g_mlp'], 'm_w_in': out['m_w_in'], 'm_q_gain': out['m_q_gain'], 'm_k_gain': out['m_k_gain'], 'm_conv_w': out['m_conv_w'], 'm_conv_b': out['m_conv_b'], 'm_w_rg': out['m_w_rg'], 'm_b_rg': out['m_b_rg'], 'm_w_ig': out['m_w_ig'], 'm_b_ig': out['m_b_ig'], 'm_lru_lambda': out['m_lru_lambda'], 'm_w_o_attn': out['m_w_o_attn'], 'm_w_o_rnn': out['m_w_o_rnn'], 'm_w_out': out['m_w_out'], 'm_w_up': out['m_w_up'], 'm_w_down': out['m_w_down'], 'm_g_final': out['m_g_final'], 'v_c_ctx': out['v_c_ctx'], 'v_w_mod': out['v_w_mod'], 'v_b_mod': out['v_b_mod'], 'v_g_mix': out['v_g_mix'], 'v_g_mlp': out['v_g_mlp'], 'v_w_in': out['v_w_in'], 'v_q_gain': out['v_q_gain'], 'v_k_gain': out['v_k_gain'], 'v_conv_w': out['v_conv_w'], 'v_conv_b': out['v_conv_b'], 'v_w_rg': out['v_w_rg'], 'v_b_rg': out['v_b_rg'], 'v_w_ig': out['v_w_ig'], 'v_b_ig': out['v_b_ig'], 'v_lru_lambda': out['v_lru_lambda'], 'v_w_o_attn': out['v_w_o_attn'], 'v_w_o_rnn': out['v_w_o_rnn'], 'v_w_out': out['v_w_out'], 'v_w_up': out['v_w_up'], 'v_w_down': out['v_w_down'], 'v_g_final': out['v_g_final']}


def _loss(weights, diff, rest, loss_target):
    with _jax.named_scope("forward"):
        args = {**rest, TWIN_DIFF_INPUT: diff, **{k: w.astype(_WEIGHT_DTYPES[k]) for k, w in weights.items()}}
        y = _forward(args)
    with _jax.named_scope("loss_head"):
        err = _jnp.square(y.astype(_jnp.float32) - loss_target)
        return 0.5 * _jnp.sum(_jnp.mean(err, axis=-1)) if err.ndim else 0.5 * err


def _adamw(w, g, m, v):
    m = ADAM_B1 * m + (1.0 - ADAM_B1) * g
    v = ADAM_B2 * v + (1.0 - ADAM_B2) * _jnp.square(g)
    m_hat = m / (1.0 - ADAM_B1 ** ADAM_STEP)
    v_hat = v / (1.0 - ADAM_B2 ** ADAM_STEP)
    delta = -ADAM_LR * (m_hat / (_jnp.sqrt(v_hat) + ADAM_EPS) + ADAM_WD * w)
    return delta, m, v


def reference(x, c, ctx, c_ctx, w_mod, b_mod, g_mix, g_mlp, w_in, q_gain, k_gain, conv_w, conv_b, w_rg, b_rg, w_ig, b_ig, lru_lambda, w_o_attn, w_o_rnn, w_out, w_up, w_down, g_final, loss_target, m_c_ctx, m_w_mod, m_b_mod, m_g_mix, m_g_mlp, m_w_in, m_q_gain, m_k_gain, m_conv_w, m_conv_b, m_w_rg, m_b_rg, m_w_ig, m_b_ig, m_lru_lambda, m_w_o_attn, m_w_o_rnn, m_w_out, m_w_up, m_w_down, m_g_final, v_c_ctx, v_w_mod, v_b_mod, v_g_mix, v_g_mlp, v_w_in, v_q_gain, v_k_gain, v_conv_w, v_conv_b, v_w_rg, v_b_rg, v_w_ig, v_b_ig, v_lru_lambda, v_w_o_attn, v_w_o_rnn, v_w_out, v_w_up, v_w_down, v_g_final):
    given = dict(x=x, c=c, ctx=ctx, c_ctx=c_ctx, w_mod=w_mod, b_mod=b_mod, g_mix=g_mix, g_mlp=g_mlp, w_in=w_in, q_gain=q_gain, k_gain=k_gain, conv_w=conv_w, conv_b=conv_b, w_rg=w_rg, b_rg=b_rg, w_ig=w_ig, b_ig=b_ig, lru_lambda=lru_lambda, w_o_attn=w_o_attn, w_o_rnn=w_o_rnn, w_out=w_out, w_up=w_up, w_down=w_down, g_final=g_final, loss_target=loss_target, m_c_ctx=m_c_ctx, m_w_mod=m_w_mod, m_b_mod=m_b_mod, m_g_mix=m_g_mix, m_g_mlp=m_g_mlp, m_w_in=m_w_in, m_q_gain=m_q_gain, m_k_gain=m_k_gain, m_conv_w=m_conv_w, m_conv_b=m_conv_b, m_w_rg=m_w_rg, m_b_rg=m_b_rg, m_w_ig=m_w_ig, m_b_ig=m_b_ig, m_lru_lambda=m_lru_lambda, m_w_o_attn=m_w_o_attn, m_w_o_rnn=m_w_o_rnn, m_w_out=m_w_out, m_w_up=m_w_up, m_w_down=m_w_down, m_g_final=m_g_final, v_c_ctx=v_c_ctx, v_w_mod=v_w_mod, v_b_mod=v_b_mod, v_g_mix=v_g_mix, v_g_mlp=v_g_mlp, v_w_in=v_w_in, v_q_gain=v_q_gain, v_k_gain=v_k_gain, v_conv_w=v_conv_w, v_conv_b=v_conv_b, v_w_rg=v_w_rg, v_b_rg=v_b_rg, v_w_ig=v_w_ig, v_b_ig=v_b_ig, v_lru_lambda=v_lru_lambda, v_w_o_attn=v_w_o_attn, v_w_o_rnn=v_w_o_rnn, v_w_out=v_w_out, v_w_up=v_w_up, v_w_down=v_w_down, v_g_final=v_g_final)
    weights = {n: given[n] for n in TWIN_WEIGHTS}
    shared = {n: given[n] for n in SHARED_INPUTS}
    per_example = {n: given[n] for n in ['x', 'c', 'ctx']}
    grad_fn = _jax.value_and_grad(_loss, argnums=(0, 1))

    def one_microbatch(ex, loss_target):
        ex = dict(ex)
        diff = ex.pop(TWIN_DIFF_INPUT)
        return grad_fn(weights, diff, {**shared, **ex}, loss_target)

    if N_MICROBATCH == 1:
        loss, (grad_w, grad_x) = one_microbatch(per_example, given["loss_target"])
    else:
        def body(carry, xs):
            loss_sum, grad_sum = carry
            l_k, (gw_k, gx_k) = one_microbatch(xs[0], xs[1])
            with _jax.named_scope("update"):
                return (loss_sum + l_k, _jax.tree.map(_jnp.add, grad_sum, gw_k)), gx_k

        init = (_jnp.zeros((), _jnp.float32), _jax.tree.map(_jnp.zeros_like, weights))
        (loss, grad_w), grad_x = _jax.lax.scan(body, init, (per_example, given["loss_target"]))
    with _jax.named_scope("update"):
        delta_w, new_m, new_v = {}, {}, {}
        for n in TWIN_WEIGHTS:
            delta_w[n], new_m[n], new_v[n] = _adamw(weights[n], grad_w[n], given["m_" + n], given["v_" + n])
    return (loss, grad_x, *[grad_w[n] for n in TWIN_WEIGHTS], *[delta_w[n] for n in TWIN_WEIGHTS],
            *[new_m[n] for n in TWIN_WEIGHTS], *[new_v[n] for n in TWIN_WEIGHTS])
```

```python
import functools
import math

import jax
import jax.numpy as jnp
from jax import lax
from jax.experimental import pallas as pl
from jax.experimental.pallas import tpu as pltpu

F32 = jnp.float32
BF16 = jnp.bfloat16
MESH = pl.DeviceIdType.MESH

N_DEV = 8
HEAD_DIM = 128
Q_GROUP = 4
RNN_BLOCK = 128
GRID_W = 64
CONV_WIDTH = 4
LRU_C = 8.0
ROPE_THETA = 10000.0
NORM_EPS = 1e-6
N_MOD = 6
ADAM_LR = 0.001
ADAM_B1 = 0.9
ADAM_B2 = 0.999
ADAM_EPS = 1e-08
ADAM_WD = 0.01
ADAM_STEP = 10

LANES = 128
SUBLANES = 8
VMEM_LIMIT = 56 * 1024 * 1024
HBM_SPEC = pl.BlockSpec(memory_space=pltpu.HBM)


def _params(sem=None, vmem=VMEM_LIMIT):
    kw = dict(vmem_limit_bytes=vmem)
    if sem is not None:
        kw["dimension_semantics"] = sem
    return pltpu.CompilerParams(**kw)


def _pick(dim, target, mult):
    for t in range(min(dim, target), 0, -1):
        if dim % t == 0 and t % mult == 0:
            return t
    return dim


def _my_pos():
    return lax.axis_index("x"), lax.axis_index("y"), lax.axis_index("c")


def _flip(pos, d):
    x, y, c = pos
    return (1 - x if d & 4 else x, 1 - y if d & 2 else y, 1 - c if d & 1 else c)


def _lin(pos):
    return 4 * pos[0] + 2 * pos[1] + pos[2]


def _slab(ref, axis, idx, size):
    if axis == 0:
        return ref.at[pl.ds(idx * size, size)]
    assert axis == 1
    return ref.at[:, pl.ds(idx * size, size)]


def _all_gather(arrs, axes, name):
    n = len(arrs)
    sizes = [a.shape[ax] for a, ax in zip(arrs, axes)]

    def body(*refs):
        ins, outs = refs[:n], refs[n:2 * n]
        send_sems, recv_sems, local_sems = refs[2 * n:]
        me = _my_pos()
        sibling = _flip(me, 1)
        chips = [2, 4, 6]

        def copy(a, k, block_pos, to, src=None):
            dst = _slab(outs[a], axes[a], _lin(block_pos), sizes[a])
            return pltpu.make_async_remote_copy(
                src_ref=dst if src is None else src, dst_ref=dst,
                send_sem=send_sems.at[7 * a + k], recv_sem=recv_sems.at[7 * a + k],
                device_id=to, device_id_type=MESH)

        local, first, passed = [], [], []
        for a in range(n):
            cp = pltpu.make_async_copy(ins[a], _slab(outs[a], axes[a], _lin(me), sizes[a]), local_sems.at[a])
            cp.start()
            local.append(cp)
        for a in range(n):
            f = [copy(a, 0, me, sibling, src=ins[a])]
            f += [copy(a, 1 + j, me, _flip(me, m), src=ins[a]) for j, m in enumerate(chips)]
            for cp in f:
                cp.start()
            first += f
        for a in range(n):
            for j, m in enumerate(chips):
                copy(a, 1 + j, _flip(me, m), me).wait_recv()
                cp = copy(a, 4 + j, _flip(me, m), sibling)
                cp.start()
                passed.append(cp)
        for a in range(n):
            copy(a, 0, sibling, me).wait_recv()
            for j, m in enumerate(chips):
                copy(a, 4 + j, _flip(sibling, m), me).wait_recv()
        for cp in first + passed:
            cp.wait_send()
        for cp in local:
            cp.wait()

    out_shape = []
    for a, ax in zip(arrs, axes):
        shp = list(a.shape)
        shp[ax] *= N_DEV
        out_shape.append(jax.ShapeDtypeStruct(tuple(shp), a.dtype))
    return pl.pallas_call(
        body, name=name, out_shape=tuple(out_shape),
        in_specs=[HBM_SPEC] * n, out_specs=tuple([HBM_SPEC] * n),
        scratch_shapes=[pltpu.SemaphoreType.DMA((7 * n,)), pltpu.SemaphoreType.DMA((7 * n,)),
                        pltpu.SemaphoreType.DMA((n,))],
    )(*arrs)


def _all_to_all(arrs, axes, name):
    n = len(arrs)
    sizes = [a.shape[ax] // N_DEV for a, ax in zip(arrs, axes)]

    def body(*refs):
        ins, outs = refs[:n], refs[n:2 * n]
        send_sems, recv_sems, local_sems = refs[2 * n:]
        me = _my_pos()
        my = _lin(me)
        copies, local = [], []
        for a in range(n):
            cp = pltpu.make_async_copy(_slab(ins[a], axes[a], my, sizes[a]), outs[a].at[my], local_sems.at[a])
            cp.start()
            local.append(cp)
        for d in range(1, N_DEV):
            peer = _flip(me, d)
            for a in range(n):
                cp = pltpu.make_async_remote_copy(
                    src_ref=_slab(ins[a], axes[a], _lin(peer), sizes[a]), dst_ref=outs[a].at[my],
                    send_sem=send_sems.at[7 * a + d - 1], recv_sem=recv_sems.at[7 * a + d - 1],
                    device_id=peer, device_id_type=MESH)
                cp.start()
                copies.append((a, d, cp))
        for a, d, cp in copies:
            peer = _flip(me, d)
            pltpu.make_async_remote_copy(
                src_ref=_slab(ins[a], axes[a], _lin(peer), sizes[a]), dst_ref=outs[a].at[_lin(peer)],
                send_sem=send_sems.at[7 * a + d - 1], recv_sem=recv_sems.at[7 * a + d - 1],
                device_id=peer, device_id_type=MESH).wait_recv()
            cp.wait_send()
        for cp in local:
            cp.wait()

    out_shape = []
    for a, ax, s in zip(arrs, axes, sizes):
        shp = list(a.shape)
        shp[ax] = s
        out_shape.append(jax.ShapeDtypeStruct((N_DEV,) + tuple(shp), a.dtype))
    return pl.pallas_call(
        body, name=name, out_shape=tuple(out_shape),
        in_specs=[HBM_SPEC] * n, out_specs=tuple([HBM_SPEC] * n),
        scratch_shapes=[pltpu.SemaphoreType.DMA((7 * n,)), pltpu.SemaphoreType.DMA((7 * n,)),
                        pltpu.SemaphoreType.DMA((n,))],
    )(*arrs)


def _mm(a, b, *, name, ta=False, tb=False, out_dtype=F32, a_fn=None, out_fn=None, extras=(), bias=None,
        tm=1024, tn=512, tk=512):
    if ta:
        kdim, m = a.shape
    else:
        m, kdim = a.shape
    if tb:
        n, kb = b.shape
    else:
        kb, n = b.shape
    assert kb == kdim, (a.shape, b.shape, ta, tb)
    tm = _pick(m, tm, LANES if ta else SUBLANES)
    tn = _pick(n, tn, LANES)
    tk = _pick(kdim, tk, LANES)
    nk = kdim // tk
    ne = len(extras)
    has_bias = bias is not None

    def body(*refs):
        a_ref, b_ref = refs[0], refs[1]
        pos = 2
        bias_ref = None
        if has_bias:
            bias_ref = refs[pos]
            pos += 1
        extra_refs = refs[pos:pos + ne]
        o_ref, acc = refs[pos + ne], refs[pos + ne + 1]
        k = pl.program_id(2)

        @pl.when(k == 0)
        def _():
            acc[...] = jnp.zeros_like(acc)

        av = a_ref[...]
        if a_fn is not None:
            av = a_fn(av)
        dn = (((0 if ta else 1,), (1 if tb else 0,)), ((), ()))
        acc[...] += lax.dot_general(av.astype(BF16), b_ref[...].astype(BF16), dn, preferred_element_type=F32)

        @pl.when(k == nk - 1)
        def _():
            r = acc[...]
            if has_bias:
                r = r + bias_ref[...]
            if out_fn is not None:
                r = out_fn(r, *[e[...] for e in extra_refs])
            o_ref[...] = r.astype(out_dtype)

    in_specs = [
        pl.BlockSpec((tk, tm), lambda i, j, k: (k, i)) if ta else pl.BlockSpec((tm, tk), lambda i, j, k: (i, k)),
        pl.BlockSpec((tn, tk), lambda i, j, k: (j, k)) if tb else pl.BlockSpec((tk, tn), lambda i, j, k: (k, j)),
    ]
    args = [a, b]
    if has_bias:
        in_specs.append(pl.BlockSpec((1, tn), lambda i, j, k: (0, j)))
        args.append(bias)
    for e in extras:
        in_specs.append(pl.BlockSpec((tm, tn), lambda i, j, k: (i, j)))
        args.append(e)
    return pl.pallas_call(
        body, name=name, grid=(m // tm, n // tn, nk),
        in_specs=in_specs, out_specs=pl.BlockSpec((tm, tn), lambda i, j, k: (i, j)),
        out_shape=jax.ShapeDtypeStruct((m, n), out_dtype),
        scratch_shapes=[pltpu.VMEM((tm, tn), F32)],
        compiler_params=_params(("parallel", "parallel", "arbitrary")),
    )(*args)


def _row(arr, cb=0, rb=0, w=None):
    return ("row", arr, cb, rb, w)


def _bc(arr, cb=0, w=None):
    return ("bc", arr, cb, 0, w)


def _full(arr):
    return ("full", arr, 0, 0, None)


def _rowwise(fn, ins, outs, *, nr, tr, nc, cw, name):
    n_in = len(ins)
    outs = [(o[0], o[1], (o[2] if len(o) > 2 and o[2] else cw)) for o in outs]

    def body(*refs):
        in_refs, out_refs = refs[:n_in], refs[n_in:]
        j, i = pl.program_id(0), pl.program_id(1)
        res = fn(i, j, *[r[...] for r in in_refs])
        if not isinstance(res, (tuple, list)):
            res = (res,)
        for (kind, _, _), o_ref, r in zip(outs, out_refs, res):
            if kind == "row":
                o_ref[...] = r.astype(o_ref.dtype)
            else:
                @pl.when(i == 0)
                def _(o_ref=o_ref):
                    o_ref[...] = jnp.zeros_like(o_ref)

                o_ref[...] += r

    in_specs, args = [], []
    for kind, arr, cb, rb, w in ins:
        w = w or cw
        if kind == "row":
            in_specs.append(pl.BlockSpec((tr, w), lambda j, i, cb=cb, rb=rb: (i + rb, j + cb)))
        elif kind == "bc":
            in_specs.append(pl.BlockSpec((arr.shape[0], w), lambda j, i, cb=cb: (0, j + cb)))
        else:
            in_specs.append(pl.BlockSpec(arr.shape, lambda j, i, nd=arr.ndim: (0,) * nd))
        args.append(arr)
    out_specs, out_shape = [], []
    for kind, spec, w in outs:
        if kind == "row":
            out_specs.append(pl.BlockSpec((tr, w), lambda j, i: (i, j)))
            out_shape.append(jax.ShapeDtypeStruct((nr * tr, nc * w), spec))
        else:
            out_specs.append(pl.BlockSpec((spec, w), lambda j, i: (0, j)))
            out_shape.append(jax.ShapeDtypeStruct((spec, nc * w), F32))
    res = pl.pallas_call(
        body, name=name, grid=(nc, nr), in_specs=in_specs, out_specs=tuple(out_specs), out_shape=tuple(out_shape),
        compiler_params=_params(("parallel", "arbitrary")),
    )(*args)
    return res


def _adamw(parts, w, m, v, *, name):
    p, r, c = parts.shape
    tr = _pick(r, max(SUBLANES, (1 << 18) // c), 16 if parts.dtype == BF16 else SUBLANES)
    bc1 = 1.0 - ADAM_B1 ** ADAM_STEP
    bc2 = 1.0 - ADAM_B2 ** ADAM_STEP

    def body(p_ref, w_ref, m_ref, v_ref, g_out, d_out, m_out, v_out):
        g = p_ref[0].astype(F32)
        for q in range(1, p):
            g = g + p_ref[q].astype(F32)
        m_new = ADAM_B1 * m_ref[...] + (1.0 - ADAM_B1) * g
        v_new = ADAM_B2 * v_ref[...] + (1.0 - ADAM_B2) * (g * g)
        m_hat = m_new / bc1
        v_hat = v_new / bc2
        g_out[...] = g
        d_out[...] = -ADAM_LR * (m_hat / (jnp.sqrt(v_hat) + ADAM_EPS) + ADAM_WD * w_ref[...])
        m_out[...] = m_new
        v_out[...] = v_new

    blk = pl.BlockSpec((tr, c), lambda i: (i, 0))
    return pl.pallas_call(
        body, name=name, grid=(r // tr,),
        in_specs=[pl.BlockSpec((p, tr, c), lambda i: (0, i, 0)), blk, blk, blk],
        out_specs=(blk, blk, blk, blk), out_shape=tuple(jax.ShapeDtypeStruct((r, c), F32) for _ in range(4)),
        compiler_params=_params(("parallel",)),
    )(parts, w, m, v)


_GELU_C = math.sqrt(2.0 / math.pi)
_GELU_K = 0.044715


def _sigmoid(z):
    return 1.0 / (1.0 + jnp.exp(-z))


def _silu(z):
    return z * _sigmoid(z)


def _dsilu(z):
    s = _sigmoid(z)
    return s * (1.0 + z * (1.0 - s))


def _gelu(z):
    return 0.5 * z * (1.0 + jnp.tanh(_GELU_C * (z + _GELU_K * z * z * z)))


def _dgelu(z):
    t = jnp.tanh(_GELU_C * (z + _GELU_K * z * z * z))
    return 0.5 * (1.0 + t) + 0.5 * z * (1.0 - t * t) * (_GELU_C * (1.0 + 3.0 * _GELU_K * z * z))


def _softplus(z):
    return jnp.maximum(z, 0.0) + jnp.log(1.0 + jnp.exp(-jnp.abs(z)))


def _expm1(z):
    small = z * (1.0 + z * (0.5 + z * (1.0 / 6.0 + z * (1.0 / 24.0))))
    return jnp.where(jnp.abs(z) < 0.03, small, jnp.exp(z) - 1.0)


def _rms(xv):
    return lax.rsqrt(jnp.mean(xv * xv, axis=-1, keepdims=True) + NORM_EPS)


def _rms_bwd(xhat, r, dxhat):
    return r * (dxhat - xhat * jnp.mean(dxhat * xhat, axis=-1, keepdims=True))


def _swap_pairs(y):
    lane = lax.broadcasted_iota(jnp.int32, y.shape, 1)
    return jnp.where(lane % 2 == 0, pltpu.roll(y, LANES - 1, 1), pltpu.roll(y, 1, 1))


def _dot(a, b, ca, cb):
    return lax.dot_general(a, b, (((ca,), (cb,)), ((), ())), preferred_element_type=F32)


def _softmax_rows(qh, kk):
    s = _dot(qh, kk, 1, 1) * (HEAD_DIM ** -0.5)
    p = jnp.exp(s - jnp.max(s, axis=-1, keepdims=True))
    return p / jnp.sum(p, axis=-1, keepdims=True)


def _attn_fwd(q, k, v, *, tq, name):
    s_len, aw = q.shape
    t_len, kvw = k.shape
    gw = Q_GROUP * HEAD_DIM

    def body(q_ref, k_ref, v_ref, o_ref):
        kk, vv = k_ref[...], v_ref[...]
        for g in range(Q_GROUP):
            cols = slice(g * HEAD_DIM, (g + 1) * HEAD_DIM)
            p = _softmax_rows(q_ref[:, cols], kk)
            o_ref[:, cols] = _dot(p.astype(BF16), vv, 1, 0).astype(BF16)

    return pl.pallas_call(
        body, name=name, grid=(kvw // HEAD_DIM, s_len // tq),
        in_specs=[pl.BlockSpec((tq, gw), lambda h, i: (i, h)), pl.BlockSpec((t_len, HEAD_DIM), lambda h, i: (0, h)),
                  pl.BlockSpec((t_len, HEAD_DIM), lambda h, i: (0, h))],
        out_specs=pl.BlockSpec((tq, gw), lambda h, i: (i, h)),
        out_shape=jax.ShapeDtypeStruct((s_len, aw), BF16),
        compiler_params=_params(("parallel", "parallel")),
    )(q, k, v)


def _attn_bwd(q, k, v, do, *, tq, name):
    s_len, aw = q.shape
    t_len, kvw = k.shape
    gw = Q_GROUP * HEAD_DIM

    def body(q_ref, k_ref, v_ref, do_ref, dq_ref, dk_ref, dv_ref):
        @pl.when(pl.program_id(1) == 0)
        def _():
            dk_ref[...] = jnp.zeros_like(dk_ref)
            dv_ref[...] = jnp.zeros_like(dv_ref)

        kk, vv = k_ref[...], v_ref[...]
        for g in range(Q_GROUP):
            cols = slice(g * HEAD_DIM, (g + 1) * HEAD_DIM)
            qh, doh = q_ref[:, cols], do_ref[:, cols]
            p = _softmax_rows(qh, kk)
            dp = _dot(doh, vv, 1, 1)
            ds = p * (dp - jnp.sum(p * dp, axis=-1, keepdims=True)) * (HEAD_DIM ** -0.5)
            dsb = ds.astype(BF16)
            dq_ref[:, cols] = _dot(dsb, kk, 1, 0)
            dk_ref[...] += _dot(dsb, qh, 0, 0)
            dv_ref[...] += _dot(p.astype(BF16), doh, 0, 0)

    qspec = pl.BlockSpec((tq, gw), lambda h, i: (i, h))
    kspec = pl.BlockSpec((t_len, HEAD_DIM), lambda h, i: (0, h))
    return pl.pallas_call(
        body, name=name, grid=(kvw // HEAD_DIM, s_len // tq),
        in_specs=[qspec, kspec, kspec, qspec], out_specs=(qspec, kspec, kspec),
        out_shape=(jax.ShapeDtypeStruct((s_len, aw), F32), jax.ShapeDtypeStruct((t_len, kvw), F32),
                   jax.ShapeDtypeStruct((t_len, kvw), F32)),
        compiler_params=_params(("parallel", "arbitrary")),
    )(q, k, v, do)


def _seg_ok(t, d, l_len, t_len):
    td = t + d
    return ((t < l_len) & (td >= 0) & (td < l_len)) | ((t >= l_len) & (td >= l_len) & (td < t_len))


def _shift_rows(xv, d, t, l_len, t_len):
    if d == 0:
        return xv
    return jnp.where(_seg_ok(t, d, l_len, t_len), pltpu.roll(xv, (-d) % t_len, 0), 0.0)


def _scan_chunks(chains, c0, n, carries):
    def step(kk, hs):
        new = []
        for (a_ref, b_ref, h_ref, rev), hprev in zip(chains, hs):
            c = c0 + (n - 1 - kk if rev else kk)
            r0 = pl.multiple_of(c * SUBLANES, SUBLANES)
            a8 = a_ref[pl.ds(r0, SUBLANES), :]
            b8 = b_ref[pl.ds(r0, SUBLANES), :]
            row = lax.broadcasted_iota(jnp.int32, a8.shape, 0)
            for s in (1, 2, 4):
                if rev:
                    ok = row < SUBLANES - s
                    sh = SUBLANES - s
                else:
                    ok = row >= s
                    sh = s
                a_sh = jnp.where(ok, pltpu.roll(a8, sh, 0), 1.0)
                b_sh = jnp.where(ok, pltpu.roll(b8, sh, 0), 0.0)
                b8 = a8 * b_sh + b8
                a8 = a8 * a_sh
            h8 = b8 + a8 * hprev
            h_ref[pl.ds(r0, SUBLANES), :] = h8
            edge = 0 if rev else SUBLANES - 1
            last = jnp.sum(jnp.where(row == edge, h8, 0.0), axis=0, keepdims=True)
            new.append(jnp.broadcast_to(last, h8.shape))
        return tuple(new)

    return lax.fori_loop(0, n, step, tuple(carries))


def _rglru_conv(x, cw_ref, cb_ref, t, l_len, t_len):
    y = cb_ref[...] + _shift_rows(x, -1, t, l_len, t_len) * cw_ref[0:1, :]
    for kk in range(1, CONV_WIDTH):
        y = y + _shift_rows(x, kk - 1, t, l_len, t_len) * cw_ref[kk:kk + 1, :]
    return y


def _rglru_gates(xc, wr, wi, br, bi, lam):
    xb = xc.astype(BF16)
    r = _sigmoid(_dot(xb, wr.astype(BF16), 1, 0) + br)
    ig = _sigmoid(_dot(xb, wi.astype(BF16), 1, 0) + bi)
    sp = _softplus(-lam)
    log_a = (-LRU_C) * r * sp
    a = jnp.exp(log_a)
    mult = jnp.sqrt(-_expm1(2.0 * log_a))
    u = ig * xc
    return r, ig, sp, a, mult, u


def _rglru_specs(proj, xr_blk, xg_blk, t_len):
    col = lambda off: pl.BlockSpec((t_len, RNN_BLOCK), lambda h, off=off: (0, off + h))
    vec = lambda rows: pl.BlockSpec((rows, RNN_BLOCK), lambda h: (0, h))
    wsp = pl.BlockSpec((2, 1, RNN_BLOCK, RNN_BLOCK), lambda h: (0, h, 0, 0))
    return col(xr_blk), col(xg_blk), vec, wsp


def _rglru_fwd(proj, xr_blk, xg_blk, conv_w, conv_b, w_rg, w_ig, b_rg, b_ig, lam, *, l_len, name):
    t_len = proj.shape[0]
    s_len = t_len - l_len
    d_rnn = conv_b.shape[1]
    n_l, n_t = l_len // SUBLANES, t_len // SUBLANES

    def body(xr_ref, xg_ref, cw_ref, cb_ref, wr_ref, wi_ref, br_ref, bi_ref, lam_ref, rg_ref, a_s, b_s, h_s):
        t = lax.broadcasted_iota(jnp.int32, (t_len, RNN_BLOCK), 0)
        xc = _rglru_conv(xr_ref[...], cw_ref, cb_ref, t, l_len, t_len)
        for d in range(2):
            _, _, _, a, mult, u = _rglru_gates(xc, wr_ref[d, 0], wi_ref[d, 0], br_ref[d:d + 1, :],
                                               bi_ref[d:d + 1, :], lam_ref[d:d + 1, :])
            a_s[d] = a
            b_s[d] = mult * u
        chains = [(a_s.at[0], b_s.at[0], h_s.at[0], False), (a_s.at[1], b_s.at[1], h_s.at[1], True)]
        zero = jnp.zeros((SUBLANES, RNN_BLOCK), F32)
        carry = _scan_chunks(chains, 0, n_l, (zero, zero))
        _scan_chunks(chains, n_l, n_t - n_l, carry)
        rnn = h_s[0, pl.ds(l_len, s_len), :] + h_s[1, pl.ds(l_len, s_len), :]
        rg_ref[...] = (rnn * _gelu(xg_ref[pl.ds(l_len, s_len), :])).astype(BF16)

    xr_spec, xg_spec, vec, wsp = _rglru_specs(proj, xr_blk, xg_blk, t_len)
    return pl.pallas_call(
        body, name=name, grid=(d_rnn // RNN_BLOCK,),
        in_specs=[xr_spec, xg_spec, vec(CONV_WIDTH), vec(1), wsp, wsp, vec(2), vec(2), vec(2)],
        out_specs=pl.BlockSpec((s_len, RNN_BLOCK), lambda h: (0, h)),
        out_shape=jax.ShapeDtypeStruct((s_len, d_rnn), BF16),
        scratch_shapes=[pltpu.VMEM((2, t_len, RNN_BLOCK), F32)] * 3,
        compiler_params=_params(("parallel",)),
    )(proj, proj, conv_w, conv_b, w_rg, w_ig, b_rg, b_ig, lam)


N_RNN_VEC = 16


def _rglru_bwd(proj, xr_blk, xg_blk, drg, conv_w, conv_b, w_rg, w_ig, b_rg, b_ig, lam, *, l_len, name):
    t_len = proj.shape[0]
    s_len = t_len - l_len
    d_rnn = conv_b.shape[1]
    n_l, n_t = l_len // SUBLANES, t_len // SUBLANES

    def body(xr_ref, xg_ref, drg_ref, cw_ref, cb_ref, wr_ref, wi_ref, br_ref, bi_ref, lam_ref,
             dxr_ref, dxg_ref, dvec_ref, dwr_ref, dwi_ref, a_s, b_s, h_s):
        t = lax.broadcasted_iota(jnp.int32, (t_len, RNN_BLOCK), 0)
        x = xr_ref[...]
        xc = _rglru_conv(x, cw_ref, cb_ref, t, l_len, t_len)
        xb = xc.astype(BF16)
        gates = []
        for d in range(2):
            gt = _rglru_gates(xc, wr_ref[d, 0], wi_ref[d, 0], br_ref[d:d + 1, :], bi_ref[d:d + 1, :],
                              lam_ref[d:d + 1, :])
            gates.append(gt)
            a_s[d] = gt[3]
            b_s[d] = gt[4] * gt[5]
        zero = jnp.zeros((SUBLANES, RNN_BLOCK), F32)
        chains = [(a_s.at[0], b_s.at[0], h_s.at[0], False), (a_s.at[1], b_s.at[1], h_s.at[1], True)]
        carry = _scan_chunks(chains, 0, n_l, (zero, zero))
        _scan_chunks(chains, n_l, n_t - n_l, carry)

        h0, h1 = h_s[0], h_s[1]
        xg = xg_ref[...]
        dr = jnp.concatenate([jnp.zeros((l_len, RNN_BLOCK), F32), drg_ref[...].astype(F32)], axis=0)
        dxg_ref[...] = (dr * (h0 + h1) * _dgelu(xg)).astype(dxg_ref.dtype)
        drnn = dr * _gelu(xg)
        a_s[0] = pltpu.roll(gates[0][3], t_len - 1, 0)
        a_s[1] = pltpu.roll(gates[1][3], 1, 0)
        b_s[0] = drnn
        b_s[1] = drnn
        hprev = [jnp.where(t == 0, 0.0, pltpu.roll(h0, 1, 0)),
                 jnp.where(t == l_len - 1, 0.0, pltpu.roll(h1, t_len - 1, 0))]
        chains = [(a_s.at[0], b_s.at[0], h_s.at[0], True), (a_s.at[1], b_s.at[1], h_s.at[1], False)]
        carry = _scan_chunks(chains, n_l, n_t - n_l, (zero, zero))
        _scan_chunks(chains, 0, n_l, carry)

        dxc = jnp.zeros((t_len, RNN_BLOCK), F32)
        for d in range(2):
            r, ig, sp, a, mult, u = gates[d]
            lamb = h_s[d]
            da = lamb * hprev[d]
            dmult = lamb * u
            du = lamb * mult
            dlog_a = da * a - dmult * (a * a) / mult
            dzr = (dlog_a * ((-LRU_C) * sp)) * r * (1.0 - r)
            dsp = jnp.sum(dlog_a * ((-LRU_C) * r), axis=0, keepdims=True)
            dzi = (du * xc) * ig * (1.0 - ig)
            dzrb, dzib = dzr.astype(BF16), dzi.astype(BF16)
            dxc = dxc + du * ig + _dot(dzrb, wr_ref[d, 0].astype(BF16), 1, 1) + _dot(dzib, wi_ref[d, 0].astype(BF16), 1, 1)
            dwr_ref[d, 0] = _dot(xb, dzrb, 0, 0)
            dwi_ref[d, 0] = _dot(xb, dzib, 0, 0)
            dvec_ref[5 + d:6 + d, :] = jnp.sum(dzr, axis=0, keepdims=True)
            dvec_ref[7 + d:8 + d, :] = jnp.sum(dzi, axis=0, keepdims=True)
            dvec_ref[9 + d:10 + d, :] = dsp * (-_sigmoid(-lam_ref[d:d + 1, :]))
        dvec_ref[4:5, :] = jnp.sum(dxc, axis=0, keepdims=True)
        dvec_ref[11:N_RNN_VEC, :] = jnp.zeros((N_RNN_VEC - 11, RNN_BLOCK), F32)
        dxr = jnp.zeros((t_len, RNN_BLOCK), F32)
        for kk in range(CONV_WIDTH):
            dvec_ref[kk:kk + 1, :] = jnp.sum(dxc * _shift_rows(x, kk - 1, t, l_len, t_len), axis=0, keepdims=True)
            dxr = dxr + _shift_rows(dxc, 1 - kk, t, l_len, t_len) * cw_ref[kk:kk + 1, :]
        dxr_ref[...] = dxr.astype(dxr_ref.dtype)

    xr_spec, xg_spec, vec, wsp = _rglru_specs(proj, xr_blk, xg_blk, t_len)
    col_t = pl.BlockSpec((t_len, RNN_BLOCK), lambda h: (0, h))
    return pl.pallas_call(
        body, name=name, grid=(d_rnn // RNN_BLOCK,),
        in_specs=[xr_spec, xg_spec, pl.BlockSpec((s_len, RNN_BLOCK), lambda h: (0, h)), vec(CONV_WIDTH), vec(1),
                  wsp, wsp, vec(2), vec(2), vec(2)],
        out_specs=(col_t, col_t, vec(N_RNN_VEC), wsp, wsp),
        out_shape=(jax.ShapeDtypeStruct((t_len, d_rnn), BF16), jax.ShapeDtypeStruct((t_len, d_rnn), BF16),
                   jax.ShapeDtypeStruct((N_RNN_VEC, d_rnn), F32), jax.ShapeDtypeStruct(w_rg.shape, F32),
                   jax.ShapeDtypeStruct(w_ig.shape, F32)),
        scratch_shapes=[pltpu.VMEM((2, t_len, RNN_BLOCK), F32)] * 3,
        compiler_params=_params(("parallel",)),
    )(proj, proj, drg, conv_w, conv_b, w_rg, w_ig, b_rg, b_ig, lam)


def _colsum(v):
    return jnp.sum(v, axis=0, keepdims=True)


def _qk_fwd_heads(xv, gain, cos_e, sin_e):
    outs = []
    for h in range(xv.shape[1] // HEAD_DIM):
        xh = xv[:, h * HEAD_DIM:(h + 1) * HEAD_DIM]
        y = xh * _rms(xh) * gain
        outs.append(y * cos_e + _swap_pairs(y) * sin_e)
    return outs[0] if len(outs) == 1 else jnp.concatenate(outs, axis=1)


def _qk_bwd_heads(xv, dov, gain, cos_e, sin_e):
    outs = []
    dgain = jnp.zeros((1, HEAD_DIM), F32)
    for h in range(xv.shape[1] // HEAD_DIM):
        cols = slice(h * HEAD_DIM, (h + 1) * HEAD_DIM)
        xh, doh = xv[:, cols], dov[:, cols]
        r = _rms(xh)
        xhat = xh * r
        dy = doh * cos_e + _swap_pairs(doh * sin_e)
        dgain = dgain + _colsum(dy * xhat)
        outs.append(_rms_bwd(xhat, r, dy * gain))
    return (outs[0] if len(outs) == 1 else jnp.concatenate(outs, axis=1)), dgain


WEIGHTS = ['c_ctx', 'w_mod', 'b_mod', 'g_mix', 'g_mlp', 'w_in', 'q_gain', 'k_gain', 'conv_w', 'conv_b', 'w_rg', 'b_rg',
           'w_ig', 'b_ig', 'lru_lambda', 'w_o_attn', 'w_o_rnn', 'w_out', 'w_up', 'w_down', 'g_final']
REPLICATED = ['c_ctx', 'b_mod', 'g_mix', 'g_mlp', 'q_gain', 'k_gain', 'conv_b', 'w_rg', 'w_ig', 'g_final']
SMALL_SHARDED = [('conv_w', CONV_WIDTH), ('b_rg', 2), ('b_ig', 2), ('lru_lambda', 2)]
SMALL_ROWS = 16


PACK_ROWS = 512


def _pack_flat(arrs, rows_mult):
    flat = jnp.concatenate([a.reshape(-1) for a in arrs])
    n = flat.shape[0]
    unit = rows_mult * LANES
    pad = (-n) % unit
    if pad:
        flat = jnp.concatenate([flat, jnp.zeros((pad,), flat.dtype)])
    return flat.reshape(-1, LANES)


def _unpack_flat(packed, shapes):
    flat = packed.reshape(-1)
    out, pos = [], 0
    for shp in shapes:
        n = math.prod(shp)
        out.append(flat[pos:pos + n].reshape(shp))
        pos += n
    return out


def _pack_small(d):
    rows = [d[n][0] for n, _ in SMALL_SHARDED]
    cols = rows[0].shape[1]
    used = sum(k for _, k in SMALL_SHARDED)
    return jnp.concatenate(rows + [jnp.zeros((SMALL_ROWS - used, cols), F32)], axis=0)


def _unpack_small(p):
    out, pos = {}, 0
    for n, k in SMALL_SHARDED:
        out[n] = p[pos:pos + k][None]
        pos += k
    return out


def kernel(x, c, ctx, c_ctx, w_mod, b_mod, g_mix, g_mlp, w_in, q_gain, k_gain, conv_w, conv_b, w_rg, b_rg, w_ig, b_ig, lru_lambda, w_o_attn, w_o_rnn, w_out, w_up, w_down, g_final, loss_target, m_c_ctx, m_w_mod, m_b_mod, m_g_mix, m_g_mlp, m_w_in, m_q_gain, m_k_gain, m_conv_w, m_conv_b, m_w_rg, m_b_rg, m_w_ig, m_b_ig, m_lru_lambda, m_w_o_attn, m_w_o_rnn, m_w_out, m_w_up, m_w_down, m_g_final, v_c_ctx, v_w_mod, v_b_mod, v_g_mix, v_g_mlp, v_w_in, v_q_gain, v_k_gain, v_conv_w, v_conv_b, v_w_rg, v_b_rg, v_w_ig, v_b_ig, v_lru_lambda, v_w_o_attn, v_w_o_rnn, v_w_out, v_w_up, v_w_down, v_g_final):
    wd = dict(c_ctx=c_ctx, w_mod=w_mod, b_mod=b_mod, g_mix=g_mix, g_mlp=g_mlp, w_in=w_in, q_gain=q_gain, k_gain=k_gain,
              conv_w=conv_w, conv_b=conv_b, w_rg=w_rg, b_rg=b_rg, w_ig=w_ig, b_ig=b_ig, lru_lambda=lru_lambda,
              w_o_attn=w_o_attn, w_o_rnn=w_o_rnn, w_out=w_out, w_up=w_up, w_down=w_down, g_final=g_final)
    md = dict(zip(WEIGHTS, (m_c_ctx, m_w_mod, m_b_mod, m_g_mix, m_g_mlp, m_w_in, m_q_gain, m_k_gain, m_conv_w, m_conv_b,
                            m_w_rg, m_b_rg, m_w_ig, m_b_ig, m_lru_lambda, m_w_o_attn, m_w_o_rnn, m_w_out, m_w_up,
                            m_w_down, m_g_final)))
    vd = dict(zip(WEIGHTS, (v_c_ctx, v_w_mod, v_b_mod, v_g_mix, v_g_mlp, v_w_in, v_q_gain, v_k_gain, v_conv_w, v_conv_b,
                            v_w_rg, v_b_rg, v_w_ig, v_b_ig, v_lru_lambda, v_w_o_attn, v_w_o_rnn, v_w_out, v_w_up,
                            v_w_down, v_g_final)))

    xs, tgt, ctxs = x[0], loss_target[0], ctx[0]
    s_len, d = xs.shape
    l_len = ctxs.shape[0]
    t_len = s_len + l_len
    n_in = w_in.shape[2] * N_DEV
    aw = (n_in - 4 * d) * 2 // 3
    kvw = aw // Q_GROUP
    assert aw == d and n_in == aw + 2 * kvw + 4 * d
    nmod = N_MOD * d // N_DEV
    tr = 128
    assert l_len % tr == 0 and s_len % tr == 0
    lb = l_len // tr
    tq = min(256, s_len)
    my = _lin(_my_pos())
    c_ctx2 = c_ctx[None]
    g_final2 = g_final[None]

    c_all = _all_gather([jnp.broadcast_to(c, (SUBLANES, d))], [0], "ag_cond")[0][::SUBLANES]
    c16 = jnp.concatenate([c_all, jnp.broadcast_to(c_ctx2, (N_DEV, d))], axis=0)
    b_mod_sh = lax.dynamic_slice(b_mod, (0, my * nmod), (1, nmod))
    mod_part = _mm(c16, w_mod[0], a_fn=_silu, bias=b_mod_sh, name="mod_fwd")
    mod_g = _all_gather([mod_part], [1], "ag_mod")[0]
    mod_lat = lax.dynamic_slice(mod_g, (my, 0), (1, N_MOD * d))
    sh_a, sc_a, ga_a, sh_f, sc_f, ga_f = [mod_lat[:, k * d:(k + 1) * d] for k in range(N_MOD)]
    csh_a, csc_a = mod_g[N_DEV:N_DEV + 1, :d], mod_g[N_DEV:N_DEV + 1, d:2 * d]

    sp_w = _pack_small(wd)
    w_in_f, w_oa_f, w_or_f, w_out_f, w_up_f, w_down_f, sp_f = _all_gather(
        [w_in[0].astype(BF16), w_o_attn[0].astype(BF16), w_o_rnn[0].astype(BF16), w_out[0].astype(BF16),
         w_up[0].astype(BF16), w_down[0].astype(BF16), sp_w], [1, 0, 0, 0, 1, 0, 1], "ag_weights")
    conv_w_f, b_rg_f, b_ig_f, lam_f = sp_f[0:4], sp_f[4:6], sp_f[6:8], sp_f[8:10]

    pos = jnp.arange(s_len)
    n_freq = HEAD_DIM // 4
    inv_freq = ROPE_THETA ** (-jnp.arange(n_freq, dtype=F32) / n_freq)
    ang = jnp.concatenate([(pos // GRID_W).astype(F32)[:, None] * inv_freq,
                           (pos % GRID_W).astype(F32)[:, None] * inv_freq], axis=-1)
    sign = jnp.where(jnp.arange(HEAD_DIM) % 2 == 0, -1.0, 1.0).astype(F32)
    cos_e = jnp.concatenate([jnp.ones((l_len, HEAD_DIM), F32), jnp.repeat(jnp.cos(ang), 2, axis=-1)], axis=0)
    sin_e = jnp.concatenate([jnp.zeros((l_len, HEAD_DIM), F32), jnp.repeat(jnp.sin(ang), 2, axis=-1) * sign], axis=0)

    x_all = jnp.concatenate([ctxs, xs], axis=0)

    def nm1_fwd(i, j, xv, g, csh, csc, sh, sc):
        is_ctx = i < lb
        n = xv * _rms(xv) * g
        return n * (1.0 + jnp.where(is_ctx, csc, sc)) + jnp.where(is_ctx, csh, sh)

    h_all = _rowwise(nm1_fwd, [_row(x_all), _bc(g_mix), _bc(csh_a), _bc(csc_a), _bc(sh_a), _bc(sc_a)],
                     [("row", BF16)], nr=t_len // tr, tr=tr, nc=1, cw=d, name="norm1_fwd")[0]
    proj = _mm(h_all, w_in_f, name="proj_fwd")

    kb = aw // kvw
    q = _rowwise(lambda i, j, xv, gain, ce, se: _qk_fwd_heads(xv, gain, ce, se),
                 [_row(proj, 0, lb, aw), _full(q_gain), _row(cos_e, 0, lb, HEAD_DIM), _row(sin_e, 0, lb, HEAD_DIM)],
                 [("row", BF16, aw)], nr=s_len // tr, tr=tr, nc=1, cw=aw, name="q_fwd")[0]
    k_all, v_all = _rowwise(lambda i, j, kv_, vv_, gain, ce, se: (_qk_fwd_heads(kv_, gain, ce, se), vv_),
                            [_row(proj, kb, 0, kvw), _row(proj, kb + 1, 0, kvw), _full(k_gain),
                             _row(cos_e, 0, 0, HEAD_DIM), _row(sin_e, 0, 0, HEAD_DIM)],
                            [("row", BF16, kvw), ("row", BF16, kvw)], nr=t_len // tr, tr=tr, nc=1, cw=kvw, name="kv_fwd")
    attn_o = _attn_fwd(q, k_all, v_all, tq=tq, name="attn_fwd")

    xr_blk = (aw + 2 * kvw) // RNN_BLOCK
    xg_blk = xr_blk + d // RNN_BLOCK
    rg = _rglru_fwd(proj, xr_blk, xg_blk, conv_w_f, conv_b, w_rg[0], w_ig[0], b_rg_f, b_ig_f, lam_f, l_len=l_len,
                    name="rglru_fwd")

    y_attn = _mm(attn_o, w_oa_f, name="o_attn_fwd")
    y_rnn = _mm(rg, w_or_f, name="o_rnn_fwd")
    hw = d // 2
    glb = (aw + 2 * kvw + 2 * d) // hw

    def merge_fwd(i, j, ya, yr, gla, glr):
        return _sigmoid(gla) * ya + _sigmoid(glr) * yr

    u = _rowwise(merge_fwd, [_row(y_attn), _row(y_rnn), _row(proj, glb, lb), _row(proj, glb + 2, lb)],
                 [("row", BF16)], nr=s_len // tr, tr=tr, nc=2, cw=hw, name="merge_fwd")[0]
    mix = _mm(u, w_out_f, name="out_fwd")

    def res1_fwd(i, j, xv, mixv, gaa, gm, shf, scf):
        x1v = xv + gaa * mixv
        n = x1v * _rms(x1v) * gm
        return x1v, n * (1.0 + scf) + shf

    x1, h2 = _rowwise(res1_fwd, [_row(xs), _row(mix), _bc(ga_a), _bc(g_mlp), _bc(sh_f), _bc(sc_f)],
                      [("row", F32), ("row", BF16)], nr=s_len // tr, tr=tr, nc=1, cw=d, name="norm2_fwd")
    relu2 = lambda z: jnp.square(jnp.maximum(z, 0.0))
    up = _mm(h2, w_up_f, name="up_fwd")
    down = _mm(up, w_down_f, a_fn=relu2, name="down_fwd")

    def final_fn(i, j, x1v, dn, tg, gaf, gf):
        x2 = x1v + gaf * dn
        r = _rms(x2)
        xh = x2 * r
        e = xh * gf - tg
        dy = e * (1.0 / d)
        dx2 = _rms_bwd(xh, r, dy * gf)
        return dx2, dx2 * gaf, _colsum(e * e) * (0.5 / d), _colsum(dy * xh), _colsum(dx2 * dn)

    dx2, ddown, loss_cols, g_gfinal, d_ga_f = _rowwise(
        final_fn, [_row(x1), _row(down), _row(tgt), _bc(ga_f), _bc(g_final2)],
        [("row", F32), ("row", BF16), ("acc", 1), ("acc", 1), ("acc", 1)], nr=s_len // tr, tr=tr, nc=1, cw=d,
        name="final_bwd")
    loss = lax.psum(jnp.sum(loss_cols), ("x", "y", "c"))

    dup = _mm(ddown, w_down_f, tb=True, out_fn=lambda r, upv: r * (2.0 * jnp.maximum(upv, 0.0)), extras=[up],
              out_dtype=BF16, name="down_bwd_x")
    gw_down = _mm(up, ddown, ta=True, a_fn=relu2, out_dtype=BF16, name="down_bwd_w")
    dh2 = _mm(dup, w_up_f, tb=True, name="up_bwd_x")
    gw_up = _mm(h2, dup, ta=True, out_dtype=BF16, name="up_bwd_w")

    def norm2_bwd(i, j, x1v, mixv, dh2v, dx2v, gaa, gm, scf):
        r = _rms(x1v)
        xh = x1v * r
        n = xh * gm
        dn = dh2v * (1.0 + scf)
        dx1 = dx2v + _rms_bwd(xh, r, dn * gm)
        return dx1, dx1 * gaa, _colsum(dh2v), _colsum(dh2v * n), _colsum(dn * xh), _colsum(dx1 * mixv)

    dx1, dmix, d_sh_f, d_sc_f, g_gmlp, d_ga_a = _rowwise(
        norm2_bwd, [_row(x1), _row(mix), _row(dh2), _row(dx2), _bc(ga_a), _bc(g_mlp), _bc(sc_f)],
        [("row", F32), ("row", BF16), ("acc", 1), ("acc", 1), ("acc", 1), ("acc", 1)], nr=s_len // tr, tr=tr, nc=1,
        cw=d, name="norm2_bwd")

    du = _mm(dmix, w_out_f, tb=True, name="out_bwd_x")
    gw_out = _mm(u, dmix, ta=True, out_dtype=BF16, name="out_bwd_w")

    def merge_bwd(i, j, duv, ya, yr, gla, glr):
        ga, gr = _sigmoid(gla), _sigmoid(glr)
        return duv * ga, duv * gr, duv * ya * ga * (1.0 - ga), duv * yr * gr * (1.0 - gr)

    dya, dyr, dgla, dglr = _rowwise(
        merge_bwd, [_row(du), _row(y_attn), _row(y_rnn), _row(proj, glb, lb), _row(proj, glb + 2, lb)],
        [("row", BF16)] * 4, nr=s_len // tr, tr=tr, nc=2, cw=hw, name="merge_bwd")
    dattn = _mm(dya, w_oa_f, tb=True, out_dtype=BF16, name="o_attn_bwd_x")
    gw_oa = _mm(attn_o, dya, ta=True, out_dtype=BF16, name="o_attn_bwd_w")
    drg = _mm(dyr, w_or_f, tb=True, out_dtype=BF16, name="o_rnn_bwd_x")
    gw_or = _mm(rg, dyr, ta=True, out_dtype=BF16, name="o_rnn_bwd_w")

    dxr, dxg, dvec, g_wrg, g_wig = _rglru_bwd(proj, xr_blk, xg_blk, drg, conv_w_f, conv_b, w_rg[0], w_ig[0], b_rg_f,
                                               b_ig_f, lam_f, l_len=l_len, name="rglru_bwd")
    dq, dk, dv = _attn_bwd(q, k_all, v_all, dattn, tq=tq, name="attn_bwd")

    dq_p, g_qgain = _rowwise(lambda i, j, xv, dov, gain, ce, se: _qk_bwd_heads(xv, dov, gain, ce, se),
                             [_row(proj, 0, lb, aw), _row(dq, 0, 0, aw), _full(q_gain), _row(cos_e, 0, lb, HEAD_DIM),
                              _row(sin_e, 0, lb, HEAD_DIM)],
                             [("row", BF16, aw), ("acc", 1, HEAD_DIM)], nr=s_len // tr, tr=tr, nc=1, cw=aw, name="q_bwd")

    def kv_bwd(i, j, xv, dkv, dvv, gain, ce, se):
        dk_p, dgain = _qk_bwd_heads(xv, dkv, gain, ce, se)
        return dk_p, dvv, dgain

    dk_p, dv_p, g_kgain = _rowwise(kv_bwd, [_row(proj, kb, 0, kvw), _row(dk, 0, 0, kvw), _row(dv, 0, 0, kvw),
                                            _full(k_gain), _row(cos_e, 0, 0, HEAD_DIM), _row(sin_e, 0, 0, HEAD_DIM)],
                                   [("row", BF16, kvw), ("row", BF16, kvw), ("acc", 1, HEAD_DIM)], nr=t_len // tr, tr=tr,
                                   nc=1, cw=kvw, name="kv_bwd")

    pad_ctx = lambda a: jnp.pad(a, ((l_len, 0), (0, 0)))
    dproj = jnp.concatenate([pad_ctx(dq_p), dk_p, dv_p, dxr, dxg, pad_ctx(dgla), pad_ctx(dglr)], axis=1)
    dh_all = _mm(dproj, w_in_f, tb=True, name="proj_bwd_x")
    gw_in = _mm(h_all, dproj, ta=True, out_dtype=BF16, name="proj_bwd_w")

    def norm1_bwd(i, j, xv, dhv, dresv, g, csc, sc):
        is_ctx = i < lb
        r = _rms(xv)
        xh = xv * r
        dn = dhv * (1.0 + jnp.where(is_ctx, csc, sc))
        dsh, dsc = _colsum(dhv), _colsum(dhv * (xh * g))
        zero = jnp.zeros_like(dsh)
        return (dresv + _rms_bwd(xh, r, dn * g), jnp.where(is_ctx, dsh, zero), jnp.where(is_ctx, dsc, zero),
                jnp.where(is_ctx, zero, dsh), jnp.where(is_ctx, zero, dsc), _colsum(dn * xh))

    dx_all, d_csh, d_csc, d_sh_a, d_sc_a, g_gmix = _rowwise(
        norm1_bwd, [_row(x_all), _row(dh_all), _row(pad_ctx(dx1)), _bc(g_mix), _bc(csc_a), _bc(sc_a)],
        [("row", F32)] + [("acc", 1)] * 5, nr=t_len // tr, tr=tr, nc=1, cw=d, name="norm1_bwd")
    grad_x = dx_all[l_len:][None]

    dmod_lat = jnp.concatenate([d_sh_a, d_sc_a, d_ga_a, d_sh_f, d_sc_f, d_ga_f], axis=1)
    dmod_ctx = jnp.concatenate([d_csh, d_csc, jnp.zeros((1, (N_MOD - 2) * d), F32)], axis=1)
    dm_g = _all_gather([jnp.concatenate([dmod_lat, dmod_ctx, jnp.zeros((SUBLANES - 2, N_MOD * d), F32)], axis=0)], [0],
                       "ag_dmod")[0]
    dm16 = jnp.concatenate([dm_g[0::SUBLANES], dm_g[1::SUBLANES]], axis=0)
    dm_sh = lax.dynamic_slice(dm16, (0, my * nmod), (2 * N_DEV, nmod))
    gw_mod = _mm(c16, dm_sh, ta=True, a_fn=_silu, name="mod_bwd_w")
    dc16 = _mm(dm_sh, w_mod[0], tb=True, name="mod_bwd_x")

    def cctx_fn(i, j, pv, cc, dl, dc):
        row = lax.broadcasted_iota(jnp.int32, pv.shape, 0)
        return _colsum(jnp.where(row >= N_DEV, pv, 0.0)) * _dsilu(cc), dl + dc

    g_cctx, g_bmod = _rowwise(cctx_fn, [_full(dc16), _full(c_ctx2), _full(dmod_lat), _full(dmod_ctx)],
                              [("row", F32, d), ("row", F32, N_MOD * d)], nr=1, tr=1, nc=1, cw=d, name="cctx_bwd")

    rep_part = dict(c_ctx=g_cctx, b_mod=g_bmod, g_mix=g_gmix, g_mlp=g_gmlp, q_gain=g_qgain, k_gain=g_kgain,
                    conv_b=dvec[4:5], w_rg=g_wrg, w_ig=g_wig, g_final=g_gfinal)
    rep_shapes = [wd[n].shape for n in REPLICATED]
    part_pack = _pack_flat([rep_part[n] for n in REPLICATED], PACK_ROWS)
    rep_rows = part_pack.shape[0]
    parts = _all_gather([part_pack], [0], "ag_rep_grads")[0].reshape(N_DEV, rep_rows, LANES)
    rep_out = _adamw(parts, _pack_flat([wd[n] for n in REPLICATED], PACK_ROWS),
                     _pack_flat([md[n] for n in REPLICATED], PACK_ROWS),
                     _pack_flat([vd[n] for n in REPLICATED], PACK_ROWS), name="adamw_rep")
    res = {}
    for kind, packed in zip(("g", "d", "m", "v"), rep_out):
        for n, a in zip(REPLICATED, _unpack_flat(packed, rep_shapes)):
            res[kind, n] = a

    small_g = jnp.concatenate([dvec[0:4], dvec[5:11], jnp.zeros((SMALL_ROWS - 10, d), F32)], axis=0)
    big = [('w_in', gw_in, 1), ('w_o_attn', gw_oa, 0), ('w_o_rnn', gw_or, 0), ('w_out', gw_out, 0), ('w_up', gw_up, 1),
           ('w_down', gw_down, 0)]
    recv = _all_to_all([g for _, g, _ in big] + [small_g], [ax for _, _, ax in big] + [1], "a2a_grads")
    for (n, _, _), pr in zip(big, recv[:-1]):
        out = _adamw(pr, wd[n][0], md[n][0], vd[n][0], name="adamw_" + n)
        for kind, a in zip(("g", "d", "m", "v"), out):
            res[kind, n] = a[None]
    out = _adamw(gw_mod[None], w_mod[0], m_w_mod[0], v_w_mod[0], name="adamw_w_mod")
    for kind, a in zip(("g", "d", "m", "v"), out):
        res[kind, 'w_mod'] = a[None]
    out = _adamw(recv[-1], sp_w, _pack_small(md), _pack_small(vd), name="adamw_small")
    for kind, a in zip(("g", "d", "m", "v"), out):
        for n, piece in _unpack_small(a).items():
            res[kind, n] = piece

    outs = [loss, grad_x]
    for kind in ("g", "d", "m", "v"):
        outs += [res[kind, n].reshape(wd[n].shape) for n in WEIGHTS]
    return tuple(outs)
```

```python
import functools
import math

import jax
import jax.numpy as jnp
from jax import lax
from jax.experimental import pallas as pl
from jax.experimental.pallas import tpu as pltpu

F32 = jnp.float32
BF16 = jnp.bfloat16
MESH = pl.DeviceIdType.MESH

N_DEV = 8
HEAD_DIM = 128
Q_GROUP = 4
RNN_BLOCK = 128
GRID_W = 64
CONV_WIDTH = 4
LRU_C = 8.0
ROPE_THETA = 10000.0
NORM_EPS = 1e-6
N_MOD = 6
ADAM_LR = 0.001
ADAM_B1 = 0.9
ADAM_B2 = 0.999
ADAM_EPS = 1e-08
ADAM_WD = 0.01
ADAM_STEP = 10

LANES = 128
SUBLANES = 8
VMEM_LIMIT = 56 * 1024 * 1024
HBM_SPEC = pl.BlockSpec(memory_space=pltpu.HBM)


def _params(sem=None, vmem=VMEM_LIMIT):
    kw = dict(vmem_limit_bytes=vmem)
    if sem is not None:
        kw["dimension_semantics"] = sem
    return pltpu.CompilerParams(**kw)


def _pick(dim, target, mult):
    for t in range(min(dim, target), 0, -1):
        if dim % t == 0 and t % mult == 0:
            return t
    return dim


def _my_pos():
    return lax.axis_index("x"), lax.axis_index("y"), lax.axis_index("c")


def _flip(pos, d):
    x, y, c = pos
    return (1 - x if d & 4 else x, 1 - y if d & 2 else y, 1 - c if d & 1 else c)


def _lin(pos):
    return 4 * pos[0] + 2 * pos[1] + pos[2]


def _slab(ref, axis, idx, size):
    if axis == 0:
        return ref.at[pl.ds(idx * size, size)]
    assert axis == 1
    return ref.at[:, pl.ds(idx * size, size)]


def _all_gather(arrs, axes, name, with_token=False):
    n = len(arrs)
    sizes = [a.shape[ax] for a, ax in zip(arrs, axes)]

    def body(*refs):
        ins, outs, token = refs[:n], refs[n:2 * n], refs[2 * n]
        send_sems, recv_sems, local_sems = refs[2 * n + 1:]
        token[...] = jnp.zeros_like(token)
        me = _my_pos()
        sibling = _flip(me, 1)
        chips = [2, 4, 6]

        def copy(a, k, block_pos, to, src=None):
            dst = _slab(outs[a], axes[a], _lin(block_pos), sizes[a])
            return pltpu.make_async_remote_copy(
                src_ref=dst if src is None else src, dst_ref=dst,
                send_sem=send_sems.at[7 * a + k], recv_sem=recv_sems.at[7 * a + k],
                device_id=to, device_id_type=MESH)

        local, first, passed = [], [], []
        for a in range(n):
            cp = pltpu.make_async_copy(ins[a], _slab(outs[a], axes[a], _lin(me), sizes[a]), local_sems.at[a])
            cp.start()
            local.append(cp)
        for a in range(n):
            f = [copy(a, 0, me, sibling, src=ins[a])]
            f += [copy(a, 1 + j, me, _flip(me, m), src=ins[a]) for j, m in enumerate(chips)]
            for cp in f:
                cp.start()
            first += f
        for a in range(n):
            for j, m in enumerate(chips):
                copy(a, 1 + j, _flip(me, m), me).wait_recv()
                cp = copy(a, 4 + j, _flip(me, m), sibling)
                cp.start()
                passed.append(cp)
        for a in range(n):
            copy(a, 0, sibling, me).wait_recv()
            for j, m in enumerate(chips):
                copy(a, 4 + j, _flip(sibling, m), me).wait_recv()
        for cp in first + passed:
            cp.wait_send()
        for cp in local:
            cp.wait()

    out_shape = []
    for a, ax in zip(arrs, axes):
        shp = list(a.shape)
        shp[ax] *= N_DEV
        out_shape.append(jax.ShapeDtypeStruct(tuple(shp), a.dtype))
    out_shape.append(jax.ShapeDtypeStruct((SUBLANES, LANES), F32))
    res = pl.pallas_call(
        body, name=name, out_shape=tuple(out_shape),
        in_specs=[HBM_SPEC] * n, out_specs=tuple([HBM_SPEC] * n + [pl.BlockSpec(memory_space=pltpu.VMEM)]),
        scratch_shapes=[pltpu.SemaphoreType.DMA((7 * n,)), pltpu.SemaphoreType.DMA((7 * n,)),
                        pltpu.SemaphoreType.DMA((n,))],
    )(*arrs)
    return (list(res[:n]), res[n]) if with_token else list(res[:n])


def _all_to_all(arrs, axes, name):
    n = len(arrs)
    sizes = [a.shape[ax] // N_DEV for a, ax in zip(arrs, axes)]

    def body(*refs):
        ins, outs = refs[:n], refs[n:2 * n]
        send_sems, recv_sems, local_sems = refs[2 * n:]
        me = _my_pos()
        my = _lin(me)
        copies, local = [], []
        for a in range(n):
            cp = pltpu.make_async_copy(_slab(ins[a], axes[a], my, sizes[a]), outs[a].at[my], local_sems.at[a])
            cp.start()
            local.append(cp)
        for d in range(1, N_DEV):
            peer = _flip(me, d)
            for a in range(n):
                cp = pltpu.make_async_remote_copy(
                    src_ref=_slab(ins[a], axes[a], _lin(peer), sizes[a]), dst_ref=outs[a].at[my],
                    send_sem=send_sems.at[7 * a + d - 1], recv_sem=recv_sems.at[7 * a + d - 1],
                    device_id=peer, device_id_type=MESH)
                cp.start()
                copies.append((a, d, cp))
        for a, d, cp in copies:
            peer = _flip(me, d)
            pltpu.make_async_remote_copy(
                src_ref=_slab(ins[a], axes[a], _lin(peer), sizes[a]), dst_ref=outs[a].at[_lin(peer)],
                send_sem=send_sems.at[7 * a + d - 1], recv_sem=recv_sems.at[7 * a + d - 1],
                device_id=peer, device_id_type=MESH).wait_recv()
            cp.wait_send()
        for cp in local:
            cp.wait()

    out_shape = []
    for a, ax, s in zip(arrs, axes, sizes):
        shp = list(a.shape)
        shp[ax] = s
        out_shape.append(jax.ShapeDtypeStruct((N_DEV,) + tuple(shp), a.dtype))
    return pl.pallas_call(
        body, name=name, out_shape=tuple(out_shape),
        in_specs=[HBM_SPEC] * n, out_specs=tuple([HBM_SPEC] * n),
        scratch_shapes=[pltpu.SemaphoreType.DMA((7 * n,)), pltpu.SemaphoreType.DMA((7 * n,)),
                        pltpu.SemaphoreType.DMA((n,))],
    )(*arrs)


SEM_SPEC = pl.BlockSpec(memory_space=pltpu.SEMAPHORE)
_EFFECT = pltpu.SideEffectType.DATAFLOW_SIDE_EFFECTING
N_PEERS = N_DEV - 1


def _split_copies(kind, src_ref, land_ref, send_sems, recv_sems, axis, size, arriving):
    me = _my_pos()
    out = []
    for dd in range(1, N_DEV):
        peer = _flip(me, dd)
        if kind == "gather":
            src, dst = src_ref, _slab(land_ref, axis, _lin(peer if arriving else me), size)
        else:
            src, dst = _slab(src_ref, axis, _lin(peer), size), land_ref.at[dd - 1]
        out.append(pltpu.make_async_remote_copy(
            src_ref=src, dst_ref=dst, send_sem=send_sems.at[dd - 1], recv_sem=recv_sems.at[dd - 1],
            device_id=peer, device_id_type=MESH))
    return out


def _xfer_start(kind, src, land, axis, name):
    size = src.shape[axis] if kind == "gather" else src.shape[axis] // N_DEV

    def body(src_ref, land_ref, send_sems, recv_sems, src_thru, land_thru, token):
        for cp in _split_copies(kind, src_ref, land_ref, send_sems, recv_sems, axis, size, False):
            cp.start()
        token[...] = jnp.zeros_like(token)

    return pl.pallas_call(
        body, name=name,
        out_shape=(pltpu.SemaphoreType.DMA((N_PEERS,)), pltpu.SemaphoreType.DMA((N_PEERS,)),
                   pltpu.HBM(src.shape, src.dtype), pltpu.HBM(land.shape, land.dtype),
                   jax.ShapeDtypeStruct((SUBLANES, LANES), F32)),
        in_specs=(HBM_SPEC, HBM_SPEC),
        out_specs=(SEM_SPEC, SEM_SPEC, HBM_SPEC, HBM_SPEC, pl.BlockSpec(memory_space=pltpu.VMEM)),
        input_output_aliases={0: 2, 1: 3},
        compiler_params=pltpu.CompilerParams(has_side_effects=_EFFECT),
    )(pltpu.with_memory_space_constraint(src, pltpu.HBM), pltpu.with_memory_space_constraint(land, pltpu.HBM))


def _xfer_wait(kind, handle, axis, after, name):
    send_sems, recv_sems, src_thru, land_thru, _ = handle
    size = src_thru.shape[axis] if kind == "gather" else src_thru.shape[axis] // N_DEV

    def body(src_ref, land_ref, send_sems, recv_sems, after_ref, src_dead, got_ref):
        for cp in _split_copies(kind, src_ref, land_ref, send_sems, recv_sems, axis, size, True):
            cp.wait_send()
            cp.wait_recv()

    return pl.pallas_call(
        body, name=name,
        out_shape=(pltpu.HBM(src_thru.shape, src_thru.dtype), pltpu.HBM(land_thru.shape, land_thru.dtype)),
        in_specs=(HBM_SPEC, HBM_SPEC, SEM_SPEC, SEM_SPEC, pl.BlockSpec(memory_space=pl.ANY)),
        out_specs=(HBM_SPEC, HBM_SPEC), input_output_aliases={0: 0, 1: 1},
        compiler_params=pltpu.CompilerParams(has_side_effects=_EFFECT),
    )(src_thru, land_thru, send_sems, recv_sems, after)[1]


def _gather_start(shard, axis, my, name):
    shp = list(shard.shape)
    shp[axis] *= N_DEV
    start = [0] * shard.ndim
    start[axis] = my * shard.shape[axis]
    land = lax.dynamic_update_slice(lax.empty(tuple(shp), shard.dtype), shard, tuple(start))
    return _xfer_start("gather", shard, land, axis, name)


def _scatter_start(grad, axis, my, name):
    shp = list(grad.shape)
    shp[axis] //= N_DEV
    own = lax.dynamic_slice_in_dim(grad, my * shp[axis], shp[axis], axis)
    return _xfer_start("scatter", grad, lax.empty((N_PEERS,) + tuple(shp), grad.dtype), axis, name), own, axis


def _tok(a, *tokens):
    for t in tokens:
        a = a + t[0, 0].astype(a.dtype)
    return a


def _mm(a, b, *, name, ta=False, tb=False, out_dtype=F32, a_fn=None, out_fn=None, extras=(), bias=None,
        tm=1024, tn=1024, tk=2048):
    if ta:
        kdim, m = a.shape
    else:
        m, kdim = a.shape
    if tb:
        n, kb = b.shape
    else:
        kb, n = b.shape
    assert kb == kdim, (a.shape, b.shape, ta, tb)
    tm = _pick(m, tm, LANES if ta else SUBLANES)
    tn = _pick(n, tn, LANES)
    tk = _pick(kdim, tk, LANES)
    nk = kdim // tk
    ne = len(extras)
    has_bias = bias is not None

    def body(*refs):
        a_ref, b_ref = refs[0], refs[1]
        pos = 2
        bias_ref = None
        if has_bias:
            bias_ref = refs[pos]
            pos += 1
        extra_refs = refs[pos:pos + ne]
        o_ref = refs[pos + ne]
        av = a_ref[...]
        if a_fn is not None:
            av = a_fn(av)
        dn = (((0 if ta else 1,), (1 if tb else 0,)), ((), ()))
        part = lax.dot_general(av.astype(BF16), b_ref[...].astype(BF16), dn, preferred_element_type=F32)

        def finish(r):
            if has_bias:
                r = r + bias_ref[...]
            if out_fn is not None:
                r = out_fn(r, *[e[...] for e in extra_refs])
            o_ref[...] = r.astype(out_dtype)

        if nk == 1:
            finish(part)
            return
        acc = refs[pos + ne + 1]
        k = pl.program_id(2)

        @pl.when(k == 0)
        def _():
            acc[...] = part

        @pl.when(k > 0)
        def _():
            acc[...] += part

        @pl.when(k == nk - 1)
        def _():
            finish(acc[...])

    in_specs = [
        pl.BlockSpec((tk, tm), lambda i, j, k: (k, i)) if ta else pl.BlockSpec((tm, tk), lambda i, j, k: (i, k)),
        pl.BlockSpec((tn, tk), lambda i, j, k: (j, k)) if tb else pl.BlockSpec((tk, tn), lambda i, j, k: (k, j)),
    ]
    args = [a, b]
    if has_bias:
        in_specs.append(pl.BlockSpec((1, tn), lambda i, j, k: (0, j)))
        args.append(bias)
    for e in extras:
        in_specs.append(pl.BlockSpec((tm, tn), lambda i, j, k: (i, j)))
        args.append(e)
    return pl.pallas_call(
        body, name=name, grid=(m // tm, n // tn, nk),
        in_specs=in_specs, out_specs=pl.BlockSpec((tm, tn), lambda i, j, k: (i, j)),
        out_shape=jax.ShapeDtypeStruct((m, n), out_dtype),
        scratch_shapes=[pltpu.VMEM((tm, tn), F32)] if nk > 1 else [],
        compiler_params=_params(("parallel", "parallel", "arbitrary")),
    )(*args)


def _row(arr, cb=0, rb=0, w=None):
    return ("row", arr, cb, rb, w)


def _bc(arr, cb=0, w=None):
    return ("bc", arr, cb, 0, w)


def _full(arr):
    return ("full", arr, 0, 0, None)


def _rowwise(fn, ins, outs, *, nr, tr, nc, cw, name):
    n_in = len(ins)
    outs = [(o[0], o[1], (o[2] if len(o) > 2 and o[2] else cw)) for o in outs]

    def body(*refs):
        in_refs, out_refs = refs[:n_in], refs[n_in:]
        j, i = pl.program_id(0), pl.program_id(1)
        res = fn(i, j, *[r[...] for r in in_refs])
        if not isinstance(res, (tuple, list)):
            res = (res,)
        for (kind, _, _), o_ref, r in zip(outs, out_refs, res):
            if kind == "row":
                o_ref[...] = r.astype(o_ref.dtype)
            else:
                @pl.when(i == 0)
                def _(o_ref=o_ref):
                    o_ref[...] = jnp.zeros_like(o_ref)

                o_ref[...] += r

    in_specs, args = [], []
    for kind, arr, cb, rb, w in ins:
        w = w or cw
        if kind == "row":
            in_specs.append(pl.BlockSpec((tr, w), lambda j, i, cb=cb, rb=rb: (i + rb, j + cb)))
        elif kind == "bc":
            in_specs.append(pl.BlockSpec((arr.shape[0], w), lambda j, i, cb=cb: (0, j + cb)))
        else:
            in_specs.append(pl.BlockSpec(arr.shape, lambda j, i, nd=arr.ndim: (0,) * nd))
        args.append(arr)
    out_specs, out_shape = [], []
    for kind, spec, w in outs:
        if kind == "row":
            out_specs.append(pl.BlockSpec((tr, w), lambda j, i: (i, j)))
            out_shape.append(jax.ShapeDtypeStruct((nr * tr, nc * w), spec))
        else:
            out_specs.append(pl.BlockSpec((spec, w), lambda j, i: (0, j)))
            out_shape.append(jax.ShapeDtypeStruct((spec, nc * w), F32))
    res = pl.pallas_call(
        body, name=name, grid=(nc, nr), in_specs=in_specs, out_specs=tuple(out_specs), out_shape=tuple(out_shape),
        compiler_params=_params(("parallel", "arbitrary")),
    )(*args)
    return res


def _sum_parts(own, parts, *, name):
    p, r, c = parts.shape
    tr = _pick(r, max(SUBLANES, (1 << 18) // c), SUBLANES)

    def body(o_ref, p_ref, g_out):
        g = o_ref[...].astype(F32)
        for q in range(p):
            g = g + p_ref[q].astype(F32)
        g_out[...] = g

    blk = pl.BlockSpec((tr, c), lambda i: (i, 0))
    return pl.pallas_call(
        body, name=name, grid=(r // tr,), in_specs=[blk, pl.BlockSpec((p, tr, c), lambda i: (0, i, 0))],
        out_specs=blk, out_shape=jax.ShapeDtypeStruct((r, c), F32), compiler_params=_params(("parallel",)),
    )(own, parts)


def _adamw(parts, w, m, v, *, name, own=None):
    p, r, c = parts.shape
    tr = _pick(r, max(SUBLANES, (1 << 18) // c), 16 if parts.dtype == BF16 else SUBLANES)
    bc1 = 1.0 - ADAM_B1 ** ADAM_STEP
    bc2 = 1.0 - ADAM_B2 ** ADAM_STEP
    has_own = own is not None

    def body(*refs):
        p_ref, w_ref, m_ref, v_ref = refs[:4]
        g_out, d_out, m_out, v_out = refs[4 + has_own:]
        if has_own:
            g = refs[4][...].astype(F32) + p_ref[0].astype(F32)
        else:
            g = p_ref[0].astype(F32)
        for q in range(1, p):
            g = g + p_ref[q].astype(F32)
        m_new = ADAM_B1 * m_ref[...] + (1.0 - ADAM_B1) * g
        v_new = ADAM_B2 * v_ref[...] + (1.0 - ADAM_B2) * (g * g)
        m_hat = m_new / bc1
        v_hat = v_new / bc2
        g_out[...] = g
        d_out[...] = -ADAM_LR * (m_hat / (jnp.sqrt(v_hat) + ADAM_EPS) + ADAM_WD * w_ref[...])
        m_out[...] = m_new
        v_out[...] = v_new

    blk = pl.BlockSpec((tr, c), lambda i: (i, 0))
    return pl.pallas_call(
        body, name=name, grid=(r // tr,),
        in_specs=[pl.BlockSpec((p, tr, c), lambda i: (0, i, 0)), blk, blk, blk] + ([blk] if has_own else []),
        out_specs=(blk, blk, blk, blk), out_shape=tuple(jax.ShapeDtypeStruct((r, c), F32) for _ in range(4)),
        compiler_params=_params(("parallel",)),
    )(*([parts, w, m, v] + ([own] if has_own else [])))


_GELU_C = math.sqrt(2.0 / math.pi)
_GELU_K = 0.044715


def _sigmoid(z):
    return 1.0 / (1.0 + jnp.exp(-z))


def _silu(z):
    return z * _sigmoid(z)


def _dsilu(z):
    s = _sigmoid(z)
    return s * (1.0 + z * (1.0 - s))


def _gelu(z):
    return 0.5 * z * (1.0 + jnp.tanh(_GELU_C * (z + _GELU_K * z * z * z)))


def _dgelu(z):
    t = jnp.tanh(_GELU_C * (z + _GELU_K * z * z * z))
    return 0.5 * (1.0 + t) + 0.5 * z * (1.0 - t * t) * (_GELU_C * (1.0 + 3.0 * _GELU_K * z * z))


def _softplus(z):
    return jnp.maximum(z, 0.0) + jnp.log(1.0 + jnp.exp(-jnp.abs(z)))


def _expm1(z):
    small = z * (1.0 + z * (0.5 + z * (1.0 / 6.0 + z * (1.0 / 24.0))))
    return jnp.where(jnp.abs(z) < 0.03, small, jnp.exp(z) - 1.0)


def _rms(xv):
    return lax.rsqrt(jnp.mean(xv * xv, axis=-1, keepdims=True) + NORM_EPS)


def _rms_bwd(xhat, r, dxhat):
    return r * (dxhat - xhat * jnp.mean(dxhat * xhat, axis=-1, keepdims=True))


def _swap_pairs(y):
    lane = lax.broadcasted_iota(jnp.int32, y.shape, 1)
    return jnp.where(lane % 2 == 0, pltpu.roll(y, LANES - 1, 1), pltpu.roll(y, 1, 1))


def _dot(a, b, ca, cb):
    return lax.dot_general(a, b, (((ca,), (cb,)), ((), ())), preferred_element_type=F32)


def _softmax_rows(qh, kk):
    s = _dot(qh, kk, 1, 1) * (HEAD_DIM ** -0.5)
    p = jnp.exp(s - jnp.max(s, axis=-1, keepdims=True))
    return p / jnp.sum(p, axis=-1, keepdims=True)


def _attn_fwd(q, k, v, *, tq, name):
    s_len, aw = q.shape
    t_len, kvw = k.shape
    gw = Q_GROUP * HEAD_DIM

    def body(q_ref, k_ref, v_ref, o_ref):
        kk, vv = k_ref[...], v_ref[...]
        for g in range(Q_GROUP):
            cols = slice(g * HEAD_DIM, (g + 1) * HEAD_DIM)
            p = _softmax_rows(q_ref[:, cols], kk)
            o_ref[:, cols] = _dot(p.astype(BF16), vv, 1, 0).astype(BF16)

    return pl.pallas_call(
        body, name=name, grid=(kvw // HEAD_DIM, s_len // tq),
        in_specs=[pl.BlockSpec((tq, gw), lambda h, i: (i, h)), pl.BlockSpec((t_len, HEAD_DIM), lambda h, i: (0, h)),
                  pl.BlockSpec((t_len, HEAD_DIM), lambda h, i: (0, h))],
        out_specs=pl.BlockSpec((tq, gw), lambda h, i: (i, h)),
        out_shape=jax.ShapeDtypeStruct((s_len, aw), BF16),
        compiler_params=_params(("parallel", "parallel")),
    )(q, k, v)


def _attn_bwd(q, k, v, do, *, tq, name):
    s_len, aw = q.shape
    t_len, kvw = k.shape
    gw = Q_GROUP * HEAD_DIM

    def body(q_ref, k_ref, v_ref, do_ref, dq_ref, dk_ref, dv_ref):
        @pl.when(pl.program_id(1) == 0)
        def _():
            dk_ref[...] = jnp.zeros_like(dk_ref)
            dv_ref[...] = jnp.zeros_like(dv_ref)

        kk, vv = k_ref[...], v_ref[...]
        for g in range(Q_GROUP):
            cols = slice(g * HEAD_DIM, (g + 1) * HEAD_DIM)
            qh, doh = q_ref[:, cols], do_ref[:, cols]
            p = _softmax_rows(qh, kk)
            dp = _dot(doh, vv, 1, 1)
            ds = p * (dp - jnp.sum(p * dp, axis=-1, keepdims=True)) * (HEAD_DIM ** -0.5)
            dsb = ds.astype(BF16)
            dq_ref[:, cols] = _dot(dsb, kk, 1, 0)
            dk_ref[...] += _dot(dsb, qh, 0, 0)
            dv_ref[...] += _dot(p.astype(BF16), doh, 0, 0)

    qspec = pl.BlockSpec((tq, gw), lambda h, i: (i, h))
    kspec = pl.BlockSpec((t_len, HEAD_DIM), lambda h, i: (0, h))
    return pl.pallas_call(
        body, name=name, grid=(kvw // HEAD_DIM, s_len // tq),
        in_specs=[qspec, kspec, kspec, qspec], out_specs=(qspec, kspec, kspec),
        out_shape=(jax.ShapeDtypeStruct((s_len, aw), F32), jax.ShapeDtypeStruct((t_len, kvw), F32),
                   jax.ShapeDtypeStruct((t_len, kvw), F32)),
        compiler_params=_params(("parallel", "arbitrary")),
    )(q, k, v, do)


def _seg_ok(t, d, l_len, t_len):
    td = t + d
    return ((t < l_len) & (td >= 0) & (td < l_len)) | ((t >= l_len) & (td >= l_len) & (td < t_len))


def _shift_rows(xv, d, t, l_len, t_len):
    if d == 0:
        return xv
    return jnp.where(_seg_ok(t, d, l_len, t_len), pltpu.roll(xv, (-d) % t_len, 0), 0.0)


def _scan_chunks(chains, c0, n, carries):
    def step(kk, hs):
        new = []
        for (a_ref, b_ref, h_ref, rev), hprev in zip(chains, hs):
            c = c0 + (n - 1 - kk if rev else kk)
            r0 = pl.multiple_of(c * SUBLANES, SUBLANES)
            a8 = a_ref[pl.ds(r0, SUBLANES), :]
            b8 = b_ref[pl.ds(r0, SUBLANES), :]
            row = lax.broadcasted_iota(jnp.int32, a8.shape, 0)
            for s in (1, 2, 4):
                if rev:
                    ok = row < SUBLANES - s
                    sh = SUBLANES - s
                else:
                    ok = row >= s
                    sh = s
                a_sh = jnp.where(ok, pltpu.roll(a8, sh, 0), 1.0)
                b_sh = jnp.where(ok, pltpu.roll(b8, sh, 0), 0.0)
                b8 = a8 * b_sh + b8
                a8 = a8 * a_sh
            h8 = b8 + a8 * hprev
            h_ref[pl.ds(r0, SUBLANES), :] = h8
            edge = 0 if rev else SUBLANES - 1
            last = jnp.sum(jnp.where(row == edge, h8, 0.0), axis=0, keepdims=True)
            new.append(jnp.broadcast_to(last, h8.shape))
        return tuple(new)

    return lax.fori_loop(0, n, step, tuple(carries))


def _rglru_conv(x, cw_ref, cb_ref, t, l_len, t_len):
    y = cb_ref[...] + _shift_rows(x, -1, t, l_len, t_len) * cw_ref[0:1, :]
    for kk in range(1, CONV_WIDTH):
        y = y + _shift_rows(x, kk - 1, t, l_len, t_len) * cw_ref[kk:kk + 1, :]
    return y


def _rglru_gates(xc, wr, wi, br, bi, lam):
    xb = xc.astype(BF16)
    r = _sigmoid(_dot(xb, wr.astype(BF16), 1, 0) + br)
    ig = _sigmoid(_dot(xb, wi.astype(BF16), 1, 0) + bi)
    sp = _softplus(-lam)
    log_a = (-LRU_C) * r * sp
    a = jnp.exp(log_a)
    mult = jnp.sqrt(-_expm1(2.0 * log_a))
    u = ig * xc
    return r, ig, sp, a, mult, u


def _rglru_specs(proj, xr_blk, xg_blk, t_len):
    col = lambda off: pl.BlockSpec((t_len, RNN_BLOCK), lambda h, off=off: (0, off + h))
    vec = lambda rows: pl.BlockSpec((rows, RNN_BLOCK), lambda h: (0, h))
    wsp = pl.BlockSpec((2, 1, RNN_BLOCK, RNN_BLOCK), lambda h: (0, h, 0, 0))
    return col(xr_blk), col(xg_blk), vec, wsp


def _rglru_fwd(proj, xr_blk, xg_blk, conv_w, conv_b, w_rg, w_ig, b_rg, b_ig, lam, *, l_len, name):
    t_len = proj.shape[0]
    s_len = t_len - l_len
    d_rnn = conv_b.shape[1]
    n_l, n_t = l_len // SUBLANES, t_len // SUBLANES

    def body(xr_ref, xg_ref, cw_ref, cb_ref, wr_ref, wi_ref, br_ref, bi_ref, lam_ref, rg_ref, a_s, b_s, h_s):
        t = lax.broadcasted_iota(jnp.int32, (t_len, RNN_BLOCK), 0)
        xc = _rglru_conv(xr_ref[...], cw_ref, cb_ref, t, l_len, t_len)
        for d in range(2):
            _, _, _, a, mult, u = _rglru_gates(xc, wr_ref[d, 0], wi_ref[d, 0], br_ref[d:d + 1, :],
                                               bi_ref[d:d + 1, :], lam_ref[d:d + 1, :])
            a_s[d] = a
            b_s[d] = mult * u
        chains = [(a_s.at[0], b_s.at[0], h_s.at[0], False), (a_s.at[1], b_s.at[1], h_s.at[1], True)]
        zero = jnp.zeros((SUBLANES, RNN_BLOCK), F32)
        carry = _scan_chunks(chains, 0, n_l, (zero, zero))
        _scan_chunks(chains, n_l, n_t - n_l, carry)
        rnn = h_s[0, pl.ds(l_len, s_len), :] + h_s[1, pl.ds(l_len, s_len), :]
        rg_ref[...] = (rnn * _gelu(xg_ref[pl.ds(l_len, s_len), :])).astype(BF16)

    xr_spec, xg_spec, vec, wsp = _rglru_specs(proj, xr_blk, xg_blk, t_len)
    return pl.pallas_call(
        body, name=name, grid=(d_rnn // RNN_BLOCK,),
        in_specs=[xr_spec, xg_spec, vec(CONV_WIDTH), vec(1), wsp, wsp, vec(2), vec(2), vec(2)],
        out_specs=pl.BlockSpec((s_len, RNN_BLOCK), lambda h: (0, h)),
        out_shape=jax.ShapeDtypeStruct((s_len, d_rnn), BF16),
        scratch_shapes=[pltpu.VMEM((2, t_len, RNN_BLOCK), F32)] * 3,
        compiler_params=_params(("parallel",)),
    )(proj, proj, conv_w, conv_b, w_rg, w_ig, b_rg, b_ig, lam)


N_RNN_VEC = 16


def _rglru_bwd(proj, xr_blk, xg_blk, drg, conv_w, conv_b, w_rg, w_ig, b_rg, b_ig, lam, *, l_len, name):
    t_len = proj.shape[0]
    s_len = t_len - l_len
    d_rnn = conv_b.shape[1]
    n_l, n_t = l_len // SUBLANES, t_len // SUBLANES

    def body(xr_ref, xg_ref, drg_ref, cw_ref, cb_ref, wr_ref, wi_ref, br_ref, bi_ref, lam_ref,
             dxr_ref, dxg_ref, dvec_ref, dwr_ref, dwi_ref, a_s, b_s, h_s):
        t = lax.broadcasted_iota(jnp.int32, (t_len, RNN_BLOCK), 0)
        x = xr_ref[...]
        xc = _rglru_conv(x, cw_ref, cb_ref, t, l_len, t_len)
        xb = xc.astype(BF16)
        gates = []
        for d in range(2):
            gt = _rglru_gates(xc, wr_ref[d, 0], wi_ref[d, 0], br_ref[d:d + 1, :], bi_ref[d:d + 1, :],
                              lam_ref[d:d + 1, :])
            gates.append(gt)
            a_s[d] = gt[3]
            b_s[d] = gt[4] * gt[5]
        zero = jnp.zeros((SUBLANES, RNN_BLOCK), F32)
        chains = [(a_s.at[0], b_s.at[0], h_s.at[0], False), (a_s.at[1], b_s.at[1], h_s.at[1], True)]
        carry = _scan_chunks(chains, 0, n_l, (zero, zero))
        _scan_chunks(chains, n_l, n_t - n_l, carry)

        h0, h1 = h_s[0], h_s[1]
        xg = xg_ref[...]
        dr = jnp.concatenate([jnp.zeros((l_len, RNN_BLOCK), F32), drg_ref[...].astype(F32)], axis=0)
        dxg_ref[...] = (dr * (h0 + h1) * _dgelu(xg)).astype(dxg_ref.dtype)
        drnn = dr * _gelu(xg)
        a_s[0] = pltpu.roll(gates[0][3], t_len - 1, 0)
        a_s[1] = pltpu.roll(gates[1][3], 1, 0)
        b_s[0] = drnn
        b_s[1] = drnn
        hprev = [jnp.where(t == 0, 0.0, pltpu.roll(h0, 1, 0)),
                 jnp.where(t == l_len - 1, 0.0, pltpu.roll(h1, t_len - 1, 0))]
        chains = [(a_s.at[0], b_s.at[0], h_s.at[0], True), (a_s.at[1], b_s.at[1], h_s.at[1], False)]
        carry = _scan_chunks(chains, n_l, n_t - n_l, (zero, zero))
        _scan_chunks(chains, 0, n_l, carry)

        dxc = jnp.zeros((t_len, RNN_BLOCK), F32)
        for d in range(2):
            r, ig, sp, a, mult, u = gates[d]
            lamb = h_s[d]
            da = lamb * hprev[d]
            dmult = lamb * u
            du = lamb * mult
            dlog_a = da * a - dmult * (a * a) / mult
            dzr = (dlog_a * ((-LRU_C) * sp)) * r * (1.0 - r)
            dsp = jnp.sum(dlog_a * ((-LRU_C) * r), axis=0, keepdims=True)
            dzi = (du * xc) * ig * (1.0 - ig)
            dzrb, dzib = dzr.astype(BF16), dzi.astype(BF16)
            dxc = dxc + du * ig + _dot(dzrb, wr_ref[d, 0].astype(BF16), 1, 1) + _dot(dzib, wi_ref[d, 0].astype(BF16), 1, 1)
            dwr_ref[d, 0] = _dot(xb, dzrb, 0, 0)
            dwi_ref[d, 0] = _dot(xb, dzib, 0, 0)
            dvec_ref[5 + d:6 + d, :] = jnp.sum(dzr, axis=0, keepdims=True)
            dvec_ref[7 + d:8 + d, :] = jnp.sum(dzi, axis=0, keepdims=True)
            dvec_ref[9 + d:10 + d, :] = dsp * (-_sigmoid(-lam_ref[d:d + 1, :]))
        dvec_ref[4:5, :] = jnp.sum(dxc, axis=0, keepdims=True)
        dvec_ref[11:N_RNN_VEC, :] = jnp.zeros((N_RNN_VEC - 11, RNN_BLOCK), F32)
        dxr = jnp.zeros((t_len, RNN_BLOCK), F32)
        for kk in range(CONV_WIDTH):
            dvec_ref[kk:kk + 1, :] = jnp.sum(dxc * _shift_rows(x, kk - 1, t, l_len, t_len), axis=0, keepdims=True)
            dxr = dxr + _shift_rows(dxc, 1 - kk, t, l_len, t_len) * cw_ref[kk:kk + 1, :]
        dxr_ref[...] = dxr.astype(dxr_ref.dtype)

    xr_spec, xg_spec, vec, wsp = _rglru_specs(proj, xr_blk, xg_blk, t_len)
    col_t = pl.BlockSpec((t_len, RNN_BLOCK), lambda h: (0, h))
    return pl.pallas_call(
        body, name=name, grid=(d_rnn // RNN_BLOCK,),
        in_specs=[xr_spec, xg_spec, pl.BlockSpec((s_len, RNN_BLOCK), lambda h: (0, h)), vec(CONV_WIDTH), vec(1),
                  wsp, wsp, vec(2), vec(2), vec(2)],
        out_specs=(col_t, col_t, vec(N_RNN_VEC), wsp, wsp),
        out_shape=(jax.ShapeDtypeStruct((t_len, d_rnn), BF16), jax.ShapeDtypeStruct((t_len, d_rnn), BF16),
                   jax.ShapeDtypeStruct((N_RNN_VEC, d_rnn), F32), jax.ShapeDtypeStruct(w_rg.shape, F32),
                   jax.ShapeDtypeStruct(w_ig.shape, F32)),
        scratch_shapes=[pltpu.VMEM((2, t_len, RNN_BLOCK), F32)] * 3,
        compiler_params=_params(("parallel",)),
    )(proj, proj, drg, conv_w, conv_b, w_rg, w_ig, b_rg, b_ig, lam)


def _colsum(v):
    return jnp.sum(v, axis=0, keepdims=True)


def _qk_fwd_heads(xv, gain, cos_e, sin_e):
    outs = []
    for h in range(xv.shape[1] // HEAD_DIM):
        xh = xv[:, h * HEAD_DIM:(h + 1) * HEAD_DIM]
        y = xh * _rms(xh) * gain
        outs.append(y * cos_e + _swap_pairs(y) * sin_e)
    return outs[0] if len(outs) == 1 else jnp.concatenate(outs, axis=1)


def _qk_bwd_heads(xv, dov, gain, cos_e, sin_e):
    outs = []
    dgain = jnp.zeros((1, HEAD_DIM), F32)
    for h in range(xv.shape[1] // HEAD_DIM):
        cols = slice(h * HEAD_DIM, (h + 1) * HEAD_DIM)
        xh, doh = xv[:, cols], dov[:, cols]
        r = _rms(xh)
        xhat = xh * r
        dy = doh * cos_e + _swap_pairs(doh * sin_e)
        dgain = dgain + _colsum(dy * xhat)
        outs.append(_rms_bwd(xhat, r, dy * gain))
    return (outs[0] if len(outs) == 1 else jnp.concatenate(outs, axis=1)), dgain


WEIGHTS = ['c_ctx', 'w_mod', 'b_mod', 'g_mix', 'g_mlp', 'w_in', 'q_gain', 'k_gain', 'conv_w', 'conv_b', 'w_rg', 'b_rg',
           'w_ig', 'b_ig', 'lru_lambda', 'w_o_attn', 'w_o_rnn', 'w_out', 'w_up', 'w_down', 'g_final']
REPLICATED = ['c_ctx', 'b_mod', 'g_mix', 'g_mlp', 'q_gain', 'k_gain', 'conv_b', 'w_rg', 'w_ig', 'g_final']
SMALL_SHARDED = [('conv_w', CONV_WIDTH), ('b_rg', 2), ('b_ig', 2), ('lru_lambda', 2)]
SMALL_ROWS = 16


PACK_ROWS = 512


def _pack_flat(arrs, rows_mult):
    flat = jnp.concatenate([a.reshape(-1) for a in arrs])
    n = flat.shape[0]
    unit = rows_mult * LANES
    pad = (-n) % unit
    if pad:
        flat = jnp.concatenate([flat, jnp.zeros((pad,), flat.dtype)])
    return flat.reshape(-1, LANES)


def _unpack_flat(packed, shapes):
    flat = packed.reshape(-1)
    out, pos = [], 0
    for shp in shapes:
        n = math.prod(shp)
        out.append(flat[pos:pos + n].reshape(shp))
        pos += n
    return out


def _pack_small(d):
    rows = [d[n][0] for n, _ in SMALL_SHARDED]
    cols = rows[0].shape[1]
    used = sum(k for _, k in SMALL_SHARDED)
    return jnp.concatenate(rows + [jnp.zeros((SMALL_ROWS - used, cols), F32)], axis=0)


def _unpack_small(p):
    out, pos = {}, 0
    for n, k in SMALL_SHARDED:
        out[n] = p[pos:pos + k][None]
        pos += k
    return out


def kernel(x, c, ctx, c_ctx, w_mod, b_mod, g_mix, g_mlp, w_in, q_gain, k_gain, conv_w, conv_b, w_rg, b_rg, w_ig, b_ig, lru_lambda, w_o_attn, w_o_rnn, w_out, w_up, w_down, g_final, loss_target, m_c_ctx, m_w_mod, m_b_mod, m_g_mix, m_g_mlp, m_w_in, m_q_gain, m_k_gain, m_conv_w, m_conv_b, m_w_rg, m_b_rg, m_w_ig, m_b_ig, m_lru_lambda, m_w_o_attn, m_w_o_rnn, m_w_out, m_w_up, m_w_down, m_g_final, v_c_ctx, v_w_mod, v_b_mod, v_g_mix, v_g_mlp, v_w_in, v_q_gain, v_k_gain, v_conv_w, v_conv_b, v_w_rg, v_b_rg, v_w_ig, v_b_ig, v_lru_lambda, v_w_o_attn, v_w_o_rnn, v_w_out, v_w_up, v_w_down, v_g_final):
    wd = dict(c_ctx=c_ctx, w_mod=w_mod, b_mod=b_mod, g_mix=g_mix, g_mlp=g_mlp, w_in=w_in, q_gain=q_gain, k_gain=k_gain,
              conv_w=conv_w, conv_b=conv_b, w_rg=w_rg, b_rg=b_rg, w_ig=w_ig, b_ig=b_ig, lru_lambda=lru_lambda,
              w_o_attn=w_o_attn, w_o_rnn=w_o_rnn, w_out=w_out, w_up=w_up, w_down=w_down, g_final=g_final)
    md = dict(zip(WEIGHTS, (m_c_ctx, m_w_mod, m_b_mod, m_g_mix, m_g_mlp, m_w_in, m_q_gain, m_k_gain, m_conv_w, m_conv_b,
                            m_w_rg, m_b_rg, m_w_ig, m_b_ig, m_lru_lambda, m_w_o_attn, m_w_o_rnn, m_w_out, m_w_up,
                            m_w_down, m_g_final)))
    vd = dict(zip(WEIGHTS, (v_c_ctx, v_w_mod, v_b_mod, v_g_mix, v_g_mlp, v_w_in, v_q_gain, v_k_gain, v_conv_w, v_conv_b,
                            v_w_rg, v_b_rg, v_w_ig, v_b_ig, v_lru_lambda, v_w_o_attn, v_w_o_rnn, v_w_out, v_w_up,
                            v_w_down, v_g_final)))

    xs, tgt, ctxs = x[0], loss_target[0], ctx[0]
    s_len, d = xs.shape
    l_len = ctxs.shape[0]
    t_len = s_len + l_len
    n_in = w_in.shape[2] * N_DEV
    aw = (n_in - 4 * d) * 2 // 3
    kvw = aw // Q_GROUP
    assert aw == d and n_in == aw + 2 * kvw + 4 * d
    nmod = N_MOD * d // N_DEV
    tr = 128
    assert l_len % tr == 0 and s_len % tr == 0
    lb = l_len // tr
    tq = min(256, s_len)
    my = _lin(_my_pos())
    c_ctx2 = c_ctx[None]
    g_final2 = g_final[None]

    c_all = _all_gather([jnp.broadcast_to(c, (SUBLANES, d))], [0], "ag_cond")[0][::SUBLANES]
    c16 = jnp.concatenate([c_all, jnp.broadcast_to(c_ctx2, (N_DEV, d))], axis=0)
    b_mod_sh = lax.dynamic_slice(b_mod, (0, my * nmod), (1, nmod))
    mod_part = _mm(c16, w_mod[0], a_fn=_silu, bias=b_mod_sh, name="mod_fwd")
    mod_g = _all_gather([mod_part], [1], "ag_mod")[0]
    mod_lat = lax.dynamic_slice(mod_g, (my, 0), (1, N_MOD * d))
    sh_a, sc_a, ga_a, sh_f, sc_f, ga_f = [mod_lat[:, k * d:(k + 1) * d] for k in range(N_MOD)]
    csh_a, csc_a = mod_g[N_DEV:N_DEV + 1, :d], mod_g[N_DEV:N_DEV + 1, d:2 * d]

    sp_w = _pack_small(wd)
    (w_in_f, sp_f), order = _all_gather([w_in[0].astype(BF16), sp_w], [1, 1], "ag_w_in", with_token=True)
    conv_w_f, b_rg_f, b_ig_f, lam_f = sp_f[0:4], sp_f[4:6], sp_f[6:8], sp_f[8:10]
    late_axis = dict(w_o_attn=0, w_o_rnn=0, w_out=0, w_up=1, w_down=0)
    late = {}
    for n, ax in late_axis.items():
        late[n] = _gather_start(_tok(wd[n][0], order).astype(BF16), ax, my, "ag_start_" + n)
        order = late[n][4]
    g_mix_fwd = _tok(g_mix, order)
    late_wait = lambda n, after: _xfer_wait("gather", late[n], late_axis[n], after, "ag_wait_" + n)

    pos = jnp.arange(s_len)
    n_freq = HEAD_DIM // 4
    inv_freq = ROPE_THETA ** (-jnp.arange(n_freq, dtype=F32) / n_freq)
    ang = jnp.concatenate([(pos // GRID_W).astype(F32)[:, None] * inv_freq,
                           (pos % GRID_W).astype(F32)[:, None] * inv_freq], axis=-1)
    sign = jnp.where(jnp.arange(HEAD_DIM) % 2 == 0, -1.0, 1.0).astype(F32)
    cos_e = jnp.concatenate([jnp.ones((l_len, HEAD_DIM), F32), jnp.repeat(jnp.cos(ang), 2, axis=-1)], axis=0)
    sin_e = jnp.concatenate([jnp.zeros((l_len, HEAD_DIM), F32), jnp.repeat(jnp.sin(ang), 2, axis=-1) * sign], axis=0)

    x_all = jnp.concatenate([ctxs, xs], axis=0)

    def nm1_fwd(i, j, xv, g, csh, csc, sh, sc):
        is_ctx = i < lb
        n = xv * _rms(xv) * g
        return n * (1.0 + jnp.where(is_ctx, csc, sc)) + jnp.where(is_ctx, csh, sh)

    h_all = _rowwise(nm1_fwd, [_row(x_all), _bc(g_mix_fwd), _bc(csh_a), _bc(csc_a), _bc(sh_a), _bc(sc_a)],
                     [("row", BF16)], nr=t_len // tr, tr=tr, nc=1, cw=d, name="norm1_fwd")[0]
    proj = _mm(h_all, w_in_f, name="proj_fwd")

    kb = aw // kvw
    q = _rowwise(lambda i, j, xv, gain, ce, se: _qk_fwd_heads(xv, gain, ce, se),
                 [_row(proj, 0, lb, aw), _full(q_gain), _row(cos_e, 0, lb, HEAD_DIM), _row(sin_e, 0, lb, HEAD_DIM)],
                 [("row", BF16, aw)], nr=s_len // tr, tr=tr, nc=1, cw=aw, name="q_fwd")[0]
    k_all, v_all = _rowwise(lambda i, j, kv_, vv_, gain, ce, se: (_qk_fwd_heads(kv_, gain, ce, se), vv_),
                            [_row(proj, kb, 0, kvw), _row(proj, kb + 1, 0, kvw), _full(k_gain),
                             _row(cos_e, 0, 0, HEAD_DIM), _row(sin_e, 0, 0, HEAD_DIM)],
                            [("row", BF16, kvw), ("row", BF16, kvw)], nr=t_len // tr, tr=tr, nc=1, cw=kvw, name="kv_fwd")
    attn_o = _attn_fwd(q, k_all, v_all, tq=tq, name="attn_fwd")

    xr_blk = (aw + 2 * kvw) // RNN_BLOCK
    xg_blk = xr_blk + d // RNN_BLOCK
    rg = _rglru_fwd(proj, xr_blk, xg_blk, conv_w_f, conv_b, w_rg[0], w_ig[0], b_rg_f, b_ig_f, lam_f, l_len=l_len,
                    name="rglru_fwd")

    w_oa_f = late_wait("w_o_attn", attn_o)
    y_attn = _mm(attn_o, w_oa_f, name="o_attn_fwd")
    w_or_f = late_wait("w_o_rnn", rg)
    y_rnn = _mm(rg, w_or_f, name="o_rnn_fwd")
    hw = d // 2
    glb = (aw + 2 * kvw + 2 * d) // hw

    def merge_fwd(i, j, ya, yr, gla, glr):
        return _sigmoid(gla) * ya + _sigmoid(glr) * yr

    u = _rowwise(merge_fwd, [_row(y_attn), _row(y_rnn), _row(proj, glb, lb), _row(proj, glb + 2, lb)],
                 [("row", BF16)], nr=s_len // tr, tr=tr, nc=2, cw=hw, name="merge_fwd")[0]
    w_out_f = late_wait("w_out", u)
    mix = _mm(u, w_out_f, name="out_fwd")

    def res1_fwd(i, j, xv, mixv, gaa, gm, shf, scf):
        x1v = xv + gaa * mixv
        n = x1v * _rms(x1v) * gm
        return x1v, n * (1.0 + scf) + shf

    x1, h2 = _rowwise(res1_fwd, [_row(xs), _row(mix), _bc(ga_a), _bc(g_mlp), _bc(sh_f), _bc(sc_f)],
                      [("row", F32), ("row", BF16)], nr=s_len // tr, tr=tr, nc=1, cw=d, name="norm2_fwd")
    relu2 = lambda z: jnp.square(jnp.maximum(z, 0.0))
    w_up_f = late_wait("w_up", h2)
    up = _mm(h2, w_up_f, name="up_fwd")
    w_down_f = late_wait("w_down", up)
    down = _mm(up, w_down_f, a_fn=relu2, name="down_fwd")

    def final_fn(i, j, x1v, dn, tg, gaf, gf):
        x2 = x1v + gaf * dn
        r = _rms(x2)
        xh = x2 * r
        e = xh * gf - tg
        dy = e * (1.0 / d)
        dx2 = _rms_bwd(xh, r, dy * gf)
        return dx2, dx2 * gaf, _colsum(e * e) * (0.5 / d), _colsum(dy * xh), _colsum(dx2 * dn)

    dx2, ddown, loss_cols, g_gfinal, d_ga_f = _rowwise(
        final_fn, [_row(x1), _row(down), _row(tgt), _bc(ga_f), _bc(g_final2)],
        [("row", F32), ("row", BF16), ("acc", 1), ("acc", 1), ("acc", 1)], nr=s_len // tr, tr=tr, nc=1, cw=d,
        name="final_bwd")
    loss = lax.psum(jnp.sum(loss_cols), ("x", "y", "c"))

    dup = _mm(ddown, w_down_f, tb=True, out_fn=lambda r, upv: r * (2.0 * jnp.maximum(upv, 0.0)), extras=[up],
              out_dtype=BF16, name="down_bwd_x")
    gw_down = _mm(up, ddown, ta=True, a_fn=relu2, out_dtype=BF16, name="down_bwd_w")
    sent = dict(w_down=_scatter_start(gw_down, 0, my, "rs_start_w_down"))
    dh2 = _mm(dup, w_up_f, tb=True, name="up_bwd_x")
    gw_up = _mm(h2, dup, ta=True, out_dtype=BF16, name="up_bwd_w")
    sent["w_up"] = _scatter_start(gw_up, 1, my, "rs_start_w_up")
    sc_f_bwd = _tok(sc_f, sent["w_down"][0][4], sent["w_up"][0][4])

    def norm2_bwd(i, j, x1v, mixv, dh2v, dx2v, gaa, gm, scf):
        r = _rms(x1v)
        xh = x1v * r
        n = xh * gm
        dn = dh2v * (1.0 + scf)
        dx1 = dx2v + _rms_bwd(xh, r, dn * gm)
        return dx1, dx1 * gaa, _colsum(dh2v), _colsum(dh2v * n), _colsum(dn * xh), _colsum(dx1 * mixv)

    dx1, dmix, d_sh_f, d_sc_f, g_gmlp, d_ga_a = _rowwise(
        norm2_bwd, [_row(x1), _row(mix), _row(dh2), _row(dx2), _bc(ga_a), _bc(g_mlp), _bc(sc_f_bwd)],
        [("row", F32), ("row", BF16), ("acc", 1), ("acc", 1), ("acc", 1), ("acc", 1)], nr=s_len // tr, tr=tr, nc=1,
        cw=d, name="norm2_bwd")

    du = _mm(dmix, w_out_f, tb=True, name="out_bwd_x")
    gw_out = _mm(u, dmix, ta=True, out_dtype=BF16, name="out_bwd_w")
    sent["w_out"] = _scatter_start(gw_out, 0, my, "rs_start_w_out")

    def merge_bwd(i, j, duv, ya, yr, gla, glr):
        ga, gr = _sigmoid(gla), _sigmoid(glr)
        return duv * ga, duv * gr, duv * ya * ga * (1.0 - ga), duv * yr * gr * (1.0 - gr)

    dya, dyr, dgla, dglr = _rowwise(
        merge_bwd, [_row(du), _row(y_attn), _row(y_rnn), _row(proj, glb, lb), _row(proj, glb + 2, lb)],
        [("row", BF16)] * 4, nr=s_len // tr, tr=tr, nc=2, cw=hw, name="merge_bwd")
    dattn = _mm(dya, w_oa_f, tb=True, out_dtype=BF16, name="o_attn_bwd_x")
    gw_oa = _mm(attn_o, dya, ta=True, out_dtype=BF16, name="o_attn_bwd_w")
    drg = _mm(dyr, w_or_f, tb=True, out_dtype=BF16, name="o_rnn_bwd_x")
    gw_or = _mm(rg, dyr, ta=True, out_dtype=BF16, name="o_rnn_bwd_w")
    sent["w_o_attn"] = _scatter_start(gw_oa, 0, my, "rs_start_w_o_attn")
    sent["w_o_rnn"] = _scatter_start(gw_or, 0, my, "rs_start_w_o_rnn")
    conv_b_bwd = _tok(conv_b, sent["w_out"][0][4], sent["w_o_attn"][0][4], sent["w_o_rnn"][0][4])

    dq, dk, dv = _attn_bwd(q, k_all, v_all, dattn, tq=tq, name="attn_bwd")
    dxr, dxg, dvec, g_wrg, g_wig = _rglru_bwd(proj, xr_blk, xg_blk, drg, conv_w_f, conv_b_bwd, w_rg[0], w_ig[0], b_rg_f,
                                               b_ig_f, lam_f, l_len=l_len, name="rglru_bwd")
    gate_part = _pack_flat([g_wrg, g_wig], N_DEV * SUBLANES)
    gate_sent = _scatter_start(gate_part, 0, my, "rs_start_gates")
    q_gain_bwd = _tok(q_gain, gate_sent[0][4])

    dq_p, g_qgain = _rowwise(lambda i, j, xv, dov, gain, ce, se: _qk_bwd_heads(xv, dov, gain, ce, se),
                             [_row(proj, 0, lb, aw), _row(dq, 0, 0, aw), _full(q_gain_bwd), _row(cos_e, 0, lb, HEAD_DIM),
                              _row(sin_e, 0, lb, HEAD_DIM)],
                             [("row", BF16, aw), ("acc", 1, HEAD_DIM)], nr=s_len // tr, tr=tr, nc=1, cw=aw, name="q_bwd")

    def kv_bwd(i, j, xv, dkv, dvv, gain, ce, se):
        dk_p, dgain = _qk_bwd_heads(xv, dkv, gain, ce, se)
        return dk_p, dvv, dgain

    dk_p, dv_p, g_kgain = _rowwise(kv_bwd, [_row(proj, kb, 0, kvw), _row(dk, 0, 0, kvw), _row(dv, 0, 0, kvw),
                                            _full(k_gain), _row(cos_e, 0, 0, HEAD_DIM), _row(sin_e, 0, 0, HEAD_DIM)],
                                   [("row", BF16, kvw), ("row", BF16, kvw), ("acc", 1, HEAD_DIM)], nr=t_len // tr, tr=tr,
                                   nc=1, cw=kvw, name="kv_bwd")

    pad_ctx = lambda a: jnp.pad(a, ((l_len, 0), (0, 0)))
    dproj = jnp.concatenate([pad_ctx(dq_p), dk_p, dv_p, dxr, dxg, pad_ctx(dgla), pad_ctx(dglr)], axis=1)
    gw_in = _mm(h_all, dproj, ta=True, out_dtype=BF16, name="proj_bwd_w")
    sent["w_in"] = _scatter_start(gw_in, 1, my, "rs_start_w_in")
    dh_all = _mm(dproj, w_in_f, tb=True, bias=_tok(jnp.zeros((1, d), F32), sent["w_in"][0][4]), name="proj_bwd_x")
    g_mix_bwd = g_mix

    def norm1_bwd(i, j, xv, dhv, dresv, g, csc, sc):
        is_ctx = i < lb
        r = _rms(xv)
        xh = xv * r
        dn = dhv * (1.0 + jnp.where(is_ctx, csc, sc))
        dsh, dsc = _colsum(dhv), _colsum(dhv * (xh * g))
        zero = jnp.zeros_like(dsh)
        return (dresv + _rms_bwd(xh, r, dn * g), jnp.where(is_ctx, dsh, zero), jnp.where(is_ctx, dsc, zero),
                jnp.where(is_ctx, zero, dsh), jnp.where(is_ctx, zero, dsc), _colsum(dn * xh))

    dx_all, d_csh, d_csc, d_sh_a, d_sc_a, g_gmix = _rowwise(
        norm1_bwd, [_row(x_all), _row(dh_all), _row(pad_ctx(dx1)), _bc(g_mix_bwd), _bc(csc_a), _bc(sc_a)],
        [("row", F32)] + [("acc", 1)] * 5, nr=t_len // tr, tr=tr, nc=1, cw=d, name="norm1_bwd")
    grad_x = dx_all[l_len:][None]

    dmod_lat = jnp.concatenate([d_sh_a, d_sc_a, d_ga_a, d_sh_f, d_sc_f, d_ga_f], axis=1)
    dmod_ctx = jnp.concatenate([d_csh, d_csc, jnp.zeros((1, (N_MOD - 2) * d), F32)], axis=1)
    dm_g = _all_gather([jnp.concatenate([dmod_lat, dmod_ctx, jnp.zeros((SUBLANES - 2, N_MOD * d), F32)], axis=0)], [0],
                       "ag_dmod")[0]
    dm16 = jnp.concatenate([dm_g[0::SUBLANES], dm_g[1::SUBLANES]], axis=0)
    dm_sh = lax.dynamic_slice(dm16, (0, my * nmod), (2 * N_DEV, nmod))
    gw_mod = _mm(c16, dm_sh, ta=True, a_fn=_silu, name="mod_bwd_w")
    dc16 = _mm(dm_sh, w_mod[0], tb=True, name="mod_bwd_x")

    def cctx_fn(i, j, pv, cc, dl, dc):
        row = lax.broadcasted_iota(jnp.int32, pv.shape, 0)
        return _colsum(jnp.where(row >= N_DEV, pv, 0.0)) * _dsilu(cc), dl + dc

    g_cctx, g_bmod = _rowwise(cctx_fn, [_full(dc16), _full(c_ctx2), _full(dmod_lat), _full(dmod_ctx)],
                              [("row", F32, d), ("row", F32, N_MOD * d)], nr=1, tr=1, nc=1, cw=d, name="cctx_bwd")

    rep_part = dict(c_ctx=g_cctx, b_mod=g_bmod, g_mix=g_gmix, g_mlp=g_gmlp, q_gain=g_qgain, k_gain=g_kgain,
                    conv_b=dvec[4:5], g_final=g_gfinal)
    rep_small = [n for n in REPLICATED if n in rep_part]
    gates = [n for n in REPLICATED if n not in rep_part]
    part_pack = _pack_flat([rep_part[n] for n in rep_small], PACK_ROWS)
    rep_rows = part_pack.shape[0]
    gate_land = _xfer_wait("scatter", gate_sent[0], 0, g_bmod, "rs_wait_gates")
    gate_sum = _sum_parts(gate_sent[1], gate_land, name="gate_sum")
    gate_full, parts = _all_gather([gate_sum, part_pack], [0, 0], "ag_rep_grads")
    res = {}
    for names, pr, mult, nm in ((rep_small, parts.reshape(N_DEV, rep_rows, LANES), PACK_ROWS, "adamw_rep"),
                                (gates, gate_full[None], N_DEV * SUBLANES, "adamw_gates")):
        rep_out = _adamw(pr, _pack_flat([wd[n] for n in names], mult), _pack_flat([md[n] for n in names], mult),
                         _pack_flat([vd[n] for n in names], mult), name=nm)
        for kind, packed in zip(("g", "d", "m", "v"), rep_out):
            for n, a in zip(names, _unpack_flat(packed, [wd[n].shape for n in names])):
                res[kind, n] = a

    small_g = jnp.concatenate([dvec[0:4], dvec[5:11], jnp.zeros((SMALL_ROWS - 10, d), F32)], axis=0)
    small_recv = _all_to_all([small_g], [1], "a2a_small_grads")[0]
    after = gate_full
    for n, (handle, own, ax) in sent.items():
        land = _xfer_wait("scatter", handle, ax, after, "rs_wait_" + n)
        out = _adamw(land, wd[n][0], md[n][0], vd[n][0], own=own, name="adamw_" + n)
        after = out[0]
        for kind, a in zip(("g", "d", "m", "v"), out):
            res[kind, n] = a[None]
    out = _adamw(gw_mod[None], w_mod[0], m_w_mod[0], v_w_mod[0], name="adamw_w_mod")
    for kind, a in zip(("g", "d", "m", "v"), out):
        res[kind, 'w_mod'] = a[None]
    out = _adamw(small_recv, sp_w, _pack_small(md), _pack_small(vd), name="adamw_small")
    for kind, a in zip(("g", "d", "m", "v"), out):
        for n, piece in _unpack_small(a).items():
            res[kind, n] = piece

    outs = [loss, grad_x]
    for kind in ("g", "d", "m", "v"):
        outs += [res[kind, n].reshape(wd[n].shape) for n in WEIGHTS]
    return tuple(outs)
```

```python
import functools
import math

import jax
import jax.numpy as jnp
from jax import lax
from jax.experimental import pallas as pl
from jax.experimental.pallas import tpu as pltpu

F32 = jnp.float32
BF16 = jnp.bfloat16
MESH = pl.DeviceIdType.MESH

N_DEV = 8
HEAD_DIM = 128
Q_GROUP = 4
RNN_BLOCK = 128
GRID_W = 64
CONV_WIDTH = 4
LRU_C = 8.0
ROPE_THETA = 10000.0
NORM_EPS = 1e-6
N_MOD = 6
ADAM_LR = 0.001
ADAM_B1 = 0.9
ADAM_B2 = 0.999
ADAM_EPS = 1e-08
ADAM_WD = 0.01
ADAM_STEP = 10

LANES = 128
SUBLANES = 8
VMEM_LIMIT = 56 * 1024 * 1024
HBM_SPEC = pl.BlockSpec(memory_space=pltpu.HBM)


def _params(sem=None, vmem=VMEM_LIMIT):
    kw = dict(vmem_limit_bytes=vmem)
    if sem is not None:
        kw["dimension_semantics"] = sem
    return pltpu.CompilerParams(**kw)


def _pick(dim, target, mult):
    for t in range(min(dim, target), 0, -1):
        if dim % t == 0 and t % mult == 0:
            return t
    return dim


def _my_pos():
    return lax.axis_index("x"), lax.axis_index("y"), lax.axis_index("c")


def _flip(pos, d):
    x, y, c = pos
    return (1 - x if d & 4 else x, 1 - y if d & 2 else y, 1 - c if d & 1 else c)


def _lin(pos):
    return 4 * pos[0] + 2 * pos[1] + pos[2]


def _slab(ref, axis, idx, size):
    if axis == 0:
        return ref.at[pl.ds(idx * size, size)]
    assert axis == 1
    return ref.at[:, pl.ds(idx * size, size)]


def _all_gather(arrs, axes, name, with_token=False):
    n = len(arrs)
    sizes = [a.shape[ax] for a, ax in zip(arrs, axes)]

    def body(*refs):
        ins, outs, token = refs[:n], refs[n:2 * n], refs[2 * n]
        send_sems, recv_sems, local_sems = refs[2 * n + 1:]
        token[...] = jnp.zeros_like(token)
        me = _my_pos()
        sibling = _flip(me, 1)
        chips = [2, 4, 6]

        def copy(a, k, block_pos, to, src=None):
            dst = _slab(outs[a], axes[a], _lin(block_pos), sizes[a])
            return pltpu.make_async_remote_copy(
                src_ref=dst if src is None else src, dst_ref=dst,
                send_sem=send_sems.at[7 * a + k], recv_sem=recv_sems.at[7 * a + k],
                device_id=to, device_id_type=MESH)

        local, first, passed = [], [], []
        for a in range(n):
            cp = pltpu.make_async_copy(ins[a], _slab(outs[a], axes[a], _lin(me), sizes[a]), local_sems.at[a])
            cp.start()
            local.append(cp)
        for a in range(n):
            f = [copy(a, 0, me, sibling, src=ins[a])]
            f += [copy(a, 1 + j, me, _flip(me, m), src=ins[a]) for j, m in enumerate(chips)]
            for cp in f:
                cp.start()
            first += f
        for a in range(n):
            for j, m in enumerate(chips):
                copy(a, 1 + j, _flip(me, m), me).wait_recv()
                cp = copy(a, 4 + j, _flip(me, m), sibling)
                cp.start()
                passed.append(cp)
        for a in range(n):
            copy(a, 0, sibling, me).wait_recv()
            for j, m in enumerate(chips):
                copy(a, 4 + j, _flip(sibling, m), me).wait_recv()
        for cp in first + passed:
            cp.wait_send()
        for cp in local:
            cp.wait()

    out_shape = []
    for a, ax in zip(arrs, axes):
        shp = list(a.shape)
        shp[ax] *= N_DEV
        out_shape.append(jax.ShapeDtypeStruct(tuple(shp), a.dtype))
    out_shape.append(jax.ShapeDtypeStruct((SUBLANES, LANES), F32))
    res = pl.pallas_call(
        body, name=name, out_shape=tuple(out_shape),
        in_specs=[HBM_SPEC] * n, out_specs=tuple([HBM_SPEC] * n + [pl.BlockSpec(memory_space=pltpu.VMEM)]),
        scratch_shapes=[pltpu.SemaphoreType.DMA((7 * n,)), pltpu.SemaphoreType.DMA((7 * n,)),
                        pltpu.SemaphoreType.DMA((n,))],
    )(*arrs)
    return (list(res[:n]), res[n]) if with_token else list(res[:n])


def _all_to_all(arrs, axes, name):
    n = len(arrs)
    sizes = [a.shape[ax] // N_DEV for a, ax in zip(arrs, axes)]

    def body(*refs):
        ins, outs, token = refs[:n], refs[n:2 * n], refs[2 * n]
        send_sems, recv_sems, local_sems = refs[2 * n + 1:]
        token[...] = jnp.zeros_like(token)
        me = _my_pos()
        my = _lin(me)
        copies, local = [], []
        for a in range(n):
            cp = pltpu.make_async_copy(_slab(ins[a], axes[a], my, sizes[a]), outs[a].at[my], local_sems.at[a])
            cp.start()
            local.append(cp)
        for d in range(1, N_DEV):
            peer = _flip(me, d)
            for a in range(n):
                cp = pltpu.make_async_remote_copy(
                    src_ref=_slab(ins[a], axes[a], _lin(peer), sizes[a]), dst_ref=outs[a].at[my],
                    send_sem=send_sems.at[7 * a + d - 1], recv_sem=recv_sems.at[7 * a + d - 1],
                    device_id=peer, device_id_type=MESH)
                cp.start()
                copies.append((a, d, cp))
        for a, d, cp in copies:
            peer = _flip(me, d)
            pltpu.make_async_remote_copy(
                src_ref=_slab(ins[a], axes[a], _lin(peer), sizes[a]), dst_ref=outs[a].at[_lin(peer)],
                send_sem=send_sems.at[7 * a + d - 1], recv_sem=recv_sems.at[7 * a + d - 1],
                device_id=peer, device_id_type=MESH).wait_recv()
            cp.wait_send()
        for cp in local:
            cp.wait()

    out_shape = []
    for a, ax, s in zip(arrs, axes, sizes):
        shp = list(a.shape)
        shp[ax] = s
        out_shape.append(jax.ShapeDtypeStruct((N_DEV,) + tuple(shp), a.dtype))
    out_shape.append(jax.ShapeDtypeStruct((SUBLANES, LANES), F32))
    res = pl.pallas_call(
        body, name=name, out_shape=tuple(out_shape),
        in_specs=[HBM_SPEC] * n, out_specs=tuple([HBM_SPEC] * n + [pl.BlockSpec(memory_space=pltpu.VMEM)]),
        scratch_shapes=[pltpu.SemaphoreType.DMA((7 * n,)), pltpu.SemaphoreType.DMA((7 * n,)),
                        pltpu.SemaphoreType.DMA((n,))],
    )(*arrs)
    return list(res[:n]), res[n]


SEM_SPEC = pl.BlockSpec(memory_space=pltpu.SEMAPHORE)
_EFFECT = pltpu.SideEffectType.DATAFLOW_SIDE_EFFECTING
N_PEERS = N_DEV - 1


def _split_copies(kind, src_ref, land_ref, send_sems, recv_sems, axis, size, arriving):
    me = _my_pos()
    out = []
    for dd in range(1, N_DEV):
        peer = _flip(me, dd)
        if kind == "gather":
            src, dst = src_ref, _slab(land_ref, axis, _lin(peer if arriving else me), size)
        else:
            src, dst = _slab(src_ref, axis, _lin(peer), size), land_ref.at[dd - 1]
        out.append(pltpu.make_async_remote_copy(
            src_ref=src, dst_ref=dst, send_sem=send_sems.at[dd - 1], recv_sem=recv_sems.at[dd - 1],
            device_id=peer, device_id_type=MESH))
    return out


def _xfer_start(kind, src, land, axis, name):
    size = src.shape[axis] if kind == "gather" else src.shape[axis] // N_DEV

    def body(src_ref, land_ref, send_sems, recv_sems, src_thru, land_thru, token, local_sem):
        for cp in _split_copies(kind, src_ref, land_ref, send_sems, recv_sems, axis, size, False):
            cp.start()
        if kind == "gather":
            own = pltpu.make_async_copy(src_ref, _slab(land_ref, axis, _lin(_my_pos()), size), local_sem)
            own.start()
            own.wait()
        token[...] = jnp.zeros_like(token)

    return pl.pallas_call(
        body, name=name,
        out_shape=(pltpu.SemaphoreType.DMA((N_PEERS,)), pltpu.SemaphoreType.DMA((N_PEERS,)),
                   pltpu.HBM(src.shape, src.dtype), pltpu.HBM(land.shape, land.dtype),
                   jax.ShapeDtypeStruct((SUBLANES, LANES), F32)),
        in_specs=(HBM_SPEC, HBM_SPEC),
        out_specs=(SEM_SPEC, SEM_SPEC, HBM_SPEC, HBM_SPEC, pl.BlockSpec(memory_space=pltpu.VMEM)),
        scratch_shapes=[pltpu.SemaphoreType.DMA],
        input_output_aliases={0: 2, 1: 3},
        compiler_params=pltpu.CompilerParams(has_side_effects=_EFFECT),
    )(pltpu.with_memory_space_constraint(src, pltpu.HBM), pltpu.with_memory_space_constraint(land, pltpu.HBM))


def _xfer_wait(kind, handle, axis, after, name):
    send_sems, recv_sems, src_thru, land_thru, _ = handle
    size = src_thru.shape[axis] if kind == "gather" else src_thru.shape[axis] // N_DEV

    def body(src_ref, land_ref, send_sems, recv_sems, after_ref, src_dead, got_ref):
        for cp in _split_copies(kind, src_ref, land_ref, send_sems, recv_sems, axis, size, True):
            cp.wait_send()
            cp.wait_recv()

    return pl.pallas_call(
        body, name=name,
        out_shape=(pltpu.HBM(src_thru.shape, src_thru.dtype), pltpu.HBM(land_thru.shape, land_thru.dtype)),
        in_specs=(HBM_SPEC, HBM_SPEC, SEM_SPEC, SEM_SPEC, pl.BlockSpec(memory_space=pl.ANY)),
        out_specs=(HBM_SPEC, HBM_SPEC), input_output_aliases={0: 0, 1: 1},
        compiler_params=pltpu.CompilerParams(has_side_effects=_EFFECT),
    )(src_thru, land_thru, send_sems, recv_sems, after)[1]


def _gather_start(shard, axis, my, name):
    shp = list(shard.shape)
    shp[axis] *= N_DEV
    return _xfer_start("gather", shard, lax.empty(tuple(shp), shard.dtype), axis, name)


def _scatter_start(grad, axis, my, name):
    shp = list(grad.shape)
    shp[axis] //= N_DEV
    own = lax.dynamic_slice_in_dim(grad, my * shp[axis], shp[axis], axis)
    return _xfer_start("scatter", grad, lax.empty((N_PEERS,) + tuple(shp), grad.dtype), axis, name), own, axis


def _tok(a, *tokens):
    for t in tokens:
        a = a + t[0, 0].astype(a.dtype)
    return a


def _mm(a, b, *, name, ta=False, tb=False, out_dtype=F32, a_fn=None, out_fn=None, extras=(), bias=None,
        tm=1024, tn=1024, tk=2048):
    if ta:
        kdim, m = a.shape
    else:
        m, kdim = a.shape
    if tb:
        n, kb = b.shape
    else:
        kb, n = b.shape
    assert kb == kdim, (a.shape, b.shape, ta, tb)
    tm = _pick(m, tm, LANES if ta else SUBLANES)
    tn = _pick(n, tn, LANES)
    tk = _pick(kdim, tk, LANES)
    nk = kdim // tk
    ne = len(extras)
    has_bias = bias is not None

    def body(*refs):
        a_ref, b_ref = refs[0], refs[1]
        pos = 2
        bias_ref = None
        if has_bias:
            bias_ref = refs[pos]
            pos += 1
        extra_refs = refs[pos:pos + ne]
        o_ref = refs[pos + ne]
        av = a_ref[...]
        if a_fn is not None:
            av = a_fn(av)
        dn = (((0 if ta else 1,), (1 if tb else 0,)), ((), ()))
        part = lax.dot_general(av.astype(BF16), b_ref[...].astype(BF16), dn, preferred_element_type=F32)

        def finish(r):
            if has_bias:
                r = r + bias_ref[...]
            if out_fn is not None:
                r = out_fn(r, *[e[...] for e in extra_refs])
            o_ref[...] = r.astype(out_dtype)

        if nk == 1:
            finish(part)
            return
        acc = refs[pos + ne + 1]
        k = pl.program_id(2)

        @pl.when(k == 0)
        def _():
            acc[...] = part

        @pl.when(k > 0)
        def _():
            acc[...] += part

        @pl.when(k == nk - 1)
        def _():
            finish(acc[...])

    in_specs = [
        pl.BlockSpec((tk, tm), lambda i, j, k: (k, i)) if ta else pl.BlockSpec((tm, tk), lambda i, j, k: (i, k)),
        pl.BlockSpec((tn, tk), lambda i, j, k: (j, k)) if tb else pl.BlockSpec((tk, tn), lambda i, j, k: (k, j)),
    ]
    args = [a, b]
    if has_bias:
        in_specs.append(pl.BlockSpec((1, tn), lambda i, j, k: (0, j)))
        args.append(bias)
    for e in extras:
        in_specs.append(pl.BlockSpec((tm, tn), lambda i, j, k: (i, j)))
        args.append(e)
    return pl.pallas_call(
        body, name=name, grid=(m // tm, n // tn, nk),
        in_specs=in_specs, out_specs=pl.BlockSpec((tm, tn), lambda i, j, k: (i, j)),
        out_shape=jax.ShapeDtypeStruct((m, n), out_dtype),
        scratch_shapes=[pltpu.VMEM((tm, tn), F32)] if nk > 1 else [],
        compiler_params=_params(("parallel", "parallel", "arbitrary")),
    )(*args)


def _row(arr, cb=0, rb=0, w=None):
    return ("row", arr, cb, rb, w)


def _bc(arr, cb=0, w=None):
    return ("bc", arr, cb, 0, w)


def _full(arr):
    return ("full", arr, 0, 0, None)


def _rowwise(fn, ins, outs, *, nr, tr, nc, cw, name):
    n_in = len(ins)
    outs = [(o[0], o[1], (o[2] if len(o) > 2 and o[2] else cw)) for o in outs]

    def body(*refs):
        in_refs, out_refs = refs[:n_in], refs[n_in:]
        j, i = pl.program_id(0), pl.program_id(1)
        res = fn(i, j, *[r[...] for r in in_refs])
        if not isinstance(res, (tuple, list)):
            res = (res,)
        for (kind, _, _), o_ref, r in zip(outs, out_refs, res):
            if kind == "row":
                o_ref[...] = r.astype(o_ref.dtype)
            else:
                @pl.when(i == 0)
                def _(o_ref=o_ref):
                    o_ref[...] = jnp.zeros_like(o_ref)

                o_ref[...] += r

    in_specs, args = [], []
    for kind, arr, cb, rb, w in ins:
        w = w or cw
        if kind == "row":
            in_specs.append(pl.BlockSpec((tr, w), lambda j, i, cb=cb, rb=rb: (i + rb, j + cb)))
        elif kind == "bc":
            in_specs.append(pl.BlockSpec((arr.shape[0], w), lambda j, i, cb=cb: (0, j + cb)))
        else:
            in_specs.append(pl.BlockSpec(arr.shape, lambda j, i, nd=arr.ndim: (0,) * nd))
        args.append(arr)
    out_specs, out_shape = [], []
    for kind, spec, w in outs:
        if kind == "row":
            out_specs.append(pl.BlockSpec((tr, w), lambda j, i: (i, j)))
            out_shape.append(jax.ShapeDtypeStruct((nr * tr, nc * w), spec))
        else:
            out_specs.append(pl.BlockSpec((spec, w), lambda j, i: (0, j)))
            out_shape.append(jax.ShapeDtypeStruct((spec, nc * w), F32))
    res = pl.pallas_call(
        body, name=name, grid=(nc, nr), in_specs=in_specs, out_specs=tuple(out_specs), out_shape=tuple(out_shape),
        compiler_params=_params(("parallel", "arbitrary")),
    )(*args)
    return res


def _sum_parts(own, parts, *, name):
    p, r, c = parts.shape
    tr = _pick(r, max(SUBLANES, (1 << 18) // c), SUBLANES)

    def body(o_ref, p_ref, g_out):
        g = o_ref[...].astype(F32)
        for q in range(p):
            g = g + p_ref[q].astype(F32)
        g_out[...] = g

    blk = pl.BlockSpec((tr, c), lambda i: (i, 0))
    return pl.pallas_call(
        body, name=name, grid=(r // tr,), in_specs=[blk, pl.BlockSpec((p, tr, c), lambda i: (0, i, 0))],
        out_specs=blk, out_shape=jax.ShapeDtypeStruct((r, c), F32), compiler_params=_params(("parallel",)),
    )(own, parts)


def _adamw(parts, w, m, v, *, name, own=None):
    p, r, c = parts.shape
    tr = _pick(r, max(SUBLANES, (1 << 18) // c), 16 if parts.dtype == BF16 else SUBLANES)
    bc1 = 1.0 - ADAM_B1 ** ADAM_STEP
    bc2 = 1.0 - ADAM_B2 ** ADAM_STEP
    has_own = own is not None

    def body(*refs):
        p_ref, w_ref, m_ref, v_ref = refs[:4]
        g_out, d_out, m_out, v_out = refs[4 + has_own:]
        if has_own:
            g = refs[4][...].astype(F32) + p_ref[0].astype(F32)
        else:
            g = p_ref[0].astype(F32)
        for q in range(1, p):
            g = g + p_ref[q].astype(F32)
        m_new = ADAM_B1 * m_ref[...] + (1.0 - ADAM_B1) * g
        v_new = ADAM_B2 * v_ref[...] + (1.0 - ADAM_B2) * (g * g)
        m_hat = m_new / bc1
        v_hat = v_new / bc2
        g_out[...] = g
        d_out[...] = -ADAM_LR * (m_hat / (jnp.sqrt(v_hat) + ADAM_EPS) + ADAM_WD * w_ref[...])
        m_out[...] = m_new
        v_out[...] = v_new

    blk = pl.BlockSpec((tr, c), lambda i: (i, 0))
    return pl.pallas_call(
        body, name=name, grid=(r // tr,),
        in_specs=[pl.BlockSpec((p, tr, c), lambda i: (0, i, 0)), blk, blk, blk] + ([blk] if has_own else []),
        out_specs=(blk, blk, blk, blk), out_shape=tuple(jax.ShapeDtypeStruct((r, c), F32) for _ in range(4)),
        compiler_params=_params(("parallel",)),
    )(*([parts, w, m, v] + ([own] if has_own else [])))


_GELU_C = math.sqrt(2.0 / math.pi)
_GELU_K = 0.044715


def _sigmoid(z):
    return 1.0 / (1.0 + jnp.exp(-z))


def _silu(z):
    return z * _sigmoid(z)


def _dsilu(z):
    s = _sigmoid(z)
    return s * (1.0 + z * (1.0 - s))


def _gelu(z):
    return 0.5 * z * (1.0 + jnp.tanh(_GELU_C * (z + _GELU_K * z * z * z)))


def _dgelu(z):
    t = jnp.tanh(_GELU_C * (z + _GELU_K * z * z * z))
    return 0.5 * (1.0 + t) + 0.5 * z * (1.0 - t * t) * (_GELU_C * (1.0 + 3.0 * _GELU_K * z * z))


def _softplus(z):
    return jnp.maximum(z, 0.0) + jnp.log(1.0 + jnp.exp(-jnp.abs(z)))


def _expm1(z):
    small = z * (1.0 + z * (0.5 + z * (1.0 / 6.0 + z * (1.0 / 24.0))))
    return jnp.where(jnp.abs(z) < 0.03, small, jnp.exp(z) - 1.0)


def _rms(xv):
    return lax.rsqrt(jnp.mean(xv * xv, axis=-1, keepdims=True) + NORM_EPS)


def _rms_bwd(xhat, r, dxhat):
    return r * (dxhat - xhat * jnp.mean(dxhat * xhat, axis=-1, keepdims=True))


def _swap_pairs(y):
    lane = lax.broadcasted_iota(jnp.int32, y.shape, 1)
    return jnp.where(lane % 2 == 0, pltpu.roll(y, LANES - 1, 1), pltpu.roll(y, 1, 1))


def _dot(a, b, ca, cb):
    return lax.dot_general(a, b, (((ca,), (cb,)), ((), ())), preferred_element_type=F32)


def _softmax_rows(qh, kk):
    s = _dot(qh, kk, 1, 1) * (HEAD_DIM ** -0.5)
    p = jnp.exp(s - jnp.max(s, axis=-1, keepdims=True))
    return p / jnp.sum(p, axis=-1, keepdims=True)


def _attn_fwd(q, k, v, *, tq, name):
    s_len, aw = q.shape
    t_len, kvw = k.shape
    gw = Q_GROUP * HEAD_DIM

    def body(q_ref, k_ref, v_ref, o_ref):
        kk, vv = k_ref[...], v_ref[...]
        for g in range(Q_GROUP):
            cols = slice(g * HEAD_DIM, (g + 1) * HEAD_DIM)
            p = _softmax_rows(q_ref[:, cols], kk)
            o_ref[:, cols] = _dot(p.astype(BF16), vv, 1, 0).astype(BF16)

    return pl.pallas_call(
        body, name=name, grid=(kvw // HEAD_DIM, s_len // tq),
        in_specs=[pl.BlockSpec((tq, gw), lambda h, i: (i, h)), pl.BlockSpec((t_len, HEAD_DIM), lambda h, i: (0, h)),
                  pl.BlockSpec((t_len, HEAD_DIM), lambda h, i: (0, h))],
        out_specs=pl.BlockSpec((tq, gw), lambda h, i: (i, h)),
        out_shape=jax.ShapeDtypeStruct((s_len, aw), BF16),
        compiler_params=_params(("parallel", "parallel")),
    )(q, k, v)


def _attn_bwd(q, k, v, do, *, tq, name):
    s_len, aw = q.shape
    t_len, kvw = k.shape
    gw = Q_GROUP * HEAD_DIM

    def body(q_ref, k_ref, v_ref, do_ref, dq_ref, dk_ref, dv_ref):
        @pl.when(pl.program_id(1) == 0)
        def _():
            dk_ref[...] = jnp.zeros_like(dk_ref)
            dv_ref[...] = jnp.zeros_like(dv_ref)

        kk, vv = k_ref[...], v_ref[...]
        for g in range(Q_GROUP):
            cols = slice(g * HEAD_DIM, (g + 1) * HEAD_DIM)
            qh, doh = q_ref[:, cols], do_ref[:, cols]
            p = _softmax_rows(qh, kk)
            dp = _dot(doh, vv, 1, 1)
            ds = p * (dp - jnp.sum(p * dp, axis=-1, keepdims=True)) * (HEAD_DIM ** -0.5)
            dsb = ds.astype(BF16)
            dq_ref[:, cols] = _dot(dsb, kk, 1, 0)
            dk_ref[...] += _dot(dsb, qh, 0, 0)
            dv_ref[...] += _dot(p.astype(BF16), doh, 0, 0)

    qspec = pl.BlockSpec((tq, gw), lambda h, i: (i, h))
    kspec = pl.BlockSpec((t_len, HEAD_DIM), lambda h, i: (0, h))
    return pl.pallas_call(
        body, name=name, grid=(kvw // HEAD_DIM, s_len // tq),
        in_specs=[qspec, kspec, kspec, qspec], out_specs=(qspec, kspec, kspec),
        out_shape=(jax.ShapeDtypeStruct((s_len, aw), F32), jax.ShapeDtypeStruct((t_len, kvw), F32),
                   jax.ShapeDtypeStruct((t_len, kvw), F32)),
        compiler_params=_params(("parallel", "arbitrary")),
    )(q, k, v, do)


def _seg_ok(t, d, l_len, t_len):
    td = t + d
    return ((t < l_len) & (td >= 0) & (td < l_len)) | ((t >= l_len) & (td >= l_len) & (td < t_len))


def _shift_rows(xv, d, t, l_len, t_len):
    if d == 0:
        return xv
    return jnp.where(_seg_ok(t, d, l_len, t_len), pltpu.roll(xv, (-d) % t_len, 0), 0.0)


def _scan_chunks(chains, c0, n, carries):
    def step(kk, hs):
        new = []
        for (a_ref, b_ref, h_ref, rev), hprev in zip(chains, hs):
            c = c0 + (n - 1 - kk if rev else kk)
            r0 = pl.multiple_of(c * SUBLANES, SUBLANES)
            a8 = a_ref[pl.ds(r0, SUBLANES), :]
            b8 = b_ref[pl.ds(r0, SUBLANES), :]
            row = lax.broadcasted_iota(jnp.int32, a8.shape, 0)
            for s in (1, 2, 4):
                if rev:
                    ok = row < SUBLANES - s
                    sh = SUBLANES - s
                else:
                    ok = row >= s
                    sh = s
                a_sh = jnp.where(ok, pltpu.roll(a8, sh, 0), 1.0)
                b_sh = jnp.where(ok, pltpu.roll(b8, sh, 0), 0.0)
                b8 = a8 * b_sh + b8
                a8 = a8 * a_sh
            h8 = b8 + a8 * hprev
            h_ref[pl.ds(r0, SUBLANES), :] = h8
            edge = 0 if rev else SUBLANES - 1
            last = jnp.sum(jnp.where(row == edge, h8, 0.0), axis=0, keepdims=True)
            new.append(jnp.broadcast_to(last, h8.shape))
        return tuple(new)

    return lax.fori_loop(0, n, step, tuple(carries))


def _rglru_conv(x, cw_ref, cb_ref, t, l_len, t_len):
    y = cb_ref[...] + _shift_rows(x, -1, t, l_len, t_len) * cw_ref[0:1, :]
    for kk in range(1, CONV_WIDTH):
        y = y + _shift_rows(x, kk - 1, t, l_len, t_len) * cw_ref[kk:kk + 1, :]
    return y


def _rglru_gates(xc, wr, wi, br, bi, lam):
    xb = xc.astype(BF16)
    r = _sigmoid(_dot(xb, wr.astype(BF16), 1, 0) + br)
    ig = _sigmoid(_dot(xb, wi.astype(BF16), 1, 0) + bi)
    sp = _softplus(-lam)
    log_a = (-LRU_C) * r * sp
    a = jnp.exp(log_a)
    mult = jnp.sqrt(-_expm1(2.0 * log_a))
    u = ig * xc
    return r, ig, sp, a, mult, u


def _rglru_specs(proj, xr_blk, xg_blk, t_len):
    col = lambda off: pl.BlockSpec((t_len, RNN_BLOCK), lambda h, off=off: (0, off + h))
    vec = lambda rows: pl.BlockSpec((rows, RNN_BLOCK), lambda h: (0, h))
    wsp = pl.BlockSpec((2, 1, RNN_BLOCK, RNN_BLOCK), lambda h: (0, h, 0, 0))
    return col(xr_blk), col(xg_blk), vec, wsp


def _rglru_fwd(proj, xr_blk, xg_blk, conv_w, conv_b, w_rg, w_ig, b_rg, b_ig, lam, *, l_len, name):
    t_len = proj.shape[0]
    s_len = t_len - l_len
    d_rnn = conv_b.shape[1]
    n_l, n_t = l_len // SUBLANES, t_len // SUBLANES

    def body(xr_ref, xg_ref, cw_ref, cb_ref, wr_ref, wi_ref, br_ref, bi_ref, lam_ref, rg_ref, a_s, b_s, h_s):
        t = lax.broadcasted_iota(jnp.int32, (t_len, RNN_BLOCK), 0)
        xc = _rglru_conv(xr_ref[...], cw_ref, cb_ref, t, l_len, t_len)
        for d in range(2):
            _, _, _, a, mult, u = _rglru_gates(xc, wr_ref[d, 0], wi_ref[d, 0], br_ref[d:d + 1, :],
                                               bi_ref[d:d + 1, :], lam_ref[d:d + 1, :])
            a_s[d] = a
            b_s[d] = mult * u
        chains = [(a_s.at[0], b_s.at[0], h_s.at[0], False), (a_s.at[1], b_s.at[1], h_s.at[1], True)]
        zero = jnp.zeros((SUBLANES, RNN_BLOCK), F32)
        carry = _scan_chunks(chains, 0, n_l, (zero, zero))
        _scan_chunks(chains, n_l, n_t - n_l, carry)
        rnn = h_s[0, pl.ds(l_len, s_len), :] + h_s[1, pl.ds(l_len, s_len), :]
        rg_ref[...] = (rnn * _gelu(xg_ref[pl.ds(l_len, s_len), :])).astype(BF16)

    xr_spec, xg_spec, vec, wsp = _rglru_specs(proj, xr_blk, xg_blk, t_len)
    return pl.pallas_call(
        body, name=name, grid=(d_rnn // RNN_BLOCK,),
        in_specs=[xr_spec, xg_spec, vec(CONV_WIDTH), vec(1), wsp, wsp, vec(2), vec(2), vec(2)],
        out_specs=pl.BlockSpec((s_len, RNN_BLOCK), lambda h: (0, h)),
        out_shape=jax.ShapeDtypeStruct((s_len, d_rnn), BF16),
        scratch_shapes=[pltpu.VMEM((2, t_len, RNN_BLOCK), F32)] * 3,
        compiler_params=_params(("parallel",)),
    )(proj, proj, conv_w, conv_b, w_rg, w_ig, b_rg, b_ig, lam)


N_RNN_VEC = 16


def _rglru_bwd(proj, xr_blk, xg_blk, drg, conv_w, conv_b, w_rg, w_ig, b_rg, b_ig, lam, *, l_len, name):
    t_len = proj.shape[0]
    s_len = t_len - l_len
    d_rnn = conv_b.shape[1]
    n_l, n_t = l_len // SUBLANES, t_len // SUBLANES

    def body(xr_ref, xg_ref, drg_ref, cw_ref, cb_ref, wr_ref, wi_ref, br_ref, bi_ref, lam_ref,
             dxr_ref, dxg_ref, dvec_ref, dwr_ref, dwi_ref, a_s, b_s, h_s):
        t = lax.broadcasted_iota(jnp.int32, (t_len, RNN_BLOCK), 0)
        x = xr_ref[...]
        xc = _rglru_conv(x, cw_ref, cb_ref, t, l_len, t_len)
        xb = xc.astype(BF16)
        gates = []
        for d in range(2):
            gt = _rglru_gates(xc, wr_ref[d, 0], wi_ref[d, 0], br_ref[d:d + 1, :], bi_ref[d:d + 1, :],
                              lam_ref[d:d + 1, :])
            gates.append(gt)
            a_s[d] = gt[3]
            b_s[d] = gt[4] * gt[5]
        zero = jnp.zeros((SUBLANES, RNN_BLOCK), F32)
        chains = [(a_s.at[0], b_s.at[0], h_s.at[0], False), (a_s.at[1], b_s.at[1], h_s.at[1], True)]
        carry = _scan_chunks(chains, 0, n_l, (zero, zero))
        _scan_chunks(chains, n_l, n_t - n_l, carry)

        h0, h1 = h_s[0], h_s[1]
        xg = xg_ref[...]
        dr = jnp.concatenate([jnp.zeros((l_len, RNN_BLOCK), F32), drg_ref[...].astype(F32)], axis=0)
        dxg_ref[...] = (dr * (h0 + h1) * _dgelu(xg)).astype(dxg_ref.dtype)
        drnn = dr * _gelu(xg)
        a_s[0] = pltpu.roll(gates[0][3], t_len - 1, 0)
        a_s[1] = pltpu.roll(gates[1][3], 1, 0)
        b_s[0] = drnn
        b_s[1] = drnn
        hprev = [jnp.where(t == 0, 0.0, pltpu.roll(h0, 1, 0)),
                 jnp.where(t == l_len - 1, 0.0, pltpu.roll(h1, t_len - 1, 0))]
        chains = [(a_s.at[0], b_s.at[0], h_s.at[0], True), (a_s.at[1], b_s.at[1], h_s.at[1], False)]
        carry = _scan_chunks(chains, n_l, n_t - n_l, (zero, zero))
        _scan_chunks(chains, 0, n_l, carry)

        dxc = jnp.zeros((t_len, RNN_BLOCK), F32)
        for d in range(2):
            r, ig, sp, a, mult, u = gates[d]
            lamb = h_s[d]
            da = lamb * hprev[d]
            dmult = lamb * u
            du = lamb * mult
            dlog_a = da * a - dmult * (a * a) / mult
            dzr = (dlog_a * ((-LRU_C) * sp)) * r * (1.0 - r)
            dsp = jnp.sum(dlog_a * ((-LRU_C) * r), axis=0, keepdims=True)
            dzi = (du * xc) * ig * (1.0 - ig)
            dzrb, dzib = dzr.astype(BF16), dzi.astype(BF16)
            dxc = dxc + du * ig + _dot(dzrb, wr_ref[d, 0].astype(BF16), 1, 1) + _dot(dzib, wi_ref[d, 0].astype(BF16), 1, 1)
            dwr_ref[d, 0] = _dot(xb, dzrb, 0, 0)
            dwi_ref[d, 0] = _dot(xb, dzib, 0, 0)
            dvec_ref[5 + d:6 + d, :] = jnp.sum(dzr, axis=0, keepdims=True)
            dvec_ref[7 + d:8 + d, :] = jnp.sum(dzi, axis=0, keepdims=True)
            dvec_ref[9 + d:10 + d, :] = dsp * (-_sigmoid(-lam_ref[d:d + 1, :]))
        dvec_ref[4:5, :] = jnp.sum(dxc, axis=0, keepdims=True)
        dvec_ref[11:N_RNN_VEC, :] = jnp.zeros((N_RNN_VEC - 11, RNN_BLOCK), F32)
        dxr = jnp.zeros((t_len, RNN_BLOCK), F32)
        for kk in range(CONV_WIDTH):
            dvec_ref[kk:kk + 1, :] = jnp.sum(dxc * _shift_rows(x, kk - 1, t, l_len, t_len), axis=0, keepdims=True)
            dxr = dxr + _shift_rows(dxc, 1 - kk, t, l_len, t_len) * cw_ref[kk:kk + 1, :]
        dxr_ref[...] = dxr.astype(dxr_ref.dtype)

    xr_spec, xg_spec, vec, wsp = _rglru_specs(proj, xr_blk, xg_blk, t_len)
    col_t = pl.BlockSpec((t_len, RNN_BLOCK), lambda h: (0, h))
    return pl.pallas_call(
        body, name=name, grid=(d_rnn // RNN_BLOCK,),
        in_specs=[xr_spec, xg_spec, pl.BlockSpec((s_len, RNN_BLOCK), lambda h: (0, h)), vec(CONV_WIDTH), vec(1),
                  wsp, wsp, vec(2), vec(2), vec(2)],
        out_specs=(col_t, col_t, vec(N_RNN_VEC), wsp, wsp),
        out_shape=(jax.ShapeDtypeStruct((t_len, d_rnn), BF16), jax.ShapeDtypeStruct((t_len, d_rnn), BF16),
                   jax.ShapeDtypeStruct((N_RNN_VEC, d_rnn), F32), jax.ShapeDtypeStruct(w_rg.shape, F32),
                   jax.ShapeDtypeStruct(w_ig.shape, F32)),
        scratch_shapes=[pltpu.VMEM((2, t_len, RNN_BLOCK), F32)] * 3,
        compiler_params=_params(("parallel",)),
    )(proj, proj, drg, conv_w, conv_b, w_rg, w_ig, b_rg, b_ig, lam)


def _colsum(v):
    return jnp.sum(v, axis=0, keepdims=True)


def _qk_fwd_heads(xv, gain, cos_e, sin_e):
    outs = []
    for h in range(xv.shape[1] // HEAD_DIM):
        xh = xv[:, h * HEAD_DIM:(h + 1) * HEAD_DIM]
        y = xh * _rms(xh) * gain
        outs.append(y * cos_e + _swap_pairs(y) * sin_e)
    return outs[0] if len(outs) == 1 else jnp.concatenate(outs, axis=1)


def _qk_bwd_heads(xv, dov, gain, cos_e, sin_e):
    outs = []
    dgain = jnp.zeros((1, HEAD_DIM), F32)
    for h in range(xv.shape[1] // HEAD_DIM):
        cols = slice(h * HEAD_DIM, (h + 1) * HEAD_DIM)
        xh, doh = xv[:, cols], dov[:, cols]
        r = _rms(xh)
        xhat = xh * r
        dy = doh * cos_e + _swap_pairs(doh * sin_e)
        dgain = dgain + _colsum(dy * xhat)
        outs.append(_rms_bwd(xhat, r, dy * gain))
    return (outs[0] if len(outs) == 1 else jnp.concatenate(outs, axis=1)), dgain


WEIGHTS = ['c_ctx', 'w_mod', 'b_mod', 'g_mix', 'g_mlp', 'w_in', 'q_gain', 'k_gain', 'conv_w', 'conv_b', 'w_rg', 'b_rg',
           'w_ig', 'b_ig', 'lru_lambda', 'w_o_attn', 'w_o_rnn', 'w_out', 'w_up', 'w_down', 'g_final']
REPLICATED = ['c_ctx', 'b_mod', 'g_mix', 'g_mlp', 'q_gain', 'k_gain', 'conv_b', 'w_rg', 'w_ig', 'g_final']
SMALL_SHARDED = [('conv_w', CONV_WIDTH), ('b_rg', 2), ('b_ig', 2), ('lru_lambda', 2)]
SMALL_ROWS = 16


PACK_ROWS = 512


def _pack_flat(arrs, rows_mult):
    flat = jnp.concatenate([a.reshape(-1) for a in arrs])
    n = flat.shape[0]
    unit = rows_mult * LANES
    pad = (-n) % unit
    if pad:
        flat = jnp.concatenate([flat, jnp.zeros((pad,), flat.dtype)])
    return flat.reshape(-1, LANES)


def _unpack_flat(packed, shapes):
    flat = packed.reshape(-1)
    out, pos = [], 0
    for shp in shapes:
        n = math.prod(shp)
        out.append(flat[pos:pos + n].reshape(shp))
        pos += n
    return out


def _pack_small(d):
    rows = [d[n][0] for n, _ in SMALL_SHARDED]
    cols = rows[0].shape[1]
    used = sum(k for _, k in SMALL_SHARDED)
    return jnp.concatenate(rows + [jnp.zeros((SMALL_ROWS - used, cols), F32)], axis=0)


def _unpack_small(p):
    out, pos = {}, 0
    for n, k in SMALL_SHARDED:
        out[n] = p[pos:pos + k][None]
        pos += k
    return out


def kernel(x, c, ctx, c_ctx, w_mod, b_mod, g_mix, g_mlp, w_in, q_gain, k_gain, conv_w, conv_b, w_rg, b_rg, w_ig, b_ig, lru_lambda, w_o_attn, w_o_rnn, w_out, w_up, w_down, g_final, loss_target, m_c_ctx, m_w_mod, m_b_mod, m_g_mix, m_g_mlp, m_w_in, m_q_gain, m_k_gain, m_conv_w, m_conv_b, m_w_rg, m_b_rg, m_w_ig, m_b_ig, m_lru_lambda, m_w_o_attn, m_w_o_rnn, m_w_out, m_w_up, m_w_down, m_g_final, v_c_ctx, v_w_mod, v_b_mod, v_g_mix, v_g_mlp, v_w_in, v_q_gain, v_k_gain, v_conv_w, v_conv_b, v_w_rg, v_b_rg, v_w_ig, v_b_ig, v_lru_lambda, v_w_o_attn, v_w_o_rnn, v_w_out, v_w_up, v_w_down, v_g_final):
    wd = dict(c_ctx=c_ctx, w_mod=w_mod, b_mod=b_mod, g_mix=g_mix, g_mlp=g_mlp, w_in=w_in, q_gain=q_gain, k_gain=k_gain,
              conv_w=conv_w, conv_b=conv_b, w_rg=w_rg, b_rg=b_rg, w_ig=w_ig, b_ig=b_ig, lru_lambda=lru_lambda,
              w_o_attn=w_o_attn, w_o_rnn=w_o_rnn, w_out=w_out, w_up=w_up, w_down=w_down, g_final=g_final)
    md = dict(zip(WEIGHTS, (m_c_ctx, m_w_mod, m_b_mod, m_g_mix, m_g_mlp, m_w_in, m_q_gain, m_k_gain, m_conv_w, m_conv_b,
                            m_w_rg, m_b_rg, m_w_ig, m_b_ig, m_lru_lambda, m_w_o_attn, m_w_o_rnn, m_w_out, m_w_up,
                            m_w_down, m_g_final)))
    vd = dict(zip(WEIGHTS, (v_c_ctx, v_w_mod, v_b_mod, v_g_mix, v_g_mlp, v_w_in, v_q_gain, v_k_gain, v_conv_w, v_conv_b,
                            v_w_rg, v_b_rg, v_w_ig, v_b_ig, v_lru_lambda, v_w_o_attn, v_w_o_rnn, v_w_out, v_w_up,
                            v_w_down, v_g_final)))

    xs, tgt, ctxs = x[0], loss_target[0], ctx[0]
    s_len, d = xs.shape
    l_len = ctxs.shape[0]
    t_len = s_len + l_len
    n_in = w_in.shape[2] * N_DEV
    aw = (n_in - 4 * d) * 2 // 3
    kvw = aw // Q_GROUP
    assert aw == d and n_in == aw + 2 * kvw + 4 * d
    nmod = N_MOD * d // N_DEV
    tr = 128
    assert l_len % tr == 0 and s_len % tr == 0
    lb = l_len // tr
    tq = min(256, s_len)
    my = _lin(_my_pos())
    c_ctx2 = c_ctx[None]
    g_final2 = g_final[None]

    c_all = _all_gather([jnp.broadcast_to(c, (SUBLANES, d))], [0], "ag_cond")[0][::SUBLANES]
    c16 = jnp.concatenate([c_all, jnp.broadcast_to(c_ctx2, (N_DEV, d))], axis=0)
    b_mod_sh = lax.dynamic_slice(b_mod, (0, my * nmod), (1, nmod))
    mod_part = _mm(c16, w_mod[0], a_fn=_silu, bias=b_mod_sh, name="mod_fwd")
    (mod_g,), order = _all_gather([mod_part], [1], "ag_mod", with_token=True)
    mod_lat = lax.dynamic_slice(mod_g, (my, 0), (1, N_MOD * d))
    sh_a, sc_a, ga_a, sh_f, sc_f, ga_f = [mod_lat[:, k * d:(k + 1) * d] for k in range(N_MOD)]
    csh_a, csc_a = mod_g[N_DEV:N_DEV + 1, :d], mod_g[N_DEV:N_DEV + 1, d:2 * d]

    sp_w = _pack_small(wd)
    (w_in_f, sp_f), order = _all_gather([_tok(w_in[0], order).astype(BF16), sp_w], [1, 1], "ag_w_in", with_token=True)
    conv_w_f, b_rg_f, b_ig_f, lam_f = sp_f[0:4], sp_f[4:6], sp_f[6:8], sp_f[8:10]
    late_axis = dict(w_o_attn=0, w_o_rnn=0, w_out=0, w_up=1, w_down=0)
    late = {}
    for n, ax in late_axis.items():
        late[n] = _gather_start(_tok(wd[n][0], order).astype(BF16), ax, my, "ag_start_" + n)
        order = late[n][4]
    g_mix_fwd = _tok(g_mix, order)
    late_wait = lambda n, after: _xfer_wait("gather", late[n], late_axis[n], after, "ag_wait_" + n)

    pos = jnp.arange(s_len)
    n_freq = HEAD_DIM // 4
    inv_freq = ROPE_THETA ** (-jnp.arange(n_freq, dtype=F32) / n_freq)
    ang = jnp.concatenate([(pos // GRID_W).astype(F32)[:, None] * inv_freq,
                           (pos % GRID_W).astype(F32)[:, None] * inv_freq], axis=-1)
    sign = jnp.where(jnp.arange(HEAD_DIM) % 2 == 0, -1.0, 1.0).astype(F32)
    cos_e = jnp.concatenate([jnp.ones((l_len, HEAD_DIM), F32), jnp.repeat(jnp.cos(ang), 2, axis=-1)], axis=0)
    sin_e = jnp.concatenate([jnp.zeros((l_len, HEAD_DIM), F32), jnp.repeat(jnp.sin(ang), 2, axis=-1) * sign], axis=0)

    x_all = jnp.concatenate([ctxs, xs], axis=0)

    def nm1_fwd(i, j, xv, g, csh, csc, sh, sc):
        is_ctx = i < lb
        n = xv * _rms(xv) * g
        return n * (1.0 + jnp.where(is_ctx, csc, sc)) + jnp.where(is_ctx, csh, sh)

    h_all = _rowwise(nm1_fwd, [_row(x_all), _bc(g_mix_fwd), _bc(csh_a), _bc(csc_a), _bc(sh_a), _bc(sc_a)],
                     [("row", BF16)], nr=t_len // tr, tr=tr, nc=1, cw=d, name="norm1_fwd")[0]
    proj = _mm(h_all, w_in_f, name="proj_fwd")

    kb = aw // kvw
    q = _rowwise(lambda i, j, xv, gain, ce, se: _qk_fwd_heads(xv, gain, ce, se),
                 [_row(proj, 0, lb, aw), _full(q_gain), _row(cos_e, 0, lb, HEAD_DIM), _row(sin_e, 0, lb, HEAD_DIM)],
                 [("row", BF16, aw)], nr=s_len // tr, tr=tr, nc=1, cw=aw, name="q_fwd")[0]
    k_all, v_all = _rowwise(lambda i, j, kv_, vv_, gain, ce, se: (_qk_fwd_heads(kv_, gain, ce, se), vv_),
                            [_row(proj, kb, 0, kvw), _row(proj, kb + 1, 0, kvw), _full(k_gain),
                             _row(cos_e, 0, 0, HEAD_DIM), _row(sin_e, 0, 0, HEAD_DIM)],
                            [("row", BF16, kvw), ("row", BF16, kvw)], nr=t_len // tr, tr=tr, nc=1, cw=kvw, name="kv_fwd")
    attn_o = _attn_fwd(q, k_all, v_all, tq=tq, name="attn_fwd")

    xr_blk = (aw + 2 * kvw) // RNN_BLOCK
    xg_blk = xr_blk + d // RNN_BLOCK
    rg = _rglru_fwd(proj, xr_blk, xg_blk, conv_w_f, conv_b, w_rg[0], w_ig[0], b_rg_f, b_ig_f, lam_f, l_len=l_len,
                    name="rglru_fwd")

    w_oa_f = late_wait("w_o_attn", attn_o)
    y_attn = _mm(attn_o, w_oa_f, name="o_attn_fwd")
    w_or_f = late_wait("w_o_rnn", rg)
    y_rnn = _mm(rg, w_or_f, name="o_rnn_fwd")
    hw = d // 2
    glb = (aw + 2 * kvw + 2 * d) // hw

    def merge_fwd(i, j, ya, yr, gla, glr):
        return _sigmoid(gla) * ya + _sigmoid(glr) * yr

    u = _rowwise(merge_fwd, [_row(y_attn), _row(y_rnn), _row(proj, glb, lb), _row(proj, glb + 2, lb)],
                 [("row", BF16)], nr=s_len // tr, tr=tr, nc=2, cw=hw, name="merge_fwd")[0]
    w_out_f = late_wait("w_out", u)
    mix = _mm(u, w_out_f, name="out_fwd")

    def res1_fwd(i, j, xv, mixv, gaa, gm, shf, scf):
        x1v = xv + gaa * mixv
        n = x1v * _rms(x1v) * gm
        return x1v, n * (1.0 + scf) + shf

    x1, h2 = _rowwise(res1_fwd, [_row(xs), _row(mix), _bc(ga_a), _bc(g_mlp), _bc(sh_f), _bc(sc_f)],
                      [("row", F32), ("row", BF16)], nr=s_len // tr, tr=tr, nc=1, cw=d, name="norm2_fwd")
    relu2 = lambda z: jnp.square(jnp.maximum(z, 0.0))
    w_up_f = late_wait("w_up", h2)
    up = _mm(h2, w_up_f, name="up_fwd")
    w_down_f = late_wait("w_down", up)
    down = _mm(up, w_down_f, a_fn=relu2, name="down_fwd")

    def final_fn(i, j, x1v, dn, tg, gaf, gf):
        x2 = x1v + gaf * dn
        r = _rms(x2)
        xh = x2 * r
        e = xh * gf - tg
        dy = e * (1.0 / d)
        dx2 = _rms_bwd(xh, r, dy * gf)
        return dx2, dx2 * gaf, _colsum(e * e) * (0.5 / d), _colsum(dy * xh), _colsum(dx2 * dn)

    dx2, ddown, loss_cols, g_gfinal, d_ga_f = _rowwise(
        final_fn, [_row(x1), _row(down), _row(tgt), _bc(ga_f), _bc(g_final2)],
        [("row", F32), ("row", BF16), ("acc", 1), ("acc", 1), ("acc", 1)], nr=s_len // tr, tr=tr, nc=1, cw=d,
        name="final_bwd")
    loss = lax.psum(jnp.sum(loss_cols), ("x", "y", "c"))

    dup = _mm(ddown, w_down_f, tb=True, out_fn=lambda r, upv: r * (2.0 * jnp.maximum(upv, 0.0)), extras=[up],
              out_dtype=BF16, name="down_bwd_x")
    gw_down = _mm(up, ddown, ta=True, a_fn=relu2, out_dtype=BF16, name="down_bwd_w")
    sent = dict(w_down=_scatter_start(gw_down, 0, my, "rs_start_w_down"))
    dh2 = _mm(dup, w_up_f, tb=True, name="up_bwd_x")
    gw_up = _mm(h2, dup, ta=True, out_dtype=BF16, name="up_bwd_w")
    sent["w_up"] = _scatter_start(gw_up, 1, my, "rs_start_w_up")
    sc_f_bwd = _tok(sc_f, sent["w_down"][0][4], sent["w_up"][0][4])

    def norm2_bwd(i, j, x1v, mixv, dh2v, dx2v, gaa, gm, scf):
        r = _rms(x1v)
        xh = x1v * r
        n = xh * gm
        dn = dh2v * (1.0 + scf)
        dx1 = dx2v + _rms_bwd(xh, r, dn * gm)
        return dx1, dx1 * gaa, _colsum(dh2v), _colsum(dh2v * n), _colsum(dn * xh), _colsum(dx1 * mixv)

    dx1, dmix, d_sh_f, d_sc_f, g_gmlp, d_ga_a = _rowwise(
        norm2_bwd, [_row(x1), _row(mix), _row(dh2), _row(dx2), _bc(ga_a), _bc(g_mlp), _bc(sc_f_bwd)],
        [("row", F32), ("row", BF16), ("acc", 1), ("acc", 1), ("acc", 1), ("acc", 1)], nr=s_len // tr, tr=tr, nc=1,
        cw=d, name="norm2_bwd")

    du = _mm(dmix, w_out_f, tb=True, name="out_bwd_x")
    gw_out = _mm(u, dmix, ta=True, out_dtype=BF16, name="out_bwd_w")
    sent["w_out"] = _scatter_start(gw_out, 0, my, "rs_start_w_out")

    def merge_bwd(i, j, duv, ya, yr, gla, glr):
        ga, gr = _sigmoid(gla), _sigmoid(glr)
        return duv * ga, duv * gr, duv * ya * ga * (1.0 - ga), duv * yr * gr * (1.0 - gr)

    dya, dyr, dgla, dglr = _rowwise(
        merge_bwd, [_row(du), _row(y_attn), _row(y_rnn), _row(proj, glb, lb), _row(proj, glb + 2, lb)],
        [("row", BF16)] * 4, nr=s_len // tr, tr=tr, nc=2, cw=hw, name="merge_bwd")
    dattn = _mm(dya, w_oa_f, tb=True, out_dtype=BF16, name="o_attn_bwd_x")
    gw_oa = _mm(attn_o, dya, ta=True, out_dtype=BF16, name="o_attn_bwd_w")
    drg = _mm(dyr, w_or_f, tb=True, out_dtype=BF16, name="o_rnn_bwd_x")
    gw_or = _mm(rg, dyr, ta=True, out_dtype=BF16, name="o_rnn_bwd_w")
    sent["w_o_attn"] = _scatter_start(gw_oa, 0, my, "rs_start_w_o_attn")
    sent["w_o_rnn"] = _scatter_start(gw_or, 0, my, "rs_start_w_o_rnn")
    conv_b_bwd = _tok(conv_b, sent["w_out"][0][4], sent["w_o_attn"][0][4], sent["w_o_rnn"][0][4])

    dq, dk, dv = _attn_bwd(q, k_all, v_all, dattn, tq=tq, name="attn_bwd")
    dxr, dxg, dvec, g_wrg, g_wig = _rglru_bwd(proj, xr_blk, xg_blk, drg, conv_w_f, conv_b_bwd, w_rg[0], w_ig[0], b_rg_f,
                                               b_ig_f, lam_f, l_len=l_len, name="rglru_bwd")
    gate_part = _pack_flat([g_wrg, g_wig], N_DEV * SUBLANES)
    gate_sent = _scatter_start(gate_part, 0, my, "rs_start_gates")
    q_gain_bwd = _tok(q_gain, gate_sent[0][4])

    dq_p, g_qgain = _rowwise(lambda i, j, xv, dov, gain, ce, se: _qk_bwd_heads(xv, dov, gain, ce, se),
                             [_row(proj, 0, lb, aw), _row(dq, 0, 0, aw), _full(q_gain_bwd), _row(cos_e, 0, lb, HEAD_DIM),
                              _row(sin_e, 0, lb, HEAD_DIM)],
                             [("row", BF16, aw), ("acc", 1, HEAD_DIM)], nr=s_len // tr, tr=tr, nc=1, cw=aw, name="q_bwd")

    def kv_bwd(i, j, xv, dkv, dvv, gain, ce, se):
        dk_p, dgain = _qk_bwd_heads(xv, dkv, gain, ce, se)
        return dk_p, dvv, dgain

    dk_p, dv_p, g_kgain = _rowwise(kv_bwd, [_row(proj, kb, 0, kvw), _row(dk, 0, 0, kvw), _row(dv, 0, 0, kvw),
                                            _full(k_gain), _row(cos_e, 0, 0, HEAD_DIM), _row(sin_e, 0, 0, HEAD_DIM)],
                                   [("row", BF16, kvw), ("row", BF16, kvw), ("acc", 1, HEAD_DIM)], nr=t_len // tr, tr=tr,
                                   nc=1, cw=kvw, name="kv_bwd")

    pad_ctx = lambda a: jnp.pad(a, ((l_len, 0), (0, 0)))
    dproj = jnp.concatenate([pad_ctx(dq_p), dk_p, dv_p, dxr, dxg, pad_ctx(dgla), pad_ctx(dglr)], axis=1)
    dh_all = _mm(dproj, w_in_f, tb=True, name="proj_bwd_x")
    g_mix_bwd = g_mix

    def norm1_bwd(i, j, xv, dhv, dresv, g, csc, sc):
        is_ctx = i < lb
        r = _rms(xv)
        xh = xv * r
        dn = dhv * (1.0 + jnp.where(is_ctx, csc, sc))
        dsh, dsc = _colsum(dhv), _colsum(dhv * (xh * g))
        zero = jnp.zeros_like(dsh)
        return (dresv + _rms_bwd(xh, r, dn * g), jnp.where(is_ctx, dsh, zero), jnp.where(is_ctx, dsc, zero),
                jnp.where(is_ctx, zero, dsh), jnp.where(is_ctx, zero, dsc), _colsum(dn * xh))

    dx_all, d_csh, d_csc, d_sh_a, d_sc_a, g_gmix = _rowwise(
        norm1_bwd, [_row(x_all), _row(dh_all), _row(pad_ctx(dx1)), _bc(g_mix_bwd), _bc(csc_a), _bc(sc_a)],
        [("row", F32)] + [("acc", 1)] * 5, nr=t_len // tr, tr=tr, nc=1, cw=d, name="norm1_bwd")
    grad_x = dx_all[l_len:][None]

    dmod_lat = jnp.concatenate([d_sh_a, d_sc_a, d_ga_a, d_sh_f, d_sc_f, d_ga_f], axis=1)
    dmod_ctx = jnp.concatenate([d_csh, d_csc, jnp.zeros((1, (N_MOD - 2) * d), F32)], axis=1)
    dm_g = _all_gather([jnp.concatenate([dmod_lat, dmod_ctx, jnp.zeros((SUBLANES - 2, N_MOD * d), F32)], axis=0)], [0],
                       "ag_dmod")[0]
    dm16 = jnp.concatenate([dm_g[0::SUBLANES], dm_g[1::SUBLANES]], axis=0)
    dm_sh = lax.dynamic_slice(dm16, (0, my * nmod), (2 * N_DEV, nmod))
    gw_mod = _mm(c16, dm_sh, ta=True, a_fn=_silu, name="mod_bwd_w")
    dc16 = _mm(dm_sh, w_mod[0], tb=True, name="mod_bwd_x")

    def cctx_fn(i, j, pv, cc, dl, dc):
        row = lax.broadcasted_iota(jnp.int32, pv.shape, 0)
        return _colsum(jnp.where(row >= N_DEV, pv, 0.0)) * _dsilu(cc), dl + dc

    g_cctx, g_bmod = _rowwise(cctx_fn, [_full(dc16), _full(c_ctx2), _full(dmod_lat), _full(dmod_ctx)],
                              [("row", F32, d), ("row", F32, N_MOD * d)], nr=1, tr=1, nc=1, cw=d, name="cctx_bwd")

    rep_part = dict(c_ctx=g_cctx, b_mod=g_bmod, g_mix=g_gmix, g_mlp=g_gmlp, q_gain=g_qgain, k_gain=g_kgain,
                    conv_b=dvec[4:5], g_final=g_gfinal)
    rep_small = [n for n in REPLICATED if n in rep_part]
    gates = [n for n in REPLICATED if n not in rep_part]
    part_pack = _pack_flat([rep_part[n] for n in rep_small], PACK_ROWS)
    rep_rows = part_pack.shape[0]
    gate_land = _xfer_wait("scatter", gate_sent[0], 0, g_bmod, "rs_wait_gates")
    gate_sum = _sum_parts(gate_sent[1], gate_land, name="gate_sum")
    (gate_full, parts), order = _all_gather([gate_sum, part_pack], [0, 0], "ag_rep_grads", with_token=True)
    small_g = jnp.concatenate([dvec[0:4], dvec[5:11], jnp.zeros((SMALL_ROWS - 10, d), F32)], axis=0)
    (small_recv,), order = _all_to_all([_tok(small_g, order)], [1], "a2a_small_grads")

    gw_in = _mm(h_all, dproj, ta=True, out_dtype=BF16, name="proj_bwd_w", bias=_tok(jnp.zeros((1, n_in), F32), order))
    sent["w_in"] = _scatter_start(gw_in, 1, my, "rs_start_w_in")
    started = sent["w_in"][0][4]

    res = {}
    for names, pr, mult, nm in ((rep_small, parts.reshape(N_DEV, rep_rows, LANES), PACK_ROWS, "adamw_rep"),
                                (gates, gate_full[None], N_DEV * SUBLANES, "adamw_gates")):
        rep_out = _adamw(pr, _tok(_pack_flat([wd[n] for n in names], mult), started),
                         _pack_flat([md[n] for n in names], mult), _pack_flat([vd[n] for n in names], mult), name=nm)
        for kind, packed in zip(("g", "d", "m", "v"), rep_out):
            for n, a in zip(names, _unpack_flat(packed, [wd[n].shape for n in names])):
                res[kind, n] = a
    out = _adamw(small_recv, _tok(sp_w, started), _pack_small(md), _pack_small(vd), name="adamw_small")
    for kind, a in zip(("g", "d", "m", "v"), out):
        for n, piece in _unpack_small(a).items():
            res[kind, n] = piece

    after = started
    for n, (handle, own, ax) in sent.items():
        land = _xfer_wait("scatter", handle, ax, after, "rs_wait_" + n)
        out = _adamw(land, wd[n][0], md[n][0], vd[n][0], own=own, name="adamw_" + n)
        after = out[0]
        for kind, a in zip(("g", "d", "m", "v"), out):
            res[kind, n] = a[None]
        if n == "w_o_rnn":
            out = _adamw(gw_mod[None], w_mod[0], m_w_mod[0], v_w_mod[0], name="adamw_w_mod")
            after = out[0]
            for kind, a in zip(("g", "d", "m", "v"), out):
                res[kind, 'w_mod'] = a[None]

    outs = [loss, grad_x]
    for kind in ("g", "d", "m", "v"):
        outs += [res[kind, n].reshape(wd[n].shape) for n in WEIGHTS]
    return tuple(outs)
```

```python
import functools
import math

import jax
import jax.numpy as jnp
from jax import lax
from jax.experimental import pallas as pl
from jax.experimental.pallas import tpu as pltpu

F32 = jnp.float32
BF16 = jnp.bfloat16
MESH = pl.DeviceIdType.MESH

N_DEV = 8
HEAD_DIM = 128
Q_GROUP = 4
RNN_BLOCK = 128
GRID_W = 64
CONV_WIDTH = 4
LRU_C = 8.0
ROPE_THETA = 10000.0
NORM_EPS = 1e-6
N_MOD = 6
ADAM_LR = 0.001
ADAM_B1 = 0.9
ADAM_B2 = 0.999
ADAM_EPS = 1e-08
ADAM_WD = 0.01
ADAM_STEP = 10

LANES = 128
SUBLANES = 8
VMEM_LIMIT = 56 * 1024 * 1024
HBM_SPEC = pl.BlockSpec(memory_space=pltpu.HBM)


def _params(sem=None, vmem=VMEM_LIMIT):
    kw = dict(vmem_limit_bytes=vmem)
    if sem is not None:
        kw["dimension_semantics"] = sem
    return pltpu.CompilerParams(**kw)


def _pick(dim, target, mult):
    for t in range(min(dim, target), 0, -1):
        if dim % t == 0 and t % mult == 0:
            return t
    return dim


def _my_pos():
    return lax.axis_index("x"), lax.axis_index("y"), lax.axis_index("c")


def _flip(pos, d):
    x, y, c = pos
    return (1 - x if d & 4 else x, 1 - y if d & 2 else y, 1 - c if d & 1 else c)


def _lin(pos):
    return 4 * pos[0] + 2 * pos[1] + pos[2]


def _slab(ref, axis, idx, size):
    if axis == 0:
        return ref.at[pl.ds(idx * size, size)]
    assert axis == 1
    return ref.at[:, pl.ds(idx * size, size)]


def _all_gather(arrs, axes, name, with_token=False):
    n = len(arrs)
    sizes = [a.shape[ax] for a, ax in zip(arrs, axes)]

    def body(*refs):
        ins, outs, token = refs[:n], refs[n:2 * n], refs[2 * n]
        send_sems, recv_sems, local_sems = refs[2 * n + 1:]
        token[...] = jnp.zeros_like(token)
        me = _my_pos()
        sibling = _flip(me, 1)
        chips = [2, 4, 6]

        def copy(a, k, block_pos, to, src=None):
            dst = _slab(outs[a], axes[a], _lin(block_pos), sizes[a])
            return pltpu.make_async_remote_copy(
                src_ref=dst if src is None else src, dst_ref=dst,
                send_sem=send_sems.at[7 * a + k], recv_sem=recv_sems.at[7 * a + k],
                device_id=to, device_id_type=MESH)

        local, first, passed = [], [], []
        for a in range(n):
            cp = pltpu.make_async_copy(ins[a], _slab(outs[a], axes[a], _lin(me), sizes[a]), local_sems.at[a])
            cp.start()
            local.append(cp)
        for a in range(n):
            f = [copy(a, 0, me, sibling, src=ins[a])]
            f += [copy(a, 1 + j, me, _flip(me, m), src=ins[a]) for j, m in enumerate(chips)]
            for cp in f:
                cp.start()
            first += f
        for a in range(n):
            for j, m in enumerate(chips):
                copy(a, 1 + j, _flip(me, m), me).wait_recv()
                cp = copy(a, 4 + j, _flip(me, m), sibling)
                cp.start()
                passed.append(cp)
        for a in range(n):
            copy(a, 0, sibling, me).wait_recv()
            for j, m in enumerate(chips):
                copy(a, 4 + j, _flip(sibling, m), me).wait_recv()
        for cp in first + passed:
            cp.wait_send()
        for cp in local:
            cp.wait()

    out_shape = []
    for a, ax in zip(arrs, axes):
        shp = list(a.shape)
        shp[ax] *= N_DEV
        out_shape.append(jax.ShapeDtypeStruct(tuple(shp), a.dtype))
    out_shape.append(jax.ShapeDtypeStruct((SUBLANES, LANES), F32))
    res = pl.pallas_call(
        body, name=name, out_shape=tuple(out_shape),
        in_specs=[HBM_SPEC] * n, out_specs=tuple([HBM_SPEC] * n + [pl.BlockSpec(memory_space=pltpu.VMEM)]),
        scratch_shapes=[pltpu.SemaphoreType.DMA((7 * n,)), pltpu.SemaphoreType.DMA((7 * n,)),
                        pltpu.SemaphoreType.DMA((n,))],
    )(*arrs)
    return (list(res[:n]), res[n]) if with_token else list(res[:n])


def _all_to_all(arrs, axes, name):
    n = len(arrs)
    sizes = [a.shape[ax] // N_DEV for a, ax in zip(arrs, axes)]

    def body(*refs):
        ins, outs, token = refs[:n], refs[n:2 * n], refs[2 * n]
        send_sems, recv_sems, local_sems = refs[2 * n + 1:]
        token[...] = jnp.zeros_like(token)
        me = _my_pos()
        my = _lin(me)
        copies, local = [], []
        for a in range(n):
            cp = pltpu.make_async_copy(_slab(ins[a], axes[a], my, sizes[a]), outs[a].at[my], local_sems.at[a])
            cp.start()
            local.append(cp)
        for d in range(1, N_DEV):
            peer = _flip(me, d)
            for a in range(n):
                cp = pltpu.make_async_remote_copy(
                    src_ref=_slab(ins[a], axes[a], _lin(peer), sizes[a]), dst_ref=outs[a].at[my],
                    send_sem=send_sems.at[7 * a + d - 1], recv_sem=recv_sems.at[7 * a + d - 1],
                    device_id=peer, device_id_type=MESH)
                cp.start()
                copies.append((a, d, cp))
        for a, d, cp in copies:
            peer = _flip(me, d)
            pltpu.make_async_remote_copy(
                src_ref=_slab(ins[a], axes[a], _lin(peer), sizes[a]), dst_ref=outs[a].at[_lin(peer)],
                send_sem=send_sems.at[7 * a + d - 1], recv_sem=recv_sems.at[7 * a + d - 1],
                device_id=peer, device_id_type=MESH).wait_recv()
            cp.wait_send()
        for cp in local:
            cp.wait()

    out_shape = []
    for a, ax, s in zip(arrs, axes, sizes):
        shp = list(a.shape)
        shp[ax] = s
        out_shape.append(jax.ShapeDtypeStruct((N_DEV,) + tuple(shp), a.dtype))
    out_shape.append(jax.ShapeDtypeStruct((SUBLANES, LANES), F32))
    res = pl.pallas_call(
        body, name=name, out_shape=tuple(out_shape),
        in_specs=[HBM_SPEC] * n, out_specs=tuple([HBM_SPEC] * n + [pl.BlockSpec(memory_space=pltpu.VMEM)]),
        scratch_shapes=[pltpu.SemaphoreType.DMA((7 * n,)), pltpu.SemaphoreType.DMA((7 * n,)),
                        pltpu.SemaphoreType.DMA((n,))],
    )(*arrs)
    return list(res[:n]), res[n]


SEM_SPEC = pl.BlockSpec(memory_space=pltpu.SEMAPHORE)
_EFFECT = pltpu.SideEffectType.DATAFLOW_SIDE_EFFECTING
N_PEERS = N_DEV - 1


def _split_copies(kind, src_ref, land_ref, send_sems, recv_sems, axis, size, arriving):
    me = _my_pos()
    out = []
    for dd in range(1, N_DEV):
        peer = _flip(me, dd)
        if kind == "gather":
            src, dst = src_ref, _slab(land_ref, axis, _lin(peer if arriving else me), size)
        else:
            src, dst = _slab(src_ref, axis, _lin(peer), size), land_ref.at[dd - 1]
        out.append(pltpu.make_async_remote_copy(
            src_ref=src, dst_ref=dst, send_sem=send_sems.at[dd - 1], recv_sem=recv_sems.at[dd - 1],
            device_id=peer, device_id_type=MESH))
    return out


def _xfer_start(kind, src, land, axis, name):
    size = src.shape[axis] if kind == "gather" else src.shape[axis] // N_DEV

    def body(src_ref, land_ref, send_sems, recv_sems, src_thru, land_thru, token):
        for cp in _split_copies(kind, src_ref, land_ref, send_sems, recv_sems, axis, size, False):
            cp.start()
        token[...] = jnp.zeros_like(token)

    return pl.pallas_call(
        body, name=name,
        out_shape=(pltpu.SemaphoreType.DMA((N_PEERS,)), pltpu.SemaphoreType.DMA((N_PEERS,)),
                   pltpu.HBM(src.shape, src.dtype), pltpu.HBM(land.shape, land.dtype),
                   jax.ShapeDtypeStruct((SUBLANES, LANES), F32)),
        in_specs=(HBM_SPEC, HBM_SPEC),
        out_specs=(SEM_SPEC, SEM_SPEC, HBM_SPEC, HBM_SPEC, pl.BlockSpec(memory_space=pltpu.VMEM)),
        input_output_aliases={0: 2, 1: 3},
        compiler_params=pltpu.CompilerParams(has_side_effects=_EFFECT),
    )(pltpu.with_memory_space_constraint(src, pltpu.HBM), pltpu.with_memory_space_constraint(land, pltpu.HBM))


def _xfer_wait(kind, handle, axis, after, name):
    send_sems, recv_sems, src_thru, land_thru, _ = handle
    size = src_thru.shape[axis] if kind == "gather" else src_thru.shape[axis] // N_DEV

    def body(src_ref, land_ref, send_sems, recv_sems, after_ref, src_dead, got_ref):
        for cp in _split_copies(kind, src_ref, land_ref, send_sems, recv_sems, axis, size, True):
            cp.wait_send()
            cp.wait_recv()

    return pl.pallas_call(
        body, name=name,
        out_shape=(pltpu.HBM(src_thru.shape, src_thru.dtype), pltpu.HBM(land_thru.shape, land_thru.dtype)),
        in_specs=(HBM_SPEC, HBM_SPEC, SEM_SPEC, SEM_SPEC, pl.BlockSpec(memory_space=pl.ANY)),
        out_specs=(HBM_SPEC, HBM_SPEC), input_output_aliases={0: 0, 1: 1},
        compiler_params=pltpu.CompilerParams(has_side_effects=_EFFECT),
    )(src_thru, land_thru, send_sems, recv_sems, after)[1]


def _place_blocks(shards, axes, name):
    n = len(shards)

    def body(*refs):
        ins, outs, sems = refs[:n], refs[n:2 * n], refs[2 * n]
        my = _lin(_my_pos())
        cps = [pltpu.make_async_copy(ins[a], _slab(outs[a], axes[a], my, ins[a].shape[axes[a]]), sems.at[a])
               for a in range(n)]
        for cp in cps:
            cp.start()
        for cp in cps:
            cp.wait()

    out_shape = []
    for a, ax in zip(shards, axes):
        shp = list(a.shape)
        shp[ax] *= N_DEV
        out_shape.append(jax.ShapeDtypeStruct(tuple(shp), a.dtype))
    return pl.pallas_call(
        body, name=name, out_shape=tuple(out_shape), in_specs=[HBM_SPEC] * n, out_specs=tuple([HBM_SPEC] * n),
        scratch_shapes=[pltpu.SemaphoreType.DMA((n,))],
    )(*shards)


def _scatter_start(grad, axis, my, name):
    shp = list(grad.shape)
    shp[axis] //= N_DEV
    own = lax.dynamic_slice_in_dim(grad, my * shp[axis], shp[axis], axis)
    return _xfer_start("scatter", grad, lax.empty((N_PEERS,) + tuple(shp), grad.dtype), axis, name), own, axis


def _tok(a, *tokens):
    for t in tokens:
        a = a + t[0, 0].astype(a.dtype)
    return a


def _mm(a, b, *, name, ta=False, tb=False, out_dtype=F32, a_fn=None, out_fn=None, extras=(), bias=None,
        tm=1024, tn=1024, tk=2048):
    if ta:
        kdim, m = a.shape
    else:
        m, kdim = a.shape
    if tb:
        n, kb = b.shape
    else:
        kb, n = b.shape
    assert kb == kdim, (a.shape, b.shape, ta, tb)
    tm = _pick(m, tm, LANES if ta else SUBLANES)
    tn = _pick(n, tn, LANES)
    tk = _pick(kdim, tk, LANES)
    nk = kdim // tk
    ne = len(extras)
    has_bias = bias is not None

    def body(*refs):
        a_ref, b_ref = refs[0], refs[1]
        pos = 2
        bias_ref = None
        if has_bias:
            bias_ref = refs[pos]
            pos += 1
        extra_refs = refs[pos:pos + ne]
        o_ref = refs[pos + ne]
        av = a_ref[...]
        if a_fn is not None:
            av = a_fn(av)
        dn = (((0 if ta else 1,), (1 if tb else 0,)), ((), ()))
        part = lax.dot_general(av.astype(BF16), b_ref[...].astype(BF16), dn, preferred_element_type=F32)

        def finish(r):
            if has_bias:
                r = r + bias_ref[...]
            if out_fn is not None:
                r = out_fn(r, *[e[...] for e in extra_refs])
            o_ref[...] = r.astype(out_dtype)

        if nk == 1:
            finish(part)
            return
        acc = refs[pos + ne + 1]
        k = pl.program_id(2)

        @pl.when(k == 0)
        def _():
            acc[...] = part

        @pl.when(k > 0)
        def _():
            acc[...] += part

        @pl.when(k == nk - 1)
        def _():
            finish(acc[...])

    in_specs = [
        pl.BlockSpec((tk, tm), lambda i, j, k: (k, i)) if ta else pl.BlockSpec((tm, tk), lambda i, j, k: (i, k)),
        pl.BlockSpec((tn, tk), lambda i, j, k: (j, k)) if tb else pl.BlockSpec((tk, tn), lambda i, j, k: (k, j)),
    ]
    args = [a, b]
    if has_bias:
        in_specs.append(pl.BlockSpec((1, tn), lambda i, j, k: (0, j)))
        args.append(bias)
    for e in extras:
        in_specs.append(pl.BlockSpec((tm, tn), lambda i, j, k: (i, j)))
        args.append(e)
    return pl.pallas_call(
        body, name=name, grid=(m // tm, n // tn, nk),
        in_specs=in_specs, out_specs=pl.BlockSpec((tm, tn), lambda i, j, k: (i, j)),
        out_shape=jax.ShapeDtypeStruct((m, n), out_dtype),
        scratch_shapes=[pltpu.VMEM((tm, tn), F32)] if nk > 1 else [],
        compiler_params=_params(("parallel", "parallel", "arbitrary")),
    )(*args)


def _row(arr, cb=0, rb=0, w=None):
    return ("row", arr, cb, rb, w)


def _bc(arr, cb=0, w=None):
    return ("bc", arr, cb, 0, w)


def _full(arr):
    return ("full", arr, 0, 0, None)


def _rowwise(fn, ins, outs, *, nr, tr, nc, cw, name):
    n_in = len(ins)
    outs = [(o[0], o[1], (o[2] if len(o) > 2 and o[2] else cw)) for o in outs]

    def body(*refs):
        in_refs, out_refs = refs[:n_in], refs[n_in:]
        j, i = pl.program_id(0), pl.program_id(1)
        res = fn(i, j, *[r[...] for r in in_refs])
        if not isinstance(res, (tuple, list)):
            res = (res,)
        for (kind, _, _), o_ref, r in zip(outs, out_refs, res):
            if kind == "row":
                o_ref[...] = r.astype(o_ref.dtype)
            else:
                @pl.when(i == 0)
                def _(o_ref=o_ref):
                    o_ref[...] = jnp.zeros_like(o_ref)

                o_ref[...] += r

    in_specs, args = [], []
    for kind, arr, cb, rb, w in ins:
        w = w or cw
        if kind == "row":
            in_specs.append(pl.BlockSpec((tr, w), lambda j, i, cb=cb, rb=rb: (i + rb, j + cb)))
        elif kind == "bc":
            in_specs.append(pl.BlockSpec((arr.shape[0], w), lambda j, i, cb=cb: (0, j + cb)))
        else:
            in_specs.append(pl.BlockSpec(arr.shape, lambda j, i, nd=arr.ndim: (0,) * nd))
        args.append(arr)
    out_specs, out_shape = [], []
    for kind, spec, w in outs:
        if kind == "row":
            out_specs.append(pl.BlockSpec((tr, w), lambda j, i: (i, j)))
            out_shape.append(jax.ShapeDtypeStruct((nr * tr, nc * w), spec))
        else:
            out_specs.append(pl.BlockSpec((spec, w), lambda j, i: (0, j)))
            out_shape.append(jax.ShapeDtypeStruct((spec, nc * w), F32))
    res = pl.pallas_call(
        body, name=name, grid=(nc, nr), in_specs=in_specs, out_specs=tuple(out_specs), out_shape=tuple(out_shape),
        compiler_params=_params(("parallel", "arbitrary")),
    )(*args)
    return res


def _sum_parts(own, parts, *, name):
    p, r, c = parts.shape
    tr = _pick(r, max(SUBLANES, (1 << 18) // c), SUBLANES)

    def body(o_ref, p_ref, g_out):
        g = o_ref[...].astype(F32)
        for q in range(p):
            g = g + p_ref[q].astype(F32)
        g_out[...] = g

    blk = pl.BlockSpec((tr, c), lambda i: (i, 0))
    return pl.pallas_call(
        body, name=name, grid=(r // tr,), in_specs=[blk, pl.BlockSpec((p, tr, c), lambda i: (0, i, 0))],
        out_specs=blk, out_shape=jax.ShapeDtypeStruct((r, c), F32), compiler_params=_params(("parallel",)),
    )(own, parts)


def _adamw(parts, w, m, v, *, name, own=None):
    p, r, c = parts.shape
    tr = _pick(r, max(SUBLANES, (1 << 18) // c), 16 if parts.dtype == BF16 else SUBLANES)
    bc1 = 1.0 - ADAM_B1 ** ADAM_STEP
    bc2 = 1.0 - ADAM_B2 ** ADAM_STEP
    has_own = own is not None

    def body(*refs):
        p_ref, w_ref, m_ref, v_ref = refs[:4]
        g_out, d_out, m_out, v_out = refs[4 + has_own:]
        if has_own:
            g = refs[4][...].astype(F32) + p_ref[0].astype(F32)
        else:
            g = p_ref[0].astype(F32)
        for q in range(1, p):
            g = g + p_ref[q].astype(F32)
        m_new = ADAM_B1 * m_ref[...] + (1.0 - ADAM_B1) * g
        v_new = ADAM_B2 * v_ref[...] + (1.0 - ADAM_B2) * (g * g)
        m_hat = m_new / bc1
        v_hat = v_new / bc2
        g_out[...] = g
        d_out[...] = -ADAM_LR * (m_hat / (jnp.sqrt(v_hat) + ADAM_EPS) + ADAM_WD * w_ref[...])
        m_out[...] = m_new
        v_out[...] = v_new

    blk = pl.BlockSpec((tr, c), lambda i: (i, 0))
    return pl.pallas_call(
        body, name=name, grid=(r // tr,),
        in_specs=[pl.BlockSpec((p, tr, c), lambda i: (0, i, 0)), blk, blk, blk] + ([blk] if has_own else []),
        out_specs=(blk, blk, blk, blk), out_shape=tuple(jax.ShapeDtypeStruct((r, c), F32) for _ in range(4)),
        compiler_params=_params(("parallel",)),
    )(*([parts, w, m, v] + ([own] if has_own else [])))


_GELU_C = math.sqrt(2.0 / math.pi)
_GELU_K = 0.044715


def _sigmoid(z):
    return 0.5 * jnp.tanh(0.5 * z) + 0.5


def _silu(z):
    return z * _sigmoid(z)


def _dsilu(z):
    s = _sigmoid(z)
    return s * (1.0 + z * (1.0 - s))


def _gelu(z):
    return 0.5 * z * (1.0 + jnp.tanh(_GELU_C * (z + _GELU_K * z * z * z)))


def _dgelu(z):
    t = jnp.tanh(_GELU_C * (z + _GELU_K * z * z * z))
    return 0.5 * (1.0 + t) + 0.5 * z * (1.0 - t * t) * (_GELU_C * (1.0 + 3.0 * _GELU_K * z * z))


def _softplus(z):
    return jnp.maximum(z, 0.0) + jnp.log(1.0 + jnp.exp(-jnp.abs(z)))


def _expm1(z):
    small = z * (1.0 + z * (0.5 + z * (1.0 / 6.0 + z * (1.0 / 24.0))))
    return jnp.where(jnp.abs(z) < 0.03, small, jnp.exp(z) - 1.0)


def _rms(xv):
    return lax.rsqrt(jnp.mean(xv * xv, axis=-1, keepdims=True) + NORM_EPS)


def _rms_bwd(xhat, r, dxhat):
    return r * (dxhat - xhat * jnp.mean(dxhat * xhat, axis=-1, keepdims=True))


def _swap_pairs(y):
    lane = lax.broadcasted_iota(jnp.int32, y.shape, 1)
    return jnp.where(lane % 2 == 0, pltpu.roll(y, LANES - 1, 1), pltpu.roll(y, 1, 1))


def _dot(a, b, ca, cb):
    return lax.dot_general(a, b, (((ca,), (cb,)), ((), ())), preferred_element_type=F32)


def _softmax_rows(qh, kk):
    s = _dot(qh, kk, 1, 1) * (HEAD_DIM ** -0.5)
    p = jnp.exp(s - jnp.max(s, axis=-1, keepdims=True))
    return p / jnp.sum(p, axis=-1, keepdims=True)


def _attn_fwd(q, k, v, *, tq, name):
    s_len, aw = q.shape
    t_len, kvw = k.shape
    gw = Q_GROUP * HEAD_DIM

    def body(q_ref, k_ref, v_ref, o_ref):
        kk, vv = k_ref[...], v_ref[...]
        for g in range(Q_GROUP):
            cols = slice(g * HEAD_DIM, (g + 1) * HEAD_DIM)
            p = _softmax_rows(q_ref[:, cols], kk)
            o_ref[:, cols] = _dot(p.astype(BF16), vv, 1, 0).astype(BF16)

    return pl.pallas_call(
        body, name=name, grid=(kvw // HEAD_DIM, s_len // tq),
        in_specs=[pl.BlockSpec((tq, gw), lambda h, i: (i, h)), pl.BlockSpec((t_len, HEAD_DIM), lambda h, i: (0, h)),
                  pl.BlockSpec((t_len, HEAD_DIM), lambda h, i: (0, h))],
        out_specs=pl.BlockSpec((tq, gw), lambda h, i: (i, h)),
        out_shape=jax.ShapeDtypeStruct((s_len, aw), BF16),
        compiler_params=_params(("parallel", "parallel")),
    )(q, k, v)


def _attn_bwd(q, k, v, do, *, tq, name):
    s_len, aw = q.shape
    t_len, kvw = k.shape
    gw = Q_GROUP * HEAD_DIM

    def body(q_ref, k_ref, v_ref, do_ref, dq_ref, dk_ref, dv_ref):
        @pl.when(pl.program_id(1) == 0)
        def _():
            dk_ref[...] = jnp.zeros_like(dk_ref)
            dv_ref[...] = jnp.zeros_like(dv_ref)

        kk, vv = k_ref[...], v_ref[...]
        for g in range(Q_GROUP):
            cols = slice(g * HEAD_DIM, (g + 1) * HEAD_DIM)
            qh, doh = q_ref[:, cols], do_ref[:, cols]
            p = _softmax_rows(qh, kk)
            dp = _dot(doh, vv, 1, 1)
            ds = p * (dp - jnp.sum(p * dp, axis=-1, keepdims=True)) * (HEAD_DIM ** -0.5)
            dsb = ds.astype(BF16)
            dq_ref[:, cols] = _dot(dsb, kk, 1, 0)
            dk_ref[...] += _dot(dsb, qh, 0, 0)
            dv_ref[...] += _dot(p.astype(BF16), doh, 0, 0)

    qspec = pl.BlockSpec((tq, gw), lambda h, i: (i, h))
    kspec = pl.BlockSpec((t_len, HEAD_DIM), lambda h, i: (0, h))
    return pl.pallas_call(
        body, name=name, grid=(kvw // HEAD_DIM, s_len // tq),
        in_specs=[qspec, kspec, kspec, qspec], out_specs=(qspec, kspec, kspec),
        out_shape=(jax.ShapeDtypeStruct((s_len, aw), F32), jax.ShapeDtypeStruct((t_len, kvw), F32),
                   jax.ShapeDtypeStruct((t_len, kvw), F32)),
        compiler_params=_params(("parallel", "arbitrary")),
    )(q, k, v, do)


SHIFTS = (-2, -1, 1, 2)
SCAN_UNROLL = 4


def _shift_masks(l_len, t_len):
    t = jnp.arange(t_len)[:, None]
    cols = []
    for d in SHIFTS:
        td = t + d
        ok = ((t < l_len) & (td >= 0) & (td < l_len)) | ((t >= l_len) & (td >= l_len) & (td < t_len))
        cols.append(jnp.broadcast_to(ok.astype(F32), (t_len, RNN_BLOCK)))
    return jnp.concatenate(cols, axis=1)


def _shift_rows(xv, d, m_ref, l_len, t_len):
    if d == 0:
        return xv
    k = SHIFTS.index(d)
    return pltpu.roll(xv, (-d) % t_len, 0) * m_ref[:, k * RNN_BLOCK:(k + 1) * RNN_BLOCK]


def _scan_chunks(chains, c0, n, carries):
    def step(kk, hs):
        new = []
        for (a_ref, b_ref, h_ref, rev), hprev in zip(chains, hs):
            c = c0 + (n - 1 - kk if rev else kk)
            r0 = pl.multiple_of(c * SUBLANES, SUBLANES)
            a8 = a_ref[pl.ds(r0, SUBLANES), :]
            b8 = b_ref[pl.ds(r0, SUBLANES), :]
            row = lax.broadcasted_iota(jnp.int32, a8.shape, 0)
            for s in (1, 2, 4):
                if rev:
                    ok = row < SUBLANES - s
                    sh = SUBLANES - s
                else:
                    ok = row >= s
                    sh = s
                a_sh = jnp.where(ok, pltpu.roll(a8, sh, 0), 1.0)
                b_sh = jnp.where(ok, pltpu.roll(b8, sh, 0), 0.0)
                b8 = a8 * b_sh + b8
                a8 = a8 * a_sh
            h8 = b8 + a8 * hprev
            h_ref[pl.ds(r0, SUBLANES), :] = h8
            edge = 0 if rev else SUBLANES - 1
            new.append(jnp.broadcast_to(h8[edge:edge + 1, :], h8.shape))
        return tuple(new)

    return lax.fori_loop(0, n, step, tuple(carries), unroll=SCAN_UNROLL)


def _rglru_conv(x, cw_ref, cb_ref, m_ref, l_len, t_len):
    y = cb_ref[...] + _shift_rows(x, -1, m_ref, l_len, t_len) * cw_ref[0:1, :]
    for kk in range(1, CONV_WIDTH):
        y = y + _shift_rows(x, kk - 1, m_ref, l_len, t_len) * cw_ref[kk:kk + 1, :]
    return y


def _rglru_gates(xc, wr, wi, br, bi, lam):
    xb = xc.astype(BF16)
    r = _sigmoid(_dot(xb, wr.astype(BF16), 1, 0) + br)
    ig = _sigmoid(_dot(xb, wi.astype(BF16), 1, 0) + bi)
    sp = _softplus(-lam)
    log_a = (-LRU_C) * r * sp
    a = jnp.exp(log_a)
    mult = jnp.sqrt(-_expm1(2.0 * log_a))
    u = ig * xc
    return r, ig, sp, a, mult, u


def _rglru_specs(proj, xr_blk, xg_blk, t_len):
    col = lambda off: pl.BlockSpec((t_len, RNN_BLOCK), lambda h, off=off: (0, off + h))
    vec = lambda rows: pl.BlockSpec((rows, RNN_BLOCK), lambda h: (0, h))
    wsp = pl.BlockSpec((2, 1, RNN_BLOCK, RNN_BLOCK), lambda h: (0, h, 0, 0))
    msp = pl.BlockSpec((t_len, len(SHIFTS) * RNN_BLOCK), lambda h: (0, 0))
    return col(xr_blk), col(xg_blk), vec, wsp, msp


def _rglru_fwd(proj, xr_blk, xg_blk, masks, conv_w, conv_b, w_rg, w_ig, b_rg, b_ig, lam, *, l_len, name):
    t_len = proj.shape[0]
    s_len = t_len - l_len
    d_rnn = conv_b.shape[1]
    n_l, n_t = l_len // SUBLANES, t_len // SUBLANES

    def body(xr_ref, xg_ref, m_ref, cw_ref, cb_ref, wr_ref, wi_ref, br_ref, bi_ref, lam_ref, rg_ref, a_s, b_s, h_s):
        xc = _rglru_conv(xr_ref[...], cw_ref, cb_ref, m_ref, l_len, t_len)
        for d in range(2):
            _, _, _, a, mult, u = _rglru_gates(xc, wr_ref[d, 0], wi_ref[d, 0], br_ref[d:d + 1, :],
                                               bi_ref[d:d + 1, :], lam_ref[d:d + 1, :])
            a_s[d] = a
            b_s[d] = mult * u
        chains = [(a_s.at[0], b_s.at[0], h_s.at[0], False), (a_s.at[1], b_s.at[1], h_s.at[1], True)]
        zero = jnp.zeros((SUBLANES, RNN_BLOCK), F32)
        carry = _scan_chunks(chains, 0, n_l, (zero, zero))
        _scan_chunks(chains, n_l, n_t - n_l, carry)
        rnn = h_s[0, pl.ds(l_len, s_len), :] + h_s[1, pl.ds(l_len, s_len), :]
        rg_ref[...] = (rnn * _gelu(xg_ref[pl.ds(l_len, s_len), :])).astype(BF16)

    xr_spec, xg_spec, vec, wsp, msp = _rglru_specs(proj, xr_blk, xg_blk, t_len)
    return pl.pallas_call(
        body, name=name, grid=(d_rnn // RNN_BLOCK,),
        in_specs=[xr_spec, xg_spec, msp, vec(CONV_WIDTH), vec(1), wsp, wsp, vec(2), vec(2), vec(2)],
        out_specs=pl.BlockSpec((s_len, RNN_BLOCK), lambda h: (0, h)),
        out_shape=jax.ShapeDtypeStruct((s_len, d_rnn), BF16),
        scratch_shapes=[pltpu.VMEM((2, t_len, RNN_BLOCK), F32)] * 3,
        compiler_params=_params(("parallel",)),
    )(proj, proj, masks, conv_w, conv_b, w_rg, w_ig, b_rg, b_ig, lam)


N_RNN_VEC = 16


def _rglru_bwd(proj, xr_blk, xg_blk, masks, drg, conv_w, conv_b, w_rg, w_ig, b_rg, b_ig, lam, *, l_len, name):
    t_len = proj.shape[0]
    s_len = t_len - l_len
    d_rnn = conv_b.shape[1]
    n_l, n_t = l_len // SUBLANES, t_len // SUBLANES

    def body(xr_ref, xg_ref, m_ref, drg_ref, cw_ref, cb_ref, wr_ref, wi_ref, br_ref, bi_ref, lam_ref,
             dxr_ref, dxg_ref, dvec_ref, dwr_ref, dwi_ref, a_s, b_s, h_s):
        t = lax.broadcasted_iota(jnp.int32, (t_len, RNN_BLOCK), 0)
        x = xr_ref[...]
        xc = _rglru_conv(x, cw_ref, cb_ref, m_ref, l_len, t_len)
        xb = xc.astype(BF16)
        gates = []
        for d in range(2):
            gt = _rglru_gates(xc, wr_ref[d, 0], wi_ref[d, 0], br_ref[d:d + 1, :], bi_ref[d:d + 1, :],
                              lam_ref[d:d + 1, :])
            gates.append(gt)
            a_s[d] = gt[3]
            b_s[d] = gt[4] * gt[5]
        zero = jnp.zeros((SUBLANES, RNN_BLOCK), F32)
        chains = [(a_s.at[0], b_s.at[0], h_s.at[0], False), (a_s.at[1], b_s.at[1], h_s.at[1], True)]
        carry = _scan_chunks(chains, 0, n_l, (zero, zero))
        _scan_chunks(chains, n_l, n_t - n_l, carry)

        h0, h1 = h_s[0], h_s[1]
        xg = xg_ref[...]
        dr = jnp.concatenate([jnp.zeros((l_len, RNN_BLOCK), F32), drg_ref[...].astype(F32)], axis=0)
        dxg_ref[...] = (dr * (h0 + h1) * _dgelu(xg)).astype(dxg_ref.dtype)
        drnn = dr * _gelu(xg)
        a_s[0] = pltpu.roll(gates[0][3], t_len - 1, 0)
        a_s[1] = pltpu.roll(gates[1][3], 1, 0)
        b_s[0] = drnn
        b_s[1] = drnn
        hprev = [jnp.where(t == 0, 0.0, pltpu.roll(h0, 1, 0)),
                 jnp.where(t == l_len - 1, 0.0, pltpu.roll(h1, t_len - 1, 0))]
        chains = [(a_s.at[0], b_s.at[0], h_s.at[0], True), (a_s.at[1], b_s.at[1], h_s.at[1], False)]
        carry = _scan_chunks(chains, n_l, n_t - n_l, (zero, zero))
        _scan_chunks(chains, 0, n_l, carry)

        dxc = jnp.zeros((t_len, RNN_BLOCK), F32)
        for d in range(2):
            r, ig, sp, a, mult, u = gates[d]
            lamb = h_s[d]
            da = lamb * hprev[d]
            dmult = lamb * u
            du = lamb * mult
            dlog_a = da * a - dmult * (a * a) / mult
            dzr = (dlog_a * ((-LRU_C) * sp)) * r * (1.0 - r)
            dsp = jnp.sum(dlog_a * ((-LRU_C) * r), axis=0, keepdims=True)
            dzi = (du * xc) * ig * (1.0 - ig)
            dzrb, dzib = dzr.astype(BF16), dzi.astype(BF16)
            dxc = dxc + du * ig + _dot(dzrb, wr_ref[d, 0].astype(BF16), 1, 1) + _dot(dzib, wi_ref[d, 0].astype(BF16), 1, 1)
            dwr_ref[d, 0] = _dot(xb, dzrb, 0, 0)
            dwi_ref[d, 0] = _dot(xb, dzib, 0, 0)
            dvec_ref[5 + d:6 + d, :] = jnp.sum(dzr, axis=0, keepdims=True)
            dvec_ref[7 + d:8 + d, :] = jnp.sum(dzi, axis=0, keepdims=True)
            dvec_ref[9 + d:10 + d, :] = dsp * (-_sigmoid(-lam_ref[d:d + 1, :]))
        dvec_ref[4:5, :] = jnp.sum(dxc, axis=0, keepdims=True)
        dvec_ref[11:N_RNN_VEC, :] = jnp.zeros((N_RNN_VEC - 11, RNN_BLOCK), F32)
        dxr = jnp.zeros((t_len, RNN_BLOCK), F32)
        for kk in range(CONV_WIDTH):
            dvec_ref[kk:kk + 1, :] = jnp.sum(dxc * _shift_rows(x, kk - 1, m_ref, l_len, t_len), axis=0, keepdims=True)
            dxr = dxr + _shift_rows(dxc, 1 - kk, m_ref, l_len, t_len) * cw_ref[kk:kk + 1, :]
        dxr_ref[...] = dxr.astype(dxr_ref.dtype)

    xr_spec, xg_spec, vec, wsp, msp = _rglru_specs(proj, xr_blk, xg_blk, t_len)
    col_t = pl.BlockSpec((t_len, RNN_BLOCK), lambda h: (0, h))
    return pl.pallas_call(
        body, name=name, grid=(d_rnn // RNN_BLOCK,),
        in_specs=[xr_spec, xg_spec, msp, pl.BlockSpec((s_len, RNN_BLOCK), lambda h: (0, h)), vec(CONV_WIDTH), vec(1),
                  wsp, wsp, vec(2), vec(2), vec(2)],
        out_specs=(col_t, col_t, vec(N_RNN_VEC), wsp, wsp),
        out_shape=(jax.ShapeDtypeStruct((t_len, d_rnn), BF16), jax.ShapeDtypeStruct((t_len, d_rnn), BF16),
                   jax.ShapeDtypeStruct((N_RNN_VEC, d_rnn), F32), jax.ShapeDtypeStruct(w_rg.shape, F32),
                   jax.ShapeDtypeStruct(w_ig.shape, F32)),
        scratch_shapes=[pltpu.VMEM((2, t_len, RNN_BLOCK), F32)] * 3,
        compiler_params=_params(("parallel",)),
    )(proj, proj, masks, drg, conv_w, conv_b, w_rg, w_ig, b_rg, b_ig, lam)


def _colsum(v):
    return jnp.sum(v, axis=0, keepdims=True)


def _qk_fwd_heads(xv, gain, cos_e, sin_e):
    outs = []
    for h in range(xv.shape[1] // HEAD_DIM):
        xh = xv[:, h * HEAD_DIM:(h + 1) * HEAD_DIM]
        y = xh * _rms(xh) * gain
        outs.append(y * cos_e + _swap_pairs(y) * sin_e)
    return outs[0] if len(outs) == 1 else jnp.concatenate(outs, axis=1)


def _qk_bwd_heads(xv, dov, gain, cos_e, sin_e):
    outs = []
    dgain = jnp.zeros((1, HEAD_DIM), F32)
    for h in range(xv.shape[1] // HEAD_DIM):
        cols = slice(h * HEAD_DIM, (h + 1) * HEAD_DIM)
        xh, doh = xv[:, cols], dov[:, cols]
        r = _rms(xh)
        xhat = xh * r
        dy = doh * cos_e + _swap_pairs(doh * sin_e)
        dgain = dgain + _colsum(dy * xhat)
        outs.append(_rms_bwd(xhat, r, dy * gain))
    return (outs[0] if len(outs) == 1 else jnp.concatenate(outs, axis=1)), dgain


WEIGHTS = ['c_ctx', 'w_mod', 'b_mod', 'g_mix', 'g_mlp', 'w_in', 'q_gain', 'k_gain', 'conv_w', 'conv_b', 'w_rg', 'b_rg',
           'w_ig', 'b_ig', 'lru_lambda', 'w_o_attn', 'w_o_rnn', 'w_out', 'w_up', 'w_down', 'g_final']
REPLICATED = ['c_ctx', 'b_mod', 'g_mix', 'g_mlp', 'q_gain', 'k_gain', 'conv_b', 'w_rg', 'w_ig', 'g_final']
SMALL_SHARDED = [('conv_w', CONV_WIDTH), ('b_rg', 2), ('b_ig', 2), ('lru_lambda', 2)]
SMALL_ROWS = 16


PACK_ROWS = 512


def _pack_flat(arrs, rows_mult):
    flat = jnp.concatenate([a.reshape(-1) for a in arrs])
    n = flat.shape[0]
    unit = rows_mult * LANES
    pad = (-n) % unit
    if pad:
        flat = jnp.concatenate([flat, jnp.zeros((pad,), flat.dtype)])
    return flat.reshape(-1, LANES)


def _unpack_flat(packed, shapes):
    flat = packed.reshape(-1)
    out, pos = [], 0
    for shp in shapes:
        n = math.prod(shp)
        out.append(flat[pos:pos + n].reshape(shp))
        pos += n
    return out


def _pack_small(d):
    rows = [d[n][0] for n, _ in SMALL_SHARDED]
    cols = rows[0].shape[1]
    used = sum(k for _, k in SMALL_SHARDED)
    return jnp.concatenate(rows + [jnp.zeros((SMALL_ROWS - used, cols), F32)], axis=0)


def _unpack_small(p):
    out, pos = {}, 0
    for n, k in SMALL_SHARDED:
        out[n] = p[pos:pos + k][None]
        pos += k
    return out


def kernel(x, c, ctx, c_ctx, w_mod, b_mod, g_mix, g_mlp, w_in, q_gain, k_gain, conv_w, conv_b, w_rg, b_rg, w_ig, b_ig, lru_lambda, w_o_attn, w_o_rnn, w_out, w_up, w_down, g_final, loss_target, m_c_ctx, m_w_mod, m_b_mod, m_g_mix, m_g_mlp, m_w_in, m_q_gain, m_k_gain, m_conv_w, m_conv_b, m_w_rg, m_b_rg, m_w_ig, m_b_ig, m_lru_lambda, m_w_o_attn, m_w_o_rnn, m_w_out, m_w_up, m_w_down, m_g_final, v_c_ctx, v_w_mod, v_b_mod, v_g_mix, v_g_mlp, v_w_in, v_q_gain, v_k_gain, v_conv_w, v_conv_b, v_w_rg, v_b_rg, v_w_ig, v_b_ig, v_lru_lambda, v_w_o_attn, v_w_o_rnn, v_w_out, v_w_up, v_w_down, v_g_final):
    wd = dict(c_ctx=c_ctx, w_mod=w_mod, b_mod=b_mod, g_mix=g_mix, g_mlp=g_mlp, w_in=w_in, q_gain=q_gain, k_gain=k_gain,
              conv_w=conv_w, conv_b=conv_b, w_rg=w_rg, b_rg=b_rg, w_ig=w_ig, b_ig=b_ig, lru_lambda=lru_lambda,
              w_o_attn=w_o_attn, w_o_rnn=w_o_rnn, w_out=w_out, w_up=w_up, w_down=w_down, g_final=g_final)
    md = dict(zip(WEIGHTS, (m_c_ctx, m_w_mod, m_b_mod, m_g_mix, m_g_mlp, m_w_in, m_q_gain, m_k_gain, m_conv_w, m_conv_b,
                            m_w_rg, m_b_rg, m_w_ig, m_b_ig, m_lru_lambda, m_w_o_attn, m_w_o_rnn, m_w_out, m_w_up,
                            m_w_down, m_g_final)))
    vd = dict(zip(WEIGHTS, (v_c_ctx, v_w_mod, v_b_mod, v_g_mix, v_g_mlp, v_w_in, v_q_gain, v_k_gain, v_conv_w, v_conv_b,
                            v_w_rg, v_b_rg, v_w_ig, v_b_ig, v_lru_lambda, v_w_o_attn, v_w_o_rnn, v_w_out, v_w_up,
                            v_w_down, v_g_final)))

    xs, tgt, ctxs = x[0], loss_target[0], ctx[0]
    s_len, d = xs.shape
    l_len = ctxs.shape[0]
    t_len = s_len + l_len
    n_in = w_in.shape[2] * N_DEV
    aw = (n_in - 4 * d) * 2 // 3
    kvw = aw // Q_GROUP
    assert aw == d and n_in == aw + 2 * kvw + 4 * d
    nmod = N_MOD * d // N_DEV
    tr = 128
    assert l_len % tr == 0 and s_len % tr == 0
    lb = l_len // tr
    tq = min(256, s_len)
    my = _lin(_my_pos())
    c_ctx2 = c_ctx[None]
    g_final2 = g_final[None]

    c_all = _all_gather([jnp.broadcast_to(c, (SUBLANES, d))], [0], "ag_cond")[0][::SUBLANES]
    c16 = jnp.concatenate([c_all, jnp.broadcast_to(c_ctx2, (N_DEV, d))], axis=0)
    b_mod_sh = lax.dynamic_slice(b_mod, (0, my * nmod), (1, nmod))
    mod_part = _mm(c16, w_mod[0], a_fn=_silu, bias=b_mod_sh, name="mod_fwd")
    (mod_g,), order = _all_gather([mod_part], [1], "ag_mod", with_token=True)
    mod_lat = lax.dynamic_slice(mod_g, (my, 0), (1, N_MOD * d))
    sh_a, sc_a, ga_a, sh_f, sc_f, ga_f = [mod_lat[:, k * d:(k + 1) * d] for k in range(N_MOD)]
    csh_a, csc_a = mod_g[N_DEV:N_DEV + 1, :d], mod_g[N_DEV:N_DEV + 1, d:2 * d]

    sp_w = _pack_small(wd)
    (w_in_f, sp_f), order = _all_gather([_tok(w_in[0], order).astype(BF16), sp_w], [1, 1], "ag_w_in", with_token=True)
    conv_w_f, b_rg_f, b_ig_f, lam_f = sp_f[0:4], sp_f[4:6], sp_f[6:8], sp_f[8:10]
    late_axis = dict(w_o_attn=0, w_o_rnn=0, w_out=0, w_up=1, w_down=0)
    placed = _place_blocks([_tok(wd[n][0], order).astype(BF16) for n in late_axis], list(late_axis.values()), "place_own")
    late = {}
    for (n, ax), full in zip(late_axis.items(), placed):
        late[n] = _xfer_start("gather", _tok(wd[n][0], order).astype(BF16), full, ax, "ag_start_" + n)
        order = late[n][4]
    g_mix_fwd = _tok(g_mix, order)
    late_wait = lambda n, after: _xfer_wait("gather", late[n], late_axis[n], after, "ag_wait_" + n)

    pos = jnp.arange(s_len)
    n_freq = HEAD_DIM // 4
    inv_freq = ROPE_THETA ** (-jnp.arange(n_freq, dtype=F32) / n_freq)
    ang = jnp.concatenate([(pos // GRID_W).astype(F32)[:, None] * inv_freq,
                           (pos % GRID_W).astype(F32)[:, None] * inv_freq], axis=-1)
    sign = jnp.where(jnp.arange(HEAD_DIM) % 2 == 0, -1.0, 1.0).astype(F32)
    cos_e = jnp.concatenate([jnp.ones((l_len, HEAD_DIM), F32), jnp.repeat(jnp.cos(ang), 2, axis=-1)], axis=0)
    sin_e = jnp.concatenate([jnp.zeros((l_len, HEAD_DIM), F32), jnp.repeat(jnp.sin(ang), 2, axis=-1) * sign], axis=0)

    x_all = jnp.concatenate([ctxs, xs], axis=0)

    def nm1_fwd(i, j, xv, g, csh, csc, sh, sc):
        is_ctx = i < lb
        n = xv * _rms(xv) * g
        return n * (1.0 + jnp.where(is_ctx, csc, sc)) + jnp.where(is_ctx, csh, sh)

    h_all = _rowwise(nm1_fwd, [_row(x_all), _bc(g_mix_fwd), _bc(csh_a), _bc(csc_a), _bc(sh_a), _bc(sc_a)],
                     [("row", BF16)], nr=t_len // tr, tr=tr, nc=1, cw=d, name="norm1_fwd")[0]
    proj = _mm(h_all, w_in_f, name="proj_fwd")

    kb = aw // kvw
    q = _rowwise(lambda i, j, xv, gain, ce, se: _qk_fwd_heads(xv, gain, ce, se),
                 [_row(proj, 0, lb, aw), _full(q_gain), _row(cos_e, 0, lb, HEAD_DIM), _row(sin_e, 0, lb, HEAD_DIM)],
                 [("row", BF16, aw)], nr=s_len // tr, tr=tr, nc=1, cw=aw, name="q_fwd")[0]
    k_all, v_all = _rowwise(lambda i, j, kv_, vv_, gain, ce, se: (_qk_fwd_heads(kv_, gain, ce, se), vv_),
                            [_row(proj, kb, 0, kvw), _row(proj, kb + 1, 0, kvw), _full(k_gain),
                             _row(cos_e, 0, 0, HEAD_DIM), _row(sin_e, 0, 0, HEAD_DIM)],
                            [("row", BF16, kvw), ("row", BF16, kvw)], nr=t_len // tr, tr=tr, nc=1, cw=kvw, name="kv_fwd")
    attn_o = _attn_fwd(q, k_all, v_all, tq=tq, name="attn_fwd")

    xr_blk = (aw + 2 * kvw) // RNN_BLOCK
    xg_blk = xr_blk + d // RNN_BLOCK
    shift_masks = _shift_masks(l_len, t_len)
    rg = _rglru_fwd(proj, xr_blk, xg_blk, shift_masks, conv_w_f, conv_b, w_rg[0], w_ig[0], b_rg_f, b_ig_f, lam_f, l_len=l_len,
                    name="rglru_fwd")

    w_oa_f = late_wait("w_o_attn", attn_o)
    y_attn = _mm(attn_o, w_oa_f, name="o_attn_fwd")
    w_or_f = late_wait("w_o_rnn", rg)
    y_rnn = _mm(rg, w_or_f, name="o_rnn_fwd")
    hw = d // 2
    glb = (aw + 2 * kvw + 2 * d) // hw

    def merge_fwd(i, j, ya, yr, gla, glr):
        return _sigmoid(gla) * ya + _sigmoid(glr) * yr

    u = _rowwise(merge_fwd, [_row(y_attn), _row(y_rnn), _row(proj, glb, lb), _row(proj, glb + 2, lb)],
                 [("row", BF16)], nr=s_len // tr, tr=tr, nc=2, cw=hw, name="merge_fwd")[0]
    w_out_f = late_wait("w_out", u)
    mix = _mm(u, w_out_f, name="out_fwd")

    def res1_fwd(i, j, xv, mixv, gaa, gm, shf, scf):
        x1v = xv + gaa * mixv
        n = x1v * _rms(x1v) * gm
        return x1v, n * (1.0 + scf) + shf

    x1, h2 = _rowwise(res1_fwd, [_row(xs), _row(mix), _bc(ga_a), _bc(g_mlp), _bc(sh_f), _bc(sc_f)],
                      [("row", F32), ("row", BF16)], nr=s_len // tr, tr=tr, nc=1, cw=d, name="norm2_fwd")
    relu2 = lambda z: jnp.square(jnp.maximum(z, 0.0))
    w_up_f = late_wait("w_up", h2)
    up = _mm(h2, w_up_f, name="up_fwd")
    w_down_f = late_wait("w_down", up)
    down = _mm(up, w_down_f, a_fn=relu2, name="down_fwd")

    def final_fn(i, j, x1v, dn, tg, gaf, gf):
        x2 = x1v + gaf * dn
        r = _rms(x2)
        xh = x2 * r
        e = xh * gf - tg
        dy = e * (1.0 / d)
        dx2 = _rms_bwd(xh, r, dy * gf)
        return dx2, dx2 * gaf, _colsum(e * e) * (0.5 / d), _colsum(dy * xh), _colsum(dx2 * dn)

    dx2, ddown, loss_cols, g_gfinal, d_ga_f = _rowwise(
        final_fn, [_row(x1), _row(down), _row(tgt), _bc(ga_f), _bc(g_final2)],
        [("row", F32), ("row", BF16), ("acc", 1), ("acc", 1), ("acc", 1)], nr=s_len // tr, tr=tr, nc=1, cw=d,
        name="final_bwd")
    loss = lax.psum(jnp.sum(loss_cols), ("x", "y", "c"))

    dup = _mm(ddown, w_down_f, tb=True, out_fn=lambda r, upv: r * (2.0 * jnp.maximum(upv, 0.0)), extras=[up],
              out_dtype=BF16, name="down_bwd_x")
    gw_down = _mm(up, ddown, ta=True, a_fn=relu2, out_dtype=BF16, name="down_bwd_w")
    sent = dict(w_down=_scatter_start(gw_down, 0, my, "rs_start_w_down"))
    dh2 = _mm(dup, w_up_f, tb=True, name="up_bwd_x")
    gw_up = _mm(h2, dup, ta=True, out_dtype=BF16, name="up_bwd_w")
    sent["w_up"] = _scatter_start(gw_up, 1, my, "rs_start_w_up")
    sc_f_bwd = _tok(sc_f, sent["w_down"][0][4], sent["w_up"][0][4])

    def norm2_bwd(i, j, x1v, mixv, dh2v, dx2v, gaa, gm, scf):
        r = _rms(x1v)
        xh = x1v * r
        n = xh * gm
        dn = dh2v * (1.0 + scf)
        dx1 = dx2v + _rms_bwd(xh, r, dn * gm)
        return dx1, dx1 * gaa, _colsum(dh2v), _colsum(dh2v * n), _colsum(dn * xh), _colsum(dx1 * mixv)

    dx1, dmix, d_sh_f, d_sc_f, g_gmlp, d_ga_a = _rowwise(
        norm2_bwd, [_row(x1), _row(mix), _row(dh2), _row(dx2), _bc(ga_a), _bc(g_mlp), _bc(sc_f_bwd)],
        [("row", F32), ("row", BF16), ("acc", 1), ("acc", 1), ("acc", 1), ("acc", 1)], nr=s_len // tr, tr=tr, nc=1,
        cw=d, name="norm2_bwd")

    du = _mm(dmix, w_out_f, tb=True, name="out_bwd_x")
    gw_out = _mm(u, dmix, ta=True, out_dtype=BF16, name="out_bwd_w")
    sent["w_out"] = _scatter_start(gw_out, 0, my, "rs_start_w_out")

    def merge_bwd(i, j, duv, ya, yr, gla, glr):
        ga, gr = _sigmoid(gla), _sigmoid(glr)
        return duv * ga, duv * gr, duv * ya * ga * (1.0 - ga), duv * yr * gr * (1.0 - gr)

    dya, dyr, dgla, dglr = _rowwise(
        merge_bwd, [_row(du), _row(y_attn), _row(y_rnn), _row(proj, glb, lb), _row(proj, glb + 2, lb)],
        [("row", BF16)] * 4, nr=s_len // tr, tr=tr, nc=2, cw=hw, name="merge_bwd")
    dattn = _mm(dya, w_oa_f, tb=True, out_dtype=BF16, name="o_attn_bwd_x")
    gw_oa = _mm(attn_o, dya, ta=True, out_dtype=BF16, name="o_attn_bwd_w")
    drg = _mm(dyr, w_or_f, tb=True, out_dtype=BF16, name="o_rnn_bwd_x")
    gw_or = _mm(rg, dyr, ta=True, out_dtype=BF16, name="o_rnn_bwd_w")
    sent["w_o_attn"] = _scatter_start(gw_oa, 0, my, "rs_start_w_o_attn")
    sent["w_o_rnn"] = _scatter_start(gw_or, 0, my, "rs_start_w_o_rnn")
    conv_b_bwd = _tok(conv_b, sent["w_out"][0][4], sent["w_o_attn"][0][4], sent["w_o_rnn"][0][4])

    dq, dk, dv = _attn_bwd(q, k_all, v_all, dattn, tq=tq, name="attn_bwd")
    dxr, dxg, dvec, g_wrg, g_wig = _rglru_bwd(proj, xr_blk, xg_blk, shift_masks, drg, conv_w_f, conv_b_bwd, w_rg[0], w_ig[0], b_rg_f,
                                               b_ig_f, lam_f, l_len=l_len, name="rglru_bwd")
    gate_part = _pack_flat([g_wrg, g_wig], N_DEV * SUBLANES)
    gate_sent = _scatter_start(gate_part, 0, my, "rs_start_gates")
    q_gain_bwd = _tok(q_gain, gate_sent[0][4])

    dq_p, g_qgain = _rowwise(lambda i, j, xv, dov, gain, ce, se: _qk_bwd_heads(xv, dov, gain, ce, se),
                             [_row(proj, 0, lb, aw), _row(dq, 0, 0, aw), _full(q_gain_bwd), _row(cos_e, 0, lb, HEAD_DIM),
                              _row(sin_e, 0, lb, HEAD_DIM)],
                             [("row", BF16, aw), ("acc", 1, HEAD_DIM)], nr=s_len // tr, tr=tr, nc=1, cw=aw, name="q_bwd")

    def kv_bwd(i, j, xv, dkv, dvv, gain, ce, se):
        dk_p, dgain = _qk_bwd_heads(xv, dkv, gain, ce, se)
        return dk_p, dvv, dgain

    dk_p, dv_p, g_kgain = _rowwise(kv_bwd, [_row(proj, kb, 0, kvw), _row(dk, 0, 0, kvw), _row(dv, 0, 0, kvw),
                                            _full(k_gain), _row(cos_e, 0, 0, HEAD_DIM), _row(sin_e, 0, 0, HEAD_DIM)],
                                   [("row", BF16, kvw), ("row", BF16, kvw), ("acc", 1, HEAD_DIM)], nr=t_len // tr, tr=tr,
                                   nc=1, cw=kvw, name="kv_bwd")

    pad_ctx = lambda a: jnp.pad(a, ((l_len, 0), (0, 0)))
    dproj = jnp.concatenate([pad_ctx(dq_p), dk_p, dv_p, dxr, dxg, pad_ctx(dgla), pad_ctx(dglr)], axis=1)
    dh_all = _mm(dproj, w_in_f, tb=True, name="proj_bwd_x")
    g_mix_bwd = g_mix

    def norm1_bwd(i, j, xv, dhv, dresv, g, csc, sc):
        is_ctx = i < lb
        r = _rms(xv)
        xh = xv * r
        dn = dhv * (1.0 + jnp.where(is_ctx, csc, sc))
        dsh, dsc = _colsum(dhv), _colsum(dhv * (xh * g))
        zero = jnp.zeros_like(dsh)
        return (dresv + _rms_bwd(xh, r, dn * g), jnp.where(is_ctx, dsh, zero), jnp.where(is_ctx, dsc, zero),
                jnp.where(is_ctx, zero, dsh), jnp.where(is_ctx, zero, dsc), _colsum(dn * xh))

    dx_all, d_csh, d_csc, d_sh_a, d_sc_a, g_gmix = _rowwise(
        norm1_bwd, [_row(x_all), _row(dh_all), _row(pad_ctx(dx1)), _bc(g_mix_bwd), _bc(csc_a), _bc(sc_a)],
        [("row", F32)] + [("acc", 1)] * 5, nr=t_len // tr, tr=tr, nc=1, cw=d, name="norm1_bwd")
    grad_x = dx_all[l_len:][None]

    dmod_lat = jnp.concatenate([d_sh_a, d_sc_a, d_ga_a, d_sh_f, d_sc_f, d_ga_f], axis=1)
    dmod_ctx = jnp.concatenate([d_csh, d_csc, jnp.zeros((1, (N_MOD - 2) * d), F32)], axis=1)
    dm_g = _all_gather([jnp.concatenate([dmod_lat, dmod_ctx, jnp.zeros((SUBLANES - 2, N_MOD * d), F32)], axis=0)], [0],
                       "ag_dmod")[0]
    dm16 = jnp.concatenate([dm_g[0::SUBLANES], dm_g[1::SUBLANES]], axis=0)
    dm_sh = lax.dynamic_slice(dm16, (0, my * nmod), (2 * N_DEV, nmod))
    gw_mod = _mm(c16, dm_sh, ta=True, a_fn=_silu, name="mod_bwd_w")
    dc16 = _mm(dm_sh, w_mod[0], tb=True, name="mod_bwd_x")

    def cctx_fn(i, j, pv, cc, dl, dc):
        row = lax.broadcasted_iota(jnp.int32, pv.shape, 0)
        return _colsum(jnp.where(row >= N_DEV, pv, 0.0)) * _dsilu(cc), dl + dc

    g_cctx, g_bmod = _rowwise(cctx_fn, [_full(dc16), _full(c_ctx2), _full(dmod_lat), _full(dmod_ctx)],
                              [("row", F32, d), ("row", F32, N_MOD * d)], nr=1, tr=1, nc=1, cw=d, name="cctx_bwd")

    rep_part = dict(c_ctx=g_cctx, b_mod=g_bmod, g_mix=g_gmix, g_mlp=g_gmlp, q_gain=g_qgain, k_gain=g_kgain,
                    conv_b=dvec[4:5], g_final=g_gfinal)
    rep_small = [n for n in REPLICATED if n in rep_part]
    gates = [n for n in REPLICATED if n not in rep_part]
    part_pack = _pack_flat([rep_part[n] for n in rep_small], PACK_ROWS)
    rep_rows = part_pack.shape[0]
    gate_land = _xfer_wait("scatter", gate_sent[0], 0, g_bmod, "rs_wait_gates")
    gate_sum = _sum_parts(gate_sent[1], gate_land, name="gate_sum")
    (gate_full, parts), order = _all_gather([gate_sum, part_pack], [0, 0], "ag_rep_grads", with_token=True)
    small_g = jnp.concatenate([dvec[0:4], dvec[5:11], jnp.zeros((SMALL_ROWS - 10, d), F32)], axis=0)
    (small_recv,), order = _all_to_all([_tok(small_g, order)], [1], "a2a_small_grads")

    gw_in = _mm(h_all, dproj, ta=True, out_dtype=BF16, name="proj_bwd_w", bias=_tok(jnp.zeros((1, n_in), F32), order))
    sent["w_in"] = _scatter_start(gw_in, 1, my, "rs_start_w_in")
    started = sent["w_in"][0][4]

    res = {}
    for names, pr, mult, nm in ((rep_small, parts.reshape(N_DEV, rep_rows, LANES), PACK_ROWS, "adamw_rep"),
                                (gates, gate_full[None], N_DEV * SUBLANES, "adamw_gates")):
        rep_out = _adamw(pr, _tok(_pack_flat([wd[n] for n in names], mult), started),
                         _pack_flat([md[n] for n in names], mult), _pack_flat([vd[n] for n in names], mult), name=nm)
        for kind, packed in zip(("g", "d", "m", "v"), rep_out):
            for n, a in zip(names, _unpack_flat(packed, [wd[n].shape for n in names])):
                res[kind, n] = a
    out = _adamw(small_recv, _tok(sp_w, started), _pack_small(md), _pack_small(vd), name="adamw_small")
    for kind, a in zip(("g", "d", "m", "v"), out):
        for n, piece in _unpack_small(a).items():
            res[kind, n] = piece

    after = started
    for n, (handle, own, ax) in sent.items():
        land = _xfer_wait("scatter", handle, ax, after, "rs_wait_" + n)
        out = _adamw(land, wd[n][0], md[n][0], vd[n][0], own=own, name="adamw_" + n)
        after = out[0]
        for kind, a in zip(("g", "d", "m", "v"), out):
            res[kind, n] = a[None]
        if n == "w_o_rnn":
            out = _adamw(gw_mod[None], w_mod[0], m_w_mod[0], v_w_mod[0], name="adamw_w_mod")
            after = out[0]
            for kind, a in zip(("g", "d", "m", "v"), out):
                res[kind, 'w_mod'] = a[None]

    outs = [loss, grad_x]
    for kind in ("g", "d", "m", "v"):
        outs += [res[kind, n].reshape(wd[n].shape) for n in WEIGHTS]
    return tuple(outs)
```

```python
import functools
import math

import jax
import jax.numpy as jnp
from jax import lax
from jax.experimental import pallas as pl
from jax.experimental.pallas import tpu as pltpu

F32 = jnp.float32
BF16 = jnp.bfloat16
MESH = pl.DeviceIdType.MESH

N_DEV = 8
HEAD_DIM = 128
Q_GROUP = 4
RNN_BLOCK = 128
GRID_W = 64
CONV_WIDTH = 4
LRU_C = 8.0
ROPE_THETA = 10000.0
NORM_EPS = 1e-6
N_MOD = 6
ADAM_LR = 0.001
ADAM_B1 = 0.9
ADAM_B2 = 0.999
ADAM_EPS = 1e-08
ADAM_WD = 0.01
ADAM_STEP = 10

LANES = 128
SUBLANES = 8
VMEM_LIMIT = 56 * 1024 * 1024
HBM_SPEC = pl.BlockSpec(memory_space=pltpu.HBM)


def _params(sem=None, vmem=VMEM_LIMIT):
    kw = dict(vmem_limit_bytes=vmem)
    if sem is not None:
        kw["dimension_semantics"] = sem
    return pltpu.CompilerParams(**kw)


def _pick(dim, target, mult):
    for t in range(min(dim, target), 0, -1):
        if dim % t == 0 and t % mult == 0:
            return t
    return dim


def _my_pos():
    return lax.axis_index("x"), lax.axis_index("y"), lax.axis_index("c")


def _flip(pos, d):
    x, y, c = pos
    return (1 - x if d & 4 else x, 1 - y if d & 2 else y, 1 - c if d & 1 else c)


def _lin(pos):
    return 4 * pos[0] + 2 * pos[1] + pos[2]


def _slab(ref, axis, idx, size):
    if axis == 0:
        return ref.at[pl.ds(idx * size, size)]
    assert axis == 1
    return ref.at[:, pl.ds(idx * size, size)]


def _all_gather(arrs, axes, name, with_token=False):
    n = len(arrs)
    sizes = [a.shape[ax] for a, ax in zip(arrs, axes)]

    def body(*refs):
        ins, outs, token = refs[:n], refs[n:2 * n], refs[2 * n]
        send_sems, recv_sems, local_sems = refs[2 * n + 1:]
        token[...] = jnp.zeros_like(token)
        me = _my_pos()
        sibling = _flip(me, 1)
        chips = [2, 4, 6]

        def copy(a, k, block_pos, to, src=None):
            dst = _slab(outs[a], axes[a], _lin(block_pos), sizes[a])
            return pltpu.make_async_remote_copy(
                src_ref=dst if src is None else src, dst_ref=dst,
                send_sem=send_sems.at[7 * a + k], recv_sem=recv_sems.at[7 * a + k],
                device_id=to, device_id_type=MESH)

        local, first, passed = [], [], []
        for a in range(n):
            cp = pltpu.make_async_copy(ins[a], _slab(outs[a], axes[a], _lin(me), sizes[a]), local_sems.at[a])
            cp.start()
            local.append(cp)
        for a in range(n):
            f = [copy(a, 0, me, sibling, src=ins[a])]
            f += [copy(a, 1 + j, me, _flip(me, m), src=ins[a]) for j, m in enumerate(chips)]
            for cp in f:
                cp.start()
            first += f
        for a in range(n):
            for j, m in enumerate(chips):
                copy(a, 1 + j, _flip(me, m), me).wait_recv()
                cp = copy(a, 4 + j, _flip(me, m), sibling)
                cp.start()
                passed.append(cp)
        for a in range(n):
            copy(a, 0, sibling, me).wait_recv()
            for j, m in enumerate(chips):
                copy(a, 4 + j, _flip(sibling, m), me).wait_recv()
        for cp in first + passed:
            cp.wait_send()
        for cp in local:
            cp.wait()

    out_shape = []
    for a, ax in zip(arrs, axes):
        shp = list(a.shape)
        shp[ax] *= N_DEV
        out_shape.append(jax.ShapeDtypeStruct(tuple(shp), a.dtype))
    out_shape.append(jax.ShapeDtypeStruct((SUBLANES, LANES), F32))
    res = pl.pallas_call(
        body, name=name, out_shape=tuple(out_shape),
        in_specs=[HBM_SPEC] * n, out_specs=tuple([HBM_SPEC] * n + [pl.BlockSpec(memory_space=pltpu.VMEM)]),
        scratch_shapes=[pltpu.SemaphoreType.DMA((7 * n,)), pltpu.SemaphoreType.DMA((7 * n,)),
                        pltpu.SemaphoreType.DMA((n,))],
    )(*arrs)
    return (list(res[:n]), res[n]) if with_token else list(res[:n])


def _all_to_all(arrs, axes, name):
    n = len(arrs)
    sizes = [a.shape[ax] // N_DEV for a, ax in zip(arrs, axes)]

    def body(*refs):
        ins, outs, token = refs[:n], refs[n:2 * n], refs[2 * n]
        send_sems, recv_sems, local_sems = refs[2 * n + 1:]
        token[...] = jnp.zeros_like(token)
        me = _my_pos()
        my = _lin(me)
        copies, local = [], []
        for a in range(n):
            cp = pltpu.make_async_copy(_slab(ins[a], axes[a], my, sizes[a]), outs[a].at[my], local_sems.at[a])
            cp.start()
            local.append(cp)
        for d in range(1, N_DEV):
            peer = _flip(me, d)
            for a in range(n):
                cp = pltpu.make_async_remote_copy(
                    src_ref=_slab(ins[a], axes[a], _lin(peer), sizes[a]), dst_ref=outs[a].at[my],
                    send_sem=send_sems.at[7 * a + d - 1], recv_sem=recv_sems.at[7 * a + d - 1],
                    device_id=peer, device_id_type=MESH)
                cp.start()
                copies.append((a, d, cp))
        for a, d, cp in copies:
            peer = _flip(me, d)
            pltpu.make_async_remote_copy(
                src_ref=_slab(ins[a], axes[a], _lin(peer), sizes[a]), dst_ref=outs[a].at[_lin(peer)],
                send_sem=send_sems.at[7 * a + d - 1], recv_sem=recv_sems.at[7 * a + d - 1],
                device_id=peer, device_id_type=MESH).wait_recv()
            cp.wait_send()
        for cp in local:
            cp.wait()

    out_shape = []
    for a, ax, s in zip(arrs, axes, sizes):
        shp = list(a.shape)
        shp[ax] = s
        out_shape.append(jax.ShapeDtypeStruct((N_DEV,) + tuple(shp), a.dtype))
    out_shape.append(jax.ShapeDtypeStruct((SUBLANES, LANES), F32))
    res = pl.pallas_call(
        body, name=name, out_shape=tuple(out_shape),
        in_specs=[HBM_SPEC] * n, out_specs=tuple([HBM_SPEC] * n + [pl.BlockSpec(memory_space=pltpu.VMEM)]),
        scratch_shapes=[pltpu.SemaphoreType.DMA((7 * n,)), pltpu.SemaphoreType.DMA((7 * n,)),
                        pltpu.SemaphoreType.DMA((n,))],
    )(*arrs)
    return list(res[:n]), res[n]


SEM_SPEC = pl.BlockSpec(memory_space=pltpu.SEMAPHORE)
_EFFECT = pltpu.SideEffectType.DATAFLOW_SIDE_EFFECTING
N_PEERS = N_DEV - 1


def _split_copies(kind, src_ref, land_ref, send_sems, recv_sems, axis, size, arriving):
    me = _my_pos()
    out = []
    for dd in (range(2, N_DEV, 2) if kind == "chips" else range(1, N_DEV)):
        peer = _flip(me, dd)
        k = dd - 1
        if kind == "gather":
            src, dst = src_ref, _slab(land_ref, axis, _lin(peer if arriving else me), size)
        elif kind == "scatter":
            src, dst = _slab(src_ref, axis, _lin(peer), size), land_ref.at[k]
        else:
            k = dd // 2 - 1
            src, dst = _slab(src_ref, axis, 2 * peer[0] + peer[1], size), land_ref.at[k]
        out.append(pltpu.make_async_remote_copy(
            src_ref=src, dst_ref=dst, send_sem=send_sems.at[k], recv_sem=recv_sems.at[k],
            device_id=peer, device_id_type=MESH))
    return out


N_CHIPS = N_DEV // 2


def _split_size(kind, src_shape, axis):
    return src_shape[axis] // {"gather": 1, "scatter": N_DEV, "chips": N_CHIPS}[kind]


def _xfer_start(kind, src, land, axis, name):
    size = _split_size(kind, src.shape, axis)

    def body(src_ref, land_ref, send_sems, recv_sems, src_thru, land_thru, token):
        for cp in _split_copies(kind, src_ref, land_ref, send_sems, recv_sems, axis, size, False):
            cp.start()
        token[...] = jnp.zeros_like(token)

    return pl.pallas_call(
        body, name=name,
        out_shape=(pltpu.SemaphoreType.DMA((N_PEERS,)), pltpu.SemaphoreType.DMA((N_PEERS,)),
                   pltpu.HBM(src.shape, src.dtype), pltpu.HBM(land.shape, land.dtype),
                   jax.ShapeDtypeStruct((SUBLANES, LANES), F32)),
        in_specs=(HBM_SPEC, HBM_SPEC),
        out_specs=(SEM_SPEC, SEM_SPEC, HBM_SPEC, HBM_SPEC, pl.BlockSpec(memory_space=pltpu.VMEM)),
        input_output_aliases={0: 2, 1: 3},
        compiler_params=pltpu.CompilerParams(has_side_effects=_EFFECT),
    )(pltpu.with_memory_space_constraint(src, pltpu.HBM), pltpu.with_memory_space_constraint(land, pltpu.HBM))


def _xfer_wait(kind, handle, axis, after, name):
    send_sems, recv_sems, src_thru, land_thru, _ = handle
    size = _split_size(kind, src_thru.shape, axis)

    def body(src_ref, land_ref, send_sems, recv_sems, after_ref, src_dead, got_ref):
        for cp in _split_copies(kind, src_ref, land_ref, send_sems, recv_sems, axis, size, True):
            cp.wait_send()
            cp.wait_recv()

    return pl.pallas_call(
        body, name=name,
        out_shape=(pltpu.HBM(src_thru.shape, src_thru.dtype), pltpu.HBM(land_thru.shape, land_thru.dtype)),
        in_specs=(HBM_SPEC, HBM_SPEC, SEM_SPEC, SEM_SPEC, pl.BlockSpec(memory_space=pl.ANY)),
        out_specs=(HBM_SPEC, HBM_SPEC), input_output_aliases={0: 0, 1: 1},
        compiler_params=pltpu.CompilerParams(has_side_effects=_EFFECT),
    )(src_thru, land_thru, send_sems, recv_sems, after)[1]


def _place_block(shard, axis, my, name):
    r, c = shard.shape
    tr = _pick(r, max(16, (1 << 19) // c), 16)
    nr = r // tr
    if axis == 0:
        full, out_map = (r * N_DEV, c), (lambda i, my_ref: (my_ref[0] * nr + i, 0))
    else:
        full, out_map = (r, c * N_DEV), (lambda i, my_ref: (i, my_ref[0]))

    def body(my_ref, x_ref, o_ref):
        o_ref[...] = x_ref[...].astype(BF16)

    return pl.pallas_call(
        body, name=name, out_shape=jax.ShapeDtypeStruct(full, BF16),
        grid_spec=pltpu.PrefetchScalarGridSpec(
            num_scalar_prefetch=1, grid=(nr,), in_specs=[pl.BlockSpec((tr, c), lambda i, my_ref: (i, 0))],
            out_specs=pl.BlockSpec((tr, c), out_map)),
        compiler_params=_params(("arbitrary",)),
    )(jnp.reshape(my, (1,)).astype(jnp.int32), shard)


def _scatter_start(grad, axis, my, name):
    shp = list(grad.shape)
    shp[axis] //= N_DEV
    own = lax.dynamic_slice_in_dim(grad, my * shp[axis], shp[axis], axis)
    return (_xfer_start("scatter", grad, lax.empty((N_PEERS,) + tuple(shp), grad.dtype), axis, name), own, axis,
            "scatter")


def _pair_reduce(grad, my, name):
    r, cols = grad.shape
    sz = cols // N_DEV

    def exchange(g_ref, got_ref, send_sems, recv_sems):
        me = _my_pos()
        sibling = _flip(me, 1)
        cps = []
        for k in range(N_CHIPS):
            cp = pltpu.make_async_remote_copy(
                src_ref=_slab(g_ref, 1, 2 * k + sibling[2], sz), dst_ref=_slab(got_ref, 1, k, sz),
                send_sem=send_sems.at[k], recv_sem=recv_sems.at[k], device_id=sibling, device_id_type=MESH)
            cp.start()
            cps.append(cp)
        for cp in cps:
            cp.wait()

    got = pl.pallas_call(
        exchange, name=name + "_swap", out_shape=jax.ShapeDtypeStruct((r, N_CHIPS * sz), grad.dtype),
        in_specs=[HBM_SPEC], out_specs=HBM_SPEC,
        scratch_shapes=[pltpu.SemaphoreType.DMA((N_CHIPS,)), pltpu.SemaphoreType.DMA((N_CHIPS,))],
    )(grad)

    tr = _pick(r, 512, 16)

    def add(c_ref, g_ref, p_ref, o_ref):
        o_ref[...] = (g_ref[...].astype(F32) + p_ref[...].astype(F32)).astype(o_ref.dtype)

    blk = pl.BlockSpec((tr, sz), lambda k, i, c_ref: (i, k))
    return pl.pallas_call(
        add, name=name + "_sum", out_shape=jax.ShapeDtypeStruct((r, N_CHIPS * sz), grad.dtype),
        grid_spec=pltpu.PrefetchScalarGridSpec(
            num_scalar_prefetch=1, grid=(N_CHIPS, r // tr),
            in_specs=[pl.BlockSpec((tr, sz), lambda k, i, c_ref: (i, 2 * k + c_ref[0])), blk], out_specs=blk),
        compiler_params=_params(("parallel", "parallel")),
    )(jnp.reshape(my % 2, (1,)).astype(jnp.int32), grad, got)


def _chip_scatter_start(pair_sum, my, name):
    r, cols = pair_sum.shape
    sz = cols // N_CHIPS
    own = lax.dynamic_slice_in_dim(pair_sum, (my // 2) * sz, sz, 1)
    return (_xfer_start("chips", pair_sum, lax.empty((N_CHIPS - 1, r, sz), pair_sum.dtype), 1, name), own, 1, "chips")


def _tok(a, *tokens):
    for t in tokens:
        a = a + t[0, 0].astype(a.dtype)
    return a


def _mm(a, b, *, name, ta=False, tb=False, out_dtype=F32, a_fn=None, out_fn=None, extras=(), bias=None,
        tm=1024, tn=1024, tk=2048):
    if ta:
        kdim, m = a.shape
    else:
        m, kdim = a.shape
    if tb:
        n, kb = b.shape
    else:
        kb, n = b.shape
    assert kb == kdim, (a.shape, b.shape, ta, tb)
    tm = _pick(m, tm, LANES if ta else SUBLANES)
    tn = _pick(n, tn, LANES)
    tk = _pick(kdim, tk, LANES)
    nk = kdim // tk
    ne = len(extras)
    has_bias = bias is not None

    def body(*refs):
        a_ref, b_ref = refs[0], refs[1]
        pos = 2
        bias_ref = None
        if has_bias:
            bias_ref = refs[pos]
            pos += 1
        extra_refs = refs[pos:pos + ne]
        o_ref = refs[pos + ne]
        av = a_ref[...]
        if a_fn is not None:
            av = a_fn(av)
        dn = (((0 if ta else 1,), (1 if tb else 0,)), ((), ()))
        part = lax.dot_general(av.astype(BF16), b_ref[...].astype(BF16), dn, preferred_element_type=F32)

        def finish(r):
            if has_bias:
                r = r + bias_ref[...]
            if out_fn is not None:
                r = out_fn(r, *[e[...] for e in extra_refs])
            o_ref[...] = r.astype(out_dtype)

        if nk == 1:
            finish(part)
            return
        acc = refs[pos + ne + 1]
        k = pl.program_id(2)

        @pl.when(k == 0)
        def _():
            acc[...] = part

        @pl.when(k > 0)
        def _():
            acc[...] += part

        @pl.when(k == nk - 1)
        def _():
            finish(acc[...])

    in_specs = [
        pl.BlockSpec((tk, tm), lambda i, j, k: (k, i)) if ta else pl.BlockSpec((tm, tk), lambda i, j, k: (i, k)),
        pl.BlockSpec((tn, tk), lambda i, j, k: (j, k)) if tb else pl.BlockSpec((tk, tn), lambda i, j, k: (k, j)),
    ]
    args = [a, b]
    if has_bias:
        in_specs.append(pl.BlockSpec((1, tn), lambda i, j, k: (0, j)))
        args.append(bias)
    for e in extras:
        in_specs.append(pl.BlockSpec((tm, tn), lambda i, j, k: (i, j)))
        args.append(e)
    return pl.pallas_call(
        body, name=name, grid=(m // tm, n // tn, nk),
        in_specs=in_specs, out_specs=pl.BlockSpec((tm, tn), lambda i, j, k: (i, j)),
        out_shape=jax.ShapeDtypeStruct((m, n), out_dtype),
        scratch_shapes=[pltpu.VMEM((tm, tn), F32)] if nk > 1 else [],
        compiler_params=_params(("parallel", "parallel", "arbitrary")),
    )(*args)


def _row(arr, cb=0, rb=0, w=None):
    return ("row", arr, cb, rb, w)


def _bc(arr, cb=0, w=None):
    return ("bc", arr, cb, 0, w)


def _full(arr):
    return ("full", arr, 0, 0, None)


def _rowwise(fn, ins, outs, *, nr, tr, nc, cw, name):
    n_in = len(ins)
    outs = [(o[0], o[1], (o[2] if len(o) > 2 and o[2] else cw)) for o in outs]

    def body(*refs):
        in_refs, out_refs = refs[:n_in], refs[n_in:]
        j, i = pl.program_id(0), pl.program_id(1)
        res = fn(i, j, *[r[...] for r in in_refs])
        if not isinstance(res, (tuple, list)):
            res = (res,)
        for (kind, _, _), o_ref, r in zip(outs, out_refs, res):
            if kind == "row":
                o_ref[...] = r.astype(o_ref.dtype)
            else:
                @pl.when(i == 0)
                def _(o_ref=o_ref):
                    o_ref[...] = jnp.zeros_like(o_ref)

                o_ref[...] += r

    in_specs, args = [], []
    for kind, arr, cb, rb, w in ins:
        w = w or cw
        if kind == "row":
            in_specs.append(pl.BlockSpec((tr, w), lambda j, i, cb=cb, rb=rb: (i + rb, j + cb)))
        elif kind == "bc":
            in_specs.append(pl.BlockSpec((arr.shape[0], w), lambda j, i, cb=cb: (0, j + cb)))
        else:
            in_specs.append(pl.BlockSpec(arr.shape, lambda j, i, nd=arr.ndim: (0,) * nd))
        args.append(arr)
    out_specs, out_shape = [], []
    for kind, spec, w in outs:
        if kind == "row":
            out_specs.append(pl.BlockSpec((tr, w), lambda j, i: (i, j)))
            out_shape.append(jax.ShapeDtypeStruct((nr * tr, nc * w), spec))
        else:
            out_specs.append(pl.BlockSpec((spec, w), lambda j, i: (0, j)))
            out_shape.append(jax.ShapeDtypeStruct((spec, nc * w), F32))
    res = pl.pallas_call(
        body, name=name, grid=(nc, nr), in_specs=in_specs, out_specs=tuple(out_specs), out_shape=tuple(out_shape),
        compiler_params=_params(("parallel", "arbitrary")),
    )(*args)
    return res


def _sum_parts(own, parts, *, name):
    p, r, c = parts.shape
    tr = _pick(r, max(SUBLANES, (1 << 18) // c), SUBLANES)

    def body(o_ref, p_ref, g_out):
        g = o_ref[...].astype(F32)
        for q in range(p):
            g = g + p_ref[q].astype(F32)
        g_out[...] = g

    blk = pl.BlockSpec((tr, c), lambda i: (i, 0))
    return pl.pallas_call(
        body, name=name, grid=(r // tr,), in_specs=[blk, pl.BlockSpec((p, tr, c), lambda i: (0, i, 0))],
        out_specs=blk, out_shape=jax.ShapeDtypeStruct((r, c), F32), compiler_params=_params(("parallel",)),
    )(own, parts)


def _adamw(parts, w, m, v, *, name, own=None):
    p, r, c = parts.shape
    tr = _pick(r, max(SUBLANES, (1 << 18) // c), 16 if parts.dtype == BF16 else SUBLANES)
    bc1 = 1.0 - ADAM_B1 ** ADAM_STEP
    bc2 = 1.0 - ADAM_B2 ** ADAM_STEP
    has_own = own is not None

    def body(*refs):
        p_ref, w_ref, m_ref, v_ref = refs[:4]
        g_out, d_out, m_out, v_out = refs[4 + has_own:]
        if has_own:
            g = refs[4][...].astype(F32) + p_ref[0].astype(F32)
        else:
            g = p_ref[0].astype(F32)
        for q in range(1, p):
            g = g + p_ref[q].astype(F32)
        m_new = ADAM_B1 * m_ref[...] + (1.0 - ADAM_B1) * g
        v_new = ADAM_B2 * v_ref[...] + (1.0 - ADAM_B2) * (g * g)
        m_hat = m_new / bc1
        v_hat = v_new / bc2
        g_out[...] = g
        d_out[...] = -ADAM_LR * (m_hat / (jnp.sqrt(v_hat) + ADAM_EPS) + ADAM_WD * w_ref[...])
        m_out[...] = m_new
        v_out[...] = v_new

    blk = pl.BlockSpec((tr, c), lambda i: (i, 0))
    return pl.pallas_call(
        body, name=name, grid=(r // tr,),
        in_specs=[pl.BlockSpec((p, tr, c), lambda i: (0, i, 0)), blk, blk, blk] + ([blk] if has_own else []),
        out_specs=(blk, blk, blk, blk), out_shape=tuple(jax.ShapeDtypeStruct((r, c), F32) for _ in range(4)),
        compiler_params=_params(("parallel",)),
    )(*([parts, w, m, v] + ([own] if has_own else [])))


_GELU_C = math.sqrt(2.0 / math.pi)
_GELU_K = 0.044715


def _sigmoid(z):
    return 0.5 * jnp.tanh(0.5 * z) + 0.5


def _silu(z):
    return z * _sigmoid(z)


def _dsilu(z):
    s = _sigmoid(z)
    return s * (1.0 + z * (1.0 - s))


def _gelu(z):
    return 0.5 * z * (1.0 + jnp.tanh(_GELU_C * (z + _GELU_K * z * z * z)))


def _dgelu(z):
    t = jnp.tanh(_GELU_C * (z + _GELU_K * z * z * z))
    return 0.5 * (1.0 + t) + 0.5 * z * (1.0 - t * t) * (_GELU_C * (1.0 + 3.0 * _GELU_K * z * z))


def _softplus(z):
    return jnp.maximum(z, 0.0) + jnp.log(1.0 + jnp.exp(-jnp.abs(z)))


def _expm1(z):
    small = z * (1.0 + z * (0.5 + z * (1.0 / 6.0 + z * (1.0 / 24.0))))
    return jnp.where(jnp.abs(z) < 0.03, small, jnp.exp(z) - 1.0)


def _rms(xv):
    return lax.rsqrt(jnp.mean(xv * xv, axis=-1, keepdims=True) + NORM_EPS)


def _rms_bwd(xhat, r, dxhat):
    return r * (dxhat - xhat * jnp.mean(dxhat * xhat, axis=-1, keepdims=True))


def _swap_pairs(y):
    lane = lax.broadcasted_iota(jnp.int32, y.shape, 1)
    return jnp.where(lane % 2 == 0, pltpu.roll(y, LANES - 1, 1), pltpu.roll(y, 1, 1))


def _dot(a, b, ca, cb):
    return lax.dot_general(a, b, (((ca,), (cb,)), ((), ())), preferred_element_type=F32)


def _softmax_rows(qh, kk):
    s = _dot(qh, kk, 1, 1) * (HEAD_DIM ** -0.5)
    p = jnp.exp(s - jnp.max(s, axis=-1, keepdims=True))
    return p / jnp.sum(p, axis=-1, keepdims=True)


def _attn_fwd(q, k, v, *, tq, name):
    s_len, aw = q.shape
    t_len, kvw = k.shape
    gw = Q_GROUP * HEAD_DIM

    def body(q_ref, k_ref, v_ref, o_ref):
        kk, vv = k_ref[...], v_ref[...]
        for g in range(Q_GROUP):
            cols = slice(g * HEAD_DIM, (g + 1) * HEAD_DIM)
            p = _softmax_rows(q_ref[:, cols], kk)
            o_ref[:, cols] = _dot(p.astype(BF16), vv, 1, 0).astype(BF16)

    return pl.pallas_call(
        body, name=name, grid=(kvw // HEAD_DIM, s_len // tq),
        in_specs=[pl.BlockSpec((tq, gw), lambda h, i: (i, h)), pl.BlockSpec((t_len, HEAD_DIM), lambda h, i: (0, h)),
                  pl.BlockSpec((t_len, HEAD_DIM), lambda h, i: (0, h))],
        out_specs=pl.BlockSpec((tq, gw), lambda h, i: (i, h)),
        out_shape=jax.ShapeDtypeStruct((s_len, aw), BF16),
        compiler_params=_params(("parallel", "parallel")),
    )(q, k, v)


def _attn_bwd(q, k, v, do, *, tq, name):
    s_len, aw = q.shape
    t_len, kvw = k.shape
    gw = Q_GROUP * HEAD_DIM

    def body(q_ref, k_ref, v_ref, do_ref, dq_ref, dk_ref, dv_ref):
        @pl.when(pl.program_id(1) == 0)
        def _():
            dk_ref[...] = jnp.zeros_like(dk_ref)
            dv_ref[...] = jnp.zeros_like(dv_ref)

        kk, vv = k_ref[...], v_ref[...]
        for g in range(Q_GROUP):
            cols = slice(g * HEAD_DIM, (g + 1) * HEAD_DIM)
            qh, doh = q_ref[:, cols], do_ref[:, cols]
            p = _softmax_rows(qh, kk)
            dp = _dot(doh, vv, 1, 1)
            ds = p * (dp - jnp.sum(p * dp, axis=-1, keepdims=True)) * (HEAD_DIM ** -0.5)
            dsb = ds.astype(BF16)
            dq_ref[:, cols] = _dot(dsb, kk, 1, 0)
            dk_ref[...] += _dot(dsb, qh, 0, 0)
            dv_ref[...] += _dot(p.astype(BF16), doh, 0, 0)

    qspec = pl.BlockSpec((tq, gw), lambda h, i: (i, h))
    kspec = pl.BlockSpec((t_len, HEAD_DIM), lambda h, i: (0, h))
    return pl.pallas_call(
        body, name=name, grid=(kvw // HEAD_DIM, s_len // tq),
        in_specs=[qspec, kspec, kspec, qspec], out_specs=(qspec, kspec, kspec),
        out_shape=(jax.ShapeDtypeStruct((s_len, aw), F32), jax.ShapeDtypeStruct((t_len, kvw), F32),
                   jax.ShapeDtypeStruct((t_len, kvw), F32)),
        compiler_params=_params(("parallel", "arbitrary")),
    )(q, k, v, do)


SHIFTS = (-2, -1, 1, 2)
SCAN_UNROLL = 4


def _shift_masks(l_len, t_len):
    t = jnp.arange(t_len)[:, None]
    cols = []
    for d in SHIFTS:
        td = t + d
        ok = ((t < l_len) & (td >= 0) & (td < l_len)) | ((t >= l_len) & (td >= l_len) & (td < t_len))
        cols.append(jnp.broadcast_to(ok.astype(F32), (t_len, RNN_BLOCK)))
    return jnp.concatenate(cols, axis=1)


def _shift_rows(xv, d, m_ref, l_len, t_len):
    if d == 0:
        return xv
    k = SHIFTS.index(d)
    return pltpu.roll(xv, (-d) % t_len, 0) * m_ref[:, k * RNN_BLOCK:(k + 1) * RNN_BLOCK]


def _scan_chunks(chains, c0, n, carries):
    def step(kk, hs):
        new = []
        for (a_ref, b_ref, h_ref, rev), hprev in zip(chains, hs):
            c = c0 + (n - 1 - kk if rev else kk)
            r0 = pl.multiple_of(c * SUBLANES, SUBLANES)
            a8 = a_ref[pl.ds(r0, SUBLANES), :]
            b8 = b_ref[pl.ds(r0, SUBLANES), :]
            row = lax.broadcasted_iota(jnp.int32, a8.shape, 0)
            for s in (1, 2, 4):
                if rev:
                    ok = row < SUBLANES - s
                    sh = SUBLANES - s
                else:
                    ok = row >= s
                    sh = s
                a_sh = jnp.where(ok, pltpu.roll(a8, sh, 0), 1.0)
                b_sh = jnp.where(ok, pltpu.roll(b8, sh, 0), 0.0)
                b8 = a8 * b_sh + b8
                a8 = a8 * a_sh
            h8 = b8 + a8 * hprev
            h_ref[pl.ds(r0, SUBLANES), :] = h8
            edge = 0 if rev else SUBLANES - 1
            new.append(jnp.broadcast_to(h8[edge:edge + 1, :], h8.shape))
        return tuple(new)

    return lax.fori_loop(0, n, step, tuple(carries), unroll=SCAN_UNROLL)


def _rglru_conv(x, cw_ref, cb_ref, m_ref, l_len, t_len):
    y = cb_ref[...] + _shift_rows(x, -1, m_ref, l_len, t_len) * cw_ref[0:1, :]
    for kk in range(1, CONV_WIDTH):
        y = y + _shift_rows(x, kk - 1, m_ref, l_len, t_len) * cw_ref[kk:kk + 1, :]
    return y


def _rglru_gates(xc, wr, wi, br, bi, lam):
    xb = xc.astype(BF16)
    r = _sigmoid(_dot(xb, wr.astype(BF16), 1, 0) + br)
    ig = _sigmoid(_dot(xb, wi.astype(BF16), 1, 0) + bi)
    sp = _softplus(-lam)
    log_a = (-LRU_C) * r * sp
    a = jnp.exp(log_a)
    mult = jnp.sqrt(-_expm1(2.0 * log_a))
    u = ig * xc
    return r, ig, sp, a, mult, u


def _rglru_specs(proj, xr_blk, xg_blk, t_len):
    col = lambda off: pl.BlockSpec((t_len, RNN_BLOCK), lambda h, off=off: (0, off + h))
    vec = lambda rows: pl.BlockSpec((rows, RNN_BLOCK), lambda h: (0, h))
    wsp = pl.BlockSpec((2, 1, RNN_BLOCK, RNN_BLOCK), lambda h: (0, h, 0, 0))
    msp = pl.BlockSpec((t_len, len(SHIFTS) * RNN_BLOCK), lambda h: (0, 0))
    return col(xr_blk), col(xg_blk), vec, wsp, msp


def _rglru_fwd(proj, xr_blk, xg_blk, masks, conv_w, conv_b, w_rg, w_ig, b_rg, b_ig, lam, *, l_len, name):
    t_len = proj.shape[0]
    s_len = t_len - l_len
    d_rnn = conv_b.shape[1]
    n_l, n_t = l_len // SUBLANES, t_len // SUBLANES

    def body(xr_ref, xg_ref, m_ref, cw_ref, cb_ref, wr_ref, wi_ref, br_ref, bi_ref, lam_ref, rg_ref, a_s, b_s, h_s):
        xc = _rglru_conv(xr_ref[...], cw_ref, cb_ref, m_ref, l_len, t_len)
        for d in range(2):
            _, _, _, a, mult, u = _rglru_gates(xc, wr_ref[d, 0], wi_ref[d, 0], br_ref[d:d + 1, :],
                                               bi_ref[d:d + 1, :], lam_ref[d:d + 1, :])
            a_s[d] = a
            b_s[d] = mult * u
        chains = [(a_s.at[0], b_s.at[0], h_s.at[0], False), (a_s.at[1], b_s.at[1], h_s.at[1], True)]
        zero = jnp.zeros((SUBLANES, RNN_BLOCK), F32)
        carry = _scan_chunks(chains, 0, n_l, (zero, zero))
        _scan_chunks(chains, n_l, n_t - n_l, carry)
        rnn = h_s[0, pl.ds(l_len, s_len), :] + h_s[1, pl.ds(l_len, s_len), :]
        rg_ref[...] = (rnn * _gelu(xg_ref[pl.ds(l_len, s_len), :])).astype(BF16)

    xr_spec, xg_spec, vec, wsp, msp = _rglru_specs(proj, xr_blk, xg_blk, t_len)
    return pl.pallas_call(
        body, name=name, grid=(d_rnn // RNN_BLOCK,),
        in_specs=[xr_spec, xg_spec, msp, vec(CONV_WIDTH), vec(1), wsp, wsp, vec(2), vec(2), vec(2)],
        out_specs=pl.BlockSpec((s_len, RNN_BLOCK), lambda h: (0, h)),
        out_shape=jax.ShapeDtypeStruct((s_len, d_rnn), BF16),
        scratch_shapes=[pltpu.VMEM((2, t_len, RNN_BLOCK), F32)] * 3,
        compiler_params=_params(("parallel",)),
    )(proj, proj, masks, conv_w, conv_b, w_rg, w_ig, b_rg, b_ig, lam)


N_RNN_VEC = 16


def _rglru_bwd(proj, xr_blk, xg_blk, masks, drg, conv_w, conv_b, w_rg, w_ig, b_rg, b_ig, lam, *, l_len, name):
    t_len = proj.shape[0]
    s_len = t_len - l_len
    d_rnn = conv_b.shape[1]
    n_l, n_t = l_len // SUBLANES, t_len // SUBLANES

    def body(xr_ref, xg_ref, m_ref, drg_ref, cw_ref, cb_ref, wr_ref, wi_ref, br_ref, bi_ref, lam_ref,
             dxr_ref, dxg_ref, dvec_ref, dwr_ref, dwi_ref, a_s, b_s, h_s):
        t = lax.broadcasted_iota(jnp.int32, (t_len, RNN_BLOCK), 0)
        x = xr_ref[...]
        xc = _rglru_conv(x, cw_ref, cb_ref, m_ref, l_len, t_len)
        xb = xc.astype(BF16)
        gates = []
        for d in range(2):
            gt = _rglru_gates(xc, wr_ref[d, 0], wi_ref[d, 0], br_ref[d:d + 1, :], bi_ref[d:d + 1, :],
                              lam_ref[d:d + 1, :])
            gates.append(gt)
            a_s[d] = gt[3]
            b_s[d] = gt[4] * gt[5]
        zero = jnp.zeros((SUBLANES, RNN_BLOCK), F32)
        chains = [(a_s.at[0], b_s.at[0], h_s.at[0], False), (a_s.at[1], b_s.at[1], h_s.at[1], True)]
        carry = _scan_chunks(chains, 0, n_l, (zero, zero))
        _scan_chunks(chains, n_l, n_t - n_l, carry)

        h0, h1 = h_s[0], h_s[1]
        xg = xg_ref[...]
        dr = jnp.concatenate([jnp.zeros((l_len, RNN_BLOCK), F32), drg_ref[...].astype(F32)], axis=0)
        dxg_ref[...] = (dr * (h0 + h1) * _dgelu(xg)).astype(dxg_ref.dtype)
        drnn = dr * _gelu(xg)
        a_s[0] = pltpu.roll(gates[0][3], t_len - 1, 0)
        a_s[1] = pltpu.roll(gates[1][3], 1, 0)
        b_s[0] = drnn
        b_s[1] = drnn
        hprev = [jnp.where(t == 0, 0.0, pltpu.roll(h0, 1, 0)),
                 jnp.where(t == l_len - 1, 0.0, pltpu.roll(h1, t_len - 1, 0))]
        chains = [(a_s.at[0], b_s.at[0], h_s.at[0], True), (a_s.at[1], b_s.at[1], h_s.at[1], False)]
        carry = _scan_chunks(chains, n_l, n_t - n_l, (zero, zero))
        _scan_chunks(chains, 0, n_l, carry)

        dxc = jnp.zeros((t_len, RNN_BLOCK), F32)
        for d in range(2):
            r, ig, sp, a, mult, u = gates[d]
            lamb = h_s[d]
            da = lamb * hprev[d]
            dmult = lamb * u
            du = lamb * mult
            dlog_a = da * a - dmult * (a * a) / mult
            dzr = (dlog_a * ((-LRU_C) * sp)) * r * (1.0 - r)
            dsp = jnp.sum(dlog_a * ((-LRU_C) * r), axis=0, keepdims=True)
            dzi = (du * xc) * ig * (1.0 - ig)
            dzrb, dzib = dzr.astype(BF16), dzi.astype(BF16)
            dxc = dxc + du * ig + _dot(dzrb, wr_ref[d, 0].astype(BF16), 1, 1) + _dot(dzib, wi_ref[d, 0].astype(BF16), 1, 1)
            dwr_ref[d, 0] = _dot(xb, dzrb, 0, 0)
            dwi_ref[d, 0] = _dot(xb, dzib, 0, 0)
            dvec_ref[5 + d:6 + d, :] = jnp.sum(dzr, axis=0, keepdims=True)
            dvec_ref[7 + d:8 + d, :] = jnp.sum(dzi, axis=0, keepdims=True)
            dvec_ref[9 + d:10 + d, :] = dsp * (-_sigmoid(-lam_ref[d:d + 1, :]))
        dvec_ref[4:5, :] = jnp.sum(dxc, axis=0, keepdims=True)
        dvec_ref[11:N_RNN_VEC, :] = jnp.zeros((N_RNN_VEC - 11, RNN_BLOCK), F32)
        dxr = jnp.zeros((t_len, RNN_BLOCK), F32)
        for kk in range(CONV_WIDTH):
            dvec_ref[kk:kk + 1, :] = jnp.sum(dxc * _shift_rows(x, kk - 1, m_ref, l_len, t_len), axis=0, keepdims=True)
            dxr = dxr + _shift_rows(dxc, 1 - kk, m_ref, l_len, t_len) * cw_ref[kk:kk + 1, :]
        dxr_ref[...] = dxr.astype(dxr_ref.dtype)

    xr_spec, xg_spec, vec, wsp, msp = _rglru_specs(proj, xr_blk, xg_blk, t_len)
    col_t = pl.BlockSpec((t_len, RNN_BLOCK), lambda h: (0, h))
    return pl.pallas_call(
        body, name=name, grid=(d_rnn // RNN_BLOCK,),
        in_specs=[xr_spec, xg_spec, msp, pl.BlockSpec((s_len, RNN_BLOCK), lambda h: (0, h)), vec(CONV_WIDTH), vec(1),
                  wsp, wsp, vec(2), vec(2), vec(2)],
        out_specs=(col_t, col_t, vec(N_RNN_VEC), wsp, wsp),
        out_shape=(jax.ShapeDtypeStruct((t_len, d_rnn), BF16), jax.ShapeDtypeStruct((t_len, d_rnn), BF16),
                   jax.ShapeDtypeStruct((N_RNN_VEC, d_rnn), F32), jax.ShapeDtypeStruct(w_rg.shape, F32),
                   jax.ShapeDtypeStruct(w_ig.shape, F32)),
        scratch_shapes=[pltpu.VMEM((2, t_len, RNN_BLOCK), F32)] * 3,
        compiler_params=_params(("parallel",)),
    )(proj, proj, masks, drg, conv_w, conv_b, w_rg, w_ig, b_rg, b_ig, lam)


def _colsum(v):
    return jnp.sum(v, axis=0, keepdims=True)


def _qk_fwd_heads(xv, gain, cos_e, sin_e):
    outs = []
    for h in range(xv.shape[1] // HEAD_DIM):
        xh = xv[:, h * HEAD_DIM:(h + 1) * HEAD_DIM]
        y = xh * _rms(xh) * gain
        outs.append(y * cos_e + _swap_pairs(y) * sin_e)
    return outs[0] if len(outs) == 1 else jnp.concatenate(outs, axis=1)


def _qk_bwd_heads(xv, dov, gain, cos_e, sin_e):
    outs = []
    dgain = jnp.zeros((1, HEAD_DIM), F32)
    for h in range(xv.shape[1] // HEAD_DIM):
        cols = slice(h * HEAD_DIM, (h + 1) * HEAD_DIM)
        xh, doh = xv[:, cols], dov[:, cols]
        r = _rms(xh)
        xhat = xh * r
        dy = doh * cos_e + _swap_pairs(doh * sin_e)
        dgain = dgain + _colsum(dy * xhat)
        outs.append(_rms_bwd(xhat, r, dy * gain))
    return (outs[0] if len(outs) == 1 else jnp.concatenate(outs, axis=1)), dgain


WEIGHTS = ['c_ctx', 'w_mod', 'b_mod', 'g_mix', 'g_mlp', 'w_in', 'q_gain', 'k_gain', 'conv_w', 'conv_b', 'w_rg', 'b_rg',
           'w_ig', 'b_ig', 'lru_lambda', 'w_o_attn', 'w_o_rnn', 'w_out', 'w_up', 'w_down', 'g_final']
REPLICATED = ['c_ctx', 'b_mod', 'g_mix', 'g_mlp', 'q_gain', 'k_gain', 'conv_b', 'w_rg', 'w_ig', 'g_final']
SMALL_SHARDED = [('conv_w', CONV_WIDTH), ('b_rg', 2), ('b_ig', 2), ('lru_lambda', 2)]
SMALL_ROWS = 16


PACK_ROWS = 512


def _pack_flat(arrs, rows_mult):
    flat = jnp.concatenate([a.reshape(-1) for a in arrs])
    n = flat.shape[0]
    unit = rows_mult * LANES
    pad = (-n) % unit
    if pad:
        flat = jnp.concatenate([flat, jnp.zeros((pad,), flat.dtype)])
    return flat.reshape(-1, LANES)


def _unpack_flat(packed, shapes):
    flat = packed.reshape(-1)
    out, pos = [], 0
    for shp in shapes:
        n = math.prod(shp)
        out.append(flat[pos:pos + n].reshape(shp))
        pos += n
    return out


def _pack_small(d):
    rows = [d[n][0] for n, _ in SMALL_SHARDED]
    cols = rows[0].shape[1]
    used = sum(k for _, k in SMALL_SHARDED)
    return jnp.concatenate(rows + [jnp.zeros((SMALL_ROWS - used, cols), F32)], axis=0)


def _unpack_small(p):
    out, pos = {}, 0
    for n, k in SMALL_SHARDED:
        out[n] = p[pos:pos + k][None]
        pos += k
    return out


def kernel(x, c, ctx, c_ctx, w_mod, b_mod, g_mix, g_mlp, w_in, q_gain, k_gain, conv_w, conv_b, w_rg, b_rg, w_ig, b_ig, lru_lambda, w_o_attn, w_o_rnn, w_out, w_up, w_down, g_final, loss_target, m_c_ctx, m_w_mod, m_b_mod, m_g_mix, m_g_mlp, m_w_in, m_q_gain, m_k_gain, m_conv_w, m_conv_b, m_w_rg, m_b_rg, m_w_ig, m_b_ig, m_lru_lambda, m_w_o_attn, m_w_o_rnn, m_w_out, m_w_up, m_w_down, m_g_final, v_c_ctx, v_w_mod, v_b_mod, v_g_mix, v_g_mlp, v_w_in, v_q_gain, v_k_gain, v_conv_w, v_conv_b, v_w_rg, v_b_rg, v_w_ig, v_b_ig, v_lru_lambda, v_w_o_attn, v_w_o_rnn, v_w_out, v_w_up, v_w_down, v_g_final):
    wd = dict(c_ctx=c_ctx, w_mod=w_mod, b_mod=b_mod, g_mix=g_mix, g_mlp=g_mlp, w_in=w_in, q_gain=q_gain, k_gain=k_gain,
              conv_w=conv_w, conv_b=conv_b, w_rg=w_rg, b_rg=b_rg, w_ig=w_ig, b_ig=b_ig, lru_lambda=lru_lambda,
              w_o_attn=w_o_attn, w_o_rnn=w_o_rnn, w_out=w_out, w_up=w_up, w_down=w_down, g_final=g_final)
    md = dict(zip(WEIGHTS, (m_c_ctx, m_w_mod, m_b_mod, m_g_mix, m_g_mlp, m_w_in, m_q_gain, m_k_gain, m_conv_w, m_conv_b,
                            m_w_rg, m_b_rg, m_w_ig, m_b_ig, m_lru_lambda, m_w_o_attn, m_w_o_rnn, m_w_out, m_w_up,
                            m_w_down, m_g_final)))
    vd = dict(zip(WEIGHTS, (v_c_ctx, v_w_mod, v_b_mod, v_g_mix, v_g_mlp, v_w_in, v_q_gain, v_k_gain, v_conv_w, v_conv_b,
                            v_w_rg, v_b_rg, v_w_ig, v_b_ig, v_lru_lambda, v_w_o_attn, v_w_o_rnn, v_w_out, v_w_up,
                            v_w_down, v_g_final)))

    xs, tgt, ctxs = x[0], loss_target[0], ctx[0]
    s_len, d = xs.shape
    l_len = ctxs.shape[0]
    t_len = s_len + l_len
    n_in = w_in.shape[2] * N_DEV
    aw = (n_in - 4 * d) * 2 // 3
    kvw = aw // Q_GROUP
    assert aw == d and n_in == aw + 2 * kvw + 4 * d
    nmod = N_MOD * d // N_DEV
    tr = 128
    assert l_len % tr == 0 and s_len % tr == 0
    lb = l_len // tr
    tq = min(256, s_len)
    my = _lin(_my_pos())
    c_ctx2 = c_ctx[None]
    g_final2 = g_final[None]

    c_all = _all_gather([jnp.broadcast_to(c, (SUBLANES, d))], [0], "ag_cond")[0][::SUBLANES]
    c16 = jnp.concatenate([c_all, jnp.broadcast_to(c_ctx2, (N_DEV, d))], axis=0)
    b_mod_sh = lax.dynamic_slice(b_mod, (0, my * nmod), (1, nmod))
    mod_part = _mm(c16, w_mod[0], a_fn=_silu, bias=b_mod_sh, name="mod_fwd")
    (mod_g,), order = _all_gather([mod_part], [1], "ag_mod", with_token=True)
    mod_lat = lax.dynamic_slice(mod_g, (my, 0), (1, N_MOD * d))
    sh_a, sc_a, ga_a, sh_f, sc_f, ga_f = [mod_lat[:, k * d:(k + 1) * d] for k in range(N_MOD)]
    csh_a, csc_a = mod_g[N_DEV:N_DEV + 1, :d], mod_g[N_DEV:N_DEV + 1, d:2 * d]

    sp_w = _pack_small(wd)
    (w_in_f, sp_f), order = _all_gather([_tok(w_in[0], order).astype(BF16), sp_w], [1, 1], "ag_w_in", with_token=True)
    conv_w_f, b_rg_f, b_ig_f, lam_f = sp_f[0:4], sp_f[4:6], sp_f[6:8], sp_f[8:10]
    late_axis = dict(w_o_attn=0, w_o_rnn=0, w_out=0, w_up=1, w_down=0)
    placed = [_place_block(wd[n][0], ax, my, "place_" + n) for n, ax in late_axis.items()]
    late = {}
    for (n, ax), full in zip(late_axis.items(), placed):
        late[n] = _xfer_start("gather", _tok(wd[n][0], order).astype(BF16), full, ax, "ag_start_" + n)
        order = late[n][4]
    g_mix_fwd = _tok(g_mix, order)
    late_wait = lambda n, after: _xfer_wait("gather", late[n], late_axis[n], after, "ag_wait_" + n)

    pos = jnp.arange(s_len)
    n_freq = HEAD_DIM // 4
    inv_freq = ROPE_THETA ** (-jnp.arange(n_freq, dtype=F32) / n_freq)
    ang = jnp.concatenate([(pos // GRID_W).astype(F32)[:, None] * inv_freq,
                           (pos % GRID_W).astype(F32)[:, None] * inv_freq], axis=-1)
    sign = jnp.where(jnp.arange(HEAD_DIM) % 2 == 0, -1.0, 1.0).astype(F32)
    cos_e = jnp.concatenate([jnp.ones((l_len, HEAD_DIM), F32), jnp.repeat(jnp.cos(ang), 2, axis=-1)], axis=0)
    sin_e = jnp.concatenate([jnp.zeros((l_len, HEAD_DIM), F32), jnp.repeat(jnp.sin(ang), 2, axis=-1) * sign], axis=0)

    x_all = jnp.concatenate([ctxs, xs], axis=0)

    def nm1_fwd(i, j, xv, g, csh, csc, sh, sc):
        is_ctx = i < lb
        n = xv * _rms(xv) * g
        return n * (1.0 + jnp.where(is_ctx, csc, sc)) + jnp.where(is_ctx, csh, sh)

    h_all = _rowwise(nm1_fwd, [_row(x_all), _bc(g_mix_fwd), _bc(csh_a), _bc(csc_a), _bc(sh_a), _bc(sc_a)],
                     [("row", BF16)], nr=t_len // tr, tr=tr, nc=1, cw=d, name="norm1_fwd")[0]
    proj = _mm(h_all, w_in_f, name="proj_fwd")

    kb = aw // kvw
    q = _rowwise(lambda i, j, xv, gain, ce, se: _qk_fwd_heads(xv, gain, ce, se),
                 [_row(proj, 0, lb, aw), _full(q_gain), _row(cos_e, 0, lb, HEAD_DIM), _row(sin_e, 0, lb, HEAD_DIM)],
                 [("row", BF16, aw)], nr=s_len // tr, tr=tr, nc=1, cw=aw, name="q_fwd")[0]
    k_all, v_all = _rowwise(lambda i, j, kv_, vv_, gain, ce, se: (_qk_fwd_heads(kv_, gain, ce, se), vv_),
                            [_row(proj, kb, 0, kvw), _row(proj, kb + 1, 0, kvw), _full(k_gain),
                             _row(cos_e, 0, 0, HEAD_DIM), _row(sin_e, 0, 0, HEAD_DIM)],
                            [("row", BF16, kvw), ("row", BF16, kvw)], nr=t_len // tr, tr=tr, nc=1, cw=kvw, name="kv_fwd")
    attn_o = _attn_fwd(q, k_all, v_all, tq=tq, name="attn_fwd")

    xr_blk = (aw + 2 * kvw) // RNN_BLOCK
    xg_blk = xr_blk + d // RNN_BLOCK
    shift_masks = _shift_masks(l_len, t_len)
    rg = _rglru_fwd(proj, xr_blk, xg_blk, shift_masks, conv_w_f, conv_b, w_rg[0], w_ig[0], b_rg_f, b_ig_f, lam_f, l_len=l_len,
                    name="rglru_fwd")

    w_oa_f = late_wait("w_o_attn", attn_o)
    y_attn = _mm(attn_o, w_oa_f, name="o_attn_fwd")
    w_or_f = late_wait("w_o_rnn", rg)
    y_rnn = _mm(rg, w_or_f, name="o_rnn_fwd")
    hw = d // 2
    glb = (aw + 2 * kvw + 2 * d) // hw

    def merge_fwd(i, j, ya, yr, gla, glr):
        return _sigmoid(gla) * ya + _sigmoid(glr) * yr

    u = _rowwise(merge_fwd, [_row(y_attn), _row(y_rnn), _row(proj, glb, lb), _row(proj, glb + 2, lb)],
                 [("row", BF16)], nr=s_len // tr, tr=tr, nc=2, cw=hw, name="merge_fwd")[0]
    w_out_f = late_wait("w_out", u)
    mix = _mm(u, w_out_f, name="out_fwd")

    def res1_fwd(i, j, xv, mixv, gaa, gm, shf, scf):
        x1v = xv + gaa * mixv
        n = x1v * _rms(x1v) * gm
        return x1v, n * (1.0 + scf) + shf

    x1, h2 = _rowwise(res1_fwd, [_row(xs), _row(mix), _bc(ga_a), _bc(g_mlp), _bc(sh_f), _bc(sc_f)],
                      [("row", F32), ("row", BF16)], nr=s_len // tr, tr=tr, nc=1, cw=d, name="norm2_fwd")
    relu2 = lambda z: jnp.square(jnp.maximum(z, 0.0))
    w_up_f = late_wait("w_up", h2)
    up = _mm(h2, w_up_f, name="up_fwd")
    w_down_f = late_wait("w_down", up)
    down = _mm(up, w_down_f, a_fn=relu2, name="down_fwd")

    def final_fn(i, j, x1v, dn, tg, gaf, gf):
        x2 = x1v + gaf * dn
        r = _rms(x2)
        xh = x2 * r
        e = xh * gf - tg
        dy = e * (1.0 / d)
        dx2 = _rms_bwd(xh, r, dy * gf)
        return dx2, dx2 * gaf, _colsum(e * e) * (0.5 / d), _colsum(dy * xh), _colsum(dx2 * dn)

    dx2, ddown, loss_cols, g_gfinal, d_ga_f = _rowwise(
        final_fn, [_row(x1), _row(down), _row(tgt), _bc(ga_f), _bc(g_final2)],
        [("row", F32), ("row", BF16), ("acc", 1), ("acc", 1), ("acc", 1)], nr=s_len // tr, tr=tr, nc=1, cw=d,
        name="final_bwd")
    loss = lax.psum(jnp.sum(loss_cols), ("x", "y", "c"))

    dup = _mm(ddown, w_down_f, tb=True, out_fn=lambda r, upv: r * (2.0 * jnp.maximum(upv, 0.0)), extras=[up],
              out_dtype=BF16, name="down_bwd_x")
    gw_down = _mm(up, ddown, ta=True, a_fn=relu2, out_dtype=BF16, name="down_bwd_w")
    sent = dict(w_down=_scatter_start(gw_down, 0, my, "rs_start_w_down"))
    dh2 = _mm(dup, w_up_f, tb=True, name="up_bwd_x")
    gw_up = _mm(h2, dup, ta=True, out_dtype=BF16, name="up_bwd_w")
    sent["w_up"] = _scatter_start(gw_up, 1, my, "rs_start_w_up")
    sc_f_bwd = _tok(sc_f, sent["w_down"][0][4], sent["w_up"][0][4])

    def norm2_bwd(i, j, x1v, mixv, dh2v, dx2v, gaa, gm, scf):
        r = _rms(x1v)
        xh = x1v * r
        n = xh * gm
        dn = dh2v * (1.0 + scf)
        dx1 = dx2v + _rms_bwd(xh, r, dn * gm)
        return dx1, dx1 * gaa, _colsum(dh2v), _colsum(dh2v * n), _colsum(dn * xh), _colsum(dx1 * mixv)

    dx1, dmix, d_sh_f, d_sc_f, g_gmlp, d_ga_a = _rowwise(
        norm2_bwd, [_row(x1), _row(mix), _row(dh2), _row(dx2), _bc(ga_a), _bc(g_mlp), _bc(sc_f_bwd)],
        [("row", F32), ("row", BF16), ("acc", 1), ("acc", 1), ("acc", 1), ("acc", 1)], nr=s_len // tr, tr=tr, nc=1,
        cw=d, name="norm2_bwd")

    du = _mm(dmix, w_out_f, tb=True, name="out_bwd_x")
    gw_out = _mm(u, dmix, ta=True, out_dtype=BF16, name="out_bwd_w")
    sent["w_out"] = _scatter_start(gw_out, 0, my, "rs_start_w_out")

    def merge_bwd(i, j, duv, ya, yr, gla, glr):
        ga, gr = _sigmoid(gla), _sigmoid(glr)
        return duv * ga, duv * gr, duv * ya * ga * (1.0 - ga), duv * yr * gr * (1.0 - gr)

    dya, dyr, dgla, dglr = _rowwise(
        merge_bwd, [_row(du), _row(y_attn), _row(y_rnn), _row(proj, glb, lb), _row(proj, glb + 2, lb)],
        [("row", BF16)] * 4, nr=s_len // tr, tr=tr, nc=2, cw=hw, name="merge_bwd")
    dattn = _mm(dya, w_oa_f, tb=True, out_dtype=BF16, name="o_attn_bwd_x")
    gw_oa = _mm(attn_o, dya, ta=True, out_dtype=BF16, name="o_attn_bwd_w")
    drg = _mm(dyr, w_or_f, tb=True, out_dtype=BF16, name="o_rnn_bwd_x")
    gw_or = _mm(rg, dyr, ta=True, out_dtype=BF16, name="o_rnn_bwd_w")
    sent["w_o_attn"] = _scatter_start(gw_oa, 0, my, "rs_start_w_o_attn")
    sent["w_o_rnn"] = _scatter_start(gw_or, 0, my, "rs_start_w_o_rnn")
    conv_b_bwd = _tok(conv_b, sent["w_out"][0][4], sent["w_o_attn"][0][4], sent["w_o_rnn"][0][4])

    dq, dk, dv = _attn_bwd(q, k_all, v_all, dattn, tq=tq, name="attn_bwd")
    dxr, dxg, dvec, g_wrg, g_wig = _rglru_bwd(proj, xr_blk, xg_blk, shift_masks, drg, conv_w_f, conv_b_bwd, w_rg[0], w_ig[0], b_rg_f,
                                               b_ig_f, lam_f, l_len=l_len, name="rglru_bwd")
    gate_part = _pack_flat([g_wrg, g_wig], N_DEV * SUBLANES)
    gate_sent = _scatter_start(gate_part, 0, my, "rs_start_gates")
    q_gain_bwd = _tok(q_gain, gate_sent[0][4])

    dq_p, g_qgain = _rowwise(lambda i, j, xv, dov, gain, ce, se: _qk_bwd_heads(xv, dov, gain, ce, se),
                             [_row(proj, 0, lb, aw), _row(dq, 0, 0, aw), _full(q_gain_bwd), _row(cos_e, 0, lb, HEAD_DIM),
                              _row(sin_e, 0, lb, HEAD_DIM)],
                             [("row", BF16, aw), ("acc", 1, HEAD_DIM)], nr=s_len // tr, tr=tr, nc=1, cw=aw, name="q_bwd")

    def kv_bwd(i, j, xv, dkv, dvv, gain, ce, se):
        dk_p, dgain = _qk_bwd_heads(xv, dkv, gain, ce, se)
        return dk_p, dvv, dgain

    dk_p, dv_p, g_kgain = _rowwise(kv_bwd, [_row(proj, kb, 0, kvw), _row(dk, 0, 0, kvw), _row(dv, 0, 0, kvw),
                                            _full(k_gain), _row(cos_e, 0, 0, HEAD_DIM), _row(sin_e, 0, 0, HEAD_DIM)],
                                   [("row", BF16, kvw), ("row", BF16, kvw), ("acc", 1, HEAD_DIM)], nr=t_len // tr, tr=tr,
                                   nc=1, cw=kvw, name="kv_bwd")

    pad_ctx = lambda a: jnp.pad(a, ((l_len, 0), (0, 0)))
    dproj = jnp.concatenate([pad_ctx(dq_p), dk_p, dv_p, dxr, dxg, pad_ctx(dgla), pad_ctx(dglr)], axis=1)
    dh_all = _mm(dproj, w_in_f, tb=True, name="proj_bwd_x")
    g_mix_bwd = g_mix

    def norm1_bwd(i, j, xv, dhv, dresv, g, csc, sc):
        is_ctx = i < lb
        r = _rms(xv)
        xh = xv * r
        dn = dhv * (1.0 + jnp.where(is_ctx, csc, sc))
        dsh, dsc = _colsum(dhv), _colsum(dhv * (xh * g))
        zero = jnp.zeros_like(dsh)
        return (dresv + _rms_bwd(xh, r, dn * g), jnp.where(is_ctx, dsh, zero), jnp.where(is_ctx, dsc, zero),
                jnp.where(is_ctx, zero, dsh), jnp.where(is_ctx, zero, dsc), _colsum(dn * xh))

    dx_all, d_csh, d_csc, d_sh_a, d_sc_a, g_gmix = _rowwise(
        norm1_bwd, [_row(x_all), _row(dh_all), _row(pad_ctx(dx1)), _bc(g_mix_bwd), _bc(csc_a), _bc(sc_a)],
        [("row", F32)] + [("acc", 1)] * 5, nr=t_len // tr, tr=tr, nc=1, cw=d, name="norm1_bwd")
    grad_x = dx_all[l_len:][None]

    dmod_lat = jnp.concatenate([d_sh_a, d_sc_a, d_ga_a, d_sh_f, d_sc_f, d_ga_f], axis=1)
    dmod_ctx = jnp.concatenate([d_csh, d_csc, jnp.zeros((1, (N_MOD - 2) * d), F32)], axis=1)
    dm_g = _all_gather([jnp.concatenate([dmod_lat, dmod_ctx, jnp.zeros((SUBLANES - 2, N_MOD * d), F32)], axis=0)], [0],
                       "ag_dmod")[0]
    dm16 = jnp.concatenate([dm_g[0::SUBLANES], dm_g[1::SUBLANES]], axis=0)
    dm_sh = lax.dynamic_slice(dm16, (0, my * nmod), (2 * N_DEV, nmod))
    gw_mod = _mm(c16, dm_sh, ta=True, a_fn=_silu, name="mod_bwd_w")
    dc16 = _mm(dm_sh, w_mod[0], tb=True, name="mod_bwd_x")

    def cctx_fn(i, j, pv, cc, dl, dc):
        row = lax.broadcasted_iota(jnp.int32, pv.shape, 0)
        return _colsum(jnp.where(row >= N_DEV, pv, 0.0)) * _dsilu(cc), dl + dc

    g_cctx, g_bmod = _rowwise(cctx_fn, [_full(dc16), _full(c_ctx2), _full(dmod_lat), _full(dmod_ctx)],
                              [("row", F32, d), ("row", F32, N_MOD * d)], nr=1, tr=1, nc=1, cw=d, name="cctx_bwd")

    rep_part = dict(c_ctx=g_cctx, b_mod=g_bmod, g_mix=g_gmix, g_mlp=g_gmlp, q_gain=g_qgain, k_gain=g_kgain,
                    conv_b=dvec[4:5], g_final=g_gfinal)
    rep_small = [n for n in REPLICATED if n in rep_part]
    gates = [n for n in REPLICATED if n not in rep_part]
    part_pack = _pack_flat([rep_part[n] for n in rep_small], PACK_ROWS)
    rep_rows = part_pack.shape[0]
    gate_land = _xfer_wait("scatter", gate_sent[0], 0, g_bmod, "rs_wait_gates")
    gate_sum = _sum_parts(gate_sent[1], gate_land, name="gate_sum")
    (gate_full, parts), order = _all_gather([gate_sum, part_pack], [0, 0], "ag_rep_grads", with_token=True)
    small_g = jnp.concatenate([dvec[0:4], dvec[5:11], jnp.zeros((SMALL_ROWS - 10, d), F32)], axis=0)
    (small_recv,), order = _all_to_all([_tok(small_g, order)], [1], "a2a_small_grads")

    gw_in = _mm(h_all, dproj, ta=True, out_dtype=BF16, name="proj_bwd_w", bias=_tok(jnp.zeros((1, n_in), F32), order))
    sent["w_in"] = _chip_scatter_start(_pair_reduce(gw_in, my, "pair_w_in"), my, "rs_start_w_in")
    started = sent["w_in"][0][4]

    res = {}
    for names, pr, mult, nm in ((rep_small, parts.reshape(N_DEV, rep_rows, LANES), PACK_ROWS, "adamw_rep"),
                                (gates, gate_full[None], N_DEV * SUBLANES, "adamw_gates")):
        rep_out = _adamw(pr, _tok(_pack_flat([wd[n] for n in names], mult), started),
                         _pack_flat([md[n] for n in names], mult), _pack_flat([vd[n] for n in names], mult), name=nm)
        for kind, packed in zip(("g", "d", "m", "v"), rep_out):
            for n, a in zip(names, _unpack_flat(packed, [wd[n].shape for n in names])):
                res[kind, n] = a
    out = _adamw(small_recv, _tok(sp_w, started), _pack_small(md), _pack_small(vd), name="adamw_small")
    for kind, a in zip(("g", "d", "m", "v"), out):
        for n, piece in _unpack_small(a).items():
            res[kind, n] = piece

    after = started
    for n, (handle, own, ax, kind) in sent.items():
        land = _xfer_wait(kind, handle, ax, after, "rs_wait_" + n)
        out = _adamw(land, wd[n][0], md[n][0], vd[n][0], own=own, name="adamw_" + n)
        after = out[0]
        for kind, a in zip(("g", "d", "m", "v"), out):
            res[kind, n] = a[None]
        if n == "w_o_rnn":
            out = _adamw(gw_mod[None], w_mod[0], m_w_mod[0], v_w_mod[0], name="adamw_w_mod")
            after = out[0]
            for kind, a in zip(("g", "d", "m", "v"), out):
                res[kind, 'w_mod'] = a[None]

    outs = [loss, grad_x]
    for kind in ("g", "d", "m", "v"):
        outs += [res[kind, n].reshape(wd[n].shape) for n in WEIGHTS]
    return tuple(outs)
```

```python
import functools
import math

import jax
import jax.numpy as jnp
from jax import lax
from jax.experimental import pallas as pl
from jax.experimental.pallas import tpu as pltpu

F32 = jnp.float32
BF16 = jnp.bfloat16
MESH = pl.DeviceIdType.MESH

N_DEV = 8
HEAD_DIM = 128
Q_GROUP = 4
RNN_BLOCK = 128
GRID_W = 64
CONV_WIDTH = 4
LRU_C = 8.0
ROPE_THETA = 10000.0
NORM_EPS = 1e-6
N_MOD = 6
ADAM_LR = 0.001
ADAM_B1 = 0.9
ADAM_B2 = 0.999
ADAM_EPS = 1e-08
ADAM_WD = 0.01
ADAM_STEP = 10

LANES = 128
SUBLANES = 8
VMEM_LIMIT = 56 * 1024 * 1024
HBM_SPEC = pl.BlockSpec(memory_space=pltpu.HBM)


def _params(sem=None, vmem=VMEM_LIMIT):
    kw = dict(vmem_limit_bytes=vmem)
    if sem is not None:
        kw["dimension_semantics"] = sem
    return pltpu.CompilerParams(**kw)


def _pick(dim, target, mult):
    for t in range(min(dim, target), 0, -1):
        if dim % t == 0 and t % mult == 0:
            return t
    return dim


def _my_pos():
    return lax.axis_index("x"), lax.axis_index("y"), lax.axis_index("c")


def _flip(pos, d):
    x, y, c = pos
    return (1 - x if d & 4 else x, 1 - y if d & 2 else y, 1 - c if d & 1 else c)


def _lin(pos):
    return 4 * pos[0] + 2 * pos[1] + pos[2]


def _slab(ref, axis, idx, size):
    if axis == 0:
        return ref.at[pl.ds(idx * size, size)]
    assert axis == 1
    return ref.at[:, pl.ds(idx * size, size)]


def _all_gather(arrs, axes, name, with_token=False):
    n = len(arrs)
    sizes = [a.shape[ax] for a, ax in zip(arrs, axes)]

    def body(*refs):
        ins, outs, token = refs[:n], refs[n:2 * n], refs[2 * n]
        send_sems, recv_sems, local_sems = refs[2 * n + 1:]
        token[...] = jnp.zeros_like(token)
        me = _my_pos()
        sibling = _flip(me, 1)
        chips = [2, 4, 6]

        def copy(a, k, block_pos, to, src=None):
            dst = _slab(outs[a], axes[a], _lin(block_pos), sizes[a])
            return pltpu.make_async_remote_copy(
                src_ref=dst if src is None else src, dst_ref=dst,
                send_sem=send_sems.at[7 * a + k], recv_sem=recv_sems.at[7 * a + k],
                device_id=to, device_id_type=MESH)

        local, first, passed = [], [], []
        for a in range(n):
            cp = pltpu.make_async_copy(ins[a], _slab(outs[a], axes[a], _lin(me), sizes[a]), local_sems.at[a])
            cp.start()
            local.append(cp)
        for a in range(n):
            f = [copy(a, 0, me, sibling, src=ins[a])]
            f += [copy(a, 1 + j, me, _flip(me, m), src=ins[a]) for j, m in enumerate(chips)]
            for cp in f:
                cp.start()
            first += f
        for a in range(n):
            for j, m in enumerate(chips):
                copy(a, 1 + j, _flip(me, m), me).wait_recv()
                cp = copy(a, 4 + j, _flip(me, m), sibling)
                cp.start()
                passed.append(cp)
        for a in range(n):
            copy(a, 0, sibling, me).wait_recv()
            for j, m in enumerate(chips):
                copy(a, 4 + j, _flip(sibling, m), me).wait_recv()
        for cp in first + passed:
            cp.wait_send()
        for cp in local:
            cp.wait()

    out_shape = []
    for a, ax in zip(arrs, axes):
        shp = list(a.shape)
        shp[ax] *= N_DEV
        out_shape.append(jax.ShapeDtypeStruct(tuple(shp), a.dtype))
    out_shape.append(jax.ShapeDtypeStruct((SUBLANES, LANES), F32))
    res = pl.pallas_call(
        body, name=name, out_shape=tuple(out_shape),
        in_specs=[HBM_SPEC] * n, out_specs=tuple([HBM_SPEC] * n + [pl.BlockSpec(memory_space=pltpu.VMEM)]),
        scratch_shapes=[pltpu.SemaphoreType.DMA((7 * n,)), pltpu.SemaphoreType.DMA((7 * n,)),
                        pltpu.SemaphoreType.DMA((n,))],
    )(*arrs)
    return (list(res[:n]), res[n]) if with_token else list(res[:n])


def _all_to_all(arrs, axes, name):
    n = len(arrs)
    sizes = [a.shape[ax] // N_DEV for a, ax in zip(arrs, axes)]

    def body(*refs):
        ins, outs, token = refs[:n], refs[n:2 * n], refs[2 * n]
        send_sems, recv_sems, local_sems = refs[2 * n + 1:]
        token[...] = jnp.zeros_like(token)
        me = _my_pos()
        my = _lin(me)
        copies, local = [], []
        for a in range(n):
            cp = pltpu.make_async_copy(_slab(ins[a], axes[a], my, sizes[a]), outs[a].at[my], local_sems.at[a])
            cp.start()
            local.append(cp)
        for d in range(1, N_DEV):
            peer = _flip(me, d)
            for a in range(n):
                cp = pltpu.make_async_remote_copy(
                    src_ref=_slab(ins[a], axes[a], _lin(peer), sizes[a]), dst_ref=outs[a].at[my],
                    send_sem=send_sems.at[7 * a + d - 1], recv_sem=recv_sems.at[7 * a + d - 1],
                    device_id=peer, device_id_type=MESH)
                cp.start()
                copies.append((a, d, cp))
        for a, d, cp in copies:
            peer = _flip(me, d)
            pltpu.make_async_remote_copy(
                src_ref=_slab(ins[a], axes[a], _lin(peer), sizes[a]), dst_ref=outs[a].at[_lin(peer)],
                send_sem=send_sems.at[7 * a + d - 1], recv_sem=recv_sems.at[7 * a + d - 1],
                device_id=peer, device_id_type=MESH).wait_recv()
            cp.wait_send()
        for cp in local:
            cp.wait()

    out_shape = []
    for a, ax, s in zip(arrs, axes, sizes):
        shp = list(a.shape)
        shp[ax] = s
        out_shape.append(jax.ShapeDtypeStruct((N_DEV,) + tuple(shp), a.dtype))
    out_shape.append(jax.ShapeDtypeStruct((SUBLANES, LANES), F32))
    res = pl.pallas_call(
        body, name=name, out_shape=tuple(out_shape),
        in_specs=[HBM_SPEC] * n, out_specs=tuple([HBM_SPEC] * n + [pl.BlockSpec(memory_space=pltpu.VMEM)]),
        scratch_shapes=[pltpu.SemaphoreType.DMA((7 * n,)), pltpu.SemaphoreType.DMA((7 * n,)),
                        pltpu.SemaphoreType.DMA((n,))],
    )(*arrs)
    return list(res[:n]), res[n]


SEM_SPEC = pl.BlockSpec(memory_space=pltpu.SEMAPHORE)
_EFFECT = pltpu.SideEffectType.DATAFLOW_SIDE_EFFECTING
N_PEERS = N_DEV - 1


def _split_copies(kind, src_ref, land_ref, send_sems, recv_sems, axis, size, arriving):
    me = _my_pos()
    out = []
    for dd in (range(2, N_DEV, 2) if kind in ("chips", "gather_chips") else range(1, N_DEV)):
        peer = _flip(me, dd)
        k = dd - 1
        if kind in ("gather", "gather_chips"):
            src, dst = src_ref, _slab(land_ref, axis, _lin(peer if arriving else me), size)
        elif kind == "scatter":
            src, dst = _slab(src_ref, axis, _lin(peer), size), land_ref.at[k]
        else:
            k = dd // 2 - 1
            src, dst = _slab(src_ref, axis, 2 * peer[0] + peer[1], size), land_ref.at[k]
        out.append(pltpu.make_async_remote_copy(
            src_ref=src, dst_ref=dst, send_sem=send_sems.at[k], recv_sem=recv_sems.at[k],
            device_id=peer, device_id_type=MESH))
    return out


N_CHIPS = N_DEV // 2


def _split_size(kind, src_shape, axis):
    return src_shape[axis] // {"gather": 1, "gather_chips": 1, "scatter": N_DEV, "chips": N_CHIPS}[kind]


def _sibling_swap(full, axis, name):
    size = full.shape[axis] // N_DEV

    def body(in_ref, o_ref, send_sems, recv_sems):
        me = _my_pos()
        sibling = _flip(me, 1)
        cps = []
        for k in range(N_CHIPS):
            mine = _slab(o_ref, axis, 2 * k + me[2], size)
            cp = pltpu.make_async_remote_copy(src_ref=mine, dst_ref=mine, send_sem=send_sems.at[k],
                                              recv_sem=recv_sems.at[k], device_id=sibling, device_id_type=MESH)
            cp.start()
            cps.append(cp)
        for k, cp in enumerate(cps):
            cp.wait_send()
            theirs = _slab(o_ref, axis, 2 * k + sibling[2], size)
            pltpu.make_async_remote_copy(src_ref=theirs, dst_ref=theirs, send_sem=send_sems.at[k],
                                         recv_sem=recv_sems.at[k], device_id=sibling, device_id_type=MESH).wait_recv()

    return pl.pallas_call(
        body, name=name, out_shape=jax.ShapeDtypeStruct(full.shape, full.dtype), in_specs=[HBM_SPEC], out_specs=HBM_SPEC,
        scratch_shapes=[pltpu.SemaphoreType.DMA((N_CHIPS,)), pltpu.SemaphoreType.DMA((N_CHIPS,))],
        input_output_aliases={0: 0},
    )(full)


def _xfer_start(kind, src, land, axis, name):
    size = _split_size(kind, src.shape, axis)

    def body(src_ref, land_ref, send_sems, recv_sems, src_thru, land_thru, token):
        for cp in _split_copies(kind, src_ref, land_ref, send_sems, recv_sems, axis, size, False):
            cp.start()
        token[...] = jnp.zeros_like(token)

    return pl.pallas_call(
        body, name=name,
        out_shape=(pltpu.SemaphoreType.DMA((N_PEERS,)), pltpu.SemaphoreType.DMA((N_PEERS,)),
                   pltpu.HBM(src.shape, src.dtype), pltpu.HBM(land.shape, land.dtype),
                   jax.ShapeDtypeStruct((SUBLANES, LANES), F32)),
        in_specs=(HBM_SPEC, HBM_SPEC),
        out_specs=(SEM_SPEC, SEM_SPEC, HBM_SPEC, HBM_SPEC, pl.BlockSpec(memory_space=pltpu.VMEM)),
        input_output_aliases={0: 2, 1: 3},
        compiler_params=pltpu.CompilerParams(has_side_effects=_EFFECT),
    )(pltpu.with_memory_space_constraint(src, pltpu.HBM), pltpu.with_memory_space_constraint(land, pltpu.HBM))


def _xfer_wait(kind, handle, axis, after, name):
    send_sems, recv_sems, src_thru, land_thru, _ = handle
    size = _split_size(kind, src_thru.shape, axis)

    def body(src_ref, land_ref, send_sems, recv_sems, after_ref, src_dead, got_ref):
        for cp in _split_copies(kind, src_ref, land_ref, send_sems, recv_sems, axis, size, True):
            cp.wait_send()
            cp.wait_recv()

    return pl.pallas_call(
        body, name=name,
        out_shape=(pltpu.HBM(src_thru.shape, src_thru.dtype), pltpu.HBM(land_thru.shape, land_thru.dtype)),
        in_specs=(HBM_SPEC, HBM_SPEC, SEM_SPEC, SEM_SPEC, pl.BlockSpec(memory_space=pl.ANY)),
        out_specs=(HBM_SPEC, HBM_SPEC), input_output_aliases={0: 0, 1: 1},
        compiler_params=pltpu.CompilerParams(has_side_effects=_EFFECT),
    )(src_thru, land_thru, send_sems, recv_sems, after)[1]


def _place_block(shard, axis, my, name):
    r, c = shard.shape
    tr = _pick(r, max(16, (1 << 19) // c), 16)
    nr = r // tr
    if axis == 0:
        full, out_map = (r * N_DEV, c), (lambda i, my_ref: (my_ref[0] * nr + i, 0))
    else:
        full, out_map = (r, c * N_DEV), (lambda i, my_ref: (i, my_ref[0]))

    def body(my_ref, x_ref, o_ref):
        o_ref[...] = x_ref[...].astype(BF16)

    return pl.pallas_call(
        body, name=name, out_shape=jax.ShapeDtypeStruct(full, BF16),
        grid_spec=pltpu.PrefetchScalarGridSpec(
            num_scalar_prefetch=1, grid=(nr,), in_specs=[pl.BlockSpec((tr, c), lambda i, my_ref: (i, 0))],
            out_specs=pl.BlockSpec((tr, c), out_map)),
        compiler_params=_params(("arbitrary",)),
    )(jnp.reshape(my, (1,)).astype(jnp.int32), shard)


def _scatter_start(grad, axis, my, name):
    shp = list(grad.shape)
    shp[axis] //= N_DEV
    own = lax.dynamic_slice_in_dim(grad, my * shp[axis], shp[axis], axis)
    return (_xfer_start("scatter", grad, lax.empty((N_PEERS,) + tuple(shp), grad.dtype), axis, name), own, axis,
            "scatter")


def _pair_reduce(grad, my, name):
    r, cols = grad.shape
    sz = cols // N_DEV

    def exchange(g_ref, got_ref, send_sems, recv_sems):
        me = _my_pos()
        sibling = _flip(me, 1)
        cps = []
        for k in range(N_CHIPS):
            cp = pltpu.make_async_remote_copy(
                src_ref=_slab(g_ref, 1, 2 * k + sibling[2], sz), dst_ref=_slab(got_ref, 1, k, sz),
                send_sem=send_sems.at[k], recv_sem=recv_sems.at[k], device_id=sibling, device_id_type=MESH)
            cp.start()
            cps.append(cp)
        for cp in cps:
            cp.wait()

    got = pl.pallas_call(
        exchange, name=name + "_swap", out_shape=jax.ShapeDtypeStruct((r, N_CHIPS * sz), grad.dtype),
        in_specs=[HBM_SPEC], out_specs=HBM_SPEC,
        scratch_shapes=[pltpu.SemaphoreType.DMA((N_CHIPS,)), pltpu.SemaphoreType.DMA((N_CHIPS,))],
    )(grad)

    tr = _pick(r, 512, 16)

    def add(c_ref, g_ref, p_ref, o_ref):
        o_ref[...] = (g_ref[...].astype(F32) + p_ref[...].astype(F32)).astype(o_ref.dtype)

    blk = pl.BlockSpec((tr, sz), lambda k, i, c_ref: (i, k))
    return pl.pallas_call(
        add, name=name + "_sum", out_shape=jax.ShapeDtypeStruct((r, N_CHIPS * sz), grad.dtype),
        grid_spec=pltpu.PrefetchScalarGridSpec(
            num_scalar_prefetch=1, grid=(N_CHIPS, r // tr),
            in_specs=[pl.BlockSpec((tr, sz), lambda k, i, c_ref: (i, 2 * k + c_ref[0])), blk], out_specs=blk),
        compiler_params=_params(("parallel", "parallel")),
    )(jnp.reshape(my % 2, (1,)).astype(jnp.int32), grad, got)


def _chip_scatter_start(pair_sum, my, name):
    r, cols = pair_sum.shape
    sz = cols // N_CHIPS
    own = lax.dynamic_slice_in_dim(pair_sum, (my // 2) * sz, sz, 1)
    return (_xfer_start("chips", pair_sum, lax.empty((N_CHIPS - 1, r, sz), pair_sum.dtype), 1, name), own, 1, "chips")


def _tok(a, *tokens):
    for t in tokens:
        a = a + t[0, 0].astype(a.dtype)
    return a


def _mm(a, b, *, name, ta=False, tb=False, out_dtype=F32, a_fn=None, out_fn=None, extras=(), bias=None,
        tm=1024, tn=1024, tk=2048):
    if ta:
        kdim, m = a.shape
    else:
        m, kdim = a.shape
    if tb:
        n, kb = b.shape
    else:
        kb, n = b.shape
    assert kb == kdim, (a.shape, b.shape, ta, tb)
    tm = _pick(m, tm, LANES if ta else SUBLANES)
    tn = _pick(n, tn, LANES)
    tk = _pick(kdim, tk, LANES)
    nk = kdim // tk
    ne = len(extras)
    has_bias = bias is not None

    def body(*refs):
        a_ref, b_ref = refs[0], refs[1]
        pos = 2
        bias_ref = None
        if has_bias:
            bias_ref = refs[pos]
            pos += 1
        extra_refs = refs[pos:pos + ne]
        o_ref = refs[pos + ne]
        av = a_ref[...]
        if a_fn is not None:
            av = a_fn(av)
        dn = (((0 if ta else 1,), (1 if tb else 0,)), ((), ()))
        part = lax.dot_general(av.astype(BF16), b_ref[...].astype(BF16), dn, preferred_element_type=F32)

        def finish(r):
            if has_bias:
                r = r + bias_ref[...]
            if out_fn is not None:
                r = out_fn(r, *[e[...] for e in extra_refs])
            o_ref[...] = r.astype(out_dtype)

        if nk == 1:
            finish(part)
            return
        acc = refs[pos + ne + 1]
        k = pl.program_id(2)

        @pl.when(k == 0)
        def _():
            acc[...] = part

        @pl.when(k > 0)
        def _():
            acc[...] += part

        @pl.when(k == nk - 1)
        def _():
            finish(acc[...])

    in_specs = [
        pl.BlockSpec((tk, tm), lambda i, j, k: (k, i)) if ta else pl.BlockSpec((tm, tk), lambda i, j, k: (i, k)),
        pl.BlockSpec((tn, tk), lambda i, j, k: (j, k)) if tb else pl.BlockSpec((tk, tn), lambda i, j, k: (k, j)),
    ]
    args = [a, b]
    if has_bias:
        in_specs.append(pl.BlockSpec((1, tn), lambda i, j, k: (0, j)))
        args.append(bias)
    for e in extras:
        in_specs.append(pl.BlockSpec((tm, tn), lambda i, j, k: (i, j)))
        args.append(e)
    return pl.pallas_call(
        body, name=name, grid=(m // tm, n // tn, nk),
        in_specs=in_specs, out_specs=pl.BlockSpec((tm, tn), lambda i, j, k: (i, j)),
        out_shape=jax.ShapeDtypeStruct((m, n), out_dtype),
        scratch_shapes=[pltpu.VMEM((tm, tn), F32)] if nk > 1 else [],
        compiler_params=_params(("parallel", "parallel", "arbitrary")),
    )(*args)


def _row(arr, cb=0, rb=0, w=None):
    return ("row", arr, cb, rb, w)


def _bc(arr, cb=0, w=None):
    return ("bc", arr, cb, 0, w)


def _full(arr):
    return ("full", arr, 0, 0, None)


def _rowwise(fn, ins, outs, *, nr, tr, nc, cw, name):
    n_in = len(ins)
    outs = [(o[0], o[1], (o[2] if len(o) > 2 and o[2] else cw)) for o in outs]

    def body(*refs):
        in_refs, out_refs = refs[:n_in], refs[n_in:]
        j, i = pl.program_id(0), pl.program_id(1)
        res = fn(i, j, *[r[...] for r in in_refs])
        if not isinstance(res, (tuple, list)):
            res = (res,)
        for (kind, _, _), o_ref, r in zip(outs, out_refs, res):
            if kind == "row":
                o_ref[...] = r.astype(o_ref.dtype)
            else:
                @pl.when(i == 0)
                def _(o_ref=o_ref):
                    o_ref[...] = jnp.zeros_like(o_ref)

                o_ref[...] += r

    in_specs, args = [], []
    for kind, arr, cb, rb, w in ins:
        w = w or cw
        if kind == "row":
            in_specs.append(pl.BlockSpec((tr, w), lambda j, i, cb=cb, rb=rb: (i + rb, j + cb)))
        elif kind == "bc":
            in_specs.append(pl.BlockSpec((arr.shape[0], w), lambda j, i, cb=cb: (0, j + cb)))
        else:
            in_specs.append(pl.BlockSpec(arr.shape, lambda j, i, nd=arr.ndim: (0,) * nd))
        args.append(arr)
    out_specs, out_shape = [], []
    for kind, spec, w in outs:
        if kind == "row":
            out_specs.append(pl.BlockSpec((tr, w), lambda j, i: (i, j)))
            out_shape.append(jax.ShapeDtypeStruct((nr * tr, nc * w), spec))
        else:
            out_specs.append(pl.BlockSpec((spec, w), lambda j, i: (0, j)))
            out_shape.append(jax.ShapeDtypeStruct((spec, nc * w), F32))
    res = pl.pallas_call(
        body, name=name, grid=(nc, nr), in_specs=in_specs, out_specs=tuple(out_specs), out_shape=tuple(out_shape),
        compiler_params=_params(("parallel", "arbitrary")),
    )(*args)
    return res


def _sum_parts(own, parts, *, name):
    p, r, c = parts.shape
    tr = _pick(r, max(SUBLANES, (1 << 18) // c), SUBLANES)

    def body(o_ref, p_ref, g_out):
        g = o_ref[...].astype(F32)
        for q in range(p):
            g = g + p_ref[q].astype(F32)
        g_out[...] = g

    blk = pl.BlockSpec((tr, c), lambda i: (i, 0))
    return pl.pallas_call(
        body, name=name, grid=(r // tr,), in_specs=[blk, pl.BlockSpec((p, tr, c), lambda i: (0, i, 0))],
        out_specs=blk, out_shape=jax.ShapeDtypeStruct((r, c), F32), compiler_params=_params(("parallel",)),
    )(own, parts)


def _adamw(parts, w, m, v, *, name, own=None):
    p, r, c = parts.shape
    tr = _pick(r, max(SUBLANES, (1 << 18) // c), 16 if parts.dtype == BF16 else SUBLANES)
    bc1 = 1.0 - ADAM_B1 ** ADAM_STEP
    bc2 = 1.0 - ADAM_B2 ** ADAM_STEP
    has_own = own is not None

    def body(*refs):
        p_ref, w_ref, m_ref, v_ref = refs[:4]
        g_out, d_out, m_out, v_out = refs[4 + has_own:]
        if has_own:
            g = refs[4][...].astype(F32) + p_ref[0].astype(F32)
        else:
            g = p_ref[0].astype(F32)
        for q in range(1, p):
            g = g + p_ref[q].astype(F32)
        m_new = ADAM_B1 * m_ref[...] + (1.0 - ADAM_B1) * g
        v_new = ADAM_B2 * v_ref[...] + (1.0 - ADAM_B2) * (g * g)
        m_hat = m_new / bc1
        v_hat = v_new / bc2
        g_out[...] = g
        d_out[...] = -ADAM_LR * (m_hat / (jnp.sqrt(v_hat) + ADAM_EPS) + ADAM_WD * w_ref[...])
        m_out[...] = m_new
        v_out[...] = v_new

    blk = pl.BlockSpec((tr, c), lambda i: (i, 0))
    return pl.pallas_call(
        body, name=name, grid=(r // tr,),
        in_specs=[pl.BlockSpec((p, tr, c), lambda i: (0, i, 0)), blk, blk, blk] + ([blk] if has_own else []),
        out_specs=(blk, blk, blk, blk), out_shape=tuple(jax.ShapeDtypeStruct((r, c), F32) for _ in range(4)),
        compiler_params=_params(("parallel",)),
    )(*([parts, w, m, v] + ([own] if has_own else [])))


_GELU_C = math.sqrt(2.0 / math.pi)
_GELU_K = 0.044715


def _sigmoid(z):
    return 0.5 * jnp.tanh(0.5 * z) + 0.5


def _silu(z):
    return z * _sigmoid(z)


def _dsilu(z):
    s = _sigmoid(z)
    return s * (1.0 + z * (1.0 - s))


def _gelu(z):
    return 0.5 * z * (1.0 + jnp.tanh(_GELU_C * (z + _GELU_K * z * z * z)))


def _dgelu(z):
    t = jnp.tanh(_GELU_C * (z + _GELU_K * z * z * z))
    return 0.5 * (1.0 + t) + 0.5 * z * (1.0 - t * t) * (_GELU_C * (1.0 + 3.0 * _GELU_K * z * z))


def _softplus(z):
    return jnp.maximum(z, 0.0) + jnp.log(1.0 + jnp.exp(-jnp.abs(z)))


def _expm1(z):
    small = z * (1.0 + z * (0.5 + z * (1.0 / 6.0 + z * (1.0 / 24.0))))
    return jnp.where(jnp.abs(z) < 0.03, small, jnp.exp(z) - 1.0)


def _rms(xv):
    return lax.rsqrt(jnp.mean(xv * xv, axis=-1, keepdims=True) + NORM_EPS)


def _rms_bwd(xhat, r, dxhat):
    return r * (dxhat - xhat * jnp.mean(dxhat * xhat, axis=-1, keepdims=True))


def _swap_pairs(y):
    lane = lax.broadcasted_iota(jnp.int32, y.shape, 1)
    return jnp.where(lane % 2 == 0, pltpu.roll(y, LANES - 1, 1), pltpu.roll(y, 1, 1))


def _dot(a, b, ca, cb):
    return lax.dot_general(a, b, (((ca,), (cb,)), ((), ())), preferred_element_type=F32)


_SM_SCALE = HEAD_DIM ** -0.5
_EXP2_SCALE = _SM_SCALE * math.log2(math.e)


def _softmax_parts(qh, kk):
    s = _dot(qh, kk, 1, 1)
    p = jnp.exp2((s - jnp.max(s, axis=-1, keepdims=True)) * _EXP2_SCALE)
    return p, 1.0 / jnp.sum(p, axis=-1, keepdims=True)


def _attn_fwd(q, k, v, *, tq, name):
    s_len, aw = q.shape
    t_len, kvw = k.shape
    gw = Q_GROUP * HEAD_DIM

    def body(q_ref, k_ref, v_ref, o_ref):
        kk, vv = k_ref[...], v_ref[...]
        for g in range(Q_GROUP):
            cols = slice(g * HEAD_DIM, (g + 1) * HEAD_DIM)
            p, inv_l = _softmax_parts(q_ref[:, cols], kk)
            o_ref[:, cols] = (_dot(p.astype(BF16), vv, 1, 0) * inv_l).astype(BF16)

    return pl.pallas_call(
        body, name=name, grid=(kvw // HEAD_DIM, s_len // tq),
        in_specs=[pl.BlockSpec((tq, gw), lambda h, i: (i, h)), pl.BlockSpec((t_len, HEAD_DIM), lambda h, i: (0, h)),
                  pl.BlockSpec((t_len, HEAD_DIM), lambda h, i: (0, h))],
        out_specs=pl.BlockSpec((tq, gw), lambda h, i: (i, h)),
        out_shape=jax.ShapeDtypeStruct((s_len, aw), BF16),
        compiler_params=_params(("parallel", "parallel")),
    )(q, k, v)


def _attn_bwd(q, k, v, do, *, tq, name):
    s_len, aw = q.shape
    t_len, kvw = k.shape
    gw = Q_GROUP * HEAD_DIM

    def body(q_ref, k_ref, v_ref, do_ref, dq_ref, dk_ref, dv_ref):
        @pl.when(pl.program_id(1) == 0)
        def _():
            dk_ref[...] = jnp.zeros_like(dk_ref)
            dv_ref[...] = jnp.zeros_like(dv_ref)

        kk, vv = k_ref[...], v_ref[...]
        for g in range(Q_GROUP):
            cols = slice(g * HEAD_DIM, (g + 1) * HEAD_DIM)
            qh, doh = q_ref[:, cols], do_ref[:, cols]
            p, inv_l = _softmax_parts(qh, kk)
            dp = _dot(doh, vv, 1, 1)
            delta = jnp.sum(p * dp, axis=-1, keepdims=True) * inv_l
            dsb = (p * ((dp - delta) * (inv_l * _SM_SCALE))).astype(BF16)
            dq_ref[:, cols] = _dot(dsb, kk, 1, 0)
            dk_ref[...] += _dot(dsb, qh, 0, 0)
            dv_ref[...] += _dot(p.astype(BF16), (doh * inv_l).astype(BF16), 0, 0)

    qspec = pl.BlockSpec((tq, gw), lambda h, i: (i, h))
    kspec = pl.BlockSpec((t_len, HEAD_DIM), lambda h, i: (0, h))
    return pl.pallas_call(
        body, name=name, grid=(kvw // HEAD_DIM, s_len // tq),
        in_specs=[qspec, kspec, kspec, qspec], out_specs=(qspec, kspec, kspec),
        out_shape=(jax.ShapeDtypeStruct((s_len, aw), F32), jax.ShapeDtypeStruct((t_len, kvw), F32),
                   jax.ShapeDtypeStruct((t_len, kvw), F32)),
        compiler_params=_params(("parallel", "arbitrary")),
    )(q, k, v, do)


SHIFTS = (-2, -1, 1, 2)


def _shift_masks(l_len, t_len):
    t = jnp.arange(t_len)[:, None]
    cols = []
    for d in SHIFTS:
        td = t + d
        ok = ((t < l_len) & (td >= 0) & (td < l_len)) | ((t >= l_len) & (td >= l_len) & (td < t_len))
        cols.append(jnp.broadcast_to(ok.astype(F32), (t_len, RNN_BLOCK)))
    return jnp.concatenate(cols, axis=1)


def _shift_rows(xv, d, m_ref, l_len, t_len):
    if d == 0:
        return xv
    k = SHIFTS.index(d)
    return pltpu.roll(xv, (-d) % t_len, 0) * m_ref[:, k * RNN_BLOCK:(k + 1) * RNN_BLOCK]


def _sublane_scan(a8, b8, rev):
    row = lax.broadcasted_iota(jnp.int32, a8.shape, 0)
    for s in (1, 2, 4):
        if rev:
            ok = row < SUBLANES - s
            sh = SUBLANES - s
        else:
            ok = row >= s
            sh = s
        a_sh = jnp.where(ok, pltpu.roll(a8, sh, 0), 1.0)
        b_sh = jnp.where(ok, pltpu.roll(b8, sh, 0), 0.0)
        b8 = a8 * b_sh + b8
        a8 = a8 * a_sh
    return a8, b8


def _scan_chunks(chains, c0, n, carries):
    def step(kk, hs):
        new = []
        for (a_ref, b_ref, h_ref, rev), hprev in zip(chains, hs):
            c = c0 + (n - 1 - kk if rev else kk)
            r0 = pl.multiple_of(c * SUBLANES, SUBLANES)
            big_a, big_b = _sublane_scan(a_ref[pl.ds(r0, SUBLANES), :], b_ref[pl.ds(r0, SUBLANES), :], rev)
            h8 = big_b + big_a * hprev
            h_ref[pl.ds(r0, SUBLANES), :] = h8
            edge = 0 if rev else SUBLANES - 1
            new.append(jnp.broadcast_to(h8[edge:edge + 1, :], h8.shape))
        return tuple(new)

    return lax.fori_loop(0, n, step, tuple(carries))


def _rglru_conv(x, cw_ref, cb_ref, m_ref, l_len, t_len):
    y = cb_ref[...] + _shift_rows(x, -1, m_ref, l_len, t_len) * cw_ref[0:1, :]
    for kk in range(1, CONV_WIDTH):
        y = y + _shift_rows(x, kk - 1, m_ref, l_len, t_len) * cw_ref[kk:kk + 1, :]
    return y


def _rglru_gates(xc, wr, wi, br, bi, lam):
    xb = xc.astype(BF16)
    r = _sigmoid(_dot(xb, wr.astype(BF16), 1, 0) + br)
    ig = _sigmoid(_dot(xb, wi.astype(BF16), 1, 0) + bi)
    sp = _softplus(-lam)
    log_a = (-LRU_C) * r * sp
    a = jnp.exp(log_a)
    mult = jnp.sqrt(-_expm1(2.0 * log_a))
    u = ig * xc
    return r, ig, sp, a, mult, u


def _rglru_specs(proj, xr_blk, xg_blk, t_len):
    col = lambda off: pl.BlockSpec((t_len, RNN_BLOCK), lambda h, off=off: (0, off + h))
    vec = lambda rows: pl.BlockSpec((rows, RNN_BLOCK), lambda h: (0, h))
    wsp = pl.BlockSpec((2, 1, RNN_BLOCK, RNN_BLOCK), lambda h: (0, h, 0, 0))
    msp = pl.BlockSpec((t_len, len(SHIFTS) * RNN_BLOCK), lambda h: (0, 0))
    return col(xr_blk), col(xg_blk), vec, wsp, msp


def _rglru_fwd(proj, xr_blk, xg_blk, masks, conv_w, conv_b, w_rg, w_ig, b_rg, b_ig, lam, *, l_len, name):
    t_len = proj.shape[0]
    s_len = t_len - l_len
    d_rnn = conv_b.shape[1]
    n_l, n_t = l_len // SUBLANES, t_len // SUBLANES

    def body(xr_ref, xg_ref, m_ref, cw_ref, cb_ref, wr_ref, wi_ref, br_ref, bi_ref, lam_ref, rg_ref, a_s, b_s, h_s):
        xc = _rglru_conv(xr_ref[...], cw_ref, cb_ref, m_ref, l_len, t_len)
        for d in range(2):
            _, _, _, a, mult, u = _rglru_gates(xc, wr_ref[d, 0], wi_ref[d, 0], br_ref[d:d + 1, :],
                                               bi_ref[d:d + 1, :], lam_ref[d:d + 1, :])
            a_s[d] = a
            b_s[d] = mult * u
        chains = [(a_s.at[0], b_s.at[0], h_s.at[0], False), (a_s.at[1], b_s.at[1], h_s.at[1], True)]
        zero = jnp.zeros((SUBLANES, RNN_BLOCK), F32)
        carry = _scan_chunks(chains, 0, n_l, (zero, zero))
        _scan_chunks(chains, n_l, n_t - n_l, carry)
        rnn = h_s[0, pl.ds(l_len, s_len), :] + h_s[1, pl.ds(l_len, s_len), :]
        rg_ref[...] = (rnn * _gelu(xg_ref[pl.ds(l_len, s_len), :])).astype(BF16)

    xr_spec, xg_spec, vec, wsp, msp = _rglru_specs(proj, xr_blk, xg_blk, t_len)
    return pl.pallas_call(
        body, name=name, grid=(d_rnn // RNN_BLOCK,),
        in_specs=[xr_spec, xg_spec, msp, vec(CONV_WIDTH), vec(1), wsp, wsp, vec(2), vec(2), vec(2)],
        out_specs=pl.BlockSpec((s_len, RNN_BLOCK), lambda h: (0, h)),
        out_shape=jax.ShapeDtypeStruct((s_len, d_rnn), BF16),
        scratch_shapes=[pltpu.VMEM((2, t_len, RNN_BLOCK), F32)] * 3,
        compiler_params=_params(("parallel",)),
    )(proj, proj, masks, conv_w, conv_b, w_rg, w_ig, b_rg, b_ig, lam)


N_RNN_VEC = 16


def _rglru_bwd(proj, xr_blk, xg_blk, masks, drg, conv_w, conv_b, w_rg, w_ig, b_rg, b_ig, lam, *, l_len, name):
    t_len = proj.shape[0]
    s_len = t_len - l_len
    d_rnn = conv_b.shape[1]
    n_l, n_t = l_len // SUBLANES, t_len // SUBLANES

    def body(xr_ref, xg_ref, m_ref, drg_ref, cw_ref, cb_ref, wr_ref, wi_ref, br_ref, bi_ref, lam_ref,
             dxr_ref, dxg_ref, dvec_ref, dwr_ref, dwi_ref, a_s, b_s, h_s):
        t = lax.broadcasted_iota(jnp.int32, (t_len, RNN_BLOCK), 0)
        x = xr_ref[...]
        xc = _rglru_conv(x, cw_ref, cb_ref, m_ref, l_len, t_len)
        xb = xc.astype(BF16)
        gates = []
        for d in range(2):
            gt = _rglru_gates(xc, wr_ref[d, 0], wi_ref[d, 0], br_ref[d:d + 1, :], bi_ref[d:d + 1, :],
                              lam_ref[d:d + 1, :])
            gates.append(gt)
            a_s[d] = gt[3]
            b_s[d] = gt[4] * gt[5]
        zero = jnp.zeros((SUBLANES, RNN_BLOCK), F32)
        chains = [(a_s.at[0], b_s.at[0], h_s.at[0], False), (a_s.at[1], b_s.at[1], h_s.at[1], True)]
        carry = _scan_chunks(chains, 0, n_l, (zero, zero))
        _scan_chunks(chains, n_l, n_t - n_l, carry)

        h0, h1 = h_s[0], h_s[1]
        xg = xg_ref[...]
        dr = jnp.concatenate([jnp.zeros((l_len, RNN_BLOCK), F32), drg_ref[...].astype(F32)], axis=0)
        dxg_ref[...] = (dr * (h0 + h1) * _dgelu(xg)).astype(dxg_ref.dtype)
        drnn = dr * _gelu(xg)
        a_s[0] = pltpu.roll(gates[0][3], t_len - 1, 0)
        a_s[1] = pltpu.roll(gates[1][3], 1, 0)
        b_s[0] = drnn
        b_s[1] = drnn
        hprev = [jnp.where(t == 0, 0.0, pltpu.roll(h0, 1, 0)),
                 jnp.where(t == l_len - 1, 0.0, pltpu.roll(h1, t_len - 1, 0))]
        chains = [(a_s.at[0], b_s.at[0], h_s.at[0], True), (a_s.at[1], b_s.at[1], h_s.at[1], False)]
        carry = _scan_chunks(chains, n_l, n_t - n_l, (zero, zero))
        _scan_chunks(chains, 0, n_l, carry)

        dxc = jnp.zeros((t_len, RNN_BLOCK), F32)
        for d in range(2):
            r, ig, sp, a, mult, u = gates[d]
            lamb = h_s[d]
            da = lamb * hprev[d]
            dmult = lamb * u
            du = lamb * mult
            dlog_a = da * a - dmult * (a * a) / mult
            dzr = (dlog_a * ((-LRU_C) * sp)) * r * (1.0 - r)
            dsp = jnp.sum(dlog_a * ((-LRU_C) * r), axis=0, keepdims=True)
            dzi = (du * xc) * ig * (1.0 - ig)
            dzrb, dzib = dzr.astype(BF16), dzi.astype(BF16)
            dxc = dxc + du * ig + _dot(dzrb, wr_ref[d, 0].astype(BF16), 1, 1) + _dot(dzib, wi_ref[d, 0].astype(BF16), 1, 1)
            dwr_ref[d, 0] = _dot(xb, dzrb, 0, 0)
            dwi_ref[d, 0] = _dot(xb, dzib, 0, 0)
            dvec_ref[5 + d:6 + d, :] = jnp.sum(dzr, axis=0, keepdims=True)
            dvec_ref[7 + d:8 + d, :] = jnp.sum(dzi, axis=0, keepdims=True)
            dvec_ref[9 + d:10 + d, :] = dsp * (-_sigmoid(-lam_ref[d:d + 1, :]))
        dvec_ref[4:5, :] = jnp.sum(dxc, axis=0, keepdims=True)
        dvec_ref[11:N_RNN_VEC, :] = jnp.zeros((N_RNN_VEC - 11, RNN_BLOCK), F32)
        dxr = jnp.zeros((t_len, RNN_BLOCK), F32)
        for kk in range(CONV_WIDTH):
            dvec_ref[kk:kk + 1, :] = jnp.sum(dxc * _shift_rows(x, kk - 1, m_ref, l_len, t_len), axis=0, keepdims=True)
            dxr = dxr + _shift_rows(dxc, 1 - kk, m_ref, l_len, t_len) * cw_ref[kk:kk + 1, :]
        dxr_ref[...] = dxr.astype(dxr_ref.dtype)

    xr_spec, xg_spec, vec, wsp, msp = _rglru_specs(proj, xr_blk, xg_blk, t_len)
    col_t = pl.BlockSpec((t_len, RNN_BLOCK), lambda h: (0, h))
    return pl.pallas_call(
        body, name=name, grid=(d_rnn // RNN_BLOCK,),
        in_specs=[xr_spec, xg_spec, msp, pl.BlockSpec((s_len, RNN_BLOCK), lambda h: (0, h)), vec(CONV_WIDTH), vec(1),
                  wsp, wsp, vec(2), vec(2), vec(2)],
        out_specs=(col_t, col_t, vec(N_RNN_VEC), wsp, wsp),
        out_shape=(jax.ShapeDtypeStruct((t_len, d_rnn), BF16), jax.ShapeDtypeStruct((t_len, d_rnn), BF16),
                   jax.ShapeDtypeStruct((N_RNN_VEC, d_rnn), F32), jax.ShapeDtypeStruct(w_rg.shape, F32),
                   jax.ShapeDtypeStruct(w_ig.shape, F32)),
        scratch_shapes=[pltpu.VMEM((2, t_len, RNN_BLOCK), F32)] * 3,
        compiler_params=_params(("parallel",)),
    )(proj, proj, masks, drg, conv_w, conv_b, w_rg, w_ig, b_rg, b_ig, lam)


def _colsum(v):
    return jnp.sum(v, axis=0, keepdims=True)


def _qk_fwd_heads(xv, gain, cos_e, sin_e):
    outs = []
    for h in range(xv.shape[1] // HEAD_DIM):
        xh = xv[:, h * HEAD_DIM:(h + 1) * HEAD_DIM]
        y = xh * _rms(xh) * gain
        outs.append(y * cos_e + _swap_pairs(y) * sin_e)
    return outs[0] if len(outs) == 1 else jnp.concatenate(outs, axis=1)


def _qk_bwd_heads(xv, dov, gain, cos_e, sin_e):
    outs = []
    dgain = jnp.zeros((1, HEAD_DIM), F32)
    for h in range(xv.shape[1] // HEAD_DIM):
        cols = slice(h * HEAD_DIM, (h + 1) * HEAD_DIM)
        xh, doh = xv[:, cols], dov[:, cols]
        r = _rms(xh)
        xhat = xh * r
        dy = doh * cos_e + _swap_pairs(doh * sin_e)
        dgain = dgain + _colsum(dy * xhat)
        outs.append(_rms_bwd(xhat, r, dy * gain))
    return (outs[0] if len(outs) == 1 else jnp.concatenate(outs, axis=1)), dgain


WEIGHTS = ['c_ctx', 'w_mod', 'b_mod', 'g_mix', 'g_mlp', 'w_in', 'q_gain', 'k_gain', 'conv_w', 'conv_b', 'w_rg', 'b_rg',
           'w_ig', 'b_ig', 'lru_lambda', 'w_o_attn', 'w_o_rnn', 'w_out', 'w_up', 'w_down', 'g_final']
REPLICATED = ['c_ctx', 'b_mod', 'g_mix', 'g_mlp', 'q_gain', 'k_gain', 'conv_b', 'w_rg', 'w_ig', 'g_final']
SMALL_SHARDED = [('conv_w', CONV_WIDTH), ('b_rg', 2), ('b_ig', 2), ('lru_lambda', 2)]
SMALL_ROWS = 16


PACK_ROWS = 512


def _pack_flat(arrs, rows_mult):
    flat = jnp.concatenate([a.reshape(-1) for a in arrs])
    n = flat.shape[0]
    unit = rows_mult * LANES
    pad = (-n) % unit
    if pad:
        flat = jnp.concatenate([flat, jnp.zeros((pad,), flat.dtype)])
    return flat.reshape(-1, LANES)


def _unpack_flat(packed, shapes):
    flat = packed.reshape(-1)
    out, pos = [], 0
    for shp in shapes:
        n = math.prod(shp)
        out.append(flat[pos:pos + n].reshape(shp))
        pos += n
    return out


def _pack_small(d):
    rows = [d[n][0] for n, _ in SMALL_SHARDED]
    cols = rows[0].shape[1]
    used = sum(k for _, k in SMALL_SHARDED)
    return jnp.concatenate(rows + [jnp.zeros((SMALL_ROWS - used, cols), F32)], axis=0)


def _unpack_small(p):
    out, pos = {}, 0
    for n, k in SMALL_SHARDED:
        out[n] = p[pos:pos + k][None]
        pos += k
    return out


def kernel(x, c, ctx, c_ctx, w_mod, b_mod, g_mix, g_mlp, w_in, q_gain, k_gain, conv_w, conv_b, w_rg, b_rg, w_ig, b_ig, lru_lambda, w_o_attn, w_o_rnn, w_out, w_up, w_down, g_final, loss_target, m_c_ctx, m_w_mod, m_b_mod, m_g_mix, m_g_mlp, m_w_in, m_q_gain, m_k_gain, m_conv_w, m_conv_b, m_w_rg, m_b_rg, m_w_ig, m_b_ig, m_lru_lambda, m_w_o_attn, m_w_o_rnn, m_w_out, m_w_up, m_w_down, m_g_final, v_c_ctx, v_w_mod, v_b_mod, v_g_mix, v_g_mlp, v_w_in, v_q_gain, v_k_gain, v_conv_w, v_conv_b, v_w_rg, v_b_rg, v_w_ig, v_b_ig, v_lru_lambda, v_w_o_attn, v_w_o_rnn, v_w_out, v_w_up, v_w_down, v_g_final):
    wd = dict(c_ctx=c_ctx, w_mod=w_mod, b_mod=b_mod, g_mix=g_mix, g_mlp=g_mlp, w_in=w_in, q_gain=q_gain, k_gain=k_gain,
              conv_w=conv_w, conv_b=conv_b, w_rg=w_rg, b_rg=b_rg, w_ig=w_ig, b_ig=b_ig, lru_lambda=lru_lambda,
              w_o_attn=w_o_attn, w_o_rnn=w_o_rnn, w_out=w_out, w_up=w_up, w_down=w_down, g_final=g_final)
    md = dict(zip(WEIGHTS, (m_c_ctx, m_w_mod, m_b_mod, m_g_mix, m_g_mlp, m_w_in, m_q_gain, m_k_gain, m_conv_w, m_conv_b,
                            m_w_rg, m_b_rg, m_w_ig, m_b_ig, m_lru_lambda, m_w_o_attn, m_w_o_rnn, m_w_out, m_w_up,
                            m_w_down, m_g_final)))
    vd = dict(zip(WEIGHTS, (v_c_ctx, v_w_mod, v_b_mod, v_g_mix, v_g_mlp, v_w_in, v_q_gain, v_k_gain, v_conv_w, v_conv_b,
                            v_w_rg, v_b_rg, v_w_ig, v_b_ig, v_lru_lambda, v_w_o_attn, v_w_o_rnn, v_w_out, v_w_up,
                            v_w_down, v_g_final)))

    xs, tgt, ctxs = x[0], loss_target[0], ctx[0]
    s_len, d = xs.shape
    l_len = ctxs.shape[0]
    t_len = s_len + l_len
    n_in = w_in.shape[2] * N_DEV
    aw = (n_in - 4 * d) * 2 // 3
    kvw = aw // Q_GROUP
    assert aw == d and n_in == aw + 2 * kvw + 4 * d
    nmod = N_MOD * d // N_DEV
    tr = 128
    assert l_len % tr == 0 and s_len % tr == 0
    lb = l_len // tr
    tq = min(256, s_len)
    my = _lin(_my_pos())
    c_ctx2 = c_ctx[None]
    g_final2 = g_final[None]

    c_all = _all_gather([jnp.broadcast_to(c, (SUBLANES, d))], [0], "ag_cond")[0][::SUBLANES]
    c16 = jnp.concatenate([c_all, jnp.broadcast_to(c_ctx2, (N_DEV, d))], axis=0)
    b_mod_sh = lax.dynamic_slice(b_mod, (0, my * nmod), (1, nmod))
    mod_part = _mm(c16, w_mod[0], a_fn=_silu, bias=b_mod_sh, name="mod_fwd")
    (mod_g,), order = _all_gather([mod_part], [1], "ag_mod", with_token=True)
    mod_lat = lax.dynamic_slice(mod_g, (my, 0), (1, N_MOD * d))
    sh_a, sc_a, ga_a, sh_f, sc_f, ga_f = [mod_lat[:, k * d:(k + 1) * d] for k in range(N_MOD)]
    csh_a, csc_a = mod_g[N_DEV:N_DEV + 1, :d], mod_g[N_DEV:N_DEV + 1, d:2 * d]

    sp_w = _pack_small(wd)
    (w_in_f, sp_f), order = _all_gather([_tok(w_in[0], order).astype(BF16), sp_w], [1, 1], "ag_w_in", with_token=True)
    conv_w_f, b_rg_f, b_ig_f, lam_f = sp_f[0:4], sp_f[4:6], sp_f[6:8], sp_f[8:10]
    late_axis = dict(w_o_attn=0, w_o_rnn=0, w_out=0, w_up=1, w_down=0)
    placed = [_place_block(wd[n][0], ax, my, "place_" + n) for n, ax in late_axis.items()]
    late_kind = lambda n: "gather_chips" if n in ("w_up", "w_down") else "gather"
    late = {}
    for (n, ax), full in zip(late_axis.items(), placed):
        late[n] = _xfer_start(late_kind(n), _tok(wd[n][0], order).astype(BF16), full, ax, "ag_start_" + n)
        order = late[n][4]
    g_mix_fwd = _tok(g_mix, order)

    def late_wait(n, after):
        full = _xfer_wait(late_kind(n), late[n], late_axis[n], after, "ag_wait_" + n)
        return _sibling_swap(full, late_axis[n], "ag_swap_" + n) if late_kind(n) == "gather_chips" else full

    pos = jnp.arange(s_len)
    n_freq = HEAD_DIM // 4
    inv_freq = ROPE_THETA ** (-jnp.arange(n_freq, dtype=F32) / n_freq)
    ang = jnp.concatenate([(pos // GRID_W).astype(F32)[:, None] * inv_freq,
                           (pos % GRID_W).astype(F32)[:, None] * inv_freq], axis=-1)
    sign = jnp.where(jnp.arange(HEAD_DIM) % 2 == 0, -1.0, 1.0).astype(F32)
    cos_e = jnp.concatenate([jnp.ones((l_len, HEAD_DIM), F32), jnp.repeat(jnp.cos(ang), 2, axis=-1)], axis=0)
    sin_e = jnp.concatenate([jnp.zeros((l_len, HEAD_DIM), F32), jnp.repeat(jnp.sin(ang), 2, axis=-1) * sign], axis=0)

    x_all = jnp.concatenate([ctxs, xs], axis=0)

    def nm1_fwd(i, j, xv, g, csh, csc, sh, sc):
        is_ctx = i < lb
        n = xv * _rms(xv) * g
        return n * (1.0 + jnp.where(is_ctx, csc, sc)) + jnp.where(is_ctx, csh, sh)

    h_all = _rowwise(nm1_fwd, [_row(x_all), _bc(g_mix_fwd), _bc(csh_a), _bc(csc_a), _bc(sh_a), _bc(sc_a)],
                     [("row", BF16)], nr=t_len // tr, tr=tr, nc=1, cw=d, name="norm1_fwd")[0]
    proj = _mm(h_all, w_in_f, name="proj_fwd")

    kb = aw // kvw
    q = _rowwise(lambda i, j, xv, gain, ce, se: _qk_fwd_heads(xv, gain, ce, se),
                 [_row(proj, 0, lb, aw), _full(q_gain), _row(cos_e, 0, lb, HEAD_DIM), _row(sin_e, 0, lb, HEAD_DIM)],
                 [("row", BF16, aw)], nr=s_len // tr, tr=tr, nc=1, cw=aw, name="q_fwd")[0]
    k_all, v_all = _rowwise(lambda i, j, kv_, vv_, gain, ce, se: (_qk_fwd_heads(kv_, gain, ce, se), vv_),
                            [_row(proj, kb, 0, kvw), _row(proj, kb + 1, 0, kvw), _full(k_gain),
                             _row(cos_e, 0, 0, HEAD_DIM), _row(sin_e, 0, 0, HEAD_DIM)],
                            [("row", BF16, kvw), ("row", BF16, kvw)], nr=t_len // tr, tr=tr, nc=1, cw=kvw, name="kv_fwd")
    attn_o = _attn_fwd(q, k_all, v_all, tq=tq, name="attn_fwd")

    xr_blk = (aw + 2 * kvw) // RNN_BLOCK
    xg_blk = xr_blk + d // RNN_BLOCK
    shift_masks = _shift_masks(l_len, t_len)
    rg = _rglru_fwd(proj, xr_blk, xg_blk, shift_masks, conv_w_f, conv_b, w_rg[0], w_ig[0], b_rg_f, b_ig_f, lam_f, l_len=l_len,
                    name="rglru_fwd")

    w_oa_f = late_wait("w_o_attn", attn_o)
    y_attn = _mm(attn_o, w_oa_f, name="o_attn_fwd")
    w_or_f = late_wait("w_o_rnn", rg)
    y_rnn = _mm(rg, w_or_f, name="o_rnn_fwd")
    hw = d // 2
    glb = (aw + 2 * kvw + 2 * d) // hw

    def merge_fwd(i, j, ya, yr, gla, glr):
        return _sigmoid(gla) * ya + _sigmoid(glr) * yr

    u = _rowwise(merge_fwd, [_row(y_attn), _row(y_rnn), _row(proj, glb, lb), _row(proj, glb + 2, lb)],
                 [("row", BF16)], nr=s_len // tr, tr=tr, nc=2, cw=hw, name="merge_fwd")[0]
    w_out_f = late_wait("w_out", u)
    mix = _mm(u, w_out_f, name="out_fwd")

    def res1_fwd(i, j, xv, mixv, gaa, gm, shf, scf):
        x1v = xv + gaa * mixv
        n = x1v * _rms(x1v) * gm
        return x1v, n * (1.0 + scf) + shf

    x1, h2 = _rowwise(res1_fwd, [_row(xs), _row(mix), _bc(ga_a), _bc(g_mlp), _bc(sh_f), _bc(sc_f)],
                      [("row", F32), ("row", BF16)], nr=s_len // tr, tr=tr, nc=1, cw=d, name="norm2_fwd")
    relu2 = lambda z: jnp.square(jnp.maximum(z, 0.0))
    w_up_f = late_wait("w_up", h2)
    up = _mm(h2, w_up_f, name="up_fwd")
    w_down_f = late_wait("w_down", up)
    down = _mm(up, w_down_f, a_fn=relu2, name="down_fwd")

    def final_fn(i, j, x1v, dn, tg, gaf, gf):
        x2 = x1v + gaf * dn
        r = _rms(x2)
        xh = x2 * r
        e = xh * gf - tg
        dy = e * (1.0 / d)
        dx2 = _rms_bwd(xh, r, dy * gf)
        return dx2, dx2 * gaf, _colsum(e * e) * (0.5 / d), _colsum(dy * xh), _colsum(dx2 * dn)

    dx2, ddown, loss_cols, g_gfinal, d_ga_f = _rowwise(
        final_fn, [_row(x1), _row(down), _row(tgt), _bc(ga_f), _bc(g_final2)],
        [("row", F32), ("row", BF16), ("acc", 1), ("acc", 1), ("acc", 1)], nr=s_len // tr, tr=tr, nc=1, cw=d,
        name="final_bwd")
    loss = lax.psum(jnp.sum(loss_cols), ("x", "y", "c"))

    dup = _mm(ddown, w_down_f, tb=True, out_fn=lambda r, upv: r * (2.0 * jnp.maximum(upv, 0.0)), extras=[up],
              out_dtype=BF16, name="down_bwd_x")
    gw_down = _mm(up, ddown, ta=True, a_fn=relu2, out_dtype=BF16, name="down_bwd_w")
    sent = dict(w_down=_scatter_start(gw_down, 0, my, "rs_start_w_down"))
    dh2 = _mm(dup, w_up_f, tb=True, name="up_bwd_x")
    gw_up = _mm(h2, dup, ta=True, out_dtype=BF16, name="up_bwd_w")
    sent["w_up"] = _scatter_start(gw_up, 1, my, "rs_start_w_up")
    sc_f_bwd = _tok(sc_f, sent["w_down"][0][4], sent["w_up"][0][4])

    def norm2_bwd(i, j, x1v, mixv, dh2v, dx2v, gaa, gm, scf):
        r = _rms(x1v)
        xh = x1v * r
        n = xh * gm
        dn = dh2v * (1.0 + scf)
        dx1 = dx2v + _rms_bwd(xh, r, dn * gm)
        return dx1, dx1 * gaa, _colsum(dh2v), _colsum(dh2v * n), _colsum(dn * xh), _colsum(dx1 * mixv)

    dx1, dmix, d_sh_f, d_sc_f, g_gmlp, d_ga_a = _rowwise(
        norm2_bwd, [_row(x1), _row(mix), _row(dh2), _row(dx2), _bc(ga_a), _bc(g_mlp), _bc(sc_f_bwd)],
        [("row", F32), ("row", BF16), ("acc", 1), ("acc", 1), ("acc", 1), ("acc", 1)], nr=s_len // tr, tr=tr, nc=1,
        cw=d, name="norm2_bwd")

    du = _mm(dmix, w_out_f, tb=True, name="out_bwd_x")
    gw_out = _mm(u, dmix, ta=True, out_dtype=BF16, name="out_bwd_w")
    sent["w_out"] = _scatter_start(gw_out, 0, my, "rs_start_w_out")

    def merge_bwd(i, j, duv, ya, yr, gla, glr):
        ga, gr = _sigmoid(gla), _sigmoid(glr)
        return duv * ga, duv * gr, duv * ya * ga * (1.0 - ga), duv * yr * gr * (1.0 - gr)

    dya, dyr, dgla, dglr = _rowwise(
        merge_bwd, [_row(du), _row(y_attn), _row(y_rnn), _row(proj, glb, lb), _row(proj, glb + 2, lb)],
        [("row", BF16)] * 4, nr=s_len // tr, tr=tr, nc=2, cw=hw, name="merge_bwd")
    dattn = _mm(dya, w_oa_f, tb=True, out_dtype=BF16, name="o_attn_bwd_x")
    gw_oa = _mm(attn_o, dya, ta=True, out_dtype=BF16, name="o_attn_bwd_w")
    drg = _mm(dyr, w_or_f, tb=True, out_dtype=BF16, name="o_rnn_bwd_x")
    gw_or = _mm(rg, dyr, ta=True, out_dtype=BF16, name="o_rnn_bwd_w")
    sent["w_o_attn"] = _scatter_start(gw_oa, 0, my, "rs_start_w_o_attn")
    sent["w_o_rnn"] = _scatter_start(gw_or, 0, my, "rs_start_w_o_rnn")
    conv_b_bwd = _tok(conv_b, sent["w_out"][0][4], sent["w_o_attn"][0][4], sent["w_o_rnn"][0][4])

    dq, dk, dv = _attn_bwd(q, k_all, v_all, dattn, tq=tq, name="attn_bwd")
    dxr, dxg, dvec, g_wrg, g_wig = _rglru_bwd(proj, xr_blk, xg_blk, shift_masks, drg, conv_w_f, conv_b_bwd, w_rg[0], w_ig[0], b_rg_f,
                                               b_ig_f, lam_f, l_len=l_len, name="rglru_bwd")
    gate_part = _pack_flat([g_wrg, g_wig], N_DEV * SUBLANES)
    gate_sent = _scatter_start(gate_part, 0, my, "rs_start_gates")
    q_gain_bwd = _tok(q_gain, gate_sent[0][4])

    dq_p, g_qgain = _rowwise(lambda i, j, xv, dov, gain, ce, se: _qk_bwd_heads(xv, dov, gain, ce, se),
                             [_row(proj, 0, lb, aw), _row(dq, 0, 0, aw), _full(q_gain_bwd), _row(cos_e, 0, lb, HEAD_DIM),
                              _row(sin_e, 0, lb, HEAD_DIM)],
                             [("row", BF16, aw), ("acc", 1, HEAD_DIM)], nr=s_len // tr, tr=tr, nc=1, cw=aw, name="q_bwd")

    def kv_bwd(i, j, xv, dkv, dvv, gain, ce, se):
        dk_p, dgain = _qk_bwd_heads(xv, dkv, gain, ce, se)
        return dk_p, dvv, dgain

    dk_p, dv_p, g_kgain = _rowwise(kv_bwd, [_row(proj, kb, 0, kvw), _row(dk, 0, 0, kvw), _row(dv, 0, 0, kvw),
                                            _full(k_gain), _row(cos_e, 0, 0, HEAD_DIM), _row(sin_e, 0, 0, HEAD_DIM)],
                                   [("row", BF16, kvw), ("row", BF16, kvw), ("acc", 1, HEAD_DIM)], nr=t_len // tr, tr=tr,
                                   nc=1, cw=kvw, name="kv_bwd")

    pad_ctx = lambda a: jnp.pad(a, ((l_len, 0), (0, 0)))
    dproj = jnp.concatenate([pad_ctx(dq_p), dk_p, dv_p, dxr, dxg, pad_ctx(dgla), pad_ctx(dglr)], axis=1)
    dh_all = _mm(dproj, w_in_f, tb=True, name="proj_bwd_x")
    g_mix_bwd = g_mix

    def norm1_bwd(i, j, xv, dhv, dresv, g, csc, sc):
        is_ctx = i < lb
        r = _rms(xv)
        xh = xv * r
        dn = dhv * (1.0 + jnp.where(is_ctx, csc, sc))
        dsh, dsc = _colsum(dhv), _colsum(dhv * (xh * g))
        zero = jnp.zeros_like(dsh)
        return (dresv + _rms_bwd(xh, r, dn * g), jnp.where(is_ctx, dsh, zero), jnp.where(is_ctx, dsc, zero),
                jnp.where(is_ctx, zero, dsh), jnp.where(is_ctx, zero, dsc), _colsum(dn * xh))

    dx_all, d_csh, d_csc, d_sh_a, d_sc_a, g_gmix = _rowwise(
        norm1_bwd, [_row(x_all), _row(dh_all), _row(pad_ctx(dx1)), _bc(g_mix_bwd), _bc(csc_a), _bc(sc_a)],
        [("row", F32)] + [("acc", 1)] * 5, nr=t_len // tr, tr=tr, nc=1, cw=d, name="norm1_bwd")
    grad_x = dx_all[l_len:][None]

    dmod_lat = jnp.concatenate([d_sh_a, d_sc_a, d_ga_a, d_sh_f, d_sc_f, d_ga_f], axis=1)
    dmod_ctx = jnp.concatenate([d_csh, d_csc, jnp.zeros((1, (N_MOD - 2) * d), F32)], axis=1)
    dm_g = _all_gather([jnp.concatenate([dmod_lat, dmod_ctx, jnp.zeros((SUBLANES - 2, N_MOD * d), F32)], axis=0)], [0],
                       "ag_dmod")[0]
    dm16 = jnp.concatenate([dm_g[0::SUBLANES], dm_g[1::SUBLANES]], axis=0)
    dm_sh = lax.dynamic_slice(dm16, (0, my * nmod), (2 * N_DEV, nmod))
    gw_mod = _mm(c16, dm_sh, ta=True, a_fn=_silu, name="mod_bwd_w")
    dc16 = _mm(dm_sh, w_mod[0], tb=True, name="mod_bwd_x")

    def cctx_fn(i, j, pv, cc, dl, dc):
        row = lax.broadcasted_iota(jnp.int32, pv.shape, 0)
        return _colsum(jnp.where(row >= N_DEV, pv, 0.0)) * _dsilu(cc), dl + dc

    g_cctx, g_bmod = _rowwise(cctx_fn, [_full(dc16), _full(c_ctx2), _full(dmod_lat), _full(dmod_ctx)],
                              [("row", F32, d), ("row", F32, N_MOD * d)], nr=1, tr=1, nc=1, cw=d, name="cctx_bwd")

    rep_part = dict(c_ctx=g_cctx, b_mod=g_bmod, g_mix=g_gmix, g_mlp=g_gmlp, q_gain=g_qgain, k_gain=g_kgain,
                    conv_b=dvec[4:5], g_final=g_gfinal)
    rep_small = [n for n in REPLICATED if n in rep_part]
    gates = [n for n in REPLICATED if n not in rep_part]
    part_pack = _pack_flat([rep_part[n] for n in rep_small], PACK_ROWS)
    rep_rows = part_pack.shape[0]
    gate_land = _xfer_wait("scatter", gate_sent[0], 0, g_bmod, "rs_wait_gates")
    gate_sum = _sum_parts(gate_sent[1], gate_land, name="gate_sum")
    (gate_full, parts), order = _all_gather([gate_sum, part_pack], [0, 0], "ag_rep_grads", with_token=True)
    small_g = jnp.concatenate([dvec[0:4], dvec[5:11], jnp.zeros((SMALL_ROWS - 10, d), F32)], axis=0)
    (small_recv,), order = _all_to_all([_tok(small_g, order)], [1], "a2a_small_grads")

    gw_in = _mm(h_all, dproj, ta=True, out_dtype=BF16, name="proj_bwd_w", bias=_tok(jnp.zeros((1, n_in), F32), order))
    sent["w_in"] = _chip_scatter_start(_pair_reduce(gw_in, my, "pair_w_in"), my, "rs_start_w_in")
    started = sent["w_in"][0][4]

    res = {}
    for names, pr, mult, nm in ((rep_small, parts.reshape(N_DEV, rep_rows, LANES), PACK_ROWS, "adamw_rep"),
                                (gates, gate_full[None], N_DEV * SUBLANES, "adamw_gates")):
        rep_out = _adamw(pr, _tok(_pack_flat([wd[n] for n in names], mult), started),
                         _pack_flat([md[n] for n in names], mult), _pack_flat([vd[n] for n in names], mult), name=nm)
        for kind, packed in zip(("g", "d", "m", "v"), rep_out):
            for n, a in zip(names, _unpack_flat(packed, [wd[n].shape for n in names])):
                res[kind, n] = a
    out = _adamw(small_recv, _tok(sp_w, started), _pack_small(md), _pack_small(vd), name="adamw_small")
    for kind, a in zip(("g", "d", "m", "v"), out):
        for n, piece in _unpack_small(a).items():
            res[kind, n] = piece

    after = started
    for n, (handle, own, ax, kind) in sent.items():
        land = _xfer_wait(kind, handle, ax, after, "rs_wait_" + n)
        out = _adamw(land, wd[n][0], md[n][0], vd[n][0], own=own, name="adamw_" + n)
        after = out[0]
        for kind, a in zip(("g", "d", "m", "v"), out):
            res[kind, n] = a[None]
        if n == "w_o_rnn":
            out = _adamw(gw_mod[None], w_mod[0], m_w_mod[0], v_w_mod[0], name="adamw_w_mod")
            after = out[0]
            for kind, a in zip(("g", "d", "m", "v"), out):
                res[kind, 'w_mod'] = a[None]

    outs = [loss, grad_x]
    for kind in ("g", "d", "m", "v"):
        outs += [res[kind, n].reshape(wd[n].shape) for n in WEIGHTS]
    return tuple(outs)
```

```python
import functools
import math

import jax
import jax.numpy as jnp
from jax import lax
from jax.experimental import pallas as pl
from jax.experimental.pallas import tpu as pltpu

F32 = jnp.float32
BF16 = jnp.bfloat16
MESH = pl.DeviceIdType.MESH

N_DEV = 8
HEAD_DIM = 128
Q_GROUP = 4
RNN_BLOCK = 128
GRID_W = 64
CONV_WIDTH = 4
LRU_C = 8.0
ROPE_THETA = 10000.0
NORM_EPS = 1e-6
N_MOD = 6
ADAM_LR = 0.001
ADAM_B1 = 0.9
ADAM_B2 = 0.999
ADAM_EPS = 1e-08
ADAM_WD = 0.01
ADAM_STEP = 10

LANES = 128
SUBLANES = 8
VMEM_LIMIT = 56 * 1024 * 1024
HBM_SPEC = pl.BlockSpec(memory_space=pltpu.HBM)


def _params(sem=None, vmem=VMEM_LIMIT):
    kw = dict(vmem_limit_bytes=vmem)
    if sem is not None:
        kw["dimension_semantics"] = sem
    return pltpu.CompilerParams(**kw)


def _pick(dim, target, mult):
    for t in range(min(dim, target), 0, -1):
        if dim % t == 0 and t % mult == 0:
            return t
    return dim


def _my_pos():
    return lax.axis_index("x"), lax.axis_index("y"), lax.axis_index("c")


def _flip(pos, d):
    x, y, c = pos
    return (1 - x if d & 4 else x, 1 - y if d & 2 else y, 1 - c if d & 1 else c)


def _lin(pos):
    return 4 * pos[0] + 2 * pos[1] + pos[2]


def _slab(ref, axis, idx, size):
    if axis == 0:
        return ref.at[pl.ds(idx * size, size)]
    assert axis == 1
    return ref.at[:, pl.ds(idx * size, size)]


def _all_gather(arrs, axes, name, with_token=False):
    n = len(arrs)
    sizes = [a.shape[ax] for a, ax in zip(arrs, axes)]

    def body(*refs):
        ins, outs, token = refs[:n], refs[n:2 * n], refs[2 * n]
        send_sems, recv_sems, local_sems = refs[2 * n + 1:]
        token[...] = jnp.zeros_like(token)
        me = _my_pos()
        sibling = _flip(me, 1)
        chips = [2, 4, 6]

        def copy(a, k, block_pos, to, src=None):
            dst = _slab(outs[a], axes[a], _lin(block_pos), sizes[a])
            return pltpu.make_async_remote_copy(
                src_ref=dst if src is None else src, dst_ref=dst,
                send_sem=send_sems.at[7 * a + k], recv_sem=recv_sems.at[7 * a + k],
                device_id=to, device_id_type=MESH)

        local, first, passed = [], [], []
        for a in range(n):
            cp = pltpu.make_async_copy(ins[a], _slab(outs[a], axes[a], _lin(me), sizes[a]), local_sems.at[a])
            cp.start()
            local.append(cp)
        for a in range(n):
            f = [copy(a, 0, me, sibling, src=ins[a])]
            f += [copy(a, 1 + j, me, _flip(me, m), src=ins[a]) for j, m in enumerate(chips)]
            for cp in f:
                cp.start()
            first += f
        for a in range(n):
            for j, m in enumerate(chips):
                copy(a, 1 + j, _flip(me, m), me).wait_recv()
                cp = copy(a, 4 + j, _flip(me, m), sibling)
                cp.start()
                passed.append(cp)
        for a in range(n):
            copy(a, 0, sibling, me).wait_recv()
            for j, m in enumerate(chips):
                copy(a, 4 + j, _flip(sibling, m), me).wait_recv()
        for cp in first + passed:
            cp.wait_send()
        for cp in local:
            cp.wait()

    out_shape = []
    for a, ax in zip(arrs, axes):
        shp = list(a.shape)
        shp[ax] *= N_DEV
        out_shape.append(jax.ShapeDtypeStruct(tuple(shp), a.dtype))
    out_shape.append(jax.ShapeDtypeStruct((SUBLANES, LANES), F32))
    res = pl.pallas_call(
        body, name=name, out_shape=tuple(out_shape),
        in_specs=[HBM_SPEC] * n, out_specs=tuple([HBM_SPEC] * n + [pl.BlockSpec(memory_space=pltpu.VMEM)]),
        scratch_shapes=[pltpu.SemaphoreType.DMA((7 * n,)), pltpu.SemaphoreType.DMA((7 * n,)),
                        pltpu.SemaphoreType.DMA((n,))],
    )(*arrs)
    return (list(res[:n]), res[n]) if with_token else list(res[:n])


def _all_to_all(arrs, axes, name):
    n = len(arrs)
    sizes = [a.shape[ax] // N_DEV for a, ax in zip(arrs, axes)]

    def body(*refs):
        ins, outs, token = refs[:n], refs[n:2 * n], refs[2 * n]
        send_sems, recv_sems, local_sems = refs[2 * n + 1:]
        token[...] = jnp.zeros_like(token)
        me = _my_pos()
        my = _lin(me)
        copies, local = [], []
        for a in range(n):
            cp = pltpu.make_async_copy(_slab(ins[a], axes[a], my, sizes[a]), outs[a].at[my], local_sems.at[a])
            cp.start()
            local.append(cp)
        for d in range(1, N_DEV):
            peer = _flip(me, d)
            for a in range(n):
                cp = pltpu.make_async_remote_copy(
                    src_ref=_slab(ins[a], axes[a], _lin(peer), sizes[a]), dst_ref=outs[a].at[my],
                    send_sem=send_sems.at[7 * a + d - 1], recv_sem=recv_sems.at[7 * a + d - 1],
                    device_id=peer, device_id_type=MESH)
                cp.start()
                copies.append((a, d, cp))
        for a, d, cp in copies:
            peer = _flip(me, d)
            pltpu.make_async_remote_copy(
                src_ref=_slab(ins[a], axes[a], _lin(peer), sizes[a]), dst_ref=outs[a].at[_lin(peer)],
                send_sem=send_sems.at[7 * a + d - 1], recv_sem=recv_sems.at[7 * a + d - 1],
                device_id=peer, device_id_type=MESH).wait_recv()
            cp.wait_send()
        for cp in local:
            cp.wait()

    out_shape = []
    for a, ax, s in zip(arrs, axes, sizes):
        shp = list(a.shape)
        shp[ax] = s
        out_shape.append(jax.ShapeDtypeStruct((N_DEV,) + tuple(shp), a.dtype))
    out_shape.append(jax.ShapeDtypeStruct((SUBLANES, LANES), F32))
    res = pl.pallas_call(
        body, name=name, out_shape=tuple(out_shape),
        in_specs=[HBM_SPEC] * n, out_specs=tuple([HBM_SPEC] * n + [pl.BlockSpec(memory_space=pltpu.VMEM)]),
        scratch_shapes=[pltpu.SemaphoreType.DMA((7 * n,)), pltpu.SemaphoreType.DMA((7 * n,)),
                        pltpu.SemaphoreType.DMA((n,))],
    )(*arrs)
    return list(res[:n]), res[n]


SEM_SPEC = pl.BlockSpec(memory_space=pltpu.SEMAPHORE)
_EFFECT = pltpu.SideEffectType.DATAFLOW_SIDE_EFFECTING
N_PEERS = N_DEV - 1


def _split_copies(kind, src_ref, land_ref, send_sems, recv_sems, axis, size, arriving):
    me = _my_pos()
    out = []
    for dd in (range(2, N_DEV, 2) if kind in ("chips", "gather_chips") else range(1, N_DEV)):
        peer = _flip(me, dd)
        k = dd - 1
        if kind in ("gather", "gather_chips"):
            src, dst = src_ref, _slab(land_ref, axis, _lin(peer if arriving else me), size)
        elif kind == "scatter":
            src, dst = _slab(src_ref, axis, _lin(peer), size), land_ref.at[k]
        else:
            k = dd // 2 - 1
            src, dst = _slab(src_ref, axis, 2 * peer[0] + peer[1], size), land_ref.at[k]
        out.append(pltpu.make_async_remote_copy(
            src_ref=src, dst_ref=dst, send_sem=send_sems.at[k], recv_sem=recv_sems.at[k],
            device_id=peer, device_id_type=MESH))
    return out


N_CHIPS = N_DEV // 2


def _split_size(kind, src_shape, axis):
    return src_shape[axis] // {"gather": 1, "gather_chips": 1, "scatter": N_DEV, "chips": N_CHIPS}[kind]


def _sibling_swap(full, axis, name):
    size = full.shape[axis] // N_DEV

    def body(in_ref, o_ref, send_sems, recv_sems):
        me = _my_pos()
        sibling = _flip(me, 1)
        cps = []
        for k in range(N_CHIPS):
            mine = _slab(o_ref, axis, 2 * k + me[2], size)
            cp = pltpu.make_async_remote_copy(src_ref=mine, dst_ref=mine, send_sem=send_sems.at[k],
                                              recv_sem=recv_sems.at[k], device_id=sibling, device_id_type=MESH)
            cp.start()
            cps.append(cp)
        for k, cp in enumerate(cps):
            cp.wait_send()
            theirs = _slab(o_ref, axis, 2 * k + sibling[2], size)
            pltpu.make_async_remote_copy(src_ref=theirs, dst_ref=theirs, send_sem=send_sems.at[k],
                                         recv_sem=recv_sems.at[k], device_id=sibling, device_id_type=MESH).wait_recv()

    return pl.pallas_call(
        body, name=name, out_shape=jax.ShapeDtypeStruct(full.shape, full.dtype), in_specs=[HBM_SPEC], out_specs=HBM_SPEC,
        scratch_shapes=[pltpu.SemaphoreType.DMA((N_CHIPS,)), pltpu.SemaphoreType.DMA((N_CHIPS,))],
        input_output_aliases={0: 0},
    )(full)


def _xfer_start(kind, src, land, axis, name):
    size = _split_size(kind, src.shape, axis)

    def body(src_ref, land_ref, send_sems, recv_sems, src_thru, land_thru, token):
        for cp in _split_copies(kind, src_ref, land_ref, send_sems, recv_sems, axis, size, False):
            cp.start()
        token[...] = jnp.zeros_like(token)

    return pl.pallas_call(
        body, name=name,
        out_shape=(pltpu.SemaphoreType.DMA((N_PEERS,)), pltpu.SemaphoreType.DMA((N_PEERS,)),
                   pltpu.HBM(src.shape, src.dtype), pltpu.HBM(land.shape, land.dtype),
                   jax.ShapeDtypeStruct((SUBLANES, LANES), F32)),
        in_specs=(HBM_SPEC, HBM_SPEC),
        out_specs=(SEM_SPEC, SEM_SPEC, HBM_SPEC, HBM_SPEC, pl.BlockSpec(memory_space=pltpu.VMEM)),
        input_output_aliases={0: 2, 1: 3},
        compiler_params=pltpu.CompilerParams(has_side_effects=_EFFECT),
    )(pltpu.with_memory_space_constraint(src, pltpu.HBM), pltpu.with_memory_space_constraint(land, pltpu.HBM))


def _xfer_wait(kind, handle, axis, after, name):
    send_sems, recv_sems, src_thru, land_thru, _ = handle
    size = _split_size(kind, src_thru.shape, axis)

    def body(src_ref, land_ref, send_sems, recv_sems, after_ref, src_dead, got_ref):
        for cp in _split_copies(kind, src_ref, land_ref, send_sems, recv_sems, axis, size, True):
            cp.wait_send()
            cp.wait_recv()

    return pl.pallas_call(
        body, name=name,
        out_shape=(pltpu.HBM(src_thru.shape, src_thru.dtype), pltpu.HBM(land_thru.shape, land_thru.dtype)),
        in_specs=(HBM_SPEC, HBM_SPEC, SEM_SPEC, SEM_SPEC, pl.BlockSpec(memory_space=pl.ANY)),
        out_specs=(HBM_SPEC, HBM_SPEC), input_output_aliases={0: 0, 1: 1},
        compiler_params=pltpu.CompilerParams(has_side_effects=_EFFECT),
    )(src_thru, land_thru, send_sems, recv_sems, after)[1]


def _place_block(shard, axis, my, name):
    r, c = shard.shape
    tr = _pick(r, max(16, (1 << 19) // c), 16)
    nr = r // tr
    if axis == 0:
        full, out_map = (r * N_DEV, c), (lambda i, my_ref: (my_ref[0] * nr + i, 0))
    else:
        full, out_map = (r, c * N_DEV), (lambda i, my_ref: (i, my_ref[0]))

    def body(my_ref, x_ref, o_ref):
        o_ref[...] = x_ref[...].astype(BF16)

    return pl.pallas_call(
        body, name=name, out_shape=jax.ShapeDtypeStruct(full, BF16),
        grid_spec=pltpu.PrefetchScalarGridSpec(
            num_scalar_prefetch=1, grid=(nr,), in_specs=[pl.BlockSpec((tr, c), lambda i, my_ref: (i, 0))],
            out_specs=pl.BlockSpec((tr, c), out_map)),
        compiler_params=_params(("arbitrary",)),
    )(jnp.reshape(my, (1,)).astype(jnp.int32), shard)


def _scatter_start(grad, axis, my, name):
    shp = list(grad.shape)
    shp[axis] //= N_DEV
    own = lax.dynamic_slice_in_dim(grad, my * shp[axis], shp[axis], axis)
    return (_xfer_start("scatter", grad, lax.empty((N_PEERS,) + tuple(shp), grad.dtype), axis, name), own, axis,
            "scatter")


def _pair_reduce(grad, my, name):
    r, cols = grad.shape
    sz = cols // N_DEV

    def exchange(g_ref, got_ref, send_sems, recv_sems):
        me = _my_pos()
        sibling = _flip(me, 1)
        cps = []
        for k in range(N_CHIPS):
            cp = pltpu.make_async_remote_copy(
                src_ref=_slab(g_ref, 1, 2 * k + sibling[2], sz), dst_ref=_slab(got_ref, 1, k, sz),
                send_sem=send_sems.at[k], recv_sem=recv_sems.at[k], device_id=sibling, device_id_type=MESH)
            cp.start()
            cps.append(cp)
        for cp in cps:
            cp.wait()

    got = pl.pallas_call(
        exchange, name=name + "_swap", out_shape=jax.ShapeDtypeStruct((r, N_CHIPS * sz), grad.dtype),
        in_specs=[HBM_SPEC], out_specs=HBM_SPEC,
        scratch_shapes=[pltpu.SemaphoreType.DMA((N_CHIPS,)), pltpu.SemaphoreType.DMA((N_CHIPS,))],
    )(grad)

    tr = _pick(r, 512, 16)

    def add(c_ref, g_ref, p_ref, o_ref):
        o_ref[...] = (g_ref[...].astype(F32) + p_ref[...].astype(F32)).astype(o_ref.dtype)

    blk = pl.BlockSpec((tr, sz), lambda k, i, c_ref: (i, k))
    return pl.pallas_call(
        add, name=name + "_sum", out_shape=jax.ShapeDtypeStruct((r, N_CHIPS * sz), grad.dtype),
        grid_spec=pltpu.PrefetchScalarGridSpec(
            num_scalar_prefetch=1, grid=(N_CHIPS, r // tr),
            in_specs=[pl.BlockSpec((tr, sz), lambda k, i, c_ref: (i, 2 * k + c_ref[0])), blk], out_specs=blk),
        compiler_params=_params(("parallel", "parallel")),
    )(jnp.reshape(my % 2, (1,)).astype(jnp.int32), grad, got)


def _chip_scatter_start(pair_sum, my, name):
    r, cols = pair_sum.shape
    sz = cols // N_CHIPS
    own = lax.dynamic_slice_in_dim(pair_sum, (my // 2) * sz, sz, 1)
    return (_xfer_start("chips", pair_sum, lax.empty((N_CHIPS - 1, r, sz), pair_sum.dtype), 1, name), own, 1, "chips")


def _tok(a, *tokens):
    for t in tokens:
        a = a + t[0, 0].astype(a.dtype)
    return a


def _mm(a, b, *, name, ta=False, tb=False, out_dtype=F32, a_fn=None, out_fn=None, extras=(), bias=None,
        tm=1024, tn=1024, tk=2816):
    if ta:
        kdim, m = a.shape
    else:
        m, kdim = a.shape
    if tb:
        n, kb = b.shape
    else:
        kb, n = b.shape
    assert kb == kdim, (a.shape, b.shape, ta, tb)
    tm = _pick(m, tm, LANES if ta else SUBLANES)
    tn = _pick(n, tn, LANES)
    tk = _pick(kdim, tk, LANES)
    nk = kdim // tk
    ne = len(extras)
    has_bias = bias is not None

    def body(*refs):
        a_ref, b_ref = refs[0], refs[1]
        pos = 2
        bias_ref = None
        if has_bias:
            bias_ref = refs[pos]
            pos += 1
        extra_refs = refs[pos:pos + ne]
        o_ref = refs[pos + ne]
        av = a_ref[...]
        if a_fn is not None:
            av = a_fn(av)
        dn = (((0 if ta else 1,), (1 if tb else 0,)), ((), ()))
        part = lax.dot_general(av.astype(BF16), b_ref[...].astype(BF16), dn, preferred_element_type=F32)

        def finish(r):
            if has_bias:
                r = r + bias_ref[...]
            if out_fn is not None:
                r = out_fn(r, *[e[...] for e in extra_refs])
            o_ref[...] = r.astype(out_dtype)

        if nk == 1:
            finish(part)
            return
        acc = refs[pos + ne + 1]
        k = pl.program_id(2)

        @pl.when(k == 0)
        def _():
            acc[...] = part

        @pl.when(k > 0)
        def _():
            acc[...] += part

        @pl.when(k == nk - 1)
        def _():
            finish(acc[...])

    in_specs = [
        pl.BlockSpec((tk, tm), lambda i, j, k: (k, i)) if ta else pl.BlockSpec((tm, tk), lambda i, j, k: (i, k)),
        pl.BlockSpec((tn, tk), lambda i, j, k: (j, k)) if tb else pl.BlockSpec((tk, tn), lambda i, j, k: (k, j)),
    ]
    args = [a, b]
    if has_bias:
        in_specs.append(pl.BlockSpec((1, tn), lambda i, j, k: (0, j)))
        args.append(bias)
    for e in extras:
        in_specs.append(pl.BlockSpec((tm, tn), lambda i, j, k: (i, j)))
        args.append(e)
    return pl.pallas_call(
        body, name=name, grid=(m // tm, n // tn, nk),
        in_specs=in_specs, out_specs=pl.BlockSpec((tm, tn), lambda i, j, k: (i, j)),
        out_shape=jax.ShapeDtypeStruct((m, n), out_dtype),
        scratch_shapes=[pltpu.VMEM((tm, tn), F32)] if nk > 1 else [],
        compiler_params=_params(("parallel", "parallel", "arbitrary")),
    )(*args)


def _row(arr, cb=0, rb=0, w=None):
    return ("row", arr, cb, rb, w)


def _bc(arr, cb=0, w=None):
    return ("bc", arr, cb, 0, w)


def _full(arr):
    return ("full", arr, 0, 0, None)


def _rowwise(fn, ins, outs, *, nr, tr, nc, cw, name):
    n_in = len(ins)
    outs = [(o[0], o[1], (o[2] if len(o) > 2 and o[2] else cw)) for o in outs]

    def body(*refs):
        in_refs, out_refs = refs[:n_in], refs[n_in:]
        j, i = pl.program_id(0), pl.program_id(1)
        res = fn(i, j, *[r[...] for r in in_refs])
        if not isinstance(res, (tuple, list)):
            res = (res,)
        for (kind, _, _), o_ref, r in zip(outs, out_refs, res):
            if kind == "row":
                o_ref[...] = r.astype(o_ref.dtype)
            else:
                @pl.when(i == 0)
                def _(o_ref=o_ref):
                    o_ref[...] = jnp.zeros_like(o_ref)

                o_ref[...] += r

    in_specs, args = [], []
    for kind, arr, cb, rb, w in ins:
        w = w or cw
        if kind == "row":
            in_specs.append(pl.BlockSpec((tr, w), lambda j, i, cb=cb, rb=rb: (i + rb, j + cb)))
        elif kind == "bc":
            in_specs.append(pl.BlockSpec((arr.shape[0], w), lambda j, i, cb=cb: (0, j + cb)))
        else:
            in_specs.append(pl.BlockSpec(arr.shape, lambda j, i, nd=arr.ndim: (0,) * nd))
        args.append(arr)
    out_specs, out_shape = [], []
    for kind, spec, w in outs:
        if kind == "row":
            out_specs.append(pl.BlockSpec((tr, w), lambda j, i: (i, j)))
            out_shape.append(jax.ShapeDtypeStruct((nr * tr, nc * w), spec))
        else:
            out_specs.append(pl.BlockSpec((spec, w), lambda j, i: (0, j)))
            out_shape.append(jax.ShapeDtypeStruct((spec, nc * w), F32))
    res = pl.pallas_call(
        body, name=name, grid=(nc, nr), in_specs=in_specs, out_specs=tuple(out_specs), out_shape=tuple(out_shape),
        compiler_params=_params(("parallel", "arbitrary")),
    )(*args)
    return res


def _sum_parts(own, parts, *, name):
    p, r, c = parts.shape
    tr = _pick(r, max(SUBLANES, (1 << 18) // c), SUBLANES)

    def body(o_ref, p_ref, g_out):
        g = o_ref[...].astype(F32)
        for q in range(p):
            g = g + p_ref[q].astype(F32)
        g_out[...] = g

    blk = pl.BlockSpec((tr, c), lambda i: (i, 0))
    return pl.pallas_call(
        body, name=name, grid=(r // tr,), in_specs=[blk, pl.BlockSpec((p, tr, c), lambda i: (0, i, 0))],
        out_specs=blk, out_shape=jax.ShapeDtypeStruct((r, c), F32), compiler_params=_params(("parallel",)),
    )(own, parts)


def _adamw(parts, w, m, v, *, name, own=None):
    p, r, c = parts.shape
    tr = _pick(r, max(SUBLANES, (1 << 18) // c), 16 if parts.dtype == BF16 else SUBLANES)
    bc1 = 1.0 - ADAM_B1 ** ADAM_STEP
    bc2 = 1.0 - ADAM_B2 ** ADAM_STEP
    has_own = own is not None

    def body(*refs):
        p_ref, w_ref, m_ref, v_ref = refs[:4]
        g_out, d_out, m_out, v_out = refs[4 + has_own:]
        if has_own:
            g = refs[4][...].astype(F32) + p_ref[0].astype(F32)
        else:
            g = p_ref[0].astype(F32)
        for q in range(1, p):
            g = g + p_ref[q].astype(F32)
        m_new = ADAM_B1 * m_ref[...] + (1.0 - ADAM_B1) * g
        v_new = ADAM_B2 * v_ref[...] + (1.0 - ADAM_B2) * (g * g)
        m_hat = m_new / bc1
        v_hat = v_new / bc2
        g_out[...] = g
        d_out[...] = -ADAM_LR * (m_hat / (jnp.sqrt(v_hat) + ADAM_EPS) + ADAM_WD * w_ref[...])
        m_out[...] = m_new
        v_out[...] = v_new

    blk = pl.BlockSpec((tr, c), lambda i: (i, 0))
    return pl.pallas_call(
        body, name=name, grid=(r // tr,),
        in_specs=[pl.BlockSpec((p, tr, c), lambda i: (0, i, 0)), blk, blk, blk] + ([blk] if has_own else []),
        out_specs=(blk, blk, blk, blk), out_shape=tuple(jax.ShapeDtypeStruct((r, c), F32) for _ in range(4)),
        compiler_params=_params(("parallel",)),
    )(*([parts, w, m, v] + ([own] if has_own else [])))


_GELU_C = math.sqrt(2.0 / math.pi)
_GELU_K = 0.044715


def _sigmoid(z):
    return 0.5 * jnp.tanh(0.5 * z) + 0.5


def _silu(z):
    return z * _sigmoid(z)


def _dsilu(z):
    s = _sigmoid(z)
    return s * (1.0 + z * (1.0 - s))


def _gelu(z):
    return 0.5 * z * (1.0 + jnp.tanh(_GELU_C * (z + _GELU_K * z * z * z)))


def _dgelu(z):
    t = jnp.tanh(_GELU_C * (z + _GELU_K * z * z * z))
    return 0.5 * (1.0 + t) + 0.5 * z * (1.0 - t * t) * (_GELU_C * (1.0 + 3.0 * _GELU_K * z * z))


def _softplus(z):
    return jnp.maximum(z, 0.0) + jnp.log(1.0 + jnp.exp(-jnp.abs(z)))


def _expm1(z):
    small = z * (1.0 + z * (0.5 + z * (1.0 / 6.0 + z * (1.0 / 24.0))))
    return jnp.where(jnp.abs(z) < 0.03, small, jnp.exp(z) - 1.0)


def _rms(xv):
    return lax.rsqrt(jnp.mean(xv * xv, axis=-1, keepdims=True) + NORM_EPS)


def _rms_bwd(xhat, r, dxhat):
    return r * (dxhat - xhat * jnp.mean(dxhat * xhat, axis=-1, keepdims=True))


def _swap_pairs(y):
    lane = lax.broadcasted_iota(jnp.int32, y.shape, 1)
    return jnp.where(lane % 2 == 0, pltpu.roll(y, LANES - 1, 1), pltpu.roll(y, 1, 1))


def _dot(a, b, ca, cb):
    return lax.dot_general(a, b, (((ca,), (cb,)), ((), ())), preferred_element_type=F32)


_SM_SCALE = HEAD_DIM ** -0.5
_EXP2_SCALE = _SM_SCALE * math.log2(math.e)


def _softmax_parts(qh, kk):
    s = _dot(qh, kk, 1, 1)
    p = jnp.exp2((s - jnp.max(s, axis=-1, keepdims=True)) * _EXP2_SCALE)
    return p, 1.0 / jnp.sum(p, axis=-1, keepdims=True)


def _attn_fwd(q, k, v, *, tq, name):
    s_len, aw = q.shape
    t_len, kvw = k.shape
    gw = Q_GROUP * HEAD_DIM

    def body(q_ref, k_ref, v_ref, o_ref):
        kk, vv = k_ref[...], v_ref[...]
        for g in range(Q_GROUP):
            cols = slice(g * HEAD_DIM, (g + 1) * HEAD_DIM)
            p, inv_l = _softmax_parts(q_ref[:, cols], kk)
            o_ref[:, cols] = (_dot(p.astype(BF16), vv, 1, 0) * inv_l).astype(BF16)

    return pl.pallas_call(
        body, name=name, grid=(kvw // HEAD_DIM, s_len // tq),
        in_specs=[pl.BlockSpec((tq, gw), lambda h, i: (i, h)), pl.BlockSpec((t_len, HEAD_DIM), lambda h, i: (0, h)),
                  pl.BlockSpec((t_len, HEAD_DIM), lambda h, i: (0, h))],
        out_specs=pl.BlockSpec((tq, gw), lambda h, i: (i, h)),
        out_shape=jax.ShapeDtypeStruct((s_len, aw), BF16),
        compiler_params=_params(("parallel", "parallel")),
    )(q, k, v)


def _attn_bwd(q, k, v, do, *, tq, name):
    s_len, aw = q.shape
    t_len, kvw = k.shape
    gw = Q_GROUP * HEAD_DIM

    def body(q_ref, k_ref, v_ref, do_ref, dq_ref, dk_ref, dv_ref):
        @pl.when(pl.program_id(1) == 0)
        def _():
            dk_ref[...] = jnp.zeros_like(dk_ref)
            dv_ref[...] = jnp.zeros_like(dv_ref)

        kk, vv = k_ref[...], v_ref[...]
        for g in range(Q_GROUP):
            cols = slice(g * HEAD_DIM, (g + 1) * HEAD_DIM)
            qh, doh = q_ref[:, cols], do_ref[:, cols]
            p, inv_l = _softmax_parts(qh, kk)
            dp = _dot(doh, vv, 1, 1)
            delta = jnp.sum(p * dp, axis=-1, keepdims=True) * inv_l
            dsb = (p * ((dp - delta) * (inv_l * _SM_SCALE))).astype(BF16)
            dq_ref[:, cols] = _dot(dsb, kk, 1, 0)
            dk_ref[...] += _dot(dsb, qh, 0, 0)
            dv_ref[...] += _dot(p.astype(BF16), (doh * inv_l).astype(BF16), 0, 0)

    qspec = pl.BlockSpec((tq, gw), lambda h, i: (i, h))
    kspec = pl.BlockSpec((t_len, HEAD_DIM), lambda h, i: (0, h))
    return pl.pallas_call(
        body, name=name, grid=(kvw // HEAD_DIM, s_len // tq),
        in_specs=[qspec, kspec, kspec, qspec], out_specs=(qspec, kspec, kspec),
        out_shape=(jax.ShapeDtypeStruct((s_len, aw), F32), jax.ShapeDtypeStruct((t_len, kvw), F32),
                   jax.ShapeDtypeStruct((t_len, kvw), F32)),
        compiler_params=_params(("parallel", "arbitrary")),
    )(q, k, v, do)


SHIFTS = (-2, -1, 1, 2)


def _shift_masks(l_len, t_len):
    t = jnp.arange(t_len)[:, None]
    cols = []
    for d in SHIFTS:
        td = t + d
        ok = ((t < l_len) & (td >= 0) & (td < l_len)) | ((t >= l_len) & (td >= l_len) & (td < t_len))
        cols.append(jnp.broadcast_to(ok.astype(F32), (t_len, RNN_BLOCK)))
    return jnp.concatenate(cols, axis=1)


def _shift_rows(xv, d, m_ref, l_len, t_len):
    if d == 0:
        return xv
    k = SHIFTS.index(d)
    return pltpu.roll(xv, (-d) % t_len, 0) * m_ref[:, k * RNN_BLOCK:(k + 1) * RNN_BLOCK]


def _sublane_scan(a8, b8, rev):
    row = lax.broadcasted_iota(jnp.int32, a8.shape, 0)
    for s in (1, 2, 4):
        if rev:
            ok = row < SUBLANES - s
            sh = SUBLANES - s
        else:
            ok = row >= s
            sh = s
        a_sh = jnp.where(ok, pltpu.roll(a8, sh, 0), 1.0)
        b_sh = jnp.where(ok, pltpu.roll(b8, sh, 0), 0.0)
        b8 = a8 * b_sh + b8
        a8 = a8 * a_sh
    return a8, b8


def _scan_chunks(chains, c0, n, carries):
    def step(kk, hs):
        new = []
        for (a_ref, b_ref, h_ref, rev), hprev in zip(chains, hs):
            c = c0 + (n - 1 - kk if rev else kk)
            r0 = pl.multiple_of(c * SUBLANES, SUBLANES)
            big_a, big_b = _sublane_scan(a_ref[pl.ds(r0, SUBLANES), :], b_ref[pl.ds(r0, SUBLANES), :], rev)
            h8 = big_b + big_a * hprev
            h_ref[pl.ds(r0, SUBLANES), :] = h8
            edge = 0 if rev else SUBLANES - 1
            new.append(jnp.broadcast_to(h8[edge:edge + 1, :], h8.shape))
        return tuple(new)

    return lax.fori_loop(0, n, step, tuple(carries))


def _rglru_conv(x, cw_ref, cb_ref, m_ref, l_len, t_len):
    y = cb_ref[...] + _shift_rows(x, -1, m_ref, l_len, t_len) * cw_ref[0:1, :]
    for kk in range(1, CONV_WIDTH):
        y = y + _shift_rows(x, kk - 1, m_ref, l_len, t_len) * cw_ref[kk:kk + 1, :]
    return y


def _rglru_gates(xc, wr, wi, br, bi, lam):
    xb = xc.astype(BF16)
    r = _sigmoid(_dot(xb, wr.astype(BF16), 1, 0) + br)
    ig = _sigmoid(_dot(xb, wi.astype(BF16), 1, 0) + bi)
    sp = _softplus(-lam)
    log_a = (-LRU_C) * r * sp
    a = jnp.exp(log_a)
    mult = jnp.sqrt(-_expm1(2.0 * log_a))
    u = ig * xc
    return r, ig, sp, a, mult, u


def _rglru_specs(proj, xr_blk, xg_blk, t_len):
    col = lambda off: pl.BlockSpec((t_len, RNN_BLOCK), lambda h, off=off: (0, off + h))
    vec = lambda rows: pl.BlockSpec((rows, RNN_BLOCK), lambda h: (0, h))
    wsp = pl.BlockSpec((2, 1, RNN_BLOCK, RNN_BLOCK), lambda h: (0, h, 0, 0))
    msp = pl.BlockSpec((t_len, len(SHIFTS) * RNN_BLOCK), lambda h: (0, 0))
    return col(xr_blk), col(xg_blk), vec, wsp, msp


def _rglru_fwd(proj, xr_blk, xg_blk, masks, conv_w, conv_b, w_rg, w_ig, b_rg, b_ig, lam, *, l_len, name):
    t_len = proj.shape[0]
    s_len = t_len - l_len
    d_rnn = conv_b.shape[1]
    n_l, n_t = l_len // SUBLANES, t_len // SUBLANES

    def body(xr_ref, xg_ref, m_ref, cw_ref, cb_ref, wr_ref, wi_ref, br_ref, bi_ref, lam_ref, rg_ref, hf_ref, hb_ref,
             a_s, b_s):
        xc = _rglru_conv(xr_ref[...], cw_ref, cb_ref, m_ref, l_len, t_len)
        for d in range(2):
            _, _, _, a, mult, u = _rglru_gates(xc, wr_ref[d, 0], wi_ref[d, 0], br_ref[d:d + 1, :],
                                               bi_ref[d:d + 1, :], lam_ref[d:d + 1, :])
            a_s[d] = a
            b_s[d] = mult * u
        chains = [(a_s.at[0], b_s.at[0], hf_ref, False), (a_s.at[1], b_s.at[1], hb_ref, True)]
        zero = jnp.zeros((SUBLANES, RNN_BLOCK), F32)
        carry = _scan_chunks(chains, 0, n_l, (zero, zero))
        _scan_chunks(chains, n_l, n_t - n_l, carry)
        rnn = hf_ref[pl.ds(l_len, s_len), :] + hb_ref[pl.ds(l_len, s_len), :]
        rg_ref[...] = (rnn * _gelu(xg_ref[pl.ds(l_len, s_len), :])).astype(BF16)

    xr_spec, xg_spec, vec, wsp, msp = _rglru_specs(proj, xr_blk, xg_blk, t_len)
    col_t = pl.BlockSpec((t_len, RNN_BLOCK), lambda h: (0, h))
    return pl.pallas_call(
        body, name=name, grid=(d_rnn // RNN_BLOCK,),
        in_specs=[xr_spec, xg_spec, msp, vec(CONV_WIDTH), vec(1), wsp, wsp, vec(2), vec(2), vec(2)],
        out_specs=(pl.BlockSpec((s_len, RNN_BLOCK), lambda h: (0, h)), col_t, col_t),
        out_shape=(jax.ShapeDtypeStruct((s_len, d_rnn), BF16), jax.ShapeDtypeStruct((t_len, d_rnn), F32),
                   jax.ShapeDtypeStruct((t_len, d_rnn), F32)),
        scratch_shapes=[pltpu.VMEM((2, t_len, RNN_BLOCK), F32)] * 2,
        compiler_params=_params(("parallel",)),
    )(proj, proj, masks, conv_w, conv_b, w_rg, w_ig, b_rg, b_ig, lam)


N_RNN_VEC = 16


def _rglru_bwd(proj, xr_blk, xg_blk, masks, h_fwd, h_bwd, drg, conv_w, conv_b, w_rg, w_ig, b_rg, b_ig, lam, *, l_len, name):
    t_len = proj.shape[0]
    s_len = t_len - l_len
    d_rnn = conv_b.shape[1]
    n_l, n_t = l_len // SUBLANES, t_len // SUBLANES

    def body(xr_ref, xg_ref, m_ref, hf_ref, hb_ref, drg_ref, cw_ref, cb_ref, wr_ref, wi_ref, br_ref, bi_ref, lam_ref,
             dxr_ref, dxg_ref, dvec_ref, dwr_ref, dwi_ref, a_s, b_s, h_s):
        t = lax.broadcasted_iota(jnp.int32, (t_len, RNN_BLOCK), 0)
        x = xr_ref[...]
        xc = _rglru_conv(x, cw_ref, cb_ref, m_ref, l_len, t_len)
        xb = xc.astype(BF16)
        gates = []
        for d in range(2):
            gt = _rglru_gates(xc, wr_ref[d, 0], wi_ref[d, 0], br_ref[d:d + 1, :], bi_ref[d:d + 1, :],
                              lam_ref[d:d + 1, :])
            gates.append(gt)
        zero = jnp.zeros((SUBLANES, RNN_BLOCK), F32)
        h0, h1 = hf_ref[...], hb_ref[...]
        xg = xg_ref[...]
        dr = jnp.concatenate([jnp.zeros((l_len, RNN_BLOCK), F32), drg_ref[...].astype(F32)], axis=0)
        dxg_ref[...] = (dr * (h0 + h1) * _dgelu(xg)).astype(dxg_ref.dtype)
        drnn = dr * _gelu(xg)
        a_s[0] = pltpu.roll(gates[0][3], t_len - 1, 0)
        a_s[1] = pltpu.roll(gates[1][3], 1, 0)
        b_s[0] = drnn
        b_s[1] = drnn
        hprev = [jnp.where(t == 0, 0.0, pltpu.roll(h0, 1, 0)),
                 jnp.where(t == l_len - 1, 0.0, pltpu.roll(h1, t_len - 1, 0))]
        chains = [(a_s.at[0], b_s.at[0], h_s.at[0], True), (a_s.at[1], b_s.at[1], h_s.at[1], False)]
        carry = _scan_chunks(chains, n_l, n_t - n_l, (zero, zero))
        _scan_chunks(chains, 0, n_l, carry)

        dxc = jnp.zeros((t_len, RNN_BLOCK), F32)
        for d in range(2):
            r, ig, sp, a, mult, u = gates[d]
            lamb = h_s[d]
            da = lamb * hprev[d]
            dmult = lamb * u
            du = lamb * mult
            dlog_a = da * a - dmult * (a * a) / mult
            dzr = (dlog_a * ((-LRU_C) * sp)) * r * (1.0 - r)
            dsp = jnp.sum(dlog_a * ((-LRU_C) * r), axis=0, keepdims=True)
            dzi = (du * xc) * ig * (1.0 - ig)
            dzrb, dzib = dzr.astype(BF16), dzi.astype(BF16)
            dxc = dxc + du * ig + _dot(dzrb, wr_ref[d, 0].astype(BF16), 1, 1) + _dot(dzib, wi_ref[d, 0].astype(BF16), 1, 1)
            dwr_ref[d, 0] = _dot(xb, dzrb, 0, 0)
            dwi_ref[d, 0] = _dot(xb, dzib, 0, 0)
            dvec_ref[5 + d:6 + d, :] = jnp.sum(dzr, axis=0, keepdims=True)
            dvec_ref[7 + d:8 + d, :] = jnp.sum(dzi, axis=0, keepdims=True)
            dvec_ref[9 + d:10 + d, :] = dsp * (-_sigmoid(-lam_ref[d:d + 1, :]))
        dvec_ref[4:5, :] = jnp.sum(dxc, axis=0, keepdims=True)
        dvec_ref[11:N_RNN_VEC, :] = jnp.zeros((N_RNN_VEC - 11, RNN_BLOCK), F32)
        dxr = jnp.zeros((t_len, RNN_BLOCK), F32)
        for kk in range(CONV_WIDTH):
            dvec_ref[kk:kk + 1, :] = jnp.sum(dxc * _shift_rows(x, kk - 1, m_ref, l_len, t_len), axis=0, keepdims=True)
            dxr = dxr + _shift_rows(dxc, 1 - kk, m_ref, l_len, t_len) * cw_ref[kk:kk + 1, :]
        dxr_ref[...] = dxr.astype(dxr_ref.dtype)

    xr_spec, xg_spec, vec, wsp, msp = _rglru_specs(proj, xr_blk, xg_blk, t_len)
    col_t = pl.BlockSpec((t_len, RNN_BLOCK), lambda h: (0, h))
    return pl.pallas_call(
        body, name=name, grid=(d_rnn // RNN_BLOCK,),
        in_specs=[xr_spec, xg_spec, msp, col_t, col_t, pl.BlockSpec((s_len, RNN_BLOCK), lambda h: (0, h)), vec(CONV_WIDTH),
                  vec(1),
                  wsp, wsp, vec(2), vec(2), vec(2)],
        out_specs=(col_t, col_t, vec(N_RNN_VEC), wsp, wsp),
        out_shape=(jax.ShapeDtypeStruct((t_len, d_rnn), BF16), jax.ShapeDtypeStruct((t_len, d_rnn), BF16),
                   jax.ShapeDtypeStruct((N_RNN_VEC, d_rnn), F32), jax.ShapeDtypeStruct(w_rg.shape, F32),
                   jax.ShapeDtypeStruct(w_ig.shape, F32)),
        scratch_shapes=[pltpu.VMEM((2, t_len, RNN_BLOCK), F32)] * 3,
        compiler_params=_params(("parallel",)),
    )(proj, proj, masks, h_fwd, h_bwd, drg, conv_w, conv_b, w_rg, w_ig, b_rg, b_ig, lam)


def _colsum(v):
    return jnp.sum(v, axis=0, keepdims=True)


def _qk_fwd_heads(xv, gain, cos_e, sin_e):
    outs = []
    for h in range(xv.shape[1] // HEAD_DIM):
        xh = xv[:, h * HEAD_DIM:(h + 1) * HEAD_DIM]
        y = xh * _rms(xh) * gain
        outs.append(y * cos_e + _swap_pairs(y) * sin_e)
    return outs[0] if len(outs) == 1 else jnp.concatenate(outs, axis=1)


def _qk_bwd_heads(xv, dov, gain, cos_e, sin_e):
    outs = []
    dgain = jnp.zeros((1, HEAD_DIM), F32)
    for h in range(xv.shape[1] // HEAD_DIM):
        cols = slice(h * HEAD_DIM, (h + 1) * HEAD_DIM)
        xh, doh = xv[:, cols], dov[:, cols]
        r = _rms(xh)
        xhat = xh * r
        dy = doh * cos_e + _swap_pairs(doh * sin_e)
        dgain = dgain + _colsum(dy * xhat)
        outs.append(_rms_bwd(xhat, r, dy * gain))
    return (outs[0] if len(outs) == 1 else jnp.concatenate(outs, axis=1)), dgain


WEIGHTS = ['c_ctx', 'w_mod', 'b_mod', 'g_mix', 'g_mlp', 'w_in', 'q_gain', 'k_gain', 'conv_w', 'conv_b', 'w_rg', 'b_rg',
           'w_ig', 'b_ig', 'lru_lambda', 'w_o_attn', 'w_o_rnn', 'w_out', 'w_up', 'w_down', 'g_final']
REPLICATED = ['c_ctx', 'b_mod', 'g_mix', 'g_mlp', 'q_gain', 'k_gain', 'conv_b', 'w_rg', 'w_ig', 'g_final']
SMALL_SHARDED = [('conv_w', CONV_WIDTH), ('b_rg', 2), ('b_ig', 2), ('lru_lambda', 2)]
SMALL_ROWS = 16


PACK_ROWS = 512


def _pack_flat(arrs, rows_mult):
    flat = jnp.concatenate([a.reshape(-1) for a in arrs])
    n = flat.shape[0]
    unit = rows_mult * LANES
    pad = (-n) % unit
    if pad:
        flat = jnp.concatenate([flat, jnp.zeros((pad,), flat.dtype)])
    return flat.reshape(-1, LANES)


def _unpack_flat(packed, shapes):
    flat = packed.reshape(-1)
    out, pos = [], 0
    for shp in shapes:
        n = math.prod(shp)
        out.append(flat[pos:pos + n].reshape(shp))
        pos += n
    return out


def _pack_small(d):
    rows = [d[n][0] for n, _ in SMALL_SHARDED]
    cols = rows[0].shape[1]
    used = sum(k for _, k in SMALL_SHARDED)
    return jnp.concatenate(rows + [jnp.zeros((SMALL_ROWS - used, cols), F32)], axis=0)


def _unpack_small(p):
    out, pos = {}, 0
    for n, k in SMALL_SHARDED:
        out[n] = p[pos:pos + k][None]
        pos += k
    return out


def kernel(x, c, ctx, c_ctx, w_mod, b_mod, g_mix, g_mlp, w_in, q_gain, k_gain, conv_w, conv_b, w_rg, b_rg, w_ig, b_ig, lru_lambda, w_o_attn, w_o_rnn, w_out, w_up, w_down, g_final, loss_target, m_c_ctx, m_w_mod, m_b_mod, m_g_mix, m_g_mlp, m_w_in, m_q_gain, m_k_gain, m_conv_w, m_conv_b, m_w_rg, m_b_rg, m_w_ig, m_b_ig, m_lru_lambda, m_w_o_attn, m_w_o_rnn, m_w_out, m_w_up, m_w_down, m_g_final, v_c_ctx, v_w_mod, v_b_mod, v_g_mix, v_g_mlp, v_w_in, v_q_gain, v_k_gain, v_conv_w, v_conv_b, v_w_rg, v_b_rg, v_w_ig, v_b_ig, v_lru_lambda, v_w_o_attn, v_w_o_rnn, v_w_out, v_w_up, v_w_down, v_g_final):
    wd = dict(c_ctx=c_ctx, w_mod=w_mod, b_mod=b_mod, g_mix=g_mix, g_mlp=g_mlp, w_in=w_in, q_gain=q_gain, k_gain=k_gain,
              conv_w=conv_w, conv_b=conv_b, w_rg=w_rg, b_rg=b_rg, w_ig=w_ig, b_ig=b_ig, lru_lambda=lru_lambda,
              w_o_attn=w_o_attn, w_o_rnn=w_o_rnn, w_out=w_out, w_up=w_up, w_down=w_down, g_final=g_final)
    md = dict(zip(WEIGHTS, (m_c_ctx, m_w_mod, m_b_mod, m_g_mix, m_g_mlp, m_w_in, m_q_gain, m_k_gain, m_conv_w, m_conv_b,
                            m_w_rg, m_b_rg, m_w_ig, m_b_ig, m_lru_lambda, m_w_o_attn, m_w_o_rnn, m_w_out, m_w_up,
                            m_w_down, m_g_final)))
    vd = dict(zip(WEIGHTS, (v_c_ctx, v_w_mod, v_b_mod, v_g_mix, v_g_mlp, v_w_in, v_q_gain, v_k_gain, v_conv_w, v_conv_b,
                            v_w_rg, v_b_rg, v_w_ig, v_b_ig, v_lru_lambda, v_w_o_attn, v_w_o_rnn, v_w_out, v_w_up,
                            v_w_down, v_g_final)))

    xs, tgt, ctxs = x[0], loss_target[0], ctx[0]
    s_len, d = xs.shape
    l_len = ctxs.shape[0]
    t_len = s_len + l_len
    n_in = w_in.shape[2] * N_DEV
    aw = (n_in - 4 * d) * 2 // 3
    kvw = aw // Q_GROUP
    assert aw == d and n_in == aw + 2 * kvw + 4 * d
    nmod = N_MOD * d // N_DEV
    tr = 128
    assert l_len % tr == 0 and s_len % tr == 0
    lb = l_len // tr
    tq = min(256, s_len)
    my = _lin(_my_pos())
    c_ctx2 = c_ctx[None]
    g_final2 = g_final[None]

    c_all = _all_gather([jnp.broadcast_to(c, (SUBLANES, d))], [0], "ag_cond")[0][::SUBLANES]
    c16 = jnp.concatenate([c_all, jnp.broadcast_to(c_ctx2, (N_DEV, d))], axis=0)
    b_mod_sh = lax.dynamic_slice(b_mod, (0, my * nmod), (1, nmod))
    mod_part = _mm(c16, w_mod[0], a_fn=_silu, bias=b_mod_sh, name="mod_fwd")
    (mod_g,), order = _all_gather([mod_part], [1], "ag_mod", with_token=True)
    mod_lat = lax.dynamic_slice(mod_g, (my, 0), (1, N_MOD * d))
    sh_a, sc_a, ga_a, sh_f, sc_f, ga_f = [mod_lat[:, k * d:(k + 1) * d] for k in range(N_MOD)]
    csh_a, csc_a = mod_g[N_DEV:N_DEV + 1, :d], mod_g[N_DEV:N_DEV + 1, d:2 * d]

    sp_w = _pack_small(wd)
    (w_in_f, sp_f), order = _all_gather([_tok(w_in[0], order).astype(BF16), sp_w], [1, 1], "ag_w_in", with_token=True)
    conv_w_f, b_rg_f, b_ig_f, lam_f = sp_f[0:4], sp_f[4:6], sp_f[6:8], sp_f[8:10]
    late_axis = dict(w_o_attn=0, w_o_rnn=0, w_out=0, w_up=1, w_down=0)
    placed = [_place_block(wd[n][0], ax, my, "place_" + n) for n, ax in late_axis.items()]
    late_kind = lambda n: "gather_chips" if n in ("w_up", "w_down") else "gather"
    late = {}
    for (n, ax), full in zip(late_axis.items(), placed):
        late[n] = _xfer_start(late_kind(n), _tok(wd[n][0], order).astype(BF16), full, ax, "ag_start_" + n)
        order = late[n][4]
    g_mix_fwd = _tok(g_mix, order)

    def late_wait(n, after):
        full = _xfer_wait(late_kind(n), late[n], late_axis[n], after, "ag_wait_" + n)
        return _sibling_swap(full, late_axis[n], "ag_swap_" + n) if late_kind(n) == "gather_chips" else full

    pos = jnp.arange(s_len)
    n_freq = HEAD_DIM // 4
    inv_freq = ROPE_THETA ** (-jnp.arange(n_freq, dtype=F32) / n_freq)
    ang = jnp.concatenate([(pos // GRID_W).astype(F32)[:, None] * inv_freq,
                           (pos % GRID_W).astype(F32)[:, None] * inv_freq], axis=-1)
    sign = jnp.where(jnp.arange(HEAD_DIM) % 2 == 0, -1.0, 1.0).astype(F32)
    cos_e = jnp.concatenate([jnp.ones((l_len, HEAD_DIM), F32), jnp.repeat(jnp.cos(ang), 2, axis=-1)], axis=0)
    sin_e = jnp.concatenate([jnp.zeros((l_len, HEAD_DIM), F32), jnp.repeat(jnp.sin(ang), 2, axis=-1) * sign], axis=0)

    x_all = jnp.concatenate([ctxs, xs], axis=0)

    def nm1_fwd(i, j, xv, g, csh, csc, sh, sc):
        is_ctx = i < lb
        n = xv * _rms(xv) * g
        return n * (1.0 + jnp.where(is_ctx, csc, sc)) + jnp.where(is_ctx, csh, sh)

    h_all = _rowwise(nm1_fwd, [_row(x_all), _bc(g_mix_fwd), _bc(csh_a), _bc(csc_a), _bc(sh_a), _bc(sc_a)],
                     [("row", BF16)], nr=t_len // tr, tr=tr, nc=1, cw=d, name="norm1_fwd")[0]
    proj = _mm(h_all, w_in_f, name="proj_fwd")

    kb = aw // kvw
    q = _rowwise(lambda i, j, xv, gain, ce, se: _qk_fwd_heads(xv, gain, ce, se),
                 [_row(proj, 0, lb, aw), _full(q_gain), _row(cos_e, 0, lb, HEAD_DIM), _row(sin_e, 0, lb, HEAD_DIM)],
                 [("row", BF16, aw)], nr=s_len // tr, tr=tr, nc=1, cw=aw, name="q_fwd")[0]
    k_all, v_all = _rowwise(lambda i, j, kv_, vv_, gain, ce, se: (_qk_fwd_heads(kv_, gain, ce, se), vv_),
                            [_row(proj, kb, 0, kvw), _row(proj, kb + 1, 0, kvw), _full(k_gain),
                             _row(cos_e, 0, 0, HEAD_DIM), _row(sin_e, 0, 0, HEAD_DIM)],
                            [("row", BF16, kvw), ("row", BF16, kvw)], nr=t_len // tr, tr=tr, nc=1, cw=kvw, name="kv_fwd")
    attn_o = _attn_fwd(q, k_all, v_all, tq=tq, name="attn_fwd")

    xr_blk = (aw + 2 * kvw) // RNN_BLOCK
    xg_blk = xr_blk + d // RNN_BLOCK
    shift_masks = _shift_masks(l_len, t_len)
    rg, h_fwd, h_bwd = _rglru_fwd(proj, xr_blk, xg_blk, shift_masks, conv_w_f, conv_b, w_rg[0], w_ig[0], b_rg_f, b_ig_f, lam_f, l_len=l_len,
                    name="rglru_fwd")

    w_oa_f = late_wait("w_o_attn", attn_o)
    y_attn = _mm(attn_o, w_oa_f, name="o_attn_fwd")
    w_or_f = late_wait("w_o_rnn", rg)
    y_rnn = _mm(rg, w_or_f, name="o_rnn_fwd")
    hw = d // 2
    glb = (aw + 2 * kvw + 2 * d) // hw

    def merge_fwd(i, j, ya, yr, gla, glr):
        return _sigmoid(gla) * ya + _sigmoid(glr) * yr

    u = _rowwise(merge_fwd, [_row(y_attn), _row(y_rnn), _row(proj, glb, lb), _row(proj, glb + 2, lb)],
                 [("row", BF16)], nr=s_len // tr, tr=tr, nc=2, cw=hw, name="merge_fwd")[0]
    w_out_f = late_wait("w_out", u)
    mix = _mm(u, w_out_f, name="out_fwd")

    def res1_fwd(i, j, xv, mixv, gaa, gm, shf, scf):
        x1v = xv + gaa * mixv
        n = x1v * _rms(x1v) * gm
        return x1v, n * (1.0 + scf) + shf

    x1, h2 = _rowwise(res1_fwd, [_row(xs), _row(mix), _bc(ga_a), _bc(g_mlp), _bc(sh_f), _bc(sc_f)],
                      [("row", F32), ("row", BF16)], nr=s_len // tr, tr=tr, nc=1, cw=d, name="norm2_fwd")
    relu2 = lambda z: jnp.square(jnp.maximum(z, 0.0))
    w_up_f = late_wait("w_up", h2)
    up = _mm(h2, w_up_f, name="up_fwd")
    w_down_f = late_wait("w_down", up)
    down = _mm(up, w_down_f, a_fn=relu2, name="down_fwd")

    def final_fn(i, j, x1v, dn, tg, gaf, gf):
        x2 = x1v + gaf * dn
        r = _rms(x2)
        xh = x2 * r
        e = xh * gf - tg
        dy = e * (1.0 / d)
        dx2 = _rms_bwd(xh, r, dy * gf)
        return dx2, dx2 * gaf, _colsum(e * e) * (0.5 / d), _colsum(dy * xh), _colsum(dx2 * dn)

    dx2, ddown, loss_cols, g_gfinal, d_ga_f = _rowwise(
        final_fn, [_row(x1), _row(down), _row(tgt), _bc(ga_f), _bc(g_final2)],
        [("row", F32), ("row", BF16), ("acc", 1), ("acc", 1), ("acc", 1)], nr=s_len // tr, tr=tr, nc=1, cw=d,
        name="final_bwd")
    loss = lax.psum(jnp.sum(loss_cols), ("x", "y", "c"))

    dup = _mm(ddown, w_down_f, tb=True, out_fn=lambda r, upv: r * (2.0 * jnp.maximum(upv, 0.0)), extras=[up],
              out_dtype=BF16, name="down_bwd_x")
    gw_down = _mm(up, ddown, ta=True, a_fn=relu2, out_dtype=BF16, name="down_bwd_w")
    sent = dict(w_down=_scatter_start(gw_down, 0, my, "rs_start_w_down"))
    dh2 = _mm(dup, w_up_f, tb=True, name="up_bwd_x")
    gw_up = _mm(h2, dup, ta=True, out_dtype=BF16, name="up_bwd_w")
    sent["w_up"] = _scatter_start(gw_up, 1, my, "rs_start_w_up")
    sc_f_bwd = _tok(sc_f, sent["w_down"][0][4], sent["w_up"][0][4])

    def norm2_bwd(i, j, x1v, mixv, dh2v, dx2v, gaa, gm, scf):
        r = _rms(x1v)
        xh = x1v * r
        n = xh * gm
        dn = dh2v * (1.0 + scf)
        dx1 = dx2v + _rms_bwd(xh, r, dn * gm)
        return dx1, dx1 * gaa, _colsum(dh2v), _colsum(dh2v * n), _colsum(dn * xh), _colsum(dx1 * mixv)

    dx1, dmix, d_sh_f, d_sc_f, g_gmlp, d_ga_a = _rowwise(
        norm2_bwd, [_row(x1), _row(mix), _row(dh2), _row(dx2), _bc(ga_a), _bc(g_mlp), _bc(sc_f_bwd)],
        [("row", F32), ("row", BF16), ("acc", 1), ("acc", 1), ("acc", 1), ("acc", 1)], nr=s_len // tr, tr=tr, nc=1,
        cw=d, name="norm2_bwd")

    du = _mm(dmix, w_out_f, tb=True, name="out_bwd_x")
    gw_out = _mm(u, dmix, ta=True, out_dtype=BF16, name="out_bwd_w")
    sent["w_out"] = _scatter_start(gw_out, 0, my, "rs_start_w_out")

    def merge_bwd(i, j, duv, ya, yr, gla, glr):
        ga, gr = _sigmoid(gla), _sigmoid(glr)
        return duv * ga, duv * gr, duv * ya * ga * (1.0 - ga), duv * yr * gr * (1.0 - gr)

    dya, dyr, dgla, dglr = _rowwise(
        merge_bwd, [_row(du), _row(y_attn), _row(y_rnn), _row(proj, glb, lb), _row(proj, glb + 2, lb)],
        [("row", BF16)] * 4, nr=s_len // tr, tr=tr, nc=2, cw=hw, name="merge_bwd")
    dattn = _mm(dya, w_oa_f, tb=True, out_dtype=BF16, name="o_attn_bwd_x")
    gw_oa = _mm(attn_o, dya, ta=True, out_dtype=BF16, name="o_attn_bwd_w")
    drg = _mm(dyr, w_or_f, tb=True, out_dtype=BF16, name="o_rnn_bwd_x")
    gw_or = _mm(rg, dyr, ta=True, out_dtype=BF16, name="o_rnn_bwd_w")
    sent["w_o_attn"] = _scatter_start(gw_oa, 0, my, "rs_start_w_o_attn")
    sent["w_o_rnn"] = _scatter_start(gw_or, 0, my, "rs_start_w_o_rnn")
    conv_b_bwd = _tok(conv_b, sent["w_out"][0][4], sent["w_o_attn"][0][4], sent["w_o_rnn"][0][4])

    dq, dk, dv = _attn_bwd(q, k_all, v_all, dattn, tq=tq, name="attn_bwd")
    dxr, dxg, dvec, g_wrg, g_wig = _rglru_bwd(proj, xr_blk, xg_blk, shift_masks, h_fwd, h_bwd, drg, conv_w_f, conv_b_bwd, w_rg[0], w_ig[0], b_rg_f,
                                               b_ig_f, lam_f, l_len=l_len, name="rglru_bwd")
    gate_part = _pack_flat([g_wrg, g_wig], N_DEV * SUBLANES)
    gate_sent = _scatter_start(gate_part, 0, my, "rs_start_gates")
    q_gain_bwd = _tok(q_gain, gate_sent[0][4])

    dq_p, g_qgain = _rowwise(lambda i, j, xv, dov, gain, ce, se: _qk_bwd_heads(xv, dov, gain, ce, se),
                             [_row(proj, 0, lb, aw), _row(dq, 0, 0, aw), _full(q_gain_bwd), _row(cos_e, 0, lb, HEAD_DIM),
                              _row(sin_e, 0, lb, HEAD_DIM)],
                             [("row", BF16, aw), ("acc", 1, HEAD_DIM)], nr=s_len // tr, tr=tr, nc=1, cw=aw, name="q_bwd")

    def kv_bwd(i, j, xv, dkv, dvv, gain, ce, se):
        dk_p, dgain = _qk_bwd_heads(xv, dkv, gain, ce, se)
        return dk_p, dvv, dgain

    dk_p, dv_p, g_kgain = _rowwise(kv_bwd, [_row(proj, kb, 0, kvw), _row(dk, 0, 0, kvw), _row(dv, 0, 0, kvw),
                                            _full(k_gain), _row(cos_e, 0, 0, HEAD_DIM), _row(sin_e, 0, 0, HEAD_DIM)],
                                   [("row", BF16, kvw), ("row", BF16, kvw), ("acc", 1, HEAD_DIM)], nr=t_len // tr, tr=tr,
                                   nc=1, cw=kvw, name="kv_bwd")

    pad_ctx = lambda a: jnp.pad(a, ((l_len, 0), (0, 0)))
    dproj = jnp.concatenate([pad_ctx(dq_p), dk_p, dv_p, dxr, dxg, pad_ctx(dgla), pad_ctx(dglr)], axis=1)
    dh_all = _mm(dproj, w_in_f, tb=True, name="proj_bwd_x")
    g_mix_bwd = g_mix

    def norm1_bwd(i, j, xv, dhv, dresv, g, sc):
        r = _rms(xv)
        xh = xv * r
        dn = dhv * (1.0 + sc)
        return dresv + _rms_bwd(xh, r, dn * g), _colsum(dhv), _colsum(dhv * (xh * g)), _colsum(dn * xh)

    def norm1_bwd_ctx(i, j, xv, dhv, g, sc):
        xh = xv * _rms(xv)
        return _colsum(dhv), _colsum(dhv * (xh * g)), _colsum(dhv * (1.0 + sc) * xh)

    dx_lat, d_sh_a, d_sc_a, g_gmix_lat = _rowwise(
        norm1_bwd, [_row(x_all, 0, lb), _row(dh_all, 0, lb), _row(dx1), _bc(g_mix_bwd), _bc(sc_a)],
        [("row", F32)] + [("acc", 1)] * 3, nr=s_len // tr, tr=tr, nc=1, cw=d, name="norm1_bwd")
    d_csh, d_csc, g_gmix_ctx = _rowwise(
        norm1_bwd_ctx, [_row(x_all), _row(dh_all), _bc(g_mix_bwd), _bc(csc_a)],
        [("acc", 1)] * 3, nr=lb, tr=tr, nc=1, cw=d, name="norm1_bwd_ctx")
    g_gmix = g_gmix_lat + g_gmix_ctx
    grad_x = dx_lat[None]

    dmod_lat = jnp.concatenate([d_sh_a, d_sc_a, d_ga_a, d_sh_f, d_sc_f, d_ga_f], axis=1)
    dmod_ctx = jnp.concatenate([d_csh, d_csc, jnp.zeros((1, (N_MOD - 2) * d), F32)], axis=1)
    dm_g = _all_gather([jnp.concatenate([dmod_lat, dmod_ctx, jnp.zeros((SUBLANES - 2, N_MOD * d), F32)], axis=0)], [0],
                       "ag_dmod")[0]
    dm16 = jnp.concatenate([dm_g[0::SUBLANES], dm_g[1::SUBLANES]], axis=0)
    dm_sh = lax.dynamic_slice(dm16, (0, my * nmod), (2 * N_DEV, nmod))
    gw_mod = _mm(c16, dm_sh, ta=True, a_fn=_silu, name="mod_bwd_w")
    dc16 = _mm(dm_sh, w_mod[0], tb=True, name="mod_bwd_x")

    def cctx_fn(i, j, pv, cc, dl, dc):
        row = lax.broadcasted_iota(jnp.int32, pv.shape, 0)
        return _colsum(jnp.where(row >= N_DEV, pv, 0.0)) * _dsilu(cc), dl + dc

    g_cctx, g_bmod = _rowwise(cctx_fn, [_full(dc16), _full(c_ctx2), _full(dmod_lat), _full(dmod_ctx)],
                              [("row", F32, d), ("row", F32, N_MOD * d)], nr=1, tr=1, nc=1, cw=d, name="cctx_bwd")

    rep_part = dict(c_ctx=g_cctx, b_mod=g_bmod, g_mix=g_gmix, g_mlp=g_gmlp, q_gain=g_qgain, k_gain=g_kgain,
                    conv_b=dvec[4:5], g_final=g_gfinal)
    rep_small = [n for n in REPLICATED if n in rep_part]
    gates = [n for n in REPLICATED if n not in rep_part]
    part_pack = _pack_flat([rep_part[n] for n in rep_small], PACK_ROWS)
    rep_rows = part_pack.shape[0]
    gate_land = _xfer_wait("scatter", gate_sent[0], 0, g_bmod, "rs_wait_gates")
    gate_sum = _sum_parts(gate_sent[1], gate_land, name="gate_sum")
    (gate_full, parts), order = _all_gather([gate_sum, part_pack], [0, 0], "ag_rep_grads", with_token=True)
    small_g = jnp.concatenate([dvec[0:4], dvec[5:11], jnp.zeros((SMALL_ROWS - 10, d), F32)], axis=0)
    (small_recv,), order = _all_to_all([_tok(small_g, order)], [1], "a2a_small_grads")

    gw_in = _mm(h_all, dproj, ta=True, out_dtype=BF16, name="proj_bwd_w", bias=_tok(jnp.zeros((1, n_in), F32), order))
    sent["w_in"] = _chip_scatter_start(_pair_reduce(gw_in, my, "pair_w_in"), my, "rs_start_w_in")
    started = sent["w_in"][0][4]

    res = {}
    for names, pr, mult, nm in ((rep_small, parts.reshape(N_DEV, rep_rows, LANES), PACK_ROWS, "adamw_rep"),
                                (gates, gate_full[None], N_DEV * SUBLANES, "adamw_gates")):
        rep_out = _adamw(pr, _tok(_pack_flat([wd[n] for n in names], mult), started),
                         _pack_flat([md[n] for n in names], mult), _pack_flat([vd[n] for n in names], mult), name=nm)
        for kind, packed in zip(("g", "d", "m", "v"), rep_out):
            for n, a in zip(names, _unpack_flat(packed, [wd[n].shape for n in names])):
                res[kind, n] = a
    out = _adamw(small_recv, _tok(sp_w, started), _pack_small(md), _pack_small(vd), name="adamw_small")
    for kind, a in zip(("g", "d", "m", "v"), out):
        for n, piece in _unpack_small(a).items():
            res[kind, n] = piece

    after = started
    for n, (handle, own, ax, kind) in sent.items():
        land = _xfer_wait(kind, handle, ax, after, "rs_wait_" + n)
        out = _adamw(land, wd[n][0], md[n][0], vd[n][0], own=own, name="adamw_" + n)
        after = out[0]
        for kind, a in zip(("g", "d", "m", "v"), out):
            res[kind, n] = a[None]
        if n == "w_o_rnn":
            out = _adamw(gw_mod[None], w_mod[0], m_w_mod[0], v_w_mod[0], name="adamw_w_mod")
            after = out[0]
            for kind, a in zip(("g", "d", "m", "v"), out):
                res[kind, 'w_mod'] = a[None]

    outs = [loss, grad_x]
    for kind in ("g", "d", "m", "v"):
        outs += [res[kind, n].reshape(wd[n].shape) for n in WEIGHTS]
    return tuple(outs)
```

```python
import functools
import math

import jax
import jax.numpy as jnp
from jax import lax
from jax.experimental import pallas as pl
from jax.experimental.pallas import tpu as pltpu

F32 = jnp.float32
BF16 = jnp.bfloat16
MESH = pl.DeviceIdType.MESH

N_DEV = 8
HEAD_DIM = 128
Q_GROUP = 4
RNN_BLOCK = 128
GRID_W = 64
CONV_WIDTH = 4
LRU_C = 8.0
ROPE_THETA = 10000.0
NORM_EPS = 1e-6
N_MOD = 6
ADAM_LR = 0.001
ADAM_B1 = 0.9
ADAM_B2 = 0.999
ADAM_EPS = 1e-08
ADAM_WD = 0.01
ADAM_STEP = 10

LANES = 128
SUBLANES = 8
VMEM_LIMIT = 56 * 1024 * 1024
HBM_SPEC = pl.BlockSpec(memory_space=pltpu.HBM)


def _params(sem=None, vmem=VMEM_LIMIT):
    kw = dict(vmem_limit_bytes=vmem)
    if sem is not None:
        kw["dimension_semantics"] = sem
    return pltpu.CompilerParams(**kw)


def _pick(dim, target, mult):
    for t in range(min(dim, target), 0, -1):
        if dim % t == 0 and t % mult == 0:
            return t
    return dim


def _my_pos():
    return lax.axis_index("x"), lax.axis_index("y"), lax.axis_index("c")


def _flip(pos, d):
    x, y, c = pos
    return (1 - x if d & 4 else x, 1 - y if d & 2 else y, 1 - c if d & 1 else c)


def _lin(pos):
    return 4 * pos[0] + 2 * pos[1] + pos[2]


def _slab(ref, axis, idx, size):
    if axis == 0:
        return ref.at[pl.ds(idx * size, size)]
    assert axis == 1
    return ref.at[:, pl.ds(idx * size, size)]


def _all_gather(arrs, axes, name, with_token=False):
    n = len(arrs)
    sizes = [a.shape[ax] for a, ax in zip(arrs, axes)]

    def body(*refs):
        ins, outs, token = refs[:n], refs[n:2 * n], refs[2 * n]
        send_sems, recv_sems, local_sems = refs[2 * n + 1:]
        token[...] = jnp.zeros_like(token)
        me = _my_pos()
        sibling = _flip(me, 1)
        chips = [2, 4, 6]

        def copy(a, k, block_pos, to, src=None):
            dst = _slab(outs[a], axes[a], _lin(block_pos), sizes[a])
            return pltpu.make_async_remote_copy(
                src_ref=dst if src is None else src, dst_ref=dst,
                send_sem=send_sems.at[7 * a + k], recv_sem=recv_sems.at[7 * a + k],
                device_id=to, device_id_type=MESH)

        local, first, passed = [], [], []
        for a in range(n):
            cp = pltpu.make_async_copy(ins[a], _slab(outs[a], axes[a], _lin(me), sizes[a]), local_sems.at[a])
            cp.start()
            local.append(cp)
        for a in range(n):
            f = [copy(a, 0, me, sibling, src=ins[a])]
            f += [copy(a, 1 + j, me, _flip(me, m), src=ins[a]) for j, m in enumerate(chips)]
            for cp in f:
                cp.start()
            first += f
        for a in range(n):
            for j, m in enumerate(chips):
                copy(a, 1 + j, _flip(me, m), me).wait_recv()
                cp = copy(a, 4 + j, _flip(me, m), sibling)
                cp.start()
                passed.append(cp)
        for a in range(n):
            copy(a, 0, sibling, me).wait_recv()
            for j, m in enumerate(chips):
                copy(a, 4 + j, _flip(sibling, m), me).wait_recv()
        for cp in first + passed:
            cp.wait_send()
        for cp in local:
            cp.wait()

    out_shape = []
    for a, ax in zip(arrs, axes):
        shp = list(a.shape)
        shp[ax] *= N_DEV
        out_shape.append(jax.ShapeDtypeStruct(tuple(shp), a.dtype))
    out_shape.append(jax.ShapeDtypeStruct((SUBLANES, LANES), F32))
    res = pl.pallas_call(
        body, name=name, out_shape=tuple(out_shape),
        in_specs=[HBM_SPEC] * n, out_specs=tuple([HBM_SPEC] * n + [pl.BlockSpec(memory_space=pltpu.VMEM)]),
        scratch_shapes=[pltpu.SemaphoreType.DMA((7 * n,)), pltpu.SemaphoreType.DMA((7 * n,)),
                        pltpu.SemaphoreType.DMA((n,))],
    )(*arrs)
    return (list(res[:n]), res[n]) if with_token else list(res[:n])


def _all_to_all(arrs, axes, name):
    n = len(arrs)
    sizes = [a.shape[ax] // N_DEV for a, ax in zip(arrs, axes)]

    def body(*refs):
        ins, outs, token = refs[:n], refs[n:2 * n], refs[2 * n]
        send_sems, recv_sems, local_sems = refs[2 * n + 1:]
        token[...] = jnp.zeros_like(token)
        me = _my_pos()
        my = _lin(me)
        copies, local = [], []
        for a in range(n):
            cp = pltpu.make_async_copy(_slab(ins[a], axes[a], my, sizes[a]), outs[a].at[my], local_sems.at[a])
            cp.start()
            local.append(cp)
        for d in range(1, N_DEV):
            peer = _flip(me, d)
            for a in range(n):
                cp = pltpu.make_async_remote_copy(
                    src_ref=_slab(ins[a], axes[a], _lin(peer), sizes[a]), dst_ref=outs[a].at[my],
                    send_sem=send_sems.at[7 * a + d - 1], recv_sem=recv_sems.at[7 * a + d - 1],
                    device_id=peer, device_id_type=MESH)
                cp.start()
                copies.append((a, d, cp))
        for a, d, cp in copies:
            peer = _flip(me, d)
            pltpu.make_async_remote_copy(
                src_ref=_slab(ins[a], axes[a], _lin(peer), sizes[a]), dst_ref=outs[a].at[_lin(peer)],
                send_sem=send_sems.at[7 * a + d - 1], recv_sem=recv_sems.at[7 * a + d - 1],
                device_id=peer, device_id_type=MESH).wait_recv()
            cp.wait_send()
        for cp in local:
            cp.wait()

    out_shape = []
    for a, ax, s in zip(arrs, axes, sizes):
        shp = list(a.shape)
        shp[ax] = s
        out_shape.append(jax.ShapeDtypeStruct((N_DEV,) + tuple(shp), a.dtype))
    out_shape.append(jax.ShapeDtypeStruct((SUBLANES, LANES), F32))
    res = pl.pallas_call(
        body, name=name, out_shape=tuple(out_shape),
        in_specs=[HBM_SPEC] * n, out_specs=tuple([HBM_SPEC] * n + [pl.BlockSpec(memory_space=pltpu.VMEM)]),
        scratch_shapes=[pltpu.SemaphoreType.DMA((7 * n,)), pltpu.SemaphoreType.DMA((7 * n,)),
                        pltpu.SemaphoreType.DMA((n,))],
    )(*arrs)
    return list(res[:n]), res[n]


SEM_SPEC = pl.BlockSpec(memory_space=pltpu.SEMAPHORE)
_EFFECT = pltpu.SideEffectType.DATAFLOW_SIDE_EFFECTING
N_PEERS = N_DEV - 1


def _split_copies(kind, src_ref, land_ref, send_sems, recv_sems, axis, size, arriving):
    me = _my_pos()
    out = []
    for dd in (range(2, N_DEV, 2) if kind in ("chips", "gather_chips") else range(1, N_DEV)):
        peer = _flip(me, dd)
        k = dd - 1
        if kind in ("gather", "gather_chips"):
            src, dst = src_ref, _slab(land_ref, axis, _lin(peer if arriving else me), size)
        elif kind == "scatter":
            src, dst = _slab(src_ref, axis, _lin(peer), size), land_ref.at[k]
        else:
            k = dd // 2 - 1
            src, dst = src_ref.at[2 * peer[0] + peer[1]], land_ref.at[k]
        out.append(pltpu.make_async_remote_copy(
            src_ref=src, dst_ref=dst, send_sem=send_sems.at[k], recv_sem=recv_sems.at[k],
            device_id=peer, device_id_type=MESH))
    return out


N_CHIPS = N_DEV // 2


def _split_size(kind, src_shape, axis):
    return src_shape[axis] // {"gather": 1, "gather_chips": 1, "scatter": N_DEV, "chips": N_CHIPS}[kind]


def _sibling_swap(full, axis, name):
    size = full.shape[axis] // N_DEV

    def body(in_ref, o_ref, send_sems, recv_sems):
        me = _my_pos()
        sibling = _flip(me, 1)
        cps = []
        for k in range(N_CHIPS):
            mine = _slab(o_ref, axis, 2 * k + me[2], size)
            cp = pltpu.make_async_remote_copy(src_ref=mine, dst_ref=mine, send_sem=send_sems.at[k],
                                              recv_sem=recv_sems.at[k], device_id=sibling, device_id_type=MESH)
            cp.start()
            cps.append(cp)
        for k, cp in enumerate(cps):
            cp.wait_send()
            theirs = _slab(o_ref, axis, 2 * k + sibling[2], size)
            pltpu.make_async_remote_copy(src_ref=theirs, dst_ref=theirs, send_sem=send_sems.at[k],
                                         recv_sem=recv_sems.at[k], device_id=sibling, device_id_type=MESH).wait_recv()

    return pl.pallas_call(
        body, name=name, out_shape=jax.ShapeDtypeStruct(full.shape, full.dtype), in_specs=[HBM_SPEC], out_specs=HBM_SPEC,
        scratch_shapes=[pltpu.SemaphoreType.DMA((N_CHIPS,)), pltpu.SemaphoreType.DMA((N_CHIPS,))],
        input_output_aliases={0: 0},
    )(full)


def _xfer_start(kind, src, land, axis, name):
    size = _split_size(kind, src.shape, axis)

    def body(src_ref, land_ref, send_sems, recv_sems, src_thru, land_thru, token):
        for cp in _split_copies(kind, src_ref, land_ref, send_sems, recv_sems, axis, size, False):
            cp.start()
        token[...] = jnp.zeros_like(token)

    return pl.pallas_call(
        body, name=name,
        out_shape=(pltpu.SemaphoreType.DMA((N_PEERS,)), pltpu.SemaphoreType.DMA((N_PEERS,)),
                   pltpu.HBM(src.shape, src.dtype), pltpu.HBM(land.shape, land.dtype),
                   jax.ShapeDtypeStruct((SUBLANES, LANES), F32)),
        in_specs=(HBM_SPEC, HBM_SPEC),
        out_specs=(SEM_SPEC, SEM_SPEC, HBM_SPEC, HBM_SPEC, pl.BlockSpec(memory_space=pltpu.VMEM)),
        input_output_aliases={0: 2, 1: 3},
        compiler_params=pltpu.CompilerParams(has_side_effects=_EFFECT),
    )(pltpu.with_memory_space_constraint(src, pltpu.HBM), pltpu.with_memory_space_constraint(land, pltpu.HBM))


def _xfer_wait(kind, handle, axis, after, name):
    send_sems, recv_sems, src_thru, land_thru, _ = handle
    size = _split_size(kind, src_thru.shape, axis)

    def body(src_ref, land_ref, send_sems, recv_sems, after_ref, src_dead, got_ref):
        for cp in _split_copies(kind, src_ref, land_ref, send_sems, recv_sems, axis, size, True):
            cp.wait_send()
            cp.wait_recv()

    return pl.pallas_call(
        body, name=name,
        out_shape=(pltpu.HBM(src_thru.shape, src_thru.dtype), pltpu.HBM(land_thru.shape, land_thru.dtype)),
        in_specs=(HBM_SPEC, HBM_SPEC, SEM_SPEC, SEM_SPEC, pl.BlockSpec(memory_space=pl.ANY)),
        out_specs=(HBM_SPEC, HBM_SPEC), input_output_aliases={0: 0, 1: 1},
        compiler_params=pltpu.CompilerParams(has_side_effects=_EFFECT),
    )(src_thru, land_thru, send_sems, recv_sems, after)[1]


def _place_block(shard, axis, my, name):
    r, c = shard.shape
    tr = _pick(r, max(16, (1 << 19) // c), 16)
    nr = r // tr
    if axis == 0:
        full, out_map = (r * N_DEV, c), (lambda i, my_ref: (my_ref[0] * nr + i, 0))
    else:
        full, out_map = (r, c * N_DEV), (lambda i, my_ref: (i, my_ref[0]))

    def body(my_ref, x_ref, o_ref):
        o_ref[...] = x_ref[...].astype(BF16)

    return pl.pallas_call(
        body, name=name, out_shape=jax.ShapeDtypeStruct(full, BF16),
        grid_spec=pltpu.PrefetchScalarGridSpec(
            num_scalar_prefetch=1, grid=(nr,), in_specs=[pl.BlockSpec((tr, c), lambda i, my_ref: (i, 0))],
            out_specs=pl.BlockSpec((tr, c), out_map)),
        compiler_params=_params(("arbitrary",)),
    )(jnp.reshape(my, (1,)).astype(jnp.int32), shard)


def _scatter_start(grad, axis, my, name):
    shp = list(grad.shape)
    shp[axis] //= N_DEV
    own = lax.dynamic_slice_in_dim(grad, my * shp[axis], shp[axis], axis)
    return (_xfer_start("scatter", grad, lax.empty((N_PEERS,) + tuple(shp), grad.dtype), axis, name), own, axis,
            "scatter")


def _pair_reduce(grad, my, name):
    r, cols = grad.shape
    sz = cols // N_DEV

    def exchange(g_ref, got_ref, send_sems, recv_sems):
        me = _my_pos()
        sibling = _flip(me, 1)
        cps = []
        for k in range(N_CHIPS):
            cp = pltpu.make_async_remote_copy(
                src_ref=_slab(g_ref, 1, 2 * k + sibling[2], sz), dst_ref=_slab(got_ref, 1, k, sz),
                send_sem=send_sems.at[k], recv_sem=recv_sems.at[k], device_id=sibling, device_id_type=MESH)
            cp.start()
            cps.append(cp)
        for cp in cps:
            cp.wait()

    got = pl.pallas_call(
        exchange, name=name + "_swap", out_shape=jax.ShapeDtypeStruct((r, N_CHIPS * sz), grad.dtype),
        in_specs=[HBM_SPEC], out_specs=HBM_SPEC,
        scratch_shapes=[pltpu.SemaphoreType.DMA((N_CHIPS,)), pltpu.SemaphoreType.DMA((N_CHIPS,))],
    )(grad)

    tr = _pick(r, 512, 16)

    def add(c_ref, g_ref, p_ref, o_ref):
        o_ref[...] = (g_ref[...].astype(F32) + p_ref[...].astype(F32)).astype(o_ref.dtype)

    def add(c_ref, g_ref, p_ref, o_ref):
        o_ref[0] = (g_ref[...].astype(F32) + p_ref[...].astype(F32)).astype(o_ref.dtype)

    return pl.pallas_call(
        add, name=name + "_sum", out_shape=jax.ShapeDtypeStruct((N_CHIPS, r, sz), grad.dtype),
        grid_spec=pltpu.PrefetchScalarGridSpec(
            num_scalar_prefetch=1, grid=(N_CHIPS, r // tr),
            in_specs=[pl.BlockSpec((tr, sz), lambda k, i, c_ref: (i, 2 * k + c_ref[0])),
                      pl.BlockSpec((tr, sz), lambda k, i, c_ref: (i, k))],
            out_specs=pl.BlockSpec((1, tr, sz), lambda k, i, c_ref: (k, i, 0))),
        compiler_params=_params(("parallel", "parallel")),
    )(jnp.reshape(my % 2, (1,)).astype(jnp.int32), grad, got)


def _chip_scatter_start(pair_sum, my, name):
    _, r, sz = pair_sum.shape
    own = lax.dynamic_index_in_dim(pair_sum, my // 2, 0, keepdims=False)
    return (_xfer_start("chips", pair_sum, lax.empty((N_CHIPS - 1, r, sz), pair_sum.dtype), 0, name), own, 0, "chips")


def _tok(a, *tokens):
    for t in tokens:
        a = a + t[0, 0].astype(a.dtype)
    return a


def _mm(a, b, *, name, ta=False, tb=False, out_dtype=F32, a_fn=None, out_fn=None, extras=(), bias=None,
        tm=1024, tn=1024, tk=2816):
    if ta:
        kdim, m = a.shape
    else:
        m, kdim = a.shape
    if tb:
        n, kb = b.shape
    else:
        kb, n = b.shape
    assert kb == kdim, (a.shape, b.shape, ta, tb)
    tm = _pick(m, tm, LANES if ta else SUBLANES)
    tn = _pick(n, tn, LANES)
    tk = _pick(kdim, tk, LANES)
    nk = kdim // tk
    ne = len(extras)
    has_bias = bias is not None

    def body(*refs):
        a_ref, b_ref = refs[0], refs[1]
        pos = 2
        bias_ref = None
        if has_bias:
            bias_ref = refs[pos]
            pos += 1
        extra_refs = refs[pos:pos + ne]
        o_ref = refs[pos + ne]
        av = a_ref[...]
        if a_fn is not None:
            av = a_fn(av)
        dn = (((0 if ta else 1,), (1 if tb else 0,)), ((), ()))
        part = lax.dot_general(av.astype(BF16), b_ref[...].astype(BF16), dn, preferred_element_type=F32)

        def finish(r):
            if has_bias:
                r = r + bias_ref[...]
            if out_fn is not None:
                r = out_fn(r, *[e[...] for e in extra_refs])
            o_ref[...] = r.astype(out_dtype)

        if nk == 1:
            finish(part)
            return
        acc = refs[pos + ne + 1]
        k = pl.program_id(2)

        @pl.when(k == 0)
        def _():
            acc[...] = part

        @pl.when(k > 0)
        def _():
            acc[...] += part

        @pl.when(k == nk - 1)
        def _():
            finish(acc[...])

    in_specs = [
        pl.BlockSpec((tk, tm), lambda i, j, k: (k, i)) if ta else pl.BlockSpec((tm, tk), lambda i, j, k: (i, k)),
        pl.BlockSpec((tn, tk), lambda i, j, k: (j, k)) if tb else pl.BlockSpec((tk, tn), lambda i, j, k: (k, j)),
    ]
    args = [a, b]
    if has_bias:
        in_specs.append(pl.BlockSpec((1, tn), lambda i, j, k: (0, j)))
        args.append(bias)
    for e in extras:
        in_specs.append(pl.BlockSpec((tm, tn), lambda i, j, k: (i, j)))
        args.append(e)
    return pl.pallas_call(
        body, name=name, grid=(m // tm, n // tn, nk),
        in_specs=in_specs, out_specs=pl.BlockSpec((tm, tn), lambda i, j, k: (i, j)),
        out_shape=jax.ShapeDtypeStruct((m, n), out_dtype),
        scratch_shapes=[pltpu.VMEM((tm, tn), F32)] if nk > 1 else [],
        compiler_params=_params(("parallel", "parallel", "arbitrary")),
    )(*args)


def _row(arr, cb=0, rb=0, w=None):
    return ("row", arr, cb, rb, w)


def _bc(arr, cb=0, w=None):
    return ("bc", arr, cb, 0, w)


def _full(arr):
    return ("full", arr, 0, 0, None)


def _rowwise(fn, ins, outs, *, nr, tr, nc, cw, name):
    n_in = len(ins)
    outs = [(o[0], o[1], (o[2] if len(o) > 2 and o[2] else cw)) for o in outs]

    def body(*refs):
        in_refs, out_refs = refs[:n_in], refs[n_in:]
        j, i = pl.program_id(0), pl.program_id(1)
        res = fn(i, j, *[r[...] for r in in_refs])
        if not isinstance(res, (tuple, list)):
            res = (res,)
        for (kind, _, _), o_ref, r in zip(outs, out_refs, res):
            if kind == "row":
                o_ref[...] = r.astype(o_ref.dtype)
            else:
                @pl.when(i == 0)
                def _(o_ref=o_ref):
                    o_ref[...] = jnp.zeros_like(o_ref)

                o_ref[...] += r

    in_specs, args = [], []
    for kind, arr, cb, rb, w in ins:
        w = w or cw
        if kind == "row":
            in_specs.append(pl.BlockSpec((tr, w), lambda j, i, cb=cb, rb=rb: (i + rb, j + cb)))
        elif kind == "bc":
            in_specs.append(pl.BlockSpec((arr.shape[0], w), lambda j, i, cb=cb: (0, j + cb)))
        else:
            in_specs.append(pl.BlockSpec(arr.shape, lambda j, i, nd=arr.ndim: (0,) * nd))
        args.append(arr)
    out_specs, out_shape = [], []
    for kind, spec, w in outs:
        if kind == "row":
            out_specs.append(pl.BlockSpec((tr, w), lambda j, i: (i, j)))
            out_shape.append(jax.ShapeDtypeStruct((nr * tr, nc * w), spec))
        else:
            out_specs.append(pl.BlockSpec((spec, w), lambda j, i: (0, j)))
            out_shape.append(jax.ShapeDtypeStruct((spec, nc * w), F32))
    res = pl.pallas_call(
        body, name=name, grid=(nc, nr), in_specs=in_specs, out_specs=tuple(out_specs), out_shape=tuple(out_shape),
        compiler_params=_params(("parallel", "arbitrary")),
    )(*args)
    return res


def _sum_parts(own, parts, *, name):
    p, r, c = parts.shape
    tr = _pick(r, max(SUBLANES, (1 << 18) // c), SUBLANES)

    def body(o_ref, p_ref, g_out):
        g = o_ref[...].astype(F32)
        for q in range(p):
            g = g + p_ref[q].astype(F32)
        g_out[...] = g

    blk = pl.BlockSpec((tr, c), lambda i: (i, 0))
    return pl.pallas_call(
        body, name=name, grid=(r // tr,), in_specs=[blk, pl.BlockSpec((p, tr, c), lambda i: (0, i, 0))],
        out_specs=blk, out_shape=jax.ShapeDtypeStruct((r, c), F32), compiler_params=_params(("parallel",)),
    )(own, parts)


def _adamw(parts, w, m, v, *, name, own=None):
    p, r, c = parts.shape
    tr = _pick(r, max(SUBLANES, (1 << 18) // c), 16 if parts.dtype == BF16 else SUBLANES)
    bc1 = 1.0 - ADAM_B1 ** ADAM_STEP
    bc2 = 1.0 - ADAM_B2 ** ADAM_STEP
    has_own = own is not None

    def body(*refs):
        p_ref, w_ref, m_ref, v_ref = refs[:4]
        g_out, d_out, m_out, v_out = refs[4 + has_own:]
        if has_own:
            g = refs[4][...].astype(F32) + p_ref[0].astype(F32)
        else:
            g = p_ref[0].astype(F32)
        for q in range(1, p):
            g = g + p_ref[q].astype(F32)
        m_new = ADAM_B1 * m_ref[...] + (1.0 - ADAM_B1) * g
        v_new = ADAM_B2 * v_ref[...] + (1.0 - ADAM_B2) * (g * g)
        m_hat = m_new / bc1
        v_hat = v_new / bc2
        g_out[...] = g
        d_out[...] = -ADAM_LR * (m_hat / (jnp.sqrt(v_hat) + ADAM_EPS) + ADAM_WD * w_ref[...])
        m_out[...] = m_new
        v_out[...] = v_new

    blk = pl.BlockSpec((tr, c), lambda i: (i, 0))
    return pl.pallas_call(
        body, name=name, grid=(r // tr,),
        in_specs=[pl.BlockSpec((p, tr, c), lambda i: (0, i, 0)), blk, blk, blk] + ([blk] if has_own else []),
        out_specs=(blk, blk, blk, blk), out_shape=tuple(jax.ShapeDtypeStruct((r, c), F32) for _ in range(4)),
        compiler_params=_params(("parallel",)),
    )(*([parts, w, m, v] + ([own] if has_own else [])))


_GELU_C = math.sqrt(2.0 / math.pi)
_GELU_K = 0.044715


def _sigmoid(z):
    return 0.5 * jnp.tanh(0.5 * z) + 0.5


def _silu(z):
    return z * _sigmoid(z)


def _dsilu(z):
    s = _sigmoid(z)
    return s * (1.0 + z * (1.0 - s))


def _gelu(z):
    return 0.5 * z * (1.0 + jnp.tanh(_GELU_C * (z + _GELU_K * z * z * z)))


def _dgelu(z):
    t = jnp.tanh(_GELU_C * (z + _GELU_K * z * z * z))
    return 0.5 * (1.0 + t) + 0.5 * z * (1.0 - t * t) * (_GELU_C * (1.0 + 3.0 * _GELU_K * z * z))


def _softplus(z):
    return jnp.maximum(z, 0.0) + jnp.log(1.0 + jnp.exp(-jnp.abs(z)))


def _expm1(z):
    small = z * (1.0 + z * (0.5 + z * (1.0 / 6.0 + z * (1.0 / 24.0))))
    return jnp.where(jnp.abs(z) < 0.03, small, jnp.exp(z) - 1.0)


def _rms(xv):
    return lax.rsqrt(jnp.mean(xv * xv, axis=-1, keepdims=True) + NORM_EPS)


def _rms_bwd(xhat, r, dxhat):
    return r * (dxhat - xhat * jnp.mean(dxhat * xhat, axis=-1, keepdims=True))


def _swap_pairs(y):
    lane = lax.broadcasted_iota(jnp.int32, y.shape, 1)
    return jnp.where(lane % 2 == 0, pltpu.roll(y, LANES - 1, 1), pltpu.roll(y, 1, 1))


def _dot(a, b, ca, cb):
    return lax.dot_general(a, b, (((ca,), (cb,)), ((), ())), preferred_element_type=F32)


_SM_SCALE = HEAD_DIM ** -0.5
_EXP2_SCALE = _SM_SCALE * math.log2(math.e)


def _softmax_parts(qh, kk):
    s = _dot(qh, kk, 1, 1)
    p = jnp.exp2((s - jnp.max(s, axis=-1, keepdims=True)) * _EXP2_SCALE)
    return p, 1.0 / jnp.sum(p, axis=-1, keepdims=True)


def _attn_fwd(q, k, v, *, tq, name):
    s_len, aw = q.shape
    t_len, kvw = k.shape
    gw = Q_GROUP * HEAD_DIM

    def body(q_ref, k_ref, v_ref, o_ref):
        kk, vv = k_ref[...], v_ref[...]
        for g in range(Q_GROUP):
            cols = slice(g * HEAD_DIM, (g + 1) * HEAD_DIM)
            p, inv_l = _softmax_parts(q_ref[:, cols], kk)
            o_ref[:, cols] = (_dot(p.astype(BF16), vv, 1, 0) * inv_l).astype(BF16)

    return pl.pallas_call(
        body, name=name, grid=(kvw // HEAD_DIM, s_len // tq),
        in_specs=[pl.BlockSpec((tq, gw), lambda h, i: (i, h)), pl.BlockSpec((t_len, HEAD_DIM), lambda h, i: (0, h)),
                  pl.BlockSpec((t_len, HEAD_DIM), lambda h, i: (0, h))],
        out_specs=pl.BlockSpec((tq, gw), lambda h, i: (i, h)),
        out_shape=jax.ShapeDtypeStruct((s_len, aw), BF16),
        compiler_params=_params(("parallel", "parallel")),
    )(q, k, v)


def _attn_bwd(q, k, v, do, *, tq, name):
    s_len, aw = q.shape
    t_len, kvw = k.shape
    gw = Q_GROUP * HEAD_DIM

    def body(q_ref, k_ref, v_ref, do_ref, dq_ref, dk_ref, dv_ref):
        @pl.when(pl.program_id(1) == 0)
        def _():
            dk_ref[...] = jnp.zeros_like(dk_ref)
            dv_ref[...] = jnp.zeros_like(dv_ref)

        kk, vv = k_ref[...], v_ref[...]
        for g in range(Q_GROUP):
            cols = slice(g * HEAD_DIM, (g + 1) * HEAD_DIM)
            qh, doh = q_ref[:, cols], do_ref[:, cols]
            p, inv_l = _softmax_parts(qh, kk)
            dp = _dot(doh, vv, 1, 1)
            delta = jnp.sum(p * dp, axis=-1, keepdims=True) * inv_l
            dsb = (p * ((dp - delta) * (inv_l * _SM_SCALE))).astype(BF16)
            dq_ref[:, cols] = _dot(dsb, kk, 1, 0)
            dk_ref[...] += _dot(dsb, qh, 0, 0)
            dv_ref[...] += _dot(p.astype(BF16), (doh * inv_l).astype(BF16), 0, 0)

    qspec = pl.BlockSpec((tq, gw), lambda h, i: (i, h))
    kspec = pl.BlockSpec((t_len, HEAD_DIM), lambda h, i: (0, h))
    return pl.pallas_call(
        body, name=name, grid=(kvw // HEAD_DIM, s_len // tq),
        in_specs=[qspec, kspec, kspec, qspec], out_specs=(qspec, kspec, kspec),
        out_shape=(jax.ShapeDtypeStruct((s_len, aw), F32), jax.ShapeDtypeStruct((t_len, kvw), F32),
                   jax.ShapeDtypeStruct((t_len, kvw), F32)),
        compiler_params=_params(("parallel", "arbitrary")),
    )(q, k, v, do)


SHIFTS = (-2, -1, 1, 2)


def _shift_masks(l_len, t_len):
    t = jnp.arange(t_len)[:, None]
    cols = []
    for d in SHIFTS:
        td = t + d
        ok = ((t < l_len) & (td >= 0) & (td < l_len)) | ((t >= l_len) & (td >= l_len) & (td < t_len))
        cols.append(jnp.broadcast_to(ok.astype(F32), (t_len, RNN_BLOCK)))
    return jnp.concatenate(cols, axis=1)


def _shift_rows(xv, d, m_ref, l_len, t_len):
    if d == 0:
        return xv
    k = SHIFTS.index(d)
    return pltpu.roll(xv, (-d) % t_len, 0) * m_ref[:, k * RNN_BLOCK:(k + 1) * RNN_BLOCK]


def _sublane_scan(a8, b8, rev):
    row = lax.broadcasted_iota(jnp.int32, a8.shape, 0)
    for s in (1, 2, 4):
        if rev:
            ok = row < SUBLANES - s
            sh = SUBLANES - s
        else:
            ok = row >= s
            sh = s
        a_sh = jnp.where(ok, pltpu.roll(a8, sh, 0), 1.0)
        b_sh = jnp.where(ok, pltpu.roll(b8, sh, 0), 0.0)
        b8 = a8 * b_sh + b8
        a8 = a8 * a_sh
    return a8, b8


def _scan_chunks(chains, c0, n, carries):
    def step(kk, hs):
        new = []
        for (a_ref, b_ref, h_ref, rev), hprev in zip(chains, hs):
            c = c0 + (n - 1 - kk if rev else kk)
            r0 = pl.multiple_of(c * SUBLANES, SUBLANES)
            big_a, big_b = _sublane_scan(a_ref[pl.ds(r0, SUBLANES), :], b_ref[pl.ds(r0, SUBLANES), :], rev)
            h8 = big_b + big_a * hprev
            h_ref[pl.ds(r0, SUBLANES), :] = h8
            edge = 0 if rev else SUBLANES - 1
            new.append(jnp.broadcast_to(h8[edge:edge + 1, :], h8.shape))
        return tuple(new)

    return lax.fori_loop(0, n, step, tuple(carries))


def _rglru_conv(x, cw_ref, cb_ref, m_ref, l_len, t_len):
    y = cb_ref[...] + _shift_rows(x, -1, m_ref, l_len, t_len) * cw_ref[0:1, :]
    for kk in range(1, CONV_WIDTH):
        y = y + _shift_rows(x, kk - 1, m_ref, l_len, t_len) * cw_ref[kk:kk + 1, :]
    return y


def _rglru_gates(xc, wr, wi, br, bi, lam):
    xb = xc.astype(BF16)
    r = _sigmoid(_dot(xb, wr.astype(BF16), 1, 0) + br)
    ig = _sigmoid(_dot(xb, wi.astype(BF16), 1, 0) + bi)
    sp = _softplus(-lam)
    log_a = (-LRU_C) * r * sp
    a = jnp.exp(log_a)
    mult = jnp.sqrt(-_expm1(2.0 * log_a))
    u = ig * xc
    return r, ig, sp, a, mult, u


def _rglru_specs(proj, xr_blk, xg_blk, t_len):
    col = lambda off: pl.BlockSpec((t_len, RNN_BLOCK), lambda h, off=off: (0, off + h))
    vec = lambda rows: pl.BlockSpec((rows, RNN_BLOCK), lambda h: (0, h))
    wsp = pl.BlockSpec((2, 1, RNN_BLOCK, RNN_BLOCK), lambda h: (0, h, 0, 0))
    msp = pl.BlockSpec((t_len, len(SHIFTS) * RNN_BLOCK), lambda h: (0, 0))
    return col(xr_blk), col(xg_blk), vec, wsp, msp


def _rglru_fwd(proj, xr_blk, xg_blk, masks, conv_w, conv_b, w_rg, w_ig, b_rg, b_ig, lam, *, l_len, name):
    t_len = proj.shape[0]
    s_len = t_len - l_len
    d_rnn = conv_b.shape[1]
    n_l, n_t = l_len // SUBLANES, t_len // SUBLANES

    def body(xr_ref, xg_ref, m_ref, cw_ref, cb_ref, wr_ref, wi_ref, br_ref, bi_ref, lam_ref, rg_ref, hf_ref, hb_ref,
             a_s, b_s):
        xc = _rglru_conv(xr_ref[...], cw_ref, cb_ref, m_ref, l_len, t_len)
        for d in range(2):
            _, _, _, a, mult, u = _rglru_gates(xc, wr_ref[d, 0], wi_ref[d, 0], br_ref[d:d + 1, :],
                                               bi_ref[d:d + 1, :], lam_ref[d:d + 1, :])
            a_s[d] = a
            b_s[d] = mult * u
        chains = [(a_s.at[0], b_s.at[0], hf_ref, False), (a_s.at[1], b_s.at[1], hb_ref, True)]
        zero = jnp.zeros((SUBLANES, RNN_BLOCK), F32)
        carry = _scan_chunks(chains, 0, n_l, (zero, zero))
        _scan_chunks(chains, n_l, n_t - n_l, carry)
        rnn = hf_ref[pl.ds(l_len, s_len), :] + hb_ref[pl.ds(l_len, s_len), :]
        rg_ref[...] = (rnn * _gelu(xg_ref[pl.ds(l_len, s_len), :])).astype(BF16)

    xr_spec, xg_spec, vec, wsp, msp = _rglru_specs(proj, xr_blk, xg_blk, t_len)
    col_t = pl.BlockSpec((t_len, RNN_BLOCK), lambda h: (0, h))
    return pl.pallas_call(
        body, name=name, grid=(d_rnn // RNN_BLOCK,),
        in_specs=[xr_spec, xg_spec, msp, vec(CONV_WIDTH), vec(1), wsp, wsp, vec(2), vec(2), vec(2)],
        out_specs=(pl.BlockSpec((s_len, RNN_BLOCK), lambda h: (0, h)), col_t, col_t),
        out_shape=(jax.ShapeDtypeStruct((s_len, d_rnn), BF16), jax.ShapeDtypeStruct((t_len, d_rnn), F32),
                   jax.ShapeDtypeStruct((t_len, d_rnn), F32)),
        scratch_shapes=[pltpu.VMEM((2, t_len, RNN_BLOCK), F32)] * 2,
        compiler_params=_params(("parallel",)),
    )(proj, proj, masks, conv_w, conv_b, w_rg, w_ig, b_rg, b_ig, lam)


N_RNN_VEC = 16


def _rglru_bwd(proj, xr_blk, xg_blk, masks, h_fwd, h_bwd, drg, conv_w, conv_b, w_rg, w_ig, b_rg, b_ig, lam, *, l_len, name):
    t_len = proj.shape[0]
    s_len = t_len - l_len
    d_rnn = conv_b.shape[1]
    n_l, n_t = l_len // SUBLANES, t_len // SUBLANES

    def body(xr_ref, xg_ref, m_ref, hf_ref, hb_ref, drg_ref, cw_ref, cb_ref, wr_ref, wi_ref, br_ref, bi_ref, lam_ref,
             dxr_ref, dxg_ref, dvec_ref, dwr_ref, dwi_ref, a_s, b_s, h_s):
        t = lax.broadcasted_iota(jnp.int32, (t_len, RNN_BLOCK), 0)
        x = xr_ref[...]
        xc = _rglru_conv(x, cw_ref, cb_ref, m_ref, l_len, t_len)
        xb = xc.astype(BF16)
        gates = []
        for d in range(2):
            gt = _rglru_gates(xc, wr_ref[d, 0], wi_ref[d, 0], br_ref[d:d + 1, :], bi_ref[d:d + 1, :],
                              lam_ref[d:d + 1, :])
            gates.append(gt)
        zero = jnp.zeros((SUBLANES, RNN_BLOCK), F32)
        h0, h1 = hf_ref[...], hb_ref[...]
        xg = xg_ref[...]
        dr = jnp.concatenate([jnp.zeros((l_len, RNN_BLOCK), F32), drg_ref[...].astype(F32)], axis=0)
        dxg_ref[...] = (dr * (h0 + h1) * _dgelu(xg)).astype(dxg_ref.dtype)
        drnn = dr * _gelu(xg)
        a_s[0] = pltpu.roll(gates[0][3], t_len - 1, 0)
        a_s[1] = pltpu.roll(gates[1][3], 1, 0)
        b_s[0] = drnn
        b_s[1] = drnn
        hprev = [jnp.where(t == 0, 0.0, pltpu.roll(h0, 1, 0)),
                 jnp.where(t == l_len - 1, 0.0, pltpu.roll(h1, t_len - 1, 0))]
        chains = [(a_s.at[0], b_s.at[0], h_s.at[0], True), (a_s.at[1], b_s.at[1], h_s.at[1], False)]
        carry = _scan_chunks(chains, n_l, n_t - n_l, (zero, zero))
        _scan_chunks(chains, 0, n_l, carry)

        dxc = jnp.zeros((t_len, RNN_BLOCK), F32)
        for d in range(2):
            r, ig, sp, a, mult, u = gates[d]
            lamb = h_s[d]
            da = lamb * hprev[d]
            dmult = lamb * u
            du = lamb * mult
            dlog_a = da * a - dmult * (a * a) / mult
            dzr = (dlog_a * ((-LRU_C) * sp)) * r * (1.0 - r)
            dsp = jnp.sum(dlog_a * ((-LRU_C) * r), axis=0, keepdims=True)
            dzi = (du * xc) * ig * (1.0 - ig)
            dzrb, dzib = dzr.astype(BF16), dzi.astype(BF16)
            dxc = dxc + du * ig + _dot(dzrb, wr_ref[d, 0].astype(BF16), 1, 1) + _dot(dzib, wi_ref[d, 0].astype(BF16), 1, 1)
            dwr_ref[d, 0] = _dot(xb, dzrb, 0, 0)
            dwi_ref[d, 0] = _dot(xb, dzib, 0, 0)
            dvec_ref[5 + d:6 + d, :] = jnp.sum(dzr, axis=0, keepdims=True)
            dvec_ref[7 + d:8 + d, :] = jnp.sum(dzi, axis=0, keepdims=True)
            dvec_ref[9 + d:10 + d, :] = dsp * (-_sigmoid(-lam_ref[d:d + 1, :]))
        dvec_ref[4:5, :] = jnp.sum(dxc, axis=0, keepdims=True)
        dvec_ref[11:N_RNN_VEC, :] = jnp.zeros((N_RNN_VEC - 11, RNN_BLOCK), F32)
        dxr = jnp.zeros((t_len, RNN_BLOCK), F32)
        for kk in range(CONV_WIDTH):
            dvec_ref[kk:kk + 1, :] = jnp.sum(dxc * _shift_rows(x, kk - 1, m_ref, l_len, t_len), axis=0, keepdims=True)
            dxr = dxr + _shift_rows(dxc, 1 - kk, m_ref, l_len, t_len) * cw_ref[kk:kk + 1, :]
        dxr_ref[...] = dxr.astype(dxr_ref.dtype)

    xr_spec, xg_spec, vec, wsp, msp = _rglru_specs(proj, xr_blk, xg_blk, t_len)
    col_t = pl.BlockSpec((t_len, RNN_BLOCK), lambda h: (0, h))
    return pl.pallas_call(
        body, name=name, grid=(d_rnn // RNN_BLOCK,),
        in_specs=[xr_spec, xg_spec, msp, col_t, col_t, pl.BlockSpec((s_len, RNN_BLOCK), lambda h: (0, h)), vec(CONV_WIDTH),
                  vec(1),
                  wsp, wsp, vec(2), vec(2), vec(2)],
        out_specs=(col_t, col_t, vec(N_RNN_VEC), wsp, wsp),
        out_shape=(jax.ShapeDtypeStruct((t_len, d_rnn), BF16), jax.ShapeDtypeStruct((t_len, d_rnn), BF16),
                   jax.ShapeDtypeStruct((N_RNN_VEC, d_rnn), F32), jax.ShapeDtypeStruct(w_rg.shape, F32),
                   jax.ShapeDtypeStruct(w_ig.shape, F32)),
        scratch_shapes=[pltpu.VMEM((2, t_len, RNN_BLOCK), F32)] * 3,
        compiler_params=_params(("parallel",)),
    )(proj, proj, masks, h_fwd, h_bwd, drg, conv_w, conv_b, w_rg, w_ig, b_rg, b_ig, lam)


def _colsum(v):
    return jnp.sum(v, axis=0, keepdims=True)


def _qk_fwd_heads(xv, gain, cos_e, sin_e):
    outs = []
    for h in range(xv.shape[1] // HEAD_DIM):
        xh = xv[:, h * HEAD_DIM:(h + 1) * HEAD_DIM]
        y = xh * _rms(xh) * gain
        outs.append(y * cos_e + _swap_pairs(y) * sin_e)
    return outs[0] if len(outs) == 1 else jnp.concatenate(outs, axis=1)


def _qk_bwd_heads(xv, dov, gain, cos_e, sin_e):
    outs = []
    dgain = jnp.zeros((1, HEAD_DIM), F32)
    for h in range(xv.shape[1] // HEAD_DIM):
        cols = slice(h * HEAD_DIM, (h + 1) * HEAD_DIM)
        xh, doh = xv[:, cols], dov[:, cols]
        r = _rms(xh)
        xhat = xh * r
        dy = doh * cos_e + _swap_pairs(doh * sin_e)
        dgain = dgain + _colsum(dy * xhat)
        outs.append(_rms_bwd(xhat, r, dy * gain))
    return (outs[0] if len(outs) == 1 else jnp.concatenate(outs, axis=1)), dgain


WEIGHTS = ['c_ctx', 'w_mod', 'b_mod', 'g_mix', 'g_mlp', 'w_in', 'q_gain', 'k_gain', 'conv_w', 'conv_b', 'w_rg', 'b_rg',
           'w_ig', 'b_ig', 'lru_lambda', 'w_o_attn', 'w_o_rnn', 'w_out', 'w_up', 'w_down', 'g_final']
REPLICATED = ['c_ctx', 'b_mod', 'g_mix', 'g_mlp', 'q_gain', 'k_gain', 'conv_b', 'w_rg', 'w_ig', 'g_final']
SMALL_SHARDED = [('conv_w', CONV_WIDTH), ('b_rg', 2), ('b_ig', 2), ('lru_lambda', 2)]
SMALL_ROWS = 16


PACK_ROWS = 512


def _pack_flat(arrs, rows_mult):
    flat = jnp.concatenate([a.reshape(-1) for a in arrs])
    n = flat.shape[0]
    unit = rows_mult * LANES
    pad = (-n) % unit
    if pad:
        flat = jnp.concatenate([flat, jnp.zeros((pad,), flat.dtype)])
    return flat.reshape(-1, LANES)


def _unpack_flat(packed, shapes):
    flat = packed.reshape(-1)
    out, pos = [], 0
    for shp in shapes:
        n = math.prod(shp)
        out.append(flat[pos:pos + n].reshape(shp))
        pos += n
    return out


def _pack_small(d):
    rows = [d[n][0] for n, _ in SMALL_SHARDED]
    cols = rows[0].shape[1]
    used = sum(k for _, k in SMALL_SHARDED)
    return jnp.concatenate(rows + [jnp.zeros((SMALL_ROWS - used, cols), F32)], axis=0)


def _unpack_small(p):
    out, pos = {}, 0
    for n, k in SMALL_SHARDED:
        out[n] = p[pos:pos + k][None]
        pos += k
    return out


def kernel(x, c, ctx, c_ctx, w_mod, b_mod, g_mix, g_mlp, w_in, q_gain, k_gain, conv_w, conv_b, w_rg, b_rg, w_ig, b_ig, lru_lambda, w_o_attn, w_o_rnn, w_out, w_up, w_down, g_final, loss_target, m_c_ctx, m_w_mod, m_b_mod, m_g_mix, m_g_mlp, m_w_in, m_q_gain, m_k_gain, m_conv_w, m_conv_b, m_w_rg, m_b_rg, m_w_ig, m_b_ig, m_lru_lambda, m_w_o_attn, m_w_o_rnn, m_w_out, m_w_up, m_w_down, m_g_final, v_c_ctx, v_w_mod, v_b_mod, v_g_mix, v_g_mlp, v_w_in, v_q_gain, v_k_gain, v_conv_w, v_conv_b, v_w_rg, v_b_rg, v_w_ig, v_b_ig, v_lru_lambda, v_w_o_attn, v_w_o_rnn, v_w_out, v_w_up, v_w_down, v_g_final):
    wd = dict(c_ctx=c_ctx, w_mod=w_mod, b_mod=b_mod, g_mix=g_mix, g_mlp=g_mlp, w_in=w_in, q_gain=q_gain, k_gain=k_gain,
              conv_w=conv_w, conv_b=conv_b, w_rg=w_rg, b_rg=b_rg, w_ig=w_ig, b_ig=b_ig, lru_lambda=lru_lambda,
              w_o_attn=w_o_attn, w_o_rnn=w_o_rnn, w_out=w_out, w_up=w_up, w_down=w_down, g_final=g_final)
    md = dict(zip(WEIGHTS, (m_c_ctx, m_w_mod, m_b_mod, m_g_mix, m_g_mlp, m_w_in, m_q_gain, m_k_gain, m_conv_w, m_conv_b,
                            m_w_rg, m_b_rg, m_w_ig, m_b_ig, m_lru_lambda, m_w_o_attn, m_w_o_rnn, m_w_out, m_w_up,
                            m_w_down, m_g_final)))
    vd = dict(zip(WEIGHTS, (v_c_ctx, v_w_mod, v_b_mod, v_g_mix, v_g_mlp, v_w_in, v_q_gain, v_k_gain, v_conv_w, v_conv_b,
                            v_w_rg, v_b_rg, v_w_ig, v_b_ig, v_lru_lambda, v_w_o_attn, v_w_o_rnn, v_w_out, v_w_up,
                            v_w_down, v_g_final)))

    xs, tgt, ctxs = x[0], loss_target[0], ctx[0]
    s_len, d = xs.shape
    l_len = ctxs.shape[0]
    t_len = s_len + l_len
    n_in = w_in.shape[2] * N_DEV
    aw = (n_in - 4 * d) * 2 // 3
    kvw = aw // Q_GROUP
    assert aw == d and n_in == aw + 2 * kvw + 4 * d
    nmod = N_MOD * d // N_DEV
    tr = 128
    assert l_len % tr == 0 and s_len % tr == 0
    lb = l_len // tr
    tq = min(256, s_len)
    my = _lin(_my_pos())
    c_ctx2 = c_ctx[None]
    g_final2 = g_final[None]

    c_all = _all_gather([jnp.broadcast_to(c, (SUBLANES, d))], [0], "ag_cond")[0][::SUBLANES]
    c16 = jnp.concatenate([c_all, jnp.broadcast_to(c_ctx2, (N_DEV, d))], axis=0)
    b_mod_sh = lax.dynamic_slice(b_mod, (0, my * nmod), (1, nmod))
    mod_part = _mm(c16, w_mod[0], a_fn=_silu, bias=b_mod_sh, name="mod_fwd")
    (mod_g,), order = _all_gather([mod_part], [1], "ag_mod", with_token=True)
    mod_lat = lax.dynamic_slice(mod_g, (my, 0), (1, N_MOD * d))
    sh_a, sc_a, ga_a, sh_f, sc_f, ga_f = [mod_lat[:, k * d:(k + 1) * d] for k in range(N_MOD)]
    csh_a, csc_a = mod_g[N_DEV:N_DEV + 1, :d], mod_g[N_DEV:N_DEV + 1, d:2 * d]

    sp_w = _pack_small(wd)
    (w_in_f, sp_f), order = _all_gather([_tok(w_in[0], order).astype(BF16), sp_w], [1, 1], "ag_w_in", with_token=True)
    conv_w_f, b_rg_f, b_ig_f, lam_f = sp_f[0:4], sp_f[4:6], sp_f[6:8], sp_f[8:10]
    late_axis = dict(w_o_attn=0, w_o_rnn=0, w_out=0, w_up=1, w_down=0)
    placed = [_place_block(wd[n][0], ax, my, "place_" + n) for n, ax in late_axis.items()]
    late_kind = lambda n: "gather_chips" if n in ("w_up", "w_down") else "gather"
    late = {}
    for (n, ax), full in zip(late_axis.items(), placed):
        late[n] = _xfer_start(late_kind(n), _tok(wd[n][0], order).astype(BF16), full, ax, "ag_start_" + n)
        order = late[n][4]
    g_mix_fwd = _tok(g_mix, order)

    def late_wait(n, after):
        full = _xfer_wait(late_kind(n), late[n], late_axis[n], after, "ag_wait_" + n)
        return _sibling_swap(full, late_axis[n], "ag_swap_" + n) if late_kind(n) == "gather_chips" else full

    pos = jnp.arange(s_len)
    n_freq = HEAD_DIM // 4
    inv_freq = ROPE_THETA ** (-jnp.arange(n_freq, dtype=F32) / n_freq)
    ang = jnp.concatenate([(pos // GRID_W).astype(F32)[:, None] * inv_freq,
                           (pos % GRID_W).astype(F32)[:, None] * inv_freq], axis=-1)
    sign = jnp.where(jnp.arange(HEAD_DIM) % 2 == 0, -1.0, 1.0).astype(F32)
    cos_e = jnp.concatenate([jnp.ones((l_len, HEAD_DIM), F32), jnp.repeat(jnp.cos(ang), 2, axis=-1)], axis=0)
    sin_e = jnp.concatenate([jnp.zeros((l_len, HEAD_DIM), F32), jnp.repeat(jnp.sin(ang), 2, axis=-1) * sign], axis=0)

    x_all = jnp.concatenate([ctxs, xs], axis=0)

    def nm1_fwd(i, j, xv, g, csh, csc, sh, sc):
        is_ctx = i < lb
        n = xv * _rms(xv) * g
        return n * (1.0 + jnp.where(is_ctx, csc, sc)) + jnp.where(is_ctx, csh, sh)

    h_all = _rowwise(nm1_fwd, [_row(x_all), _bc(g_mix_fwd), _bc(csh_a), _bc(csc_a), _bc(sh_a), _bc(sc_a)],
                     [("row", BF16)], nr=t_len // tr, tr=tr, nc=1, cw=d, name="norm1_fwd")[0]
    proj = _mm(h_all, w_in_f, name="proj_fwd")

    kb = aw // kvw
    q = _rowwise(lambda i, j, xv, gain, ce, se: _qk_fwd_heads(xv, gain, ce, se),
                 [_row(proj, 0, lb, aw), _full(q_gain), _row(cos_e, 0, lb, HEAD_DIM), _row(sin_e, 0, lb, HEAD_DIM)],
                 [("row", BF16, aw)], nr=s_len // tr, tr=tr, nc=1, cw=aw, name="q_fwd")[0]
    k_all, v_all = _rowwise(lambda i, j, kv_, vv_, gain, ce, se: (_qk_fwd_heads(kv_, gain, ce, se), vv_),
                            [_row(proj, kb, 0, kvw), _row(proj, kb + 1, 0, kvw), _full(k_gain),
                             _row(cos_e, 0, 0, HEAD_DIM), _row(sin_e, 0, 0, HEAD_DIM)],
                            [("row", BF16, kvw), ("row", BF16, kvw)], nr=t_len // tr, tr=tr, nc=1, cw=kvw, name="kv_fwd")
    attn_o = _attn_fwd(q, k_all, v_all, tq=tq, name="attn_fwd")

    xr_blk = (aw + 2 * kvw) // RNN_BLOCK
    xg_blk = xr_blk + d // RNN_BLOCK
    shift_masks = _shift_masks(l_len, t_len)
    rg, h_fwd, h_bwd = _rglru_fwd(proj, xr_blk, xg_blk, shift_masks, conv_w_f, conv_b, w_rg[0], w_ig[0], b_rg_f, b_ig_f, lam_f, l_len=l_len,
                    name="rglru_fwd")

    w_oa_f = late_wait("w_o_attn", attn_o)
    y_attn = _mm(attn_o, w_oa_f, name="o_attn_fwd")
    w_or_f = late_wait("w_o_rnn", rg)
    y_rnn = _mm(rg, w_or_f, name="o_rnn_fwd")
    hw = d // 2
    glb = (aw + 2 * kvw + 2 * d) // hw

    def merge_fwd(i, j, ya, yr, gla, glr):
        return _sigmoid(gla) * ya + _sigmoid(glr) * yr

    u = _rowwise(merge_fwd, [_row(y_attn), _row(y_rnn), _row(proj, glb, lb), _row(proj, glb + 2, lb)],
                 [("row", BF16)], nr=s_len // tr, tr=tr, nc=2, cw=hw, name="merge_fwd")[0]
    w_out_f = late_wait("w_out", u)
    mix = _mm(u, w_out_f, name="out_fwd")

    def res1_fwd(i, j, xv, mixv, gaa, gm, shf, scf):
        x1v = xv + gaa * mixv
        n = x1v * _rms(x1v) * gm
        return x1v, n * (1.0 + scf) + shf

    x1, h2 = _rowwise(res1_fwd, [_row(xs), _row(mix), _bc(ga_a), _bc(g_mlp), _bc(sh_f), _bc(sc_f)],
                      [("row", F32), ("row", BF16)], nr=s_len // tr, tr=tr, nc=1, cw=d, name="norm2_fwd")
    relu2 = lambda z: jnp.square(jnp.maximum(z, 0.0))
    w_up_f = late_wait("w_up", h2)
    up = _mm(h2, w_up_f, name="up_fwd")
    w_down_f = late_wait("w_down", up)
    down = _mm(up, w_down_f, a_fn=relu2, name="down_fwd")

    def final_fn(i, j, x1v, dn, tg, gaf, gf):
        x2 = x1v + gaf * dn
        r = _rms(x2)
        xh = x2 * r
        e = xh * gf - tg
        dy = e * (1.0 / d)
        dx2 = _rms_bwd(xh, r, dy * gf)
        return dx2, dx2 * gaf, _colsum(e * e) * (0.5 / d), _colsum(dy * xh), _colsum(dx2 * dn)

    dx2, ddown, loss_cols, g_gfinal, d_ga_f = _rowwise(
        final_fn, [_row(x1), _row(down), _row(tgt), _bc(ga_f), _bc(g_final2)],
        [("row", F32), ("row", BF16), ("acc", 1), ("acc", 1), ("acc", 1)], nr=s_len // tr, tr=tr, nc=1, cw=d,
        name="final_bwd")
    loss = lax.psum(jnp.sum(loss_cols), ("x", "y", "c"))

    dup = _mm(ddown, w_down_f, tb=True, out_fn=lambda r, upv: r * (2.0 * jnp.maximum(upv, 0.0)), extras=[up],
              out_dtype=BF16, name="down_bwd_x")
    gw_down = _mm(up, ddown, ta=True, a_fn=relu2, out_dtype=BF16, name="down_bwd_w")
    sent = dict(w_down=_scatter_start(gw_down, 0, my, "rs_start_w_down"))
    dh2 = _mm(dup, w_up_f, tb=True, name="up_bwd_x")
    gw_up = _mm(h2, dup, ta=True, out_dtype=BF16, name="up_bwd_w")
    sent["w_up"] = _scatter_start(gw_up, 1, my, "rs_start_w_up")
    sc_f_bwd = _tok(sc_f, sent["w_down"][0][4], sent["w_up"][0][4])

    def norm2_bwd(i, j, x1v, mixv, dh2v, dx2v, gaa, gm, scf):
        r = _rms(x1v)
        xh = x1v * r
        n = xh * gm
        dn = dh2v * (1.0 + scf)
        dx1 = dx2v + _rms_bwd(xh, r, dn * gm)
        return dx1, dx1 * gaa, _colsum(dh2v), _colsum(dh2v * n), _colsum(dn * xh), _colsum(dx1 * mixv)

    dx1, dmix, d_sh_f, d_sc_f, g_gmlp, d_ga_a = _rowwise(
        norm2_bwd, [_row(x1), _row(mix), _row(dh2), _row(dx2), _bc(ga_a), _bc(g_mlp), _bc(sc_f_bwd)],
        [("row", F32), ("row", BF16), ("acc", 1), ("acc", 1), ("acc", 1), ("acc", 1)], nr=s_len // tr, tr=tr, nc=1,
        cw=d, name="norm2_bwd")

    du = _mm(dmix, w_out_f, tb=True, name="out_bwd_x")
    gw_out = _mm(u, dmix, ta=True, out_dtype=BF16, name="out_bwd_w")
    sent["w_out"] = _scatter_start(gw_out, 0, my, "rs_start_w_out")

    def merge_bwd(i, j, duv, ya, yr, gla, glr):
        ga, gr = _sigmoid(gla), _sigmoid(glr)
        return duv * ga, duv * gr, duv * ya * ga * (1.0 - ga), duv * yr * gr * (1.0 - gr)

    dya, dyr, dgla, dglr = _rowwise(
        merge_bwd, [_row(du), _row(y_attn), _row(y_rnn), _row(proj, glb, lb), _row(proj, glb + 2, lb)],
        [("row", BF16)] * 4, nr=s_len // tr, tr=tr, nc=2, cw=hw, name="merge_bwd")
    dattn = _mm(dya, w_oa_f, tb=True, out_dtype=BF16, name="o_attn_bwd_x")
    gw_oa = _mm(attn_o, dya, ta=True, out_dtype=BF16, name="o_attn_bwd_w")
    drg = _mm(dyr, w_or_f, tb=True, out_dtype=BF16, name="o_rnn_bwd_x")
    gw_or = _mm(rg, dyr, ta=True, out_dtype=BF16, name="o_rnn_bwd_w")
    sent["w_o_attn"] = _scatter_start(gw_oa, 0, my, "rs_start_w_o_attn")
    sent["w_o_rnn"] = _scatter_start(gw_or, 0, my, "rs_start_w_o_rnn")
    conv_b_bwd = _tok(conv_b, sent["w_out"][0][4], sent["w_o_attn"][0][4], sent["w_o_rnn"][0][4])

    dq, dk, dv = _attn_bwd(q, k_all, v_all, dattn, tq=tq, name="attn_bwd")
    dxr, dxg, dvec, g_wrg, g_wig = _rglru_bwd(proj, xr_blk, xg_blk, shift_masks, h_fwd, h_bwd, drg, conv_w_f, conv_b_bwd, w_rg[0], w_ig[0], b_rg_f,
                                               b_ig_f, lam_f, l_len=l_len, name="rglru_bwd")
    gate_part = _pack_flat([g_wrg, g_wig], N_DEV * SUBLANES)
    gate_sent = _scatter_start(gate_part, 0, my, "rs_start_gates")
    q_gain_bwd = _tok(q_gain, gate_sent[0][4])

    dq_p, g_qgain = _rowwise(lambda i, j, xv, dov, gain, ce, se: _qk_bwd_heads(xv, dov, gain, ce, se),
                             [_row(proj, 0, lb, aw), _row(dq, 0, 0, aw), _full(q_gain_bwd), _row(cos_e, 0, lb, HEAD_DIM),
                              _row(sin_e, 0, lb, HEAD_DIM)],
                             [("row", BF16, aw), ("acc", 1, HEAD_DIM)], nr=s_len // tr, tr=tr, nc=1, cw=aw, name="q_bwd")

    def kv_bwd(i, j, xv, dkv, dvv, gain, ce, se):
        dk_p, dgain = _qk_bwd_heads(xv, dkv, gain, ce, se)
        return dk_p, dvv, dgain

    dk_p, dv_p, g_kgain = _rowwise(kv_bwd, [_row(proj, kb, 0, kvw), _row(dk, 0, 0, kvw), _row(dv, 0, 0, kvw),
                                            _full(k_gain), _row(cos_e, 0, 0, HEAD_DIM), _row(sin_e, 0, 0, HEAD_DIM)],
                                   [("row", BF16, kvw), ("row", BF16, kvw), ("acc", 1, HEAD_DIM)], nr=t_len // tr, tr=tr,
                                   nc=1, cw=kvw, name="kv_bwd")

    pad_ctx = lambda a: jnp.pad(a, ((l_len, 0), (0, 0)))
    dproj = jnp.concatenate([pad_ctx(dq_p), dk_p, dv_p, dxr, dxg, pad_ctx(dgla), pad_ctx(dglr)], axis=1)
    dh_all = _mm(dproj, w_in_f, tb=True, name="proj_bwd_x")
    g_mix_bwd = g_mix

    def norm1_bwd(i, j, xv, dhv, dresv, g, sc):
        r = _rms(xv)
        xh = xv * r
        dn = dhv * (1.0 + sc)
        return dresv + _rms_bwd(xh, r, dn * g), _colsum(dhv), _colsum(dhv * (xh * g)), _colsum(dn * xh)

    def norm1_bwd_ctx(i, j, xv, dhv, g, sc):
        xh = xv * _rms(xv)
        return _colsum(dhv), _colsum(dhv * (xh * g)), _colsum(dhv * (1.0 + sc) * xh)

    dx_lat, d_sh_a, d_sc_a, g_gmix_lat = _rowwise(
        norm1_bwd, [_row(x_all, 0, lb), _row(dh_all, 0, lb), _row(dx1), _bc(g_mix_bwd), _bc(sc_a)],
        [("row", F32)] + [("acc", 1)] * 3, nr=s_len // tr, tr=tr, nc=1, cw=d, name="norm1_bwd")
    d_csh, d_csc, g_gmix_ctx = _rowwise(
        norm1_bwd_ctx, [_row(x_all), _row(dh_all), _bc(g_mix_bwd), _bc(csc_a)],
        [("acc", 1)] * 3, nr=lb, tr=tr, nc=1, cw=d, name="norm1_bwd_ctx")
    g_gmix = g_gmix_lat + g_gmix_ctx
    grad_x = dx_lat[None]

    dmod_lat = jnp.concatenate([d_sh_a, d_sc_a, d_ga_a, d_sh_f, d_sc_f, d_ga_f], axis=1)
    dmod_ctx = jnp.concatenate([d_csh, d_csc, jnp.zeros((1, (N_MOD - 2) * d), F32)], axis=1)
    dm_g = _all_gather([jnp.concatenate([dmod_lat, dmod_ctx, jnp.zeros((SUBLANES - 2, N_MOD * d), F32)], axis=0)], [0],
                       "ag_dmod")[0]
    dm16 = jnp.concatenate([dm_g[0::SUBLANES], dm_g[1::SUBLANES]], axis=0)
    dm_sh = lax.dynamic_slice(dm16, (0, my * nmod), (2 * N_DEV, nmod))
    gw_mod = _mm(c16, dm_sh, ta=True, a_fn=_silu, name="mod_bwd_w")
    dc16 = _mm(dm_sh, w_mod[0], tb=True, name="mod_bwd_x")

    def cctx_fn(i, j, pv, cc, dl, dc):
        row = lax.broadcasted_iota(jnp.int32, pv.shape, 0)
        return _colsum(jnp.where(row >= N_DEV, pv, 0.0)) * _dsilu(cc), dl + dc

    g_cctx, g_bmod = _rowwise(cctx_fn, [_full(dc16), _full(c_ctx2), _full(dmod_lat), _full(dmod_ctx)],
                              [("row", F32, d), ("row", F32, N_MOD * d)], nr=1, tr=1, nc=1, cw=d, name="cctx_bwd")

    rep_part = dict(c_ctx=g_cctx, b_mod=g_bmod, g_mix=g_gmix, g_mlp=g_gmlp, q_gain=g_qgain, k_gain=g_kgain,
                    conv_b=dvec[4:5], g_final=g_gfinal)
    rep_small = [n for n in REPLICATED if n in rep_part]
    gates = [n for n in REPLICATED if n not in rep_part]
    part_pack = _pack_flat([rep_part[n] for n in rep_small], PACK_ROWS)
    rep_rows = part_pack.shape[0]
    gate_land = _xfer_wait("scatter", gate_sent[0], 0, g_bmod, "rs_wait_gates")
    gate_sum = _sum_parts(gate_sent[1], gate_land, name="gate_sum")
    (gate_full, parts), order = _all_gather([gate_sum, part_pack], [0, 0], "ag_rep_grads", with_token=True)
    small_g = jnp.concatenate([dvec[0:4], dvec[5:11], jnp.zeros((SMALL_ROWS - 10, d), F32)], axis=0)
    (small_recv,), order = _all_to_all([_tok(small_g, order)], [1], "a2a_small_grads")

    halves = {}
    for part, lo in (("top", 0), ("bot", d // 2)):
        gw_part = _mm(h_all[:, lo:lo + d // 2], dproj, ta=True, out_dtype=BF16, name="proj_bwd_w_" + part,
                      bias=_tok(jnp.zeros((1, n_in), F32), order))
        halves[part] = _chip_scatter_start(_pair_reduce(gw_part, my, "pair_w_in_" + part), my, "rs_start_w_in_" + part)
        order = halves[part][0][4]
    started = order

    res = {}
    for names, pr, mult, nm in ((rep_small, parts.reshape(N_DEV, rep_rows, LANES), PACK_ROWS, "adamw_rep"),
                                (gates, gate_full[None], N_DEV * SUBLANES, "adamw_gates")):
        rep_out = _adamw(pr, _tok(_pack_flat([wd[n] for n in names], mult), started),
                         _pack_flat([md[n] for n in names], mult), _pack_flat([vd[n] for n in names], mult), name=nm)
        for kind, packed in zip(("g", "d", "m", "v"), rep_out):
            for n, a in zip(names, _unpack_flat(packed, [wd[n].shape for n in names])):
                res[kind, n] = a
    out = _adamw(small_recv, _tok(sp_w, started), _pack_small(md), _pack_small(vd), name="adamw_small")
    for kind, a in zip(("g", "d", "m", "v"), out):
        for n, piece in _unpack_small(a).items():
            res[kind, n] = piece

    after = started
    for n, (handle, own, ax, kind) in sent.items():
        land = _xfer_wait(kind, handle, ax, after, "rs_wait_" + n)
        out = _adamw(land, wd[n][0], md[n][0], vd[n][0], own=own, name="adamw_" + n)
        after = out[0]
        for kind, a in zip(("g", "d", "m", "v"), out):
            res[kind, n] = a[None]
    out = _adamw(gw_mod[None], w_mod[0], m_w_mod[0], v_w_mod[0], name="adamw_w_mod")
    after = out[0]
    for kind, a in zip(("g", "d", "m", "v"), out):
        res[kind, 'w_mod'] = a[None]
    lands, owns = [], []
    for part, (handle, own, ax, kind) in halves.items():
        lands.append(_xfer_wait(kind, handle, ax, after, "rs_wait_w_in_" + part))
        owns.append(own)
        after = lands[-1]
    out = _adamw(jnp.concatenate(lands, axis=1), w_in[0], m_w_in[0], v_w_in[0], own=jnp.concatenate(owns, axis=0),
                 name="adamw_w_in")
    for kind, a in zip(("g", "d", "m", "v"), out):
        res[kind, 'w_in'] = a[None]

    outs = [loss, grad_x]
    for kind in ("g", "d", "m", "v"):
        outs += [res[kind, n].reshape(wd[n].shape) for n in WEIGHTS]
    return tuple(outs)
```

```python
import functools
import math

import jax
import jax.numpy as jnp
from jax import lax
from jax.experimental import pallas as pl
from jax.experimental.pallas import tpu as pltpu

F32 = jnp.float32
BF16 = jnp.bfloat16
MESH = pl.DeviceIdType.MESH

N_DEV = 8
HEAD_DIM = 128
Q_GROUP = 4
RNN_BLOCK = 128
GRID_W = 64
CONV_WIDTH = 4
LRU_C = 8.0
ROPE_THETA = 10000.0
NORM_EPS = 1e-6
N_MOD = 6
ADAM_LR = 0.001
ADAM_B1 = 0.9
ADAM_B2 = 0.999
ADAM_EPS = 1e-08
ADAM_WD = 0.01
ADAM_STEP = 10

LANES = 128
SUBLANES = 8
VMEM_LIMIT = 56 * 1024 * 1024
HBM_SPEC = pl.BlockSpec(memory_space=pltpu.HBM)


def _params(sem=None, vmem=VMEM_LIMIT):
    kw = dict(vmem_limit_bytes=vmem)
    if sem is not None:
        kw["dimension_semantics"] = sem
    return pltpu.CompilerParams(**kw)


def _pick(dim, target, mult):
    for t in range(min(dim, target), 0, -1):
        if dim % t == 0 and t % mult == 0:
            return t
    return dim


def _my_pos():
    return lax.axis_index("x"), lax.axis_index("y"), lax.axis_index("c")


def _flip(pos, d):
    x, y, c = pos
    return (1 - x if d & 4 else x, 1 - y if d & 2 else y, 1 - c if d & 1 else c)


def _lin(pos):
    return 4 * pos[0] + 2 * pos[1] + pos[2]


def _slab(ref, axis, idx, size):
    if axis == 0:
        return ref.at[pl.ds(idx * size, size)]
    assert axis == 1
    return ref.at[:, pl.ds(idx * size, size)]


def _all_gather(arrs, axes, name, with_token=False):
    n = len(arrs)
    sizes = [a.shape[ax] for a, ax in zip(arrs, axes)]

    def body(*refs):
        ins, outs, token = refs[:n], refs[n:2 * n], refs[2 * n]
        send_sems, recv_sems, local_sems = refs[2 * n + 1:]
        token[...] = jnp.zeros_like(token)
        me = _my_pos()
        sibling = _flip(me, 1)
        chips = [2, 4, 6]

        def copy(a, k, block_pos, to, src=None):
            dst = _slab(outs[a], axes[a], _lin(block_pos), sizes[a])
            return pltpu.make_async_remote_copy(
                src_ref=dst if src is None else src, dst_ref=dst,
                send_sem=send_sems.at[7 * a + k], recv_sem=recv_sems.at[7 * a + k],
                device_id=to, device_id_type=MESH)

        local, first, passed = [], [], []
        for a in range(n):
            cp = pltpu.make_async_copy(ins[a], _slab(outs[a], axes[a], _lin(me), sizes[a]), local_sems.at[a])
            cp.start()
            local.append(cp)
        for a in range(n):
            f = [copy(a, 0, me, sibling, src=ins[a])]
            f += [copy(a, 1 + j, me, _flip(me, m), src=ins[a]) for j, m in enumerate(chips)]
            for cp in f:
                cp.start()
            first += f
        for a in range(n):
            for j, m in enumerate(chips):
                copy(a, 1 + j, _flip(me, m), me).wait_recv()
                cp = copy(a, 4 + j, _flip(me, m), sibling)
                cp.start()
                passed.append(cp)
        for a in range(n):
            copy(a, 0, sibling, me).wait_recv()
            for j, m in enumerate(chips):
                copy(a, 4 + j, _flip(sibling, m), me).wait_recv()
        for cp in first + passed:
            cp.wait_send()
        for cp in local:
            cp.wait()

    out_shape = []
    for a, ax in zip(arrs, axes):
        shp = list(a.shape)
        shp[ax] *= N_DEV
        out_shape.append(jax.ShapeDtypeStruct(tuple(shp), a.dtype))
    out_shape.append(jax.ShapeDtypeStruct((SUBLANES, LANES), F32))
    res = pl.pallas_call(
        body, name=name, out_shape=tuple(out_shape),
        in_specs=[HBM_SPEC] * n, out_specs=tuple([HBM_SPEC] * n + [pl.BlockSpec(memory_space=pltpu.VMEM)]),
        scratch_shapes=[pltpu.SemaphoreType.DMA((7 * n,)), pltpu.SemaphoreType.DMA((7 * n,)),
                        pltpu.SemaphoreType.DMA((n,))],
    )(*arrs)
    return (list(res[:n]), res[n]) if with_token else list(res[:n])


def _gather_proj(h_all, w_shard, name):
    t_len, kdim = h_all.shape
    sz = w_shard.shape[1]
    tm = _pick(t_len, 1024, 16)
    n_m = t_len // tm

    def body(h_ref, w_ref, wfull_ref, proj_ref, token, a_buf, b_buf, o_buf, send_sems, recv_sems, io_sems, out_sems):
        token[...] = jnp.zeros_like(token)
        me = _my_pos()
        sibling = _flip(me, 1)
        chips = [2, 4, 6]
        slab = lambda pos: _slab(wfull_ref, 1, _lin(pos), sz)

        def copy(k, block_pos, to, src=None):
            return pltpu.make_async_remote_copy(
                src_ref=slab(block_pos) if src is None else src, dst_ref=slab(block_pos),
                send_sem=send_sems.at[k], recv_sem=recv_sems.at[k], device_id=to, device_id_type=MESH)

        pending = []

        def block_product(pos, b_src):
            load = pltpu.make_async_copy(b_src, b_buf, io_sems.at[1])
            load.start()
            load.wait()
            for i in range(n_m):
                slot = len(pending) % 2
                if len(pending) >= 2:
                    pending[-2].wait()
                o_buf[slot] = _dot(a_buf[pl.ds(i * tm, tm), :], b_buf[...], 1, 0)
                out = pltpu.make_async_copy(
                    o_buf.at[slot], proj_ref.at[pl.ds(i * tm, tm), pl.ds(_lin(pos) * sz, sz)], out_sems.at[slot])
                out.start()
                pending.append(out)

        local = pltpu.make_async_copy(w_ref, slab(me), io_sems.at[0])
        local.start()
        first = [copy(0, me, sibling, src=w_ref)] + [copy(1 + j, me, _flip(me, m), src=w_ref) for j, m in enumerate(chips)]
        for cp in first:
            cp.start()
        load_a = pltpu.make_async_copy(h_ref, a_buf, io_sems.at[2])
        load_a.start()
        load_a.wait()
        block_product(me, w_ref)
        copy(0, sibling, me).wait_recv()
        block_product(sibling, slab(sibling))
        passed = []
        for j, m in enumerate(chips):
            src_pos = _flip(me, m)
            copy(1 + j, src_pos, me).wait_recv()
            cp = copy(4 + j, src_pos, sibling)
            cp.start()
            passed.append(cp)
            block_product(src_pos, slab(src_pos))
        for j, m in enumerate(chips):
            src_pos = _flip(sibling, m)
            copy(4 + j, src_pos, me).wait_recv()
            block_product(src_pos, slab(src_pos))
        for out in pending[-2:]:
            out.wait()
        for cp in first + passed:
            cp.wait_send()
        local.wait()

    return pl.pallas_call(
        body, name=name,
        out_shape=(jax.ShapeDtypeStruct((kdim, N_DEV * sz), w_shard.dtype), jax.ShapeDtypeStruct((t_len, N_DEV * sz), F32),
                   jax.ShapeDtypeStruct((SUBLANES, LANES), F32)),
        in_specs=[HBM_SPEC, HBM_SPEC], out_specs=(HBM_SPEC, HBM_SPEC, pl.BlockSpec(memory_space=pltpu.VMEM)),
        scratch_shapes=[pltpu.VMEM((t_len, kdim), h_all.dtype), pltpu.VMEM((kdim, sz), w_shard.dtype),
                        pltpu.VMEM((2, tm, sz), F32), pltpu.SemaphoreType.DMA((7,)), pltpu.SemaphoreType.DMA((7,)),
                        pltpu.SemaphoreType.DMA((3,)), pltpu.SemaphoreType.DMA((2,))],
        compiler_params=_params(),
    )(h_all, w_shard)


def _all_to_all(arrs, axes, name):
    n = len(arrs)
    sizes = [a.shape[ax] // N_DEV for a, ax in zip(arrs, axes)]

    def body(*refs):
        ins, outs, token = refs[:n], refs[n:2 * n], refs[2 * n]
        send_sems, recv_sems, local_sems = refs[2 * n + 1:]
        token[...] = jnp.zeros_like(token)
        me = _my_pos()
        my = _lin(me)
        copies, local = [], []
        for a in range(n):
            cp = pltpu.make_async_copy(_slab(ins[a], axes[a], my, sizes[a]), outs[a].at[my], local_sems.at[a])
            cp.start()
            local.append(cp)
        for d in range(1, N_DEV):
            peer = _flip(me, d)
            for a in range(n):
                cp = pltpu.make_async_remote_copy(
                    src_ref=_slab(ins[a], axes[a], _lin(peer), sizes[a]), dst_ref=outs[a].at[my],
                    send_sem=send_sems.at[7 * a + d - 1], recv_sem=recv_sems.at[7 * a + d - 1],
                    device_id=peer, device_id_type=MESH)
                cp.start()
                copies.append((a, d, cp))
        for a, d, cp in copies:
            peer = _flip(me, d)
            pltpu.make_async_remote_copy(
                src_ref=_slab(ins[a], axes[a], _lin(peer), sizes[a]), dst_ref=outs[a].at[_lin(peer)],
                send_sem=send_sems.at[7 * a + d - 1], recv_sem=recv_sems.at[7 * a + d - 1],
                device_id=peer, device_id_type=MESH).wait_recv()
            cp.wait_send()
        for cp in local:
            cp.wait()

    out_shape = []
    for a, ax, s in zip(arrs, axes, sizes):
        shp = list(a.shape)
        shp[ax] = s
        out_shape.append(jax.ShapeDtypeStruct((N_DEV,) + tuple(shp), a.dtype))
    out_shape.append(jax.ShapeDtypeStruct((SUBLANES, LANES), F32))
    res = pl.pallas_call(
        body, name=name, out_shape=tuple(out_shape),
        in_specs=[HBM_SPEC] * n, out_specs=tuple([HBM_SPEC] * n + [pl.BlockSpec(memory_space=pltpu.VMEM)]),
        scratch_shapes=[pltpu.SemaphoreType.DMA((7 * n,)), pltpu.SemaphoreType.DMA((7 * n,)),
                        pltpu.SemaphoreType.DMA((n,))],
    )(*arrs)
    return list(res[:n]), res[n]


SEM_SPEC = pl.BlockSpec(memory_space=pltpu.SEMAPHORE)
_EFFECT = pltpu.SideEffectType.DATAFLOW_SIDE_EFFECTING
N_PEERS = N_DEV - 1


def _split_copies(kind, src_ref, land_ref, send_sems, recv_sems, axis, size, arriving):
    me = _my_pos()
    out = []
    for dd in (range(2, N_DEV, 2) if kind in ("chips", "gather_chips") else range(1, N_DEV)):
        peer = _flip(me, dd)
        k = dd - 1
        if kind in ("gather", "gather_chips"):
            src, dst = src_ref, _slab(land_ref, axis, _lin(peer if arriving else me), size)
        elif kind == "scatter":
            src, dst = _slab(src_ref, axis, _lin(peer), size), land_ref.at[k]
        else:
            k = dd // 2 - 1
            src, dst = src_ref.at[2 * peer[0] + peer[1]], land_ref.at[k]
        out.append(pltpu.make_async_remote_copy(
            src_ref=src, dst_ref=dst, send_sem=send_sems.at[k], recv_sem=recv_sems.at[k],
            device_id=peer, device_id_type=MESH))
    return out


N_CHIPS = N_DEV // 2


def _split_size(kind, src_shape, axis):
    return src_shape[axis] // {"gather": 1, "gather_chips": 1, "scatter": N_DEV, "chips": N_CHIPS}[kind]


def _sibling_swap(full, axis, name):
    size = full.shape[axis] // N_DEV

    def body(in_ref, o_ref, send_sems, recv_sems):
        me = _my_pos()
        sibling = _flip(me, 1)
        cps = []
        for k in range(N_CHIPS):
            mine = _slab(o_ref, axis, 2 * k + me[2], size)
            cp = pltpu.make_async_remote_copy(src_ref=mine, dst_ref=mine, send_sem=send_sems.at[k],
                                              recv_sem=recv_sems.at[k], device_id=sibling, device_id_type=MESH)
            cp.start()
            cps.append(cp)
        for k, cp in enumerate(cps):
            cp.wait_send()
            theirs = _slab(o_ref, axis, 2 * k + sibling[2], size)
            pltpu.make_async_remote_copy(src_ref=theirs, dst_ref=theirs, send_sem=send_sems.at[k],
                                         recv_sem=recv_sems.at[k], device_id=sibling, device_id_type=MESH).wait_recv()

    return pl.pallas_call(
        body, name=name, out_shape=jax.ShapeDtypeStruct(full.shape, full.dtype), in_specs=[HBM_SPEC], out_specs=HBM_SPEC,
        scratch_shapes=[pltpu.SemaphoreType.DMA((N_CHIPS,)), pltpu.SemaphoreType.DMA((N_CHIPS,))],
        input_output_aliases={0: 0},
    )(full)


def _xfer_start(kind, src, land, axis, name):
    size = _split_size(kind, src.shape, axis)

    def body(src_ref, land_ref, send_sems, recv_sems, src_thru, land_thru, token):
        for cp in _split_copies(kind, src_ref, land_ref, send_sems, recv_sems, axis, size, False):
            cp.start()
        token[...] = jnp.zeros_like(token)

    return pl.pallas_call(
        body, name=name,
        out_shape=(pltpu.SemaphoreType.DMA((N_PEERS,)), pltpu.SemaphoreType.DMA((N_PEERS,)),
                   pltpu.HBM(src.shape, src.dtype), pltpu.HBM(land.shape, land.dtype),
                   jax.ShapeDtypeStruct((SUBLANES, LANES), F32)),
        in_specs=(HBM_SPEC, HBM_SPEC),
        out_specs=(SEM_SPEC, SEM_SPEC, HBM_SPEC, HBM_SPEC, pl.BlockSpec(memory_space=pltpu.VMEM)),
        input_output_aliases={0: 2, 1: 3},
        compiler_params=pltpu.CompilerParams(has_side_effects=_EFFECT),
    )(pltpu.with_memory_space_constraint(src, pltpu.HBM), pltpu.with_memory_space_constraint(land, pltpu.HBM))


def _xfer_wait(kind, handle, axis, after, name):
    send_sems, recv_sems, src_thru, land_thru, _ = handle
    size = _split_size(kind, src_thru.shape, axis)

    def body(src_ref, land_ref, send_sems, recv_sems, after_ref, src_dead, got_ref):
        for cp in _split_copies(kind, src_ref, land_ref, send_sems, recv_sems, axis, size, True):
            cp.wait_send()
            cp.wait_recv()

    return pl.pallas_call(
        body, name=name,
        out_shape=(pltpu.HBM(src_thru.shape, src_thru.dtype), pltpu.HBM(land_thru.shape, land_thru.dtype)),
        in_specs=(HBM_SPEC, HBM_SPEC, SEM_SPEC, SEM_SPEC, pl.BlockSpec(memory_space=pl.ANY)),
        out_specs=(HBM_SPEC, HBM_SPEC), input_output_aliases={0: 0, 1: 1},
        compiler_params=pltpu.CompilerParams(has_side_effects=_EFFECT),
    )(src_thru, land_thru, send_sems, recv_sems, after)[1]


def _place_block(shard, axis, my, name):
    r, c = shard.shape
    tr = _pick(r, max(16, (1 << 19) // c), 16)
    nr = r // tr
    if axis == 0:
        full, out_map = (r * N_DEV, c), (lambda i, my_ref: (my_ref[0] * nr + i, 0))
    else:
        full, out_map = (r, c * N_DEV), (lambda i, my_ref: (i, my_ref[0]))

    def body(my_ref, x_ref, o_ref):
        o_ref[...] = x_ref[...].astype(BF16)

    return pl.pallas_call(
        body, name=name, out_shape=jax.ShapeDtypeStruct(full, BF16),
        grid_spec=pltpu.PrefetchScalarGridSpec(
            num_scalar_prefetch=1, grid=(nr,), in_specs=[pl.BlockSpec((tr, c), lambda i, my_ref: (i, 0))],
            out_specs=pl.BlockSpec((tr, c), out_map)),
        compiler_params=_params(("arbitrary",)),
    )(jnp.reshape(my, (1,)).astype(jnp.int32), shard)


def _scatter_start(grad, axis, my, name):
    shp = list(grad.shape)
    shp[axis] //= N_DEV
    own = lax.dynamic_slice_in_dim(grad, my * shp[axis], shp[axis], axis)
    return (_xfer_start("scatter", grad, lax.empty((N_PEERS,) + tuple(shp), grad.dtype), axis, name), own, axis,
            "scatter")


def _pair_reduce(grad, my, name):
    r, cols = grad.shape
    sz = cols // N_DEV

    def exchange(g_ref, got_ref, send_sems, recv_sems):
        me = _my_pos()
        sibling = _flip(me, 1)
        cps = []
        for k in range(N_CHIPS):
            cp = pltpu.make_async_remote_copy(
                src_ref=_slab(g_ref, 1, 2 * k + sibling[2], sz), dst_ref=_slab(got_ref, 1, k, sz),
                send_sem=send_sems.at[k], recv_sem=recv_sems.at[k], device_id=sibling, device_id_type=MESH)
            cp.start()
            cps.append(cp)
        for cp in cps:
            cp.wait()

    got = pl.pallas_call(
        exchange, name=name + "_swap", out_shape=jax.ShapeDtypeStruct((r, N_CHIPS * sz), grad.dtype),
        in_specs=[HBM_SPEC], out_specs=HBM_SPEC,
        scratch_shapes=[pltpu.SemaphoreType.DMA((N_CHIPS,)), pltpu.SemaphoreType.DMA((N_CHIPS,))],
    )(grad)

    tr = _pick(r, 512, 16)

    def add(c_ref, g_ref, p_ref, o_ref):
        o_ref[...] = (g_ref[...].astype(F32) + p_ref[...].astype(F32)).astype(o_ref.dtype)

    def add(c_ref, g_ref, p_ref, o_ref):
        o_ref[0] = (g_ref[...].astype(F32) + p_ref[...].astype(F32)).astype(o_ref.dtype)

    return pl.pallas_call(
        add, name=name + "_sum", out_shape=jax.ShapeDtypeStruct((N_CHIPS, r, sz), grad.dtype),
        grid_spec=pltpu.PrefetchScalarGridSpec(
            num_scalar_prefetch=1, grid=(N_CHIPS, r // tr),
            in_specs=[pl.BlockSpec((tr, sz), lambda k, i, c_ref: (i, 2 * k + c_ref[0])),
                      pl.BlockSpec((tr, sz), lambda k, i, c_ref: (i, k))],
            out_specs=pl.BlockSpec((1, tr, sz), lambda k, i, c_ref: (k, i, 0))),
        compiler_params=_params(("parallel", "parallel")),
    )(jnp.reshape(my % 2, (1,)).astype(jnp.int32), grad, got)


def _chip_scatter_start(pair_sum, my, name):
    _, r, sz = pair_sum.shape
    own = lax.dynamic_index_in_dim(pair_sum, my // 2, 0, keepdims=False)
    return (_xfer_start("chips", pair_sum, lax.empty((N_CHIPS - 1, r, sz), pair_sum.dtype), 0, name), own, 0, "chips")


def _tok(a, *tokens):
    for t in tokens:
        a = a + t[0, 0].astype(a.dtype)
    return a


def _mm(a, b, *, name, ta=False, tb=False, out_dtype=F32, a_fn=None, out_fn=None, extras=(), bias=None,
        tm=1024, tn=1024, tk=2816):
    if ta:
        kdim, m = a.shape
    else:
        m, kdim = a.shape
    if tb:
        n, kb = b.shape
    else:
        kb, n = b.shape
    assert kb == kdim, (a.shape, b.shape, ta, tb)
    tm = _pick(m, tm, LANES if ta else SUBLANES)
    tn = _pick(n, tn, LANES)
    tk = _pick(kdim, tk, LANES)
    nk = kdim // tk
    ne = len(extras)
    has_bias = bias is not None

    def body(*refs):
        a_ref, b_ref = refs[0], refs[1]
        pos = 2
        bias_ref = None
        if has_bias:
            bias_ref = refs[pos]
            pos += 1
        extra_refs = refs[pos:pos + ne]
        o_ref = refs[pos + ne]
        av = a_ref[...]
        if a_fn is not None:
            av = a_fn(av)
        dn = (((0 if ta else 1,), (1 if tb else 0,)), ((), ()))
        part = lax.dot_general(av.astype(BF16), b_ref[...].astype(BF16), dn, preferred_element_type=F32)

        def finish(r):
            if has_bias:
                r = r + bias_ref[...]
            if out_fn is not None:
                r = out_fn(r, *[e[...] for e in extra_refs])
            o_ref[...] = r.astype(out_dtype)

        if nk == 1:
            finish(part)
            return
        acc = refs[pos + ne + 1]
        k = pl.program_id(2)

        @pl.when(k == 0)
        def _():
            acc[...] = part

        @pl.when(k > 0)
        def _():
            acc[...] += part

        @pl.when(k == nk - 1)
        def _():
            finish(acc[...])

    in_specs = [
        pl.BlockSpec((tk, tm), lambda i, j, k: (k, i)) if ta else pl.BlockSpec((tm, tk), lambda i, j, k: (i, k)),
        pl.BlockSpec((tn, tk), lambda i, j, k: (j, k)) if tb else pl.BlockSpec((tk, tn), lambda i, j, k: (k, j)),
    ]
    args = [a, b]
    if has_bias:
        in_specs.append(pl.BlockSpec((1, tn), lambda i, j, k: (0, j)))
        args.append(bias)
    for e in extras:
        in_specs.append(pl.BlockSpec((tm, tn), lambda i, j, k: (i, j)))
        args.append(e)
    return pl.pallas_call(
        body, name=name, grid=(m // tm, n // tn, nk),
        in_specs=in_specs, out_specs=pl.BlockSpec((tm, tn), lambda i, j, k: (i, j)),
        out_shape=jax.ShapeDtypeStruct((m, n), out_dtype),
        scratch_shapes=[pltpu.VMEM((tm, tn), F32)] if nk > 1 else [],
        compiler_params=_params(("parallel", "parallel", "arbitrary")),
    )(*args)


def _row(arr, cb=0, rb=0, w=None):
    return ("row", arr, cb, rb, w)


def _bc(arr, cb=0, w=None):
    return ("bc", arr, cb, 0, w)


def _full(arr):
    return ("full", arr, 0, 0, None)


def _rowwise(fn, ins, outs, *, nr, tr, nc, cw, name):
    n_in = len(ins)
    outs = [(o[0], o[1], (o[2] if len(o) > 2 and o[2] else cw)) for o in outs]

    def body(*refs):
        in_refs, out_refs = refs[:n_in], refs[n_in:]
        j, i = pl.program_id(0), pl.program_id(1)
        res = fn(i, j, *[r[...] for r in in_refs])
        if not isinstance(res, (tuple, list)):
            res = (res,)
        for (kind, _, _), o_ref, r in zip(outs, out_refs, res):
            if kind == "row":
                o_ref[...] = r.astype(o_ref.dtype)
            else:
                @pl.when(i == 0)
                def _(o_ref=o_ref):
                    o_ref[...] = jnp.zeros_like(o_ref)

                o_ref[...] += r

    in_specs, args = [], []
    for kind, arr, cb, rb, w in ins:
        w = w or cw
        if kind == "row":
            in_specs.append(pl.BlockSpec((tr, w), lambda j, i, cb=cb, rb=rb: (i + rb, j + cb)))
        elif kind == "bc":
            in_specs.append(pl.BlockSpec((arr.shape[0], w), lambda j, i, cb=cb: (0, j + cb)))
        else:
            in_specs.append(pl.BlockSpec(arr.shape, lambda j, i, nd=arr.ndim: (0,) * nd))
        args.append(arr)
    out_specs, out_shape = [], []
    for kind, spec, w in outs:
        if kind == "row":
            out_specs.append(pl.BlockSpec((tr, w), lambda j, i: (i, j)))
            out_shape.append(jax.ShapeDtypeStruct((nr * tr, nc * w), spec))
        else:
            out_specs.append(pl.BlockSpec((spec, w), lambda j, i: (0, j)))
            out_shape.append(jax.ShapeDtypeStruct((spec, nc * w), F32))
    res = pl.pallas_call(
        body, name=name, grid=(nc, nr), in_specs=in_specs, out_specs=tuple(out_specs), out_shape=tuple(out_shape),
        compiler_params=_params(("parallel", "arbitrary")),
    )(*args)
    return res


def _sum_parts(own, parts, *, name):
    p, r, c = parts.shape
    tr = _pick(r, max(SUBLANES, (1 << 18) // c), SUBLANES)

    def body(o_ref, p_ref, g_out):
        g = o_ref[...].astype(F32)
        for q in range(p):
            g = g + p_ref[q].astype(F32)
        g_out[...] = g

    blk = pl.BlockSpec((tr, c), lambda i: (i, 0))
    return pl.pallas_call(
        body, name=name, grid=(r // tr,), in_specs=[blk, pl.BlockSpec((p, tr, c), lambda i: (0, i, 0))],
        out_specs=blk, out_shape=jax.ShapeDtypeStruct((r, c), F32), compiler_params=_params(("parallel",)),
    )(own, parts)


def _adamw(parts, w, m, v, *, name, own=None):
    p, r, c = parts.shape
    tr = _pick(r, max(SUBLANES, (1 << 18) // c), 16 if parts.dtype == BF16 else SUBLANES)
    bc1 = 1.0 - ADAM_B1 ** ADAM_STEP
    bc2 = 1.0 - ADAM_B2 ** ADAM_STEP
    has_own = own is not None

    def body(*refs):
        p_ref, w_ref, m_ref, v_ref = refs[:4]
        g_out, d_out, m_out, v_out = refs[4 + has_own:]
        if has_own:
            g = refs[4][...].astype(F32) + p_ref[0].astype(F32)
        else:
            g = p_ref[0].astype(F32)
        for q in range(1, p):
            g = g + p_ref[q].astype(F32)
        m_new = ADAM_B1 * m_ref[...] + (1.0 - ADAM_B1) * g
        v_new = ADAM_B2 * v_ref[...] + (1.0 - ADAM_B2) * (g * g)
        m_hat = m_new / bc1
        v_hat = v_new / bc2
        g_out[...] = g
        d_out[...] = -ADAM_LR * (m_hat / (jnp.sqrt(v_hat) + ADAM_EPS) + ADAM_WD * w_ref[...])
        m_out[...] = m_new
        v_out[...] = v_new

    blk = pl.BlockSpec((tr, c), lambda i: (i, 0))
    return pl.pallas_call(
        body, name=name, grid=(r // tr,),
        in_specs=[pl.BlockSpec((p, tr, c), lambda i: (0, i, 0)), blk, blk, blk] + ([blk] if has_own else []),
        out_specs=(blk, blk, blk, blk), out_shape=tuple(jax.ShapeDtypeStruct((r, c), F32) for _ in range(4)),
        compiler_params=_params(("parallel",)),
    )(*([parts, w, m, v] + ([own] if has_own else [])))


_GELU_C = math.sqrt(2.0 / math.pi)
_GELU_K = 0.044715


def _sigmoid(z):
    return 0.5 * jnp.tanh(0.5 * z) + 0.5


def _silu(z):
    return z * _sigmoid(z)


def _dsilu(z):
    s = _sigmoid(z)
    return s * (1.0 + z * (1.0 - s))


def _gelu(z):
    return 0.5 * z * (1.0 + jnp.tanh(_GELU_C * (z + _GELU_K * z * z * z)))


def _dgelu(z):
    t = jnp.tanh(_GELU_C * (z + _GELU_K * z * z * z))
    return 0.5 * (1.0 + t) + 0.5 * z * (1.0 - t * t) * (_GELU_C * (1.0 + 3.0 * _GELU_K * z * z))


def _softplus(z):
    return jnp.maximum(z, 0.0) + jnp.log(1.0 + jnp.exp(-jnp.abs(z)))


def _expm1(z):
    small = z * (1.0 + z * (0.5 + z * (1.0 / 6.0 + z * (1.0 / 24.0))))
    return jnp.where(jnp.abs(z) < 0.03, small, jnp.exp(z) - 1.0)


def _rms(xv):
    return lax.rsqrt(jnp.mean(xv * xv, axis=-1, keepdims=True) + NORM_EPS)


def _rms_bwd(xhat, r, dxhat):
    return r * (dxhat - xhat * jnp.mean(dxhat * xhat, axis=-1, keepdims=True))


def _swap_pairs(y):
    lane = lax.broadcasted_iota(jnp.int32, y.shape, 1)
    return jnp.where(lane % 2 == 0, pltpu.roll(y, LANES - 1, 1), pltpu.roll(y, 1, 1))


def _dot(a, b, ca, cb):
    return lax.dot_general(a, b, (((ca,), (cb,)), ((), ())), preferred_element_type=F32)


_SM_SCALE = HEAD_DIM ** -0.5
_EXP2_SCALE = _SM_SCALE * math.log2(math.e)


def _softmax_parts(qh, kk):
    s = _dot(qh, kk, 1, 1)
    p = jnp.exp2((s - jnp.max(s, axis=-1, keepdims=True)) * _EXP2_SCALE)
    return p, 1.0 / jnp.sum(p, axis=-1, keepdims=True)


def _attn_fwd(q, k, v, *, tq, name):
    s_len, aw = q.shape
    t_len, kvw = k.shape
    gw = Q_GROUP * HEAD_DIM

    def body(q_ref, k_ref, v_ref, o_ref):
        kk, vv = k_ref[...], v_ref[...]
        for g in range(Q_GROUP):
            cols = slice(g * HEAD_DIM, (g + 1) * HEAD_DIM)
            p, inv_l = _softmax_parts(q_ref[:, cols], kk)
            o_ref[:, cols] = (_dot(p.astype(BF16), vv, 1, 0) * inv_l).astype(BF16)

    return pl.pallas_call(
        body, name=name, grid=(kvw // HEAD_DIM, s_len // tq),
        in_specs=[pl.BlockSpec((tq, gw), lambda h, i: (i, h)), pl.BlockSpec((t_len, HEAD_DIM), lambda h, i: (0, h)),
                  pl.BlockSpec((t_len, HEAD_DIM), lambda h, i: (0, h))],
        out_specs=pl.BlockSpec((tq, gw), lambda h, i: (i, h)),
        out_shape=jax.ShapeDtypeStruct((s_len, aw), BF16),
        compiler_params=_params(("parallel", "parallel")),
    )(q, k, v)


def _attn_bwd(q, k, v, do, *, tq, name):
    s_len, aw = q.shape
    t_len, kvw = k.shape
    gw = Q_GROUP * HEAD_DIM

    def body(q_ref, k_ref, v_ref, do_ref, dq_ref, dk_ref, dv_ref):
        @pl.when(pl.program_id(1) == 0)
        def _():
            dk_ref[...] = jnp.zeros_like(dk_ref)
            dv_ref[...] = jnp.zeros_like(dv_ref)

        kk, vv = k_ref[...], v_ref[...]
        for g in range(Q_GROUP):
            cols = slice(g * HEAD_DIM, (g + 1) * HEAD_DIM)
            qh, doh = q_ref[:, cols], do_ref[:, cols]
            p, inv_l = _softmax_parts(qh, kk)
            dp = _dot(doh, vv, 1, 1)
            delta = jnp.sum(p * dp, axis=-1, keepdims=True) * inv_l
            dsb = (p * ((dp - delta) * (inv_l * _SM_SCALE))).astype(BF16)
            dq_ref[:, cols] = _dot(dsb, kk, 1, 0)
            dk_ref[...] += _dot(dsb, qh, 0, 0)
            dv_ref[...] += _dot(p.astype(BF16), (doh * inv_l).astype(BF16), 0, 0)

    qspec = pl.BlockSpec((tq, gw), lambda h, i: (i, h))
    kspec = pl.BlockSpec((t_len, HEAD_DIM), lambda h, i: (0, h))
    return pl.pallas_call(
        body, name=name, grid=(kvw // HEAD_DIM, s_len // tq),
        in_specs=[qspec, kspec, kspec, qspec], out_specs=(qspec, kspec, kspec),
        out_shape=(jax.ShapeDtypeStruct((s_len, aw), F32), jax.ShapeDtypeStruct((t_len, kvw), F32),
                   jax.ShapeDtypeStruct((t_len, kvw), F32)),
        compiler_params=_params(("parallel", "arbitrary")),
    )(q, k, v, do)


SHIFTS = (-2, -1, 1, 2)


def _shift_masks(l_len, t_len):
    t = jnp.arange(t_len)[:, None]
    cols = []
    for d in SHIFTS:
        td = t + d
        ok = ((t < l_len) & (td >= 0) & (td < l_len)) | ((t >= l_len) & (td >= l_len) & (td < t_len))
        cols.append(jnp.broadcast_to(ok.astype(F32), (t_len, RNN_BLOCK)))
    return jnp.concatenate(cols, axis=1)


def _shift_rows(xv, d, m_ref, l_len, t_len):
    if d == 0:
        return xv
    k = SHIFTS.index(d)
    return pltpu.roll(xv, (-d) % t_len, 0) * m_ref[:, k * RNN_BLOCK:(k + 1) * RNN_BLOCK]


def _sublane_scan(a8, b8, rev):
    row = lax.broadcasted_iota(jnp.int32, a8.shape, 0)
    for s in (1, 2, 4):
        if rev:
            ok = row < SUBLANES - s
            sh = SUBLANES - s
        else:
            ok = row >= s
            sh = s
        a_sh = jnp.where(ok, pltpu.roll(a8, sh, 0), 1.0)
        b_sh = jnp.where(ok, pltpu.roll(b8, sh, 0), 0.0)
        b8 = a8 * b_sh + b8
        a8 = a8 * a_sh
    return a8, b8


def _scan_chunks(chains, c0, n, carries):
    def step(kk, hs):
        new = []
        for (a_ref, b_ref, h_ref, rev), hprev in zip(chains, hs):
            c = c0 + (n - 1 - kk if rev else kk)
            r0 = pl.multiple_of(c * SUBLANES, SUBLANES)
            big_a, big_b = _sublane_scan(a_ref[pl.ds(r0, SUBLANES), :], b_ref[pl.ds(r0, SUBLANES), :], rev)
            h8 = big_b + big_a * hprev
            h_ref[pl.ds(r0, SUBLANES), :] = h8
            edge = 0 if rev else SUBLANES - 1
            new.append(jnp.broadcast_to(h8[edge:edge + 1, :], h8.shape))
        return tuple(new)

    return lax.fori_loop(0, n, step, tuple(carries))


def _rglru_conv(x, cw_ref, cb_ref, m_ref, l_len, t_len):
    y = cb_ref[...] + _shift_rows(x, -1, m_ref, l_len, t_len) * cw_ref[0:1, :]
    for kk in range(1, CONV_WIDTH):
        y = y + _shift_rows(x, kk - 1, m_ref, l_len, t_len) * cw_ref[kk:kk + 1, :]
    return y


def _rglru_gates(xc, wr, wi, br, bi, lam):
    xb = xc.astype(BF16)
    r = _sigmoid(_dot(xb, wr.astype(BF16), 1, 0) + br)
    ig = _sigmoid(_dot(xb, wi.astype(BF16), 1, 0) + bi)
    sp = _softplus(-lam)
    log_a = (-LRU_C) * r * sp
    a = jnp.exp(log_a)
    mult = jnp.sqrt(-_expm1(2.0 * log_a))
    u = ig * xc
    return r, ig, sp, a, mult, u


def _rglru_specs(proj, xr_blk, xg_blk, t_len):
    col = lambda off: pl.BlockSpec((t_len, RNN_BLOCK), lambda h, off=off: (0, off + h))
    vec = lambda rows: pl.BlockSpec((rows, RNN_BLOCK), lambda h: (0, h))
    wsp = pl.BlockSpec((2, 1, RNN_BLOCK, RNN_BLOCK), lambda h: (0, h, 0, 0))
    msp = pl.BlockSpec((t_len, len(SHIFTS) * RNN_BLOCK), lambda h: (0, 0))
    return col(xr_blk), col(xg_blk), vec, wsp, msp


def _rglru_fwd(proj, xr_blk, xg_blk, masks, conv_w, conv_b, w_rg, w_ig, b_rg, b_ig, lam, *, l_len, name):
    t_len = proj.shape[0]
    s_len = t_len - l_len
    d_rnn = conv_b.shape[1]
    n_l, n_t = l_len // SUBLANES, t_len // SUBLANES

    def body(xr_ref, xg_ref, m_ref, cw_ref, cb_ref, wr_ref, wi_ref, br_ref, bi_ref, lam_ref, rg_ref, hf_ref, hb_ref,
             a_s, b_s):
        xc = _rglru_conv(xr_ref[...], cw_ref, cb_ref, m_ref, l_len, t_len)
        for d in range(2):
            _, _, _, a, mult, u = _rglru_gates(xc, wr_ref[d, 0], wi_ref[d, 0], br_ref[d:d + 1, :],
                                               bi_ref[d:d + 1, :], lam_ref[d:d + 1, :])
            a_s[d] = a
            b_s[d] = mult * u
        chains = [(a_s.at[0], b_s.at[0], hf_ref, False), (a_s.at[1], b_s.at[1], hb_ref, True)]
        zero = jnp.zeros((SUBLANES, RNN_BLOCK), F32)
        carry = _scan_chunks(chains, 0, n_l, (zero, zero))
        _scan_chunks(chains, n_l, n_t - n_l, carry)
        rnn = hf_ref[pl.ds(l_len, s_len), :] + hb_ref[pl.ds(l_len, s_len), :]
        rg_ref[...] = (rnn * _gelu(xg_ref[pl.ds(l_len, s_len), :])).astype(BF16)

    xr_spec, xg_spec, vec, wsp, msp = _rglru_specs(proj, xr_blk, xg_blk, t_len)
    col_t = pl.BlockSpec((t_len, RNN_BLOCK), lambda h: (0, h))
    return pl.pallas_call(
        body, name=name, grid=(d_rnn // RNN_BLOCK,),
        in_specs=[xr_spec, xg_spec, msp, vec(CONV_WIDTH), vec(1), wsp, wsp, vec(2), vec(2), vec(2)],
        out_specs=(pl.BlockSpec((s_len, RNN_BLOCK), lambda h: (0, h)), col_t, col_t),
        out_shape=(jax.ShapeDtypeStruct((s_len, d_rnn), BF16), jax.ShapeDtypeStruct((t_len, d_rnn), F32),
                   jax.ShapeDtypeStruct((t_len, d_rnn), F32)),
        scratch_shapes=[pltpu.VMEM((2, t_len, RNN_BLOCK), F32)] * 2,
        compiler_params=_params(("parallel",)),
    )(proj, proj, masks, conv_w, conv_b, w_rg, w_ig, b_rg, b_ig, lam)


N_RNN_VEC = 16


def _rglru_bwd(proj, xr_blk, xg_blk, masks, h_fwd, h_bwd, drg, conv_w, conv_b, w_rg, w_ig, b_rg, b_ig, lam, *, l_len, name):
    t_len = proj.shape[0]
    s_len = t_len - l_len
    d_rnn = conv_b.shape[1]
    n_l, n_t = l_len // SUBLANES, t_len // SUBLANES

    def body(xr_ref, xg_ref, m_ref, hf_ref, hb_ref, drg_ref, cw_ref, cb_ref, wr_ref, wi_ref, br_ref, bi_ref, lam_ref,
             dxr_ref, dxg_ref, dvec_ref, dwr_ref, dwi_ref, a_s, b_s, h_s):
        t = lax.broadcasted_iota(jnp.int32, (t_len, RNN_BLOCK), 0)
        x = xr_ref[...]
        xc = _rglru_conv(x, cw_ref, cb_ref, m_ref, l_len, t_len)
        xb = xc.astype(BF16)
        gates = []
        for d in range(2):
            gt = _rglru_gates(xc, wr_ref[d, 0], wi_ref[d, 0], br_ref[d:d + 1, :], bi_ref[d:d + 1, :],
                              lam_ref[d:d + 1, :])
            gates.append(gt)
        zero = jnp.zeros((SUBLANES, RNN_BLOCK), F32)
        h0, h1 = hf_ref[...], hb_ref[...]
        xg = xg_ref[...]
        dr = jnp.concatenate([jnp.zeros((l_len, RNN_BLOCK), F32), drg_ref[...].astype(F32)], axis=0)
        dxg_ref[...] = (dr * (h0 + h1) * _dgelu(xg)).astype(dxg_ref.dtype)
        drnn = dr * _gelu(xg)
        a_s[0] = pltpu.roll(gates[0][3], t_len - 1, 0)
        a_s[1] = pltpu.roll(gates[1][3], 1, 0)
        b_s[0] = drnn
        b_s[1] = drnn
        hprev = [jnp.where(t == 0, 0.0, pltpu.roll(h0, 1, 0)),
                 jnp.where(t == l_len - 1, 0.0, pltpu.roll(h1, t_len - 1, 0))]
        chains = [(a_s.at[0], b_s.at[0], h_s.at[0], True), (a_s.at[1], b_s.at[1], h_s.at[1], False)]
        carry = _scan_chunks(chains, n_l, n_t - n_l, (zero, zero))
        _scan_chunks(chains, 0, n_l, carry)

        dxc = jnp.zeros((t_len, RNN_BLOCK), F32)
        for d in range(2):
            r, ig, sp, a, mult, u = gates[d]
            lamb = h_s[d]
            da = lamb * hprev[d]
            dmult = lamb * u
            du = lamb * mult
            dlog_a = da * a - dmult * (a * a) / mult
            dzr = (dlog_a * ((-LRU_C) * sp)) * r * (1.0 - r)
            dsp = jnp.sum(dlog_a * ((-LRU_C) * r), axis=0, keepdims=True)
            dzi = (du * xc) * ig * (1.0 - ig)
            dzrb, dzib = dzr.astype(BF16), dzi.astype(BF16)
            dxc = dxc + du * ig + _dot(dzrb, wr_ref[d, 0].astype(BF16), 1, 1) + _dot(dzib, wi_ref[d, 0].astype(BF16), 1, 1)
            dwr_ref[d, 0] = _dot(xb, dzrb, 0, 0)
            dwi_ref[d, 0] = _dot(xb, dzib, 0, 0)
            dvec_ref[5 + d:6 + d, :] = jnp.sum(dzr, axis=0, keepdims=True)
            dvec_ref[7 + d:8 + d, :] = jnp.sum(dzi, axis=0, keepdims=True)
            dvec_ref[9 + d:10 + d, :] = dsp * (-_sigmoid(-lam_ref[d:d + 1, :]))
        dvec_ref[4:5, :] = jnp.sum(dxc, axis=0, keepdims=True)
        dvec_ref[11:N_RNN_VEC, :] = jnp.zeros((N_RNN_VEC - 11, RNN_BLOCK), F32)
        dxr = jnp.zeros((t_len, RNN_BLOCK), F32)
        for kk in range(CONV_WIDTH):
            dvec_ref[kk:kk + 1, :] = jnp.sum(dxc * _shift_rows(x, kk - 1, m_ref, l_len, t_len), axis=0, keepdims=True)
            dxr = dxr + _shift_rows(dxc, 1 - kk, m_ref, l_len, t_len) * cw_ref[kk:kk + 1, :]
        dxr_ref[...] = dxr.astype(dxr_ref.dtype)

    xr_spec, xg_spec, vec, wsp, msp = _rglru_specs(proj, xr_blk, xg_blk, t_len)
    col_t = pl.BlockSpec((t_len, RNN_BLOCK), lambda h: (0, h))
    return pl.pallas_call(
        body, name=name, grid=(d_rnn // RNN_BLOCK,),
        in_specs=[xr_spec, xg_spec, msp, col_t, col_t, pl.BlockSpec((s_len, RNN_BLOCK), lambda h: (0, h)), vec(CONV_WIDTH),
                  vec(1),
                  wsp, wsp, vec(2), vec(2), vec(2)],
        out_specs=(col_t, col_t, vec(N_RNN_VEC), wsp, wsp),
        out_shape=(jax.ShapeDtypeStruct((t_len, d_rnn), BF16), jax.ShapeDtypeStruct((t_len, d_rnn), BF16),
                   jax.ShapeDtypeStruct((N_RNN_VEC, d_rnn), F32), jax.ShapeDtypeStruct(w_rg.shape, F32),
                   jax.ShapeDtypeStruct(w_ig.shape, F32)),
        scratch_shapes=[pltpu.VMEM((2, t_len, RNN_BLOCK), F32)] * 3,
        compiler_params=_params(("parallel",)),
    )(proj, proj, masks, h_fwd, h_bwd, drg, conv_w, conv_b, w_rg, w_ig, b_rg, b_ig, lam)


def _colsum(v):
    return jnp.sum(v, axis=0, keepdims=True)


def _qk_fwd_heads(xv, gain, cos_e, sin_e):
    outs = []
    for h in range(xv.shape[1] // HEAD_DIM):
        xh = xv[:, h * HEAD_DIM:(h + 1) * HEAD_DIM]
        y = xh * _rms(xh) * gain
        outs.append(y * cos_e + _swap_pairs(y) * sin_e)
    return outs[0] if len(outs) == 1 else jnp.concatenate(outs, axis=1)


def _qk_bwd_heads(xv, dov, gain, cos_e, sin_e):
    outs = []
    dgain = jnp.zeros((1, HEAD_DIM), F32)
    for h in range(xv.shape[1] // HEAD_DIM):
        cols = slice(h * HEAD_DIM, (h + 1) * HEAD_DIM)
        xh, doh = xv[:, cols], dov[:, cols]
        r = _rms(xh)
        xhat = xh * r
        dy = doh * cos_e + _swap_pairs(doh * sin_e)
        dgain = dgain + _colsum(dy * xhat)
        outs.append(_rms_bwd(xhat, r, dy * gain))
    return (outs[0] if len(outs) == 1 else jnp.concatenate(outs, axis=1)), dgain


WEIGHTS = ['c_ctx', 'w_mod', 'b_mod', 'g_mix', 'g_mlp', 'w_in', 'q_gain', 'k_gain', 'conv_w', 'conv_b', 'w_rg', 'b_rg',
           'w_ig', 'b_ig', 'lru_lambda', 'w_o_attn', 'w_o_rnn', 'w_out', 'w_up', 'w_down', 'g_final']
REPLICATED = ['c_ctx', 'b_mod', 'g_mix', 'g_mlp', 'q_gain', 'k_gain', 'conv_b', 'w_rg', 'w_ig', 'g_final']
SMALL_SHARDED = [('conv_w', CONV_WIDTH), ('b_rg', 2), ('b_ig', 2), ('lru_lambda', 2)]
SMALL_ROWS = 16


PACK_ROWS = 512


def _pack_flat(arrs, rows_mult):
    flat = jnp.concatenate([a.reshape(-1) for a in arrs])
    n = flat.shape[0]
    unit = rows_mult * LANES
    pad = (-n) % unit
    if pad:
        flat = jnp.concatenate([flat, jnp.zeros((pad,), flat.dtype)])
    return flat.reshape(-1, LANES)


def _unpack_flat(packed, shapes):
    flat = packed.reshape(-1)
    out, pos = [], 0
    for shp in shapes:
        n = math.prod(shp)
        out.append(flat[pos:pos + n].reshape(shp))
        pos += n
    return out


def _pack_small(d):
    rows = [d[n][0] for n, _ in SMALL_SHARDED]
    cols = rows[0].shape[1]
    used = sum(k for _, k in SMALL_SHARDED)
    return jnp.concatenate(rows + [jnp.zeros((SMALL_ROWS - used, cols), F32)], axis=0)


def _unpack_small(p):
    out, pos = {}, 0
    for n, k in SMALL_SHARDED:
        out[n] = p[pos:pos + k][None]
        pos += k
    return out


def kernel(x, c, ctx, c_ctx, w_mod, b_mod, g_mix, g_mlp, w_in, q_gain, k_gain, conv_w, conv_b, w_rg, b_rg, w_ig, b_ig, lru_lambda, w_o_attn, w_o_rnn, w_out, w_up, w_down, g_final, loss_target, m_c_ctx, m_w_mod, m_b_mod, m_g_mix, m_g_mlp, m_w_in, m_q_gain, m_k_gain, m_conv_w, m_conv_b, m_w_rg, m_b_rg, m_w_ig, m_b_ig, m_lru_lambda, m_w_o_attn, m_w_o_rnn, m_w_out, m_w_up, m_w_down, m_g_final, v_c_ctx, v_w_mod, v_b_mod, v_g_mix, v_g_mlp, v_w_in, v_q_gain, v_k_gain, v_conv_w, v_conv_b, v_w_rg, v_b_rg, v_w_ig, v_b_ig, v_lru_lambda, v_w_o_attn, v_w_o_rnn, v_w_out, v_w_up, v_w_down, v_g_final):
    wd = dict(c_ctx=c_ctx, w_mod=w_mod, b_mod=b_mod, g_mix=g_mix, g_mlp=g_mlp, w_in=w_in, q_gain=q_gain, k_gain=k_gain,
              conv_w=conv_w, conv_b=conv_b, w_rg=w_rg, b_rg=b_rg, w_ig=w_ig, b_ig=b_ig, lru_lambda=lru_lambda,
              w_o_attn=w_o_attn, w_o_rnn=w_o_rnn, w_out=w_out, w_up=w_up, w_down=w_down, g_final=g_final)
    md = dict(zip(WEIGHTS, (m_c_ctx, m_w_mod, m_b_mod, m_g_mix, m_g_mlp, m_w_in, m_q_gain, m_k_gain, m_conv_w, m_conv_b,
                            m_w_rg, m_b_rg, m_w_ig, m_b_ig, m_lru_lambda, m_w_o_attn, m_w_o_rnn, m_w_out, m_w_up,
                            m_w_down, m_g_final)))
    vd = dict(zip(WEIGHTS, (v_c_ctx, v_w_mod, v_b_mod, v_g_mix, v_g_mlp, v_w_in, v_q_gain, v_k_gain, v_conv_w, v_conv_b,
                            v_w_rg, v_b_rg, v_w_ig, v_b_ig, v_lru_lambda, v_w_o_attn, v_w_o_rnn, v_w_out, v_w_up,
                            v_w_down, v_g_final)))

    xs, tgt, ctxs = x[0], loss_target[0], ctx[0]
    s_len, d = xs.shape
    l_len = ctxs.shape[0]
    t_len = s_len + l_len
    n_in = w_in.shape[2] * N_DEV
    aw = (n_in - 4 * d) * 2 // 3
    kvw = aw // Q_GROUP
    assert aw == d and n_in == aw + 2 * kvw + 4 * d
    nmod = N_MOD * d // N_DEV
    tr = 128
    assert l_len % tr == 0 and s_len % tr == 0
    lb = l_len // tr
    tq = min(256, s_len)
    my = _lin(_my_pos())
    c_ctx2 = c_ctx[None]
    g_final2 = g_final[None]

    c_all = _all_gather([jnp.broadcast_to(c, (SUBLANES, d))], [0], "ag_cond")[0][::SUBLANES]
    c16 = jnp.concatenate([c_all, jnp.broadcast_to(c_ctx2, (N_DEV, d))], axis=0)
    b_mod_sh = lax.dynamic_slice(b_mod, (0, my * nmod), (1, nmod))
    mod_part = _mm(c16, w_mod[0], a_fn=_silu, bias=b_mod_sh, name="mod_fwd")
    sp_w = _pack_small(wd)
    (mod_g, sp_f), order = _all_gather([mod_part, sp_w], [1, 1], "ag_mod", with_token=True)
    conv_w_f, b_rg_f, b_ig_f, lam_f = sp_f[0:4], sp_f[4:6], sp_f[6:8], sp_f[8:10]
    mod_lat = lax.dynamic_slice(mod_g, (my, 0), (1, N_MOD * d))
    sh_a, sc_a, ga_a, sh_f, sc_f, ga_f = [mod_lat[:, k * d:(k + 1) * d] for k in range(N_MOD)]
    csh_a, csc_a = mod_g[N_DEV:N_DEV + 1, :d], mod_g[N_DEV:N_DEV + 1, d:2 * d]

    x_all = jnp.concatenate([ctxs, xs], axis=0)

    def nm1_fwd(i, j, xv, g, csh, csc, sh, sc):
        is_ctx = i < lb
        n = xv * _rms(xv) * g
        return n * (1.0 + jnp.where(is_ctx, csc, sc)) + jnp.where(is_ctx, csh, sh)

    h_all = _rowwise(nm1_fwd, [_row(x_all), _bc(g_mix), _bc(csh_a), _bc(csc_a), _bc(sh_a), _bc(sc_a)],
                     [("row", BF16)], nr=t_len // tr, tr=tr, nc=1, cw=d, name="norm1_fwd")[0]
    w_in_f, proj, order = _gather_proj(h_all, _tok(w_in[0], order).astype(BF16), "ag_proj")
    late_axis = dict(w_o_attn=0, w_o_rnn=0, w_out=0, w_up=1, w_down=0)
    placed = [_place_block(wd[n][0], ax, my, "place_" + n) for n, ax in late_axis.items()]
    late_kind = lambda n: "gather_chips" if n in ("w_up", "w_down") else "gather"
    late = {}
    for (n, ax), full in zip(late_axis.items(), placed):
        late[n] = _xfer_start(late_kind(n), _tok(wd[n][0], order).astype(BF16), full, ax, "ag_start_" + n)
        order = late[n][4]
    q_gain_fwd = _tok(q_gain, order)

    def late_wait(n, after):
        full = _xfer_wait(late_kind(n), late[n], late_axis[n], after, "ag_wait_" + n)
        return _sibling_swap(full, late_axis[n], "ag_swap_" + n) if late_kind(n) == "gather_chips" else full

    pos = jnp.arange(s_len)
    n_freq = HEAD_DIM // 4
    inv_freq = ROPE_THETA ** (-jnp.arange(n_freq, dtype=F32) / n_freq)
    ang = jnp.concatenate([(pos // GRID_W).astype(F32)[:, None] * inv_freq,
                           (pos % GRID_W).astype(F32)[:, None] * inv_freq], axis=-1)
    sign = jnp.where(jnp.arange(HEAD_DIM) % 2 == 0, -1.0, 1.0).astype(F32)
    cos_e = jnp.concatenate([jnp.ones((l_len, HEAD_DIM), F32), jnp.repeat(jnp.cos(ang), 2, axis=-1)], axis=0)
    sin_e = jnp.concatenate([jnp.zeros((l_len, HEAD_DIM), F32), jnp.repeat(jnp.sin(ang), 2, axis=-1) * sign], axis=0)


    kb = aw // kvw
    q = _rowwise(lambda i, j, xv, gain, ce, se: _qk_fwd_heads(xv, gain, ce, se),
                 [_row(proj, 0, lb, aw), _full(q_gain_fwd), _row(cos_e, 0, lb, HEAD_DIM), _row(sin_e, 0, lb, HEAD_DIM)],
                 [("row", BF16, aw)], nr=s_len // tr, tr=tr, nc=1, cw=aw, name="q_fwd")[0]
    k_all, v_all = _rowwise(lambda i, j, kv_, vv_, gain, ce, se: (_qk_fwd_heads(kv_, gain, ce, se), vv_),
                            [_row(proj, kb, 0, kvw), _row(proj, kb + 1, 0, kvw), _full(k_gain),
                             _row(cos_e, 0, 0, HEAD_DIM), _row(sin_e, 0, 0, HEAD_DIM)],
                            [("row", BF16, kvw), ("row", BF16, kvw)], nr=t_len // tr, tr=tr, nc=1, cw=kvw, name="kv_fwd")
    attn_o = _attn_fwd(q, k_all, v_all, tq=tq, name="attn_fwd")

    xr_blk = (aw + 2 * kvw) // RNN_BLOCK
    xg_blk = xr_blk + d // RNN_BLOCK
    shift_masks = _shift_masks(l_len, t_len)
    rg, h_fwd, h_bwd = _rglru_fwd(proj, xr_blk, xg_blk, shift_masks, conv_w_f, conv_b, w_rg[0], w_ig[0], b_rg_f, b_ig_f, lam_f, l_len=l_len,
                    name="rglru_fwd")

    w_oa_f = late_wait("w_o_attn", attn_o)
    y_attn = _mm(attn_o, w_oa_f, name="o_attn_fwd")
    w_or_f = late_wait("w_o_rnn", rg)
    y_rnn = _mm(rg, w_or_f, name="o_rnn_fwd")
    hw = d // 2
    glb = (aw + 2 * kvw + 2 * d) // hw

    def merge_fwd(i, j, ya, yr, gla, glr):
        return _sigmoid(gla) * ya + _sigmoid(glr) * yr

    u = _rowwise(merge_fwd, [_row(y_attn), _row(y_rnn), _row(proj, glb, lb), _row(proj, glb + 2, lb)],
                 [("row", BF16)], nr=s_len // tr, tr=tr, nc=2, cw=hw, name="merge_fwd")[0]
    w_out_f = late_wait("w_out", u)
    mix = _mm(u, w_out_f, name="out_fwd")

    def res1_fwd(i, j, xv, mixv, gaa, gm, shf, scf):
        x1v = xv + gaa * mixv
        n = x1v * _rms(x1v) * gm
        return x1v, n * (1.0 + scf) + shf

    x1, h2 = _rowwise(res1_fwd, [_row(xs), _row(mix), _bc(ga_a), _bc(g_mlp), _bc(sh_f), _bc(sc_f)],
                      [("row", F32), ("row", BF16)], nr=s_len // tr, tr=tr, nc=1, cw=d, name="norm2_fwd")
    relu2 = lambda z: jnp.square(jnp.maximum(z, 0.0))
    w_up_f = late_wait("w_up", h2)
    up = _mm(h2, w_up_f, name="up_fwd")
    w_down_f = late_wait("w_down", up)
    down = _mm(up, w_down_f, a_fn=relu2, name="down_fwd")

    def final_fn(i, j, x1v, dn, tg, gaf, gf):
        x2 = x1v + gaf * dn
        r = _rms(x2)
        xh = x2 * r
        e = xh * gf - tg
        dy = e * (1.0 / d)
        dx2 = _rms_bwd(xh, r, dy * gf)
        return dx2, dx2 * gaf, _colsum(e * e) * (0.5 / d), _colsum(dy * xh), _colsum(dx2 * dn)

    dx2, ddown, loss_cols, g_gfinal, d_ga_f = _rowwise(
        final_fn, [_row(x1), _row(down), _row(tgt), _bc(ga_f), _bc(g_final2)],
        [("row", F32), ("row", BF16), ("acc", 1), ("acc", 1), ("acc", 1)], nr=s_len // tr, tr=tr, nc=1, cw=d,
        name="final_bwd")
    loss = lax.psum(jnp.sum(loss_cols), ("x", "y", "c"))

    dup = _mm(ddown, w_down_f, tb=True, out_fn=lambda r, upv: r * (2.0 * jnp.maximum(upv, 0.0)), extras=[up],
              out_dtype=BF16, name="down_bwd_x")
    gw_down = _mm(up, ddown, ta=True, a_fn=relu2, out_dtype=BF16, name="down_bwd_w")
    sent = dict(w_down=_scatter_start(gw_down, 0, my, "rs_start_w_down"))
    dh2 = _mm(dup, w_up_f, tb=True, name="up_bwd_x")
    gw_up = _mm(h2, dup, ta=True, out_dtype=BF16, name="up_bwd_w")
    sent["w_up"] = _scatter_start(gw_up, 1, my, "rs_start_w_up")
    sc_f_bwd = _tok(sc_f, sent["w_down"][0][4], sent["w_up"][0][4])

    def norm2_bwd(i, j, x1v, mixv, dh2v, dx2v, gaa, gm, scf):
        r = _rms(x1v)
        xh = x1v * r
        n = xh * gm
        dn = dh2v * (1.0 + scf)
        dx1 = dx2v + _rms_bwd(xh, r, dn * gm)
        return dx1, dx1 * gaa, _colsum(dh2v), _colsum(dh2v * n), _colsum(dn * xh), _colsum(dx1 * mixv)

    dx1, dmix, d_sh_f, d_sc_f, g_gmlp, d_ga_a = _rowwise(
        norm2_bwd, [_row(x1), _row(mix), _row(dh2), _row(dx2), _bc(ga_a), _bc(g_mlp), _bc(sc_f_bwd)],
        [("row", F32), ("row", BF16), ("acc", 1), ("acc", 1), ("acc", 1), ("acc", 1)], nr=s_len // tr, tr=tr, nc=1,
        cw=d, name="norm2_bwd")

    du = _mm(dmix, w_out_f, tb=True, name="out_bwd_x")
    gw_out = _mm(u, dmix, ta=True, out_dtype=BF16, name="out_bwd_w")
    sent["w_out"] = _scatter_start(gw_out, 0, my, "rs_start_w_out")

    def merge_bwd(i, j, duv, ya, yr, gla, glr):
        ga, gr = _sigmoid(gla), _sigmoid(glr)
        return duv * ga, duv * gr, duv * ya * ga * (1.0 - ga), duv * yr * gr * (1.0 - gr)

    dya, dyr, dgla, dglr = _rowwise(
        merge_bwd, [_row(du), _row(y_attn), _row(y_rnn), _row(proj, glb, lb), _row(proj, glb + 2, lb)],
        [("row", BF16)] * 4, nr=s_len // tr, tr=tr, nc=2, cw=hw, name="merge_bwd")
    dattn = _mm(dya, w_oa_f, tb=True, out_dtype=BF16, name="o_attn_bwd_x")
    gw_oa = _mm(attn_o, dya, ta=True, out_dtype=BF16, name="o_attn_bwd_w")
    drg = _mm(dyr, w_or_f, tb=True, out_dtype=BF16, name="o_rnn_bwd_x")
    gw_or = _mm(rg, dyr, ta=True, out_dtype=BF16, name="o_rnn_bwd_w")
    sent["w_o_attn"] = _scatter_start(gw_oa, 0, my, "rs_start_w_o_attn")
    sent["w_o_rnn"] = _scatter_start(gw_or, 0, my, "rs_start_w_o_rnn")
    conv_b_bwd = _tok(conv_b, sent["w_out"][0][4], sent["w_o_attn"][0][4], sent["w_o_rnn"][0][4])

    dq, dk, dv = _attn_bwd(q, k_all, v_all, dattn, tq=tq, name="attn_bwd")
    dxr, dxg, dvec, g_wrg, g_wig = _rglru_bwd(proj, xr_blk, xg_blk, shift_masks, h_fwd, h_bwd, drg, conv_w_f, conv_b_bwd, w_rg[0], w_ig[0], b_rg_f,
                                               b_ig_f, lam_f, l_len=l_len, name="rglru_bwd")
    gate_part = _pack_flat([g_wrg, g_wig], N_DEV * SUBLANES)
    gate_sent = _scatter_start(gate_part, 0, my, "rs_start_gates")
    q_gain_bwd = _tok(q_gain, gate_sent[0][4])

    dq_p, g_qgain = _rowwise(lambda i, j, xv, dov, gain, ce, se: _qk_bwd_heads(xv, dov, gain, ce, se),
                             [_row(proj, 0, lb, aw), _row(dq, 0, 0, aw), _full(q_gain_bwd), _row(cos_e, 0, lb, HEAD_DIM),
                              _row(sin_e, 0, lb, HEAD_DIM)],
                             [("row", BF16, aw), ("acc", 1, HEAD_DIM)], nr=s_len // tr, tr=tr, nc=1, cw=aw, name="q_bwd")

    def kv_bwd(i, j, xv, dkv, dvv, gain, ce, se):
        dk_p, dgain = _qk_bwd_heads(xv, dkv, gain, ce, se)
        return dk_p, dvv, dgain

    dk_p, dv_p, g_kgain = _rowwise(kv_bwd, [_row(proj, kb, 0, kvw), _row(dk, 0, 0, kvw), _row(dv, 0, 0, kvw),
                                            _full(k_gain), _row(cos_e, 0, 0, HEAD_DIM), _row(sin_e, 0, 0, HEAD_DIM)],
                                   [("row", BF16, kvw), ("row", BF16, kvw), ("acc", 1, HEAD_DIM)], nr=t_len // tr, tr=tr,
                                   nc=1, cw=kvw, name="kv_bwd")

    pad_ctx = lambda a: jnp.pad(a, ((l_len, 0), (0, 0)))
    dproj = jnp.concatenate([pad_ctx(dq_p), dk_p, dv_p, dxr, dxg, pad_ctx(dgla), pad_ctx(dglr)], axis=1)
    dh_all = _mm(dproj, w_in_f, tb=True, name="proj_bwd_x")
    g_mix_bwd = g_mix

    def norm1_bwd(i, j, xv, dhv, dresv, g, sc):
        r = _rms(xv)
        xh = xv * r
        dn = dhv * (1.0 + sc)
        return dresv + _rms_bwd(xh, r, dn * g), _colsum(dhv), _colsum(dhv * (xh * g)), _colsum(dn * xh)

    def norm1_bwd_ctx(i, j, xv, dhv, g, sc):
        xh = xv * _rms(xv)
        return _colsum(dhv), _colsum(dhv * (xh * g)), _colsum(dhv * (1.0 + sc) * xh)

    dx_lat, d_sh_a, d_sc_a, g_gmix_lat = _rowwise(
        norm1_bwd, [_row(x_all, 0, lb), _row(dh_all, 0, lb), _row(dx1), _bc(g_mix_bwd), _bc(sc_a)],
        [("row", F32)] + [("acc", 1)] * 3, nr=s_len // tr, tr=tr, nc=1, cw=d, name="norm1_bwd")
    d_csh, d_csc, g_gmix_ctx = _rowwise(
        norm1_bwd_ctx, [_row(x_all), _row(dh_all), _bc(g_mix_bwd), _bc(csc_a)],
        [("acc", 1)] * 3, nr=lb, tr=tr, nc=1, cw=d, name="norm1_bwd_ctx")
    g_gmix = g_gmix_lat + g_gmix_ctx
    grad_x = dx_lat[None]

    dmod_lat = jnp.concatenate([d_sh_a, d_sc_a, d_ga_a, d_sh_f, d_sc_f, d_ga_f], axis=1)
    dmod_ctx = jnp.concatenate([d_csh, d_csc, jnp.zeros((1, (N_MOD - 2) * d), F32)], axis=1)
    dm_g = _all_gather([jnp.concatenate([dmod_lat, dmod_ctx, jnp.zeros((SUBLANES - 2, N_MOD * d), F32)], axis=0)], [0],
                       "ag_dmod")[0]
    dm16 = jnp.concatenate([dm_g[0::SUBLANES], dm_g[1::SUBLANES]], axis=0)
    dm_sh = lax.dynamic_slice(dm16, (0, my * nmod), (2 * N_DEV, nmod))
    gw_mod = _mm(c16, dm_sh, ta=True, a_fn=_silu, name="mod_bwd_w")
    dc16 = _mm(dm_sh, w_mod[0], tb=True, name="mod_bwd_x")

    def cctx_fn(i, j, pv, cc, dl, dc):
        row = lax.broadcasted_iota(jnp.int32, pv.shape, 0)
        return _colsum(jnp.where(row >= N_DEV, pv, 0.0)) * _dsilu(cc), dl + dc

    g_cctx, g_bmod = _rowwise(cctx_fn, [_full(dc16), _full(c_ctx2), _full(dmod_lat), _full(dmod_ctx)],
                              [("row", F32, d), ("row", F32, N_MOD * d)], nr=1, tr=1, nc=1, cw=d, name="cctx_bwd")

    rep_part = dict(c_ctx=g_cctx, b_mod=g_bmod, g_mix=g_gmix, g_mlp=g_gmlp, q_gain=g_qgain, k_gain=g_kgain,
                    conv_b=dvec[4:5], g_final=g_gfinal)
    rep_small = [n for n in REPLICATED if n in rep_part]
    gates = [n for n in REPLICATED if n not in rep_part]
    part_pack = _pack_flat([rep_part[n] for n in rep_small], PACK_ROWS)
    rep_rows = part_pack.shape[0]
    gate_land = _xfer_wait("scatter", gate_sent[0], 0, g_bmod, "rs_wait_gates")
    gate_sum = _sum_parts(gate_sent[1], gate_land, name="gate_sum")
    (gate_full, parts), order = _all_gather([gate_sum, part_pack], [0, 0], "ag_rep_grads", with_token=True)
    small_g = jnp.concatenate([dvec[0:4], dvec[5:11], jnp.zeros((SMALL_ROWS - 10, d), F32)], axis=0)
    (small_recv,), order = _all_to_all([_tok(small_g, order)], [1], "a2a_small_grads")

    halves = {}
    for part, lo in (("top", 0), ("bot", d // 2)):
        gw_part = _mm(h_all[:, lo:lo + d // 2], dproj, ta=True, out_dtype=BF16, name="proj_bwd_w_" + part,
                      bias=_tok(jnp.zeros((1, n_in), F32), order))
        halves[part] = _chip_scatter_start(_pair_reduce(gw_part, my, "pair_w_in_" + part), my, "rs_start_w_in_" + part)
        order = halves[part][0][4]
    started = order

    res = {}
    for names, pr, mult, nm in ((rep_small, parts.reshape(N_DEV, rep_rows, LANES), PACK_ROWS, "adamw_rep"),
                                (gates, gate_full[None], N_DEV * SUBLANES, "adamw_gates")):
        rep_out = _adamw(pr, _tok(_pack_flat([wd[n] for n in names], mult), started),
                         _pack_flat([md[n] for n in names], mult), _pack_flat([vd[n] for n in names], mult), name=nm)
        for kind, packed in zip(("g", "d", "m", "v"), rep_out):
            for n, a in zip(names, _unpack_flat(packed, [wd[n].shape for n in names])):
                res[kind, n] = a
    out = _adamw(small_recv, _tok(sp_w, started), _pack_small(md), _pack_small(vd), name="adamw_small")
    for kind, a in zip(("g", "d", "m", "v"), out):
        for n, piece in _unpack_small(a).items():
            res[kind, n] = piece

    after = started
    for n, (handle, own, ax, kind) in sent.items():
        land = _xfer_wait(kind, handle, ax, after, "rs_wait_" + n)
        out = _adamw(land, wd[n][0], md[n][0], vd[n][0], own=own, name="adamw_" + n)
        after = out[0]
        for kind, a in zip(("g", "d", "m", "v"), out):
            res[kind, n] = a[None]
    out = _adamw(gw_mod[None], w_mod[0], m_w_mod[0], v_w_mod[0], name="adamw_w_mod")
    after = out[0]
    for kind, a in zip(("g", "d", "m", "v"), out):
        res[kind, 'w_mod'] = a[None]
    lands, owns = [], []
    for part, (handle, own, ax, kind) in halves.items():
        lands.append(_xfer_wait(kind, handle, ax, after, "rs_wait_w_in_" + part))
        owns.append(own)
        after = lands[-1]
    out = _adamw(jnp.concatenate(lands, axis=1), w_in[0], m_w_in[0], v_w_in[0], own=jnp.concatenate(owns, axis=0),
                 name="adamw_w_in")
    for kind, a in zip(("g", "d", "m", "v"), out):
        res[kind, 'w_in'] = a[None]

    outs = [loss, grad_x]
    for kind in ("g", "d", "m", "v"):
        outs += [res[kind, n].reshape(wd[n].shape) for n in WEIGHTS]
    return tuple(outs)
```

```python
import functools
import math

import jax
import jax.numpy as jnp
from jax import lax
from jax.experimental import pallas as pl
from jax.experimental.pallas import tpu as pltpu

F32 = jnp.float32
BF16 = jnp.bfloat16
MESH = pl.DeviceIdType.MESH

N_DEV = 8
HEAD_DIM = 128
Q_GROUP = 4
RNN_BLOCK = 128
GRID_W = 64
CONV_WIDTH = 4
LRU_C = 8.0
ROPE_THETA = 10000.0
NORM_EPS = 1e-6
N_MOD = 6
ADAM_LR = 0.001
ADAM_B1 = 0.9
ADAM_B2 = 0.999
ADAM_EPS = 1e-08
ADAM_WD = 0.01
ADAM_STEP = 10

LANES = 128
SUBLANES = 8
VMEM_LIMIT = 56 * 1024 * 1024
HBM_SPEC = pl.BlockSpec(memory_space=pltpu.HBM)


def _params(sem=None, vmem=VMEM_LIMIT):
    kw = dict(vmem_limit_bytes=vmem)
    if sem is not None:
        kw["dimension_semantics"] = sem
    return pltpu.CompilerParams(**kw)


def _pick(dim, target, mult):
    for t in range(min(dim, target), 0, -1):
        if dim % t == 0 and t % mult == 0:
            return t
    return dim


def _my_pos():
    return lax.axis_index("x"), lax.axis_index("y"), lax.axis_index("c")


def _flip(pos, d):
    x, y, c = pos
    return (1 - x if d & 4 else x, 1 - y if d & 2 else y, 1 - c if d & 1 else c)


def _lin(pos):
    return 4 * pos[0] + 2 * pos[1] + pos[2]


def _slab(ref, axis, idx, size):
    if axis == 0:
        return ref.at[pl.ds(idx * size, size)]
    assert axis == 1
    return ref.at[:, pl.ds(idx * size, size)]


def _all_gather(arrs, axes, name, with_token=False):
    n = len(arrs)
    sizes = [a.shape[ax] for a, ax in zip(arrs, axes)]

    def body(*refs):
        ins, outs, token = refs[:n], refs[n:2 * n], refs[2 * n]
        send_sems, recv_sems, local_sems = refs[2 * n + 1:]
        token[...] = jnp.zeros_like(token)
        me = _my_pos()
        sibling = _flip(me, 1)
        chips = [2, 4, 6]

        def copy(a, k, block_pos, to, src=None):
            dst = _slab(outs[a], axes[a], _lin(block_pos), sizes[a])
            return pltpu.make_async_remote_copy(
                src_ref=dst if src is None else src, dst_ref=dst,
                send_sem=send_sems.at[7 * a + k], recv_sem=recv_sems.at[7 * a + k],
                device_id=to, device_id_type=MESH)

        local, first, passed = [], [], []
        for a in range(n):
            cp = pltpu.make_async_copy(ins[a], _slab(outs[a], axes[a], _lin(me), sizes[a]), local_sems.at[a])
            cp.start()
            local.append(cp)
        for a in range(n):
            f = [copy(a, 0, me, sibling, src=ins[a])]
            f += [copy(a, 1 + j, me, _flip(me, m), src=ins[a]) for j, m in enumerate(chips)]
            for cp in f:
                cp.start()
            first += f
        for a in range(n):
            for j, m in enumerate(chips):
                copy(a, 1 + j, _flip(me, m), me).wait_recv()
                cp = copy(a, 4 + j, _flip(me, m), sibling)
                cp.start()
                passed.append(cp)
        for a in range(n):
            copy(a, 0, sibling, me).wait_recv()
            for j, m in enumerate(chips):
                copy(a, 4 + j, _flip(sibling, m), me).wait_recv()
        for cp in first + passed:
            cp.wait_send()
        for cp in local:
            cp.wait()

    out_shape = []
    for a, ax in zip(arrs, axes):
        shp = list(a.shape)
        shp[ax] *= N_DEV
        out_shape.append(jax.ShapeDtypeStruct(tuple(shp), a.dtype))
    out_shape.append(jax.ShapeDtypeStruct((SUBLANES, LANES), F32))
    res = pl.pallas_call(
        body, name=name, out_shape=tuple(out_shape),
        in_specs=[HBM_SPEC] * n, out_specs=tuple([HBM_SPEC] * n + [pl.BlockSpec(memory_space=pltpu.VMEM)]),
        scratch_shapes=[pltpu.SemaphoreType.DMA((7 * n,)), pltpu.SemaphoreType.DMA((7 * n,)),
                        pltpu.SemaphoreType.DMA((n,))],
    )(*arrs)
    return (list(res[:n]), res[n]) if with_token else list(res[:n])


def _gather_proj(h_all, w_shard, name):
    t_len, kdim = h_all.shape
    sz = w_shard.shape[1]
    tm = _pick(t_len, 1024, 16)
    n_m = t_len // tm

    def body(h_ref, w_ref, wfull_ref, proj_ref, token, a_buf, b_buf, o_buf, send_sems, recv_sems, io_sems, out_sems):
        token[...] = jnp.zeros_like(token)
        me = _my_pos()
        sibling = _flip(me, 1)
        chips = [2, 4, 6]
        slab = lambda pos: _slab(wfull_ref, 1, _lin(pos), sz)

        def copy(k, block_pos, to, src=None):
            return pltpu.make_async_remote_copy(
                src_ref=slab(block_pos) if src is None else src, dst_ref=slab(block_pos),
                send_sem=send_sems.at[k], recv_sem=recv_sems.at[k], device_id=to, device_id_type=MESH)

        pending = []

        def block_product(pos, b_src):
            load = pltpu.make_async_copy(b_src, b_buf, io_sems.at[1])
            load.start()
            load.wait()
            for i in range(n_m):
                slot = len(pending) % 2
                if len(pending) >= 2:
                    pending[-2].wait()
                o_buf[slot] = _dot(a_buf[pl.ds(i * tm, tm), :], b_buf[...], 1, 0)
                out = pltpu.make_async_copy(
                    o_buf.at[slot], proj_ref.at[pl.ds(i * tm, tm), pl.ds(_lin(pos) * sz, sz)], out_sems.at[slot])
                out.start()
                pending.append(out)

        local = pltpu.make_async_copy(w_ref, slab(me), io_sems.at[0])
        local.start()
        first = [copy(0, me, sibling, src=w_ref)] + [copy(1 + j, me, _flip(me, m), src=w_ref) for j, m in enumerate(chips)]
        for cp in first:
            cp.start()
        load_a = pltpu.make_async_copy(h_ref, a_buf, io_sems.at[2])
        load_a.start()
        load_a.wait()
        block_product(me, w_ref)
        copy(0, sibling, me).wait_recv()
        block_product(sibling, slab(sibling))
        passed = []
        for j, m in enumerate(chips):
            src_pos = _flip(me, m)
            copy(1 + j, src_pos, me).wait_recv()
            cp = copy(4 + j, src_pos, sibling)
            cp.start()
            passed.append(cp)
            block_product(src_pos, slab(src_pos))
        for j, m in enumerate(chips):
            src_pos = _flip(sibling, m)
            copy(4 + j, src_pos, me).wait_recv()
            block_product(src_pos, slab(src_pos))
        for out in pending[-2:]:
            out.wait()
        for cp in first + passed:
            cp.wait_send()
        local.wait()

    return pl.pallas_call(
        body, name=name,
        out_shape=(jax.ShapeDtypeStruct((kdim, N_DEV * sz), w_shard.dtype), jax.ShapeDtypeStruct((t_len, N_DEV * sz), F32),
                   jax.ShapeDtypeStruct((SUBLANES, LANES), F32)),
        in_specs=[HBM_SPEC, HBM_SPEC], out_specs=(HBM_SPEC, HBM_SPEC, pl.BlockSpec(memory_space=pltpu.VMEM)),
        scratch_shapes=[pltpu.VMEM((t_len, kdim), h_all.dtype), pltpu.VMEM((kdim, sz), w_shard.dtype),
                        pltpu.VMEM((2, tm, sz), F32), pltpu.SemaphoreType.DMA((7,)), pltpu.SemaphoreType.DMA((7,)),
                        pltpu.SemaphoreType.DMA((3,)), pltpu.SemaphoreType.DMA((2,))],
        compiler_params=_params(),
    )(h_all, w_shard)


def _all_to_all(arrs, axes, name):
    n = len(arrs)
    sizes = [a.shape[ax] // N_DEV for a, ax in zip(arrs, axes)]

    def body(*refs):
        ins, outs, token = refs[:n], refs[n:2 * n], refs[2 * n]
        send_sems, recv_sems, local_sems = refs[2 * n + 1:]
        token[...] = jnp.zeros_like(token)
        me = _my_pos()
        my = _lin(me)
        copies, local = [], []
        for a in range(n):
            cp = pltpu.make_async_copy(_slab(ins[a], axes[a], my, sizes[a]), outs[a].at[my], local_sems.at[a])
            cp.start()
            local.append(cp)
        for d in range(1, N_DEV):
            peer = _flip(me, d)
            for a in range(n):
                cp = pltpu.make_async_remote_copy(
                    src_ref=_slab(ins[a], axes[a], _lin(peer), sizes[a]), dst_ref=outs[a].at[my],
                    send_sem=send_sems.at[7 * a + d - 1], recv_sem=recv_sems.at[7 * a + d - 1],
                    device_id=peer, device_id_type=MESH)
                cp.start()
                copies.append((a, d, cp))
        for a, d, cp in copies:
            peer = _flip(me, d)
            pltpu.make_async_remote_copy(
                src_ref=_slab(ins[a], axes[a], _lin(peer), sizes[a]), dst_ref=outs[a].at[_lin(peer)],
                send_sem=send_sems.at[7 * a + d - 1], recv_sem=recv_sems.at[7 * a + d - 1],
                device_id=peer, device_id_type=MESH).wait_recv()
            cp.wait_send()
        for cp in local:
            cp.wait()

    out_shape = []
    for a, ax, s in zip(arrs, axes, sizes):
        shp = list(a.shape)
        shp[ax] = s
        out_shape.append(jax.ShapeDtypeStruct((N_DEV,) + tuple(shp), a.dtype))
    out_shape.append(jax.ShapeDtypeStruct((SUBLANES, LANES), F32))
    res = pl.pallas_call(
        body, name=name, out_shape=tuple(out_shape),
        in_specs=[HBM_SPEC] * n, out_specs=tuple([HBM_SPEC] * n + [pl.BlockSpec(memory_space=pltpu.VMEM)]),
        scratch_shapes=[pltpu.SemaphoreType.DMA((7 * n,)), pltpu.SemaphoreType.DMA((7 * n,)),
                        pltpu.SemaphoreType.DMA((n,))],
    )(*arrs)
    return list(res[:n]), res[n]


SEM_SPEC = pl.BlockSpec(memory_space=pltpu.SEMAPHORE)
_EFFECT = pltpu.SideEffectType.DATAFLOW_SIDE_EFFECTING
N_PEERS = N_DEV - 1


def _split_copies(kind, src_ref, land_ref, send_sems, recv_sems, axis, size, arriving):
    me = _my_pos()
    out = []
    for dd in (range(2, N_DEV, 2) if kind in ("chips", "gather_chips") else range(1, N_DEV)):
        peer = _flip(me, dd)
        k = dd - 1
        if kind in ("gather", "gather_chips"):
            src = _slab(land_ref, axis, _lin(me), size)
            dst = _slab(land_ref, axis, _lin(peer if arriving else me), size)
        elif kind == "scatter":
            src, dst = _slab(src_ref, axis, _lin(peer), size), land_ref.at[k]
        else:
            k = dd // 2 - 1
            src, dst = src_ref.at[2 * peer[0] + peer[1]], land_ref.at[k]
        out.append(pltpu.make_async_remote_copy(
            src_ref=src, dst_ref=dst, send_sem=send_sems.at[k], recv_sem=recv_sems.at[k],
            device_id=peer, device_id_type=MESH))
    return out


N_CHIPS = N_DEV // 2


def _split_size(kind, src_shape, land_shape, axis):
    if kind in ("gather", "gather_chips"):
        return land_shape[axis] // N_DEV
    return src_shape[axis] // {"scatter": N_DEV, "chips": N_CHIPS}[kind]


def _sibling_swap(full, axis, name):
    size = full.shape[axis] // N_DEV

    def body(in_ref, o_ref, send_sems, recv_sems):
        me = _my_pos()
        sibling = _flip(me, 1)
        cps = []
        for k in range(N_CHIPS):
            mine = _slab(o_ref, axis, 2 * k + me[2], size)
            cp = pltpu.make_async_remote_copy(src_ref=mine, dst_ref=mine, send_sem=send_sems.at[k],
                                              recv_sem=recv_sems.at[k], device_id=sibling, device_id_type=MESH)
            cp.start()
            cps.append(cp)
        for k, cp in enumerate(cps):
            cp.wait_send()
            theirs = _slab(o_ref, axis, 2 * k + sibling[2], size)
            pltpu.make_async_remote_copy(src_ref=theirs, dst_ref=theirs, send_sem=send_sems.at[k],
                                         recv_sem=recv_sems.at[k], device_id=sibling, device_id_type=MESH).wait_recv()

    return pl.pallas_call(
        body, name=name, out_shape=jax.ShapeDtypeStruct(full.shape, full.dtype), in_specs=[HBM_SPEC], out_specs=HBM_SPEC,
        scratch_shapes=[pltpu.SemaphoreType.DMA((N_CHIPS,)), pltpu.SemaphoreType.DMA((N_CHIPS,))],
        input_output_aliases={0: 0},
    )(full)


def _xfer_start(kind, src, land, axis, name):
    size = _split_size(kind, src.shape, land.shape, axis)

    def body(src_ref, land_ref, send_sems, recv_sems, src_thru, land_thru, token):
        for cp in _split_copies(kind, src_ref, land_ref, send_sems, recv_sems, axis, size, False):
            cp.start()
        token[...] = jnp.zeros_like(token)

    return pl.pallas_call(
        body, name=name,
        out_shape=(pltpu.SemaphoreType.DMA((N_PEERS,)), pltpu.SemaphoreType.DMA((N_PEERS,)),
                   pltpu.HBM(src.shape, src.dtype), pltpu.HBM(land.shape, land.dtype),
                   jax.ShapeDtypeStruct((SUBLANES, LANES), F32)),
        in_specs=(HBM_SPEC, HBM_SPEC),
        out_specs=(SEM_SPEC, SEM_SPEC, HBM_SPEC, HBM_SPEC, pl.BlockSpec(memory_space=pltpu.VMEM)),
        input_output_aliases={0: 2, 1: 3},
        compiler_params=pltpu.CompilerParams(has_side_effects=_EFFECT),
    )(pltpu.with_memory_space_constraint(src, pltpu.HBM), pltpu.with_memory_space_constraint(land, pltpu.HBM))


def _xfer_wait(kind, handle, axis, after, name):
    send_sems, recv_sems, src_thru, land_thru, _ = handle
    size = _split_size(kind, src_thru.shape, land_thru.shape, axis)

    def body(src_ref, land_ref, send_sems, recv_sems, after_ref, src_dead, got_ref):
        for cp in _split_copies(kind, src_ref, land_ref, send_sems, recv_sems, axis, size, True):
            cp.wait_send()
            cp.wait_recv()

    return pl.pallas_call(
        body, name=name,
        out_shape=(pltpu.HBM(src_thru.shape, src_thru.dtype), pltpu.HBM(land_thru.shape, land_thru.dtype)),
        in_specs=(HBM_SPEC, HBM_SPEC, SEM_SPEC, SEM_SPEC, pl.BlockSpec(memory_space=pl.ANY)),
        out_specs=(HBM_SPEC, HBM_SPEC), input_output_aliases={0: 0, 1: 1},
        compiler_params=pltpu.CompilerParams(has_side_effects=_EFFECT),
    )(src_thru, land_thru, send_sems, recv_sems, after)[1]


def _place_block(shard, axis, my, name):
    r, c = shard.shape
    tr = _pick(r, max(16, (1 << 19) // c), 16)
    nr = r // tr
    if axis == 0:
        full, out_map = (r * N_DEV, c), (lambda i, my_ref: (my_ref[0] * nr + i, 0))
    else:
        full, out_map = (r, c * N_DEV), (lambda i, my_ref: (i, my_ref[0]))

    def body(my_ref, x_ref, o_ref):
        o_ref[...] = x_ref[...].astype(BF16)

    return pl.pallas_call(
        body, name=name, out_shape=jax.ShapeDtypeStruct(full, BF16),
        grid_spec=pltpu.PrefetchScalarGridSpec(
            num_scalar_prefetch=1, grid=(nr,), in_specs=[pl.BlockSpec((tr, c), lambda i, my_ref: (i, 0))],
            out_specs=pl.BlockSpec((tr, c), out_map)),
        compiler_params=_params(("arbitrary",)),
    )(jnp.reshape(my, (1,)).astype(jnp.int32), shard)


def _scatter_start(grad, axis, my, name):
    shp = list(grad.shape)
    shp[axis] //= N_DEV
    own = lax.dynamic_slice_in_dim(grad, my * shp[axis], shp[axis], axis)
    return (_xfer_start("scatter", grad, lax.empty((N_PEERS,) + tuple(shp), grad.dtype), axis, name), own, axis,
            "scatter")


def _pair_reduce(grad, my, name):
    r, cols = grad.shape
    sz = cols // N_DEV

    def exchange(g_ref, got_ref, send_sems, recv_sems):
        me = _my_pos()
        sibling = _flip(me, 1)
        cps = []
        for k in range(N_CHIPS):
            cp = pltpu.make_async_remote_copy(
                src_ref=_slab(g_ref, 1, 2 * k + sibling[2], sz), dst_ref=_slab(got_ref, 1, k, sz),
                send_sem=send_sems.at[k], recv_sem=recv_sems.at[k], device_id=sibling, device_id_type=MESH)
            cp.start()
            cps.append(cp)
        for cp in cps:
            cp.wait()

    got = pl.pallas_call(
        exchange, name=name + "_swap", out_shape=jax.ShapeDtypeStruct((r, N_CHIPS * sz), grad.dtype),
        in_specs=[HBM_SPEC], out_specs=HBM_SPEC,
        scratch_shapes=[pltpu.SemaphoreType.DMA((N_CHIPS,)), pltpu.SemaphoreType.DMA((N_CHIPS,))],
    )(grad)

    tr = _pick(r, 512, 16)

    def add(c_ref, g_ref, p_ref, o_ref):
        o_ref[...] = (g_ref[...].astype(F32) + p_ref[...].astype(F32)).astype(o_ref.dtype)

    def add(c_ref, g_ref, p_ref, o_ref):
        o_ref[0] = (g_ref[...].astype(F32) + p_ref[...].astype(F32)).astype(o_ref.dtype)

    return pl.pallas_call(
        add, name=name + "_sum", out_shape=jax.ShapeDtypeStruct((N_CHIPS, r, sz), grad.dtype),
        grid_spec=pltpu.PrefetchScalarGridSpec(
            num_scalar_prefetch=1, grid=(N_CHIPS, r // tr),
            in_specs=[pl.BlockSpec((tr, sz), lambda k, i, c_ref: (i, 2 * k + c_ref[0])),
                      pl.BlockSpec((tr, sz), lambda k, i, c_ref: (i, k))],
            out_specs=pl.BlockSpec((1, tr, sz), lambda k, i, c_ref: (k, i, 0))),
        compiler_params=_params(("parallel", "parallel")),
    )(jnp.reshape(my % 2, (1,)).astype(jnp.int32), grad, got)


def _chip_scatter_start(pair_sum, my, name):
    _, r, sz = pair_sum.shape
    own = lax.dynamic_index_in_dim(pair_sum, my // 2, 0, keepdims=False)
    return (_xfer_start("chips", pair_sum, lax.empty((N_CHIPS - 1, r, sz), pair_sum.dtype), 0, name), own, 0, "chips")


def _tok(a, *tokens):
    for t in tokens:
        a = a + t[0, 0].astype(a.dtype)
    return a


def _mm(a, b, *, name, ta=False, tb=False, out_dtype=F32, a_fn=None, out_fn=None, extras=(), bias=None,
        tm=1024, tn=1024, tk=2816):
    if ta:
        kdim, m = a.shape
    else:
        m, kdim = a.shape
    if tb:
        n, kb = b.shape
    else:
        kb, n = b.shape
    assert kb == kdim, (a.shape, b.shape, ta, tb)
    tm = _pick(m, tm, LANES if ta else SUBLANES)
    tn = _pick(n, tn, LANES)
    tk = _pick(kdim, tk, LANES)
    nk = kdim // tk
    ne = len(extras)
    has_bias = bias is not None

    def body(*refs):
        a_ref, b_ref = refs[0], refs[1]
        pos = 2
        bias_ref = None
        if has_bias:
            bias_ref = refs[pos]
            pos += 1
        extra_refs = refs[pos:pos + ne]
        o_ref = refs[pos + ne]
        av = a_ref[...]
        if a_fn is not None:
            av = a_fn(av)
        dn = (((0 if ta else 1,), (1 if tb else 0,)), ((), ()))
        part = lax.dot_general(av.astype(BF16), b_ref[...].astype(BF16), dn, preferred_element_type=F32)

        def finish(r):
            if has_bias:
                r = r + bias_ref[...]
            if out_fn is not None:
                r = out_fn(r, *[e[...] for e in extra_refs])
            o_ref[...] = r.astype(out_dtype)

        if nk == 1:
            finish(part)
            return
        acc = refs[pos + ne + 1]
        k = pl.program_id(2)

        @pl.when(k == 0)
        def _():
            acc[...] = part

        @pl.when(k > 0)
        def _():
            acc[...] += part

        @pl.when(k == nk - 1)
        def _():
            finish(acc[...])

    in_specs = [
        pl.BlockSpec((tk, tm), lambda i, j, k: (k, i)) if ta else pl.BlockSpec((tm, tk), lambda i, j, k: (i, k)),
        pl.BlockSpec((tn, tk), lambda i, j, k: (j, k)) if tb else pl.BlockSpec((tk, tn), lambda i, j, k: (k, j)),
    ]
    args = [a, b]
    if has_bias:
        in_specs.append(pl.BlockSpec((1, tn), lambda i, j, k: (0, j)))
        args.append(bias)
    for e in extras:
        in_specs.append(pl.BlockSpec((tm, tn), lambda i, j, k: (i, j)))
        args.append(e)
    return pl.pallas_call(
        body, name=name, grid=(m // tm, n // tn, nk),
        in_specs=in_specs, out_specs=pl.BlockSpec((tm, tn), lambda i, j, k: (i, j)),
        out_shape=jax.ShapeDtypeStruct((m, n), out_dtype),
        scratch_shapes=[pltpu.VMEM((tm, tn), F32)] if nk > 1 else [],
        compiler_params=_params(("parallel", "parallel", "arbitrary")),
    )(*args)


def _row(arr, cb=0, rb=0, w=None):
    return ("row", arr, cb, rb, w)


def _bc(arr, cb=0, w=None):
    return ("bc", arr, cb, 0, w)


def _full(arr):
    return ("full", arr, 0, 0, None)


def _rowwise(fn, ins, outs, *, nr, tr, nc, cw, name):
    n_in = len(ins)
    outs = [(o[0], o[1], (o[2] if len(o) > 2 and o[2] else cw)) for o in outs]

    def body(*refs):
        in_refs, out_refs = refs[:n_in], refs[n_in:]
        j, i = pl.program_id(0), pl.program_id(1)
        res = fn(i, j, *[r[...] for r in in_refs])
        if not isinstance(res, (tuple, list)):
            res = (res,)
        for (kind, _, _), o_ref, r in zip(outs, out_refs, res):
            if kind == "row":
                o_ref[...] = r.astype(o_ref.dtype)
            else:
                @pl.when(i == 0)
                def _(o_ref=o_ref):
                    o_ref[...] = jnp.zeros_like(o_ref)

                o_ref[...] += r

    in_specs, args = [], []
    for kind, arr, cb, rb, w in ins:
        w = w or cw
        if kind == "row":
            in_specs.append(pl.BlockSpec((tr, w), lambda j, i, cb=cb, rb=rb: (i + rb, j + cb)))
        elif kind == "bc":
            in_specs.append(pl.BlockSpec((arr.shape[0], w), lambda j, i, cb=cb: (0, j + cb)))
        else:
            in_specs.append(pl.BlockSpec(arr.shape, lambda j, i, nd=arr.ndim: (0,) * nd))
        args.append(arr)
    out_specs, out_shape = [], []
    for kind, spec, w in outs:
        if kind == "row":
            out_specs.append(pl.BlockSpec((tr, w), lambda j, i: (i, j)))
            out_shape.append(jax.ShapeDtypeStruct((nr * tr, nc * w), spec))
        else:
            out_specs.append(pl.BlockSpec((spec, w), lambda j, i: (0, j)))
            out_shape.append(jax.ShapeDtypeStruct((spec, nc * w), F32))
    res = pl.pallas_call(
        body, name=name, grid=(nc, nr), in_specs=in_specs, out_specs=tuple(out_specs), out_shape=tuple(out_shape),
        compiler_params=_params(("parallel", "arbitrary")),
    )(*args)
    return res


def _sum_parts(own, parts, *, name):
    p, r, c = parts.shape
    tr = _pick(r, max(SUBLANES, (1 << 18) // c), SUBLANES)

    def body(o_ref, p_ref, g_out):
        g = o_ref[...].astype(F32)
        for q in range(p):
            g = g + p_ref[q].astype(F32)
        g_out[...] = g

    blk = pl.BlockSpec((tr, c), lambda i: (i, 0))
    return pl.pallas_call(
        body, name=name, grid=(r // tr,), in_specs=[blk, pl.BlockSpec((p, tr, c), lambda i: (0, i, 0))],
        out_specs=blk, out_shape=jax.ShapeDtypeStruct((r, c), F32), compiler_params=_params(("parallel",)),
    )(own, parts)


def _adamw(parts, w, m, v, *, name, own=None):
    p, r, c = parts.shape
    tr = _pick(r, max(SUBLANES, (1 << 18) // c), 16 if parts.dtype == BF16 else SUBLANES)
    bc1 = 1.0 - ADAM_B1 ** ADAM_STEP
    bc2 = 1.0 - ADAM_B2 ** ADAM_STEP
    has_own = own is not None

    def body(*refs):
        p_ref, w_ref, m_ref, v_ref = refs[:4]
        g_out, d_out, m_out, v_out = refs[4 + has_own:]
        if has_own:
            g = refs[4][...].astype(F32) + p_ref[0].astype(F32)
        else:
            g = p_ref[0].astype(F32)
        for q in range(1, p):
            g = g + p_ref[q].astype(F32)
        m_new = ADAM_B1 * m_ref[...] + (1.0 - ADAM_B1) * g
        v_new = ADAM_B2 * v_ref[...] + (1.0 - ADAM_B2) * (g * g)
        m_hat = m_new / bc1
        v_hat = v_new / bc2
        g_out[...] = g
        d_out[...] = -ADAM_LR * (m_hat / (jnp.sqrt(v_hat) + ADAM_EPS) + ADAM_WD * w_ref[...])
        m_out[...] = m_new
        v_out[...] = v_new

    blk = pl.BlockSpec((tr, c), lambda i: (i, 0))
    return pl.pallas_call(
        body, name=name, grid=(r // tr,),
        in_specs=[pl.BlockSpec((p, tr, c), lambda i: (0, i, 0)), blk, blk, blk] + ([blk] if has_own else []),
        out_specs=(blk, blk, blk, blk), out_shape=tuple(jax.ShapeDtypeStruct((r, c), F32) for _ in range(4)),
        compiler_params=_params(("parallel",)),
    )(*([parts, w, m, v] + ([own] if has_own else [])))


_GELU_C = math.sqrt(2.0 / math.pi)
_GELU_K = 0.044715


def _sigmoid(z):
    return 0.5 * jnp.tanh(0.5 * z) + 0.5


def _silu(z):
    return z * _sigmoid(z)


def _dsilu(z):
    s = _sigmoid(z)
    return s * (1.0 + z * (1.0 - s))


def _gelu(z):
    return 0.5 * z * (1.0 + jnp.tanh(_GELU_C * (z + _GELU_K * z * z * z)))


def _gelu_pair(z):
    zz = z * z
    t = jnp.tanh(_GELU_C * (z * (1.0 + _GELU_K * zz)))
    half = 0.5 * (1.0 + t)
    return z * half, half + 0.5 * z * (1.0 - t * t) * (_GELU_C * (1.0 + 3.0 * _GELU_K * zz))


def _softplus(z):
    return jnp.maximum(z, 0.0) + jnp.log(1.0 + jnp.exp(-jnp.abs(z)))


def _rms(xv):
    return lax.rsqrt(jnp.mean(xv * xv, axis=-1, keepdims=True) + NORM_EPS)


def _rms_bwd(xhat, r, dxhat):
    return r * (dxhat - xhat * jnp.mean(dxhat * xhat, axis=-1, keepdims=True))


def _swap_pairs(y):
    lane = lax.broadcasted_iota(jnp.int32, y.shape, 1)
    return jnp.where(lane % 2 == 0, pltpu.roll(y, LANES - 1, 1), pltpu.roll(y, 1, 1))


def _dot(a, b, ca, cb):
    return lax.dot_general(a, b, (((ca,), (cb,)), ((), ())), preferred_element_type=F32)


_SM_SCALE = HEAD_DIM ** -0.5
_EXP2_SCALE = _SM_SCALE * math.log2(math.e)


def _softmax_parts(qh, kk):
    s = _dot(qh, kk, 1, 1)
    p = jnp.exp2((s - jnp.max(s, axis=-1, keepdims=True)) * _EXP2_SCALE)
    return p, 1.0 / jnp.sum(p, axis=-1, keepdims=True)


def _attn_fwd(q, k, v, *, tq, name):
    s_len, aw = q.shape
    t_len, kvw = k.shape
    gw = Q_GROUP * HEAD_DIM

    def body(q_ref, k_ref, v_ref, o_ref):
        kk, vv = k_ref[...], v_ref[...]
        for g in range(Q_GROUP):
            cols = slice(g * HEAD_DIM, (g + 1) * HEAD_DIM)
            p, inv_l = _softmax_parts(q_ref[:, cols], kk)
            o_ref[:, cols] = (_dot(p.astype(BF16), vv, 1, 0) * inv_l).astype(BF16)

    return pl.pallas_call(
        body, name=name, grid=(kvw // HEAD_DIM, s_len // tq),
        in_specs=[pl.BlockSpec((tq, gw), lambda h, i: (i, h)), pl.BlockSpec((t_len, HEAD_DIM), lambda h, i: (0, h)),
                  pl.BlockSpec((t_len, HEAD_DIM), lambda h, i: (0, h))],
        out_specs=pl.BlockSpec((tq, gw), lambda h, i: (i, h)),
        out_shape=jax.ShapeDtypeStruct((s_len, aw), BF16),
        compiler_params=_params(("parallel", "parallel")),
    )(q, k, v)


def _attn_bwd(q, k, v, do, *, tq, name):
    s_len, aw = q.shape
    t_len, kvw = k.shape
    gw = Q_GROUP * HEAD_DIM

    def body(q_ref, k_ref, v_ref, do_ref, dq_ref, dk_ref, dv_ref):
        @pl.when(pl.program_id(1) == 0)
        def _():
            dk_ref[...] = jnp.zeros_like(dk_ref)
            dv_ref[...] = jnp.zeros_like(dv_ref)

        kk, vv = k_ref[...], v_ref[...]
        for g in range(Q_GROUP):
            cols = slice(g * HEAD_DIM, (g + 1) * HEAD_DIM)
            qh, doh = q_ref[:, cols], do_ref[:, cols]
            p, inv_l = _softmax_parts(qh, kk)
            dp = _dot(doh, vv, 1, 1)
            delta = jnp.sum(p * dp, axis=-1, keepdims=True) * inv_l
            dsb = (p * ((dp - delta) * (inv_l * _SM_SCALE))).astype(BF16)
            dq_ref[:, cols] = _dot(dsb, kk, 1, 0)
            dk_ref[...] += _dot(dsb, qh, 0, 0)
            dv_ref[...] += _dot(p.astype(BF16), (doh * inv_l).astype(BF16), 0, 0)

    qspec = pl.BlockSpec((tq, gw), lambda h, i: (i, h))
    kspec = pl.BlockSpec((t_len, HEAD_DIM), lambda h, i: (0, h))
    return pl.pallas_call(
        body, name=name, grid=(kvw // HEAD_DIM, s_len // tq),
        in_specs=[qspec, kspec, kspec, qspec], out_specs=(qspec, kspec, kspec),
        out_shape=(jax.ShapeDtypeStruct((s_len, aw), F32), jax.ShapeDtypeStruct((t_len, kvw), F32),
                   jax.ShapeDtypeStruct((t_len, kvw), F32)),
        compiler_params=_params(("parallel", "arbitrary")),
    )(q, k, v, do)


SHIFTS = (-2, -1, 1, 2)


def _shift_masks(l_len, t_len):
    t = jnp.arange(t_len)[:, None]
    cols = []
    for d in SHIFTS:
        td = t + d
        ok = ((t < l_len) & (td >= 0) & (td < l_len)) | ((t >= l_len) & (td >= l_len) & (td < t_len))
        cols.append(jnp.broadcast_to(ok.astype(F32), (t_len, RNN_BLOCK)))
    return jnp.concatenate(cols, axis=1)


def _shift_rows(xv, d, m_ref, l_len, t_len):
    if d == 0:
        return xv
    k = SHIFTS.index(d)
    return pltpu.roll(xv, (-d) % t_len, 0) * m_ref[:, k * RNN_BLOCK:(k + 1) * RNN_BLOCK]


def _sublane_scan(a8, b8, rev):
    row = lax.broadcasted_iota(jnp.int32, a8.shape, 0)
    for s in (1, 2, 4):
        if rev:
            ok = row < SUBLANES - s
            sh = SUBLANES - s
        else:
            ok = row >= s
            sh = s
        a_sh = jnp.where(ok, pltpu.roll(a8, sh, 0), 1.0)
        b_sh = jnp.where(ok, pltpu.roll(b8, sh, 0), 0.0)
        b8 = a8 * b_sh + b8
        a8 = a8 * a_sh
    return a8, b8


def _scan_chunks(chains, c0, n, carries):
    def step(kk, hs):
        new = []
        for (a_ref, b_ref, h_ref, rev), hprev in zip(chains, hs):
            c = c0 + (n - 1 - kk if rev else kk)
            r0 = pl.multiple_of(c * SUBLANES, SUBLANES)
            big_a, big_b = _sublane_scan(a_ref[pl.ds(r0, SUBLANES), :], b_ref[pl.ds(r0, SUBLANES), :], rev)
            h8 = big_b + big_a * hprev
            h_ref[pl.ds(r0, SUBLANES), :] = h8
            edge = 0 if rev else SUBLANES - 1
            new.append(jnp.broadcast_to(h8[edge:edge + 1, :], h8.shape))
        return tuple(new)

    return lax.fori_loop(0, n, step, tuple(carries))


def _rglru_conv(x, cw_ref, cb_ref, m_ref, l_len, t_len):
    y = cb_ref[...] + _shift_rows(x, -1, m_ref, l_len, t_len) * cw_ref[0:1, :]
    for kk in range(1, CONV_WIDTH):
        y = y + _shift_rows(x, kk - 1, m_ref, l_len, t_len) * cw_ref[kk:kk + 1, :]
    return y


def _rglru_gates(xc, wr, wi, br, bi, lam):
    xb = xc.astype(BF16)
    r = _sigmoid(_dot(xb, wr.astype(BF16), 1, 0) + br)
    ig = _sigmoid(_dot(xb, wi.astype(BF16), 1, 0) + bi)
    sp = _softplus(-lam)
    log_a = (-LRU_C) * r * sp
    a = jnp.exp(log_a)
    z2 = 2.0 * log_a
    series = z2 * (1.0 + z2 * (0.5 + z2 * (1.0 / 6.0 + z2 * (1.0 / 24.0))))
    one_m = jnp.where(jnp.abs(z2) < 0.03, -series, 1.0 - a * a)
    rinv = lax.rsqrt(jnp.maximum(one_m, 1e-30))
    u = ig * xc
    return r, ig, sp, a, one_m * rinv, u, rinv


def _rglru_specs(proj, xr_blk, xg_blk, t_len):
    col = lambda off: pl.BlockSpec((t_len, RNN_BLOCK), lambda h, off=off: (0, off + h))
    vec = lambda rows: pl.BlockSpec((rows, RNN_BLOCK), lambda h: (0, h))
    wsp = pl.BlockSpec((2, 1, RNN_BLOCK, RNN_BLOCK), lambda h: (0, h, 0, 0))
    msp = pl.BlockSpec((t_len, len(SHIFTS) * RNN_BLOCK), lambda h: (0, 0))
    return col(xr_blk), col(xg_blk), vec, wsp, msp


def _rglru_fwd(proj, xr_blk, xg_blk, masks, conv_w, conv_b, w_rg, w_ig, b_rg, b_ig, lam, *, l_len, name):
    t_len = proj.shape[0]
    s_len = t_len - l_len
    d_rnn = conv_b.shape[1]
    n_l, n_t = l_len // SUBLANES, t_len // SUBLANES

    def body(xr_ref, xg_ref, m_ref, cw_ref, cb_ref, wr_ref, wi_ref, br_ref, bi_ref, lam_ref, rg_ref, hf_ref, hb_ref,
             a_s, b_s):
        xc = _rglru_conv(xr_ref[...], cw_ref, cb_ref, m_ref, l_len, t_len)
        for d in range(2):
            _, _, _, a, mult, u, _ = _rglru_gates(xc, wr_ref[d, 0], wi_ref[d, 0], br_ref[d:d + 1, :],
                                               bi_ref[d:d + 1, :], lam_ref[d:d + 1, :])
            a_s[d] = a
            b_s[d] = mult * u
        chains = [(a_s.at[0], b_s.at[0], hf_ref, False), (a_s.at[1], b_s.at[1], hb_ref, True)]
        zero = jnp.zeros((SUBLANES, RNN_BLOCK), F32)
        carry = _scan_chunks(chains, 0, n_l, (zero, zero))
        _scan_chunks(chains, n_l, n_t - n_l, carry)
        rnn = hf_ref[pl.ds(l_len, s_len), :] + hb_ref[pl.ds(l_len, s_len), :]
        rg_ref[...] = (rnn * _gelu(xg_ref[pl.ds(l_len, s_len), :])).astype(BF16)

    xr_spec, xg_spec, vec, wsp, msp = _rglru_specs(proj, xr_blk, xg_blk, t_len)
    col_t = pl.BlockSpec((t_len, RNN_BLOCK), lambda h: (0, h))
    return pl.pallas_call(
        body, name=name, grid=(d_rnn // RNN_BLOCK,),
        in_specs=[xr_spec, xg_spec, msp, vec(CONV_WIDTH), vec(1), wsp, wsp, vec(2), vec(2), vec(2)],
        out_specs=(pl.BlockSpec((s_len, RNN_BLOCK), lambda h: (0, h)), col_t, col_t),
        out_shape=(jax.ShapeDtypeStruct((s_len, d_rnn), BF16), jax.ShapeDtypeStruct((t_len, d_rnn), F32),
                   jax.ShapeDtypeStruct((t_len, d_rnn), F32)),
        scratch_shapes=[pltpu.VMEM((2, t_len, RNN_BLOCK), F32)] * 2,
        compiler_params=_params(("parallel",)),
    )(proj, proj, masks, conv_w, conv_b, w_rg, w_ig, b_rg, b_ig, lam)


N_RNN_VEC = 16


def _rglru_bwd(proj, xr_blk, xg_blk, masks, h_fwd, h_bwd, drg, conv_w, conv_b, w_rg, w_ig, b_rg, b_ig, lam, *, l_len, name):
    t_len = proj.shape[0]
    s_len = t_len - l_len
    d_rnn = conv_b.shape[1]
    n_l, n_t = l_len // SUBLANES, t_len // SUBLANES

    def body(xr_ref, xg_ref, m_ref, hf_ref, hb_ref, drg_ref, cw_ref, cb_ref, wr_ref, wi_ref, br_ref, bi_ref, lam_ref,
             dxr_ref, dxg_ref, dvec_ref, dwr_ref, dwi_ref, a_s, b_s, h_s):
        t = lax.broadcasted_iota(jnp.int32, (t_len, RNN_BLOCK), 0)
        x = xr_ref[...]
        xc = _rglru_conv(x, cw_ref, cb_ref, m_ref, l_len, t_len)
        xb = xc.astype(BF16)
        gates = []
        for d in range(2):
            gt = _rglru_gates(xc, wr_ref[d, 0], wi_ref[d, 0], br_ref[d:d + 1, :], bi_ref[d:d + 1, :],
                              lam_ref[d:d + 1, :])
            gates.append(gt)
        zero = jnp.zeros((SUBLANES, RNN_BLOCK), F32)
        h0, h1 = hf_ref[...], hb_ref[...]
        xg = xg_ref[...]
        dr = jnp.concatenate([jnp.zeros((l_len, RNN_BLOCK), F32), drg_ref[...].astype(F32)], axis=0)
        gel, dgel = _gelu_pair(xg)
        dxg_ref[...] = (dr * (h0 + h1) * dgel).astype(dxg_ref.dtype)
        drnn = dr * gel
        a_s[0] = pltpu.roll(gates[0][3], t_len - 1, 0)
        a_s[1] = pltpu.roll(gates[1][3], 1, 0)
        b_s[0] = drnn
        b_s[1] = drnn
        hprev = [jnp.where(t == 0, 0.0, pltpu.roll(h0, 1, 0)),
                 jnp.where(t == l_len - 1, 0.0, pltpu.roll(h1, t_len - 1, 0))]
        chains = [(a_s.at[0], b_s.at[0], h_s.at[0], True), (a_s.at[1], b_s.at[1], h_s.at[1], False)]
        carry = _scan_chunks(chains, n_l, n_t - n_l, (zero, zero))
        _scan_chunks(chains, 0, n_l, carry)

        dxc = jnp.zeros((t_len, RNN_BLOCK), F32)
        for d in range(2):
            r, ig, sp, a, mult, u, rinv = gates[d]
            lamb = h_s[d]
            da = lamb * hprev[d]
            dmult = lamb * u
            du = lamb * mult
            dlog_a = da * a - dmult * (a * a) * rinv
            dzr = (dlog_a * ((-LRU_C) * sp)) * r * (1.0 - r)
            dsp = jnp.sum(dlog_a * ((-LRU_C) * r), axis=0, keepdims=True)
            dzi = (du * xc) * ig * (1.0 - ig)
            dzrb, dzib = dzr.astype(BF16), dzi.astype(BF16)
            dxc = dxc + du * ig + _dot(dzrb, wr_ref[d, 0].astype(BF16), 1, 1) + _dot(dzib, wi_ref[d, 0].astype(BF16), 1, 1)
            dwr_ref[d, 0] = _dot(xb, dzrb, 0, 0)
            dwi_ref[d, 0] = _dot(xb, dzib, 0, 0)
            dvec_ref[5 + d:6 + d, :] = jnp.sum(dzr, axis=0, keepdims=True)
            dvec_ref[7 + d:8 + d, :] = jnp.sum(dzi, axis=0, keepdims=True)
            dvec_ref[9 + d:10 + d, :] = dsp * (-_sigmoid(-lam_ref[d:d + 1, :]))
        dvec_ref[4:5, :] = jnp.sum(dxc, axis=0, keepdims=True)
        dvec_ref[11:N_RNN_VEC, :] = jnp.zeros((N_RNN_VEC - 11, RNN_BLOCK), F32)
        dxr = jnp.zeros((t_len, RNN_BLOCK), F32)
        for kk in range(CONV_WIDTH):
            dvec_ref[kk:kk + 1, :] = jnp.sum(dxc * _shift_rows(x, kk - 1, m_ref, l_len, t_len), axis=0, keepdims=True)
            dxr = dxr + _shift_rows(dxc, 1 - kk, m_ref, l_len, t_len) * cw_ref[kk:kk + 1, :]
        dxr_ref[...] = dxr.astype(dxr_ref.dtype)

    xr_spec, xg_spec, vec, wsp, msp = _rglru_specs(proj, xr_blk, xg_blk, t_len)
    col_t = pl.BlockSpec((t_len, RNN_BLOCK), lambda h: (0, h))
    return pl.pallas_call(
        body, name=name, grid=(d_rnn // RNN_BLOCK,),
        in_specs=[xr_spec, xg_spec, msp, col_t, col_t, pl.BlockSpec((s_len, RNN_BLOCK), lambda h: (0, h)), vec(CONV_WIDTH),
                  vec(1),
                  wsp, wsp, vec(2), vec(2), vec(2)],
        out_specs=(col_t, col_t, vec(N_RNN_VEC), wsp, wsp),
        out_shape=(jax.ShapeDtypeStruct((t_len, d_rnn), BF16), jax.ShapeDtypeStruct((t_len, d_rnn), BF16),
                   jax.ShapeDtypeStruct((N_RNN_VEC, d_rnn), F32), jax.ShapeDtypeStruct(w_rg.shape, F32),
                   jax.ShapeDtypeStruct(w_ig.shape, F32)),
        scratch_shapes=[pltpu.VMEM((2, t_len, RNN_BLOCK), F32)] * 3,
        compiler_params=_params(("parallel",)),
    )(proj, proj, masks, h_fwd, h_bwd, drg, conv_w, conv_b, w_rg, w_ig, b_rg, b_ig, lam)


def _colsum(v):
    return jnp.sum(v, axis=0, keepdims=True)


def _qk_fwd_heads(xv, gain, cos_e, sin_e):
    outs = []
    for h in range(xv.shape[1] // HEAD_DIM):
        xh = xv[:, h * HEAD_DIM:(h + 1) * HEAD_DIM]
        y = xh * _rms(xh) * gain
        outs.append(y * cos_e + _swap_pairs(y) * sin_e)
    return outs[0] if len(outs) == 1 else jnp.concatenate(outs, axis=1)


def _qk_bwd_heads(xv, dov, gain, cos_e, sin_e):
    outs = []
    dgain = jnp.zeros((1, HEAD_DIM), F32)
    for h in range(xv.shape[1] // HEAD_DIM):
        cols = slice(h * HEAD_DIM, (h + 1) * HEAD_DIM)
        xh, doh = xv[:, cols], dov[:, cols]
        r = _rms(xh)
        xhat = xh * r
        dy = doh * cos_e + _swap_pairs(doh * sin_e)
        dgain = dgain + _colsum(dy * xhat)
        outs.append(_rms_bwd(xhat, r, dy * gain))
    return (outs[0] if len(outs) == 1 else jnp.concatenate(outs, axis=1)), dgain


WEIGHTS = ['c_ctx', 'w_mod', 'b_mod', 'g_mix', 'g_mlp', 'w_in', 'q_gain', 'k_gain', 'conv_w', 'conv_b', 'w_rg', 'b_rg',
           'w_ig', 'b_ig', 'lru_lambda', 'w_o_attn', 'w_o_rnn', 'w_out', 'w_up', 'w_down', 'g_final']
REPLICATED = ['c_ctx', 'b_mod', 'g_mix', 'g_mlp', 'q_gain', 'k_gain', 'conv_b', 'w_rg', 'w_ig', 'g_final']
SMALL_SHARDED = [('conv_w', CONV_WIDTH), ('b_rg', 2), ('b_ig', 2), ('lru_lambda', 2)]
SMALL_ROWS = 16


PACK_ROWS = 512


def _pack_flat(arrs, rows_mult):
    flat = jnp.concatenate([a.reshape(-1) for a in arrs])
    n = flat.shape[0]
    unit = rows_mult * LANES
    pad = (-n) % unit
    if pad:
        flat = jnp.concatenate([flat, jnp.zeros((pad,), flat.dtype)])
    return flat.reshape(-1, LANES)


def _unpack_flat(packed, shapes):
    flat = packed.reshape(-1)
    out, pos = [], 0
    for shp in shapes:
        n = math.prod(shp)
        out.append(flat[pos:pos + n].reshape(shp))
        pos += n
    return out


def _pack_small(d):
    rows = [d[n][0] for n, _ in SMALL_SHARDED]
    cols = rows[0].shape[1]
    used = sum(k for _, k in SMALL_SHARDED)
    return jnp.concatenate(rows + [jnp.zeros((SMALL_ROWS - used, cols), F32)], axis=0)


def _unpack_small(p):
    out, pos = {}, 0
    for n, k in SMALL_SHARDED:
        out[n] = p[pos:pos + k][None]
        pos += k
    return out


def kernel(x, c, ctx, c_ctx, w_mod, b_mod, g_mix, g_mlp, w_in, q_gain, k_gain, conv_w, conv_b, w_rg, b_rg, w_ig, b_ig, lru_lambda, w_o_attn, w_o_rnn, w_out, w_up, w_down, g_final, loss_target, m_c_ctx, m_w_mod, m_b_mod, m_g_mix, m_g_mlp, m_w_in, m_q_gain, m_k_gain, m_conv_w, m_conv_b, m_w_rg, m_b_rg, m_w_ig, m_b_ig, m_lru_lambda, m_w_o_attn, m_w_o_rnn, m_w_out, m_w_up, m_w_down, m_g_final, v_c_ctx, v_w_mod, v_b_mod, v_g_mix, v_g_mlp, v_w_in, v_q_gain, v_k_gain, v_conv_w, v_conv_b, v_w_rg, v_b_rg, v_w_ig, v_b_ig, v_lru_lambda, v_w_o_attn, v_w_o_rnn, v_w_out, v_w_up, v_w_down, v_g_final):
    wd = dict(c_ctx=c_ctx, w_mod=w_mod, b_mod=b_mod, g_mix=g_mix, g_mlp=g_mlp, w_in=w_in, q_gain=q_gain, k_gain=k_gain,
              conv_w=conv_w, conv_b=conv_b, w_rg=w_rg, b_rg=b_rg, w_ig=w_ig, b_ig=b_ig, lru_lambda=lru_lambda,
              w_o_attn=w_o_attn, w_o_rnn=w_o_rnn, w_out=w_out, w_up=w_up, w_down=w_down, g_final=g_final)
    md = dict(zip(WEIGHTS, (m_c_ctx, m_w_mod, m_b_mod, m_g_mix, m_g_mlp, m_w_in, m_q_gain, m_k_gain, m_conv_w, m_conv_b,
                            m_w_rg, m_b_rg, m_w_ig, m_b_ig, m_lru_lambda, m_w_o_attn, m_w_o_rnn, m_w_out, m_w_up,
                            m_w_down, m_g_final)))
    vd = dict(zip(WEIGHTS, (v_c_ctx, v_w_mod, v_b_mod, v_g_mix, v_g_mlp, v_w_in, v_q_gain, v_k_gain, v_conv_w, v_conv_b,
                            v_w_rg, v_b_rg, v_w_ig, v_b_ig, v_lru_lambda, v_w_o_attn, v_w_o_rnn, v_w_out, v_w_up,
                            v_w_down, v_g_final)))

    xs, tgt, ctxs = x[0], loss_target[0], ctx[0]
    s_len, d = xs.shape
    l_len = ctxs.shape[0]
    t_len = s_len + l_len
    n_in = w_in.shape[2] * N_DEV
    aw = (n_in - 4 * d) * 2 // 3
    kvw = aw // Q_GROUP
    assert aw == d and n_in == aw + 2 * kvw + 4 * d
    nmod = N_MOD * d // N_DEV
    tr = 128
    assert l_len % tr == 0 and s_len % tr == 0
    lb = l_len // tr
    tq = min(256, s_len)
    my = _lin(_my_pos())
    c_ctx2 = c_ctx[None]
    g_final2 = g_final[None]

    c_all = _all_gather([jnp.broadcast_to(c, (SUBLANES, d))], [0], "ag_cond")[0][::SUBLANES]
    c16 = jnp.concatenate([c_all, jnp.broadcast_to(c_ctx2, (N_DEV, d))], axis=0)
    b_mod_sh = lax.dynamic_slice(b_mod, (0, my * nmod), (1, nmod))
    mod_part = _mm(c16, w_mod[0], a_fn=_silu, bias=b_mod_sh, name="mod_fwd")
    sp_w = _pack_small(wd)
    (mod_g, sp_f), order = _all_gather([mod_part, sp_w], [1, 1], "ag_mod", with_token=True)
    conv_w_f, b_rg_f, b_ig_f, lam_f = sp_f[0:4], sp_f[4:6], sp_f[6:8], sp_f[8:10]
    mod_lat = lax.dynamic_slice(mod_g, (my, 0), (1, N_MOD * d))
    sh_a, sc_a, ga_a, sh_f, sc_f, ga_f = [mod_lat[:, k * d:(k + 1) * d] for k in range(N_MOD)]
    csh_a, csc_a = mod_g[N_DEV:N_DEV + 1, :d], mod_g[N_DEV:N_DEV + 1, d:2 * d]

    x_all = jnp.concatenate([ctxs, xs], axis=0)

    def nm1_fwd(i, j, xv, g, csh, csc, sh, sc):
        is_ctx = i < lb
        n = xv * _rms(xv) * g
        return n * (1.0 + jnp.where(is_ctx, csc, sc)) + jnp.where(is_ctx, csh, sh)

    h_all = _rowwise(nm1_fwd, [_row(x_all), _bc(g_mix), _bc(csh_a), _bc(csc_a), _bc(sh_a), _bc(sc_a)],
                     [("row", BF16)], nr=t_len // tr, tr=tr, nc=1, cw=d, name="norm1_fwd")[0]
    w_in_f, proj, order = _gather_proj(h_all, _tok(w_in[0], order).astype(BF16), "ag_proj")
    late_axis = dict(w_o_attn=0, w_o_rnn=0, w_out=0, w_up=1, w_down=0)
    placed = [_place_block(wd[n][0], ax, my, "place_" + n) for n, ax in late_axis.items()]
    late_kind = lambda n: "gather_chips" if n in ("w_up", "w_down") else "gather"
    late = {}
    for (n, ax), full in zip(late_axis.items(), placed):
        late[n] = _xfer_start(late_kind(n), order, full, ax, "ag_start_" + n)
        order = late[n][4]
    q_gain_fwd = _tok(q_gain, order)

    def late_wait(n, after):
        full = _xfer_wait(late_kind(n), late[n], late_axis[n], after, "ag_wait_" + n)
        return _sibling_swap(full, late_axis[n], "ag_swap_" + n) if late_kind(n) == "gather_chips" else full

    pos = jnp.arange(s_len)
    n_freq = HEAD_DIM // 4
    inv_freq = ROPE_THETA ** (-jnp.arange(n_freq, dtype=F32) / n_freq)
    ang = jnp.concatenate([(pos // GRID_W).astype(F32)[:, None] * inv_freq,
                           (pos % GRID_W).astype(F32)[:, None] * inv_freq], axis=-1)
    sign = jnp.where(jnp.arange(HEAD_DIM) % 2 == 0, -1.0, 1.0).astype(F32)
    cos_e = jnp.concatenate([jnp.ones((l_len, HEAD_DIM), F32), jnp.repeat(jnp.cos(ang), 2, axis=-1)], axis=0)
    sin_e = jnp.concatenate([jnp.zeros((l_len, HEAD_DIM), F32), jnp.repeat(jnp.sin(ang), 2, axis=-1) * sign], axis=0)


    kb = aw // kvw
    q = _rowwise(lambda i, j, xv, gain, ce, se: _qk_fwd_heads(xv, gain, ce, se),
                 [_row(proj, 0, lb, aw), _full(q_gain_fwd), _row(cos_e, 0, lb, HEAD_DIM), _row(sin_e, 0, lb, HEAD_DIM)],
                 [("row", BF16, aw)], nr=s_len // tr, tr=tr, nc=1, cw=aw, name="q_fwd")[0]
    k_all, v_all = _rowwise(lambda i, j, kv_, vv_, gain, ce, se: (_qk_fwd_heads(kv_, gain, ce, se), vv_),
                            [_row(proj, kb, 0, kvw), _row(proj, kb + 1, 0, kvw), _full(k_gain),
                             _row(cos_e, 0, 0, HEAD_DIM), _row(sin_e, 0, 0, HEAD_DIM)],
                            [("row", BF16, kvw), ("row", BF16, kvw)], nr=t_len // tr, tr=tr, nc=1, cw=kvw, name="kv_fwd")
    attn_o = _attn_fwd(q, k_all, v_all, tq=tq, name="attn_fwd")

    xr_blk = (aw + 2 * kvw) // RNN_BLOCK
    xg_blk = xr_blk + d // RNN_BLOCK
    shift_masks = _shift_masks(l_len, t_len)
    rg, h_fwd, h_bwd = _rglru_fwd(proj, xr_blk, xg_blk, shift_masks, conv_w_f, conv_b, w_rg[0], w_ig[0], b_rg_f, b_ig_f, lam_f, l_len=l_len,
                    name="rglru_fwd")

    w_oa_f = late_wait("w_o_attn", attn_o)
    y_attn = _mm(attn_o, w_oa_f, name="o_attn_fwd")
    w_or_f = late_wait("w_o_rnn", rg)
    y_rnn = _mm(rg, w_or_f, name="o_rnn_fwd")
    hw = d // 2
    glb = (aw + 2 * kvw + 2 * d) // hw

    def merge_fwd(i, j, ya, yr, gla, glr):
        return _sigmoid(gla) * ya + _sigmoid(glr) * yr

    u = _rowwise(merge_fwd, [_row(y_attn), _row(y_rnn), _row(proj, glb, lb), _row(proj, glb + 2, lb)],
                 [("row", BF16)], nr=s_len // tr, tr=tr, nc=2, cw=hw, name="merge_fwd")[0]
    w_out_f = late_wait("w_out", u)
    mix = _mm(u, w_out_f, name="out_fwd")

    def res1_fwd(i, j, xv, mixv, gaa, gm, shf, scf):
        x1v = xv + gaa * mixv
        n = x1v * _rms(x1v) * gm
        return x1v, n * (1.0 + scf) + shf

    x1, h2 = _rowwise(res1_fwd, [_row(xs), _row(mix), _bc(ga_a), _bc(g_mlp), _bc(sh_f), _bc(sc_f)],
                      [("row", F32), ("row", BF16)], nr=s_len // tr, tr=tr, nc=1, cw=d, name="norm2_fwd")
    relu2 = lambda z: jnp.square(jnp.maximum(z, 0.0))
    w_up_f = late_wait("w_up", h2)
    up = _mm(h2, w_up_f, name="up_fwd")
    w_down_f = late_wait("w_down", up)
    down = _mm(up, w_down_f, a_fn=relu2, name="down_fwd")

    def final_fn(i, j, x1v, dn, tg, gaf, gf):
        x2 = x1v + gaf * dn
        r = _rms(x2)
        xh = x2 * r
        e = xh * gf - tg
        dy = e * (1.0 / d)
        dx2 = _rms_bwd(xh, r, dy * gf)
        return dx2, dx2 * gaf, _colsum(e * e) * (0.5 / d), _colsum(dy * xh), _colsum(dx2 * dn)

    dx2, ddown, loss_cols, g_gfinal, d_ga_f = _rowwise(
        final_fn, [_row(x1), _row(down), _row(tgt), _bc(ga_f), _bc(g_final2)],
        [("row", F32), ("row", BF16), ("acc", 1), ("acc", 1), ("acc", 1)], nr=s_len // tr, tr=tr, nc=1, cw=d,
        name="final_bwd")
    loss = lax.psum(jnp.sum(loss_cols), ("x", "y", "c"))

    dup = _mm(ddown, w_down_f, tb=True, out_fn=lambda r, upv: r * (2.0 * jnp.maximum(upv, 0.0)), extras=[up],
              out_dtype=BF16, name="down_bwd_x")
    gw_down = _mm(up, ddown, ta=True, a_fn=relu2, out_dtype=BF16, name="down_bwd_w")
    sent = dict(w_down=_scatter_start(gw_down, 0, my, "rs_start_w_down"))
    dh2 = _mm(dup, w_up_f, tb=True, name="up_bwd_x")
    gw_up = _mm(h2, dup, ta=True, out_dtype=BF16, name="up_bwd_w")
    sent["w_up"] = _scatter_start(gw_up, 1, my, "rs_start_w_up")
    sc_f_bwd = _tok(sc_f, sent["w_down"][0][4], sent["w_up"][0][4])

    def norm2_bwd(i, j, x1v, mixv, dh2v, dx2v, gaa, gm, scf):
        r = _rms(x1v)
        xh = x1v * r
        n = xh * gm
        dn = dh2v * (1.0 + scf)
        dx1 = dx2v + _rms_bwd(xh, r, dn * gm)
        return dx1, dx1 * gaa, _colsum(dh2v), _colsum(dh2v * n), _colsum(dn * xh), _colsum(dx1 * mixv)

    dx1, dmix, d_sh_f, d_sc_f, g_gmlp, d_ga_a = _rowwise(
        norm2_bwd, [_row(x1), _row(mix), _row(dh2), _row(dx2), _bc(ga_a), _bc(g_mlp), _bc(sc_f_bwd)],
        [("row", F32), ("row", BF16), ("acc", 1), ("acc", 1), ("acc", 1), ("acc", 1)], nr=s_len // tr, tr=tr, nc=1,
        cw=d, name="norm2_bwd")

    du = _mm(dmix, w_out_f, tb=True, name="out_bwd_x")
    gw_out = _mm(u, dmix, ta=True, out_dtype=BF16, name="out_bwd_w")
    sent["w_out"] = _scatter_start(gw_out, 0, my, "rs_start_w_out")

    def merge_bwd(i, j, duv, ya, yr, gla, glr):
        ga, gr = _sigmoid(gla), _sigmoid(glr)
        return duv * ga, duv * gr, duv * ya * ga * (1.0 - ga), duv * yr * gr * (1.0 - gr)

    dya, dyr, dgla, dglr = _rowwise(
        merge_bwd, [_row(du), _row(y_attn), _row(y_rnn), _row(proj, glb, lb), _row(proj, glb + 2, lb)],
        [("row", BF16)] * 4, nr=s_len // tr, tr=tr, nc=2, cw=hw, name="merge_bwd")
    dattn = _mm(dya, w_oa_f, tb=True, out_dtype=BF16, name="o_attn_bwd_x")
    gw_oa = _mm(attn_o, dya, ta=True, out_dtype=BF16, name="o_attn_bwd_w")
    drg = _mm(dyr, w_or_f, tb=True, out_dtype=BF16, name="o_rnn_bwd_x")
    gw_or = _mm(rg, dyr, ta=True, out_dtype=BF16, name="o_rnn_bwd_w")
    sent["w_o_attn"] = _scatter_start(gw_oa, 0, my, "rs_start_w_o_attn")
    sent["w_o_rnn"] = _scatter_start(gw_or, 0, my, "rs_start_w_o_rnn")
    conv_b_bwd = _tok(conv_b, sent["w_out"][0][4], sent["w_o_attn"][0][4], sent["w_o_rnn"][0][4])

    dq, dk, dv = _attn_bwd(q, k_all, v_all, dattn, tq=tq, name="attn_bwd")
    dxr, dxg, dvec, g_wrg, g_wig = _rglru_bwd(proj, xr_blk, xg_blk, shift_masks, h_fwd, h_bwd, drg, conv_w_f, conv_b_bwd, w_rg[0], w_ig[0], b_rg_f,
                                               b_ig_f, lam_f, l_len=l_len, name="rglru_bwd")
    gate_part = _pack_flat([g_wrg, g_wig], N_DEV * SUBLANES)
    gate_sent = _scatter_start(gate_part, 0, my, "rs_start_gates")
    q_gain_bwd = _tok(q_gain, gate_sent[0][4])

    dq_p, g_qgain = _rowwise(lambda i, j, xv, dov, gain, ce, se: _qk_bwd_heads(xv, dov, gain, ce, se),
                             [_row(proj, 0, lb, aw), _row(dq, 0, 0, aw), _full(q_gain_bwd), _row(cos_e, 0, lb, HEAD_DIM),
                              _row(sin_e, 0, lb, HEAD_DIM)],
                             [("row", BF16, aw), ("acc", 1, HEAD_DIM)], nr=s_len // tr, tr=tr, nc=1, cw=aw, name="q_bwd")

    def kv_bwd(i, j, xv, dkv, dvv, gain, ce, se):
        dk_p, dgain = _qk_bwd_heads(xv, dkv, gain, ce, se)
        return dk_p, dvv, dgain

    dk_p, dv_p, g_kgain = _rowwise(kv_bwd, [_row(proj, kb, 0, kvw), _row(dk, 0, 0, kvw), _row(dv, 0, 0, kvw),
                                            _full(k_gain), _row(cos_e, 0, 0, HEAD_DIM), _row(sin_e, 0, 0, HEAD_DIM)],
                                   [("row", BF16, kvw), ("row", BF16, kvw), ("acc", 1, HEAD_DIM)], nr=t_len // tr, tr=tr,
                                   nc=1, cw=kvw, name="kv_bwd")

    pad_ctx = lambda a: jnp.pad(a, ((l_len, 0), (0, 0)))
    dproj = jnp.concatenate([pad_ctx(dq_p), dk_p, dv_p, dxr, dxg, pad_ctx(dgla), pad_ctx(dglr)], axis=1)
    dh_all = _mm(dproj, w_in_f, tb=True, name="proj_bwd_x")
    g_mix_bwd = g_mix

    def norm1_bwd(i, j, xv, dhv, dresv, g, sc):
        r = _rms(xv)
        xh = xv * r
        dn = dhv * (1.0 + sc)
        return dresv + _rms_bwd(xh, r, dn * g), _colsum(dhv), _colsum(dhv * (xh * g)), _colsum(dn * xh)

    def norm1_bwd_ctx(i, j, xv, dhv, g, sc):
        xh = xv * _rms(xv)
        return _colsum(dhv), _colsum(dhv * (xh * g)), _colsum(dhv * (1.0 + sc) * xh)

    dx_lat, d_sh_a, d_sc_a, g_gmix_lat = _rowwise(
        norm1_bwd, [_row(x_all, 0, lb), _row(dh_all, 0, lb), _row(dx1), _bc(g_mix_bwd), _bc(sc_a)],
        [("row", F32)] + [("acc", 1)] * 3, nr=s_len // tr, tr=tr, nc=1, cw=d, name="norm1_bwd")
    d_csh, d_csc, g_gmix_ctx = _rowwise(
        norm1_bwd_ctx, [_row(x_all), _row(dh_all), _bc(g_mix_bwd), _bc(csc_a)],
        [("acc", 1)] * 3, nr=lb, tr=tr, nc=1, cw=d, name="norm1_bwd_ctx")
    g_gmix = g_gmix_lat + g_gmix_ctx
    grad_x = dx_lat[None]

    dmod_lat = jnp.concatenate([d_sh_a, d_sc_a, d_ga_a, d_sh_f, d_sc_f, d_ga_f], axis=1)
    dmod_ctx = jnp.concatenate([d_csh, d_csc, jnp.zeros((1, (N_MOD - 2) * d), F32)], axis=1)
    dm_g = _all_gather([jnp.concatenate([dmod_lat, dmod_ctx, jnp.zeros((SUBLANES - 2, N_MOD * d), F32)], axis=0)], [0],
                       "ag_dmod")[0]
    dm16 = jnp.concatenate([dm_g[0::SUBLANES], dm_g[1::SUBLANES]], axis=0)
    dm_sh = lax.dynamic_slice(dm16, (0, my * nmod), (2 * N_DEV, nmod))
    gw_mod = _mm(c16, dm_sh, ta=True, a_fn=_silu, name="mod_bwd_w")
    dc16 = _mm(dm_sh, w_mod[0], tb=True, name="mod_bwd_x")

    def cctx_fn(i, j, pv, cc, dl, dc):
        row = lax.broadcasted_iota(jnp.int32, pv.shape, 0)
        return _colsum(jnp.where(row >= N_DEV, pv, 0.0)) * _dsilu(cc), dl + dc

    g_cctx, g_bmod = _rowwise(cctx_fn, [_full(dc16), _full(c_ctx2), _full(dmod_lat), _full(dmod_ctx)],
                              [("row", F32, d), ("row", F32, N_MOD * d)], nr=1, tr=1, nc=1, cw=d, name="cctx_bwd")

    rep_part = dict(c_ctx=g_cctx, b_mod=g_bmod, g_mix=g_gmix, g_mlp=g_gmlp, q_gain=g_qgain, k_gain=g_kgain,
                    conv_b=dvec[4:5], g_final=g_gfinal)
    rep_small = [n for n in REPLICATED if n in rep_part]
    gates = [n for n in REPLICATED if n not in rep_part]
    part_pack = _pack_flat([rep_part[n] for n in rep_small], PACK_ROWS)
    rep_rows = part_pack.shape[0]
    gate_land = _xfer_wait("scatter", gate_sent[0], 0, g_bmod, "rs_wait_gates")
    gate_sum = _sum_parts(gate_sent[1], gate_land, name="gate_sum")
    (gate_full, parts), order = _all_gather([gate_sum, part_pack], [0, 0], "ag_rep_grads", with_token=True)
    small_g = jnp.concatenate([dvec[0:4], dvec[5:11], jnp.zeros((SMALL_ROWS - 10, d), F32)], axis=0)
    (small_recv,), order = _all_to_all([_tok(small_g, order)], [1], "a2a_small_grads")

    halves = {}
    for part, lo in (("top", 0), ("bot", d // 2)):
        gw_part = _mm(h_all[:, lo:lo + d // 2], dproj, ta=True, out_dtype=BF16, name="proj_bwd_w_" + part,
                      bias=_tok(jnp.zeros((1, n_in), F32), order))
        halves[part] = _chip_scatter_start(_pair_reduce(gw_part, my, "pair_w_in_" + part), my, "rs_start_w_in_" + part)
        order = halves[part][0][4]
    started = order

    res = {}
    for names, pr, mult, nm in ((rep_small, parts.reshape(N_DEV, rep_rows, LANES), PACK_ROWS, "adamw_rep"),
                                (gates, gate_full[None], N_DEV * SUBLANES, "adamw_gates")):
        rep_out = _adamw(pr, _tok(_pack_flat([wd[n] for n in names], mult), started),
                         _pack_flat([md[n] for n in names], mult), _pack_flat([vd[n] for n in names], mult), name=nm)
        for kind, packed in zip(("g", "d", "m", "v"), rep_out):
            for n, a in zip(names, _unpack_flat(packed, [wd[n].shape for n in names])):
                res[kind, n] = a
    out = _adamw(small_recv, _tok(sp_w, started), _pack_small(md), _pack_small(vd), name="adamw_small")
    for kind, a in zip(("g", "d", "m", "v"), out):
        for n, piece in _unpack_small(a).items():
            res[kind, n] = piece

    after = started
    for n, (handle, own, ax, kind) in sent.items():
        land = _xfer_wait(kind, handle, ax, after, "rs_wait_" + n)
        out = _adamw(land, wd[n][0], md[n][0], vd[n][0], own=own, name="adamw_" + n)
        after = out[0]
        for kind, a in zip(("g", "d", "m", "v"), out):
            res[kind, n] = a[None]
    out = _adamw(gw_mod[None], w_mod[0], m_w_mod[0], v_w_mod[0], name="adamw_w_mod")
    after = out[0]
    for kind, a in zip(("g", "d", "m", "v"), out):
        res[kind, 'w_mod'] = a[None]
    lands, owns = [], []
    for part, (handle, own, ax, kind) in halves.items():
        lands.append(_xfer_wait(kind, handle, ax, after, "rs_wait_w_in_" + part))
        owns.append(own)
        after = lands[-1]
    out = _adamw(jnp.concatenate(lands, axis=1), w_in[0], m_w_in[0], v_w_in[0], own=jnp.concatenate(owns, axis=0),
                 name="adamw_w_in")
    for kind, a in zip(("g", "d", "m", "v"), out):
        res[kind, 'w_in'] = a[None]

    outs = [loss, grad_x]
    for kind in ("g", "d", "m", "v"):
        outs += [res[kind, n].reshape(wd[n].shape) for n in WEIGHTS]
    return tuple(outs)
```

```python
import functools
import math

import jax
import jax.numpy as jnp
from jax import lax
from jax.experimental import pallas as pl
from jax.experimental.pallas import tpu as pltpu

F32 = jnp.float32
BF16 = jnp.bfloat16
MESH = pl.DeviceIdType.MESH

N_DEV = 8
HEAD_DIM = 128
Q_GROUP = 4
RNN_BLOCK = 128
GRID_W = 64
CONV_WIDTH = 4
LRU_C = 8.0
ROPE_THETA = 10000.0
NORM_EPS = 1e-6
N_MOD = 6
ADAM_LR = 0.001
ADAM_B1 = 0.9
ADAM_B2 = 0.999
ADAM_EPS = 1e-08
ADAM_WD = 0.01
ADAM_STEP = 10

LANES = 128
SUBLANES = 8
VMEM_LIMIT = 56 * 1024 * 1024
HBM_SPEC = pl.BlockSpec(memory_space=pltpu.HBM)


def _params(sem=None, vmem=VMEM_LIMIT):
    kw = dict(vmem_limit_bytes=vmem)
    if sem is not None:
        kw["dimension_semantics"] = sem
    return pltpu.CompilerParams(**kw)


def _pick(dim, target, mult):
    for t in range(min(dim, target), 0, -1):
        if dim % t == 0 and t % mult == 0:
            return t
    return dim


def _my_pos():
    return lax.axis_index("x"), lax.axis_index("y"), lax.axis_index("c")


def _flip(pos, d):
    x, y, c = pos
    return (1 - x if d & 4 else x, 1 - y if d & 2 else y, 1 - c if d & 1 else c)


def _lin(pos):
    return 4 * pos[0] + 2 * pos[1] + pos[2]


def _slab(ref, axis, idx, size):
    if axis == 0:
        return ref.at[pl.ds(idx * size, size)]
    assert axis == 1
    return ref.at[:, pl.ds(idx * size, size)]


def _all_gather(arrs, axes, name, with_token=False):
    n = len(arrs)
    sizes = [a.shape[ax] for a, ax in zip(arrs, axes)]

    def body(*refs):
        ins, outs, token = refs[:n], refs[n:2 * n], refs[2 * n]
        send_sems, recv_sems, local_sems = refs[2 * n + 1:]
        token[...] = jnp.zeros_like(token)
        me = _my_pos()
        sibling = _flip(me, 1)
        chips = [2, 4, 6]

        def copy(a, k, block_pos, to, src=None):
            dst = _slab(outs[a], axes[a], _lin(block_pos), sizes[a])
            return pltpu.make_async_remote_copy(
                src_ref=dst if src is None else src, dst_ref=dst,
                send_sem=send_sems.at[7 * a + k], recv_sem=recv_sems.at[7 * a + k],
                device_id=to, device_id_type=MESH)

        local, first, passed = [], [], []
        for a in range(n):
            cp = pltpu.make_async_copy(ins[a], _slab(outs[a], axes[a], _lin(me), sizes[a]), local_sems.at[a])
            cp.start()
            local.append(cp)
        for a in range(n):
            f = [copy(a, 0, me, sibling, src=ins[a])]
            f += [copy(a, 1 + j, me, _flip(me, m), src=ins[a]) for j, m in enumerate(chips)]
            for cp in f:
                cp.start()
            first += f
        for a in range(n):
            for j, m in enumerate(chips):
                copy(a, 1 + j, _flip(me, m), me).wait_recv()
                cp = copy(a, 4 + j, _flip(me, m), sibling)
                cp.start()
                passed.append(cp)
        for a in range(n):
            copy(a, 0, sibling, me).wait_recv()
            for j, m in enumerate(chips):
                copy(a, 4 + j, _flip(sibling, m), me).wait_recv()
        for cp in first + passed:
            cp.wait_send()
        for cp in local:
            cp.wait()

    out_shape = []
    for a, ax in zip(arrs, axes):
        shp = list(a.shape)
        shp[ax] *= N_DEV
        out_shape.append(jax.ShapeDtypeStruct(tuple(shp), a.dtype))
    out_shape.append(jax.ShapeDtypeStruct((SUBLANES, LANES), F32))
    res = pl.pallas_call(
        body, name=name, out_shape=tuple(out_shape),
        in_specs=[HBM_SPEC] * n, out_specs=tuple([HBM_SPEC] * n + [pl.BlockSpec(memory_space=pltpu.VMEM)]),
        scratch_shapes=[pltpu.SemaphoreType.DMA((7 * n,)), pltpu.SemaphoreType.DMA((7 * n,)),
                        pltpu.SemaphoreType.DMA((n,))],
    )(*arrs)
    return (list(res[:n]), res[n]) if with_token else list(res[:n])


def _gather_proj(h_all, w_shard, name):
    t_len, kdim = h_all.shape
    sz = w_shard.shape[1]
    tm = _pick(t_len, 1024, 16)
    n_m = t_len // tm

    def body(h_ref, w_ref, wfull_ref, proj_ref, token, a_buf, b_buf, o_buf, send_sems, recv_sems, io_sems, out_sems):
        token[...] = jnp.zeros_like(token)
        me = _my_pos()
        sibling = _flip(me, 1)
        chips = [2, 4, 6]
        slab = lambda pos: _slab(wfull_ref, 1, _lin(pos), sz)

        def copy(k, block_pos, to, src=None):
            return pltpu.make_async_remote_copy(
                src_ref=slab(block_pos) if src is None else src, dst_ref=slab(block_pos),
                send_sem=send_sems.at[k], recv_sem=recv_sems.at[k], device_id=to, device_id_type=MESH)

        pending = []

        def block_product(pos, b_src):
            load = pltpu.make_async_copy(b_src, b_buf, io_sems.at[1])
            load.start()
            load.wait()
            for i in range(n_m):
                slot = len(pending) % 2
                if len(pending) >= 2:
                    pending[-2].wait()
                o_buf[slot] = _dot(a_buf[pl.ds(i * tm, tm), :], b_buf[...], 1, 0)
                out = pltpu.make_async_copy(
                    o_buf.at[slot], proj_ref.at[pl.ds(i * tm, tm), pl.ds(_lin(pos) * sz, sz)], out_sems.at[slot])
                out.start()
                pending.append(out)

        local = pltpu.make_async_copy(w_ref, slab(me), io_sems.at[0])
        local.start()
        first = [copy(0, me, sibling, src=w_ref)] + [copy(1 + j, me, _flip(me, m), src=w_ref) for j, m in enumerate(chips)]
        for cp in first:
            cp.start()
        load_a = pltpu.make_async_copy(h_ref, a_buf, io_sems.at[2])
        load_a.start()
        load_a.wait()
        block_product(me, w_ref)
        copy(0, sibling, me).wait_recv()
        block_product(sibling, slab(sibling))
        passed = []
        for j, m in enumerate(chips):
            src_pos = _flip(me, m)
            copy(1 + j, src_pos, me).wait_recv()
            cp = copy(4 + j, src_pos, sibling)
            cp.start()
            passed.append(cp)
            block_product(src_pos, slab(src_pos))
        for j, m in enumerate(chips):
            src_pos = _flip(sibling, m)
            copy(4 + j, src_pos, me).wait_recv()
            block_product(src_pos, slab(src_pos))
        for out in pending[-2:]:
            out.wait()
        for cp in first + passed:
            cp.wait_send()
        local.wait()

    return pl.pallas_call(
        body, name=name,
        out_shape=(jax.ShapeDtypeStruct((kdim, N_DEV * sz), w_shard.dtype), jax.ShapeDtypeStruct((t_len, N_DEV * sz), F32),
                   jax.ShapeDtypeStruct((SUBLANES, LANES), F32)),
        in_specs=[HBM_SPEC, HBM_SPEC], out_specs=(HBM_SPEC, HBM_SPEC, pl.BlockSpec(memory_space=pltpu.VMEM)),
        scratch_shapes=[pltpu.VMEM((t_len, kdim), h_all.dtype), pltpu.VMEM((kdim, sz), w_shard.dtype),
                        pltpu.VMEM((2, tm, sz), F32), pltpu.SemaphoreType.DMA((7,)), pltpu.SemaphoreType.DMA((7,)),
                        pltpu.SemaphoreType.DMA((3,)), pltpu.SemaphoreType.DMA((2,))],
        compiler_params=_params(),
    )(h_all, w_shard)


def _all_to_all(arrs, axes, name):
    n = len(arrs)
    sizes = [a.shape[ax] // N_DEV for a, ax in zip(arrs, axes)]

    def body(*refs):
        ins, outs, token = refs[:n], refs[n:2 * n], refs[2 * n]
        send_sems, recv_sems, local_sems = refs[2 * n + 1:]
        token[...] = jnp.zeros_like(token)
        me = _my_pos()
        my = _lin(me)
        copies, local = [], []
        for a in range(n):
            cp = pltpu.make_async_copy(_slab(ins[a], axes[a], my, sizes[a]), outs[a].at[my], local_sems.at[a])
            cp.start()
            local.append(cp)
        for d in range(1, N_DEV):
            peer = _flip(me, d)
            for a in range(n):
                cp = pltpu.make_async_remote_copy(
                    src_ref=_slab(ins[a], axes[a], _lin(peer), sizes[a]), dst_ref=outs[a].at[my],
                    send_sem=send_sems.at[7 * a + d - 1], recv_sem=recv_sems.at[7 * a + d - 1],
                    device_id=peer, device_id_type=MESH)
                cp.start()
                copies.append((a, d, cp))
        for a, d, cp in copies:
            peer = _flip(me, d)
            pltpu.make_async_remote_copy(
                src_ref=_slab(ins[a], axes[a], _lin(peer), sizes[a]), dst_ref=outs[a].at[_lin(peer)],
                send_sem=send_sems.at[7 * a + d - 1], recv_sem=recv_sems.at[7 * a + d - 1],
                device_id=peer, device_id_type=MESH).wait_recv()
            cp.wait_send()
        for cp in local:
            cp.wait()

    out_shape = []
    for a, ax, s in zip(arrs, axes, sizes):
        shp = list(a.shape)
        shp[ax] = s
        out_shape.append(jax.ShapeDtypeStruct((N_DEV,) + tuple(shp), a.dtype))
    out_shape.append(jax.ShapeDtypeStruct((SUBLANES, LANES), F32))
    res = pl.pallas_call(
        body, name=name, out_shape=tuple(out_shape),
        in_specs=[HBM_SPEC] * n, out_specs=tuple([HBM_SPEC] * n + [pl.BlockSpec(memory_space=pltpu.VMEM)]),
        scratch_shapes=[pltpu.SemaphoreType.DMA((7 * n,)), pltpu.SemaphoreType.DMA((7 * n,)),
                        pltpu.SemaphoreType.DMA((n,))],
    )(*arrs)
    return list(res[:n]), res[n]


SEM_SPEC = pl.BlockSpec(memory_space=pltpu.SEMAPHORE)
_EFFECT = pltpu.SideEffectType.DATAFLOW_SIDE_EFFECTING
N_PEERS = N_DEV - 1


def _split_copies(kind, src_ref, land_ref, send_sems, recv_sems, axis, size, arriving):
    me = _my_pos()
    out = []
    if kind == "sibling":
        sibling = _flip(me, 1)
        for k in range(N_CHIPS):
            blk = _slab(land_ref, axis, 2 * k + (sibling[2] if arriving else me[2]), size)
            out.append(pltpu.make_async_remote_copy(src_ref=blk, dst_ref=blk, send_sem=send_sems.at[k],
                                                    recv_sem=recv_sems.at[k], device_id=sibling, device_id_type=MESH))
        return out
    for dd in (range(2, N_DEV, 2) if kind in ("chips", "gather_chips") else range(1, N_DEV)):
        peer = _flip(me, dd)
        k = dd - 1
        if kind in ("gather", "gather_chips"):
            src = _slab(land_ref, axis, _lin(me), size)
            dst = _slab(land_ref, axis, _lin(peer if arriving else me), size)
        elif kind == "scatter":
            src, dst = _slab(src_ref, axis, _lin(peer), size), land_ref.at[k]
        else:
            k = dd // 2 - 1
            src, dst = src_ref.at[2 * peer[0] + peer[1]], land_ref.at[k]
        out.append(pltpu.make_async_remote_copy(
            src_ref=src, dst_ref=dst, send_sem=send_sems.at[k], recv_sem=recv_sems.at[k],
            device_id=peer, device_id_type=MESH))
    return out


N_CHIPS = N_DEV // 2


def _split_size(kind, src_shape, land_shape, axis):
    if kind in ("gather", "gather_chips", "sibling"):
        return land_shape[axis] // N_DEV
    return src_shape[axis] // {"scatter": N_DEV, "chips": N_CHIPS}[kind]


def _xfer_start(kind, src, land, axis, name):
    size = _split_size(kind, src.shape, land.shape, axis)

    def body(src_ref, land_ref, send_sems, recv_sems, src_thru, land_thru, token):
        for cp in _split_copies(kind, src_ref, land_ref, send_sems, recv_sems, axis, size, False):
            cp.start()
        token[...] = jnp.zeros_like(token)

    return pl.pallas_call(
        body, name=name,
        out_shape=(pltpu.SemaphoreType.DMA((N_PEERS,)), pltpu.SemaphoreType.DMA((N_PEERS,)),
                   pltpu.HBM(src.shape, src.dtype), pltpu.HBM(land.shape, land.dtype),
                   jax.ShapeDtypeStruct((SUBLANES, LANES), F32)),
        in_specs=(HBM_SPEC, HBM_SPEC),
        out_specs=(SEM_SPEC, SEM_SPEC, HBM_SPEC, HBM_SPEC, pl.BlockSpec(memory_space=pltpu.VMEM)),
        input_output_aliases={0: 2, 1: 3},
        compiler_params=pltpu.CompilerParams(has_side_effects=_EFFECT),
    )(pltpu.with_memory_space_constraint(src, pltpu.HBM), pltpu.with_memory_space_constraint(land, pltpu.HBM))


def _xfer_wait(kind, handle, axis, after, name):
    send_sems, recv_sems, src_thru, land_thru, _ = handle
    size = _split_size(kind, src_thru.shape, land_thru.shape, axis)

    def body(src_ref, land_ref, send_sems, recv_sems, after_ref, src_dead, got_ref):
        for cp in _split_copies(kind, src_ref, land_ref, send_sems, recv_sems, axis, size, True):
            cp.wait_send()
            cp.wait_recv()

    return pl.pallas_call(
        body, name=name,
        out_shape=(pltpu.HBM(src_thru.shape, src_thru.dtype), pltpu.HBM(land_thru.shape, land_thru.dtype)),
        in_specs=(HBM_SPEC, HBM_SPEC, SEM_SPEC, SEM_SPEC, pl.BlockSpec(memory_space=pl.ANY)),
        out_specs=(HBM_SPEC, HBM_SPEC), input_output_aliases={0: 0, 1: 1},
        compiler_params=pltpu.CompilerParams(has_side_effects=_EFFECT),
    )(src_thru, land_thru, send_sems, recv_sems, after)[1]


def _place_block(shard, axis, my, name):
    r, c = shard.shape
    tr = _pick(r, max(16, (1 << 19) // c), 16)
    nr = r // tr
    if axis == 0:
        full, out_map = (r * N_DEV, c), (lambda i, my_ref: (my_ref[0] * nr + i, 0))
    else:
        full, out_map = (r, c * N_DEV), (lambda i, my_ref: (i, my_ref[0]))

    def body(my_ref, x_ref, o_ref):
        o_ref[...] = x_ref[...].astype(BF16)

    return pl.pallas_call(
        body, name=name, out_shape=jax.ShapeDtypeStruct(full, BF16),
        grid_spec=pltpu.PrefetchScalarGridSpec(
            num_scalar_prefetch=1, grid=(nr,), in_specs=[pl.BlockSpec((tr, c), lambda i, my_ref: (i, 0))],
            out_specs=pl.BlockSpec((tr, c), out_map)),
        compiler_params=_params(("arbitrary",)),
    )(jnp.reshape(my, (1,)).astype(jnp.int32), shard)


def _scatter_start(grad, axis, my, name):
    shp = list(grad.shape)
    shp[axis] //= N_DEV
    own = lax.dynamic_slice_in_dim(grad, my * shp[axis], shp[axis], axis)
    return (_xfer_start("scatter", grad, lax.empty((N_PEERS,) + tuple(shp), grad.dtype), axis, name), own, axis,
            "scatter")


def _pair_reduce(grad, my, name):
    r, cols = grad.shape
    sz = cols // N_DEV

    def exchange(g_ref, got_ref, send_sems, recv_sems):
        me = _my_pos()
        sibling = _flip(me, 1)
        cps = []
        for k in range(N_CHIPS):
            cp = pltpu.make_async_remote_copy(
                src_ref=_slab(g_ref, 1, 2 * k + sibling[2], sz), dst_ref=_slab(got_ref, 1, k, sz),
                send_sem=send_sems.at[k], recv_sem=recv_sems.at[k], device_id=sibling, device_id_type=MESH)
            cp.start()
            cps.append(cp)
        for cp in cps:
            cp.wait()

    got = pl.pallas_call(
        exchange, name=name + "_swap", out_shape=jax.ShapeDtypeStruct((r, N_CHIPS * sz), grad.dtype),
        in_specs=[HBM_SPEC], out_specs=HBM_SPEC,
        scratch_shapes=[pltpu.SemaphoreType.DMA((N_CHIPS,)), pltpu.SemaphoreType.DMA((N_CHIPS,))],
    )(grad)

    tr = _pick(r, 512, 16)

    def add(c_ref, g_ref, p_ref, o_ref):
        o_ref[...] = (g_ref[...].astype(F32) + p_ref[...].astype(F32)).astype(o_ref.dtype)

    def add(c_ref, g_ref, p_ref, o_ref):
        o_ref[0] = (g_ref[...].astype(F32) + p_ref[...].astype(F32)).astype(o_ref.dtype)

    return pl.pallas_call(
        add, name=name + "_sum", out_shape=jax.ShapeDtypeStruct((N_CHIPS, r, sz), grad.dtype),
        grid_spec=pltpu.PrefetchScalarGridSpec(
            num_scalar_prefetch=1, grid=(N_CHIPS, r // tr),
            in_specs=[pl.BlockSpec((tr, sz), lambda k, i, c_ref: (i, 2 * k + c_ref[0])),
                      pl.BlockSpec((tr, sz), lambda k, i, c_ref: (i, k))],
            out_specs=pl.BlockSpec((1, tr, sz), lambda k, i, c_ref: (k, i, 0))),
        compiler_params=_params(("parallel", "parallel")),
    )(jnp.reshape(my % 2, (1,)).astype(jnp.int32), grad, got)


def _chip_scatter_start(pair_sum, my, name):
    _, r, sz = pair_sum.shape
    own = lax.dynamic_index_in_dim(pair_sum, my // 2, 0, keepdims=False)
    return (_xfer_start("chips", pair_sum, lax.empty((N_CHIPS - 1, r, sz), pair_sum.dtype), 0, name), own, 0, "chips")


def _tok(a, *tokens):
    for t in tokens:
        a = a + t[0, 0].astype(a.dtype)
    return a


def _mm(a, b, *, name, ta=False, tb=False, out_dtype=F32, a_fn=None, out_fn=None, extras=(), bias=None,
        tm=1024, tn=1024, tk=2816):
    if ta:
        kdim, m = a.shape
    else:
        m, kdim = a.shape
    if tb:
        n, kb = b.shape
    else:
        kb, n = b.shape
    assert kb == kdim, (a.shape, b.shape, ta, tb)
    tm = _pick(m, tm, LANES if ta else SUBLANES)
    tn = _pick(n, tn, LANES)
    tk = _pick(kdim, tk, LANES)
    nk = kdim // tk
    ne = len(extras)
    has_bias = bias is not None

    def body(*refs):
        a_ref, b_ref = refs[0], refs[1]
        pos = 2
        bias_ref = None
        if has_bias:
            bias_ref = refs[pos]
            pos += 1
        extra_refs = refs[pos:pos + ne]
        o_ref = refs[pos + ne]
        av = a_ref[...]
        if a_fn is not None:
            av = a_fn(av)
        dn = (((0 if ta else 1,), (1 if tb else 0,)), ((), ()))
        part = lax.dot_general(av.astype(BF16), b_ref[...].astype(BF16), dn, preferred_element_type=F32)

        def finish(r):
            if has_bias:
                r = r + bias_ref[...]
            if out_fn is not None:
                r = out_fn(r, *[e[...] for e in extra_refs])
            o_ref[...] = r.astype(out_dtype)

        if nk == 1:
            finish(part)
            return
        acc = refs[pos + ne + 1]
        k = pl.program_id(2)

        @pl.when(k == 0)
        def _():
            acc[...] = part

        @pl.when(k > 0)
        def _():
            acc[...] += part

        @pl.when(k == nk - 1)
        def _():
            finish(acc[...])

    in_specs = [
        pl.BlockSpec((tk, tm), lambda i, j, k: (k, i)) if ta else pl.BlockSpec((tm, tk), lambda i, j, k: (i, k)),
        pl.BlockSpec((tn, tk), lambda i, j, k: (j, k)) if tb else pl.BlockSpec((tk, tn), lambda i, j, k: (k, j)),
    ]
    args = [a, b]
    if has_bias:
        in_specs.append(pl.BlockSpec((1, tn), lambda i, j, k: (0, j)))
        args.append(bias)
    for e in extras:
        in_specs.append(pl.BlockSpec((tm, tn), lambda i, j, k: (i, j)))
        args.append(e)
    return pl.pallas_call(
        body, name=name, grid=(m // tm, n // tn, nk),
        in_specs=in_specs, out_specs=pl.BlockSpec((tm, tn), lambda i, j, k: (i, j)),
        out_shape=jax.ShapeDtypeStruct((m, n), out_dtype),
        scratch_shapes=[pltpu.VMEM((tm, tn), F32)] if nk > 1 else [],
        compiler_params=_params(("parallel", "parallel", "arbitrary")),
    )(*args)


def _row(arr, cb=0, rb=0, w=None):
    return ("row", arr, cb, rb, w)


def _bc(arr, cb=0, w=None):
    return ("bc", arr, cb, 0, w)


def _full(arr):
    return ("full", arr, 0, 0, None)


def _rowwise(fn, ins, outs, *, nr, tr, nc, cw, name):
    n_in = len(ins)
    outs = [(o[0], o[1], (o[2] if len(o) > 2 and o[2] else cw)) for o in outs]

    def body(*refs):
        in_refs, out_refs = refs[:n_in], refs[n_in:]
        j, i = pl.program_id(0), pl.program_id(1)
        res = fn(i, j, *[r[...] for r in in_refs])
        if not isinstance(res, (tuple, list)):
            res = (res,)
        for (kind, _, _), o_ref, r in zip(outs, out_refs, res):
            if kind == "row":
                o_ref[...] = r.astype(o_ref.dtype)
            else:
                @pl.when(i == 0)
                def _(o_ref=o_ref):
                    o_ref[...] = jnp.zeros_like(o_ref)

                o_ref[...] += r

    in_specs, args = [], []
    for kind, arr, cb, rb, w in ins:
        w = w or cw
        if kind == "row":
            in_specs.append(pl.BlockSpec((tr, w), lambda j, i, cb=cb, rb=rb: (i + rb, j + cb)))
        elif kind == "bc":
            in_specs.append(pl.BlockSpec((arr.shape[0], w), lambda j, i, cb=cb: (0, j + cb)))
        else:
            in_specs.append(pl.BlockSpec(arr.shape, lambda j, i, nd=arr.ndim: (0,) * nd))
        args.append(arr)
    out_specs, out_shape = [], []
    for kind, spec, w in outs:
        if kind == "row":
            out_specs.append(pl.BlockSpec((tr, w), lambda j, i: (i, j)))
            out_shape.append(jax.ShapeDtypeStruct((nr * tr, nc * w), spec))
        else:
            out_specs.append(pl.BlockSpec((spec, w), lambda j, i: (0, j)))
            out_shape.append(jax.ShapeDtypeStruct((spec, nc * w), F32))
    res = pl.pallas_call(
        body, name=name, grid=(nc, nr), in_specs=in_specs, out_specs=tuple(out_specs), out_shape=tuple(out_shape),
        compiler_params=_params(("parallel", "arbitrary")),
    )(*args)
    return res


def _sum_parts(own, parts, *, name):
    p, r, c = parts.shape
    tr = _pick(r, max(SUBLANES, (1 << 18) // c), SUBLANES)

    def body(o_ref, p_ref, g_out):
        g = o_ref[...].astype(F32)
        for q in range(p):
            g = g + p_ref[q].astype(F32)
        g_out[...] = g

    blk = pl.BlockSpec((tr, c), lambda i: (i, 0))
    return pl.pallas_call(
        body, name=name, grid=(r // tr,), in_specs=[blk, pl.BlockSpec((p, tr, c), lambda i: (0, i, 0))],
        out_specs=blk, out_shape=jax.ShapeDtypeStruct((r, c), F32), compiler_params=_params(("parallel",)),
    )(own, parts)


def _adamw(parts, w, m, v, *, name, own=None):
    p, r, c = parts.shape
    tr = _pick(r, max(SUBLANES, (1 << 18) // c), 16 if parts.dtype == BF16 else SUBLANES)
    bc1 = 1.0 - ADAM_B1 ** ADAM_STEP
    bc2 = 1.0 - ADAM_B2 ** ADAM_STEP
    has_own = own is not None

    def body(*refs):
        p_ref, w_ref, m_ref, v_ref = refs[:4]
        g_out, d_out, m_out, v_out = refs[4 + has_own:]
        if has_own:
            g = refs[4][...].astype(F32) + p_ref[0].astype(F32)
        else:
            g = p_ref[0].astype(F32)
        for q in range(1, p):
            g = g + p_ref[q].astype(F32)
        m_new = ADAM_B1 * m_ref[...] + (1.0 - ADAM_B1) * g
        v_new = ADAM_B2 * v_ref[...] + (1.0 - ADAM_B2) * (g * g)
        m_hat = m_new / bc1
        v_hat = v_new / bc2
        g_out[...] = g
        d_out[...] = -ADAM_LR * (m_hat / (jnp.sqrt(v_hat) + ADAM_EPS) + ADAM_WD * w_ref[...])
        m_out[...] = m_new
        v_out[...] = v_new

    blk = pl.BlockSpec((tr, c), lambda i: (i, 0))
    return pl.pallas_call(
        body, name=name, grid=(r // tr,),
        in_specs=[pl.BlockSpec((p, tr, c), lambda i: (0, i, 0)), blk, blk, blk] + ([blk] if has_own else []),
        out_specs=(blk, blk, blk, blk), out_shape=tuple(jax.ShapeDtypeStruct((r, c), F32) for _ in range(4)),
        compiler_params=_params(("parallel",)),
    )(*([parts, w, m, v] + ([own] if has_own else [])))


_GELU_C = math.sqrt(2.0 / math.pi)
_GELU_K = 0.044715


def _sigmoid(z):
    return 0.5 * jnp.tanh(0.5 * z) + 0.5


def _silu(z):
    return z * _sigmoid(z)


def _dsilu(z):
    s = _sigmoid(z)
    return s * (1.0 + z * (1.0 - s))


def _gelu(z):
    return 0.5 * z * (1.0 + jnp.tanh(_GELU_C * (z + _GELU_K * z * z * z)))


def _gelu_pair(z):
    zz = z * z
    t = jnp.tanh(_GELU_C * (z * (1.0 + _GELU_K * zz)))
    half = 0.5 * (1.0 + t)
    return z * half, half + 0.5 * z * (1.0 - t * t) * (_GELU_C * (1.0 + 3.0 * _GELU_K * zz))


def _softplus(z):
    return jnp.maximum(z, 0.0) + jnp.log(1.0 + jnp.exp(-jnp.abs(z)))


def _rms(xv):
    return lax.rsqrt(jnp.mean(xv * xv, axis=-1, keepdims=True) + NORM_EPS)


def _rms_bwd(xhat, r, dxhat):
    return r * (dxhat - xhat * jnp.mean(dxhat * xhat, axis=-1, keepdims=True))


def _swap_pairs(y):
    lane = lax.broadcasted_iota(jnp.int32, y.shape, 1)
    return jnp.where(lane % 2 == 0, pltpu.roll(y, LANES - 1, 1), pltpu.roll(y, 1, 1))


def _dot(a, b, ca, cb):
    return lax.dot_general(a, b, (((ca,), (cb,)), ((), ())), preferred_element_type=F32)


_SM_SCALE = HEAD_DIM ** -0.5
_EXP2_SCALE = _SM_SCALE * math.log2(math.e)


def _softmax_parts(qh, kk):
    s = _dot(qh, kk, 1, 1)
    p = jnp.exp2((s - jnp.max(s, axis=-1, keepdims=True)) * _EXP2_SCALE)
    return p, 1.0 / jnp.sum(p, axis=-1, keepdims=True)


def _attn_fwd(q, k, v, *, tq, name):
    s_len, aw = q.shape
    t_len, kvw = k.shape
    gw = Q_GROUP * HEAD_DIM

    def body(q_ref, k_ref, v_ref, o_ref):
        kk, vv = k_ref[...], v_ref[...]
        for g in range(Q_GROUP):
            cols = slice(g * HEAD_DIM, (g + 1) * HEAD_DIM)
            p, inv_l = _softmax_parts(q_ref[:, cols], kk)
            o_ref[:, cols] = (_dot(p.astype(BF16), vv, 1, 0) * inv_l).astype(BF16)

    return pl.pallas_call(
        body, name=name, grid=(kvw // HEAD_DIM, s_len // tq),
        in_specs=[pl.BlockSpec((tq, gw), lambda h, i: (i, h)), pl.BlockSpec((t_len, HEAD_DIM), lambda h, i: (0, h)),
                  pl.BlockSpec((t_len, HEAD_DIM), lambda h, i: (0, h))],
        out_specs=pl.BlockSpec((tq, gw), lambda h, i: (i, h)),
        out_shape=jax.ShapeDtypeStruct((s_len, aw), BF16),
        compiler_params=_params(("parallel", "parallel")),
    )(q, k, v)


def _attn_bwd(q, k, v, do, *, tq, name):
    s_len, aw = q.shape
    t_len, kvw = k.shape
    gw = Q_GROUP * HEAD_DIM

    def body(q_ref, k_ref, v_ref, do_ref, dq_ref, dk_ref, dv_ref):
        @pl.when(pl.program_id(1) == 0)
        def _():
            dk_ref[...] = jnp.zeros_like(dk_ref)
            dv_ref[...] = jnp.zeros_like(dv_ref)

        kk, vv = k_ref[...], v_ref[...]
        for g in range(Q_GROUP):
            cols = slice(g * HEAD_DIM, (g + 1) * HEAD_DIM)
            qh, doh = q_ref[:, cols], do_ref[:, cols]
            p, inv_l = _softmax_parts(qh, kk)
            dp = _dot(doh, vv, 1, 1)
            delta = jnp.sum(p * dp, axis=-1, keepdims=True) * inv_l
            dsb = (p * ((dp - delta) * (inv_l * _SM_SCALE))).astype(BF16)
            dq_ref[:, cols] = _dot(dsb, kk, 1, 0)
            dk_ref[...] += _dot(dsb, qh, 0, 0)
            dv_ref[...] += _dot(p.astype(BF16), (doh * inv_l).astype(BF16), 0, 0)

    qspec = pl.BlockSpec((tq, gw), lambda h, i: (i, h))
    kspec = pl.BlockSpec((t_len, HEAD_DIM), lambda h, i: (0, h))
    return pl.pallas_call(
        body, name=name, grid=(kvw // HEAD_DIM, s_len // tq),
        in_specs=[qspec, kspec, kspec, qspec], out_specs=(qspec, kspec, kspec),
        out_shape=(jax.ShapeDtypeStruct((s_len, aw), F32), jax.ShapeDtypeStruct((t_len, kvw), F32),
                   jax.ShapeDtypeStruct((t_len, kvw), F32)),
        compiler_params=_params(("parallel", "arbitrary")),
    )(q, k, v, do)


SHIFTS = (-2, -1, 1, 2)


def _shift_masks(l_len, t_len):
    t = jnp.arange(t_len)[:, None]
    cols = []
    for d in SHIFTS:
        td = t + d
        ok = ((t < l_len) & (td >= 0) & (td < l_len)) | ((t >= l_len) & (td >= l_len) & (td < t_len))
        cols.append(jnp.broadcast_to(ok.astype(F32), (t_len, RNN_BLOCK)))
    return jnp.concatenate(cols, axis=1)


def _shift_rows(xv, d, m_ref, l_len, t_len):
    if d == 0:
        return xv
    k = SHIFTS.index(d)
    return pltpu.roll(xv, (-d) % t_len, 0) * m_ref[:, k * RNN_BLOCK:(k + 1) * RNN_BLOCK]


def _sublane_scan(a8, b8, rev):
    row = lax.broadcasted_iota(jnp.int32, a8.shape, 0)
    for s in (1, 2, 4):
        if rev:
            ok = row < SUBLANES - s
            sh = SUBLANES - s
        else:
            ok = row >= s
            sh = s
        a_sh = jnp.where(ok, pltpu.roll(a8, sh, 0), 1.0)
        b_sh = jnp.where(ok, pltpu.roll(b8, sh, 0), 0.0)
        b8 = a8 * b_sh + b8
        a8 = a8 * a_sh
    return a8, b8


def _scan_chunks(chains, c0, n, carries):
    def step(kk, hs):
        new = []
        for (a_ref, b_ref, h_ref, rev), hprev in zip(chains, hs):
            c = c0 + (n - 1 - kk if rev else kk)
            r0 = pl.multiple_of(c * SUBLANES, SUBLANES)
            big_a, big_b = _sublane_scan(a_ref[pl.ds(r0, SUBLANES), :], b_ref[pl.ds(r0, SUBLANES), :], rev)
            h8 = big_b + big_a * hprev
            h_ref[pl.ds(r0, SUBLANES), :] = h8
            edge = 0 if rev else SUBLANES - 1
            new.append(jnp.broadcast_to(h8[edge:edge + 1, :], h8.shape))
        return tuple(new)

    return lax.fori_loop(0, n, step, tuple(carries))


def _rglru_conv(x, cw_ref, cb_ref, m_ref, l_len, t_len):
    y = cb_ref[...] + _shift_rows(x, -1, m_ref, l_len, t_len) * cw_ref[0:1, :]
    for kk in range(1, CONV_WIDTH):
        y = y + _shift_rows(x, kk - 1, m_ref, l_len, t_len) * cw_ref[kk:kk + 1, :]
    return y


def _rglru_gates(xc, wr, wi, br, bi, lam):
    xb = xc.astype(BF16)
    r = _sigmoid(_dot(xb, wr.astype(BF16), 1, 0) + br)
    ig = _sigmoid(_dot(xb, wi.astype(BF16), 1, 0) + bi)
    sp = _softplus(-lam)
    log_a = (-LRU_C) * r * sp
    a = jnp.exp(log_a)
    z2 = 2.0 * log_a
    series = z2 * (1.0 + z2 * (0.5 + z2 * (1.0 / 6.0 + z2 * (1.0 / 24.0))))
    one_m = jnp.where(jnp.abs(z2) < 0.03, -series, 1.0 - a * a)
    rinv = lax.rsqrt(jnp.maximum(one_m, 1e-30))
    u = ig * xc
    return r, ig, sp, a, one_m * rinv, u, rinv


def _rglru_specs(proj, xr_blk, xg_blk, t_len):
    col = lambda off: pl.BlockSpec((t_len, RNN_BLOCK), lambda h, off=off: (0, off + h))
    vec = lambda rows: pl.BlockSpec((rows, RNN_BLOCK), lambda h: (0, h))
    wsp = pl.BlockSpec((2, 1, RNN_BLOCK, RNN_BLOCK), lambda h: (0, h, 0, 0))
    msp = pl.BlockSpec((t_len, len(SHIFTS) * RNN_BLOCK), lambda h: (0, 0))
    return col(xr_blk), col(xg_blk), vec, wsp, msp


def _rglru_fwd(proj, xr_blk, xg_blk, masks, conv_w, conv_b, w_rg, w_ig, b_rg, b_ig, lam, *, l_len, name):
    t_len = proj.shape[0]
    s_len = t_len - l_len
    d_rnn = conv_b.shape[1]
    n_l, n_t = l_len // SUBLANES, t_len // SUBLANES

    def body(xr_ref, xg_ref, m_ref, cw_ref, cb_ref, wr_ref, wi_ref, br_ref, bi_ref, lam_ref, rg_ref, hf_ref, hb_ref,
             a_s, b_s):
        xc = _rglru_conv(xr_ref[...], cw_ref, cb_ref, m_ref, l_len, t_len)
        for d in range(2):
            _, _, _, a, mult, u, _ = _rglru_gates(xc, wr_ref[d, 0], wi_ref[d, 0], br_ref[d:d + 1, :],
                                               bi_ref[d:d + 1, :], lam_ref[d:d + 1, :])
            a_s[d] = a
            b_s[d] = mult * u
        chains = [(a_s.at[0], b_s.at[0], hf_ref, False), (a_s.at[1], b_s.at[1], hb_ref, True)]
        zero = jnp.zeros((SUBLANES, RNN_BLOCK), F32)
        carry = _scan_chunks(chains, 0, n_l, (zero, zero))
        _scan_chunks(chains, n_l, n_t - n_l, carry)
        rnn = hf_ref[pl.ds(l_len, s_len), :] + hb_ref[pl.ds(l_len, s_len), :]
        rg_ref[...] = (rnn * _gelu(xg_ref[pl.ds(l_len, s_len), :])).astype(BF16)

    xr_spec, xg_spec, vec, wsp, msp = _rglru_specs(proj, xr_blk, xg_blk, t_len)
    col_t = pl.BlockSpec((t_len, RNN_BLOCK), lambda h: (0, h))
    return pl.pallas_call(
        body, name=name, grid=(d_rnn // RNN_BLOCK,),
        in_specs=[xr_spec, xg_spec, msp, vec(CONV_WIDTH), vec(1), wsp, wsp, vec(2), vec(2), vec(2)],
        out_specs=(pl.BlockSpec((s_len, RNN_BLOCK), lambda h: (0, h)), col_t, col_t),
        out_shape=(jax.ShapeDtypeStruct((s_len, d_rnn), BF16), jax.ShapeDtypeStruct((t_len, d_rnn), F32),
                   jax.ShapeDtypeStruct((t_len, d_rnn), F32)),
        scratch_shapes=[pltpu.VMEM((2, t_len, RNN_BLOCK), F32)] * 2,
        compiler_params=_params(("parallel",)),
    )(proj, proj, masks, conv_w, conv_b, w_rg, w_ig, b_rg, b_ig, lam)


N_RNN_VEC = 16


def _rglru_bwd(proj, xr_blk, xg_blk, masks, h_fwd, h_bwd, drg, conv_w, conv_b, w_rg, w_ig, b_rg, b_ig, lam, *, l_len, name):
    t_len = proj.shape[0]
    s_len = t_len - l_len
    d_rnn = conv_b.shape[1]
    n_l, n_t = l_len // SUBLANES, t_len // SUBLANES

    def body(xr_ref, xg_ref, m_ref, hf_ref, hb_ref, drg_ref, cw_ref, cb_ref, wr_ref, wi_ref, br_ref, bi_ref, lam_ref,
             dxr_ref, dxg_ref, dvec_ref, dwr_ref, dwi_ref, a_s, b_s, h_s):
        t = lax.broadcasted_iota(jnp.int32, (t_len, RNN_BLOCK), 0)
        x = xr_ref[...]
        xc = _rglru_conv(x, cw_ref, cb_ref, m_ref, l_len, t_len)
        xb = xc.astype(BF16)
        gates = []
        for d in range(2):
            gt = _rglru_gates(xc, wr_ref[d, 0], wi_ref[d, 0], br_ref[d:d + 1, :], bi_ref[d:d + 1, :],
                              lam_ref[d:d + 1, :])
            gates.append(gt)
        zero = jnp.zeros((SUBLANES, RNN_BLOCK), F32)
        h0, h1 = hf_ref[...], hb_ref[...]
        xg = xg_ref[...]
        dr = jnp.concatenate([jnp.zeros((l_len, RNN_BLOCK), F32), drg_ref[...].astype(F32)], axis=0)
        gel, dgel = _gelu_pair(xg)
        dxg_ref[...] = (dr * (h0 + h1) * dgel).astype(dxg_ref.dtype)
        drnn = dr * gel
        a_s[0] = pltpu.roll(gates[0][3], t_len - 1, 0)
        a_s[1] = pltpu.roll(gates[1][3], 1, 0)
        b_s[0] = drnn
        b_s[1] = drnn
        hprev = [jnp.where(t == 0, 0.0, pltpu.roll(h0, 1, 0)),
                 jnp.where(t == l_len - 1, 0.0, pltpu.roll(h1, t_len - 1, 0))]
        chains = [(a_s.at[0], b_s.at[0], h_s.at[0], True), (a_s.at[1], b_s.at[1], h_s.at[1], False)]
        carry = _scan_chunks(chains, n_l, n_t - n_l, (zero, zero))
        _scan_chunks(chains, 0, n_l, carry)

        dxc = jnp.zeros((t_len, RNN_BLOCK), F32)
        for d in range(2):
            r, ig, sp, a, mult, u, rinv = gates[d]
            lamb = h_s[d]
            da = lamb * hprev[d]
            dmult = lamb * u
            du = lamb * mult
            dlog_a = da * a - dmult * (a * a) * rinv
            dzr = (dlog_a * ((-LRU_C) * sp)) * r * (1.0 - r)
            dsp = jnp.sum(dlog_a * ((-LRU_C) * r), axis=0, keepdims=True)
            dzi = (du * xc) * ig * (1.0 - ig)
            dzrb, dzib = dzr.astype(BF16), dzi.astype(BF16)
            dxc = dxc + du * ig + _dot(dzrb, wr_ref[d, 0].astype(BF16), 1, 1) + _dot(dzib, wi_ref[d, 0].astype(BF16), 1, 1)
            dwr_ref[d, 0] = _dot(xb, dzrb, 0, 0)
            dwi_ref[d, 0] = _dot(xb, dzib, 0, 0)
            dvec_ref[5 + d:6 + d, :] = jnp.sum(dzr, axis=0, keepdims=True)
            dvec_ref[7 + d:8 + d, :] = jnp.sum(dzi, axis=0, keepdims=True)
            dvec_ref[9 + d:10 + d, :] = dsp * (-_sigmoid(-lam_ref[d:d + 1, :]))
        dvec_ref[4:5, :] = jnp.sum(dxc, axis=0, keepdims=True)
        dvec_ref[11:N_RNN_VEC, :] = jnp.zeros((N_RNN_VEC - 11, RNN_BLOCK), F32)
        dxr = jnp.zeros((t_len, RNN_BLOCK), F32)
        for kk in range(CONV_WIDTH):
            dvec_ref[kk:kk + 1, :] = jnp.sum(dxc * _shift_rows(x, kk - 1, m_ref, l_len, t_len), axis=0, keepdims=True)
            dxr = dxr + _shift_rows(dxc, 1 - kk, m_ref, l_len, t_len) * cw_ref[kk:kk + 1, :]
        dxr_ref[...] = dxr.astype(dxr_ref.dtype)

    xr_spec, xg_spec, vec, wsp, msp = _rglru_specs(proj, xr_blk, xg_blk, t_len)
    col_t = pl.BlockSpec((t_len, RNN_BLOCK), lambda h: (0, h))
    return pl.pallas_call(
        body, name=name, grid=(d_rnn // RNN_BLOCK,),
        in_specs=[xr_spec, xg_spec, msp, col_t, col_t, pl.BlockSpec((s_len, RNN_BLOCK), lambda h: (0, h)), vec(CONV_WIDTH),
                  vec(1),
                  wsp, wsp, vec(2), vec(2), vec(2)],
        out_specs=(col_t, col_t, vec(N_RNN_VEC), wsp, wsp),
        out_shape=(jax.ShapeDtypeStruct((t_len, d_rnn), BF16), jax.ShapeDtypeStruct((t_len, d_rnn), BF16),
                   jax.ShapeDtypeStruct((N_RNN_VEC, d_rnn), F32), jax.ShapeDtypeStruct(w_rg.shape, F32),
                   jax.ShapeDtypeStruct(w_ig.shape, F32)),
        scratch_shapes=[pltpu.VMEM((2, t_len, RNN_BLOCK), F32)] * 3,
        compiler_params=_params(("parallel",)),
    )(proj, proj, masks, h_fwd, h_bwd, drg, conv_w, conv_b, w_rg, w_ig, b_rg, b_ig, lam)


def _colsum(v):
    return jnp.sum(v, axis=0, keepdims=True)


def _qk_fwd_heads(xv, gain, cos_e, sin_e):
    outs = []
    for h in range(xv.shape[1] // HEAD_DIM):
        xh = xv[:, h * HEAD_DIM:(h + 1) * HEAD_DIM]
        y = xh * _rms(xh) * gain
        outs.append(y * cos_e + _swap_pairs(y) * sin_e)
    return outs[0] if len(outs) == 1 else jnp.concatenate(outs, axis=1)


def _qk_bwd_heads(xv, dov, gain, cos_e, sin_e):
    outs = []
    dgain = jnp.zeros((1, HEAD_DIM), F32)
    for h in range(xv.shape[1] // HEAD_DIM):
        cols = slice(h * HEAD_DIM, (h + 1) * HEAD_DIM)
        xh, doh = xv[:, cols], dov[:, cols]
        r = _rms(xh)
        xhat = xh * r
        dy = doh * cos_e + _swap_pairs(doh * sin_e)
        dgain = dgain + _colsum(dy * xhat)
        outs.append(_rms_bwd(xhat, r, dy * gain))
    return (outs[0] if len(outs) == 1 else jnp.concatenate(outs, axis=1)), dgain


WEIGHTS = ['c_ctx', 'w_mod', 'b_mod', 'g_mix', 'g_mlp', 'w_in', 'q_gain', 'k_gain', 'conv_w', 'conv_b', 'w_rg', 'b_rg',
           'w_ig', 'b_ig', 'lru_lambda', 'w_o_attn', 'w_o_rnn', 'w_out', 'w_up', 'w_down', 'g_final']
REPLICATED = ['c_ctx', 'b_mod', 'g_mix', 'g_mlp', 'q_gain', 'k_gain', 'conv_b', 'w_rg', 'w_ig', 'g_final']
SMALL_SHARDED = [('conv_w', CONV_WIDTH), ('b_rg', 2), ('b_ig', 2), ('lru_lambda', 2)]
SMALL_ROWS = 16


PACK_ROWS = 512


def _pack_flat(arrs, rows_mult):
    flat = jnp.concatenate([a.reshape(-1) for a in arrs])
    n = flat.shape[0]
    unit = rows_mult * LANES
    pad = (-n) % unit
    if pad:
        flat = jnp.concatenate([flat, jnp.zeros((pad,), flat.dtype)])
    return flat.reshape(-1, LANES)


def _unpack_flat(packed, shapes):
    flat = packed.reshape(-1)
    out, pos = [], 0
    for shp in shapes:
        n = math.prod(shp)
        out.append(flat[pos:pos + n].reshape(shp))
        pos += n
    return out


def _pack_small(d):
    rows = [d[n][0] for n, _ in SMALL_SHARDED]
    cols = rows[0].shape[1]
    used = sum(k for _, k in SMALL_SHARDED)
    return jnp.concatenate(rows + [jnp.zeros((SMALL_ROWS - used, cols), F32)], axis=0)


def _unpack_small(p):
    out, pos = {}, 0
    for n, k in SMALL_SHARDED:
        out[n] = p[pos:pos + k][None]
        pos += k
    return out


def kernel(x, c, ctx, c_ctx, w_mod, b_mod, g_mix, g_mlp, w_in, q_gain, k_gain, conv_w, conv_b, w_rg, b_rg, w_ig, b_ig, lru_lambda, w_o_attn, w_o_rnn, w_out, w_up, w_down, g_final, loss_target, m_c_ctx, m_w_mod, m_b_mod, m_g_mix, m_g_mlp, m_w_in, m_q_gain, m_k_gain, m_conv_w, m_conv_b, m_w_rg, m_b_rg, m_w_ig, m_b_ig, m_lru_lambda, m_w_o_attn, m_w_o_rnn, m_w_out, m_w_up, m_w_down, m_g_final, v_c_ctx, v_w_mod, v_b_mod, v_g_mix, v_g_mlp, v_w_in, v_q_gain, v_k_gain, v_conv_w, v_conv_b, v_w_rg, v_b_rg, v_w_ig, v_b_ig, v_lru_lambda, v_w_o_attn, v_w_o_rnn, v_w_out, v_w_up, v_w_down, v_g_final):
    wd = dict(c_ctx=c_ctx, w_mod=w_mod, b_mod=b_mod, g_mix=g_mix, g_mlp=g_mlp, w_in=w_in, q_gain=q_gain, k_gain=k_gain,
              conv_w=conv_w, conv_b=conv_b, w_rg=w_rg, b_rg=b_rg, w_ig=w_ig, b_ig=b_ig, lru_lambda=lru_lambda,
              w_o_attn=w_o_attn, w_o_rnn=w_o_rnn, w_out=w_out, w_up=w_up, w_down=w_down, g_final=g_final)
    md = dict(zip(WEIGHTS, (m_c_ctx, m_w_mod, m_b_mod, m_g_mix, m_g_mlp, m_w_in, m_q_gain, m_k_gain, m_conv_w, m_conv_b,
                            m_w_rg, m_b_rg, m_w_ig, m_b_ig, m_lru_lambda, m_w_o_attn, m_w_o_rnn, m_w_out, m_w_up,
                            m_w_down, m_g_final)))
    vd = dict(zip(WEIGHTS, (v_c_ctx, v_w_mod, v_b_mod, v_g_mix, v_g_mlp, v_w_in, v_q_gain, v_k_gain, v_conv_w, v_conv_b,
                            v_w_rg, v_b_rg, v_w_ig, v_b_ig, v_lru_lambda, v_w_o_attn, v_w_o_rnn, v_w_out, v_w_up,
                            v_w_down, v_g_final)))

    xs, tgt, ctxs = x[0], loss_target[0], ctx[0]
    s_len, d = xs.shape
    l_len = ctxs.shape[0]
    t_len = s_len + l_len
    n_in = w_in.shape[2] * N_DEV
    aw = (n_in - 4 * d) * 2 // 3
    kvw = aw // Q_GROUP
    assert aw == d and n_in == aw + 2 * kvw + 4 * d
    nmod = N_MOD * d // N_DEV
    tr = 128
    assert l_len % tr == 0 and s_len % tr == 0
    lb = l_len // tr
    tq = min(256, s_len)
    my = _lin(_my_pos())
    c_ctx2 = c_ctx[None]
    g_final2 = g_final[None]

    c_all = _all_gather([jnp.broadcast_to(c, (SUBLANES, d))], [0], "ag_cond")[0][::SUBLANES]
    c16 = jnp.concatenate([c_all, jnp.broadcast_to(c_ctx2, (N_DEV, d))], axis=0)
    b_mod_sh = lax.dynamic_slice(b_mod, (0, my * nmod), (1, nmod))
    mod_part = _mm(c16, w_mod[0], a_fn=_silu, bias=b_mod_sh, name="mod_fwd")
    sp_w = _pack_small(wd)
    (mod_g, sp_f), order = _all_gather([mod_part, sp_w], [1, 1], "ag_mod", with_token=True)
    conv_w_f, b_rg_f, b_ig_f, lam_f = sp_f[0:4], sp_f[4:6], sp_f[6:8], sp_f[8:10]
    mod_lat = lax.dynamic_slice(mod_g, (my, 0), (1, N_MOD * d))
    sh_a, sc_a, ga_a, sh_f, sc_f, ga_f = [mod_lat[:, k * d:(k + 1) * d] for k in range(N_MOD)]
    csh_a, csc_a = mod_g[N_DEV:N_DEV + 1, :d], mod_g[N_DEV:N_DEV + 1, d:2 * d]

    x_all = jnp.concatenate([ctxs, xs], axis=0)

    def nm1_fwd(i, j, xv, g, csh, csc, sh, sc):
        is_ctx = i < lb
        n = xv * _rms(xv) * g
        return n * (1.0 + jnp.where(is_ctx, csc, sc)) + jnp.where(is_ctx, csh, sh)

    h_all = _rowwise(nm1_fwd, [_row(x_all), _bc(g_mix), _bc(csh_a), _bc(csc_a), _bc(sh_a), _bc(sc_a)],
                     [("row", BF16)], nr=t_len // tr, tr=tr, nc=1, cw=d, name="norm1_fwd")[0]
    w_in_f, proj, order = _gather_proj(h_all, _tok(w_in[0], order).astype(BF16), "ag_proj")
    late_axis = dict(w_o_attn=0, w_o_rnn=0, w_out=0, w_up=1, w_down=0)
    placed = [_place_block(wd[n][0], ax, my, "place_" + n) for n, ax in late_axis.items()]
    late_kind = lambda n: "gather_chips" if n in ("w_up", "w_down") else "gather"
    late = {}
    for (n, ax), full in zip(late_axis.items(), placed):
        late[n] = _xfer_start(late_kind(n), order, full, ax, "ag_start_" + n)
        order = late[n][4]
    q_gain_fwd = _tok(q_gain, order)

    late_wait = lambda n, after: _xfer_wait(late_kind(n), late[n], late_axis[n], after, "ag_wait_" + n)

    def sibling_start(n, after):
        arrived = late_wait(n, after)
        return _xfer_start("sibling", jnp.zeros((SUBLANES, LANES), F32), arrived, late_axis[n], "ag_sib_start_" + n)

    sibling_wait = lambda n, handle, after: _xfer_wait("sibling", handle, late_axis[n], after, "ag_sib_wait_" + n)

    pos = jnp.arange(s_len)
    n_freq = HEAD_DIM // 4
    inv_freq = ROPE_THETA ** (-jnp.arange(n_freq, dtype=F32) / n_freq)
    ang = jnp.concatenate([(pos // GRID_W).astype(F32)[:, None] * inv_freq,
                           (pos % GRID_W).astype(F32)[:, None] * inv_freq], axis=-1)
    sign = jnp.where(jnp.arange(HEAD_DIM) % 2 == 0, -1.0, 1.0).astype(F32)
    cos_e = jnp.concatenate([jnp.ones((l_len, HEAD_DIM), F32), jnp.repeat(jnp.cos(ang), 2, axis=-1)], axis=0)
    sin_e = jnp.concatenate([jnp.zeros((l_len, HEAD_DIM), F32), jnp.repeat(jnp.sin(ang), 2, axis=-1) * sign], axis=0)


    kb = aw // kvw
    q = _rowwise(lambda i, j, xv, gain, ce, se: _qk_fwd_heads(xv, gain, ce, se),
                 [_row(proj, 0, lb, aw), _full(q_gain_fwd), _row(cos_e, 0, lb, HEAD_DIM), _row(sin_e, 0, lb, HEAD_DIM)],
                 [("row", BF16, aw)], nr=s_len // tr, tr=tr, nc=1, cw=aw, name="q_fwd")[0]
    k_all, v_all = _rowwise(lambda i, j, kv_, vv_, gain, ce, se: (_qk_fwd_heads(kv_, gain, ce, se), vv_),
                            [_row(proj, kb, 0, kvw), _row(proj, kb + 1, 0, kvw), _full(k_gain),
                             _row(cos_e, 0, 0, HEAD_DIM), _row(sin_e, 0, 0, HEAD_DIM)],
                            [("row", BF16, kvw), ("row", BF16, kvw)], nr=t_len // tr, tr=tr, nc=1, cw=kvw, name="kv_fwd")
    attn_o = _attn_fwd(q, k_all, v_all, tq=tq, name="attn_fwd")

    xr_blk = (aw + 2 * kvw) // RNN_BLOCK
    xg_blk = xr_blk + d // RNN_BLOCK
    shift_masks = _shift_masks(l_len, t_len)
    rg, h_fwd, h_bwd = _rglru_fwd(proj, xr_blk, xg_blk, shift_masks, conv_w_f, conv_b, w_rg[0], w_ig[0], b_rg_f, b_ig_f, lam_f, l_len=l_len,
                    name="rglru_fwd")

    w_oa_f = late_wait("w_o_attn", attn_o)
    y_attn = _mm(attn_o, w_oa_f, name="o_attn_fwd")
    w_or_f = late_wait("w_o_rnn", rg)
    y_rnn = _mm(rg, w_or_f, name="o_rnn_fwd")
    hw = d // 2
    glb = (aw + 2 * kvw + 2 * d) // hw

    def merge_fwd(i, j, ya, yr, gla, glr):
        return _sigmoid(gla) * ya + _sigmoid(glr) * yr

    u = _rowwise(merge_fwd, [_row(y_attn), _row(y_rnn), _row(proj, glb, lb), _row(proj, glb + 2, lb)],
                 [("row", BF16)], nr=s_len // tr, tr=tr, nc=2, cw=hw, name="merge_fwd")[0]
    w_out_f = late_wait("w_out", u)
    sib_up = sibling_start("w_up", u)
    mix = _mm(u, w_out_f, bias=_tok(jnp.zeros((1, d), F32), sib_up[4]), name="out_fwd")

    def res1_fwd(i, j, xv, mixv, gaa, gm, shf, scf):
        x1v = xv + gaa * mixv
        n = x1v * _rms(x1v) * gm
        return x1v, n * (1.0 + scf) + shf

    x1, h2 = _rowwise(res1_fwd, [_row(xs), _row(mix), _bc(ga_a), _bc(g_mlp), _bc(sh_f), _bc(sc_f)],
                      [("row", F32), ("row", BF16)], nr=s_len // tr, tr=tr, nc=1, cw=d, name="norm2_fwd")
    relu2 = lambda z: jnp.square(jnp.maximum(z, 0.0))
    sib_down = sibling_start("w_down", h2)
    w_up_f = sibling_wait("w_up", sib_up, h2)
    up = _mm(h2, w_up_f, bias=_tok(jnp.zeros((1, w_up.shape[2] * N_DEV), F32), sib_down[4]), name="up_fwd")
    w_down_f = sibling_wait("w_down", sib_down, up)
    down = _mm(up, w_down_f, a_fn=relu2, name="down_fwd")

    def final_fn(i, j, x1v, dn, tg, gaf, gf):
        x2 = x1v + gaf * dn
        r = _rms(x2)
        xh = x2 * r
        e = xh * gf - tg
        dy = e * (1.0 / d)
        dx2 = _rms_bwd(xh, r, dy * gf)
        return dx2, dx2 * gaf, _colsum(e * e) * (0.5 / d), _colsum(dy * xh), _colsum(dx2 * dn)

    dx2, ddown, loss_cols, g_gfinal, d_ga_f = _rowwise(
        final_fn, [_row(x1), _row(down), _row(tgt), _bc(ga_f), _bc(g_final2)],
        [("row", F32), ("row", BF16), ("acc", 1), ("acc", 1), ("acc", 1)], nr=s_len // tr, tr=tr, nc=1, cw=d,
        name="final_bwd")
    loss = lax.psum(jnp.sum(loss_cols), ("x", "y", "c"))

    dup = _mm(ddown, w_down_f, tb=True, out_fn=lambda r, upv: r * (2.0 * jnp.maximum(upv, 0.0)), extras=[up],
              out_dtype=BF16, name="down_bwd_x")
    gw_down = _mm(up, ddown, ta=True, a_fn=relu2, out_dtype=BF16, name="down_bwd_w")
    sent = dict(w_down=_scatter_start(gw_down, 0, my, "rs_start_w_down"))
    dh2 = _mm(dup, w_up_f, tb=True, name="up_bwd_x")
    gw_up = _mm(h2, dup, ta=True, out_dtype=BF16, name="up_bwd_w")
    sent["w_up"] = _scatter_start(gw_up, 1, my, "rs_start_w_up")
    sc_f_bwd = _tok(sc_f, sent["w_down"][0][4], sent["w_up"][0][4])

    def norm2_bwd(i, j, x1v, mixv, dh2v, dx2v, gaa, gm, scf):
        r = _rms(x1v)
        xh = x1v * r
        n = xh * gm
        dn = dh2v * (1.0 + scf)
        dx1 = dx2v + _rms_bwd(xh, r, dn * gm)
        return dx1, dx1 * gaa, _colsum(dh2v), _colsum(dh2v * n), _colsum(dn * xh), _colsum(dx1 * mixv)

    dx1, dmix, d_sh_f, d_sc_f, g_gmlp, d_ga_a = _rowwise(
        norm2_bwd, [_row(x1), _row(mix), _row(dh2), _row(dx2), _bc(ga_a), _bc(g_mlp), _bc(sc_f_bwd)],
        [("row", F32), ("row", BF16), ("acc", 1), ("acc", 1), ("acc", 1), ("acc", 1)], nr=s_len // tr, tr=tr, nc=1,
        cw=d, name="norm2_bwd")

    du = _mm(dmix, w_out_f, tb=True, name="out_bwd_x")
    gw_out = _mm(u, dmix, ta=True, out_dtype=BF16, name="out_bwd_w")
    sent["w_out"] = _scatter_start(gw_out, 0, my, "rs_start_w_out")

    def merge_bwd(i, j, duv, ya, yr, gla, glr):
        ga, gr = _sigmoid(gla), _sigmoid(glr)
        return duv * ga, duv * gr, duv * ya * ga * (1.0 - ga), duv * yr * gr * (1.0 - gr)

    dya, dyr, dgla, dglr = _rowwise(
        merge_bwd, [_row(du), _row(y_attn), _row(y_rnn), _row(proj, glb, lb), _row(proj, glb + 2, lb)],
        [("row", BF16)] * 4, nr=s_len // tr, tr=tr, nc=2, cw=hw, name="merge_bwd")
    dattn = _mm(dya, w_oa_f, tb=True, out_dtype=BF16, name="o_attn_bwd_x")
    gw_oa = _mm(attn_o, dya, ta=True, out_dtype=BF16, name="o_attn_bwd_w")
    drg = _mm(dyr, w_or_f, tb=True, out_dtype=BF16, name="o_rnn_bwd_x")
    gw_or = _mm(rg, dyr, ta=True, out_dtype=BF16, name="o_rnn_bwd_w")
    sent["w_o_attn"] = _scatter_start(gw_oa, 0, my, "rs_start_w_o_attn")
    sent["w_o_rnn"] = _scatter_start(gw_or, 0, my, "rs_start_w_o_rnn")
    conv_b_bwd = _tok(conv_b, sent["w_out"][0][4], sent["w_o_attn"][0][4], sent["w_o_rnn"][0][4])

    dq, dk, dv = _attn_bwd(q, k_all, v_all, dattn, tq=tq, name="attn_bwd")
    dxr, dxg, dvec, g_wrg, g_wig = _rglru_bwd(proj, xr_blk, xg_blk, shift_masks, h_fwd, h_bwd, drg, conv_w_f, conv_b_bwd, w_rg[0], w_ig[0], b_rg_f,
                                               b_ig_f, lam_f, l_len=l_len, name="rglru_bwd")
    gate_part = _pack_flat([g_wrg, g_wig], N_DEV * SUBLANES)
    gate_sent = _scatter_start(gate_part, 0, my, "rs_start_gates")
    q_gain_bwd = _tok(q_gain, gate_sent[0][4])

    dq_p, g_qgain = _rowwise(lambda i, j, xv, dov, gain, ce, se: _qk_bwd_heads(xv, dov, gain, ce, se),
                             [_row(proj, 0, lb, aw), _row(dq, 0, 0, aw), _full(q_gain_bwd), _row(cos_e, 0, lb, HEAD_DIM),
                              _row(sin_e, 0, lb, HEAD_DIM)],
                             [("row", BF16, aw), ("acc", 1, HEAD_DIM)], nr=s_len // tr, tr=tr, nc=1, cw=aw, name="q_bwd")

    def kv_bwd(i, j, xv, dkv, dvv, gain, ce, se):
        dk_p, dgain = _qk_bwd_heads(xv, dkv, gain, ce, se)
        return dk_p, dvv, dgain

    dk_p, dv_p, g_kgain = _rowwise(kv_bwd, [_row(proj, kb, 0, kvw), _row(dk, 0, 0, kvw), _row(dv, 0, 0, kvw),
                                            _full(k_gain), _row(cos_e, 0, 0, HEAD_DIM), _row(sin_e, 0, 0, HEAD_DIM)],
                                   [("row", BF16, kvw), ("row", BF16, kvw), ("acc", 1, HEAD_DIM)], nr=t_len // tr, tr=tr,
                                   nc=1, cw=kvw, name="kv_bwd")

    pad_ctx = lambda a: jnp.pad(a, ((l_len, 0), (0, 0)))
    dproj = jnp.concatenate([pad_ctx(dq_p), dk_p, dv_p, dxr, dxg, pad_ctx(dgla), pad_ctx(dglr)], axis=1)
    dh_all = _mm(dproj, w_in_f, tb=True, name="proj_bwd_x")
    g_mix_bwd = g_mix

    def norm1_bwd(i, j, xv, dhv, dresv, g, sc):
        r = _rms(xv)
        xh = xv * r
        dn = dhv * (1.0 + sc)
        return dresv + _rms_bwd(xh, r, dn * g), _colsum(dhv), _colsum(dhv * (xh * g)), _colsum(dn * xh)

    def norm1_bwd_ctx(i, j, xv, dhv, g, sc):
        xh = xv * _rms(xv)
        return _colsum(dhv), _colsum(dhv * (xh * g)), _colsum(dhv * (1.0 + sc) * xh)

    dx_lat, d_sh_a, d_sc_a, g_gmix_lat = _rowwise(
        norm1_bwd, [_row(x_all, 0, lb), _row(dh_all, 0, lb), _row(dx1), _bc(g_mix_bwd), _bc(sc_a)],
        [("row", F32)] + [("acc", 1)] * 3, nr=s_len // tr, tr=tr, nc=1, cw=d, name="norm1_bwd")
    d_csh, d_csc, g_gmix_ctx = _rowwise(
        norm1_bwd_ctx, [_row(x_all), _row(dh_all), _bc(g_mix_bwd), _bc(csc_a)],
        [("acc", 1)] * 3, nr=lb, tr=tr, nc=1, cw=d, name="norm1_bwd_ctx")
    g_gmix = g_gmix_lat + g_gmix_ctx
    grad_x = dx_lat[None]

    dmod_lat = jnp.concatenate([d_sh_a, d_sc_a, d_ga_a, d_sh_f, d_sc_f, d_ga_f], axis=1)
    dmod_ctx = jnp.concatenate([d_csh, d_csc, jnp.zeros((1, (N_MOD - 2) * d), F32)], axis=1)
    dm_g = _all_gather([jnp.concatenate([dmod_lat, dmod_ctx, jnp.zeros((SUBLANES - 2, N_MOD * d), F32)], axis=0)], [0],
                       "ag_dmod")[0]
    dm16 = jnp.concatenate([dm_g[0::SUBLANES], dm_g[1::SUBLANES]], axis=0)
    dm_sh = lax.dynamic_slice(dm16, (0, my * nmod), (2 * N_DEV, nmod))
    gw_mod = _mm(c16, dm_sh, ta=True, a_fn=_silu, name="mod_bwd_w")
    dc16 = _mm(dm_sh, w_mod[0], tb=True, name="mod_bwd_x")

    def cctx_fn(i, j, pv, cc, dl, dc):
        row = lax.broadcasted_iota(jnp.int32, pv.shape, 0)
        return _colsum(jnp.where(row >= N_DEV, pv, 0.0)) * _dsilu(cc), dl + dc

    g_cctx, g_bmod = _rowwise(cctx_fn, [_full(dc16), _full(c_ctx2), _full(dmod_lat), _full(dmod_ctx)],
                              [("row", F32, d), ("row", F32, N_MOD * d)], nr=1, tr=1, nc=1, cw=d, name="cctx_bwd")

    rep_part = dict(c_ctx=g_cctx, b_mod=g_bmod, g_mix=g_gmix, g_mlp=g_gmlp, q_gain=g_qgain, k_gain=g_kgain,
                    conv_b=dvec[4:5], g_final=g_gfinal)
    rep_small = [n for n in REPLICATED if n in rep_part]
    gates = [n for n in REPLICATED if n not in rep_part]
    part_pack = _pack_flat([rep_part[n] for n in rep_small], PACK_ROWS)
    rep_rows = part_pack.shape[0]
    gate_land = _xfer_wait("scatter", gate_sent[0], 0, g_bmod, "rs_wait_gates")
    gate_sum = _sum_parts(gate_sent[1], gate_land, name="gate_sum")
    (gate_full, parts), order = _all_gather([gate_sum, part_pack], [0, 0], "ag_rep_grads", with_token=True)
    small_g = jnp.concatenate([dvec[0:4], dvec[5:11], jnp.zeros((SMALL_ROWS - 10, d), F32)], axis=0)
    (small_recv,), order = _all_to_all([_tok(small_g, order)], [1], "a2a_small_grads")

    halves = {}
    for part, lo in (("top", 0), ("bot", d // 2)):
        gw_part = _mm(h_all[:, lo:lo + d // 2], dproj, ta=True, out_dtype=BF16, name="proj_bwd_w_" + part,
                      bias=_tok(jnp.zeros((1, n_in), F32), order))
        halves[part] = _chip_scatter_start(_pair_reduce(gw_part, my, "pair_w_in_" + part), my, "rs_start_w_in_" + part)
        order = halves[part][0][4]
    started = order

    res = {}
    for names, pr, mult, nm in ((rep_small, parts.reshape(N_DEV, rep_rows, LANES), PACK_ROWS, "adamw_rep"),
                                (gates, gate_full[None], N_DEV * SUBLANES, "adamw_gates")):
        rep_out = _adamw(pr, _tok(_pack_flat([wd[n] for n in names], mult), started),
                         _pack_flat([md[n] for n in names], mult), _pack_flat([vd[n] for n in names], mult), name=nm)
        for kind, packed in zip(("g", "d", "m", "v"), rep_out):
            for n, a in zip(names, _unpack_flat(packed, [wd[n].shape for n in names])):
                res[kind, n] = a
    out = _adamw(small_recv, _tok(sp_w, started), _pack_small(md), _pack_small(vd), name="adamw_small")
    for kind, a in zip(("g", "d", "m", "v"), out):
        for n, piece in _unpack_small(a).items():
            res[kind, n] = piece

    after = started
    for n, (handle, own, ax, kind) in sent.items():
        land = _xfer_wait(kind, handle, ax, after, "rs_wait_" + n)
        out = _adamw(land, wd[n][0], md[n][0], vd[n][0], own=own, name="adamw_" + n)
        after = out[0]
        for kind, a in zip(("g", "d", "m", "v"), out):
            res[kind, n] = a[None]
    out = _adamw(gw_mod[None], w_mod[0], m_w_mod[0], v_w_mod[0], name="adamw_w_mod")
    after = out[0]
    for kind, a in zip(("g", "d", "m", "v"), out):
        res[kind, 'w_mod'] = a[None]
    lands, owns = [], []
    for part, (handle, own, ax, kind) in halves.items():
        lands.append(_xfer_wait(kind, handle, ax, after, "rs_wait_w_in_" + part))
        owns.append(own)
        after = lands[-1]
    out = _adamw(jnp.concatenate(lands, axis=1), w_in[0], m_w_in[0], v_w_in[0], own=jnp.concatenate(owns, axis=0),
                 name="adamw_w_in")
    for kind, a in zip(("g", "d", "m", "v"), out):
        res[kind, 'w_in'] = a[None]

    outs = [loss, grad_x]
    for kind in ("g", "d", "m", "v"):
        outs += [res[kind, n].reshape(wd[n].shape) for n in WEIGHTS]
    return tuple(outs)
```

```python
import functools
import math

import jax
import jax.numpy as jnp
from jax import lax
from jax.experimental import pallas as pl
from jax.experimental.pallas import tpu as pltpu

F32 = jnp.float32
BF16 = jnp.bfloat16
MESH = pl.DeviceIdType.MESH

N_DEV = 8
HEAD_DIM = 128
Q_GROUP = 4
RNN_BLOCK = 128
GRID_W = 64
CONV_WIDTH = 4
LRU_C = 8.0
ROPE_THETA = 10000.0
NORM_EPS = 1e-6
N_MOD = 6
ADAM_LR = 0.001
ADAM_B1 = 0.9
ADAM_B2 = 0.999
ADAM_EPS = 1e-08
ADAM_WD = 0.01
ADAM_STEP = 10

LANES = 128
SUBLANES = 8
VMEM_LIMIT = 56 * 1024 * 1024
HBM_SPEC = pl.BlockSpec(memory_space=pltpu.HBM)


def _params(sem=None, vmem=VMEM_LIMIT):
    kw = dict(vmem_limit_bytes=vmem)
    if sem is not None:
        kw["dimension_semantics"] = sem
    return pltpu.CompilerParams(**kw)


def _pick(dim, target, mult):
    for t in range(min(dim, target), 0, -1):
        if dim % t == 0 and t % mult == 0:
            return t
    return dim


def _my_pos():
    return lax.axis_index("x"), lax.axis_index("y"), lax.axis_index("c")


def _flip(pos, d):
    x, y, c = pos
    return (1 - x if d & 4 else x, 1 - y if d & 2 else y, 1 - c if d & 1 else c)


def _lin(pos):
    return 4 * pos[0] + 2 * pos[1] + pos[2]


def _slab(ref, axis, idx, size):
    if axis == 0:
        return ref.at[pl.ds(idx * size, size)]
    assert axis == 1
    return ref.at[:, pl.ds(idx * size, size)]


def _all_gather(arrs, axes, name, with_token=False):
    n = len(arrs)
    sizes = [a.shape[ax] for a, ax in zip(arrs, axes)]

    def body(*refs):
        ins, outs, token = refs[:n], refs[n:2 * n], refs[2 * n]
        send_sems, recv_sems, local_sems = refs[2 * n + 1:]
        token[...] = jnp.zeros_like(token)
        me = _my_pos()
        sibling = _flip(me, 1)
        chips = [2, 4, 6]

        def copy(a, k, block_pos, to, src=None):
            dst = _slab(outs[a], axes[a], _lin(block_pos), sizes[a])
            return pltpu.make_async_remote_copy(
                src_ref=dst if src is None else src, dst_ref=dst,
                send_sem=send_sems.at[7 * a + k], recv_sem=recv_sems.at[7 * a + k],
                device_id=to, device_id_type=MESH)

        local, first, passed = [], [], []
        for a in range(n):
            cp = pltpu.make_async_copy(ins[a], _slab(outs[a], axes[a], _lin(me), sizes[a]), local_sems.at[a])
            cp.start()
            local.append(cp)
        for a in range(n):
            f = [copy(a, 0, me, sibling, src=ins[a])]
            f += [copy(a, 1 + j, me, _flip(me, m), src=ins[a]) for j, m in enumerate(chips)]
            for cp in f:
                cp.start()
            first += f
        for a in range(n):
            for j, m in enumerate(chips):
                copy(a, 1 + j, _flip(me, m), me).wait_recv()
                cp = copy(a, 4 + j, _flip(me, m), sibling)
                cp.start()
                passed.append(cp)
        for a in range(n):
            copy(a, 0, sibling, me).wait_recv()
            for j, m in enumerate(chips):
                copy(a, 4 + j, _flip(sibling, m), me).wait_recv()
        for cp in first + passed:
            cp.wait_send()
        for cp in local:
            cp.wait()

    out_shape = []
    for a, ax in zip(arrs, axes):
        shp = list(a.shape)
        shp[ax] *= N_DEV
        out_shape.append(jax.ShapeDtypeStruct(tuple(shp), a.dtype))
    out_shape.append(jax.ShapeDtypeStruct((SUBLANES, LANES), F32))
    res = pl.pallas_call(
        body, name=name, out_shape=tuple(out_shape),
        in_specs=[HBM_SPEC] * n, out_specs=tuple([HBM_SPEC] * n + [pl.BlockSpec(memory_space=pltpu.VMEM)]),
        scratch_shapes=[pltpu.SemaphoreType.DMA((7 * n,)), pltpu.SemaphoreType.DMA((7 * n,)),
                        pltpu.SemaphoreType.DMA((n,))],
    )(*arrs)
    return (list(res[:n]), res[n]) if with_token else list(res[:n])


def _gather_proj(h_all, w_shard, name):
    t_len, kdim = h_all.shape
    sz = w_shard.shape[1]
    tm = _pick(t_len, 1024, 16)
    n_m = t_len // tm

    def body(h_ref, w_ref, wfull_ref, proj_ref, token, a_buf, b_buf, o_buf, send_sems, recv_sems, io_sems, out_sems):
        token[...] = jnp.zeros_like(token)
        me = _my_pos()
        sibling = _flip(me, 1)
        chips = [2, 4, 6]
        slab = lambda pos: _slab(wfull_ref, 1, _lin(pos), sz)

        def copy(k, block_pos, to, src=None):
            return pltpu.make_async_remote_copy(
                src_ref=slab(block_pos) if src is None else src, dst_ref=slab(block_pos),
                send_sem=send_sems.at[k], recv_sem=recv_sems.at[k], device_id=to, device_id_type=MESH)

        pending = []

        def block_product(pos, b_src):
            load = pltpu.make_async_copy(b_src, b_buf, io_sems.at[1])
            load.start()
            load.wait()
            for i in range(n_m):
                slot = len(pending) % 2
                if len(pending) >= 2:
                    pending[-2].wait()
                o_buf[slot] = _dot(a_buf[pl.ds(i * tm, tm), :], b_buf[...], 1, 0)
                out = pltpu.make_async_copy(
                    o_buf.at[slot], proj_ref.at[pl.ds(i * tm, tm), pl.ds(_lin(pos) * sz, sz)], out_sems.at[slot])
                out.start()
                pending.append(out)

        local = pltpu.make_async_copy(w_ref, slab(me), io_sems.at[0])
        local.start()
        first = [copy(0, me, sibling, src=w_ref)] + [copy(1 + j, me, _flip(me, m), src=w_ref) for j, m in enumerate(chips)]
        for cp in first:
            cp.start()
        load_a = pltpu.make_async_copy(h_ref, a_buf, io_sems.at[2])
        load_a.start()
        load_a.wait()
        block_product(me, w_ref)
        copy(0, sibling, me).wait_recv()
        block_product(sibling, slab(sibling))
        passed = []
        for j, m in enumerate(chips):
            src_pos = _flip(me, m)
            copy(1 + j, src_pos, me).wait_recv()
            cp = copy(4 + j, src_pos, sibling)
            cp.start()
            passed.append(cp)
            block_product(src_pos, slab(src_pos))
        for j, m in enumerate(chips):
            src_pos = _flip(sibling, m)
            copy(4 + j, src_pos, me).wait_recv()
            block_product(src_pos, slab(src_pos))
        for out in pending[-2:]:
            out.wait()
        for cp in first + passed:
            cp.wait_send()
        local.wait()

    return pl.pallas_call(
        body, name=name,
        out_shape=(jax.ShapeDtypeStruct((kdim, N_DEV * sz), w_shard.dtype), jax.ShapeDtypeStruct((t_len, N_DEV * sz), F32),
                   jax.ShapeDtypeStruct((SUBLANES, LANES), F32)),
        in_specs=[HBM_SPEC, HBM_SPEC], out_specs=(HBM_SPEC, HBM_SPEC, pl.BlockSpec(memory_space=pltpu.VMEM)),
        scratch_shapes=[pltpu.VMEM((t_len, kdim), h_all.dtype), pltpu.VMEM((kdim, sz), w_shard.dtype),
                        pltpu.VMEM((2, tm, sz), F32), pltpu.SemaphoreType.DMA((7,)), pltpu.SemaphoreType.DMA((7,)),
                        pltpu.SemaphoreType.DMA((3,)), pltpu.SemaphoreType.DMA((2,))],
        compiler_params=_params(),
    )(h_all, w_shard)


def _all_to_all(arrs, axes, name):
    n = len(arrs)
    sizes = [a.shape[ax] // N_DEV for a, ax in zip(arrs, axes)]

    def body(*refs):
        ins, outs, token = refs[:n], refs[n:2 * n], refs[2 * n]
        send_sems, recv_sems, local_sems = refs[2 * n + 1:]
        token[...] = jnp.zeros_like(token)
        me = _my_pos()
        my = _lin(me)
        copies, local = [], []
        for a in range(n):
            cp = pltpu.make_async_copy(_slab(ins[a], axes[a], my, sizes[a]), outs[a].at[my], local_sems.at[a])
            cp.start()
            local.append(cp)
        for d in range(1, N_DEV):
            peer = _flip(me, d)
            for a in range(n):
                cp = pltpu.make_async_remote_copy(
                    src_ref=_slab(ins[a], axes[a], _lin(peer), sizes[a]), dst_ref=outs[a].at[my],
                    send_sem=send_sems.at[7 * a + d - 1], recv_sem=recv_sems.at[7 * a + d - 1],
                    device_id=peer, device_id_type=MESH)
                cp.start()
                copies.append((a, d, cp))
        for a, d, cp in copies:
            peer = _flip(me, d)
            pltpu.make_async_remote_copy(
                src_ref=_slab(ins[a], axes[a], _lin(peer), sizes[a]), dst_ref=outs[a].at[_lin(peer)],
                send_sem=send_sems.at[7 * a + d - 1], recv_sem=recv_sems.at[7 * a + d - 1],
                device_id=peer, device_id_type=MESH).wait_recv()
            cp.wait_send()
        for cp in local:
            cp.wait()

    out_shape = []
    for a, ax, s in zip(arrs, axes, sizes):
        shp = list(a.shape)
        shp[ax] = s
        out_shape.append(jax.ShapeDtypeStruct((N_DEV,) + tuple(shp), a.dtype))
    out_shape.append(jax.ShapeDtypeStruct((SUBLANES, LANES), F32))
    res = pl.pallas_call(
        body, name=name, out_shape=tuple(out_shape),
        in_specs=[HBM_SPEC] * n, out_specs=tuple([HBM_SPEC] * n + [pl.BlockSpec(memory_space=pltpu.VMEM)]),
        scratch_shapes=[pltpu.SemaphoreType.DMA((7 * n,)), pltpu.SemaphoreType.DMA((7 * n,)),
                        pltpu.SemaphoreType.DMA((n,))],
    )(*arrs)
    return list(res[:n]), res[n]


SEM_SPEC = pl.BlockSpec(memory_space=pltpu.SEMAPHORE)
_EFFECT = pltpu.SideEffectType.DATAFLOW_SIDE_EFFECTING
N_PEERS = N_DEV - 1


def _split_copies(kind, src_ref, land_ref, send_sems, recv_sems, axis, size, arriving):
    me = _my_pos()
    out = []
    if kind == "sibling":
        sibling = _flip(me, 1)
        for k in range(N_CHIPS):
            blk = _slab(land_ref, axis, 2 * k + (sibling[2] if arriving else me[2]), size)
            out.append(pltpu.make_async_remote_copy(src_ref=blk, dst_ref=blk, send_sem=send_sems.at[k],
                                                    recv_sem=recv_sems.at[k], device_id=sibling, device_id_type=MESH))
        return out
    for dd in (range(2, N_DEV, 2) if kind in ("chips", "gather_chips") else range(1, N_DEV)):
        peer = _flip(me, dd)
        k = dd - 1
        if kind in ("gather", "gather_chips"):
            src = _slab(land_ref, axis, _lin(me), size)
            dst = _slab(land_ref, axis, _lin(peer if arriving else me), size)
        elif kind == "scatter":
            src, dst = _slab(src_ref, axis, _lin(peer), size), land_ref.at[k]
        else:
            k = dd // 2 - 1
            src, dst = src_ref.at[2 * peer[0] + peer[1]], land_ref.at[k]
        out.append(pltpu.make_async_remote_copy(
            src_ref=src, dst_ref=dst, send_sem=send_sems.at[k], recv_sem=recv_sems.at[k],
            device_id=peer, device_id_type=MESH))
    return out


N_CHIPS = N_DEV // 2


def _split_size(kind, src_shape, land_shape, axis):
    if kind in ("gather", "gather_chips", "sibling"):
        return land_shape[axis] // N_DEV
    return src_shape[axis] // {"scatter": N_DEV, "chips": N_CHIPS}[kind]


def _xfer_start(kind, src, land, axis, name):
    size = _split_size(kind, src.shape, land.shape, axis)

    def body(src_ref, land_ref, send_sems, recv_sems, src_thru, land_thru, token):
        for cp in _split_copies(kind, src_ref, land_ref, send_sems, recv_sems, axis, size, False):
            cp.start()
        token[...] = jnp.zeros_like(token)

    return pl.pallas_call(
        body, name=name,
        out_shape=(pltpu.SemaphoreType.DMA((N_PEERS,)), pltpu.SemaphoreType.DMA((N_PEERS,)),
                   pltpu.HBM(src.shape, src.dtype), pltpu.HBM(land.shape, land.dtype),
                   jax.ShapeDtypeStruct((SUBLANES, LANES), F32)),
        in_specs=(HBM_SPEC, HBM_SPEC),
        out_specs=(SEM_SPEC, SEM_SPEC, HBM_SPEC, HBM_SPEC, pl.BlockSpec(memory_space=pltpu.VMEM)),
        input_output_aliases={0: 2, 1: 3},
        compiler_params=pltpu.CompilerParams(has_side_effects=_EFFECT),
    )(pltpu.with_memory_space_constraint(src, pltpu.HBM), pltpu.with_memory_space_constraint(land, pltpu.HBM))


def _xfer_wait(kind, handle, axis, after, name):
    send_sems, recv_sems, src_thru, land_thru, _ = handle
    size = _split_size(kind, src_thru.shape, land_thru.shape, axis)

    def body(src_ref, land_ref, send_sems, recv_sems, after_ref, src_dead, got_ref):
        for cp in _split_copies(kind, src_ref, land_ref, send_sems, recv_sems, axis, size, True):
            cp.wait_send()
            cp.wait_recv()

    return pl.pallas_call(
        body, name=name,
        out_shape=(pltpu.HBM(src_thru.shape, src_thru.dtype), pltpu.HBM(land_thru.shape, land_thru.dtype)),
        in_specs=(HBM_SPEC, HBM_SPEC, SEM_SPEC, SEM_SPEC, pl.BlockSpec(memory_space=pl.ANY)),
        out_specs=(HBM_SPEC, HBM_SPEC), input_output_aliases={0: 0, 1: 1},
        compiler_params=pltpu.CompilerParams(has_side_effects=_EFFECT),
    )(src_thru, land_thru, send_sems, recv_sems, after)[1]


def _place_block(shard, axis, my, name):
    r, c = shard.shape
    tr = _pick(r, max(16, (1 << 19) // c), 16)
    nr = r // tr
    if axis == 0:
        full, out_map = (r * N_DEV, c), (lambda i, my_ref: (my_ref[0] * nr + i, 0))
    else:
        full, out_map = (r, c * N_DEV), (lambda i, my_ref: (i, my_ref[0]))

    def body(my_ref, x_ref, o_ref):
        o_ref[...] = x_ref[...].astype(BF16)

    return pl.pallas_call(
        body, name=name, out_shape=jax.ShapeDtypeStruct(full, BF16),
        grid_spec=pltpu.PrefetchScalarGridSpec(
            num_scalar_prefetch=1, grid=(nr,), in_specs=[pl.BlockSpec((tr, c), lambda i, my_ref: (i, 0))],
            out_specs=pl.BlockSpec((tr, c), out_map)),
        compiler_params=_params(("arbitrary",)),
    )(jnp.reshape(my, (1,)).astype(jnp.int32), shard)


def _scatter_start(grad, axis, my, name):
    shp = list(grad.shape)
    shp[axis] //= N_DEV
    own = lax.dynamic_slice_in_dim(grad, my * shp[axis], shp[axis], axis)
    return (_xfer_start("scatter", grad, lax.empty((N_PEERS,) + tuple(shp), grad.dtype), axis, name), own, axis,
            "scatter")


def _pair_reduce(grad, my, name):
    r, cols = grad.shape
    sz = cols // N_DEV

    def exchange(g_ref, got_ref, send_sems, recv_sems):
        me = _my_pos()
        sibling = _flip(me, 1)
        cps = []
        for k in range(N_CHIPS):
            cp = pltpu.make_async_remote_copy(
                src_ref=_slab(g_ref, 1, 2 * k + sibling[2], sz), dst_ref=_slab(got_ref, 1, k, sz),
                send_sem=send_sems.at[k], recv_sem=recv_sems.at[k], device_id=sibling, device_id_type=MESH)
            cp.start()
            cps.append(cp)
        for cp in cps:
            cp.wait()

    got = pl.pallas_call(
        exchange, name=name + "_swap", out_shape=jax.ShapeDtypeStruct((r, N_CHIPS * sz), grad.dtype),
        in_specs=[HBM_SPEC], out_specs=HBM_SPEC,
        scratch_shapes=[pltpu.SemaphoreType.DMA((N_CHIPS,)), pltpu.SemaphoreType.DMA((N_CHIPS,))],
    )(grad)

    tr = _pick(r, 512, 16)

    def add(c_ref, g_ref, p_ref, o_ref):
        o_ref[...] = (g_ref[...].astype(F32) + p_ref[...].astype(F32)).astype(o_ref.dtype)

    def add(c_ref, g_ref, p_ref, o_ref):
        o_ref[0] = (g_ref[...].astype(F32) + p_ref[...].astype(F32)).astype(o_ref.dtype)

    return pl.pallas_call(
        add, name=name + "_sum", out_shape=jax.ShapeDtypeStruct((N_CHIPS, r, sz), grad.dtype),
        grid_spec=pltpu.PrefetchScalarGridSpec(
            num_scalar_prefetch=1, grid=(N_CHIPS, r // tr),
            in_specs=[pl.BlockSpec((tr, sz), lambda k, i, c_ref: (i, 2 * k + c_ref[0])),
                      pl.BlockSpec((tr, sz), lambda k, i, c_ref: (i, k))],
            out_specs=pl.BlockSpec((1, tr, sz), lambda k, i, c_ref: (k, i, 0))),
        compiler_params=_params(("parallel", "parallel")),
    )(jnp.reshape(my % 2, (1,)).astype(jnp.int32), grad, got)


def _chip_scatter_start(pair_sum, my, name):
    _, r, sz = pair_sum.shape
    own = lax.dynamic_index_in_dim(pair_sum, my // 2, 0, keepdims=False)
    return (_xfer_start("chips", pair_sum, lax.empty((N_CHIPS - 1, r, sz), pair_sum.dtype), 0, name), own, 0, "chips")


def _tok(a, *tokens):
    for t in tokens:
        a = a + t[0, 0].astype(a.dtype)
    return a


def _mm(a, b, *, name, ta=False, tb=False, out_dtype=F32, a_fn=None, out_fn=None, extras=(), bias=None,
        tm=1024, tn=1024, tk=2816):
    if ta:
        kdim, m = a.shape
    else:
        m, kdim = a.shape
    if tb:
        n, kb = b.shape
    else:
        kb, n = b.shape
    assert kb == kdim, (a.shape, b.shape, ta, tb)
    tm = _pick(m, tm, LANES if ta else SUBLANES)
    tn = _pick(n, tn, LANES)
    tk = _pick(kdim, tk, LANES)
    nk = kdim // tk
    ne = len(extras)
    has_bias = bias is not None

    def body(*refs):
        a_ref, b_ref = refs[0], refs[1]
        pos = 2
        bias_ref = None
        if has_bias:
            bias_ref = refs[pos]
            pos += 1
        extra_refs = refs[pos:pos + ne]
        o_ref = refs[pos + ne]
        av = a_ref[...]
        if a_fn is not None:
            av = a_fn(av)
        dn = (((0 if ta else 1,), (1 if tb else 0,)), ((), ()))
        part = lax.dot_general(av.astype(BF16), b_ref[...].astype(BF16), dn, preferred_element_type=F32)

        def finish(r):
            if has_bias:
                r = r + bias_ref[...]
            if out_fn is not None:
                r = out_fn(r, *[e[...] for e in extra_refs])
            o_ref[...] = r.astype(out_dtype)

        if nk == 1:
            finish(part)
            return
        acc = refs[pos + ne + 1]
        k = pl.program_id(2)

        @pl.when(k == 0)
        def _():
            acc[...] = part

        @pl.when(k > 0)
        def _():
            acc[...] += part

        @pl.when(k == nk - 1)
        def _():
            finish(acc[...])

    in_specs = [
        pl.BlockSpec((tk, tm), lambda i, j, k: (k, i)) if ta else pl.BlockSpec((tm, tk), lambda i, j, k: (i, k)),
        pl.BlockSpec((tn, tk), lambda i, j, k: (j, k)) if tb else pl.BlockSpec((tk, tn), lambda i, j, k: (k, j)),
    ]
    args = [a, b]
    if has_bias:
        in_specs.append(pl.BlockSpec((1, tn), lambda i, j, k: (0, j)))
        args.append(bias)
    for e in extras:
        in_specs.append(pl.BlockSpec((tm, tn), lambda i, j, k: (i, j)))
        args.append(e)
    return pl.pallas_call(
        body, name=name, grid=(m // tm, n // tn, nk),
        in_specs=in_specs, out_specs=pl.BlockSpec((tm, tn), lambda i, j, k: (i, j)),
        out_shape=jax.ShapeDtypeStruct((m, n), out_dtype),
        scratch_shapes=[pltpu.VMEM((tm, tn), F32)] if nk > 1 else [],
        compiler_params=_params(("parallel", "parallel", "arbitrary")),
    )(*args)


def _row(arr, cb=0, rb=0, w=None):
    return ("row", arr, cb, rb, w)


def _bc(arr, cb=0, w=None):
    return ("bc", arr, cb, 0, w)


def _full(arr):
    return ("full", arr, 0, 0, None)


def _rowwise(fn, ins, outs, *, nr, tr, nc, cw, name):
    n_in = len(ins)
    outs = [(o[0], o[1], (o[2] if len(o) > 2 and o[2] else cw)) for o in outs]

    def body(*refs):
        in_refs, out_refs = refs[:n_in], refs[n_in:]
        j, i = pl.program_id(0), pl.program_id(1)
        res = fn(i, j, *[r[...] for r in in_refs])
        if not isinstance(res, (tuple, list)):
            res = (res,)
        for (kind, _, _), o_ref, r in zip(outs, out_refs, res):
            if kind == "row":
                o_ref[...] = r.astype(o_ref.dtype)
            else:
                @pl.when(i == 0)
                def _(o_ref=o_ref):
                    o_ref[...] = jnp.zeros_like(o_ref)

                o_ref[...] += r

    in_specs, args = [], []
    for kind, arr, cb, rb, w in ins:
        w = w or cw
        if kind == "row":
            in_specs.append(pl.BlockSpec((tr, w), lambda j, i, cb=cb, rb=rb: (i + rb, j + cb)))
        elif kind == "bc":
            in_specs.append(pl.BlockSpec((arr.shape[0], w), lambda j, i, cb=cb: (0, j + cb)))
        else:
            in_specs.append(pl.BlockSpec(arr.shape, lambda j, i, nd=arr.ndim: (0,) * nd))
        args.append(arr)
    out_specs, out_shape = [], []
    for kind, spec, w in outs:
        if kind == "row":
            out_specs.append(pl.BlockSpec((tr, w), lambda j, i: (i, j)))
            out_shape.append(jax.ShapeDtypeStruct((nr * tr, nc * w), spec))
        else:
            out_specs.append(pl.BlockSpec((spec, w), lambda j, i: (0, j)))
            out_shape.append(jax.ShapeDtypeStruct((spec, nc * w), F32))
    res = pl.pallas_call(
        body, name=name, grid=(nc, nr), in_specs=in_specs, out_specs=tuple(out_specs), out_shape=tuple(out_shape),
        compiler_params=_params(("parallel", "arbitrary")),
    )(*args)
    return res


def _sum_parts(own, parts, *, name):
    p, r, c = parts.shape
    tr = _pick(r, max(SUBLANES, (1 << 18) // c), SUBLANES)

    def body(o_ref, p_ref, g_out):
        g = o_ref[...].astype(F32)
        for q in range(p):
            g = g + p_ref[q].astype(F32)
        g_out[...] = g

    blk = pl.BlockSpec((tr, c), lambda i: (i, 0))
    return pl.pallas_call(
        body, name=name, grid=(r // tr,), in_specs=[blk, pl.BlockSpec((p, tr, c), lambda i: (0, i, 0))],
        out_specs=blk, out_shape=jax.ShapeDtypeStruct((r, c), F32), compiler_params=_params(("parallel",)),
    )(own, parts)


def _adamw(parts, w, m, v, *, name, own=None):
    p, r, c = parts.shape
    tr = _pick(r, max(SUBLANES, (1 << 18) // c), 16 if parts.dtype == BF16 else SUBLANES)
    bc1 = 1.0 - ADAM_B1 ** ADAM_STEP
    bc2 = 1.0 - ADAM_B2 ** ADAM_STEP
    has_own = own is not None

    def body(*refs):
        p_ref, w_ref, m_ref, v_ref = refs[:4]
        g_out, d_out, m_out, v_out = refs[4 + has_own:]
        if has_own:
            g = refs[4][...].astype(F32) + p_ref[0].astype(F32)
        else:
            g = p_ref[0].astype(F32)
        for q in range(1, p):
            g = g + p_ref[q].astype(F32)
        m_new = ADAM_B1 * m_ref[...] + (1.0 - ADAM_B1) * g
        v_new = ADAM_B2 * v_ref[...] + (1.0 - ADAM_B2) * (g * g)
        m_hat = m_new / bc1
        v_hat = v_new / bc2
        g_out[...] = g
        d_out[...] = -ADAM_LR * (m_hat / (jnp.sqrt(v_hat) + ADAM_EPS) + ADAM_WD * w_ref[...])
        m_out[...] = m_new
        v_out[...] = v_new

    blk = pl.BlockSpec((tr, c), lambda i: (i, 0))
    return pl.pallas_call(
        body, name=name, grid=(r // tr,),
        in_specs=[pl.BlockSpec((p, tr, c), lambda i: (0, i, 0)), blk, blk, blk] + ([blk] if has_own else []),
        out_specs=(blk, blk, blk, blk), out_shape=tuple(jax.ShapeDtypeStruct((r, c), F32) for _ in range(4)),
        compiler_params=_params(("parallel",)),
    )(*([parts, w, m, v] + ([own] if has_own else [])))


_GELU_C = math.sqrt(2.0 / math.pi)
_GELU_K = 0.044715


def _sigmoid(z):
    return 0.5 * jnp.tanh(0.5 * z) + 0.5


def _silu(z):
    return z * _sigmoid(z)


def _dsilu(z):
    s = _sigmoid(z)
    return s * (1.0 + z * (1.0 - s))


def _gelu(z):
    return 0.5 * z * (1.0 + jnp.tanh(_GELU_C * (z + _GELU_K * z * z * z)))


def _gelu_pair(z):
    zz = z * z
    t = jnp.tanh(_GELU_C * (z * (1.0 + _GELU_K * zz)))
    half = 0.5 * (1.0 + t)
    return z * half, half + 0.5 * z * (1.0 - t * t) * (_GELU_C * (1.0 + 3.0 * _GELU_K * zz))


def _softplus(z):
    return jnp.maximum(z, 0.0) + jnp.log(1.0 + jnp.exp(-jnp.abs(z)))


def _rms(xv):
    return lax.rsqrt(jnp.mean(xv * xv, axis=-1, keepdims=True) + NORM_EPS)


def _rms_bwd(xhat, r, dxhat):
    return r * (dxhat - xhat * jnp.mean(dxhat * xhat, axis=-1, keepdims=True))


def _swap_pairs(y):
    lane = lax.broadcasted_iota(jnp.int32, y.shape, 1)
    return jnp.where(lane % 2 == 0, pltpu.roll(y, LANES - 1, 1), pltpu.roll(y, 1, 1))


def _dot(a, b, ca, cb):
    return lax.dot_general(a, b, (((ca,), (cb,)), ((), ())), preferred_element_type=F32)


_SM_SCALE = HEAD_DIM ** -0.5
_EXP2_SCALE = _SM_SCALE * math.log2(math.e)


def _softmax_parts(qh, kk):
    s = _dot(qh, kk, 1, 1)
    p = jnp.exp2((s - jnp.max(s, axis=-1, keepdims=True)) * _EXP2_SCALE)
    return p, 1.0 / jnp.sum(p, axis=-1, keepdims=True)


def _attn_fwd(q, k, v, *, tq, name):
    s_len, aw = q.shape
    t_len, kvw = k.shape
    gw = Q_GROUP * HEAD_DIM

    def body(q_ref, k_ref, v_ref, o_ref):
        kk, vv = k_ref[...], v_ref[...]
        for g in range(Q_GROUP):
            cols = slice(g * HEAD_DIM, (g + 1) * HEAD_DIM)
            p, inv_l = _softmax_parts(q_ref[:, cols], kk)
            o_ref[:, cols] = (_dot(p.astype(BF16), vv, 1, 0) * inv_l).astype(BF16)

    return pl.pallas_call(
        body, name=name, grid=(kvw // HEAD_DIM, s_len // tq),
        in_specs=[pl.BlockSpec((tq, gw), lambda h, i: (i, h)), pl.BlockSpec((t_len, HEAD_DIM), lambda h, i: (0, h)),
                  pl.BlockSpec((t_len, HEAD_DIM), lambda h, i: (0, h))],
        out_specs=pl.BlockSpec((tq, gw), lambda h, i: (i, h)),
        out_shape=jax.ShapeDtypeStruct((s_len, aw), BF16),
        compiler_params=_params(("parallel", "parallel")),
    )(q, k, v)


def _attn_bwd(q, k, v, do, *, tq, name):
    s_len, aw = q.shape
    t_len, kvw = k.shape
    gw = Q_GROUP * HEAD_DIM

    def body(q_ref, k_ref, v_ref, do_ref, dq_ref, dk_ref, dv_ref):
        @pl.when(pl.program_id(1) == 0)
        def _():
            dk_ref[...] = jnp.zeros_like(dk_ref)
            dv_ref[...] = jnp.zeros_like(dv_ref)

        kk, vv = k_ref[...], v_ref[...]
        for g in range(Q_GROUP):
            cols = slice(g * HEAD_DIM, (g + 1) * HEAD_DIM)
            qh, doh = q_ref[:, cols], do_ref[:, cols]
            p, inv_l = _softmax_parts(qh, kk)
            dp = _dot(doh, vv, 1, 1)
            delta = jnp.sum(p * dp, axis=-1, keepdims=True) * inv_l
            dsb = (p * ((dp - delta) * (inv_l * _SM_SCALE))).astype(BF16)
            dq_ref[:, cols] = _dot(dsb, kk, 1, 0)
            dk_ref[...] += _dot(dsb, qh, 0, 0)
            dv_ref[...] += _dot(p.astype(BF16), (doh * inv_l).astype(BF16), 0, 0)

    qspec = pl.BlockSpec((tq, gw), lambda h, i: (i, h))
    kspec = pl.BlockSpec((t_len, HEAD_DIM), lambda h, i: (0, h))
    return pl.pallas_call(
        body, name=name, grid=(kvw // HEAD_DIM, s_len // tq),
        in_specs=[qspec, kspec, kspec, qspec], out_specs=(qspec, kspec, kspec),
        out_shape=(jax.ShapeDtypeStruct((s_len, aw), F32), jax.ShapeDtypeStruct((t_len, kvw), F32),
                   jax.ShapeDtypeStruct((t_len, kvw), F32)),
        compiler_params=_params(("parallel", "arbitrary")),
    )(q, k, v, do)


SHIFTS = (-2, -1, 1, 2)


def _shift_masks(l_len, t_len):
    t = jnp.arange(t_len)[:, None]
    cols = []
    for d in SHIFTS:
        td = t + d
        ok = ((t < l_len) & (td >= 0) & (td < l_len)) | ((t >= l_len) & (td >= l_len) & (td < t_len))
        cols.append(jnp.broadcast_to(ok.astype(F32), (t_len, RNN_BLOCK)))
    return jnp.concatenate(cols, axis=1)


def _shift_rows(xv, d, m_ref, l_len, t_len):
    if d == 0:
        return xv
    k = SHIFTS.index(d)
    return pltpu.roll(xv, (-d) % t_len, 0) * m_ref[:, k * RNN_BLOCK:(k + 1) * RNN_BLOCK]


def _sublane_scan(a8, b8, rev):
    row = lax.broadcasted_iota(jnp.int32, a8.shape, 0)
    for s in (1, 2, 4):
        if rev:
            ok = row < SUBLANES - s
            sh = SUBLANES - s
        else:
            ok = row >= s
            sh = s
        a_sh = jnp.where(ok, pltpu.roll(a8, sh, 0), 1.0)
        b_sh = jnp.where(ok, pltpu.roll(b8, sh, 0), 0.0)
        b8 = a8 * b_sh + b8
        a8 = a8 * a_sh
    return a8, b8


def _scan_chunks(chains, c0, n, carries):
    def step(kk, hs):
        new = []
        for (a_ref, b_ref, h_ref, rev), hprev in zip(chains, hs):
            c = c0 + (n - 1 - kk if rev else kk)
            r0 = pl.multiple_of(c * SUBLANES, SUBLANES)
            big_a, big_b = _sublane_scan(a_ref[pl.ds(r0, SUBLANES), :], b_ref[pl.ds(r0, SUBLANES), :], rev)
            h8 = big_b + big_a * hprev
            h_ref[pl.ds(r0, SUBLANES), :] = h8
            edge = 0 if rev else SUBLANES - 1
            new.append(jnp.broadcast_to(h8[edge:edge + 1, :], h8.shape))
        return tuple(new)

    return lax.fori_loop(0, n, step, tuple(carries))


def _rglru_conv(x, cw_ref, cb_ref, m_ref, l_len, t_len):
    y = cb_ref[...] + _shift_rows(x, -1, m_ref, l_len, t_len) * cw_ref[0:1, :]
    for kk in range(1, CONV_WIDTH):
        y = y + _shift_rows(x, kk - 1, m_ref, l_len, t_len) * cw_ref[kk:kk + 1, :]
    return y


def _rglru_gates(xc, wr, wi, br, bi, lam):
    xb = xc.astype(BF16)
    r = _sigmoid(_dot(xb, wr.astype(BF16), 1, 0) + br)
    ig = _sigmoid(_dot(xb, wi.astype(BF16), 1, 0) + bi)
    sp = _softplus(-lam)
    log_a = (-LRU_C) * r * sp
    a = jnp.exp(log_a)
    z2 = 2.0 * log_a
    series = z2 * (1.0 + z2 * (0.5 + z2 * (1.0 / 6.0 + z2 * (1.0 / 24.0))))
    one_m = jnp.where(jnp.abs(z2) < 0.03, -series, 1.0 - a * a)
    rinv = lax.rsqrt(jnp.maximum(one_m, 1e-30))
    u = ig * xc
    return r, ig, sp, a, one_m * rinv, u, rinv


def _rglru_specs(proj, xr_blk, xg_blk, t_len):
    col = lambda off: pl.BlockSpec((t_len, RNN_BLOCK), lambda h, off=off: (0, off + h))
    vec = lambda rows: pl.BlockSpec((rows, RNN_BLOCK), lambda h: (0, h))
    wsp = pl.BlockSpec((2, 1, RNN_BLOCK, RNN_BLOCK), lambda h: (0, h, 0, 0))
    msp = pl.BlockSpec((t_len, len(SHIFTS) * RNN_BLOCK), lambda h: (0, 0))
    return col(xr_blk), col(xg_blk), vec, wsp, msp


def _rglru_fwd(proj, xr_blk, xg_blk, masks, conv_w, conv_b, w_rg, w_ig, b_rg, b_ig, lam, *, l_len, name):
    t_len = proj.shape[0]
    s_len = t_len - l_len
    d_rnn = conv_b.shape[1]
    n_l, n_t = l_len // SUBLANES, t_len // SUBLANES

    def body(xr_ref, xg_ref, m_ref, cw_ref, cb_ref, wr_ref, wi_ref, br_ref, bi_ref, lam_ref, rg_ref, hf_ref, hb_ref,
             a_s, b_s):
        xc = _rglru_conv(xr_ref[...], cw_ref, cb_ref, m_ref, l_len, t_len)
        for d in range(2):
            _, _, _, a, mult, u, _ = _rglru_gates(xc, wr_ref[d, 0], wi_ref[d, 0], br_ref[d:d + 1, :],
                                               bi_ref[d:d + 1, :], lam_ref[d:d + 1, :])
            a_s[d] = a
            b_s[d] = mult * u
        chains = [(a_s.at[0], b_s.at[0], hf_ref, False), (a_s.at[1], b_s.at[1], hb_ref, True)]
        zero = jnp.zeros((SUBLANES, RNN_BLOCK), F32)
        carry = _scan_chunks(chains, 0, n_l, (zero, zero))
        _scan_chunks(chains, n_l, n_t - n_l, carry)
        rnn = hf_ref[pl.ds(l_len, s_len), :] + hb_ref[pl.ds(l_len, s_len), :]
        rg_ref[...] = (rnn * _gelu(xg_ref[pl.ds(l_len, s_len), :])).astype(BF16)

    xr_spec, xg_spec, vec, wsp, msp = _rglru_specs(proj, xr_blk, xg_blk, t_len)
    col_t = pl.BlockSpec((t_len, RNN_BLOCK), lambda h: (0, h))
    return pl.pallas_call(
        body, name=name, grid=(d_rnn // RNN_BLOCK,),
        in_specs=[xr_spec, xg_spec, msp, vec(CONV_WIDTH), vec(1), wsp, wsp, vec(2), vec(2), vec(2)],
        out_specs=(pl.BlockSpec((s_len, RNN_BLOCK), lambda h: (0, h)), col_t, col_t),
        out_shape=(jax.ShapeDtypeStruct((s_len, d_rnn), BF16), jax.ShapeDtypeStruct((t_len, d_rnn), F32),
                   jax.ShapeDtypeStruct((t_len, d_rnn), F32)),
        scratch_shapes=[pltpu.VMEM((2, t_len, RNN_BLOCK), F32)] * 2,
        compiler_params=_params(("parallel",)),
    )(proj, proj, masks, conv_w, conv_b, w_rg, w_ig, b_rg, b_ig, lam)


N_RNN_VEC = 16


def _rglru_bwd(proj, xr_blk, xg_blk, masks, h_fwd, h_bwd, drg, conv_w, conv_b, w_rg, w_ig, b_rg, b_ig, lam, *, l_len, name):
    t_len = proj.shape[0]
    s_len = t_len - l_len
    d_rnn = conv_b.shape[1]
    n_l, n_t = l_len // SUBLANES, t_len // SUBLANES

    def body(xr_ref, xg_ref, m_ref, hf_ref, hb_ref, drg_ref, cw_ref, cb_ref, wr_ref, wi_ref, br_ref, bi_ref, lam_ref,
             dxr_ref, dxg_ref, dvec_ref, dwr_ref, dwi_ref, a_s, b_s, h_s):
        t = lax.broadcasted_iota(jnp.int32, (t_len, RNN_BLOCK), 0)
        x = xr_ref[...]
        xc = _rglru_conv(x, cw_ref, cb_ref, m_ref, l_len, t_len)
        xb = xc.astype(BF16)
        gates = []
        for d in range(2):
            gt = _rglru_gates(xc, wr_ref[d, 0], wi_ref[d, 0], br_ref[d:d + 1, :], bi_ref[d:d + 1, :],
                              lam_ref[d:d + 1, :])
            gates.append(gt)
        zero = jnp.zeros((SUBLANES, RNN_BLOCK), F32)
        h0, h1 = hf_ref[...], hb_ref[...]
        xg = xg_ref[...]
        dr = jnp.concatenate([jnp.zeros((l_len, RNN_BLOCK), F32), drg_ref[...].astype(F32)], axis=0)
        gel, dgel = _gelu_pair(xg)
        dxg_ref[...] = (dr * (h0 + h1) * dgel).astype(dxg_ref.dtype)
        drnn = dr * gel
        a_s[0] = pltpu.roll(gates[0][3], t_len - 1, 0)
        a_s[1] = pltpu.roll(gates[1][3], 1, 0)
        b_s[0] = drnn
        b_s[1] = drnn
        hprev = [jnp.where(t == 0, 0.0, pltpu.roll(h0, 1, 0)),
                 jnp.where(t == l_len - 1, 0.0, pltpu.roll(h1, t_len - 1, 0))]
        chains = [(a_s.at[0], b_s.at[0], h_s.at[0], True), (a_s.at[1], b_s.at[1], h_s.at[1], False)]
        carry = _scan_chunks(chains, n_l, n_t - n_l, (zero, zero))
        _scan_chunks(chains, 0, n_l, carry)

        dxc = jnp.zeros((t_len, RNN_BLOCK), F32)
        for d in range(2):
            r, ig, sp, a, mult, u, rinv = gates[d]
            lamb = h_s[d]
            da = lamb * hprev[d]
            dmult = lamb * u
            du = lamb * mult
            dlog_a = da * a - dmult * (a * a) * rinv
            dzr = (dlog_a * ((-LRU_C) * sp)) * r * (1.0 - r)
            dsp = jnp.sum(dlog_a * ((-LRU_C) * r), axis=0, keepdims=True)
            dzi = (du * xc) * ig * (1.0 - ig)
            dzrb, dzib = dzr.astype(BF16), dzi.astype(BF16)
            dxc = dxc + du * ig + _dot(dzrb, wr_ref[d, 0].astype(BF16), 1, 1) + _dot(dzib, wi_ref[d, 0].astype(BF16), 1, 1)
            dwr_ref[d, 0] = _dot(xb, dzrb, 0, 0)
            dwi_ref[d, 0] = _dot(xb, dzib, 0, 0)
            dvec_ref[5 + d:6 + d, :] = jnp.sum(dzr, axis=0, keepdims=True)
            dvec_ref[7 + d:8 + d, :] = jnp.sum(dzi, axis=0, keepdims=True)
            dvec_ref[9 + d:10 + d, :] = dsp * (-_sigmoid(-lam_ref[d:d + 1, :]))
        dvec_ref[4:5, :] = jnp.sum(dxc, axis=0, keepdims=True)
        dvec_ref[11:N_RNN_VEC, :] = jnp.zeros((N_RNN_VEC - 11, RNN_BLOCK), F32)
        dxr = jnp.zeros((t_len, RNN_BLOCK), F32)
        for kk in range(CONV_WIDTH):
            dvec_ref[kk:kk + 1, :] = jnp.sum(dxc * _shift_rows(x, kk - 1, m_ref, l_len, t_len), axis=0, keepdims=True)
            dxr = dxr + _shift_rows(dxc, 1 - kk, m_ref, l_len, t_len) * cw_ref[kk:kk + 1, :]
        dxr_ref[...] = dxr.astype(dxr_ref.dtype)

    xr_spec, xg_spec, vec, wsp, msp = _rglru_specs(proj, xr_blk, xg_blk, t_len)
    col_t = pl.BlockSpec((t_len, RNN_BLOCK), lambda h: (0, h))
    return pl.pallas_call(
        body, name=name, grid=(d_rnn // RNN_BLOCK,),
        in_specs=[xr_spec, xg_spec, msp, col_t, col_t, pl.BlockSpec((s_len, RNN_BLOCK), lambda h: (0, h)), vec(CONV_WIDTH),
                  vec(1),
                  wsp, wsp, vec(2), vec(2), vec(2)],
        out_specs=(col_t, col_t, vec(N_RNN_VEC), wsp, wsp),
        out_shape=(jax.ShapeDtypeStruct((t_len, d_rnn), BF16), jax.ShapeDtypeStruct((t_len, d_rnn), BF16),
                   jax.ShapeDtypeStruct((N_RNN_VEC, d_rnn), F32), jax.ShapeDtypeStruct(w_rg.shape, F32),
                   jax.ShapeDtypeStruct(w_ig.shape, F32)),
        scratch_shapes=[pltpu.VMEM((2, t_len, RNN_BLOCK), F32)] * 3,
        compiler_params=_params(("parallel",)),
    )(proj, proj, masks, h_fwd, h_bwd, drg, conv_w, conv_b, w_rg, w_ig, b_rg, b_ig, lam)


def _colsum(v):
    return jnp.sum(v, axis=0, keepdims=True)


def _qk_fwd_heads(xv, gain, cos_e, sin_e):
    outs = []
    for h in range(xv.shape[1] // HEAD_DIM):
        xh = xv[:, h * HEAD_DIM:(h + 1) * HEAD_DIM]
        y = xh * _rms(xh) * gain
        outs.append(y * cos_e + _swap_pairs(y) * sin_e)
    return outs[0] if len(outs) == 1 else jnp.concatenate(outs, axis=1)


def _qk_bwd_heads(xv, dov, gain, cos_e, sin_e):
    outs = []
    dgain = jnp.zeros((1, HEAD_DIM), F32)
    for h in range(xv.shape[1] // HEAD_DIM):
        cols = slice(h * HEAD_DIM, (h + 1) * HEAD_DIM)
        xh, doh = xv[:, cols], dov[:, cols]
        r = _rms(xh)
        xhat = xh * r
        dy = doh * cos_e + _swap_pairs(doh * sin_e)
        dgain = dgain + _colsum(dy * xhat)
        outs.append(_rms_bwd(xhat, r, dy * gain))
    return (outs[0] if len(outs) == 1 else jnp.concatenate(outs, axis=1)), dgain


WEIGHTS = ['c_ctx', 'w_mod', 'b_mod', 'g_mix', 'g_mlp', 'w_in', 'q_gain', 'k_gain', 'conv_w', 'conv_b', 'w_rg', 'b_rg',
           'w_ig', 'b_ig', 'lru_lambda', 'w_o_attn', 'w_o_rnn', 'w_out', 'w_up', 'w_down', 'g_final']
REPLICATED = ['c_ctx', 'b_mod', 'g_mix', 'g_mlp', 'q_gain', 'k_gain', 'conv_b', 'w_rg', 'w_ig', 'g_final']
SMALL_SHARDED = [('conv_w', CONV_WIDTH), ('b_rg', 2), ('b_ig', 2), ('lru_lambda', 2)]
SMALL_ROWS = 16


PACK_ROWS = 512


def _pack_flat(arrs, rows_mult):
    flat = jnp.concatenate([a.reshape(-1) for a in arrs])
    n = flat.shape[0]
    unit = rows_mult * LANES
    pad = (-n) % unit
    if pad:
        flat = jnp.concatenate([flat, jnp.zeros((pad,), flat.dtype)])
    return flat.reshape(-1, LANES)


def _unpack_flat(packed, shapes):
    flat = packed.reshape(-1)
    out, pos = [], 0
    for shp in shapes:
        n = math.prod(shp)
        out.append(flat[pos:pos + n].reshape(shp))
        pos += n
    return out


def _pack_small(d):
    rows = [d[n][0] for n, _ in SMALL_SHARDED]
    cols = rows[0].shape[1]
    used = sum(k for _, k in SMALL_SHARDED)
    return jnp.concatenate(rows + [jnp.zeros((SMALL_ROWS - used, cols), F32)], axis=0)


def _unpack_small(p):
    out, pos = {}, 0
    for n, k in SMALL_SHARDED:
        out[n] = p[pos:pos + k][None]
        pos += k
    return out


def kernel(x, c, ctx, c_ctx, w_mod, b_mod, g_mix, g_mlp, w_in, q_gain, k_gain, conv_w, conv_b, w_rg, b_rg, w_ig, b_ig, lru_lambda, w_o_attn, w_o_rnn, w_out, w_up, w_down, g_final, loss_target, m_c_ctx, m_w_mod, m_b_mod, m_g_mix, m_g_mlp, m_w_in, m_q_gain, m_k_gain, m_conv_w, m_conv_b, m_w_rg, m_b_rg, m_w_ig, m_b_ig, m_lru_lambda, m_w_o_attn, m_w_o_rnn, m_w_out, m_w_up, m_w_down, m_g_final, v_c_ctx, v_w_mod, v_b_mod, v_g_mix, v_g_mlp, v_w_in, v_q_gain, v_k_gain, v_conv_w, v_conv_b, v_w_rg, v_b_rg, v_w_ig, v_b_ig, v_lru_lambda, v_w_o_attn, v_w_o_rnn, v_w_out, v_w_up, v_w_down, v_g_final):
    wd = dict(c_ctx=c_ctx, w_mod=w_mod, b_mod=b_mod, g_mix=g_mix, g_mlp=g_mlp, w_in=w_in, q_gain=q_gain, k_gain=k_gain,
              conv_w=conv_w, conv_b=conv_b, w_rg=w_rg, b_rg=b_rg, w_ig=w_ig, b_ig=b_ig, lru_lambda=lru_lambda,
              w_o_attn=w_o_attn, w_o_rnn=w_o_rnn, w_out=w_out, w_up=w_up, w_down=w_down, g_final=g_final)
    md = dict(zip(WEIGHTS, (m_c_ctx, m_w_mod, m_b_mod, m_g_mix, m_g_mlp, m_w_in, m_q_gain, m_k_gain, m_conv_w, m_conv_b,
                            m_w_rg, m_b_rg, m_w_ig, m_b_ig, m_lru_lambda, m_w_o_attn, m_w_o_rnn, m_w_out, m_w_up,
                            m_w_down, m_g_final)))
    vd = dict(zip(WEIGHTS, (v_c_ctx, v_w_mod, v_b_mod, v_g_mix, v_g_mlp, v_w_in, v_q_gain, v_k_gain, v_conv_w, v_conv_b,
                            v_w_rg, v_b_rg, v_w_ig, v_b_ig, v_lru_lambda, v_w_o_attn, v_w_o_rnn, v_w_out, v_w_up,
                            v_w_down, v_g_final)))

    xs, tgt, ctxs = x[0], loss_target[0], ctx[0]
    s_len, d = xs.shape
    l_len = ctxs.shape[0]
    t_len = s_len + l_len
    n_in = w_in.shape[2] * N_DEV
    aw = (n_in - 4 * d) * 2 // 3
    kvw = aw // Q_GROUP
    assert aw == d and n_in == aw + 2 * kvw + 4 * d
    nmod = N_MOD * d // N_DEV
    tr = 128
    assert l_len % tr == 0 and s_len % tr == 0
    lb = l_len // tr
    tq = min(256, s_len)
    my = _lin(_my_pos())
    c_ctx2 = c_ctx[None]
    g_final2 = g_final[None]

    c_all = _all_gather([jnp.broadcast_to(c, (SUBLANES, d))], [0], "ag_cond")[0][::SUBLANES]
    c16 = jnp.concatenate([c_all, jnp.broadcast_to(c_ctx2, (N_DEV, d))], axis=0)
    b_mod_sh = lax.dynamic_slice(b_mod, (0, my * nmod), (1, nmod))
    mod_part = _mm(c16, w_mod[0], a_fn=_silu, bias=b_mod_sh, name="mod_fwd")
    sp_w = _pack_small(wd)
    (mod_g, sp_f), order = _all_gather([mod_part, sp_w], [1, 1], "ag_mod", with_token=True)
    conv_w_f, b_rg_f, b_ig_f, lam_f = sp_f[0:4], sp_f[4:6], sp_f[6:8], sp_f[8:10]
    mod_lat = lax.dynamic_slice(mod_g, (my, 0), (1, N_MOD * d))
    sh_a, sc_a, ga_a, sh_f, sc_f, ga_f = [mod_lat[:, k * d:(k + 1) * d] for k in range(N_MOD)]
    csh_a, csc_a = mod_g[N_DEV:N_DEV + 1, :d], mod_g[N_DEV:N_DEV + 1, d:2 * d]

    x_all = jnp.concatenate([ctxs, xs], axis=0)

    def nm1_fwd(i, j, xv, g, csh, csc, sh, sc):
        is_ctx = i < lb
        n = xv * _rms(xv) * g
        return n * (1.0 + jnp.where(is_ctx, csc, sc)) + jnp.where(is_ctx, csh, sh)

    h_all = _rowwise(nm1_fwd, [_row(x_all), _bc(g_mix), _bc(csh_a), _bc(csc_a), _bc(sh_a), _bc(sc_a)],
                     [("row", BF16)], nr=t_len // tr, tr=tr, nc=1, cw=d, name="norm1_fwd")[0]
    w_in_f, proj, order = _gather_proj(h_all, _tok(w_in[0], order).astype(BF16), "ag_proj")
    late_axis = dict(w_o_attn=0, w_o_rnn=0, w_out=0, w_up=1, w_down=0)
    placed = [_place_block(wd[n][0], ax, my, "place_" + n) for n, ax in late_axis.items()]
    late_kind = lambda n: "gather_chips" if n in ("w_up", "w_down") else "gather"
    late = {}
    for (n, ax), full in zip(late_axis.items(), placed):
        late[n] = _xfer_start(late_kind(n), order, full, ax, "ag_start_" + n)
        order = late[n][4]
    q_gain_fwd = _tok(q_gain, order)

    late_wait = lambda n, after: _xfer_wait(late_kind(n), late[n], late_axis[n], after, "ag_wait_" + n)

    def sibling_start(n, after):
        arrived = late_wait(n, after)
        return _xfer_start("sibling", jnp.zeros((SUBLANES, LANES), F32), arrived, late_axis[n], "ag_sib_start_" + n)

    sibling_wait = lambda n, handle, after: _xfer_wait("sibling", handle, late_axis[n], after, "ag_sib_wait_" + n)

    pos = jnp.arange(s_len)
    n_freq = HEAD_DIM // 4
    inv_freq = ROPE_THETA ** (-jnp.arange(n_freq, dtype=F32) / n_freq)
    ang = jnp.concatenate([(pos // GRID_W).astype(F32)[:, None] * inv_freq,
                           (pos % GRID_W).astype(F32)[:, None] * inv_freq], axis=-1)
    sign = jnp.where(jnp.arange(HEAD_DIM) % 2 == 0, -1.0, 1.0).astype(F32)
    cos_e = jnp.concatenate([jnp.ones((l_len, HEAD_DIM), F32), jnp.repeat(jnp.cos(ang), 2, axis=-1)], axis=0)
    sin_e = jnp.concatenate([jnp.zeros((l_len, HEAD_DIM), F32), jnp.repeat(jnp.sin(ang), 2, axis=-1) * sign], axis=0)


    kb = aw // kvw
    q = _rowwise(lambda i, j, xv, gain, ce, se: _qk_fwd_heads(xv, gain, ce, se),
                 [_row(proj, 0, lb, aw), _full(q_gain_fwd), _row(cos_e, 0, lb, HEAD_DIM), _row(sin_e, 0, lb, HEAD_DIM)],
                 [("row", BF16, aw)], nr=s_len // tr, tr=tr, nc=1, cw=aw, name="q_fwd")[0]
    k_all, v_all = _rowwise(lambda i, j, kv_, vv_, gain, ce, se: (_qk_fwd_heads(kv_, gain, ce, se), vv_),
                            [_row(proj, kb, 0, kvw), _row(proj, kb + 1, 0, kvw), _full(k_gain),
                             _row(cos_e, 0, 0, HEAD_DIM), _row(sin_e, 0, 0, HEAD_DIM)],
                            [("row", BF16, kvw), ("row", BF16, kvw)], nr=t_len // tr, tr=tr, nc=1, cw=kvw, name="kv_fwd")
    attn_o = _attn_fwd(q, k_all, v_all, tq=tq, name="attn_fwd")

    xr_blk = (aw + 2 * kvw) // RNN_BLOCK
    xg_blk = xr_blk + d // RNN_BLOCK
    shift_masks = _shift_masks(l_len, t_len)
    rg, h_fwd, h_bwd = _rglru_fwd(proj, xr_blk, xg_blk, shift_masks, conv_w_f, conv_b, w_rg[0], w_ig[0], b_rg_f, b_ig_f, lam_f, l_len=l_len,
                    name="rglru_fwd")

    w_oa_f = late_wait("w_o_attn", attn_o)
    y_attn = _mm(attn_o, w_oa_f, name="o_attn_fwd")
    w_or_f = late_wait("w_o_rnn", rg)
    y_rnn = _mm(rg, w_or_f, name="o_rnn_fwd")
    hw = d // 2
    glb = (aw + 2 * kvw + 2 * d) // hw

    def merge_fwd(i, j, ya, yr, gla, glr):
        return _sigmoid(gla) * ya + _sigmoid(glr) * yr

    u = _rowwise(merge_fwd, [_row(y_attn), _row(y_rnn), _row(proj, glb, lb), _row(proj, glb + 2, lb)],
                 [("row", BF16)], nr=s_len // tr, tr=tr, nc=2, cw=hw, name="merge_fwd")[0]
    w_out_f = late_wait("w_out", u)
    sib_up = sibling_start("w_up", u)
    mix = _mm(u, w_out_f, bias=_tok(jnp.zeros((1, d), F32), sib_up[4]), name="out_fwd")

    def res1_fwd(i, j, xv, mixv, gaa, gm, shf, scf):
        x1v = xv + gaa * mixv
        n = x1v * _rms(x1v) * gm
        return x1v, n * (1.0 + scf) + shf

    x1, h2 = _rowwise(res1_fwd, [_row(xs), _row(mix), _bc(ga_a), _bc(g_mlp), _bc(sh_f), _bc(sc_f)],
                      [("row", F32), ("row", BF16)], nr=s_len // tr, tr=tr, nc=1, cw=d, name="norm2_fwd")
    relu2 = lambda z: jnp.square(jnp.maximum(z, 0.0))
    w_up_f = sibling_wait("w_up", sib_up, h2)
    up = _mm(h2, w_up_f, name="up_fwd")
    w_down_f = sibling_wait("w_down", sibling_start("w_down", up), up)
    down = _mm(up, w_down_f, a_fn=relu2, name="down_fwd")

    def final_fn(i, j, x1v, dn, tg, gaf, gf):
        x2 = x1v + gaf * dn
        r = _rms(x2)
        xh = x2 * r
        e = xh * gf - tg
        dy = e * (1.0 / d)
        dx2 = _rms_bwd(xh, r, dy * gf)
        return dx2, dx2 * gaf, _colsum(e * e) * (0.5 / d), _colsum(dy * xh), _colsum(dx2 * dn)

    dx2, ddown, loss_cols, g_gfinal, d_ga_f = _rowwise(
        final_fn, [_row(x1), _row(down), _row(tgt), _bc(ga_f), _bc(g_final2)],
        [("row", F32), ("row", BF16), ("acc", 1), ("acc", 1), ("acc", 1)], nr=s_len // tr, tr=tr, nc=1, cw=d,
        name="final_bwd")
    loss = lax.psum(jnp.sum(loss_cols), ("x", "y", "c"))

    dup = _mm(ddown, w_down_f, tb=True, out_fn=lambda r, upv: r * (2.0 * jnp.maximum(upv, 0.0)), extras=[up],
              out_dtype=BF16, name="down_bwd_x")
    gw_down = _mm(up, ddown, ta=True, a_fn=relu2, out_dtype=BF16, name="down_bwd_w")
    sent = dict(w_down=_scatter_start(gw_down, 0, my, "rs_start_w_down"))
    dh2 = _mm(dup, w_up_f, tb=True, name="up_bwd_x")
    gw_up = _mm(h2, dup, ta=True, out_dtype=BF16, name="up_bwd_w")
    sent["w_up"] = _scatter_start(gw_up, 1, my, "rs_start_w_up")
    sc_f_bwd = _tok(sc_f, sent["w_down"][0][4], sent["w_up"][0][4])

    def norm2_bwd(i, j, x1v, mixv, dh2v, dx2v, gaa, gm, scf):
        r = _rms(x1v)
        xh = x1v * r
        n = xh * gm
        dn = dh2v * (1.0 + scf)
        dx1 = dx2v + _rms_bwd(xh, r, dn * gm)
        return dx1, dx1 * gaa, _colsum(dh2v), _colsum(dh2v * n), _colsum(dn * xh), _colsum(dx1 * mixv)

    dx1, dmix, d_sh_f, d_sc_f, g_gmlp, d_ga_a = _rowwise(
        norm2_bwd, [_row(x1), _row(mix), _row(dh2), _row(dx2), _bc(ga_a), _bc(g_mlp), _bc(sc_f_bwd)],
        [("row", F32), ("row", BF16), ("acc", 1), ("acc", 1), ("acc", 1), ("acc", 1)], nr=s_len // tr, tr=tr, nc=1,
        cw=d, name="norm2_bwd")

    du = _mm(dmix, w_out_f, tb=True, name="out_bwd_x")
    gw_out = _mm(u, dmix, ta=True, out_dtype=BF16, name="out_bwd_w")
    sent["w_out"] = _scatter_start(gw_out, 0, my, "rs_start_w_out")

    def merge_bwd(i, j, duv, ya, yr, gla, glr):
        ga, gr = _sigmoid(gla), _sigmoid(glr)
        return duv * ga, duv * gr, duv * ya * ga * (1.0 - ga), duv * yr * gr * (1.0 - gr)

    dya, dyr, dgla, dglr = _rowwise(
        merge_bwd, [_row(du), _row(y_attn), _row(y_rnn), _row(proj, glb, lb), _row(proj, glb + 2, lb)],
        [("row", BF16)] * 4, nr=s_len // tr, tr=tr, nc=2, cw=hw, name="merge_bwd")
    dattn = _mm(dya, w_oa_f, tb=True, out_dtype=BF16, name="o_attn_bwd_x")
    gw_oa = _mm(attn_o, dya, ta=True, out_dtype=BF16, name="o_attn_bwd_w")
    drg = _mm(dyr, w_or_f, tb=True, out_dtype=BF16, name="o_rnn_bwd_x")
    gw_or = _mm(rg, dyr, ta=True, out_dtype=BF16, name="o_rnn_bwd_w")
    sent["w_o_attn"] = _scatter_start(gw_oa, 0, my, "rs_start_w_o_attn")
    sent["w_o_rnn"] = _scatter_start(gw_or, 0, my, "rs_start_w_o_rnn")
    conv_b_bwd = _tok(conv_b, sent["w_out"][0][4], sent["w_o_attn"][0][4], sent["w_o_rnn"][0][4])

    dq, dk, dv = _attn_bwd(q, k_all, v_all, dattn, tq=tq, name="attn_bwd")
    dxr, dxg, dvec, g_wrg, g_wig = _rglru_bwd(proj, xr_blk, xg_blk, shift_masks, h_fwd, h_bwd, drg, conv_w_f, conv_b_bwd, w_rg[0], w_ig[0], b_rg_f,
                                               b_ig_f, lam_f, l_len=l_len, name="rglru_bwd")
    gate_part = _pack_flat([g_wrg, g_wig], N_DEV * SUBLANES)
    gate_sent = _scatter_start(gate_part, 0, my, "rs_start_gates")
    q_gain_bwd = _tok(q_gain, gate_sent[0][4])

    dq_p, g_qgain = _rowwise(lambda i, j, xv, dov, gain, ce, se: _qk_bwd_heads(xv, dov, gain, ce, se),
                             [_row(proj, 0, lb, aw), _row(dq, 0, 0, aw), _full(q_gain_bwd), _row(cos_e, 0, lb, HEAD_DIM),
                              _row(sin_e, 0, lb, HEAD_DIM)],
                             [("row", BF16, aw), ("acc", 1, HEAD_DIM)], nr=s_len // tr, tr=tr, nc=1, cw=aw, name="q_bwd")

    def kv_bwd(i, j, xv, dkv, dvv, gain, ce, se):
        dk_p, dgain = _qk_bwd_heads(xv, dkv, gain, ce, se)
        return dk_p, dvv, dgain

    dk_p, dv_p, g_kgain = _rowwise(kv_bwd, [_row(proj, kb, 0, kvw), _row(dk, 0, 0, kvw), _row(dv, 0, 0, kvw),
                                            _full(k_gain), _row(cos_e, 0, 0, HEAD_DIM), _row(sin_e, 0, 0, HEAD_DIM)],
                                   [("row", BF16, kvw), ("row", BF16, kvw), ("acc", 1, HEAD_DIM)], nr=t_len // tr, tr=tr,
                                   nc=1, cw=kvw, name="kv_bwd")

    pad_ctx = lambda a: jnp.pad(a, ((l_len, 0), (0, 0)))
    dproj = jnp.concatenate([pad_ctx(dq_p), dk_p, dv_p, dxr, dxg, pad_ctx(dgla), pad_ctx(dglr)], axis=1)
    dh_all = _mm(dproj, w_in_f, tb=True, name="proj_bwd_x")
    g_mix_bwd = g_mix

    def norm1_bwd(i, j, xv, dhv, dresv, g, sc):
        r = _rms(xv)
        xh = xv * r
        dn = dhv * (1.0 + sc)
        return dresv + _rms_bwd(xh, r, dn * g), _colsum(dhv), _colsum(dhv * (xh * g)), _colsum(dn * xh)

    def norm1_bwd_ctx(i, j, xv, dhv, g, sc):
        xh = xv * _rms(xv)
        return _colsum(dhv), _colsum(dhv * (xh * g)), _colsum(dhv * (1.0 + sc) * xh)

    dx_lat, d_sh_a, d_sc_a, g_gmix_lat = _rowwise(
        norm1_bwd, [_row(x_all, 0, lb), _row(dh_all, 0, lb), _row(dx1), _bc(g_mix_bwd), _bc(sc_a)],
        [("row", F32)] + [("acc", 1)] * 3, nr=s_len // tr, tr=tr, nc=1, cw=d, name="norm1_bwd")
    d_csh, d_csc, g_gmix_ctx = _rowwise(
        norm1_bwd_ctx, [_row(x_all), _row(dh_all), _bc(g_mix_bwd), _bc(csc_a)],
        [("acc", 1)] * 3, nr=lb, tr=tr, nc=1, cw=d, name="norm1_bwd_ctx")
    g_gmix = g_gmix_lat + g_gmix_ctx
    grad_x = dx_lat[None]

    dmod_lat = jnp.concatenate([d_sh_a, d_sc_a, d_ga_a, d_sh_f, d_sc_f, d_ga_f], axis=1)
    dmod_ctx = jnp.concatenate([d_csh, d_csc, jnp.zeros((1, (N_MOD - 2) * d), F32)], axis=1)
    dm_g = _all_gather([jnp.concatenate([dmod_lat, dmod_ctx, jnp.zeros((SUBLANES - 2, N_MOD * d), F32)], axis=0)], [0],
                       "ag_dmod")[0]
    dm16 = jnp.concatenate([dm_g[0::SUBLANES], dm_g[1::SUBLANES]], axis=0)
    dm_sh = lax.dynamic_slice(dm16, (0, my * nmod), (2 * N_DEV, nmod))
    gw_mod = _mm(c16, dm_sh, ta=True, a_fn=_silu, name="mod_bwd_w")
    dc16 = _mm(dm_sh, w_mod[0], tb=True, name="mod_bwd_x")

    def cctx_fn(i, j, pv, cc, dl, dc):
        row = lax.broadcasted_iota(jnp.int32, pv.shape, 0)
        return _colsum(jnp.where(row >= N_DEV, pv, 0.0)) * _dsilu(cc), dl + dc

    g_cctx, g_bmod = _rowwise(cctx_fn, [_full(dc16), _full(c_ctx2), _full(dmod_lat), _full(dmod_ctx)],
                              [("row", F32, d), ("row", F32, N_MOD * d)], nr=1, tr=1, nc=1, cw=d, name="cctx_bwd")

    rep_part = dict(c_ctx=g_cctx, b_mod=g_bmod, g_mix=g_gmix, g_mlp=g_gmlp, q_gain=g_qgain, k_gain=g_kgain,
                    conv_b=dvec[4:5], g_final=g_gfinal)
    rep_small = [n for n in REPLICATED if n in rep_part]
    gates = [n for n in REPLICATED if n not in rep_part]
    part_pack = _pack_flat([rep_part[n] for n in rep_small], PACK_ROWS)
    rep_rows = part_pack.shape[0]
    gate_land = _xfer_wait("scatter", gate_sent[0], 0, g_bmod, "rs_wait_gates")
    gate_sum = _sum_parts(gate_sent[1], gate_land, name="gate_sum")
    (gate_full, parts), order = _all_gather([gate_sum, part_pack], [0, 0], "ag_rep_grads", with_token=True)
    small_g = jnp.concatenate([dvec[0:4], dvec[5:11], jnp.zeros((SMALL_ROWS - 10, d), F32)], axis=0)
    (small_recv,), order = _all_to_all([_tok(small_g, order)], [1], "a2a_small_grads")

    halves = {}
    for part, lo in (("top", 0), ("bot", d // 2)):
        gw_part = _mm(h_all[:, lo:lo + d // 2], dproj, ta=True, out_dtype=BF16, name="proj_bwd_w_" + part,
                      bias=_tok(jnp.zeros((1, n_in), F32), order))
        halves[part] = _chip_scatter_start(_pair_reduce(gw_part, my, "pair_w_in_" + part), my, "rs_start_w_in_" + part)
        order = halves[part][0][4]
    started = order

    res = {}
    for names, pr, mult, nm in ((rep_small, parts.reshape(N_DEV, rep_rows, LANES), PACK_ROWS, "adamw_rep"),
                                (gates, gate_full[None], N_DEV * SUBLANES, "adamw_gates")):
        rep_out = _adamw(pr, _tok(_pack_flat([wd[n] for n in names], mult), started),
                         _pack_flat([md[n] for n in names], mult), _pack_flat([vd[n] for n in names], mult), name=nm)
        for kind, packed in zip(("g", "d", "m", "v"), rep_out):
            for n, a in zip(names, _unpack_flat(packed, [wd[n].shape for n in names])):
                res[kind, n] = a
    out = _adamw(small_recv, _tok(sp_w, started), _pack_small(md), _pack_small(vd), name="adamw_small")
    for kind, a in zip(("g", "d", "m", "v"), out):
        for n, piece in _unpack_small(a).items():
            res[kind, n] = piece

    after = started
    for n, (handle, own, ax, kind) in sent.items():
        land = _xfer_wait(kind, handle, ax, after, "rs_wait_" + n)
        out = _adamw(land, wd[n][0], md[n][0], vd[n][0], own=own, name="adamw_" + n)
        after = out[0]
        for kind, a in zip(("g", "d", "m", "v"), out):
            res[kind, n] = a[None]
    out = _adamw(gw_mod[None], w_mod[0], m_w_mod[0], v_w_mod[0], name="adamw_w_mod")
    after = out[0]
    for kind, a in zip(("g", "d", "m", "v"), out):
        res[kind, 'w_mod'] = a[None]
    lands, owns = [], []
    for part, (handle, own, ax, kind) in halves.items():
        lands.append(_xfer_wait(kind, handle, ax, after, "rs_wait_w_in_" + part))
        owns.append(own)
        after = lands[-1]
    out = _adamw(jnp.concatenate(lands, axis=1), w_in[0], m_w_in[0], v_w_in[0], own=jnp.concatenate(owns, axis=0),
                 name="adamw_w_in")
    for kind, a in zip(("g", "d", "m", "v"), out):
        res[kind, 'w_in'] = a[None]

    outs = [loss, grad_x]
    for kind in ("g", "d", "m", "v"):
        outs += [res[kind, n].reshape(wd[n].shape) for n in WEIGHTS]
    return tuple(outs)
```

```python
import functools
import math

import jax
import jax.numpy as jnp
from jax import lax
from jax.experimental import pallas as pl
from jax.experimental.pallas import tpu as pltpu

F32 = jnp.float32
BF16 = jnp.bfloat16
MESH = pl.DeviceIdType.MESH

N_DEV = 8
HEAD_DIM = 128
Q_GROUP = 4
RNN_BLOCK = 128
GRID_W = 64
CONV_WIDTH = 4
LRU_C = 8.0
ROPE_THETA = 10000.0
NORM_EPS = 1e-6
N_MOD = 6
ADAM_LR = 0.001
ADAM_B1 = 0.9
ADAM_B2 = 0.999
ADAM_EPS = 1e-08
ADAM_WD = 0.01
ADAM_STEP = 10

LANES = 128
SUBLANES = 8
VMEM_LIMIT = 56 * 1024 * 1024
HBM_SPEC = pl.BlockSpec(memory_space=pltpu.HBM)


def _params(sem=None, vmem=VMEM_LIMIT):
    kw = dict(vmem_limit_bytes=vmem)
    if sem is not None:
        kw["dimension_semantics"] = sem
    return pltpu.CompilerParams(**kw)


def _pick(dim, target, mult):
    for t in range(min(dim, target), 0, -1):
        if dim % t == 0 and t % mult == 0:
            return t
    return dim


def _my_pos():
    return lax.axis_index("x"), lax.axis_index("y"), lax.axis_index("c")


def _flip(pos, d):
    x, y, c = pos
    return (1 - x if d & 4 else x, 1 - y if d & 2 else y, 1 - c if d & 1 else c)


def _lin(pos):
    return 4 * pos[0] + 2 * pos[1] + pos[2]


def _slab(ref, axis, idx, size):
    if axis == 0:
        return ref.at[pl.ds(idx * size, size)]
    assert axis == 1
    return ref.at[:, pl.ds(idx * size, size)]


def _all_gather(arrs, axes, name, with_token=False):
    n = len(arrs)
    sizes = [a.shape[ax] for a, ax in zip(arrs, axes)]

    def body(*refs):
        ins, outs, token = refs[:n], refs[n:2 * n], refs[2 * n]
        send_sems, recv_sems, local_sems = refs[2 * n + 1:]
        token[...] = jnp.zeros_like(token)
        me = _my_pos()
        sibling = _flip(me, 1)
        chips = [2, 4, 6]

        def copy(a, k, block_pos, to, src=None):
            dst = _slab(outs[a], axes[a], _lin(block_pos), sizes[a])
            return pltpu.make_async_remote_copy(
                src_ref=dst if src is None else src, dst_ref=dst,
                send_sem=send_sems.at[7 * a + k], recv_sem=recv_sems.at[7 * a + k],
                device_id=to, device_id_type=MESH)

        local, first, passed = [], [], []
        for a in range(n):
            cp = pltpu.make_async_copy(ins[a], _slab(outs[a], axes[a], _lin(me), sizes[a]), local_sems.at[a])
            cp.start()
            local.append(cp)
        for a in range(n):
            f = [copy(a, 0, me, sibling, src=ins[a])]
            f += [copy(a, 1 + j, me, _flip(me, m), src=ins[a]) for j, m in enumerate(chips)]
            for cp in f:
                cp.start()
            first += f
        for a in range(n):
            for j, m in enumerate(chips):
                copy(a, 1 + j, _flip(me, m), me).wait_recv()
                cp = copy(a, 4 + j, _flip(me, m), sibling)
                cp.start()
                passed.append(cp)
        for a in range(n):
            copy(a, 0, sibling, me).wait_recv()
            for j, m in enumerate(chips):
                copy(a, 4 + j, _flip(sibling, m), me).wait_recv()
        for cp in first + passed:
            cp.wait_send()
        for cp in local:
            cp.wait()

    out_shape = []
    for a, ax in zip(arrs, axes):
        shp = list(a.shape)
        shp[ax] *= N_DEV
        out_shape.append(jax.ShapeDtypeStruct(tuple(shp), a.dtype))
    out_shape.append(jax.ShapeDtypeStruct((SUBLANES, LANES), F32))
    res = pl.pallas_call(
        body, name=name, out_shape=tuple(out_shape),
        in_specs=[HBM_SPEC] * n, out_specs=tuple([HBM_SPEC] * n + [pl.BlockSpec(memory_space=pltpu.VMEM)]),
        scratch_shapes=[pltpu.SemaphoreType.DMA((7 * n,)), pltpu.SemaphoreType.DMA((7 * n,)),
                        pltpu.SemaphoreType.DMA((n,))],
    )(*arrs)
    return (list(res[:n]), res[n]) if with_token else list(res[:n])


def _gather_proj(h_all, w_shard, name):
    t_len, kdim = h_all.shape
    sz = w_shard.shape[1]
    tm = _pick(t_len, 1024, 16)
    n_m = t_len // tm

    def body(h_ref, w_ref, wfull_ref, proj_ref, token, a_buf, b_buf, o_buf, send_sems, recv_sems, io_sems, out_sems):
        token[...] = jnp.zeros_like(token)
        me = _my_pos()
        sibling = _flip(me, 1)
        chips = [2, 4, 6]
        slab = lambda pos: _slab(wfull_ref, 1, _lin(pos), sz)

        def copy(k, block_pos, to, src=None):
            return pltpu.make_async_remote_copy(
                src_ref=slab(block_pos) if src is None else src, dst_ref=slab(block_pos),
                send_sem=send_sems.at[k], recv_sem=recv_sems.at[k], device_id=to, device_id_type=MESH)

        pending = []

        def block_product(pos, b_src):
            load = pltpu.make_async_copy(b_src, b_buf, io_sems.at[1])
            load.start()
            load.wait()
            for i in range(n_m):
                slot = len(pending) % 2
                if len(pending) >= 2:
                    pending[-2].wait()
                o_buf[slot] = _dot(a_buf[pl.ds(i * tm, tm), :], b_buf[...], 1, 0)
                out = pltpu.make_async_copy(
                    o_buf.at[slot], proj_ref.at[pl.ds(i * tm, tm), pl.ds(_lin(pos) * sz, sz)], out_sems.at[slot])
                out.start()
                pending.append(out)

        local = pltpu.make_async_copy(w_ref, slab(me), io_sems.at[0])
        local.start()
        first = [copy(0, me, sibling, src=w_ref)] + [copy(1 + j, me, _flip(me, m), src=w_ref) for j, m in enumerate(chips)]
        for cp in first:
            cp.start()
        load_a = pltpu.make_async_copy(h_ref, a_buf, io_sems.at[2])
        load_a.start()
        load_a.wait()
        block_product(me, w_ref)
        copy(0, sibling, me).wait_recv()
        block_product(sibling, slab(sibling))
        passed = []
        for j, m in enumerate(chips):
            src_pos = _flip(me, m)
            copy(1 + j, src_pos, me).wait_recv()
            cp = copy(4 + j, src_pos, sibling)
            cp.start()
            passed.append(cp)
            block_product(src_pos, slab(src_pos))
        for j, m in enumerate(chips):
            src_pos = _flip(sibling, m)
            copy(4 + j, src_pos, me).wait_recv()
            block_product(src_pos, slab(src_pos))
        for out in pending[-2:]:
            out.wait()
        for cp in first + passed:
            cp.wait_send()
        local.wait()

    return pl.pallas_call(
        body, name=name,
        out_shape=(jax.ShapeDtypeStruct((kdim, N_DEV * sz), w_shard.dtype), jax.ShapeDtypeStruct((t_len, N_DEV * sz), F32),
                   jax.ShapeDtypeStruct((SUBLANES, LANES), F32)),
        in_specs=[HBM_SPEC, HBM_SPEC], out_specs=(HBM_SPEC, HBM_SPEC, pl.BlockSpec(memory_space=pltpu.VMEM)),
        scratch_shapes=[pltpu.VMEM((t_len, kdim), h_all.dtype), pltpu.VMEM((kdim, sz), w_shard.dtype),
                        pltpu.VMEM((2, tm, sz), F32), pltpu.SemaphoreType.DMA((7,)), pltpu.SemaphoreType.DMA((7,)),
                        pltpu.SemaphoreType.DMA((3,)), pltpu.SemaphoreType.DMA((2,))],
        compiler_params=_params(),
    )(h_all, w_shard)


def _all_to_all(arrs, axes, name):
    n = len(arrs)
    sizes = [a.shape[ax] // N_DEV for a, ax in zip(arrs, axes)]

    def body(*refs):
        ins, outs, token = refs[:n], refs[n:2 * n], refs[2 * n]
        send_sems, recv_sems, local_sems = refs[2 * n + 1:]
        token[...] = jnp.zeros_like(token)
        me = _my_pos()
        my = _lin(me)
        copies, local = [], []
        for a in range(n):
            cp = pltpu.make_async_copy(_slab(ins[a], axes[a], my, sizes[a]), outs[a].at[my], local_sems.at[a])
            cp.start()
            local.append(cp)
        for d in range(1, N_DEV):
            peer = _flip(me, d)
            for a in range(n):
                cp = pltpu.make_async_remote_copy(
                    src_ref=_slab(ins[a], axes[a], _lin(peer), sizes[a]), dst_ref=outs[a].at[my],
                    send_sem=send_sems.at[7 * a + d - 1], recv_sem=recv_sems.at[7 * a + d - 1],
                    device_id=peer, device_id_type=MESH)
                cp.start()
                copies.append((a, d, cp))
        for a, d, cp in copies:
            peer = _flip(me, d)
            pltpu.make_async_remote_copy(
                src_ref=_slab(ins[a], axes[a], _lin(peer), sizes[a]), dst_ref=outs[a].at[_lin(peer)],
                send_sem=send_sems.at[7 * a + d - 1], recv_sem=recv_sems.at[7 * a + d - 1],
                device_id=peer, device_id_type=MESH).wait_recv()
            cp.wait_send()
        for cp in local:
            cp.wait()

    out_shape = []
    for a, ax, s in zip(arrs, axes, sizes):
        shp = list(a.shape)
        shp[ax] = s
        out_shape.append(jax.ShapeDtypeStruct((N_DEV,) + tuple(shp), a.dtype))
    out_shape.append(jax.ShapeDtypeStruct((SUBLANES, LANES), F32))
    res = pl.pallas_call(
        body, name=name, out_shape=tuple(out_shape),
        in_specs=[HBM_SPEC] * n, out_specs=tuple([HBM_SPEC] * n + [pl.BlockSpec(memory_space=pltpu.VMEM)]),
        scratch_shapes=[pltpu.SemaphoreType.DMA((7 * n,)), pltpu.SemaphoreType.DMA((7 * n,)),
                        pltpu.SemaphoreType.DMA((n,))],
    )(*arrs)
    return list(res[:n]), res[n]


SEM_SPEC = pl.BlockSpec(memory_space=pltpu.SEMAPHORE)
_EFFECT = pltpu.SideEffectType.DATAFLOW_SIDE_EFFECTING
N_PEERS = N_DEV - 1


def _split_copies(kind, src_ref, land_ref, send_sems, recv_sems, axis, size, arriving):
    me = _my_pos()
    out = []
    if kind == "sibling":
        sibling = _flip(me, 1)
        for k in range(N_CHIPS):
            blk = _slab(land_ref, axis, 2 * k + (sibling[2] if arriving else me[2]), size)
            out.append(pltpu.make_async_remote_copy(src_ref=blk, dst_ref=blk, send_sem=send_sems.at[k],
                                                    recv_sem=recv_sems.at[k], device_id=sibling, device_id_type=MESH))
        return out
    for dd in (range(2, N_DEV, 2) if kind in ("chips", "gather_chips") else range(1, N_DEV)):
        peer = _flip(me, dd)
        k = dd - 1
        if kind in ("gather", "gather_chips"):
            src = _slab(land_ref, axis, _lin(me), size)
            dst = _slab(land_ref, axis, _lin(peer if arriving else me), size)
        elif kind == "scatter":
            src, dst = _slab(src_ref, axis, _lin(peer), size), land_ref.at[k]
        else:
            k = dd // 2 - 1
            src, dst = src_ref.at[2 * peer[0] + peer[1]], land_ref.at[k]
        out.append(pltpu.make_async_remote_copy(
            src_ref=src, dst_ref=dst, send_sem=send_sems.at[k], recv_sem=recv_sems.at[k],
            device_id=peer, device_id_type=MESH))
    return out


N_CHIPS = N_DEV // 2


def _split_size(kind, src_shape, land_shape, axis):
    if kind in ("gather", "gather_chips", "sibling"):
        return land_shape[axis] // N_DEV
    return src_shape[axis] // {"scatter": N_DEV, "chips": N_CHIPS}[kind]


def _xfer_start(kind, src, land, axis, name):
    size = _split_size(kind, src.shape, land.shape, axis)

    def body(src_ref, land_ref, send_sems, recv_sems, src_thru, land_thru, token):
        for cp in _split_copies(kind, src_ref, land_ref, send_sems, recv_sems, axis, size, False):
            cp.start()
        token[...] = jnp.zeros_like(token)

    return pl.pallas_call(
        body, name=name,
        out_shape=(pltpu.SemaphoreType.DMA((N_PEERS,)), pltpu.SemaphoreType.DMA((N_PEERS,)),
                   pltpu.HBM(src.shape, src.dtype), pltpu.HBM(land.shape, land.dtype),
                   jax.ShapeDtypeStruct((SUBLANES, LANES), F32)),
        in_specs=(HBM_SPEC, HBM_SPEC),
        out_specs=(SEM_SPEC, SEM_SPEC, HBM_SPEC, HBM_SPEC, pl.BlockSpec(memory_space=pltpu.VMEM)),
        input_output_aliases={0: 2, 1: 3},
        compiler_params=pltpu.CompilerParams(has_side_effects=_EFFECT),
    )(pltpu.with_memory_space_constraint(src, pltpu.HBM), pltpu.with_memory_space_constraint(land, pltpu.HBM))


def _xfer_wait(kind, handle, axis, after, name):
    send_sems, recv_sems, src_thru, land_thru, _ = handle
    size = _split_size(kind, src_thru.shape, land_thru.shape, axis)

    def body(src_ref, land_ref, send_sems, recv_sems, after_ref, src_dead, got_ref):
        for cp in _split_copies(kind, src_ref, land_ref, send_sems, recv_sems, axis, size, True):
            cp.wait_send()
            cp.wait_recv()

    return pl.pallas_call(
        body, name=name,
        out_shape=(pltpu.HBM(src_thru.shape, src_thru.dtype), pltpu.HBM(land_thru.shape, land_thru.dtype)),
        in_specs=(HBM_SPEC, HBM_SPEC, SEM_SPEC, SEM_SPEC, pl.BlockSpec(memory_space=pl.ANY)),
        out_specs=(HBM_SPEC, HBM_SPEC), input_output_aliases={0: 0, 1: 1},
        compiler_params=pltpu.CompilerParams(has_side_effects=_EFFECT),
    )(src_thru, land_thru, send_sems, recv_sems, after)[1]


def _place_block(shard, axis, my, name):
    r, c = shard.shape
    tr = _pick(r, max(16, (1 << 19) // c), 16)
    nr = r // tr
    if axis == 0:
        full, out_map = (r * N_DEV, c), (lambda i, my_ref: (my_ref[0] * nr + i, 0))
    else:
        full, out_map = (r, c * N_DEV), (lambda i, my_ref: (i, my_ref[0]))

    def body(my_ref, x_ref, o_ref):
        o_ref[...] = x_ref[...].astype(BF16)

    return pl.pallas_call(
        body, name=name, out_shape=jax.ShapeDtypeStruct(full, BF16),
        grid_spec=pltpu.PrefetchScalarGridSpec(
            num_scalar_prefetch=1, grid=(nr,), in_specs=[pl.BlockSpec((tr, c), lambda i, my_ref: (i, 0))],
            out_specs=pl.BlockSpec((tr, c), out_map)),
        compiler_params=_params(("arbitrary",)),
    )(jnp.reshape(my, (1,)).astype(jnp.int32), shard)


def _scatter_start(grad, axis, my, name):
    shp = list(grad.shape)
    shp[axis] //= N_DEV
    own = lax.dynamic_slice_in_dim(grad, my * shp[axis], shp[axis], axis)
    return (_xfer_start("scatter", grad, lax.empty((N_PEERS,) + tuple(shp), grad.dtype), axis, name), own, axis,
            "scatter")


def _pair_reduce(grad, my, name):
    r, cols = grad.shape
    sz = cols // N_DEV

    def exchange(g_ref, got_ref, send_sems, recv_sems):
        me = _my_pos()
        sibling = _flip(me, 1)
        cps = []
        for k in range(N_CHIPS):
            cp = pltpu.make_async_remote_copy(
                src_ref=_slab(g_ref, 1, 2 * k + sibling[2], sz), dst_ref=_slab(got_ref, 1, k, sz),
                send_sem=send_sems.at[k], recv_sem=recv_sems.at[k], device_id=sibling, device_id_type=MESH)
            cp.start()
            cps.append(cp)
        for cp in cps:
            cp.wait()

    got = pl.pallas_call(
        exchange, name=name + "_swap", out_shape=jax.ShapeDtypeStruct((r, N_CHIPS * sz), grad.dtype),
        in_specs=[HBM_SPEC], out_specs=HBM_SPEC,
        scratch_shapes=[pltpu.SemaphoreType.DMA((N_CHIPS,)), pltpu.SemaphoreType.DMA((N_CHIPS,))],
    )(grad)

    tr = _pick(r, 512, 16)

    def add(c_ref, g_ref, p_ref, o_ref):
        o_ref[...] = (g_ref[...].astype(F32) + p_ref[...].astype(F32)).astype(o_ref.dtype)

    def add(c_ref, g_ref, p_ref, o_ref):
        o_ref[0] = (g_ref[...].astype(F32) + p_ref[...].astype(F32)).astype(o_ref.dtype)

    return pl.pallas_call(
        add, name=name + "_sum", out_shape=jax.ShapeDtypeStruct((N_CHIPS, r, sz), grad.dtype),
        grid_spec=pltpu.PrefetchScalarGridSpec(
            num_scalar_prefetch=1, grid=(N_CHIPS, r // tr),
            in_specs=[pl.BlockSpec((tr, sz), lambda k, i, c_ref: (i, 2 * k + c_ref[0])),
                      pl.BlockSpec((tr, sz), lambda k, i, c_ref: (i, k))],
            out_specs=pl.BlockSpec((1, tr, sz), lambda k, i, c_ref: (k, i, 0))),
        compiler_params=_params(("parallel", "parallel")),
    )(jnp.reshape(my % 2, (1,)).astype(jnp.int32), grad, got)


def _chip_scatter_start(pair_sum, my, name):
    _, r, sz = pair_sum.shape
    own = lax.dynamic_index_in_dim(pair_sum, my // 2, 0, keepdims=False)
    return (_xfer_start("chips", pair_sum, lax.empty((N_CHIPS - 1, r, sz), pair_sum.dtype), 0, name), own, 0, "chips")


def _tok(a, *tokens):
    for t in tokens:
        a = a + t[0, 0].astype(a.dtype)
    return a


def _mm(a, b, *, name, ta=False, tb=False, out_dtype=F32, a_fn=None, out_fn=None, extras=(), bias=None,
        tm=1024, tn=1024, tk=2816):
    if ta:
        kdim, m = a.shape
    else:
        m, kdim = a.shape
    if tb:
        n, kb = b.shape
    else:
        kb, n = b.shape
    assert kb == kdim, (a.shape, b.shape, ta, tb)
    tm = _pick(m, tm, LANES if ta else SUBLANES)
    tn = _pick(n, tn, LANES)
    tk = _pick(kdim, tk, LANES)
    nk = kdim // tk
    ne = len(extras)
    has_bias = bias is not None

    def body(*refs):
        a_ref, b_ref = refs[0], refs[1]
        pos = 2
        bias_ref = None
        if has_bias:
            bias_ref = refs[pos]
            pos += 1
        extra_refs = refs[pos:pos + ne]
        o_ref = refs[pos + ne]
        av = a_ref[...]
        if a_fn is not None:
            av = a_fn(av)
        dn = (((0 if ta else 1,), (1 if tb else 0,)), ((), ()))
        part = lax.dot_general(av.astype(BF16), b_ref[...].astype(BF16), dn, preferred_element_type=F32)

        def finish(r):
            if has_bias:
                r = r + bias_ref[...]
            if out_fn is not None:
                r = out_fn(r, *[e[...] for e in extra_refs])
            o_ref[...] = r.astype(out_dtype)

        if nk == 1:
            finish(part)
            return
        acc = refs[pos + ne + 1]
        k = pl.program_id(2)

        @pl.when(k == 0)
        def _():
            acc[...] = part

        @pl.when(k > 0)
        def _():
            acc[...] += part

        @pl.when(k == nk - 1)
        def _():
            finish(acc[...])

    in_specs = [
        pl.BlockSpec((tk, tm), lambda i, j, k: (k, i)) if ta else pl.BlockSpec((tm, tk), lambda i, j, k: (i, k)),
        pl.BlockSpec((tn, tk), lambda i, j, k: (j, k)) if tb else pl.BlockSpec((tk, tn), lambda i, j, k: (k, j)),
    ]
    args = [a, b]
    if has_bias:
        in_specs.append(pl.BlockSpec((1, tn), lambda i, j, k: (0, j)))
        args.append(bias)
    for e in extras:
        in_specs.append(pl.BlockSpec((tm, tn), lambda i, j, k: (i, j)))
        args.append(e)
    return pl.pallas_call(
        body, name=name, grid=(m // tm, n // tn, nk),
        in_specs=in_specs, out_specs=pl.BlockSpec((tm, tn), lambda i, j, k: (i, j)),
        out_shape=jax.ShapeDtypeStruct((m, n), out_dtype),
        scratch_shapes=[pltpu.VMEM((tm, tn), F32)] if nk > 1 else [],
        compiler_params=_params(("parallel", "parallel", "arbitrary")),
    )(*args)


def _row(arr, cb=0, rb=0, w=None):
    return ("row", arr, cb, rb, w)


def _bc(arr, cb=0, w=None):
    return ("bc", arr, cb, 0, w)


def _full(arr):
    return ("full", arr, 0, 0, None)


def _rowwise(fn, ins, outs, *, nr, tr, nc, cw, name):
    n_in = len(ins)
    outs = [(o[0], o[1], (o[2] if len(o) > 2 and o[2] else cw)) for o in outs]

    def body(*refs):
        in_refs, out_refs = refs[:n_in], refs[n_in:]
        j, i = pl.program_id(0), pl.program_id(1)
        res = fn(i, j, *[r[...] for r in in_refs])
        if not isinstance(res, (tuple, list)):
            res = (res,)
        for (kind, _, _), o_ref, r in zip(outs, out_refs, res):
            if kind == "row":
                o_ref[...] = r.astype(o_ref.dtype)
            else:
                @pl.when(i == 0)
                def _(o_ref=o_ref):
                    o_ref[...] = jnp.zeros_like(o_ref)

                o_ref[...] += r

    in_specs, args = [], []
    for kind, arr, cb, rb, w in ins:
        w = w or cw
        if kind == "row":
            in_specs.append(pl.BlockSpec((tr, w), lambda j, i, cb=cb, rb=rb: (jnp.maximum(i + rb, 0), j + cb)))
        elif kind == "bc":
            in_specs.append(pl.BlockSpec((arr.shape[0], w), lambda j, i, cb=cb: (0, j + cb)))
        else:
            in_specs.append(pl.BlockSpec(arr.shape, lambda j, i, nd=arr.ndim: (0,) * nd))
        args.append(arr)
    out_specs, out_shape = [], []
    for kind, spec, w in outs:
        if kind == "row":
            out_specs.append(pl.BlockSpec((tr, w), lambda j, i: (i, j)))
            out_shape.append(jax.ShapeDtypeStruct((nr * tr, nc * w), spec))
        else:
            out_specs.append(pl.BlockSpec((spec, w), lambda j, i: (0, j)))
            out_shape.append(jax.ShapeDtypeStruct((spec, nc * w), F32))
    res = pl.pallas_call(
        body, name=name, grid=(nc, nr), in_specs=in_specs, out_specs=tuple(out_specs), out_shape=tuple(out_shape),
        compiler_params=_params(("parallel", "arbitrary")),
    )(*args)
    return res


def _sum_parts(own, parts, *, name):
    p, r, c = parts.shape
    tr = _pick(r, max(SUBLANES, (1 << 18) // c), SUBLANES)

    def body(o_ref, p_ref, g_out):
        g = o_ref[...].astype(F32)
        for q in range(p):
            g = g + p_ref[q].astype(F32)
        g_out[...] = g

    blk = pl.BlockSpec((tr, c), lambda i: (i, 0))
    return pl.pallas_call(
        body, name=name, grid=(r // tr,), in_specs=[blk, pl.BlockSpec((p, tr, c), lambda i: (0, i, 0))],
        out_specs=blk, out_shape=jax.ShapeDtypeStruct((r, c), F32), compiler_params=_params(("parallel",)),
    )(own, parts)


def _adamw(parts, w, m, v, *, name, own=None):
    p, r, c = parts.shape
    tr = _pick(r, max(SUBLANES, (1 << 18) // c), 16 if parts.dtype == BF16 else SUBLANES)
    bc1 = 1.0 - ADAM_B1 ** ADAM_STEP
    bc2 = 1.0 - ADAM_B2 ** ADAM_STEP
    has_own = own is not None

    def body(*refs):
        p_ref, w_ref, m_ref, v_ref = refs[:4]
        g_out, d_out, m_out, v_out = refs[4 + has_own:]
        if has_own:
            g = refs[4][...].astype(F32) + p_ref[0].astype(F32)
        else:
            g = p_ref[0].astype(F32)
        for q in range(1, p):
            g = g + p_ref[q].astype(F32)
        m_new = ADAM_B1 * m_ref[...] + (1.0 - ADAM_B1) * g
        v_new = ADAM_B2 * v_ref[...] + (1.0 - ADAM_B2) * (g * g)
        m_hat = m_new / bc1
        v_hat = v_new / bc2
        g_out[...] = g
        d_out[...] = -ADAM_LR * (m_hat / (jnp.sqrt(v_hat) + ADAM_EPS) + ADAM_WD * w_ref[...])
        m_out[...] = m_new
        v_out[...] = v_new

    blk = pl.BlockSpec((tr, c), lambda i: (i, 0))
    return pl.pallas_call(
        body, name=name, grid=(r // tr,),
        in_specs=[pl.BlockSpec((p, tr, c), lambda i: (0, i, 0)), blk, blk, blk] + ([blk] if has_own else []),
        out_specs=(blk, blk, blk, blk), out_shape=tuple(jax.ShapeDtypeStruct((r, c), F32) for _ in range(4)),
        compiler_params=_params(("parallel",)),
    )(*([parts, w, m, v] + ([own] if has_own else [])))


_GELU_C = math.sqrt(2.0 / math.pi)
_GELU_K = 0.044715


def _sigmoid(z):
    return 0.5 * jnp.tanh(0.5 * z) + 0.5


def _silu(z):
    return z * _sigmoid(z)


def _dsilu(z):
    s = _sigmoid(z)
    return s * (1.0 + z * (1.0 - s))


def _gelu(z):
    return 0.5 * z * (1.0 + jnp.tanh(_GELU_C * (z + _GELU_K * z * z * z)))


def _gelu_pair(z):
    zz = z * z
    t = jnp.tanh(_GELU_C * (z * (1.0 + _GELU_K * zz)))
    half = 0.5 * (1.0 + t)
    return z * half, half + 0.5 * z * (1.0 - t * t) * (_GELU_C * (1.0 + 3.0 * _GELU_K * zz))


def _softplus(z):
    return jnp.maximum(z, 0.0) + jnp.log(1.0 + jnp.exp(-jnp.abs(z)))


def _rms(xv):
    return lax.rsqrt(jnp.mean(xv * xv, axis=-1, keepdims=True) + NORM_EPS)


def _rms_bwd(xhat, r, dxhat):
    return r * (dxhat - xhat * jnp.mean(dxhat * xhat, axis=-1, keepdims=True))


def _swap_pairs(y):
    lane = lax.broadcasted_iota(jnp.int32, y.shape, 1)
    return jnp.where(lane % 2 == 0, pltpu.roll(y, LANES - 1, 1), pltpu.roll(y, 1, 1))


def _dot(a, b, ca, cb):
    return lax.dot_general(a, b, (((ca,), (cb,)), ((), ())), preferred_element_type=F32)


_SM_SCALE = HEAD_DIM ** -0.5
_EXP2_SCALE = _SM_SCALE * math.log2(math.e)


def _softmax_parts(qh, kk):
    s = _dot(qh, kk, 1, 1)
    p = jnp.exp2((s - jnp.max(s, axis=-1, keepdims=True)) * _EXP2_SCALE)
    return p, 1.0 / jnp.sum(p, axis=-1, keepdims=True)


def _attn_fwd(q, k, v, *, tq, name):
    s_len, aw = q.shape
    t_len, kvw = k.shape
    gw = Q_GROUP * HEAD_DIM

    def body(q_ref, k_ref, v_ref, o_ref):
        kk, vv = k_ref[...], v_ref[...]
        for g in range(Q_GROUP):
            cols = slice(g * HEAD_DIM, (g + 1) * HEAD_DIM)
            p, inv_l = _softmax_parts(q_ref[:, cols], kk)
            o_ref[:, cols] = (_dot(p.astype(BF16), vv, 1, 0) * inv_l).astype(BF16)

    return pl.pallas_call(
        body, name=name, grid=(kvw // HEAD_DIM, s_len // tq),
        in_specs=[pl.BlockSpec((tq, gw), lambda h, i: (i, h)), pl.BlockSpec((t_len, HEAD_DIM), lambda h, i: (0, h)),
                  pl.BlockSpec((t_len, HEAD_DIM), lambda h, i: (0, h))],
        out_specs=pl.BlockSpec((tq, gw), lambda h, i: (i, h)),
        out_shape=jax.ShapeDtypeStruct((s_len, aw), BF16),
        compiler_params=_params(("parallel", "parallel")),
    )(q, k, v)


def _attn_bwd(q, k, v, do, *, tq, name):
    s_len, aw = q.shape
    t_len, kvw = k.shape
    gw = Q_GROUP * HEAD_DIM

    def body(q_ref, k_ref, v_ref, do_ref, dq_ref, dk_ref, dv_ref):
        @pl.when(pl.program_id(1) == 0)
        def _():
            dk_ref[...] = jnp.zeros_like(dk_ref)
            dv_ref[...] = jnp.zeros_like(dv_ref)

        kk, vv = k_ref[...], v_ref[...]
        for g in range(Q_GROUP):
            cols = slice(g * HEAD_DIM, (g + 1) * HEAD_DIM)
            qh, doh = q_ref[:, cols], do_ref[:, cols]
            p, inv_l = _softmax_parts(qh, kk)
            dp = _dot(doh, vv, 1, 1)
            delta = jnp.sum(p * dp, axis=-1, keepdims=True) * inv_l
            dsb = (p * ((dp - delta) * (inv_l * _SM_SCALE))).astype(BF16)
            dq_ref[:, cols] = _dot(dsb, kk, 1, 0)
            dk_ref[...] += _dot(dsb, qh, 0, 0)
            dv_ref[...] += _dot(p.astype(BF16), (doh * inv_l).astype(BF16), 0, 0)

    qspec = pl.BlockSpec((tq, gw), lambda h, i: (i, h))
    kspec = pl.BlockSpec((t_len, HEAD_DIM), lambda h, i: (0, h))
    return pl.pallas_call(
        body, name=name, grid=(kvw // HEAD_DIM, s_len // tq),
        in_specs=[qspec, kspec, kspec, qspec], out_specs=(qspec, kspec, kspec),
        out_shape=(jax.ShapeDtypeStruct((s_len, aw), F32), jax.ShapeDtypeStruct((t_len, kvw), F32),
                   jax.ShapeDtypeStruct((t_len, kvw), F32)),
        compiler_params=_params(("parallel", "arbitrary")),
    )(q, k, v, do)


SHIFTS = (-2, -1, 1, 2)


def _shift_masks(l_len, t_len):
    t = jnp.arange(t_len)[:, None]
    cols = []
    for d in SHIFTS:
        td = t + d
        ok = ((t < l_len) & (td >= 0) & (td < l_len)) | ((t >= l_len) & (td >= l_len) & (td < t_len))
        cols.append(jnp.broadcast_to(ok.astype(F32), (t_len, RNN_BLOCK)))
    return jnp.concatenate(cols, axis=1)


def _shift_rows(xv, d, m_ref, l_len, t_len):
    if d == 0:
        return xv
    k = SHIFTS.index(d)
    return pltpu.roll(xv, (-d) % t_len, 0) * m_ref[:, k * RNN_BLOCK:(k + 1) * RNN_BLOCK]


def _sublane_scan(a8, b8, rev):
    row = lax.broadcasted_iota(jnp.int32, a8.shape, 0)
    for s in (1, 2, 4):
        if rev:
            ok = row < SUBLANES - s
            sh = SUBLANES - s
        else:
            ok = row >= s
            sh = s
        a_sh = jnp.where(ok, pltpu.roll(a8, sh, 0), 1.0)
        b_sh = jnp.where(ok, pltpu.roll(b8, sh, 0), 0.0)
        b8 = a8 * b_sh + b8
        a8 = a8 * a_sh
    return a8, b8


def _scan_chunks(chains, c0, n, carries):
    def step(kk, hs):
        new = []
        for (a_ref, b_ref, h_ref, rev), hprev in zip(chains, hs):
            c = c0 + (n - 1 - kk if rev else kk)
            r0 = pl.multiple_of(c * SUBLANES, SUBLANES)
            big_a, big_b = _sublane_scan(a_ref[pl.ds(r0, SUBLANES), :], b_ref[pl.ds(r0, SUBLANES), :], rev)
            h8 = big_b + big_a * hprev
            h_ref[pl.ds(r0, SUBLANES), :] = h8
            edge = 0 if rev else SUBLANES - 1
            new.append(jnp.broadcast_to(h8[edge:edge + 1, :], h8.shape))
        return tuple(new)

    return lax.fori_loop(0, n, step, tuple(carries))


def _rglru_conv(x, cw_ref, cb_ref, m_ref, l_len, t_len):
    y = cb_ref[...] + _shift_rows(x, -1, m_ref, l_len, t_len) * cw_ref[0:1, :]
    for kk in range(1, CONV_WIDTH):
        y = y + _shift_rows(x, kk - 1, m_ref, l_len, t_len) * cw_ref[kk:kk + 1, :]
    return y


def _rglru_gates(xc, wr, wi, br, bi, lam):
    xb = xc.astype(BF16)
    r = _sigmoid(_dot(xb, wr.astype(BF16), 1, 0) + br)
    ig = _sigmoid(_dot(xb, wi.astype(BF16), 1, 0) + bi)
    sp = _softplus(-lam)
    log_a = (-LRU_C) * r * sp
    a = jnp.exp(log_a)
    z2 = 2.0 * log_a
    series = z2 * (1.0 + z2 * (0.5 + z2 * (1.0 / 6.0 + z2 * (1.0 / 24.0))))
    one_m = jnp.where(jnp.abs(z2) < 0.03, -series, 1.0 - a * a)
    rinv = lax.rsqrt(jnp.maximum(one_m, 1e-30))
    u = ig * xc
    return r, ig, sp, a, one_m * rinv, u, rinv


def _rglru_specs(proj, xr_blk, xg_blk, t_len):
    col = lambda off: pl.BlockSpec((t_len, RNN_BLOCK), lambda h, off=off: (0, off + h))
    vec = lambda rows: pl.BlockSpec((rows, RNN_BLOCK), lambda h: (0, h))
    wsp = pl.BlockSpec((2, 1, RNN_BLOCK, RNN_BLOCK), lambda h: (0, h, 0, 0))
    msp = pl.BlockSpec((t_len, len(SHIFTS) * RNN_BLOCK), lambda h: (0, 0))
    return col(xr_blk), col(xg_blk), vec, wsp, msp


def _rglru_fwd(proj, xr_blk, xg_blk, masks, conv_w, conv_b, w_rg, w_ig, b_rg, b_ig, lam, *, l_len, name):
    t_len = proj.shape[0]
    s_len = t_len - l_len
    d_rnn = conv_b.shape[1]
    n_l, n_t = l_len // SUBLANES, t_len // SUBLANES

    def body(xr_ref, xg_ref, m_ref, cw_ref, cb_ref, wr_ref, wi_ref, br_ref, bi_ref, lam_ref, rg_ref, hf_ref, hb_ref,
             a_s, b_s):
        xc = _rglru_conv(xr_ref[...], cw_ref, cb_ref, m_ref, l_len, t_len)
        for d in range(2):
            _, _, _, a, mult, u, _ = _rglru_gates(xc, wr_ref[d, 0], wi_ref[d, 0], br_ref[d:d + 1, :],
                                               bi_ref[d:d + 1, :], lam_ref[d:d + 1, :])
            a_s[d] = a
            b_s[d] = mult * u
        chains = [(a_s.at[0], b_s.at[0], hf_ref, False), (a_s.at[1], b_s.at[1], hb_ref, True)]
        zero = jnp.zeros((SUBLANES, RNN_BLOCK), F32)
        carry = _scan_chunks(chains, 0, n_l, (zero, zero))
        _scan_chunks(chains, n_l, n_t - n_l, carry)
        rnn = hf_ref[pl.ds(l_len, s_len), :] + hb_ref[pl.ds(l_len, s_len), :]
        rg_ref[...] = (rnn * _gelu(xg_ref[pl.ds(l_len, s_len), :])).astype(BF16)

    xr_spec, xg_spec, vec, wsp, msp = _rglru_specs(proj, xr_blk, xg_blk, t_len)
    col_t = pl.BlockSpec((t_len, RNN_BLOCK), lambda h: (0, h))
    return pl.pallas_call(
        body, name=name, grid=(d_rnn // RNN_BLOCK,),
        in_specs=[xr_spec, xg_spec, msp, vec(CONV_WIDTH), vec(1), wsp, wsp, vec(2), vec(2), vec(2)],
        out_specs=(pl.BlockSpec((s_len, RNN_BLOCK), lambda h: (0, h)), col_t, col_t),
        out_shape=(jax.ShapeDtypeStruct((s_len, d_rnn), BF16), jax.ShapeDtypeStruct((t_len, d_rnn), F32),
                   jax.ShapeDtypeStruct((t_len, d_rnn), F32)),
        scratch_shapes=[pltpu.VMEM((2, t_len, RNN_BLOCK), F32)] * 2,
        compiler_params=_params(("parallel",)),
    )(proj, proj, masks, conv_w, conv_b, w_rg, w_ig, b_rg, b_ig, lam)


N_RNN_VEC = 16


def _rglru_bwd(proj, xr_blk, xg_blk, masks, h_fwd, h_bwd, drg, conv_w, conv_b, w_rg, w_ig, b_rg, b_ig, lam, *, l_len, name):
    t_len = proj.shape[0]
    s_len = t_len - l_len
    d_rnn = conv_b.shape[1]
    n_l, n_t = l_len // SUBLANES, t_len // SUBLANES

    def body(xr_ref, xg_ref, m_ref, hf_ref, hb_ref, drg_ref, cw_ref, cb_ref, wr_ref, wi_ref, br_ref, bi_ref, lam_ref,
             dxr_ref, dxg_ref, dvec_ref, dwr_ref, dwi_ref, a_s, b_s, h_s):
        t = lax.broadcasted_iota(jnp.int32, (t_len, RNN_BLOCK), 0)
        x = xr_ref[...]
        xc = _rglru_conv(x, cw_ref, cb_ref, m_ref, l_len, t_len)
        xb = xc.astype(BF16)
        gates = []
        for d in range(2):
            gt = _rglru_gates(xc, wr_ref[d, 0], wi_ref[d, 0], br_ref[d:d + 1, :], bi_ref[d:d + 1, :],
                              lam_ref[d:d + 1, :])
            gates.append(gt)
        zero = jnp.zeros((SUBLANES, RNN_BLOCK), F32)
        h0, h1 = hf_ref[...], hb_ref[...]
        xg = xg_ref[...]
        dr = jnp.concatenate([jnp.zeros((l_len, RNN_BLOCK), F32), drg_ref[...].astype(F32)], axis=0)
        gel, dgel = _gelu_pair(xg)
        dxg_ref[...] = (dr * (h0 + h1) * dgel).astype(dxg_ref.dtype)
        drnn = dr * gel
        a_s[0] = pltpu.roll(gates[0][3], t_len - 1, 0)
        a_s[1] = pltpu.roll(gates[1][3], 1, 0)
        b_s[0] = drnn
        b_s[1] = drnn
        hprev = [jnp.where(t == 0, 0.0, pltpu.roll(h0, 1, 0)),
                 jnp.where(t == l_len - 1, 0.0, pltpu.roll(h1, t_len - 1, 0))]
        chains = [(a_s.at[0], b_s.at[0], h_s.at[0], True), (a_s.at[1], b_s.at[1], h_s.at[1], False)]
        carry = _scan_chunks(chains, n_l, n_t - n_l, (zero, zero))
        _scan_chunks(chains, 0, n_l, carry)

        dxc = jnp.zeros((t_len, RNN_BLOCK), F32)
        for d in range(2):
            r, ig, sp, a, mult, u, rinv = gates[d]
            lamb = h_s[d]
            da = lamb * hprev[d]
            dmult = lamb * u
            du = lamb * mult
            dlog_a = da * a - dmult * (a * a) * rinv
            dzr = (dlog_a * ((-LRU_C) * sp)) * r * (1.0 - r)
            dsp = jnp.sum(dlog_a * ((-LRU_C) * r), axis=0, keepdims=True)
            dzi = (du * xc) * ig * (1.0 - ig)
            dzrb, dzib = dzr.astype(BF16), dzi.astype(BF16)
            dxc = dxc + du * ig + _dot(dzrb, wr_ref[d, 0].astype(BF16), 1, 1) + _dot(dzib, wi_ref[d, 0].astype(BF16), 1, 1)
            dwr_ref[d, 0] = _dot(xb, dzrb, 0, 0)
            dwi_ref[d, 0] = _dot(xb, dzib, 0, 0)
            dvec_ref[5 + d:6 + d, :] = jnp.sum(dzr, axis=0, keepdims=True)
            dvec_ref[7 + d:8 + d, :] = jnp.sum(dzi, axis=0, keepdims=True)
            dvec_ref[9 + d:10 + d, :] = dsp * (-_sigmoid(-lam_ref[d:d + 1, :]))
        dvec_ref[4:5, :] = jnp.sum(dxc, axis=0, keepdims=True)
        dvec_ref[11:N_RNN_VEC, :] = jnp.zeros((N_RNN_VEC - 11, RNN_BLOCK), F32)
        dxr = jnp.zeros((t_len, RNN_BLOCK), F32)
        for kk in range(CONV_WIDTH):
            dvec_ref[kk:kk + 1, :] = jnp.sum(dxc * _shift_rows(x, kk - 1, m_ref, l_len, t_len), axis=0, keepdims=True)
            dxr = dxr + _shift_rows(dxc, 1 - kk, m_ref, l_len, t_len) * cw_ref[kk:kk + 1, :]
        dxr_ref[...] = dxr.astype(dxr_ref.dtype)

    xr_spec, xg_spec, vec, wsp, msp = _rglru_specs(proj, xr_blk, xg_blk, t_len)
    col_t = pl.BlockSpec((t_len, RNN_BLOCK), lambda h: (0, h))
    return pl.pallas_call(
        body, name=name, grid=(d_rnn // RNN_BLOCK,),
        in_specs=[xr_spec, xg_spec, msp, col_t, col_t, pl.BlockSpec((s_len, RNN_BLOCK), lambda h: (0, h)), vec(CONV_WIDTH),
                  vec(1),
                  wsp, wsp, vec(2), vec(2), vec(2)],
        out_specs=(col_t, col_t, vec(N_RNN_VEC), wsp, wsp),
        out_shape=(jax.ShapeDtypeStruct((t_len, d_rnn), BF16), jax.ShapeDtypeStruct((t_len, d_rnn), BF16),
                   jax.ShapeDtypeStruct((N_RNN_VEC, d_rnn), F32), jax.ShapeDtypeStruct(w_rg.shape, F32),
                   jax.ShapeDtypeStruct(w_ig.shape, F32)),
        scratch_shapes=[pltpu.VMEM((2, t_len, RNN_BLOCK), F32)] * 3,
        compiler_params=_params(("parallel",)),
    )(proj, proj, masks, h_fwd, h_bwd, drg, conv_w, conv_b, w_rg, w_ig, b_rg, b_ig, lam)


def _colsum(v):
    return jnp.sum(v, axis=0, keepdims=True)


def _qk_fwd_heads(xv, gain, cos_e, sin_e):
    outs = []
    for h in range(xv.shape[1] // HEAD_DIM):
        xh = xv[:, h * HEAD_DIM:(h + 1) * HEAD_DIM]
        y = xh * _rms(xh) * gain
        outs.append(y * cos_e + _swap_pairs(y) * sin_e)
    return outs[0] if len(outs) == 1 else jnp.concatenate(outs, axis=1)


def _qk_bwd_heads(xv, dov, gain, cos_e, sin_e):
    outs = []
    dgain = jnp.zeros((1, HEAD_DIM), F32)
    for h in range(xv.shape[1] // HEAD_DIM):
        cols = slice(h * HEAD_DIM, (h + 1) * HEAD_DIM)
        xh, doh = xv[:, cols], dov[:, cols]
        r = _rms(xh)
        xhat = xh * r
        dy = doh * cos_e + _swap_pairs(doh * sin_e)
        dgain = dgain + _colsum(dy * xhat)
        outs.append(_rms_bwd(xhat, r, dy * gain))
    return (outs[0] if len(outs) == 1 else jnp.concatenate(outs, axis=1)), dgain


WEIGHTS = ['c_ctx', 'w_mod', 'b_mod', 'g_mix', 'g_mlp', 'w_in', 'q_gain', 'k_gain', 'conv_w', 'conv_b', 'w_rg', 'b_rg',
           'w_ig', 'b_ig', 'lru_lambda', 'w_o_attn', 'w_o_rnn', 'w_out', 'w_up', 'w_down', 'g_final']
REPLICATED = ['c_ctx', 'b_mod', 'g_mix', 'g_mlp', 'q_gain', 'k_gain', 'conv_b', 'w_rg', 'w_ig', 'g_final']
SMALL_SHARDED = [('conv_w', CONV_WIDTH), ('b_rg', 2), ('b_ig', 2), ('lru_lambda', 2)]
SMALL_ROWS = 16


PACK_ROWS = 512


def _pack_flat(arrs, rows_mult):
    flat = jnp.concatenate([a.reshape(-1) for a in arrs])
    n = flat.shape[0]
    unit = rows_mult * LANES
    pad = (-n) % unit
    if pad:
        flat = jnp.concatenate([flat, jnp.zeros((pad,), flat.dtype)])
    return flat.reshape(-1, LANES)


def _unpack_flat(packed, shapes):
    flat = packed.reshape(-1)
    out, pos = [], 0
    for shp in shapes:
        n = math.prod(shp)
        out.append(flat[pos:pos + n].reshape(shp))
        pos += n
    return out


def _pack_small(d):
    rows = [d[n][0] for n, _ in SMALL_SHARDED]
    cols = rows[0].shape[1]
    used = sum(k for _, k in SMALL_SHARDED)
    return jnp.concatenate(rows + [jnp.zeros((SMALL_ROWS - used, cols), F32)], axis=0)


def _unpack_small(p):
    out, pos = {}, 0
    for n, k in SMALL_SHARDED:
        out[n] = p[pos:pos + k][None]
        pos += k
    return out


def kernel(x, c, ctx, c_ctx, w_mod, b_mod, g_mix, g_mlp, w_in, q_gain, k_gain, conv_w, conv_b, w_rg, b_rg, w_ig, b_ig, lru_lambda, w_o_attn, w_o_rnn, w_out, w_up, w_down, g_final, loss_target, m_c_ctx, m_w_mod, m_b_mod, m_g_mix, m_g_mlp, m_w_in, m_q_gain, m_k_gain, m_conv_w, m_conv_b, m_w_rg, m_b_rg, m_w_ig, m_b_ig, m_lru_lambda, m_w_o_attn, m_w_o_rnn, m_w_out, m_w_up, m_w_down, m_g_final, v_c_ctx, v_w_mod, v_b_mod, v_g_mix, v_g_mlp, v_w_in, v_q_gain, v_k_gain, v_conv_w, v_conv_b, v_w_rg, v_b_rg, v_w_ig, v_b_ig, v_lru_lambda, v_w_o_attn, v_w_o_rnn, v_w_out, v_w_up, v_w_down, v_g_final):
    wd = dict(c_ctx=c_ctx, w_mod=w_mod, b_mod=b_mod, g_mix=g_mix, g_mlp=g_mlp, w_in=w_in, q_gain=q_gain, k_gain=k_gain,
              conv_w=conv_w, conv_b=conv_b, w_rg=w_rg, b_rg=b_rg, w_ig=w_ig, b_ig=b_ig, lru_lambda=lru_lambda,
              w_o_attn=w_o_attn, w_o_rnn=w_o_rnn, w_out=w_out, w_up=w_up, w_down=w_down, g_final=g_final)
    md = dict(zip(WEIGHTS, (m_c_ctx, m_w_mod, m_b_mod, m_g_mix, m_g_mlp, m_w_in, m_q_gain, m_k_gain, m_conv_w, m_conv_b,
                            m_w_rg, m_b_rg, m_w_ig, m_b_ig, m_lru_lambda, m_w_o_attn, m_w_o_rnn, m_w_out, m_w_up,
                            m_w_down, m_g_final)))
    vd = dict(zip(WEIGHTS, (v_c_ctx, v_w_mod, v_b_mod, v_g_mix, v_g_mlp, v_w_in, v_q_gain, v_k_gain, v_conv_w, v_conv_b,
                            v_w_rg, v_b_rg, v_w_ig, v_b_ig, v_lru_lambda, v_w_o_attn, v_w_o_rnn, v_w_out, v_w_up,
                            v_w_down, v_g_final)))

    xs, tgt, ctxs = x[0], loss_target[0], ctx[0]
    s_len, d = xs.shape
    l_len = ctxs.shape[0]
    t_len = s_len + l_len
    n_in = w_in.shape[2] * N_DEV
    aw = (n_in - 4 * d) * 2 // 3
    kvw = aw // Q_GROUP
    assert aw == d and n_in == aw + 2 * kvw + 4 * d
    nmod = N_MOD * d // N_DEV
    tr = 128
    assert l_len % tr == 0 and s_len % tr == 0
    lb = l_len // tr
    tq = min(256, s_len)
    my = _lin(_my_pos())
    c_ctx2 = c_ctx[None]
    g_final2 = g_final[None]

    c_all = _all_gather([jnp.broadcast_to(c, (SUBLANES, d))], [0], "ag_cond")[0][::SUBLANES]
    c16 = jnp.concatenate([c_all, jnp.broadcast_to(c_ctx2, (N_DEV, d))], axis=0)
    b_mod_sh = lax.dynamic_slice(b_mod, (0, my * nmod), (1, nmod))
    mod_part = _mm(c16, w_mod[0], a_fn=_silu, bias=b_mod_sh, name="mod_fwd")
    sp_w = _pack_small(wd)
    (mod_g, sp_f), order = _all_gather([mod_part, sp_w], [1, 1], "ag_mod", with_token=True)
    conv_w_f, b_rg_f, b_ig_f, lam_f = sp_f[0:4], sp_f[4:6], sp_f[6:8], sp_f[8:10]
    mod_lat = lax.dynamic_slice(mod_g, (my, 0), (1, N_MOD * d))
    sh_a, sc_a, ga_a, sh_f, sc_f, ga_f = [mod_lat[:, k * d:(k + 1) * d] for k in range(N_MOD)]
    csh_a, csc_a = mod_g[N_DEV:N_DEV + 1, :d], mod_g[N_DEV:N_DEV + 1, d:2 * d]

    x_all = jnp.concatenate([ctxs, xs], axis=0)

    def nm1_fwd(i, j, xv, g, csh, csc, sh, sc):
        is_ctx = i < lb
        n = xv * _rms(xv) * g
        return n * (1.0 + jnp.where(is_ctx, csc, sc)) + jnp.where(is_ctx, csh, sh)

    h_all = _rowwise(nm1_fwd, [_row(x_all), _bc(g_mix), _bc(csh_a), _bc(csc_a), _bc(sh_a), _bc(sc_a)],
                     [("row", BF16)], nr=t_len // tr, tr=tr, nc=1, cw=d, name="norm1_fwd")[0]
    w_in_f, proj, order = _gather_proj(h_all, _tok(w_in[0], order).astype(BF16), "ag_proj")
    late_axis = dict(w_o_attn=0, w_o_rnn=0, w_out=0, w_up=1, w_down=0)
    placed = [_place_block(wd[n][0], ax, my, "place_" + n) for n, ax in late_axis.items()]
    late_kind = lambda n: "gather_chips" if n in ("w_up", "w_down") else "gather"
    late = {}
    for (n, ax), full in zip(late_axis.items(), placed):
        late[n] = _xfer_start(late_kind(n), order, full, ax, "ag_start_" + n)
        order = late[n][4]
    q_gain_fwd = _tok(q_gain, order)

    late_wait = lambda n, after: _xfer_wait(late_kind(n), late[n], late_axis[n], after, "ag_wait_" + n)

    def sibling_start(n, after):
        arrived = late_wait(n, after)
        return _xfer_start("sibling", jnp.zeros((SUBLANES, LANES), F32), arrived, late_axis[n], "ag_sib_start_" + n)

    sibling_wait = lambda n, handle, after: _xfer_wait("sibling", handle, late_axis[n], after, "ag_sib_wait_" + n)

    pos = jnp.arange(s_len)
    n_freq = HEAD_DIM // 4
    inv_freq = ROPE_THETA ** (-jnp.arange(n_freq, dtype=F32) / n_freq)
    ang = jnp.concatenate([(pos // GRID_W).astype(F32)[:, None] * inv_freq,
                           (pos % GRID_W).astype(F32)[:, None] * inv_freq], axis=-1)
    sign = jnp.where(jnp.arange(HEAD_DIM) % 2 == 0, -1.0, 1.0).astype(F32)
    cos_e = jnp.concatenate([jnp.ones((l_len, HEAD_DIM), F32), jnp.repeat(jnp.cos(ang), 2, axis=-1)], axis=0)
    sin_e = jnp.concatenate([jnp.zeros((l_len, HEAD_DIM), F32), jnp.repeat(jnp.sin(ang), 2, axis=-1) * sign], axis=0)


    kb = aw // kvw
    q = _rowwise(lambda i, j, xv, gain, ce, se: _qk_fwd_heads(xv, gain, ce, se),
                 [_row(proj, 0, lb, aw), _full(q_gain_fwd), _row(cos_e, 0, lb, HEAD_DIM), _row(sin_e, 0, lb, HEAD_DIM)],
                 [("row", BF16, aw)], nr=s_len // tr, tr=tr, nc=1, cw=aw, name="q_fwd")[0]
    k_all, v_all = _rowwise(lambda i, j, kv_, vv_, gain, ce, se: (_qk_fwd_heads(kv_, gain, ce, se), vv_),
                            [_row(proj, kb, 0, kvw), _row(proj, kb + 1, 0, kvw), _full(k_gain),
                             _row(cos_e, 0, 0, HEAD_DIM), _row(sin_e, 0, 0, HEAD_DIM)],
                            [("row", BF16, kvw), ("row", BF16, kvw)], nr=t_len // tr, tr=tr, nc=1, cw=kvw, name="kv_fwd")
    attn_o = _attn_fwd(q, k_all, v_all, tq=tq, name="attn_fwd")

    xr_blk = (aw + 2 * kvw) // RNN_BLOCK
    xg_blk = xr_blk + d // RNN_BLOCK
    shift_masks = _shift_masks(l_len, t_len)
    rg, h_fwd, h_bwd = _rglru_fwd(proj, xr_blk, xg_blk, shift_masks, conv_w_f, conv_b, w_rg[0], w_ig[0], b_rg_f, b_ig_f, lam_f, l_len=l_len,
                    name="rglru_fwd")

    w_oa_f = late_wait("w_o_attn", attn_o)
    y_attn = _mm(attn_o, w_oa_f, name="o_attn_fwd")
    w_or_f = late_wait("w_o_rnn", rg)
    y_rnn = _mm(rg, w_or_f, name="o_rnn_fwd")
    hw = d // 2
    glb = (aw + 2 * kvw + 2 * d) // hw

    def merge_fwd(i, j, ya, yr, gla, glr):
        return _sigmoid(gla) * ya + _sigmoid(glr) * yr

    u = _rowwise(merge_fwd, [_row(y_attn), _row(y_rnn), _row(proj, glb, lb), _row(proj, glb + 2, lb)],
                 [("row", BF16)], nr=s_len // tr, tr=tr, nc=2, cw=hw, name="merge_fwd")[0]
    w_out_f = late_wait("w_out", u)
    sib_up = sibling_start("w_up", u)
    mix = _mm(u, w_out_f, name="out_fwd")

    def res1_fwd(i, j, xv, mixv, gaa, gm, shf, scf):
        x1v = xv + gaa * mixv
        n = x1v * _rms(x1v) * gm
        return x1v, n * (1.0 + scf) + shf

    x1, h2 = _rowwise(res1_fwd, [_row(xs), _row(mix), _bc(_tok(ga_a, sib_up[4])), _bc(g_mlp), _bc(sh_f), _bc(sc_f)],
                      [("row", F32), ("row", BF16)], nr=s_len // tr, tr=tr, nc=1, cw=d, name="norm2_fwd")
    relu2 = lambda z: jnp.square(jnp.maximum(z, 0.0))
    w_up_f = sibling_wait("w_up", sib_up, h2)
    up = _mm(h2, w_up_f, name="up_fwd")
    w_down_f = sibling_wait("w_down", sibling_start("w_down", up), up)
    down = _mm(up, w_down_f, a_fn=relu2, name="down_fwd")

    def final_fn(i, j, x1v, dn, tg, gaf, gf):
        x2 = x1v + gaf * dn
        r = _rms(x2)
        xh = x2 * r
        e = xh * gf - tg
        dy = e * (1.0 / d)
        dx2 = _rms_bwd(xh, r, dy * gf)
        return dx2, dx2 * gaf, _colsum(e * e) * (0.5 / d), _colsum(dy * xh), _colsum(dx2 * dn)

    dx2, ddown, loss_cols, g_gfinal, d_ga_f = _rowwise(
        final_fn, [_row(x1), _row(down), _row(tgt), _bc(ga_f), _bc(g_final2)],
        [("row", F32), ("row", BF16), ("acc", 1), ("acc", 1), ("acc", 1)], nr=s_len // tr, tr=tr, nc=1, cw=d,
        name="final_bwd")
    loss = lax.psum(jnp.sum(loss_cols), ("x", "y", "c"))

    dup = _mm(ddown, w_down_f, tb=True, out_fn=lambda r, upv: r * (2.0 * jnp.maximum(upv, 0.0)), extras=[up],
              out_dtype=BF16, name="down_bwd_x")
    gw_down = _mm(up, ddown, ta=True, a_fn=relu2, out_dtype=BF16, name="down_bwd_w")
    sent = dict(w_down=_scatter_start(gw_down, 0, my, "rs_start_w_down"))
    dh2 = _mm(dup, w_up_f, tb=True, name="up_bwd_x")
    gw_up = _mm(h2, dup, ta=True, out_dtype=BF16, name="up_bwd_w")
    sent["w_up"] = _scatter_start(gw_up, 1, my, "rs_start_w_up")
    sc_f_bwd = _tok(sc_f, sent["w_down"][0][4], sent["w_up"][0][4])

    def norm2_bwd(i, j, x1v, mixv, dh2v, dx2v, gaa, gm, scf):
        r = _rms(x1v)
        xh = x1v * r
        n = xh * gm
        dn = dh2v * (1.0 + scf)
        dx1 = dx2v + _rms_bwd(xh, r, dn * gm)
        return dx1, dx1 * gaa, _colsum(dh2v), _colsum(dh2v * n), _colsum(dn * xh), _colsum(dx1 * mixv)

    dx1, dmix, d_sh_f, d_sc_f, g_gmlp, d_ga_a = _rowwise(
        norm2_bwd, [_row(x1), _row(mix), _row(dh2), _row(dx2), _bc(ga_a), _bc(g_mlp), _bc(sc_f_bwd)],
        [("row", F32), ("row", BF16), ("acc", 1), ("acc", 1), ("acc", 1), ("acc", 1)], nr=s_len // tr, tr=tr, nc=1,
        cw=d, name="norm2_bwd")

    du = _mm(dmix, w_out_f, tb=True, name="out_bwd_x")
    gw_out = _mm(u, dmix, ta=True, out_dtype=BF16, name="out_bwd_w")
    sent["w_out"] = _scatter_start(gw_out, 0, my, "rs_start_w_out")

    def merge_bwd(i, j, duv, ya, yr, gla, glr):
        ga, gr = _sigmoid(gla), _sigmoid(glr)
        return duv * ga, duv * gr, duv * ya * ga * (1.0 - ga), duv * yr * gr * (1.0 - gr)

    dya, dyr, dgla, dglr = _rowwise(
        merge_bwd, [_row(du), _row(y_attn), _row(y_rnn), _row(proj, glb, lb), _row(proj, glb + 2, lb)],
        [("row", BF16)] * 4, nr=s_len // tr, tr=tr, nc=2, cw=hw, name="merge_bwd")
    dattn = _mm(dya, w_oa_f, tb=True, out_dtype=BF16, name="o_attn_bwd_x")
    gw_oa = _mm(attn_o, dya, ta=True, out_dtype=BF16, name="o_attn_bwd_w")
    drg = _mm(dyr, w_or_f, tb=True, out_dtype=BF16, name="o_rnn_bwd_x")
    gw_or = _mm(rg, dyr, ta=True, out_dtype=BF16, name="o_rnn_bwd_w")
    sent["w_o_attn"] = _scatter_start(gw_oa, 0, my, "rs_start_w_o_attn")
    sent["w_o_rnn"] = _scatter_start(gw_or, 0, my, "rs_start_w_o_rnn")
    conv_b_bwd = _tok(conv_b, sent["w_out"][0][4], sent["w_o_attn"][0][4], sent["w_o_rnn"][0][4])

    dq, dk, dv = _attn_bwd(q, k_all, v_all, dattn, tq=tq, name="attn_bwd")
    dxr, dxg, dvec, g_wrg, g_wig = _rglru_bwd(proj, xr_blk, xg_blk, shift_masks, h_fwd, h_bwd, drg, conv_w_f, conv_b_bwd, w_rg[0], w_ig[0], b_rg_f,
                                               b_ig_f, lam_f, l_len=l_len, name="rglru_bwd")
    gate_part = _pack_flat([g_wrg, g_wig], N_DEV * SUBLANES)
    gate_sent = _scatter_start(gate_part, 0, my, "rs_start_gates")
    q_gain_bwd = _tok(q_gain, gate_sent[0][4])

    dq_p, g_qgain = _rowwise(lambda i, j, xv, dov, gain, ce, se: _qk_bwd_heads(xv, dov, gain, ce, se),
                             [_row(proj, 0, lb, aw), _row(dq, 0, 0, aw), _full(q_gain_bwd), _row(cos_e, 0, lb, HEAD_DIM),
                              _row(sin_e, 0, lb, HEAD_DIM)],
                             [("row", BF16, aw), ("acc", 1, HEAD_DIM)], nr=s_len // tr, tr=tr, nc=1, cw=aw, name="q_bwd")

    def kv_bwd(i, j, xv, dkv, dvv, gain, ce, se):
        dk_p, dgain = _qk_bwd_heads(xv, dkv, gain, ce, se)
        return dk_p, dvv, dgain

    dk_p, dv_p, g_kgain = _rowwise(kv_bwd, [_row(proj, kb, 0, kvw), _row(dk, 0, 0, kvw), _row(dv, 0, 0, kvw),
                                            _full(k_gain), _row(cos_e, 0, 0, HEAD_DIM), _row(sin_e, 0, 0, HEAD_DIM)],
                                   [("row", BF16, kvw), ("row", BF16, kvw), ("acc", 1, HEAD_DIM)], nr=t_len // tr, tr=tr,
                                   nc=1, cw=kvw, name="kv_bwd")

    def dproj_fn(i, j, dqv, dkv, dvv, dxrv, dxgv, dgav, dgrv):
        zero_ctx = lambda a: jnp.where(i < lb, jnp.zeros_like(a), a)
        return jnp.concatenate([zero_ctx(dqv), dkv, dvv, dxrv, dxgv, zero_ctx(dgav), zero_ctx(dgrv)], axis=1)

    dproj = _rowwise(dproj_fn, [_row(dq_p, 0, -lb, aw), _row(dk_p, 0, 0, kvw), _row(dv_p, 0, 0, kvw), _row(dxr, 0, 0, d),
                                _row(dxg, 0, 0, d), _row(dgla, 0, -lb, d), _row(dglr, 0, -lb, d)],
                     [("row", BF16, n_in)], nr=t_len // tr, tr=tr, nc=1, cw=n_in, name="dproj_pack")[0]
    dh_all = _mm(dproj, w_in_f, tb=True, name="proj_bwd_x")
    g_mix_bwd = g_mix

    def norm1_bwd(i, j, xv, dhv, dresv, g, sc):
        r = _rms(xv)
        xh = xv * r
        dn = dhv * (1.0 + sc)
        return dresv + _rms_bwd(xh, r, dn * g), _colsum(dhv), _colsum(dhv * (xh * g)), _colsum(dn * xh)

    def norm1_bwd_ctx(i, j, xv, dhv, g, sc):
        xh = xv * _rms(xv)
        return _colsum(dhv), _colsum(dhv * (xh * g)), _colsum(dhv * (1.0 + sc) * xh)

    dx_lat, d_sh_a, d_sc_a, g_gmix_lat = _rowwise(
        norm1_bwd, [_row(x_all, 0, lb), _row(dh_all, 0, lb), _row(dx1), _bc(g_mix_bwd), _bc(sc_a)],
        [("row", F32)] + [("acc", 1)] * 3, nr=s_len // tr, tr=tr, nc=1, cw=d, name="norm1_bwd")
    d_csh, d_csc, g_gmix_ctx = _rowwise(
        norm1_bwd_ctx, [_row(x_all), _row(dh_all), _bc(g_mix_bwd), _bc(csc_a)],
        [("acc", 1)] * 3, nr=lb, tr=tr, nc=1, cw=d, name="norm1_bwd_ctx")
    g_gmix = g_gmix_lat + g_gmix_ctx
    grad_x = dx_lat[None]

    dmod_lat = jnp.concatenate([d_sh_a, d_sc_a, d_ga_a, d_sh_f, d_sc_f, d_ga_f], axis=1)
    dmod_ctx = jnp.concatenate([d_csh, d_csc, jnp.zeros((1, (N_MOD - 2) * d), F32)], axis=1)
    dm_g = _all_gather([jnp.concatenate([dmod_lat, dmod_ctx, jnp.zeros((SUBLANES - 2, N_MOD * d), F32)], axis=0)], [0],
                       "ag_dmod")[0]
    dm16 = jnp.concatenate([dm_g[0::SUBLANES], dm_g[1::SUBLANES]], axis=0)
    dm_sh = lax.dynamic_slice(dm16, (0, my * nmod), (2 * N_DEV, nmod))
    gw_mod = _mm(c16, dm_sh, ta=True, a_fn=_silu, name="mod_bwd_w")
    dc16 = _mm(dm_sh, w_mod[0], tb=True, name="mod_bwd_x")

    def cctx_fn(i, j, pv, cc, dl, dc):
        row = lax.broadcasted_iota(jnp.int32, pv.shape, 0)
        return _colsum(jnp.where(row >= N_DEV, pv, 0.0)) * _dsilu(cc), dl + dc

    g_cctx, g_bmod = _rowwise(cctx_fn, [_full(dc16), _full(c_ctx2), _full(dmod_lat), _full(dmod_ctx)],
                              [("row", F32, d), ("row", F32, N_MOD * d)], nr=1, tr=1, nc=1, cw=d, name="cctx_bwd")

    rep_part = dict(c_ctx=g_cctx, b_mod=g_bmod, g_mix=g_gmix, g_mlp=g_gmlp, q_gain=g_qgain, k_gain=g_kgain,
                    conv_b=dvec[4:5], g_final=g_gfinal)
    rep_small = [n for n in REPLICATED if n in rep_part]
    gates = [n for n in REPLICATED if n not in rep_part]
    part_pack = _pack_flat([rep_part[n] for n in rep_small], PACK_ROWS)
    rep_rows = part_pack.shape[0]
    gate_land = _xfer_wait("scatter", gate_sent[0], 0, g_bmod, "rs_wait_gates")
    gate_sum = _sum_parts(gate_sent[1], gate_land, name="gate_sum")
    (gate_full, parts), order = _all_gather([gate_sum, part_pack], [0, 0], "ag_rep_grads", with_token=True)
    small_g = jnp.concatenate([dvec[0:4], dvec[5:11], jnp.zeros((SMALL_ROWS - 10, d), F32)], axis=0)
    (small_recv,), order = _all_to_all([_tok(small_g, order)], [1], "a2a_small_grads")

    halves = {}
    for part, lo in (("top", 0), ("bot", d // 2)):
        gw_part = _mm(h_all[:, lo:lo + d // 2], dproj, ta=True, out_dtype=BF16, name="proj_bwd_w_" + part,
                      bias=_tok(jnp.zeros((1, n_in), F32), order))
        halves[part] = _chip_scatter_start(_pair_reduce(gw_part, my, "pair_w_in_" + part), my, "rs_start_w_in_" + part)
        order = halves[part][0][4]
    started = order

    res = {}
    for names, pr, mult, nm in ((rep_small, parts.reshape(N_DEV, rep_rows, LANES), PACK_ROWS, "adamw_rep"),
                                (gates, gate_full[None], N_DEV * SUBLANES, "adamw_gates")):
        rep_out = _adamw(pr, _tok(_pack_flat([wd[n] for n in names], mult), started),
                         _pack_flat([md[n] for n in names], mult), _pack_flat([vd[n] for n in names], mult), name=nm)
        for kind, packed in zip(("g", "d", "m", "v"), rep_out):
            for n, a in zip(names, _unpack_flat(packed, [wd[n].shape for n in names])):
                res[kind, n] = a
    out = _adamw(small_recv, _tok(sp_w, started), _pack_small(md), _pack_small(vd), name="adamw_small")
    for kind, a in zip(("g", "d", "m", "v"), out):
        for n, piece in _unpack_small(a).items():
            res[kind, n] = piece

    after = started
    for n, (handle, own, ax, kind) in sent.items():
        land = _xfer_wait(kind, handle, ax, after, "rs_wait_" + n)
        out = _adamw(land, wd[n][0], md[n][0], vd[n][0], own=own, name="adamw_" + n)
        after = out[0]
        for kind, a in zip(("g", "d", "m", "v"), out):
            res[kind, n] = a[None]
    out = _adamw(gw_mod[None], w_mod[0], m_w_mod[0], v_w_mod[0], name="adamw_w_mod")
    after = out[0]
    for kind, a in zip(("g", "d", "m", "v"), out):
        res[kind, 'w_mod'] = a[None]
    lands, owns = [], []
    for part, (handle, own, ax, kind) in halves.items():
        lands.append(_xfer_wait(kind, handle, ax, after, "rs_wait_w_in_" + part))
        owns.append(own)
        after = lands[-1]
    out = _adamw(jnp.concatenate(lands, axis=1), w_in[0], m_w_in[0], v_w_in[0], own=jnp.concatenate(owns, axis=0),
                 name="adamw_w_in")
    for kind, a in zip(("g", "d", "m", "v"), out):
        res[kind, 'w_in'] = a[None]

    outs = [loss, grad_x]
    for kind in ("g", "d", "m", "v"):
        outs += [res[kind, n].reshape(wd[n].shape) for n in WEIGHTS]
    return tuple(outs)
```

```python
import functools
import math

import jax
import jax.numpy as jnp
from jax import lax
from jax.experimental import pallas as pl
from jax.experimental.pallas import tpu as pltpu

F32 = jnp.float32
BF16 = jnp.bfloat16
MESH = pl.DeviceIdType.MESH

N_DEV = 8
HEAD_DIM = 128
Q_GROUP = 4
RNN_BLOCK = 128
GRID_W = 64
CONV_WIDTH = 4
LRU_C = 8.0
ROPE_THETA = 10000.0
NORM_EPS = 1e-6
N_MOD = 6
ADAM_LR = 0.001
ADAM_B1 = 0.9
ADAM_B2 = 0.999
ADAM_EPS = 1e-08
ADAM_WD = 0.01
ADAM_STEP = 10

LANES = 128
SUBLANES = 8
VMEM_LIMIT = 56 * 1024 * 1024
HBM_SPEC = pl.BlockSpec(memory_space=pltpu.HBM)


def _params(sem=None, vmem=VMEM_LIMIT):
    kw = dict(vmem_limit_bytes=vmem)
    if sem is not None:
        kw["dimension_semantics"] = sem
    return pltpu.CompilerParams(**kw)


def _pick(dim, target, mult):
    for t in range(min(dim, target), 0, -1):
        if dim % t == 0 and t % mult == 0:
            return t
    return dim


def _my_pos():
    return lax.axis_index("x"), lax.axis_index("y"), lax.axis_index("c")


def _flip(pos, d):
    x, y, c = pos
    return (1 - x if d & 4 else x, 1 - y if d & 2 else y, 1 - c if d & 1 else c)


def _lin(pos):
    return 4 * pos[0] + 2 * pos[1] + pos[2]


def _slab(ref, axis, idx, size):
    if axis == 0:
        return ref.at[pl.ds(idx * size, size)]
    assert axis == 1
    return ref.at[:, pl.ds(idx * size, size)]


def _all_gather(arrs, axes, name, with_token=False):
    n = len(arrs)
    sizes = [a.shape[ax] for a, ax in zip(arrs, axes)]

    def body(*refs):
        ins, outs, token = refs[:n], refs[n:2 * n], refs[2 * n]
        send_sems, recv_sems, local_sems = refs[2 * n + 1:]
        token[...] = jnp.zeros_like(token)
        me = _my_pos()
        sibling = _flip(me, 1)
        chips = [2, 4, 6]

        def copy(a, k, block_pos, to, src=None):
            dst = _slab(outs[a], axes[a], _lin(block_pos), sizes[a])
            return pltpu.make_async_remote_copy(
                src_ref=dst if src is None else src, dst_ref=dst,
                send_sem=send_sems.at[7 * a + k], recv_sem=recv_sems.at[7 * a + k],
                device_id=to, device_id_type=MESH)

        local, first, passed = [], [], []
        for a in range(n):
            cp = pltpu.make_async_copy(ins[a], _slab(outs[a], axes[a], _lin(me), sizes[a]), local_sems.at[a])
            cp.start()
            local.append(cp)
        for a in range(n):
            f = [copy(a, 0, me, sibling, src=ins[a])]
            f += [copy(a, 1 + j, me, _flip(me, m), src=ins[a]) for j, m in enumerate(chips)]
            for cp in f:
                cp.start()
            first += f
        for a in range(n):
            for j, m in enumerate(chips):
                copy(a, 1 + j, _flip(me, m), me).wait_recv()
                cp = copy(a, 4 + j, _flip(me, m), sibling)
                cp.start()
                passed.append(cp)
        for a in range(n):
            copy(a, 0, sibling, me).wait_recv()
            for j, m in enumerate(chips):
                copy(a, 4 + j, _flip(sibling, m), me).wait_recv()
        for cp in first + passed:
            cp.wait_send()
        for cp in local:
            cp.wait()

    out_shape = []
    for a, ax in zip(arrs, axes):
        shp = list(a.shape)
        shp[ax] *= N_DEV
        out_shape.append(jax.ShapeDtypeStruct(tuple(shp), a.dtype))
    out_shape.append(jax.ShapeDtypeStruct((SUBLANES, LANES), F32))
    res = pl.pallas_call(
        body, name=name, out_shape=tuple(out_shape),
        in_specs=[HBM_SPEC] * n, out_specs=tuple([HBM_SPEC] * n + [pl.BlockSpec(memory_space=pltpu.VMEM)]),
        scratch_shapes=[pltpu.SemaphoreType.DMA((7 * n,)), pltpu.SemaphoreType.DMA((7 * n,)),
                        pltpu.SemaphoreType.DMA((n,))],
    )(*arrs)
    return (list(res[:n]), res[n]) if with_token else list(res[:n])


def _gather_proj(h_all, w_shard, name):
    t_len, kdim = h_all.shape
    sz = w_shard.shape[1]
    tm = _pick(t_len, 1024, 16)
    n_m = t_len // tm

    def body(h_ref, w_ref, wfull_ref, proj_ref, token, a_buf, b_buf, o_buf, send_sems, recv_sems, io_sems, out_sems):
        token[...] = jnp.zeros_like(token)
        me = _my_pos()
        sibling = _flip(me, 1)
        chips = [2, 4, 6]
        slab = lambda pos: _slab(wfull_ref, 1, _lin(pos), sz)

        def copy(k, block_pos, to, src=None):
            return pltpu.make_async_remote_copy(
                src_ref=slab(block_pos) if src is None else src, dst_ref=slab(block_pos),
                send_sem=send_sems.at[k], recv_sem=recv_sems.at[k], device_id=to, device_id_type=MESH)

        pending = []

        def block_product(pos, b_src):
            load = pltpu.make_async_copy(b_src, b_buf, io_sems.at[1])
            load.start()
            load.wait()
            for i in range(n_m):
                slot = len(pending) % 2
                if len(pending) >= 2:
                    pending[-2].wait()
                o_buf[slot] = _dot(a_buf[pl.ds(i * tm, tm), :], b_buf[...], 1, 0)
                out = pltpu.make_async_copy(
                    o_buf.at[slot], proj_ref.at[pl.ds(i * tm, tm), pl.ds(_lin(pos) * sz, sz)], out_sems.at[slot])
                out.start()
                pending.append(out)

        local = pltpu.make_async_copy(w_ref, slab(me), io_sems.at[0])
        local.start()
        first = [copy(0, me, sibling, src=w_ref)] + [copy(1 + j, me, _flip(me, m), src=w_ref) for j, m in enumerate(chips)]
        for cp in first:
            cp.start()
        load_a = pltpu.make_async_copy(h_ref, a_buf, io_sems.at[2])
        load_a.start()
        load_a.wait()
        block_product(me, w_ref)
        copy(0, sibling, me).wait_recv()
        block_product(sibling, slab(sibling))
        passed = []
        for j, m in enumerate(chips):
            src_pos = _flip(me, m)
            copy(1 + j, src_pos, me).wait_recv()
            cp = copy(4 + j, src_pos, sibling)
            cp.start()
            passed.append(cp)
            block_product(src_pos, slab(src_pos))
        for j, m in enumerate(chips):
            src_pos = _flip(sibling, m)
            copy(4 + j, src_pos, me).wait_recv()
            block_product(src_pos, slab(src_pos))
        for out in pending[-2:]:
            out.wait()
        for cp in first + passed:
            cp.wait_send()
        local.wait()

    return pl.pallas_call(
        body, name=name,
        out_shape=(jax.ShapeDtypeStruct((kdim, N_DEV * sz), w_shard.dtype), jax.ShapeDtypeStruct((t_len, N_DEV * sz), F32),
                   jax.ShapeDtypeStruct((SUBLANES, LANES), F32)),
        in_specs=[HBM_SPEC, HBM_SPEC], out_specs=(HBM_SPEC, HBM_SPEC, pl.BlockSpec(memory_space=pltpu.VMEM)),
        scratch_shapes=[pltpu.VMEM((t_len, kdim), h_all.dtype), pltpu.VMEM((kdim, sz), w_shard.dtype),
                        pltpu.VMEM((2, tm, sz), F32), pltpu.SemaphoreType.DMA((7,)), pltpu.SemaphoreType.DMA((7,)),
                        pltpu.SemaphoreType.DMA((3,)), pltpu.SemaphoreType.DMA((2,))],
        compiler_params=_params(),
    )(h_all, w_shard)


def _all_to_all(arrs, axes, name):
    n = len(arrs)
    sizes = [a.shape[ax] // N_DEV for a, ax in zip(arrs, axes)]

    def body(*refs):
        ins, outs, token = refs[:n], refs[n:2 * n], refs[2 * n]
        send_sems, recv_sems, local_sems = refs[2 * n + 1:]
        token[...] = jnp.zeros_like(token)
        me = _my_pos()
        my = _lin(me)
        copies, local = [], []
        for a in range(n):
            cp = pltpu.make_async_copy(_slab(ins[a], axes[a], my, sizes[a]), outs[a].at[my], local_sems.at[a])
            cp.start()
            local.append(cp)
        for d in range(1, N_DEV):
            peer = _flip(me, d)
            for a in range(n):
                cp = pltpu.make_async_remote_copy(
                    src_ref=_slab(ins[a], axes[a], _lin(peer), sizes[a]), dst_ref=outs[a].at[my],
                    send_sem=send_sems.at[7 * a + d - 1], recv_sem=recv_sems.at[7 * a + d - 1],
                    device_id=peer, device_id_type=MESH)
                cp.start()
                copies.append((a, d, cp))
        for a, d, cp in copies:
            peer = _flip(me, d)
            pltpu.make_async_remote_copy(
                src_ref=_slab(ins[a], axes[a], _lin(peer), sizes[a]), dst_ref=outs[a].at[_lin(peer)],
                send_sem=send_sems.at[7 * a + d - 1], recv_sem=recv_sems.at[7 * a + d - 1],
                device_id=peer, device_id_type=MESH).wait_recv()
            cp.wait_send()
        for cp in local:
            cp.wait()

    out_shape = []
    for a, ax, s in zip(arrs, axes, sizes):
        shp = list(a.shape)
        shp[ax] = s
        out_shape.append(jax.ShapeDtypeStruct((N_DEV,) + tuple(shp), a.dtype))
    out_shape.append(jax.ShapeDtypeStruct((SUBLANES, LANES), F32))
    res = pl.pallas_call(
        body, name=name, out_shape=tuple(out_shape),
        in_specs=[HBM_SPEC] * n, out_specs=tuple([HBM_SPEC] * n + [pl.BlockSpec(memory_space=pltpu.VMEM)]),
        scratch_shapes=[pltpu.SemaphoreType.DMA((7 * n,)), pltpu.SemaphoreType.DMA((7 * n,)),
                        pltpu.SemaphoreType.DMA((n,))],
    )(*arrs)
    return list(res[:n]), res[n]


SEM_SPEC = pl.BlockSpec(memory_space=pltpu.SEMAPHORE)
_EFFECT = pltpu.SideEffectType.DATAFLOW_SIDE_EFFECTING
N_PEERS = N_DEV - 1


def _split_copies(kind, src_ref, land_ref, send_sems, recv_sems, axis, size, arriving):
    me = _my_pos()
    out = []
    if kind == "sibling":
        sibling = _flip(me, 1)
        for k in range(N_CHIPS):
            blk = _slab(land_ref, axis, 2 * k + (sibling[2] if arriving else me[2]), size)
            out.append(pltpu.make_async_remote_copy(src_ref=blk, dst_ref=blk, send_sem=send_sems.at[k],
                                                    recv_sem=recv_sems.at[k], device_id=sibling, device_id_type=MESH))
        return out
    for dd in (range(2, N_DEV, 2) if kind in ("chips", "gather_chips") else range(1, N_DEV)):
        peer = _flip(me, dd)
        k = dd - 1
        if kind in ("gather", "gather_chips"):
            src = _slab(land_ref, axis, _lin(me), size)
            dst = _slab(land_ref, axis, _lin(peer if arriving else me), size)
        elif kind == "scatter":
            src, dst = _slab(src_ref, axis, _lin(peer), size), land_ref.at[k]
        else:
            k = dd // 2 - 1
            src, dst = src_ref.at[2 * peer[0] + peer[1]], land_ref.at[k]
        out.append(pltpu.make_async_remote_copy(
            src_ref=src, dst_ref=dst, send_sem=send_sems.at[k], recv_sem=recv_sems.at[k],
            device_id=peer, device_id_type=MESH))
    return out


N_CHIPS = N_DEV // 2


def _split_size(kind, src_shape, land_shape, axis):
    if kind in ("gather", "gather_chips", "sibling"):
        return land_shape[axis] // N_DEV
    return src_shape[axis] // {"scatter": N_DEV, "chips": N_CHIPS}[kind]


def _xfer_start(kind, src, land, axis, name):
    size = _split_size(kind, src.shape, land.shape, axis)

    def body(src_ref, land_ref, send_sems, recv_sems, src_thru, land_thru, token):
        for cp in _split_copies(kind, src_ref, land_ref, send_sems, recv_sems, axis, size, False):
            cp.start()
        token[...] = jnp.zeros_like(token)

    return pl.pallas_call(
        body, name=name,
        out_shape=(pltpu.SemaphoreType.DMA((N_PEERS,)), pltpu.SemaphoreType.DMA((N_PEERS,)),
                   pltpu.HBM(src.shape, src.dtype), pltpu.HBM(land.shape, land.dtype),
                   jax.ShapeDtypeStruct((SUBLANES, LANES), F32)),
        in_specs=(HBM_SPEC, HBM_SPEC),
        out_specs=(SEM_SPEC, SEM_SPEC, HBM_SPEC, HBM_SPEC, pl.BlockSpec(memory_space=pltpu.VMEM)),
        input_output_aliases={0: 2, 1: 3},
        compiler_params=pltpu.CompilerParams(has_side_effects=_EFFECT),
    )(pltpu.with_memory_space_constraint(src, pltpu.HBM), pltpu.with_memory_space_constraint(land, pltpu.HBM))


def _xfer_wait(kind, handle, axis, after, name):
    send_sems, recv_sems, src_thru, land_thru, _ = handle
    size = _split_size(kind, src_thru.shape, land_thru.shape, axis)

    def body(src_ref, land_ref, send_sems, recv_sems, after_ref, src_dead, got_ref):
        for cp in _split_copies(kind, src_ref, land_ref, send_sems, recv_sems, axis, size, True):
            cp.wait_send()
            cp.wait_recv()

    return pl.pallas_call(
        body, name=name,
        out_shape=(pltpu.HBM(src_thru.shape, src_thru.dtype), pltpu.HBM(land_thru.shape, land_thru.dtype)),
        in_specs=(HBM_SPEC, HBM_SPEC, SEM_SPEC, SEM_SPEC, pl.BlockSpec(memory_space=pl.ANY)),
        out_specs=(HBM_SPEC, HBM_SPEC), input_output_aliases={0: 0, 1: 1},
        compiler_params=pltpu.CompilerParams(has_side_effects=_EFFECT),
    )(src_thru, land_thru, send_sems, recv_sems, after)[1]


def _place_block(shard, axis, my, name):
    r, c = shard.shape
    tr = _pick(r, max(16, (1 << 19) // c), 16)
    nr = r // tr
    if axis == 0:
        full, out_map = (r * N_DEV, c), (lambda i, my_ref: (my_ref[0] * nr + i, 0))
    else:
        full, out_map = (r, c * N_DEV), (lambda i, my_ref: (i, my_ref[0]))

    def body(my_ref, x_ref, o_ref):
        o_ref[...] = x_ref[...].astype(BF16)

    return pl.pallas_call(
        body, name=name, out_shape=jax.ShapeDtypeStruct(full, BF16),
        grid_spec=pltpu.PrefetchScalarGridSpec(
            num_scalar_prefetch=1, grid=(nr,), in_specs=[pl.BlockSpec((tr, c), lambda i, my_ref: (i, 0))],
            out_specs=pl.BlockSpec((tr, c), out_map)),
        compiler_params=_params(("arbitrary",)),
    )(jnp.reshape(my, (1,)).astype(jnp.int32), shard)


def _scatter_start(grad, axis, my, name):
    shp = list(grad.shape)
    shp[axis] //= N_DEV
    own = lax.dynamic_slice_in_dim(grad, my * shp[axis], shp[axis], axis)
    return (_xfer_start("scatter", grad, lax.empty((N_PEERS,) + tuple(shp), grad.dtype), axis, name), own, axis,
            "scatter")


def _pair_reduce(grad, my, name):
    r, cols = grad.shape
    sz = cols // N_DEV

    def exchange(g_ref, got_ref, send_sems, recv_sems):
        me = _my_pos()
        sibling = _flip(me, 1)
        cps = []
        for k in range(N_CHIPS):
            cp = pltpu.make_async_remote_copy(
                src_ref=_slab(g_ref, 1, 2 * k + sibling[2], sz), dst_ref=_slab(got_ref, 1, k, sz),
                send_sem=send_sems.at[k], recv_sem=recv_sems.at[k], device_id=sibling, device_id_type=MESH)
            cp.start()
            cps.append(cp)
        for cp in cps:
            cp.wait()

    got = pl.pallas_call(
        exchange, name=name + "_swap", out_shape=jax.ShapeDtypeStruct((r, N_CHIPS * sz), grad.dtype),
        in_specs=[HBM_SPEC], out_specs=HBM_SPEC,
        scratch_shapes=[pltpu.SemaphoreType.DMA((N_CHIPS,)), pltpu.SemaphoreType.DMA((N_CHIPS,))],
    )(grad)

    tr = _pick(r, 512, 16)

    def add(c_ref, g_ref, p_ref, o_ref):
        o_ref[...] = (g_ref[...].astype(F32) + p_ref[...].astype(F32)).astype(o_ref.dtype)

    def add(c_ref, g_ref, p_ref, o_ref):
        o_ref[0] = (g_ref[...].astype(F32) + p_ref[...].astype(F32)).astype(o_ref.dtype)

    return pl.pallas_call(
        add, name=name + "_sum", out_shape=jax.ShapeDtypeStruct((N_CHIPS, r, sz), grad.dtype),
        grid_spec=pltpu.PrefetchScalarGridSpec(
            num_scalar_prefetch=1, grid=(N_CHIPS, r // tr),
            in_specs=[pl.BlockSpec((tr, sz), lambda k, i, c_ref: (i, 2 * k + c_ref[0])),
                      pl.BlockSpec((tr, sz), lambda k, i, c_ref: (i, k))],
            out_specs=pl.BlockSpec((1, tr, sz), lambda k, i, c_ref: (k, i, 0))),
        compiler_params=_params(("parallel", "parallel")),
    )(jnp.reshape(my % 2, (1,)).astype(jnp.int32), grad, got)


def _chip_scatter_start(pair_sum, my, name):
    _, r, sz = pair_sum.shape
    own = lax.dynamic_index_in_dim(pair_sum, my // 2, 0, keepdims=False)
    return (_xfer_start("chips", pair_sum, lax.empty((N_CHIPS - 1, r, sz), pair_sum.dtype), 0, name), own, 0, "chips")


def _tok(a, *tokens):
    for t in tokens:
        a = a + t[0, 0].astype(a.dtype)
    return a


def _mm(a, b, *, name, ta=False, tb=False, out_dtype=F32, a_fn=None, out_fn=None, extras=(), bias=None,
        tm=1024, tn=1024, tk=2816, a_cols=None):
    a_blk0 = 0
    if ta:
        kdim, m = a.shape
        if a_cols is not None:
            m = a_cols[1]
    else:
        m, kdim = a.shape
    if tb:
        n, kb = b.shape
    else:
        kb, n = b.shape
    assert kb == kdim, (a.shape, b.shape, ta, tb)
    tm = _pick(m, tm, LANES if ta else SUBLANES)
    if a_cols is not None:
        assert ta and a_cols[0] % tm == 0
        a_blk0 = a_cols[0] // tm
    tn = _pick(n, tn, LANES)
    tk = _pick(kdim, tk, LANES)
    nk = kdim // tk
    ne = len(extras)
    has_bias = bias is not None

    def body(*refs):
        a_ref, b_ref = refs[0], refs[1]
        pos = 2
        bias_ref = None
        if has_bias:
            bias_ref = refs[pos]
            pos += 1
        extra_refs = refs[pos:pos + ne]
        o_ref = refs[pos + ne]
        av = a_ref[...]
        if a_fn is not None:
            av = a_fn(av)
        dn = (((0 if ta else 1,), (1 if tb else 0,)), ((), ()))
        part = lax.dot_general(av.astype(BF16), b_ref[...].astype(BF16), dn, preferred_element_type=F32)

        def finish(r):
            if has_bias:
                r = r + bias_ref[...]
            if out_fn is not None:
                r = out_fn(r, *[e[...] for e in extra_refs])
            o_ref[...] = r.astype(out_dtype)

        if nk == 1:
            finish(part)
            return
        acc = refs[pos + ne + 1]
        k = pl.program_id(2)

        @pl.when(k == 0)
        def _():
            acc[...] = part

        @pl.when(k > 0)
        def _():
            acc[...] += part

        @pl.when(k == nk - 1)
        def _():
            finish(acc[...])

    in_specs = [
        pl.BlockSpec((tk, tm), lambda i, j, k: (k, i + a_blk0)) if ta else pl.BlockSpec((tm, tk), lambda i, j, k: (i, k)),
        pl.BlockSpec((tn, tk), lambda i, j, k: (j, k)) if tb else pl.BlockSpec((tk, tn), lambda i, j, k: (k, j)),
    ]
    args = [a, b]
    if has_bias:
        in_specs.append(pl.BlockSpec((1, tn), lambda i, j, k: (0, j)))
        args.append(bias)
    for e in extras:
        in_specs.append(pl.BlockSpec((tm, tn), lambda i, j, k: (i, j)))
        args.append(e)
    return pl.pallas_call(
        body, name=name, grid=(m // tm, n // tn, nk),
        in_specs=in_specs, out_specs=pl.BlockSpec((tm, tn), lambda i, j, k: (i, j)),
        out_shape=jax.ShapeDtypeStruct((m, n), out_dtype),
        scratch_shapes=[pltpu.VMEM((tm, tn), F32)] if nk > 1 else [],
        compiler_params=_params(("parallel", "parallel", "arbitrary")),
    )(*args)


def _row(arr, cb=0, rb=0, w=None):
    return ("row", arr, cb, rb, w)


def _bc(arr, cb=0, w=None):
    return ("bc", arr, cb, 0, w)


def _full(arr):
    return ("full", arr, 0, 0, None)


def _rowwise(fn, ins, outs, *, nr, tr, nc, cw, name):
    n_in = len(ins)
    outs = [(o[0], o[1], (o[2] if len(o) > 2 and o[2] else cw)) for o in outs]

    def body(*refs):
        in_refs, out_refs = refs[:n_in], refs[n_in:]
        j, i = pl.program_id(0), pl.program_id(1)
        res = fn(i, j, *[r[...] for r in in_refs])
        if not isinstance(res, (tuple, list)):
            res = (res,)
        for (kind, _, _), o_ref, r in zip(outs, out_refs, res):
            if kind == "row":
                o_ref[...] = r.astype(o_ref.dtype)
            else:
                @pl.when(i == 0)
                def _(o_ref=o_ref):
                    o_ref[...] = jnp.zeros_like(o_ref)

                o_ref[...] += r

    in_specs, args = [], []
    for kind, arr, cb, rb, w in ins:
        w = w or cw
        if kind == "row":
            last = arr.shape[0] // tr - 1
            in_specs.append(pl.BlockSpec((tr, w), lambda j, i, cb=cb, rb=rb, last=last: (jnp.clip(i + rb, 0, last), j + cb)))
        elif kind == "bc":
            in_specs.append(pl.BlockSpec((arr.shape[0], w), lambda j, i, cb=cb: (0, j + cb)))
        else:
            in_specs.append(pl.BlockSpec(arr.shape, lambda j, i, nd=arr.ndim: (0,) * nd))
        args.append(arr)
    out_specs, out_shape = [], []
    for kind, spec, w in outs:
        if kind == "row":
            out_specs.append(pl.BlockSpec((tr, w), lambda j, i: (i, j)))
            out_shape.append(jax.ShapeDtypeStruct((nr * tr, nc * w), spec))
        else:
            out_specs.append(pl.BlockSpec((spec, w), lambda j, i: (0, j)))
            out_shape.append(jax.ShapeDtypeStruct((spec, nc * w), F32))
    res = pl.pallas_call(
        body, name=name, grid=(nc, nr), in_specs=in_specs, out_specs=tuple(out_specs), out_shape=tuple(out_shape),
        compiler_params=_params(("parallel", "arbitrary")),
    )(*args)
    return res


def _sum_parts(own, parts, *, name):
    p, r, c = parts.shape
    tr = _pick(r, max(SUBLANES, (1 << 18) // c), SUBLANES)

    def body(o_ref, p_ref, g_out):
        g = o_ref[...].astype(F32)
        for q in range(p):
            g = g + p_ref[q].astype(F32)
        g_out[...] = g

    blk = pl.BlockSpec((tr, c), lambda i: (i, 0))
    return pl.pallas_call(
        body, name=name, grid=(r // tr,), in_specs=[blk, pl.BlockSpec((p, tr, c), lambda i: (0, i, 0))],
        out_specs=blk, out_shape=jax.ShapeDtypeStruct((r, c), F32), compiler_params=_params(("parallel",)),
    )(own, parts)


def _adamw(parts, w, m, v, *, name, own=None):
    p, r, c = parts.shape
    tr = _pick(r, max(SUBLANES, (1 << 18) // c), 16 if parts.dtype == BF16 else SUBLANES)
    bc1 = 1.0 - ADAM_B1 ** ADAM_STEP
    bc2 = 1.0 - ADAM_B2 ** ADAM_STEP
    has_own = own is not None

    def body(*refs):
        p_ref, w_ref, m_ref, v_ref = refs[:4]
        g_out, d_out, m_out, v_out = refs[4 + has_own:]
        if has_own:
            g = refs[4][...].astype(F32) + p_ref[0].astype(F32)
        else:
            g = p_ref[0].astype(F32)
        for q in range(1, p):
            g = g + p_ref[q].astype(F32)
        m_new = ADAM_B1 * m_ref[...] + (1.0 - ADAM_B1) * g
        v_new = ADAM_B2 * v_ref[...] + (1.0 - ADAM_B2) * (g * g)
        m_hat = m_new / bc1
        v_hat = v_new / bc2
        g_out[...] = g
        d_out[...] = -ADAM_LR * (m_hat / (jnp.sqrt(v_hat) + ADAM_EPS) + ADAM_WD * w_ref[...])
        m_out[...] = m_new
        v_out[...] = v_new

    blk = pl.BlockSpec((tr, c), lambda i: (i, 0))
    return pl.pallas_call(
        body, name=name, grid=(r // tr,),
        in_specs=[pl.BlockSpec((p, tr, c), lambda i: (0, i, 0)), blk, blk, blk] + ([blk] if has_own else []),
        out_specs=(blk, blk, blk, blk), out_shape=tuple(jax.ShapeDtypeStruct((r, c), F32) for _ in range(4)),
        compiler_params=_params(("parallel",)),
    )(*([parts, w, m, v] + ([own] if has_own else [])))


_GELU_C = math.sqrt(2.0 / math.pi)
_GELU_K = 0.044715


def _sigmoid(z):
    return 0.5 * jnp.tanh(0.5 * z) + 0.5


def _silu(z):
    return z * _sigmoid(z)


def _dsilu(z):
    s = _sigmoid(z)
    return s * (1.0 + z * (1.0 - s))


def _gelu(z):
    return 0.5 * z * (1.0 + jnp.tanh(_GELU_C * (z + _GELU_K * z * z * z)))


def _gelu_pair(z):
    zz = z * z
    t = jnp.tanh(_GELU_C * (z * (1.0 + _GELU_K * zz)))
    half = 0.5 * (1.0 + t)
    return z * half, half + 0.5 * z * (1.0 - t * t) * (_GELU_C * (1.0 + 3.0 * _GELU_K * zz))


def _softplus(z):
    return jnp.maximum(z, 0.0) + jnp.log(1.0 + jnp.exp(-jnp.abs(z)))


def _rms(xv):
    return lax.rsqrt(jnp.mean(xv * xv, axis=-1, keepdims=True) + NORM_EPS)


def _rms_bwd(xhat, r, dxhat):
    return r * (dxhat - xhat * jnp.mean(dxhat * xhat, axis=-1, keepdims=True))


def _swap_pairs(y):
    lane = lax.broadcasted_iota(jnp.int32, y.shape, 1)
    return jnp.where(lane % 2 == 0, pltpu.roll(y, LANES - 1, 1), pltpu.roll(y, 1, 1))


def _dot(a, b, ca, cb):
    return lax.dot_general(a, b, (((ca,), (cb,)), ((), ())), preferred_element_type=F32)


_SM_SCALE = HEAD_DIM ** -0.5
_EXP2_SCALE = _SM_SCALE * math.log2(math.e)


def _softmax_parts(qh, kk):
    s = _dot(qh, kk, 1, 1)
    p = jnp.exp2((s - jnp.max(s, axis=-1, keepdims=True)) * _EXP2_SCALE)
    return p, 1.0 / jnp.sum(p, axis=-1, keepdims=True)


def _attn_fwd(q, k, v, *, tq, name):
    s_len, aw = q.shape
    t_len, kvw = k.shape
    gw = Q_GROUP * HEAD_DIM

    def body(q_ref, k_ref, v_ref, o_ref):
        kk, vv = k_ref[...], v_ref[...]
        for g in range(Q_GROUP):
            cols = slice(g * HEAD_DIM, (g + 1) * HEAD_DIM)
            p, inv_l = _softmax_parts(q_ref[:, cols], kk)
            o_ref[:, cols] = (_dot(p.astype(BF16), vv, 1, 0) * inv_l).astype(BF16)

    return pl.pallas_call(
        body, name=name, grid=(kvw // HEAD_DIM, s_len // tq),
        in_specs=[pl.BlockSpec((tq, gw), lambda h, i: (i, h)), pl.BlockSpec((t_len, HEAD_DIM), lambda h, i: (0, h)),
                  pl.BlockSpec((t_len, HEAD_DIM), lambda h, i: (0, h))],
        out_specs=pl.BlockSpec((tq, gw), lambda h, i: (i, h)),
        out_shape=jax.ShapeDtypeStruct((s_len, aw), BF16),
        compiler_params=_params(("parallel", "parallel")),
    )(q, k, v)


def _attn_bwd(q, k, v, do, *, tq, name):
    s_len, aw = q.shape
    t_len, kvw = k.shape
    gw = Q_GROUP * HEAD_DIM

    def body(q_ref, k_ref, v_ref, do_ref, dq_ref, dk_ref, dv_ref):
        @pl.when(pl.program_id(1) == 0)
        def _():
            dk_ref[...] = jnp.zeros_like(dk_ref)
            dv_ref[...] = jnp.zeros_like(dv_ref)

        kk, vv = k_ref[...], v_ref[...]
        for g in range(Q_GROUP):
            cols = slice(g * HEAD_DIM, (g + 1) * HEAD_DIM)
            qh, doh = q_ref[:, cols], do_ref[:, cols]
            p, inv_l = _softmax_parts(qh, kk)
            dp = _dot(doh, vv, 1, 1)
            delta = jnp.sum(p * dp, axis=-1, keepdims=True) * inv_l
            dsb = (p * ((dp - delta) * (inv_l * _SM_SCALE))).astype(BF16)
            dq_ref[:, cols] = _dot(dsb, kk, 1, 0)
            dk_ref[...] += _dot(dsb, qh, 0, 0)
            dv_ref[...] += _dot(p.astype(BF16), (doh * inv_l).astype(BF16), 0, 0)

    qspec = pl.BlockSpec((tq, gw), lambda h, i: (i, h))
    kspec = pl.BlockSpec((t_len, HEAD_DIM), lambda h, i: (0, h))
    return pl.pallas_call(
        body, name=name, grid=(kvw // HEAD_DIM, s_len // tq),
        in_specs=[qspec, kspec, kspec, qspec], out_specs=(qspec, kspec, kspec),
        out_shape=(jax.ShapeDtypeStruct((s_len, aw), F32), jax.ShapeDtypeStruct((t_len, kvw), F32),
                   jax.ShapeDtypeStruct((t_len, kvw), F32)),
        compiler_params=_params(("parallel", "arbitrary")),
    )(q, k, v, do)


SHIFTS = (-2, -1, 1, 2)


def _shift_masks(l_len, t_len):
    t = jnp.arange(t_len)[:, None]
    cols = []
    for d in SHIFTS:
        td = t + d
        ok = ((t < l_len) & (td >= 0) & (td < l_len)) | ((t >= l_len) & (td >= l_len) & (td < t_len))
        cols.append(jnp.broadcast_to(ok.astype(F32), (t_len, RNN_BLOCK)))
    return jnp.concatenate(cols, axis=1)


def _shift_rows(xv, d, m_ref, l_len, t_len):
    if d == 0:
        return xv
    k = SHIFTS.index(d)
    return pltpu.roll(xv, (-d) % t_len, 0) * m_ref[:, k * RNN_BLOCK:(k + 1) * RNN_BLOCK]


def _sublane_scan(a8, b8, rev):
    row = lax.broadcasted_iota(jnp.int32, a8.shape, 0)
    for s in (1, 2, 4):
        if rev:
            ok = row < SUBLANES - s
            sh = SUBLANES - s
        else:
            ok = row >= s
            sh = s
        a_sh = jnp.where(ok, pltpu.roll(a8, sh, 0), 1.0)
        b_sh = jnp.where(ok, pltpu.roll(b8, sh, 0), 0.0)
        b8 = a8 * b_sh + b8
        a8 = a8 * a_sh
    return a8, b8


def _scan_chunks(chains, c0, n, carries):
    def step(kk, hs):
        new = []
        for (a_ref, b_ref, h_ref, rev), hprev in zip(chains, hs):
            c = c0 + (n - 1 - kk if rev else kk)
            r0 = pl.multiple_of(c * SUBLANES, SUBLANES)
            big_a, big_b = _sublane_scan(a_ref[pl.ds(r0, SUBLANES), :], b_ref[pl.ds(r0, SUBLANES), :], rev)
            h8 = big_b + big_a * hprev
            h_ref[pl.ds(r0, SUBLANES), :] = h8
            edge = 0 if rev else SUBLANES - 1
            new.append(jnp.broadcast_to(h8[edge:edge + 1, :], h8.shape))
        return tuple(new)

    return lax.fori_loop(0, n, step, tuple(carries))


def _rglru_conv(x, cw_ref, cb_ref, m_ref, l_len, t_len):
    y = cb_ref[...] + _shift_rows(x, -1, m_ref, l_len, t_len) * cw_ref[0:1, :]
    for kk in range(1, CONV_WIDTH):
        y = y + _shift_rows(x, kk - 1, m_ref, l_len, t_len) * cw_ref[kk:kk + 1, :]
    return y


def _rglru_gates(xc, wr, wi, br, bi, lam):
    xb = xc.astype(BF16)
    r = _sigmoid(_dot(xb, wr.astype(BF16), 1, 0) + br)
    ig = _sigmoid(_dot(xb, wi.astype(BF16), 1, 0) + bi)
    sp = _softplus(-lam)
    log_a = (-LRU_C) * r * sp
    a = jnp.exp(log_a)
    z2 = 2.0 * log_a
    series = z2 * (1.0 + z2 * (0.5 + z2 * (1.0 / 6.0 + z2 * (1.0 / 24.0))))
    one_m = jnp.where(jnp.abs(z2) < 0.03, -series, 1.0 - a * a)
    rinv = lax.rsqrt(jnp.maximum(one_m, 1e-30))
    u = ig * xc
    return r, ig, sp, a, one_m * rinv, u, rinv


def _rglru_specs(proj, xr_blk, xg_blk, t_len):
    col = lambda off: pl.BlockSpec((t_len, RNN_BLOCK), lambda h, off=off: (0, off + h))
    vec = lambda rows: pl.BlockSpec((rows, RNN_BLOCK), lambda h: (0, h))
    wsp = pl.BlockSpec((2, 1, RNN_BLOCK, RNN_BLOCK), lambda h: (0, h, 0, 0))
    msp = pl.BlockSpec((t_len, len(SHIFTS) * RNN_BLOCK), lambda h: (0, 0))
    return col(xr_blk), col(xg_blk), vec, wsp, msp


def _rglru_fwd(proj, xr_blk, xg_blk, masks, conv_w, conv_b, w_rg, w_ig, b_rg, b_ig, lam, *, l_len, name):
    t_len = proj.shape[0]
    s_len = t_len - l_len
    d_rnn = conv_b.shape[1]
    n_l, n_t = l_len // SUBLANES, t_len // SUBLANES

    def body(xr_ref, xg_ref, m_ref, cw_ref, cb_ref, wr_ref, wi_ref, br_ref, bi_ref, lam_ref, rg_ref, hf_ref, hb_ref,
             a_s, b_s):
        xc = _rglru_conv(xr_ref[...], cw_ref, cb_ref, m_ref, l_len, t_len)
        for d in range(2):
            _, _, _, a, mult, u, _ = _rglru_gates(xc, wr_ref[d, 0], wi_ref[d, 0], br_ref[d:d + 1, :],
                                               bi_ref[d:d + 1, :], lam_ref[d:d + 1, :])
            a_s[d] = a
            b_s[d] = mult * u
        chains = [(a_s.at[0], b_s.at[0], hf_ref, False), (a_s.at[1], b_s.at[1], hb_ref, True)]
        zero = jnp.zeros((SUBLANES, RNN_BLOCK), F32)
        carry = _scan_chunks(chains, 0, n_l, (zero, zero))
        _scan_chunks(chains, n_l, n_t - n_l, carry)
        rnn = hf_ref[pl.ds(l_len, s_len), :] + hb_ref[pl.ds(l_len, s_len), :]
        rg_ref[...] = (rnn * _gelu(xg_ref[pl.ds(l_len, s_len), :])).astype(BF16)

    xr_spec, xg_spec, vec, wsp, msp = _rglru_specs(proj, xr_blk, xg_blk, t_len)
    col_t = pl.BlockSpec((t_len, RNN_BLOCK), lambda h: (0, h))
    return pl.pallas_call(
        body, name=name, grid=(d_rnn // RNN_BLOCK,),
        in_specs=[xr_spec, xg_spec, msp, vec(CONV_WIDTH), vec(1), wsp, wsp, vec(2), vec(2), vec(2)],
        out_specs=(pl.BlockSpec((s_len, RNN_BLOCK), lambda h: (0, h)), col_t, col_t),
        out_shape=(jax.ShapeDtypeStruct((s_len, d_rnn), BF16), jax.ShapeDtypeStruct((t_len, d_rnn), F32),
                   jax.ShapeDtypeStruct((t_len, d_rnn), F32)),
        scratch_shapes=[pltpu.VMEM((2, t_len, RNN_BLOCK), F32)] * 2,
        compiler_params=_params(("parallel",)),
    )(proj, proj, masks, conv_w, conv_b, w_rg, w_ig, b_rg, b_ig, lam)


N_RNN_VEC = 16


def _rglru_bwd(proj, xr_blk, xg_blk, masks, h_fwd, h_bwd, drg, conv_w, conv_b, w_rg, w_ig, b_rg, b_ig, lam, *, l_len, name):
    t_len = proj.shape[0]
    s_len = t_len - l_len
    d_rnn = conv_b.shape[1]
    n_l, n_t = l_len // SUBLANES, t_len // SUBLANES

    def body(xr_ref, xg_ref, m_ref, hf_ref, hb_ref, drg_ref, cw_ref, cb_ref, wr_ref, wi_ref, br_ref, bi_ref, lam_ref,
             dxr_ref, dxg_ref, dvec_ref, dwr_ref, dwi_ref, a_s, b_s, h_s):
        t = lax.broadcasted_iota(jnp.int32, (t_len, RNN_BLOCK), 0)
        x = xr_ref[...]
        xc = _rglru_conv(x, cw_ref, cb_ref, m_ref, l_len, t_len)
        xb = xc.astype(BF16)
        gates = []
        for d in range(2):
            gt = _rglru_gates(xc, wr_ref[d, 0], wi_ref[d, 0], br_ref[d:d + 1, :], bi_ref[d:d + 1, :],
                              lam_ref[d:d + 1, :])
            gates.append(gt)
        zero = jnp.zeros((SUBLANES, RNN_BLOCK), F32)
        h0, h1 = hf_ref[...], hb_ref[...]
        xg = xg_ref[...]
        dr = jnp.concatenate([jnp.zeros((l_len, RNN_BLOCK), F32), drg_ref[...].astype(F32)], axis=0)
        gel, dgel = _gelu_pair(xg)
        dxg_ref[...] = (dr * (h0 + h1) * dgel).astype(dxg_ref.dtype)
        drnn = dr * gel
        a_s[0] = pltpu.roll(gates[0][3], t_len - 1, 0)
        a_s[1] = pltpu.roll(gates[1][3], 1, 0)
        b_s[0] = drnn
        b_s[1] = drnn
        hprev = [jnp.where(t == 0, 0.0, pltpu.roll(h0, 1, 0)),
                 jnp.where(t == l_len - 1, 0.0, pltpu.roll(h1, t_len - 1, 0))]
        chains = [(a_s.at[0], b_s.at[0], h_s.at[0], True), (a_s.at[1], b_s.at[1], h_s.at[1], False)]
        carry = _scan_chunks(chains, n_l, n_t - n_l, (zero, zero))
        _scan_chunks(chains, 0, n_l, carry)

        dxc = jnp.zeros((t_len, RNN_BLOCK), F32)
        for d in range(2):
            r, ig, sp, a, mult, u, rinv = gates[d]
            lamb = h_s[d]
            da = lamb * hprev[d]
            dmult = lamb * u
            du = lamb * mult
            dlog_a = da * a - dmult * (a * a) * rinv
            dzr = (dlog_a * ((-LRU_C) * sp)) * r * (1.0 - r)
            dsp = jnp.sum(dlog_a * ((-LRU_C) * r), axis=0, keepdims=True)
            dzi = (du * xc) * ig * (1.0 - ig)
            dzrb, dzib = dzr.astype(BF16), dzi.astype(BF16)
            dxc = dxc + du * ig + _dot(dzrb, wr_ref[d, 0].astype(BF16), 1, 1) + _dot(dzib, wi_ref[d, 0].astype(BF16), 1, 1)
            dwr_ref[d, 0] = _dot(xb, dzrb, 0, 0)
            dwi_ref[d, 0] = _dot(xb, dzib, 0, 0)
            dvec_ref[5 + d:6 + d, :] = jnp.sum(dzr, axis=0, keepdims=True)
            dvec_ref[7 + d:8 + d, :] = jnp.sum(dzi, axis=0, keepdims=True)
            dvec_ref[9 + d:10 + d, :] = dsp * (-_sigmoid(-lam_ref[d:d + 1, :]))
        dvec_ref[4:5, :] = jnp.sum(dxc, axis=0, keepdims=True)
        dvec_ref[11:N_RNN_VEC, :] = jnp.zeros((N_RNN_VEC - 11, RNN_BLOCK), F32)
        dxr = jnp.zeros((t_len, RNN_BLOCK), F32)
        for kk in range(CONV_WIDTH):
            dvec_ref[kk:kk + 1, :] = jnp.sum(dxc * _shift_rows(x, kk - 1, m_ref, l_len, t_len), axis=0, keepdims=True)
            dxr = dxr + _shift_rows(dxc, 1 - kk, m_ref, l_len, t_len) * cw_ref[kk:kk + 1, :]
        dxr_ref[...] = dxr.astype(dxr_ref.dtype)

    xr_spec, xg_spec, vec, wsp, msp = _rglru_specs(proj, xr_blk, xg_blk, t_len)
    col_t = pl.BlockSpec((t_len, RNN_BLOCK), lambda h: (0, h))
    return pl.pallas_call(
        body, name=name, grid=(d_rnn // RNN_BLOCK,),
        in_specs=[xr_spec, xg_spec, msp, col_t, col_t, pl.BlockSpec((s_len, RNN_BLOCK), lambda h: (0, h)), vec(CONV_WIDTH),
                  vec(1),
                  wsp, wsp, vec(2), vec(2), vec(2)],
        out_specs=(col_t, col_t, vec(N_RNN_VEC), wsp, wsp),
        out_shape=(jax.ShapeDtypeStruct((t_len, d_rnn), BF16), jax.ShapeDtypeStruct((t_len, d_rnn), BF16),
                   jax.ShapeDtypeStruct((N_RNN_VEC, d_rnn), F32), jax.ShapeDtypeStruct(w_rg.shape, F32),
                   jax.ShapeDtypeStruct(w_ig.shape, F32)),
        scratch_shapes=[pltpu.VMEM((2, t_len, RNN_BLOCK), F32)] * 3,
        compiler_params=_params(("parallel",)),
    )(proj, proj, masks, h_fwd, h_bwd, drg, conv_w, conv_b, w_rg, w_ig, b_rg, b_ig, lam)


def _colsum(v):
    return jnp.sum(v, axis=0, keepdims=True)


def _qk_fwd_heads(xv, gain, cos_e, sin_e):
    outs = []
    for h in range(xv.shape[1] // HEAD_DIM):
        xh = xv[:, h * HEAD_DIM:(h + 1) * HEAD_DIM]
        y = xh * _rms(xh) * gain
        outs.append(y * cos_e + _swap_pairs(y) * sin_e)
    return outs[0] if len(outs) == 1 else jnp.concatenate(outs, axis=1)


def _qk_bwd_heads(xv, dov, gain, cos_e, sin_e):
    outs = []
    dgain = jnp.zeros((1, HEAD_DIM), F32)
    for h in range(xv.shape[1] // HEAD_DIM):
        cols = slice(h * HEAD_DIM, (h + 1) * HEAD_DIM)
        xh, doh = xv[:, cols], dov[:, cols]
        r = _rms(xh)
        xhat = xh * r
        dy = doh * cos_e + _swap_pairs(doh * sin_e)
        dgain = dgain + _colsum(dy * xhat)
        outs.append(_rms_bwd(xhat, r, dy * gain))
    return (outs[0] if len(outs) == 1 else jnp.concatenate(outs, axis=1)), dgain


WEIGHTS = ['c_ctx', 'w_mod', 'b_mod', 'g_mix', 'g_mlp', 'w_in', 'q_gain', 'k_gain', 'conv_w', 'conv_b', 'w_rg', 'b_rg',
           'w_ig', 'b_ig', 'lru_lambda', 'w_o_attn', 'w_o_rnn', 'w_out', 'w_up', 'w_down', 'g_final']
REPLICATED = ['c_ctx', 'b_mod', 'g_mix', 'g_mlp', 'q_gain', 'k_gain', 'conv_b', 'w_rg', 'w_ig', 'g_final']
SMALL_SHARDED = [('conv_w', CONV_WIDTH), ('b_rg', 2), ('b_ig', 2), ('lru_lambda', 2)]
SMALL_ROWS = 16


PACK_ROWS = 512


def _pack_flat(arrs, rows_mult):
    flat = jnp.concatenate([a.reshape(-1) for a in arrs])
    n = flat.shape[0]
    unit = rows_mult * LANES
    pad = (-n) % unit
    if pad:
        flat = jnp.concatenate([flat, jnp.zeros((pad,), flat.dtype)])
    return flat.reshape(-1, LANES)


def _unpack_flat(packed, shapes):
    flat = packed.reshape(-1)
    out, pos = [], 0
    for shp in shapes:
        n = math.prod(shp)
        out.append(flat[pos:pos + n].reshape(shp))
        pos += n
    return out


def _pack_small(d):
    rows = [d[n][0] for n, _ in SMALL_SHARDED]
    cols = rows[0].shape[1]
    used = sum(k for _, k in SMALL_SHARDED)
    return jnp.concatenate(rows + [jnp.zeros((SMALL_ROWS - used, cols), F32)], axis=0)


def _unpack_small(p):
    out, pos = {}, 0
    for n, k in SMALL_SHARDED:
        out[n] = p[pos:pos + k][None]
        pos += k
    return out


def kernel(x, c, ctx, c_ctx, w_mod, b_mod, g_mix, g_mlp, w_in, q_gain, k_gain, conv_w, conv_b, w_rg, b_rg, w_ig, b_ig, lru_lambda, w_o_attn, w_o_rnn, w_out, w_up, w_down, g_final, loss_target, m_c_ctx, m_w_mod, m_b_mod, m_g_mix, m_g_mlp, m_w_in, m_q_gain, m_k_gain, m_conv_w, m_conv_b, m_w_rg, m_b_rg, m_w_ig, m_b_ig, m_lru_lambda, m_w_o_attn, m_w_o_rnn, m_w_out, m_w_up, m_w_down, m_g_final, v_c_ctx, v_w_mod, v_b_mod, v_g_mix, v_g_mlp, v_w_in, v_q_gain, v_k_gain, v_conv_w, v_conv_b, v_w_rg, v_b_rg, v_w_ig, v_b_ig, v_lru_lambda, v_w_o_attn, v_w_o_rnn, v_w_out, v_w_up, v_w_down, v_g_final):
    wd = dict(c_ctx=c_ctx, w_mod=w_mod, b_mod=b_mod, g_mix=g_mix, g_mlp=g_mlp, w_in=w_in, q_gain=q_gain, k_gain=k_gain,
              conv_w=conv_w, conv_b=conv_b, w_rg=w_rg, b_rg=b_rg, w_ig=w_ig, b_ig=b_ig, lru_lambda=lru_lambda,
              w_o_attn=w_o_attn, w_o_rnn=w_o_rnn, w_out=w_out, w_up=w_up, w_down=w_down, g_final=g_final)
    md = dict(zip(WEIGHTS, (m_c_ctx, m_w_mod, m_b_mod, m_g_mix, m_g_mlp, m_w_in, m_q_gain, m_k_gain, m_conv_w, m_conv_b,
                            m_w_rg, m_b_rg, m_w_ig, m_b_ig, m_lru_lambda, m_w_o_attn, m_w_o_rnn, m_w_out, m_w_up,
                            m_w_down, m_g_final)))
    vd = dict(zip(WEIGHTS, (v_c_ctx, v_w_mod, v_b_mod, v_g_mix, v_g_mlp, v_w_in, v_q_gain, v_k_gain, v_conv_w, v_conv_b,
                            v_w_rg, v_b_rg, v_w_ig, v_b_ig, v_lru_lambda, v_w_o_attn, v_w_o_rnn, v_w_out, v_w_up,
                            v_w_down, v_g_final)))

    xs, tgt, ctxs = x[0], loss_target[0], ctx[0]
    s_len, d = xs.shape
    l_len = ctxs.shape[0]
    t_len = s_len + l_len
    n_in = w_in.shape[2] * N_DEV
    aw = (n_in - 4 * d) * 2 // 3
    kvw = aw // Q_GROUP
    assert aw == d and n_in == aw + 2 * kvw + 4 * d
    nmod = N_MOD * d // N_DEV
    tr = 128
    assert l_len % tr == 0 and s_len % tr == 0
    lb = l_len // tr
    tq = min(256, s_len)
    my = _lin(_my_pos())
    c_ctx2 = c_ctx[None]
    g_final2 = g_final[None]

    c_all = _all_gather([jnp.broadcast_to(c, (SUBLANES, d))], [0], "ag_cond")[0][::SUBLANES]
    c16 = jnp.concatenate([c_all, jnp.broadcast_to(c_ctx2, (N_DEV, d))], axis=0)
    b_mod_sh = lax.dynamic_slice(b_mod, (0, my * nmod), (1, nmod))
    mod_part = _mm(c16, w_mod[0], a_fn=_silu, bias=b_mod_sh, name="mod_fwd")
    sp_w = _pack_small(wd)
    (mod_g, sp_f), order = _all_gather([mod_part, sp_w], [1, 1], "ag_mod", with_token=True)
    conv_w_f, b_rg_f, b_ig_f, lam_f = sp_f[0:4], sp_f[4:6], sp_f[6:8], sp_f[8:10]
    mod_lat = lax.dynamic_slice(mod_g, (my, 0), (1, N_MOD * d))
    sh_a, sc_a, ga_a, sh_f, sc_f, ga_f = [mod_lat[:, k * d:(k + 1) * d] for k in range(N_MOD)]
    csh_a, csc_a = mod_g[N_DEV:N_DEV + 1, :d], mod_g[N_DEV:N_DEV + 1, d:2 * d]

    def nm1_fwd(i, j, cv, lv, g, csh, csc, sh, sc):
        is_ctx = i < lb
        xv = jnp.where(is_ctx, cv, lv)
        n = xv * _rms(xv) * g
        return n * (1.0 + jnp.where(is_ctx, csc, sc)) + jnp.where(is_ctx, csh, sh)

    h_all = _rowwise(nm1_fwd, [_row(ctxs), _row(xs, 0, -lb), _bc(g_mix), _bc(csh_a), _bc(csc_a), _bc(sh_a), _bc(sc_a)],
                     [("row", BF16)], nr=t_len // tr, tr=tr, nc=1, cw=d, name="norm1_fwd")[0]
    w_in_f, proj, order = _gather_proj(h_all, _tok(w_in[0], order).astype(BF16), "ag_proj")
    late_axis = dict(w_o_attn=0, w_o_rnn=0, w_out=0, w_up=1, w_down=0)
    placed = [_place_block(wd[n][0], ax, my, "place_" + n) for n, ax in late_axis.items()]
    late_kind = lambda n: "gather_chips" if n in ("w_up", "w_down") else "gather"
    late = {}
    for (n, ax), full in zip(late_axis.items(), placed):
        late[n] = _xfer_start(late_kind(n), order, full, ax, "ag_start_" + n)
        order = late[n][4]
    q_gain_fwd = _tok(q_gain, order)

    late_wait = lambda n, after: _xfer_wait(late_kind(n), late[n], late_axis[n], after, "ag_wait_" + n)

    def sibling_start(n, after):
        arrived = late_wait(n, after)
        return _xfer_start("sibling", jnp.zeros((SUBLANES, LANES), F32), arrived, late_axis[n], "ag_sib_start_" + n)

    sibling_wait = lambda n, handle, after: _xfer_wait("sibling", handle, late_axis[n], after, "ag_sib_wait_" + n)

    pos = jnp.arange(s_len)
    n_freq = HEAD_DIM // 4
    inv_freq = ROPE_THETA ** (-jnp.arange(n_freq, dtype=F32) / n_freq)
    ang = jnp.concatenate([(pos // GRID_W).astype(F32)[:, None] * inv_freq,
                           (pos % GRID_W).astype(F32)[:, None] * inv_freq], axis=-1)
    sign = jnp.where(jnp.arange(HEAD_DIM) % 2 == 0, -1.0, 1.0).astype(F32)
    cos_e = jnp.concatenate([jnp.ones((l_len, HEAD_DIM), F32), jnp.repeat(jnp.cos(ang), 2, axis=-1)], axis=0)
    sin_e = jnp.concatenate([jnp.zeros((l_len, HEAD_DIM), F32), jnp.repeat(jnp.sin(ang), 2, axis=-1) * sign], axis=0)


    kb = aw // kvw
    q = _rowwise(lambda i, j, xv, gain, ce, se: _qk_fwd_heads(xv, gain, ce, se),
                 [_row(proj, 0, lb, aw), _full(q_gain_fwd), _row(cos_e, 0, lb, HEAD_DIM), _row(sin_e, 0, lb, HEAD_DIM)],
                 [("row", BF16, aw)], nr=s_len // tr, tr=tr, nc=1, cw=aw, name="q_fwd")[0]
    k_all, v_all = _rowwise(lambda i, j, kv_, vv_, gain, ce, se: (_qk_fwd_heads(kv_, gain, ce, se), vv_),
                            [_row(proj, kb, 0, kvw), _row(proj, kb + 1, 0, kvw), _full(k_gain),
                             _row(cos_e, 0, 0, HEAD_DIM), _row(sin_e, 0, 0, HEAD_DIM)],
                            [("row", BF16, kvw), ("row", BF16, kvw)], nr=t_len // tr, tr=tr, nc=1, cw=kvw, name="kv_fwd")
    attn_o = _attn_fwd(q, k_all, v_all, tq=tq, name="attn_fwd")

    xr_blk = (aw + 2 * kvw) // RNN_BLOCK
    xg_blk = xr_blk + d // RNN_BLOCK
    shift_masks = _shift_masks(l_len, t_len)
    rg, h_fwd, h_bwd = _rglru_fwd(proj, xr_blk, xg_blk, shift_masks, conv_w_f, conv_b, w_rg[0], w_ig[0], b_rg_f, b_ig_f, lam_f, l_len=l_len,
                    name="rglru_fwd")

    w_oa_f = late_wait("w_o_attn", attn_o)
    y_attn = _mm(attn_o, w_oa_f, name="o_attn_fwd")
    w_or_f = late_wait("w_o_rnn", rg)
    y_rnn = _mm(rg, w_or_f, name="o_rnn_fwd")
    hw = d // 2
    glb = (aw + 2 * kvw + 2 * d) // hw

    def merge_fwd(i, j, ya, yr, gla, glr):
        return _sigmoid(gla) * ya + _sigmoid(glr) * yr

    u = _rowwise(merge_fwd, [_row(y_attn), _row(y_rnn), _row(proj, glb, lb), _row(proj, glb + 2, lb)],
                 [("row", BF16)], nr=s_len // tr, tr=tr, nc=2, cw=hw, name="merge_fwd")[0]
    w_out_f = late_wait("w_out", u)
    sib_up = sibling_start("w_up", u)
    mix = _mm(u, w_out_f, name="out_fwd")

    def res1_fwd(i, j, xv, mixv, gaa, gm, shf, scf):
        x1v = xv + gaa * mixv
        n = x1v * _rms(x1v) * gm
        return x1v, n * (1.0 + scf) + shf

    x1, h2 = _rowwise(res1_fwd, [_row(xs), _row(mix), _bc(_tok(ga_a, sib_up[4])), _bc(g_mlp), _bc(sh_f), _bc(sc_f)],
                      [("row", F32), ("row", BF16)], nr=s_len // tr, tr=tr, nc=1, cw=d, name="norm2_fwd")
    relu2 = lambda z: jnp.square(jnp.maximum(z, 0.0))
    w_up_f = sibling_wait("w_up", sib_up, h2)
    up = _mm(h2, w_up_f, name="up_fwd")
    w_down_f = sibling_wait("w_down", sibling_start("w_down", up), up)
    down = _mm(up, w_down_f, a_fn=relu2, name="down_fwd")

    def final_fn(i, j, x1v, dn, tg, gaf, gf):
        x2 = x1v + gaf * dn
        r = _rms(x2)
        xh = x2 * r
        e = xh * gf - tg
        dy = e * (1.0 / d)
        dx2 = _rms_bwd(xh, r, dy * gf)
        return dx2, dx2 * gaf, _colsum(e * e) * (0.5 / d), _colsum(dy * xh), _colsum(dx2 * dn)

    dx2, ddown, loss_cols, g_gfinal, d_ga_f = _rowwise(
        final_fn, [_row(x1), _row(down), _row(tgt), _bc(ga_f), _bc(g_final2)],
        [("row", F32), ("row", BF16), ("acc", 1), ("acc", 1), ("acc", 1)], nr=s_len // tr, tr=tr, nc=1, cw=d,
        name="final_bwd")
    loss = lax.psum(jnp.sum(loss_cols), ("x", "y", "c"))

    dup = _mm(ddown, w_down_f, tb=True, out_fn=lambda r, upv: r * (2.0 * jnp.maximum(upv, 0.0)), extras=[up],
              out_dtype=BF16, name="down_bwd_x")
    gw_down = _mm(up, ddown, ta=True, a_fn=relu2, out_dtype=BF16, name="down_bwd_w")
    sent = dict(w_down=_scatter_start(gw_down, 0, my, "rs_start_w_down"))
    dh2 = _mm(dup, w_up_f, tb=True, name="up_bwd_x")
    gw_up = _mm(h2, dup, ta=True, out_dtype=BF16, name="up_bwd_w")
    sent["w_up"] = _scatter_start(gw_up, 1, my, "rs_start_w_up")
    sc_f_bwd = _tok(sc_f, sent["w_down"][0][4], sent["w_up"][0][4])

    def norm2_bwd(i, j, x1v, mixv, dh2v, dx2v, gaa, gm, scf):
        r = _rms(x1v)
        xh = x1v * r
        n = xh * gm
        dn = dh2v * (1.0 + scf)
        dx1 = dx2v + _rms_bwd(xh, r, dn * gm)
        return dx1, dx1 * gaa, _colsum(dh2v), _colsum(dh2v * n), _colsum(dn * xh), _colsum(dx1 * mixv)

    dx1, dmix, d_sh_f, d_sc_f, g_gmlp, d_ga_a = _rowwise(
        norm2_bwd, [_row(x1), _row(mix), _row(dh2), _row(dx2), _bc(ga_a), _bc(g_mlp), _bc(sc_f_bwd)],
        [("row", F32), ("row", BF16), ("acc", 1), ("acc", 1), ("acc", 1), ("acc", 1)], nr=s_len // tr, tr=tr, nc=1,
        cw=d, name="norm2_bwd")

    du = _mm(dmix, w_out_f, tb=True, name="out_bwd_x")
    gw_out = _mm(u, dmix, ta=True, out_dtype=BF16, name="out_bwd_w")
    sent["w_out"] = _scatter_start(gw_out, 0, my, "rs_start_w_out")

    def merge_bwd(i, j, duv, ya, yr, gla, glr):
        ga, gr = _sigmoid(gla), _sigmoid(glr)
        return duv * ga, duv * gr, duv * ya * ga * (1.0 - ga), duv * yr * gr * (1.0 - gr)

    dya, dyr, dgla, dglr = _rowwise(
        merge_bwd, [_row(du), _row(y_attn), _row(y_rnn), _row(proj, glb, lb), _row(proj, glb + 2, lb)],
        [("row", BF16)] * 4, nr=s_len // tr, tr=tr, nc=2, cw=hw, name="merge_bwd")
    dattn = _mm(dya, w_oa_f, tb=True, out_dtype=BF16, name="o_attn_bwd_x")
    gw_oa = _mm(attn_o, dya, ta=True, out_dtype=BF16, name="o_attn_bwd_w")
    drg = _mm(dyr, w_or_f, tb=True, out_dtype=BF16, name="o_rnn_bwd_x")
    gw_or = _mm(rg, dyr, ta=True, out_dtype=BF16, name="o_rnn_bwd_w")
    sent["w_o_attn"] = _scatter_start(gw_oa, 0, my, "rs_start_w_o_attn")
    sent["w_o_rnn"] = _scatter_start(gw_or, 0, my, "rs_start_w_o_rnn")
    conv_b_bwd = _tok(conv_b, sent["w_out"][0][4], sent["w_o_attn"][0][4], sent["w_o_rnn"][0][4])

    dq, dk, dv = _attn_bwd(q, k_all, v_all, dattn, tq=tq, name="attn_bwd")
    dxr, dxg, dvec, g_wrg, g_wig = _rglru_bwd(proj, xr_blk, xg_blk, shift_masks, h_fwd, h_bwd, drg, conv_w_f, conv_b_bwd, w_rg[0], w_ig[0], b_rg_f,
                                               b_ig_f, lam_f, l_len=l_len, name="rglru_bwd")
    gate_part = _pack_flat([g_wrg, g_wig], N_DEV * SUBLANES)
    gate_sent = _scatter_start(gate_part, 0, my, "rs_start_gates")
    q_gain_bwd = _tok(q_gain, gate_sent[0][4])

    dq_p, g_qgain = _rowwise(lambda i, j, xv, dov, gain, ce, se: _qk_bwd_heads(xv, dov, gain, ce, se),
                             [_row(proj, 0, lb, aw), _row(dq, 0, 0, aw), _full(q_gain_bwd), _row(cos_e, 0, lb, HEAD_DIM),
                              _row(sin_e, 0, lb, HEAD_DIM)],
                             [("row", BF16, aw), ("acc", 1, HEAD_DIM)], nr=s_len // tr, tr=tr, nc=1, cw=aw, name="q_bwd")

    def kv_bwd(i, j, xv, dkv, dvv, gain, ce, se):
        dk_p, dgain = _qk_bwd_heads(xv, dkv, gain, ce, se)
        return dk_p, dvv, dgain

    dk_p, dv_p, g_kgain = _rowwise(kv_bwd, [_row(proj, kb, 0, kvw), _row(dk, 0, 0, kvw), _row(dv, 0, 0, kvw),
                                            _full(k_gain), _row(cos_e, 0, 0, HEAD_DIM), _row(sin_e, 0, 0, HEAD_DIM)],
                                   [("row", BF16, kvw), ("row", BF16, kvw), ("acc", 1, HEAD_DIM)], nr=t_len // tr, tr=tr,
                                   nc=1, cw=kvw, name="kv_bwd")

    def dproj_fn(i, j, dqv, dkv, dvv, dxrv, dxgv, dgav, dgrv):
        zero_ctx = lambda a: jnp.where(i < lb, jnp.zeros_like(a), a)
        return jnp.concatenate([zero_ctx(dqv), dkv, dvv, dxrv, dxgv, zero_ctx(dgav), zero_ctx(dgrv)], axis=1)

    dproj = _rowwise(dproj_fn, [_row(dq_p, 0, -lb, aw), _row(dk_p, 0, 0, kvw), _row(dv_p, 0, 0, kvw), _row(dxr, 0, 0, d),
                                _row(dxg, 0, 0, d), _row(dgla, 0, -lb, d), _row(dglr, 0, -lb, d)],
                     [("row", BF16, n_in)], nr=t_len // tr, tr=tr, nc=1, cw=n_in, name="dproj_pack")[0]
    dh_all = _mm(dproj, w_in_f, tb=True, name="proj_bwd_x")
    g_mix_bwd = g_mix

    def norm1_bwd(i, j, xv, dhv, dresv, g, sc):
        r = _rms(xv)
        xh = xv * r
        dn = dhv * (1.0 + sc)
        return dresv + _rms_bwd(xh, r, dn * g), _colsum(dhv), _colsum(dhv * (xh * g)), _colsum(dn * xh)

    def norm1_bwd_ctx(i, j, xv, dhv, g, sc):
        xh = xv * _rms(xv)
        return _colsum(dhv), _colsum(dhv * (xh * g)), _colsum(dhv * (1.0 + sc) * xh)

    dx_lat, d_sh_a, d_sc_a, g_gmix_lat = _rowwise(
        norm1_bwd, [_row(xs), _row(dh_all, 0, lb), _row(dx1), _bc(g_mix_bwd), _bc(sc_a)],
        [("row", F32)] + [("acc", 1)] * 3, nr=s_len // tr, tr=tr, nc=1, cw=d, name="norm1_bwd")
    d_csh, d_csc, g_gmix_ctx = _rowwise(
        norm1_bwd_ctx, [_row(ctxs), _row(dh_all), _bc(g_mix_bwd), _bc(csc_a)],
        [("acc", 1)] * 3, nr=lb, tr=tr, nc=1, cw=d, name="norm1_bwd_ctx")
    g_gmix = g_gmix_lat + g_gmix_ctx
    grad_x = dx_lat[None]

    dmod_lat = jnp.concatenate([d_sh_a, d_sc_a, d_ga_a, d_sh_f, d_sc_f, d_ga_f], axis=1)
    dmod_ctx = jnp.concatenate([d_csh, d_csc, jnp.zeros((1, (N_MOD - 2) * d), F32)], axis=1)
    dm_g = _all_gather([jnp.concatenate([dmod_lat, dmod_ctx, jnp.zeros((SUBLANES - 2, N_MOD * d), F32)], axis=0)], [0],
                       "ag_dmod")[0]
    dm16 = jnp.concatenate([dm_g[0::SUBLANES], dm_g[1::SUBLANES]], axis=0)
    dm_sh = lax.dynamic_slice(dm16, (0, my * nmod), (2 * N_DEV, nmod))
    gw_mod = _mm(c16, dm_sh, ta=True, a_fn=_silu, name="mod_bwd_w")
    dc16 = _mm(dm_sh, w_mod[0], tb=True, name="mod_bwd_x")

    def cctx_fn(i, j, pv, cc, dl, dc):
        row = lax.broadcasted_iota(jnp.int32, pv.shape, 0)
        return _colsum(jnp.where(row >= N_DEV, pv, 0.0)) * _dsilu(cc), dl + dc

    g_cctx, g_bmod = _rowwise(cctx_fn, [_full(dc16), _full(c_ctx2), _full(dmod_lat), _full(dmod_ctx)],
                              [("row", F32, d), ("row", F32, N_MOD * d)], nr=1, tr=1, nc=1, cw=d, name="cctx_bwd")

    rep_part = dict(c_ctx=g_cctx, b_mod=g_bmod, g_mix=g_gmix, g_mlp=g_gmlp, q_gain=g_qgain, k_gain=g_kgain,
                    conv_b=dvec[4:5], g_final=g_gfinal)
    rep_small = [n for n in REPLICATED if n in rep_part]
    gates = [n for n in REPLICATED if n not in rep_part]
    part_pack = _pack_flat([rep_part[n] for n in rep_small], PACK_ROWS)
    rep_rows = part_pack.shape[0]
    gate_land = _xfer_wait("scatter", gate_sent[0], 0, g_bmod, "rs_wait_gates")
    gate_sum = _sum_parts(gate_sent[1], gate_land, name="gate_sum")
    (gate_full, parts), order = _all_gather([gate_sum, part_pack], [0, 0], "ag_rep_grads", with_token=True)
    small_g = jnp.concatenate([dvec[0:4], dvec[5:11], jnp.zeros((SMALL_ROWS - 10, d), F32)], axis=0)
    (small_recv,), order = _all_to_all([_tok(small_g, order)], [1], "a2a_small_grads")

    halves = {}
    for part, lo in (("top", 0), ("bot", d // 2)):
        gw_part = _mm(h_all, dproj, ta=True, a_cols=(lo, d // 2), out_dtype=BF16, name="proj_bwd_w_" + part,
                      bias=_tok(jnp.zeros((1, n_in), F32), order))
        halves[part] = _chip_scatter_start(_pair_reduce(gw_part, my, "pair_w_in_" + part), my, "rs_start_w_in_" + part)
        order = halves[part][0][4]
    started = order

    res = {}
    for names, pr, mult, nm in ((rep_small, parts.reshape(N_DEV, rep_rows, LANES), PACK_ROWS, "adamw_rep"),
                                (gates, gate_full[None], N_DEV * SUBLANES, "adamw_gates")):
        rep_out = _adamw(pr, _tok(_pack_flat([wd[n] for n in names], mult), started),
                         _pack_flat([md[n] for n in names], mult), _pack_flat([vd[n] for n in names], mult), name=nm)
        for kind, packed in zip(("g", "d", "m", "v"), rep_out):
            for n, a in zip(names, _unpack_flat(packed, [wd[n].shape for n in names])):
                res[kind, n] = a
    out = _adamw(small_recv, _tok(sp_w, started), _pack_small(md), _pack_small(vd), name="adamw_small")
    for kind, a in zip(("g", "d", "m", "v"), out):
        for n, piece in _unpack_small(a).items():
            res[kind, n] = piece

    after = started
    for n, (handle, own, ax, kind) in sent.items():
        land = _xfer_wait(kind, handle, ax, after, "rs_wait_" + n)
        out = _adamw(land, wd[n][0], md[n][0], vd[n][0], own=own, name="adamw_" + n)
        after = out[0]
        for kind, a in zip(("g", "d", "m", "v"), out):
            res[kind, n] = a[None]
    out = _adamw(gw_mod[None], w_mod[0], m_w_mod[0], v_w_mod[0], name="adamw_w_mod")
    after = out[0]
    for kind, a in zip(("g", "d", "m", "v"), out):
        res[kind, 'w_mod'] = a[None]
    lands, owns = [], []
    for part, (handle, own, ax, kind) in halves.items():
        lands.append(_xfer_wait(kind, handle, ax, after, "rs_wait_w_in_" + part))
        owns.append(own)
        after = lands[-1]
    out = _adamw(jnp.concatenate(lands, axis=1), w_in[0], m_w_in[0], v_w_in[0], own=jnp.concatenate(owns, axis=0),
                 name="adamw_w_in")
    for kind, a in zip(("g", "d", "m", "v"), out):
        res[kind, 'w_in'] = a[None]

    outs = [loss, grad_x]
    for kind in ("g", "d", "m", "v"):
        outs += [res[kind, n].reshape(wd[n].shape) for n in WEIGHTS]
    return tuple(outs)
```

```python
import math

import jax
import jax.numpy as jnp
from jax import lax
from jax.experimental import pallas as pl
from jax.experimental.pallas import tpu as pltpu

F32 = jnp.float32
BF16 = jnp.bfloat16
MESH = pl.DeviceIdType.MESH

N_DEV = 8
HEAD_DIM = 128
Q_GROUP = 4
RNN_BLOCK = 128
GRID_W = 64
CONV_WIDTH = 4
LRU_C = 8.0
ROPE_THETA = 10000.0
NORM_EPS = 1e-6
N_MOD = 6
ADAM_LR = 0.001
ADAM_B1 = 0.9
ADAM_B2 = 0.999
ADAM_EPS = 1e-08
ADAM_WD = 0.01
ADAM_STEP = 10

LANES = 128
SUBLANES = 8
VMEM_LIMIT = 56 * 1024 * 1024
HBM_SPEC = pl.BlockSpec(memory_space=pltpu.HBM)


def _params(sem=None, vmem=VMEM_LIMIT):
    kw = dict(vmem_limit_bytes=vmem)
    if sem is not None:
        kw["dimension_semantics"] = sem
    return pltpu.CompilerParams(**kw)


def _pick(dim, target, mult):
    for t in range(min(dim, target), 0, -1):
        if dim % t == 0 and t % mult == 0:
            return t
    return dim


def _my_pos():
    return lax.axis_index("x"), lax.axis_index("y"), lax.axis_index("c")


def _flip(pos, d):
    x, y, c = pos
    return (1 - x if d & 4 else x, 1 - y if d & 2 else y, 1 - c if d & 1 else c)


def _lin(pos):
    return 4 * pos[0] + 2 * pos[1] + pos[2]


def _slab(ref, axis, idx, size):
    if axis == 0:
        return ref.at[pl.ds(idx * size, size)]
    assert axis == 1
    return ref.at[:, pl.ds(idx * size, size)]


def _all_gather(arrs, axes, name, with_token=False):
    n = len(arrs)
    sizes = [a.shape[ax] for a, ax in zip(arrs, axes)]

    def body(*refs):
        ins, outs, token = refs[:n], refs[n:2 * n], refs[2 * n]
        send_sems, recv_sems, local_sems = refs[2 * n + 1:]
        token[...] = jnp.zeros_like(token)
        me = _my_pos()
        sibling = _flip(me, 1)
        chips = [2, 4, 6]

        def copy(a, k, block_pos, to, src=None):
            dst = _slab(outs[a], axes[a], _lin(block_pos), sizes[a])
            return pltpu.make_async_remote_copy(
                src_ref=dst if src is None else src, dst_ref=dst,
                send_sem=send_sems.at[7 * a + k], recv_sem=recv_sems.at[7 * a + k],
                device_id=to, device_id_type=MESH)

        local, first, passed = [], [], []
        for a in range(n):
            cp = pltpu.make_async_copy(ins[a], _slab(outs[a], axes[a], _lin(me), sizes[a]), local_sems.at[a])
            cp.start()
            local.append(cp)
        for a in range(n):
            f = [copy(a, 0, me, sibling, src=ins[a])]
            f += [copy(a, 1 + j, me, _flip(me, m), src=ins[a]) for j, m in enumerate(chips)]
            for cp in f:
                cp.start()
            first += f
        for a in range(n):
            for j, m in enumerate(chips):
                copy(a, 1 + j, _flip(me, m), me).wait_recv()
                cp = copy(a, 4 + j, _flip(me, m), sibling)
                cp.start()
                passed.append(cp)
        for a in range(n):
            copy(a, 0, sibling, me).wait_recv()
            for j, m in enumerate(chips):
                copy(a, 4 + j, _flip(sibling, m), me).wait_recv()
        for cp in first + passed:
            cp.wait_send()
        for cp in local:
            cp.wait()

    out_shape = []
    for a, ax in zip(arrs, axes):
        shp = list(a.shape)
        shp[ax] *= N_DEV
        out_shape.append(jax.ShapeDtypeStruct(tuple(shp), a.dtype))
    out_shape.append(jax.ShapeDtypeStruct((SUBLANES, LANES), F32))
    res = pl.pallas_call(
        body, name=name, out_shape=tuple(out_shape),
        in_specs=[HBM_SPEC] * n, out_specs=tuple([HBM_SPEC] * n + [pl.BlockSpec(memory_space=pltpu.VMEM)]),
        scratch_shapes=[pltpu.SemaphoreType.DMA((7 * n,)), pltpu.SemaphoreType.DMA((7 * n,)),
                        pltpu.SemaphoreType.DMA((n,))],
    )(*arrs)
    return (list(res[:n]), res[n]) if with_token else list(res[:n])


def _gather_proj(h_all, w_shard, name):
    t_len, kdim = h_all.shape
    sz = w_shard.shape[1]
    tm = _pick(t_len, 1024, 16)
    n_m = t_len // tm

    def body(h_ref, w_ref, wfull_ref, proj_ref, token, a_buf, b_buf, o_buf, send_sems, recv_sems, io_sems, out_sems):
        token[...] = jnp.zeros_like(token)
        me = _my_pos()
        sibling = _flip(me, 1)
        chips = [2, 4, 6]
        slab = lambda pos: _slab(wfull_ref, 1, _lin(pos), sz)

        def copy(k, block_pos, to, src=None):
            return pltpu.make_async_remote_copy(
                src_ref=slab(block_pos) if src is None else src, dst_ref=slab(block_pos),
                send_sem=send_sems.at[k], recv_sem=recv_sems.at[k], device_id=to, device_id_type=MESH)

        pending = []

        def block_product(pos, b_src):
            load = pltpu.make_async_copy(b_src, b_buf, io_sems.at[1])
            load.start()
            load.wait()
            for i in range(n_m):
                slot = len(pending) % 2
                if len(pending) >= 2:
                    pending[-2].wait()
                o_buf[slot] = _dot(a_buf[pl.ds(i * tm, tm), :], b_buf[...], 1, 0)
                out = pltpu.make_async_copy(
                    o_buf.at[slot], proj_ref.at[pl.ds(i * tm, tm), pl.ds(_lin(pos) * sz, sz)], out_sems.at[slot])
                out.start()
                pending.append(out)

        local = pltpu.make_async_copy(w_ref, slab(me), io_sems.at[0])
        local.start()
        first = [copy(0, me, sibling, src=w_ref)] + [copy(1 + j, me, _flip(me, m), src=w_ref) for j, m in enumerate(chips)]
        for cp in first:
            cp.start()
        load_a = pltpu.make_async_copy(h_ref, a_buf, io_sems.at[2])
        load_a.start()
        load_a.wait()
        block_product(me, w_ref)
        copy(0, sibling, me).wait_recv()
        block_product(sibling, slab(sibling))
        passed = []
        for j, m in enumerate(chips):
            src_pos = _flip(me, m)
            copy(1 + j, src_pos, me).wait_recv()
            cp = copy(4 + j, src_pos, sibling)
            cp.start()
            passed.append(cp)
            block_product(src_pos, slab(src_pos))
        for j, m in enumerate(chips):
            src_pos = _flip(sibling, m)
            copy(4 + j, src_pos, me).wait_recv()
            block_product(src_pos, slab(src_pos))
        for out in pending[-2:]:
            out.wait()
        for cp in first + passed:
            cp.wait_send()
        local.wait()

    return pl.pallas_call(
        body, name=name,
        out_shape=(jax.ShapeDtypeStruct((kdim, N_DEV * sz), w_shard.dtype), jax.ShapeDtypeStruct((t_len, N_DEV * sz), F32),
                   jax.ShapeDtypeStruct((SUBLANES, LANES), F32)),
        in_specs=[HBM_SPEC, HBM_SPEC], out_specs=(HBM_SPEC, HBM_SPEC, pl.BlockSpec(memory_space=pltpu.VMEM)),
        scratch_shapes=[pltpu.VMEM((t_len, kdim), h_all.dtype), pltpu.VMEM((kdim, sz), w_shard.dtype),
                        pltpu.VMEM((2, tm, sz), F32), pltpu.SemaphoreType.DMA((7,)), pltpu.SemaphoreType.DMA((7,)),
                        pltpu.SemaphoreType.DMA((3,)), pltpu.SemaphoreType.DMA((2,))],
        compiler_params=_params(),
    )(h_all, w_shard)


def _all_to_all(arrs, axes, name):
    n = len(arrs)
    sizes = [a.shape[ax] // N_DEV for a, ax in zip(arrs, axes)]

    def body(*refs):
        ins, outs, token = refs[:n], refs[n:2 * n], refs[2 * n]
        send_sems, recv_sems, local_sems = refs[2 * n + 1:]
        token[...] = jnp.zeros_like(token)
        me = _my_pos()
        my = _lin(me)
        copies, local = [], []
        for a in range(n):
            cp = pltpu.make_async_copy(_slab(ins[a], axes[a], my, sizes[a]), outs[a].at[my], local_sems.at[a])
            cp.start()
            local.append(cp)
        for d in range(1, N_DEV):
            peer = _flip(me, d)
            for a in range(n):
                cp = pltpu.make_async_remote_copy(
                    src_ref=_slab(ins[a], axes[a], _lin(peer), sizes[a]), dst_ref=outs[a].at[my],
                    send_sem=send_sems.at[7 * a + d - 1], recv_sem=recv_sems.at[7 * a + d - 1],
                    device_id=peer, device_id_type=MESH)
                cp.start()
                copies.append((a, d, cp))
        for a, d, cp in copies:
            peer = _flip(me, d)
            pltpu.make_async_remote_copy(
                src_ref=_slab(ins[a], axes[a], _lin(peer), sizes[a]), dst_ref=outs[a].at[_lin(peer)],
                send_sem=send_sems.at[7 * a + d - 1], recv_sem=recv_sems.at[7 * a + d - 1],
                device_id=peer, device_id_type=MESH).wait_recv()
            cp.wait_send()
        for cp in local:
            cp.wait()

    out_shape = []
    for a, ax, s in zip(arrs, axes, sizes):
        shp = list(a.shape)
        shp[ax] = s
        out_shape.append(jax.ShapeDtypeStruct((N_DEV,) + tuple(shp), a.dtype))
    out_shape.append(jax.ShapeDtypeStruct((SUBLANES, LANES), F32))
    res = pl.pallas_call(
        body, name=name, out_shape=tuple(out_shape),
        in_specs=[HBM_SPEC] * n, out_specs=tuple([HBM_SPEC] * n + [pl.BlockSpec(memory_space=pltpu.VMEM)]),
        scratch_shapes=[pltpu.SemaphoreType.DMA((7 * n,)), pltpu.SemaphoreType.DMA((7 * n,)),
                        pltpu.SemaphoreType.DMA((n,))],
    )(*arrs)
    return list(res[:n]), res[n]


SEM_SPEC = pl.BlockSpec(memory_space=pltpu.SEMAPHORE)
_EFFECT = pltpu.SideEffectType.DATAFLOW_SIDE_EFFECTING
N_PEERS = N_DEV - 1


def _split_copies(kind, src_ref, land_ref, send_sems, recv_sems, axis, size, arriving):
    me = _my_pos()
    out = []
    if kind == "sibling":
        sibling = _flip(me, 1)
        for k in range(N_CHIPS):
            blk = _slab(land_ref, axis, 2 * k + (sibling[2] if arriving else me[2]), size)
            out.append(pltpu.make_async_remote_copy(src_ref=blk, dst_ref=blk, send_sem=send_sems.at[k],
                                                    recv_sem=recv_sems.at[k], device_id=sibling, device_id_type=MESH))
        return out
    for dd in (range(2, N_DEV, 2) if kind in ("chips", "gather_chips") else range(1, N_DEV)):
        peer = _flip(me, dd)
        k = dd - 1
        if kind in ("gather", "gather_chips"):
            src = _slab(land_ref, axis, _lin(me), size)
            dst = _slab(land_ref, axis, _lin(peer if arriving else me), size)
        elif kind == "scatter":
            src, dst = _slab(src_ref, axis, _lin(peer), size), land_ref.at[k]
        else:
            k = dd // 2 - 1
            src, dst = src_ref.at[2 * peer[0] + peer[1]], land_ref.at[k]
        out.append(pltpu.make_async_remote_copy(
            src_ref=src, dst_ref=dst, send_sem=send_sems.at[k], recv_sem=recv_sems.at[k],
            device_id=peer, device_id_type=MESH))
    return out


N_CHIPS = N_DEV // 2


def _split_size(kind, src_shape, land_shape, axis):
    if kind in ("gather", "gather_chips", "sibling"):
        return land_shape[axis] // N_DEV
    return src_shape[axis] // {"scatter": N_DEV, "chips": N_CHIPS}[kind]


def _xfer_start(kind, src, land, axis, name):
    size = _split_size(kind, src.shape, land.shape, axis)

    def body(src_ref, land_ref, send_sems, recv_sems, src_thru, land_thru, token):
        for cp in _split_copies(kind, src_ref, land_ref, send_sems, recv_sems, axis, size, False):
            cp.start()
        token[...] = jnp.zeros_like(token)

    return pl.pallas_call(
        body, name=name,
        out_shape=(pltpu.SemaphoreType.DMA((N_PEERS,)), pltpu.SemaphoreType.DMA((N_PEERS,)),
                   pltpu.HBM(src.shape, src.dtype), pltpu.HBM(land.shape, land.dtype),
                   jax.ShapeDtypeStruct((SUBLANES, LANES), F32)),
        in_specs=(HBM_SPEC, HBM_SPEC),
        out_specs=(SEM_SPEC, SEM_SPEC, HBM_SPEC, HBM_SPEC, pl.BlockSpec(memory_space=pltpu.VMEM)),
        input_output_aliases={0: 2, 1: 3},
        compiler_params=pltpu.CompilerParams(has_side_effects=_EFFECT),
    )(pltpu.with_memory_space_constraint(src, pltpu.HBM), pltpu.with_memory_space_constraint(land, pltpu.HBM))


def _xfer_wait(kind, handle, axis, after, name):
    send_sems, recv_sems, src_thru, land_thru, _ = handle
    size = _split_size(kind, src_thru.shape, land_thru.shape, axis)

    def body(src_ref, land_ref, send_sems, recv_sems, after_ref, src_dead, got_ref):
        for cp in _split_copies(kind, src_ref, land_ref, send_sems, recv_sems, axis, size, True):
            cp.wait_send()
            cp.wait_recv()

    return pl.pallas_call(
        body, name=name,
        out_shape=(pltpu.HBM(src_thru.shape, src_thru.dtype), pltpu.HBM(land_thru.shape, land_thru.dtype)),
        in_specs=(HBM_SPEC, HBM_SPEC, SEM_SPEC, SEM_SPEC, pl.BlockSpec(memory_space=pl.ANY)),
        out_specs=(HBM_SPEC, HBM_SPEC), input_output_aliases={0: 0, 1: 1},
        compiler_params=pltpu.CompilerParams(has_side_effects=_EFFECT),
    )(src_thru, land_thru, send_sems, recv_sems, after)[1]


def _place_block(shard, axis, my, name):
    r, c = shard.shape
    tr = _pick(r, max(16, (1 << 19) // c), 16)
    nr = r // tr
    if axis == 0:
        full, out_map = (r * N_DEV, c), (lambda i, my_ref: (my_ref[0] * nr + i, 0))
    else:
        full, out_map = (r, c * N_DEV), (lambda i, my_ref: (i, my_ref[0]))

    def body(my_ref, x_ref, o_ref):
        o_ref[...] = x_ref[...].astype(BF16)

    return pl.pallas_call(
        body, name=name, out_shape=jax.ShapeDtypeStruct(full, BF16),
        grid_spec=pltpu.PrefetchScalarGridSpec(
            num_scalar_prefetch=1, grid=(nr,), in_specs=[pl.BlockSpec((tr, c), lambda i, my_ref: (i, 0))],
            out_specs=pl.BlockSpec((tr, c), out_map)),
        compiler_params=_params(("arbitrary",)),
    )(jnp.reshape(my, (1,)).astype(jnp.int32), shard)


def _scatter_start(grad, axis, my, name):
    shp = list(grad.shape)
    shp[axis] //= N_DEV
    own = lax.dynamic_slice_in_dim(grad, my * shp[axis], shp[axis], axis)
    return (_xfer_start("scatter", grad, lax.empty((N_PEERS,) + tuple(shp), grad.dtype), axis, name), own, axis,
            "scatter")


def _pair_reduce(grad, my, name):
    r, cols = grad.shape
    sz = cols // N_DEV

    def exchange(g_ref, got_ref, send_sems, recv_sems):
        me = _my_pos()
        sibling = _flip(me, 1)
        cps = []
        for k in range(N_CHIPS):
            cp = pltpu.make_async_remote_copy(
                src_ref=_slab(g_ref, 1, 2 * k + sibling[2], sz), dst_ref=_slab(got_ref, 1, k, sz),
                send_sem=send_sems.at[k], recv_sem=recv_sems.at[k], device_id=sibling, device_id_type=MESH)
            cp.start()
            cps.append(cp)
        for cp in cps:
            cp.wait()

    got = pl.pallas_call(
        exchange, name=name + "_swap", out_shape=jax.ShapeDtypeStruct((r, N_CHIPS * sz), grad.dtype),
        in_specs=[HBM_SPEC], out_specs=HBM_SPEC,
        scratch_shapes=[pltpu.SemaphoreType.DMA((N_CHIPS,)), pltpu.SemaphoreType.DMA((N_CHIPS,))],
    )(grad)

    tr = _pick(r, 512, 16)

    def add(c_ref, g_ref, p_ref, o_ref):
        o_ref[...] = (g_ref[...].astype(F32) + p_ref[...].astype(F32)).astype(o_ref.dtype)

    def add(c_ref, g_ref, p_ref, o_ref):
        o_ref[0] = (g_ref[...].astype(F32) + p_ref[...].astype(F32)).astype(o_ref.dtype)

    return pl.pallas_call(
        add, name=name + "_sum", out_shape=jax.ShapeDtypeStruct((N_CHIPS, r, sz), grad.dtype),
        grid_spec=pltpu.PrefetchScalarGridSpec(
            num_scalar_prefetch=1, grid=(N_CHIPS, r // tr),
            in_specs=[pl.BlockSpec((tr, sz), lambda k, i, c_ref: (i, 2 * k + c_ref[0])),
                      pl.BlockSpec((tr, sz), lambda k, i, c_ref: (i, k))],
            out_specs=pl.BlockSpec((1, tr, sz), lambda k, i, c_ref: (k, i, 0))),
        compiler_params=_params(("parallel", "parallel")),
    )(jnp.reshape(my % 2, (1,)).astype(jnp.int32), grad, got)


def _chip_scatter_start(pair_sum, my, name):
    _, r, sz = pair_sum.shape
    own = lax.dynamic_index_in_dim(pair_sum, my // 2, 0, keepdims=False)
    return (_xfer_start("chips", pair_sum, lax.empty((N_CHIPS - 1, r, sz), pair_sum.dtype), 0, name), own, 0, "chips")


def _tok(a, *tokens):
    for t in tokens:
        a = a + t[0, 0].astype(a.dtype)
    return a


def _mm(a, b, *, name, ta=False, tb=False, out_dtype=F32, a_fn=None, out_fn=None, extras=(), bias=None,
        tm=1024, tn=1024, tk=2816, a_cols=None):
    a_blk0 = 0
    if ta:
        kdim, m = a.shape
        if a_cols is not None:
            m = a_cols[1]
    else:
        m, kdim = a.shape
    if tb:
        n, kb = b.shape
    else:
        kb, n = b.shape
    assert kb == kdim, (a.shape, b.shape, ta, tb)
    tm = _pick(m, tm, LANES if ta else SUBLANES)
    if a_cols is not None:
        assert ta and a_cols[0] % tm == 0
        a_blk0 = a_cols[0] // tm
    tn = _pick(n, tn, LANES)
    tk = _pick(kdim, tk, LANES)
    nk = kdim // tk
    ne = len(extras)
    has_bias = bias is not None

    def body(*refs):
        a_ref, b_ref = refs[0], refs[1]
        pos = 2
        bias_ref = None
        if has_bias:
            bias_ref = refs[pos]
            pos += 1
        extra_refs = refs[pos:pos + ne]
        o_ref = refs[pos + ne]
        av = a_ref[...]
        if a_fn is not None:
            av = a_fn(av)
        dn = (((0 if ta else 1,), (1 if tb else 0,)), ((), ()))
        part = lax.dot_general(av.astype(BF16), b_ref[...].astype(BF16), dn, preferred_element_type=F32)

        def finish(r):
            if has_bias:
                r = r + bias_ref[...]
            if out_fn is not None:
                r = out_fn(r, *[e[...] for e in extra_refs])
            o_ref[...] = r.astype(out_dtype)

        if nk == 1:
            finish(part)
            return
        acc = refs[pos + ne + 1]
        k = pl.program_id(2)

        @pl.when(k == 0)
        def _():
            acc[...] = part

        @pl.when(k > 0)
        def _():
            acc[...] += part

        @pl.when(k == nk - 1)
        def _():
            finish(acc[...])

    in_specs = [
        pl.BlockSpec((tk, tm), lambda i, j, k: (k, i + a_blk0)) if ta else pl.BlockSpec((tm, tk), lambda i, j, k: (i, k)),
        pl.BlockSpec((tn, tk), lambda i, j, k: (j, k)) if tb else pl.BlockSpec((tk, tn), lambda i, j, k: (k, j)),
    ]
    args = [a, b]
    if has_bias:
        in_specs.append(pl.BlockSpec((1, tn), lambda i, j, k: (0, j)))
        args.append(bias)
    for e in extras:
        in_specs.append(pl.BlockSpec((tm, tn), lambda i, j, k: (i, j)))
        args.append(e)
    return pl.pallas_call(
        body, name=name, grid=(m // tm, n // tn, nk),
        in_specs=in_specs, out_specs=pl.BlockSpec((tm, tn), lambda i, j, k: (i, j)),
        out_shape=jax.ShapeDtypeStruct((m, n), out_dtype),
        scratch_shapes=[pltpu.VMEM((tm, tn), F32)] if nk > 1 else [],
        compiler_params=_params(("parallel", "parallel", "arbitrary")),
    )(*args)


def _row(arr, cb=0, rb=0, w=None):
    return ("row", arr, cb, rb, w)


def _bc(arr, cb=0, w=None):
    return ("bc", arr, cb, 0, w)


def _full(arr):
    return ("full", arr, 0, 0, None)


def _rowwise(fn, ins, outs, *, nr, tr, nc, cw, name):
    n_in = len(ins)
    outs = [(o[0], o[1], (o[2] if len(o) > 2 and o[2] else cw)) for o in outs]

    def body(*refs):
        in_refs, out_refs = refs[:n_in], refs[n_in:]
        j, i = pl.program_id(0), pl.program_id(1)
        res = fn(i, j, *[r[...] for r in in_refs])
        if not isinstance(res, (tuple, list)):
            res = (res,)
        for (kind, _, _), o_ref, r in zip(outs, out_refs, res):
            if kind == "row":
                o_ref[...] = r.astype(o_ref.dtype)
            else:
                @pl.when(i == 0)
                def _(o_ref=o_ref):
                    o_ref[...] = jnp.zeros_like(o_ref)

                o_ref[...] += r

    in_specs, args = [], []
    for kind, arr, cb, rb, w in ins:
        w = w or cw
        if kind == "row":
            last = arr.shape[0] // tr - 1
            in_specs.append(pl.BlockSpec((tr, w), lambda j, i, cb=cb, rb=rb, last=last: (jnp.clip(i + rb, 0, last), j + cb)))
        elif kind == "bc":
            in_specs.append(pl.BlockSpec((arr.shape[0], w), lambda j, i, cb=cb: (0, j + cb)))
        else:
            in_specs.append(pl.BlockSpec(arr.shape, lambda j, i, nd=arr.ndim: (0,) * nd))
        args.append(arr)
    out_specs, out_shape = [], []
    for kind, spec, w in outs:
        if kind == "row":
            out_specs.append(pl.BlockSpec((tr, w), lambda j, i: (i, j)))
            out_shape.append(jax.ShapeDtypeStruct((nr * tr, nc * w), spec))
        else:
            out_specs.append(pl.BlockSpec((spec, w), lambda j, i: (0, j)))
            out_shape.append(jax.ShapeDtypeStruct((spec, nc * w), F32))
    res = pl.pallas_call(
        body, name=name, grid=(nc, nr), in_specs=in_specs, out_specs=tuple(out_specs), out_shape=tuple(out_shape),
        compiler_params=_params(("parallel", "arbitrary")),
    )(*args)
    return res


def _sum_parts(own, parts, *, name):
    p, r, c = parts.shape
    tr = _pick(r, max(SUBLANES, (1 << 18) // c), SUBLANES)

    def body(o_ref, p_ref, g_out):
        g = o_ref[...].astype(F32)
        for q in range(p):
            g = g + p_ref[q].astype(F32)
        g_out[...] = g

    blk = pl.BlockSpec((tr, c), lambda i: (i, 0))
    return pl.pallas_call(
        body, name=name, grid=(r // tr,), in_specs=[blk, pl.BlockSpec((p, tr, c), lambda i: (0, i, 0))],
        out_specs=blk, out_shape=jax.ShapeDtypeStruct((r, c), F32), compiler_params=_params(("parallel",)),
    )(own, parts)


def _adamw(parts, w, m, v, *, name, own=None):
    p, r, c = parts.shape
    tr = _pick(r, max(SUBLANES, (1 << 18) // c), 16 if parts.dtype == BF16 else SUBLANES)
    bc1 = 1.0 - ADAM_B1 ** ADAM_STEP
    bc2 = 1.0 - ADAM_B2 ** ADAM_STEP
    has_own = own is not None

    def body(*refs):
        p_ref, w_ref, m_ref, v_ref = refs[:4]
        g_out, d_out, m_out, v_out = refs[4 + has_own:]
        if has_own:
            g = refs[4][...].astype(F32) + p_ref[0].astype(F32)
        else:
            g = p_ref[0].astype(F32)
        for q in range(1, p):
            g = g + p_ref[q].astype(F32)
        m_new = ADAM_B1 * m_ref[...] + (1.0 - ADAM_B1) * g
        v_new = ADAM_B2 * v_ref[...] + (1.0 - ADAM_B2) * (g * g)
        m_hat = m_new / bc1
        v_hat = v_new / bc2
        g_out[...] = g
        d_out[...] = -ADAM_LR * (m_hat / (jnp.sqrt(v_hat) + ADAM_EPS) + ADAM_WD * w_ref[...])
        m_out[...] = m_new
        v_out[...] = v_new

    blk = pl.BlockSpec((tr, c), lambda i: (i, 0))
    return pl.pallas_call(
        body, name=name, grid=(r // tr,),
        in_specs=[pl.BlockSpec((p, tr, c), lambda i: (0, i, 0)), blk, blk, blk] + ([blk] if has_own else []),
        out_specs=(blk, blk, blk, blk), out_shape=tuple(jax.ShapeDtypeStruct((r, c), F32) for _ in range(4)),
        compiler_params=_params(("parallel",)),
    )(*([parts, w, m, v] + ([own] if has_own else [])))


_GELU_C = math.sqrt(2.0 / math.pi)
_GELU_K = 0.044715


def _sigmoid(z):
    return 0.5 * jnp.tanh(0.5 * z) + 0.5


def _silu(z):
    return z * _sigmoid(z)


def _dsilu(z):
    s = _sigmoid(z)
    return s * (1.0 + z * (1.0 - s))


def _gelu(z):
    return 0.5 * z * (1.0 + jnp.tanh(_GELU_C * (z + _GELU_K * z * z * z)))


def _gelu_pair(z):
    zz = z * z
    t = jnp.tanh(_GELU_C * (z * (1.0 + _GELU_K * zz)))
    half = 0.5 * (1.0 + t)
    return z * half, half + 0.5 * z * (1.0 - t * t) * (_GELU_C * (1.0 + 3.0 * _GELU_K * zz))


def _softplus(z):
    return jnp.maximum(z, 0.0) + jnp.log(1.0 + jnp.exp(-jnp.abs(z)))


def _rms(xv):
    return lax.rsqrt(jnp.mean(xv * xv, axis=-1, keepdims=True) + NORM_EPS)


def _rms_bwd(xhat, r, dxhat):
    return r * (dxhat - xhat * jnp.mean(dxhat * xhat, axis=-1, keepdims=True))


def _swap_pairs(y):
    lane = lax.broadcasted_iota(jnp.int32, y.shape, 1)
    return jnp.where(lane % 2 == 0, pltpu.roll(y, LANES - 1, 1), pltpu.roll(y, 1, 1))


def _dot(a, b, ca, cb):
    return lax.dot_general(a, b, (((ca,), (cb,)), ((), ())), preferred_element_type=F32)


_SM_SCALE = HEAD_DIM ** -0.5
_EXP2_SCALE = _SM_SCALE * math.log2(math.e)


def _softmax_parts(qh, kk):
    s = _dot(qh, kk, 1, 1)
    p = jnp.exp2((s - jnp.max(s, axis=-1, keepdims=True)) * _EXP2_SCALE)
    return p, 1.0 / jnp.sum(p, axis=-1, keepdims=True)


def _attn_fwd(q, k, v, *, tq, name):
    s_len, aw = q.shape
    t_len, kvw = k.shape
    gw = Q_GROUP * HEAD_DIM

    def body(q_ref, k_ref, v_ref, o_ref):
        kk, vv = k_ref[...], v_ref[...]
        for g in range(Q_GROUP):
            cols = slice(g * HEAD_DIM, (g + 1) * HEAD_DIM)
            p, inv_l = _softmax_parts(q_ref[:, cols], kk)
            o_ref[:, cols] = (_dot(p.astype(BF16), vv, 1, 0) * inv_l).astype(BF16)

    return pl.pallas_call(
        body, name=name, grid=(kvw // HEAD_DIM, s_len // tq),
        in_specs=[pl.BlockSpec((tq, gw), lambda h, i: (i, h)), pl.BlockSpec((t_len, HEAD_DIM), lambda h, i: (0, h)),
                  pl.BlockSpec((t_len, HEAD_DIM), lambda h, i: (0, h))],
        out_specs=pl.BlockSpec((tq, gw), lambda h, i: (i, h)),
        out_shape=jax.ShapeDtypeStruct((s_len, aw), BF16),
        compiler_params=_params(("parallel", "parallel")),
    )(q, k, v)


def _attn_bwd(q, k, v, do, *, tq, name):
    s_len, aw = q.shape
    t_len, kvw = k.shape
    gw = Q_GROUP * HEAD_DIM

    def body(q_ref, k_ref, v_ref, do_ref, dq_ref, dk_ref, dv_ref):
        @pl.when(pl.program_id(1) == 0)
        def _():
            dk_ref[...] = jnp.zeros_like(dk_ref)
            dv_ref[...] = jnp.zeros_like(dv_ref)

        kk, vv = k_ref[...], v_ref[...]
        for g in range(Q_GROUP):
            cols = slice(g * HEAD_DIM, (g + 1) * HEAD_DIM)
            qh, doh = q_ref[:, cols], do_ref[:, cols]
            p, inv_l = _softmax_parts(qh, kk)
            dp = _dot(doh, vv, 1, 1)
            delta = jnp.sum(p * dp, axis=-1, keepdims=True) * inv_l
            dsb = (p * ((dp - delta) * (inv_l * _SM_SCALE))).astype(BF16)
            dq_ref[:, cols] = _dot(dsb, kk, 1, 0)
            dk_ref[...] += _dot(dsb, qh, 0, 0)
            dv_ref[...] += _dot(p.astype(BF16), (doh * inv_l).astype(BF16), 0, 0)

    qspec = pl.BlockSpec((tq, gw), lambda h, i: (i, h))
    kspec = pl.BlockSpec((t_len, HEAD_DIM), lambda h, i: (0, h))
    return pl.pallas_call(
        body, name=name, grid=(kvw // HEAD_DIM, s_len // tq),
        in_specs=[qspec, kspec, kspec, qspec], out_specs=(qspec, kspec, kspec),
        out_shape=(jax.ShapeDtypeStruct((s_len, aw), F32), jax.ShapeDtypeStruct((t_len, kvw), F32),
                   jax.ShapeDtypeStruct((t_len, kvw), F32)),
        compiler_params=_params(("parallel", "arbitrary")),
    )(q, k, v, do)


SHIFTS = (-2, -1, 1, 2)


def _shift_masks(l_len, t_len):
    t = jnp.arange(t_len)[:, None]
    cols = []
    for d in SHIFTS:
        td = t + d
        ok = ((t < l_len) & (td >= 0) & (td < l_len)) | ((t >= l_len) & (td >= l_len) & (td < t_len))
        cols.append(jnp.broadcast_to(ok.astype(F32), (t_len, RNN_BLOCK)))
    return jnp.concatenate(cols, axis=1)


def _shift_rows(xv, d, m_ref, l_len, t_len):
    if d == 0:
        return xv
    k = SHIFTS.index(d)
    return pltpu.roll(xv, (-d) % t_len, 0) * m_ref[:, k * RNN_BLOCK:(k + 1) * RNN_BLOCK]


def _sublane_scan(a8, b8, rev):
    row = lax.broadcasted_iota(jnp.int32, a8.shape, 0)
    for s in (1, 2, 4):
        if rev:
            ok = row < SUBLANES - s
            sh = SUBLANES - s
        else:
            ok = row >= s
            sh = s
        a_sh = jnp.where(ok, pltpu.roll(a8, sh, 0), 1.0)
        b_sh = jnp.where(ok, pltpu.roll(b8, sh, 0), 0.0)
        b8 = a8 * b_sh + b8
        a8 = a8 * a_sh
    return a8, b8


def _scan_chunks(chains, c0, n, carries):
    def step(kk, hs):
        new = []
        for (a_ref, b_ref, h_ref, rev), hprev in zip(chains, hs):
            c = c0 + (n - 1 - kk if rev else kk)
            r0 = pl.multiple_of(c * SUBLANES, SUBLANES)
            big_a, big_b = _sublane_scan(a_ref[pl.ds(r0, SUBLANES), :], b_ref[pl.ds(r0, SUBLANES), :], rev)
            h8 = big_b + big_a * hprev
            h_ref[pl.ds(r0, SUBLANES), :] = h8
            edge = 0 if rev else SUBLANES - 1
            new.append(jnp.broadcast_to(h8[edge:edge + 1, :], h8.shape))
        return tuple(new)

    return lax.fori_loop(0, n, step, tuple(carries))


def _rglru_conv(x, cw_ref, cb_ref, m_ref, l_len, t_len):
    y = cb_ref[...] + _shift_rows(x, -1, m_ref, l_len, t_len) * cw_ref[0:1, :]
    for kk in range(1, CONV_WIDTH):
        y = y + _shift_rows(x, kk - 1, m_ref, l_len, t_len) * cw_ref[kk:kk + 1, :]
    return y


def _rglru_gates(xc, wr, wi, br, bi, lam):
    xb = xc.astype(BF16)
    r = _sigmoid(_dot(xb, wr.astype(BF16), 1, 0) + br)
    ig = _sigmoid(_dot(xb, wi.astype(BF16), 1, 0) + bi)
    sp = _softplus(-lam)
    log_a = (-LRU_C) * r * sp
    a = jnp.exp(log_a)
    z2 = 2.0 * log_a
    series = z2 * (1.0 + z2 * (0.5 + z2 * (1.0 / 6.0 + z2 * (1.0 / 24.0))))
    one_m = jnp.where(jnp.abs(z2) < 0.03, -series, 1.0 - a * a)
    rinv = lax.rsqrt(jnp.maximum(one_m, 1e-30))
    u = ig * xc
    return r, ig, sp, a, one_m * rinv, u, rinv


def _rglru_specs(proj, xr_blk, xg_blk, t_len):
    col = lambda off: pl.BlockSpec((t_len, RNN_BLOCK), lambda h, off=off: (0, off + h))
    vec = lambda rows: pl.BlockSpec((rows, RNN_BLOCK), lambda h: (0, h))
    wsp = pl.BlockSpec((2, 1, RNN_BLOCK, RNN_BLOCK), lambda h: (0, h, 0, 0))
    msp = pl.BlockSpec((t_len, len(SHIFTS) * RNN_BLOCK), lambda h: (0, 0))
    return col(xr_blk), col(xg_blk), vec, wsp, msp


def _rglru_fwd(proj, xr_blk, xg_blk, masks, conv_w, conv_b, w_rg, w_ig, b_rg, b_ig, lam, *, l_len, name):
    t_len = proj.shape[0]
    s_len = t_len - l_len
    d_rnn = conv_b.shape[1]
    n_l, n_t = l_len // SUBLANES, t_len // SUBLANES

    def body(xr_ref, xg_ref, m_ref, cw_ref, cb_ref, wr_ref, wi_ref, br_ref, bi_ref, lam_ref, rg_ref, hf_ref, hb_ref,
             a_s, b_s):
        xc = _rglru_conv(xr_ref[...], cw_ref, cb_ref, m_ref, l_len, t_len)
        for d in range(2):
            _, _, _, a, mult, u, _ = _rglru_gates(xc, wr_ref[d, 0], wi_ref[d, 0], br_ref[d:d + 1, :],
                                               bi_ref[d:d + 1, :], lam_ref[d:d + 1, :])
            a_s[d] = a
            b_s[d] = mult * u
        chains = [(a_s.at[0], b_s.at[0], hf_ref, False), (a_s.at[1], b_s.at[1], hb_ref, True)]
        zero = jnp.zeros((SUBLANES, RNN_BLOCK), F32)
        carry = _scan_chunks(chains, 0, n_l, (zero, zero))
        _scan_chunks(chains, n_l, n_t - n_l, carry)
        rnn = hf_ref[pl.ds(l_len, s_len), :] + hb_ref[pl.ds(l_len, s_len), :]
        rg_ref[...] = (rnn * _gelu(xg_ref[pl.ds(l_len, s_len), :])).astype(BF16)

    xr_spec, xg_spec, vec, wsp, msp = _rglru_specs(proj, xr_blk, xg_blk, t_len)
    col_t = pl.BlockSpec((t_len, RNN_BLOCK), lambda h: (0, h))
    return pl.pallas_call(
        body, name=name, grid=(d_rnn // RNN_BLOCK,),
        in_specs=[xr_spec, xg_spec, msp, vec(CONV_WIDTH), vec(1), wsp, wsp, vec(2), vec(2), vec(2)],
        out_specs=(pl.BlockSpec((s_len, RNN_BLOCK), lambda h: (0, h)), col_t, col_t),
        out_shape=(jax.ShapeDtypeStruct((s_len, d_rnn), BF16), jax.ShapeDtypeStruct((t_len, d_rnn), F32),
                   jax.ShapeDtypeStruct((t_len, d_rnn), F32)),
        scratch_shapes=[pltpu.VMEM((2, t_len, RNN_BLOCK), F32)] * 2,
        compiler_params=_params(("parallel",)),
    )(proj, proj, masks, conv_w, conv_b, w_rg, w_ig, b_rg, b_ig, lam)


N_RNN_VEC = 16


def _rglru_bwd(proj, xr_blk, xg_blk, masks, h_fwd, h_bwd, drg, conv_w, conv_b, w_rg, w_ig, b_rg, b_ig, lam, *, l_len, name):
    t_len = proj.shape[0]
    s_len = t_len - l_len
    d_rnn = conv_b.shape[1]
    n_l, n_t = l_len // SUBLANES, t_len // SUBLANES

    def body(xr_ref, xg_ref, m_ref, hf_ref, hb_ref, drg_ref, cw_ref, cb_ref, wr_ref, wi_ref, br_ref, bi_ref, lam_ref,
             dxr_ref, dxg_ref, dvec_ref, dwr_ref, dwi_ref, a_s, b_s, h_s):
        t = lax.broadcasted_iota(jnp.int32, (t_len, RNN_BLOCK), 0)
        x = xr_ref[...]
        xc = _rglru_conv(x, cw_ref, cb_ref, m_ref, l_len, t_len)
        xb = xc.astype(BF16)
        gates = []
        for d in range(2):
            gt = _rglru_gates(xc, wr_ref[d, 0], wi_ref[d, 0], br_ref[d:d + 1, :], bi_ref[d:d + 1, :],
                              lam_ref[d:d + 1, :])
            gates.append(gt)
        zero = jnp.zeros((SUBLANES, RNN_BLOCK), F32)
        h0, h1 = hf_ref[...], hb_ref[...]
        xg = xg_ref[...]
        dr = jnp.concatenate([jnp.zeros((l_len, RNN_BLOCK), F32), drg_ref[...].astype(F32)], axis=0)
        gel, dgel = _gelu_pair(xg)
        dxg_ref[...] = (dr * (h0 + h1) * dgel).astype(dxg_ref.dtype)
        drnn = dr * gel
        a_s[0] = pltpu.roll(gates[0][3], t_len - 1, 0)
        a_s[1] = pltpu.roll(gates[1][3], 1, 0)
        b_s[0] = drnn
        b_s[1] = drnn
        hprev = [jnp.where(t == 0, 0.0, pltpu.roll(h0, 1, 0)),
                 jnp.where(t == l_len - 1, 0.0, pltpu.roll(h1, t_len - 1, 0))]
        chains = [(a_s.at[0], b_s.at[0], h_s.at[0], True), (a_s.at[1], b_s.at[1], h_s.at[1], False)]
        carry = _scan_chunks(chains, n_l, n_t - n_l, (zero, zero))
        _scan_chunks(chains, 0, n_l, carry)

        dxc = jnp.zeros((t_len, RNN_BLOCK), F32)
        for d in range(2):
            r, ig, sp, a, mult, u, rinv = gates[d]
            lamb = h_s[d]
            da = lamb * hprev[d]
            dmult = lamb * u
            du = lamb * mult
            dlog_a = da * a - dmult * (a * a) * rinv
            dzr = (dlog_a * ((-LRU_C) * sp)) * r * (1.0 - r)
            dsp = jnp.sum(dlog_a * ((-LRU_C) * r), axis=0, keepdims=True)
            dzi = (du * xc) * ig * (1.0 - ig)
            dzrb, dzib = dzr.astype(BF16), dzi.astype(BF16)
            dxc = dxc + du * ig + _dot(dzrb, wr_ref[d, 0].astype(BF16), 1, 1) + _dot(dzib, wi_ref[d, 0].astype(BF16), 1, 1)
            dwr_ref[d, 0] = _dot(xb, dzrb, 0, 0)
            dwi_ref[d, 0] = _dot(xb, dzib, 0, 0)
            dvec_ref[5 + d:6 + d, :] = jnp.sum(dzr, axis=0, keepdims=True)
            dvec_ref[7 + d:8 + d, :] = jnp.sum(dzi, axis=0, keepdims=True)
            dvec_ref[9 + d:10 + d, :] = dsp * (-_sigmoid(-lam_ref[d:d + 1, :]))
        dvec_ref[4:5, :] = jnp.sum(dxc, axis=0, keepdims=True)
        dvec_ref[11:N_RNN_VEC, :] = jnp.zeros((N_RNN_VEC - 11, RNN_BLOCK), F32)
        dxr = jnp.zeros((t_len, RNN_BLOCK), F32)
        for kk in range(CONV_WIDTH):
            dvec_ref[kk:kk + 1, :] = jnp.sum(dxc * _shift_rows(x, kk - 1, m_ref, l_len, t_len), axis=0, keepdims=True)
            dxr = dxr + _shift_rows(dxc, 1 - kk, m_ref, l_len, t_len) * cw_ref[kk:kk + 1, :]
        dxr_ref[...] = dxr.astype(dxr_ref.dtype)

    xr_spec, xg_spec, vec, wsp, msp = _rglru_specs(proj, xr_blk, xg_blk, t_len)
    col_t = pl.BlockSpec((t_len, RNN_BLOCK), lambda h: (0, h))
    return pl.pallas_call(
        body, name=name, grid=(d_rnn // RNN_BLOCK,),
        in_specs=[xr_spec, xg_spec, msp, col_t, col_t, pl.BlockSpec((s_len, RNN_BLOCK), lambda h: (0, h)), vec(CONV_WIDTH),
                  vec(1),
                  wsp, wsp, vec(2), vec(2), vec(2)],
        out_specs=(col_t, col_t, vec(N_RNN_VEC), wsp, wsp),
        out_shape=(jax.ShapeDtypeStruct((t_len, d_rnn), BF16), jax.ShapeDtypeStruct((t_len, d_rnn), BF16),
                   jax.ShapeDtypeStruct((N_RNN_VEC, d_rnn), F32), jax.ShapeDtypeStruct(w_rg.shape, F32),
                   jax.ShapeDtypeStruct(w_ig.shape, F32)),
        scratch_shapes=[pltpu.VMEM((2, t_len, RNN_BLOCK), F32)] * 3,
        compiler_params=_params(("parallel",)),
    )(proj, proj, masks, h_fwd, h_bwd, drg, conv_w, conv_b, w_rg, w_ig, b_rg, b_ig, lam)


def _colsum(v):
    return jnp.sum(v, axis=0, keepdims=True)


def _qk_fwd_heads(xv, gain, cos_e, sin_e):
    outs = []
    for h in range(xv.shape[1] // HEAD_DIM):
        xh = xv[:, h * HEAD_DIM:(h + 1) * HEAD_DIM]
        y = xh * _rms(xh) * gain
        outs.append(y * cos_e + _swap_pairs(y) * sin_e)
    return outs[0] if len(outs) == 1 else jnp.concatenate(outs, axis=1)


def _qk_bwd_heads(xv, dov, gain, cos_e, sin_e):
    outs = []
    dgain = jnp.zeros((1, HEAD_DIM), F32)
    for h in range(xv.shape[1] // HEAD_DIM):
        cols = slice(h * HEAD_DIM, (h + 1) * HEAD_DIM)
        xh, doh = xv[:, cols], dov[:, cols]
        r = _rms(xh)
        xhat = xh * r
        dy = doh * cos_e + _swap_pairs(doh * sin_e)
        dgain = dgain + _colsum(dy * xhat)
        outs.append(_rms_bwd(xhat, r, dy * gain))
    return (outs[0] if len(outs) == 1 else jnp.concatenate(outs, axis=1)), dgain


WEIGHTS = ['c_ctx', 'w_mod', 'b_mod', 'g_mix', 'g_mlp', 'w_in', 'q_gain', 'k_gain', 'conv_w', 'conv_b', 'w_rg', 'b_rg',
           'w_ig', 'b_ig', 'lru_lambda', 'w_o_attn', 'w_o_rnn', 'w_out', 'w_up', 'w_down', 'g_final']
REPLICATED = ['c_ctx', 'b_mod', 'g_mix', 'g_mlp', 'q_gain', 'k_gain', 'conv_b', 'w_rg', 'w_ig', 'g_final']
SMALL_SHARDED = [('conv_w', CONV_WIDTH), ('b_rg', 2), ('b_ig', 2), ('lru_lambda', 2)]
SMALL_ROWS = 16


PACK_ROWS = 64


def _pack_flat(arrs, rows_mult):
    flat = jnp.concatenate([a.reshape(-1) for a in arrs])
    n = flat.shape[0]
    unit = rows_mult * LANES
    pad = (-n) % unit
    if pad:
        flat = jnp.concatenate([flat, jnp.zeros((pad,), flat.dtype)])
    return flat.reshape(-1, LANES)


def _unpack_flat(packed, shapes):
    flat = packed.reshape(-1)
    out, pos = [], 0
    for shp in shapes:
        n = math.prod(shp)
        out.append(flat[pos:pos + n].reshape(shp))
        pos += n
    return out


def _pack_small(d):
    rows = [d[n][0] for n, _ in SMALL_SHARDED]
    cols = rows[0].shape[1]
    used = sum(k for _, k in SMALL_SHARDED)
    return jnp.concatenate(rows + [jnp.zeros((SMALL_ROWS - used, cols), F32)], axis=0)


def _unpack_small(p):
    out, pos = {}, 0
    for n, k in SMALL_SHARDED:
        out[n] = p[pos:pos + k][None]
        pos += k
    return out


def kernel(x, c, ctx, c_ctx, w_mod, b_mod, g_mix, g_mlp, w_in, q_gain, k_gain, conv_w, conv_b, w_rg, b_rg, w_ig, b_ig, lru_lambda, w_o_attn, w_o_rnn, w_out, w_up, w_down, g_final, loss_target, m_c_ctx, m_w_mod, m_b_mod, m_g_mix, m_g_mlp, m_w_in, m_q_gain, m_k_gain, m_conv_w, m_conv_b, m_w_rg, m_b_rg, m_w_ig, m_b_ig, m_lru_lambda, m_w_o_attn, m_w_o_rnn, m_w_out, m_w_up, m_w_down, m_g_final, v_c_ctx, v_w_mod, v_b_mod, v_g_mix, v_g_mlp, v_w_in, v_q_gain, v_k_gain, v_conv_w, v_conv_b, v_w_rg, v_b_rg, v_w_ig, v_b_ig, v_lru_lambda, v_w_o_attn, v_w_o_rnn, v_w_out, v_w_up, v_w_down, v_g_final):
    wd = dict(c_ctx=c_ctx, w_mod=w_mod, b_mod=b_mod, g_mix=g_mix, g_mlp=g_mlp, w_in=w_in, q_gain=q_gain, k_gain=k_gain,
              conv_w=conv_w, conv_b=conv_b, w_rg=w_rg, b_rg=b_rg, w_ig=w_ig, b_ig=b_ig, lru_lambda=lru_lambda,
              w_o_attn=w_o_attn, w_o_rnn=w_o_rnn, w_out=w_out, w_up=w_up, w_down=w_down, g_final=g_final)
    md = dict(zip(WEIGHTS, (m_c_ctx, m_w_mod, m_b_mod, m_g_mix, m_g_mlp, m_w_in, m_q_gain, m_k_gain, m_conv_w, m_conv_b,
                            m_w_rg, m_b_rg, m_w_ig, m_b_ig, m_lru_lambda, m_w_o_attn, m_w_o_rnn, m_w_out, m_w_up,
                            m_w_down, m_g_final)))
    vd = dict(zip(WEIGHTS, (v_c_ctx, v_w_mod, v_b_mod, v_g_mix, v_g_mlp, v_w_in, v_q_gain, v_k_gain, v_conv_w, v_conv_b,
                            v_w_rg, v_b_rg, v_w_ig, v_b_ig, v_lru_lambda, v_w_o_attn, v_w_o_rnn, v_w_out, v_w_up,
                            v_w_down, v_g_final)))

    xs, tgt, ctxs = x[0], loss_target[0], ctx[0]
    s_len, d = xs.shape
    l_len = ctxs.shape[0]
    t_len = s_len + l_len
    n_in = w_in.shape[2] * N_DEV
    aw = (n_in - 4 * d) * 2 // 3
    kvw = aw // Q_GROUP
    assert aw == d and n_in == aw + 2 * kvw + 4 * d
    nmod = N_MOD * d // N_DEV
    tr = 128
    assert l_len % tr == 0 and s_len % tr == 0
    lb = l_len // tr
    tq = min(256, s_len)
    my = _lin(_my_pos())
    c_ctx2 = c_ctx[None]
    g_final2 = g_final[None]

    c_all = _all_gather([jnp.broadcast_to(c, (SUBLANES, d))], [0], "ag_cond")[0][::SUBLANES]
    c16 = jnp.concatenate([c_all, jnp.broadcast_to(c_ctx2, (N_DEV, d))], axis=0)
    b_mod_sh = lax.dynamic_slice(b_mod, (0, my * nmod), (1, nmod))
    mod_part = _mm(c16, w_mod[0], a_fn=_silu, bias=b_mod_sh, name="mod_fwd")
    sp_w = _pack_small(wd)
    (mod_g, sp_f), order = _all_gather([mod_part, sp_w], [1, 1], "ag_mod", with_token=True)
    conv_w_f, b_rg_f, b_ig_f, lam_f = sp_f[0:4], sp_f[4:6], sp_f[6:8], sp_f[8:10]
    mod_lat = lax.dynamic_slice(mod_g, (my, 0), (1, N_MOD * d))
    sh_a, sc_a, ga_a, sh_f, sc_f, ga_f = [mod_lat[:, k * d:(k + 1) * d] for k in range(N_MOD)]
    csh_a, csc_a = mod_g[N_DEV:N_DEV + 1, :d], mod_g[N_DEV:N_DEV + 1, d:2 * d]

    def nm1_fwd(i, j, cv, lv, g, csh, csc, sh, sc):
        is_ctx = i < lb
        xv = jnp.where(is_ctx, cv, lv)
        n = xv * _rms(xv) * g
        return n * (1.0 + jnp.where(is_ctx, csc, sc)) + jnp.where(is_ctx, csh, sh)

    h_all = _rowwise(nm1_fwd, [_row(ctxs), _row(xs, 0, -lb), _bc(g_mix), _bc(csh_a), _bc(csc_a), _bc(sh_a), _bc(sc_a)],
                     [("row", BF16)], nr=t_len // tr, tr=tr, nc=1, cw=d, name="norm1_fwd")[0]
    w_in_f, proj, order = _gather_proj(h_all, _tok(w_in[0], order).astype(BF16), "ag_proj")
    late_axis = dict(w_o_attn=0, w_o_rnn=0, w_out=0, w_up=1, w_down=0)
    placed = [_place_block(wd[n][0], ax, my, "place_" + n) for n, ax in late_axis.items()]
    late_kind = lambda n: "gather_chips" if n in ("w_up", "w_down") else "gather"
    late = {}
    for (n, ax), full in zip(late_axis.items(), placed):
        late[n] = _xfer_start(late_kind(n), order, full, ax, "ag_start_" + n)
        order = late[n][4]
    q_gain_fwd = _tok(q_gain, order)

    late_wait = lambda n, after: _xfer_wait(late_kind(n), late[n], late_axis[n], after, "ag_wait_" + n)

    def sibling_start(n, after):
        arrived = late_wait(n, after)
        return _xfer_start("sibling", jnp.zeros((SUBLANES, LANES), F32), arrived, late_axis[n], "ag_sib_start_" + n)

    sibling_wait = lambda n, handle, after: _xfer_wait("sibling", handle, late_axis[n], after, "ag_sib_wait_" + n)

    pos = jnp.arange(s_len)
    n_freq = HEAD_DIM // 4
    inv_freq = ROPE_THETA ** (-jnp.arange(n_freq, dtype=F32) / n_freq)
    ang = jnp.concatenate([(pos // GRID_W).astype(F32)[:, None] * inv_freq,
                           (pos % GRID_W).astype(F32)[:, None] * inv_freq], axis=-1)
    sign = jnp.where(jnp.arange(HEAD_DIM) % 2 == 0, -1.0, 1.0).astype(F32)
    cos_e = jnp.concatenate([jnp.ones((l_len, HEAD_DIM), F32), jnp.repeat(jnp.cos(ang), 2, axis=-1)], axis=0)
    sin_e = jnp.concatenate([jnp.zeros((l_len, HEAD_DIM), F32), jnp.repeat(jnp.sin(ang), 2, axis=-1) * sign], axis=0)


    kb = aw // kvw
    q = _rowwise(lambda i, j, xv, gain, ce, se: _qk_fwd_heads(xv, gain, ce, se),
                 [_row(proj, 0, lb, aw), _full(q_gain_fwd), _row(cos_e, 0, lb, HEAD_DIM), _row(sin_e, 0, lb, HEAD_DIM)],
                 [("row", BF16, aw)], nr=s_len // tr, tr=tr, nc=1, cw=aw, name="q_fwd")[0]
    k_all, v_all = _rowwise(lambda i, j, kv_, vv_, gain, ce, se: (_qk_fwd_heads(kv_, gain, ce, se), vv_),
                            [_row(proj, kb, 0, kvw), _row(proj, kb + 1, 0, kvw), _full(k_gain),
                             _row(cos_e, 0, 0, HEAD_DIM), _row(sin_e, 0, 0, HEAD_DIM)],
                            [("row", BF16, kvw), ("row", BF16, kvw)], nr=t_len // tr, tr=tr, nc=1, cw=kvw, name="kv_fwd")
    attn_o = _attn_fwd(q, k_all, v_all, tq=tq, name="attn_fwd")

    xr_blk = (aw + 2 * kvw) // RNN_BLOCK
    xg_blk = xr_blk + d // RNN_BLOCK
    shift_masks = _shift_masks(l_len, t_len)
    rg, h_fwd, h_bwd = _rglru_fwd(proj, xr_blk, xg_blk, shift_masks, conv_w_f, conv_b, w_rg[0], w_ig[0], b_rg_f, b_ig_f, lam_f, l_len=l_len,
                    name="rglru_fwd")

    w_oa_f = late_wait("w_o_attn", attn_o)
    y_attn = _mm(attn_o, w_oa_f, name="o_attn_fwd")
    w_or_f = late_wait("w_o_rnn", rg)
    y_rnn = _mm(rg, w_or_f, name="o_rnn_fwd")
    hw = d // 2
    glb = (aw + 2 * kvw + 2 * d) // hw

    def merge_fwd(i, j, ya, yr, gla, glr):
        return _sigmoid(gla) * ya + _sigmoid(glr) * yr

    u = _rowwise(merge_fwd, [_row(y_attn), _row(y_rnn), _row(proj, glb, lb), _row(proj, glb + 2, lb)],
                 [("row", BF16)], nr=s_len // tr, tr=tr, nc=2, cw=hw, name="merge_fwd")[0]
    w_out_f = late_wait("w_out", u)
    sib_up = sibling_start("w_up", u)
    mix = _mm(u, w_out_f, name="out_fwd")

    def res1_fwd(i, j, xv, mixv, gaa, gm, shf, scf):
        x1v = xv + gaa * mixv
        n = x1v * _rms(x1v) * gm
        return x1v, n * (1.0 + scf) + shf

    x1, h2 = _rowwise(res1_fwd, [_row(xs), _row(mix), _bc(_tok(ga_a, sib_up[4])), _bc(g_mlp), _bc(sh_f), _bc(sc_f)],
                      [("row", F32), ("row", BF16)], nr=s_len // tr, tr=tr, nc=1, cw=d, name="norm2_fwd")
    relu2 = lambda z: jnp.square(jnp.maximum(z, 0.0))
    w_up_f = sibling_wait("w_up", sib_up, h2)
    up = _mm(h2, w_up_f, name="up_fwd")
    w_down_f = sibling_wait("w_down", sibling_start("w_down", up), up)
    down = _mm(up, w_down_f, a_fn=relu2, name="down_fwd")

    def final_fn(i, j, x1v, dn, tg, gaf, gf):
        x2 = x1v + gaf * dn
        r = _rms(x2)
        xh = x2 * r
        e = xh * gf - tg
        dy = e * (1.0 / d)
        dx2 = _rms_bwd(xh, r, dy * gf)
        return dx2, dx2 * gaf, _colsum(e * e) * (0.5 / d), _colsum(dy * xh), _colsum(dx2 * dn)

    dx2, ddown, loss_cols, g_gfinal, d_ga_f = _rowwise(
        final_fn, [_row(x1), _row(down), _row(tgt), _bc(ga_f), _bc(g_final2)],
        [("row", F32), ("row", BF16), ("acc", 1), ("acc", 1), ("acc", 1)], nr=s_len // tr, tr=tr, nc=1, cw=d,
        name="final_bwd")
    loss = lax.psum(jnp.sum(loss_cols), ("x", "y", "c"))

    dup = _mm(ddown, w_down_f, tb=True, out_fn=lambda r, upv: r * (2.0 * jnp.maximum(upv, 0.0)), extras=[up],
              out_dtype=BF16, name="down_bwd_x")
    gw_down = _mm(up, ddown, ta=True, a_fn=relu2, out_dtype=BF16, name="down_bwd_w")
    sent = dict(w_down=_scatter_start(gw_down, 0, my, "rs_start_w_down"))
    dh2 = _mm(dup, w_up_f, tb=True, name="up_bwd_x")
    gw_up = _mm(h2, dup, ta=True, out_dtype=BF16, name="up_bwd_w")
    sent["w_up"] = _scatter_start(gw_up, 1, my, "rs_start_w_up")
    sc_f_bwd = _tok(sc_f, sent["w_down"][0][4], sent["w_up"][0][4])

    def norm2_bwd(i, j, x1v, mixv, dh2v, dx2v, gaa, gm, scf):
        r = _rms(x1v)
        xh = x1v * r
        n = xh * gm
        dn = dh2v * (1.0 + scf)
        dx1 = dx2v + _rms_bwd(xh, r, dn * gm)
        return dx1, dx1 * gaa, _colsum(dh2v), _colsum(dh2v * n), _colsum(dn * xh), _colsum(dx1 * mixv)

    dx1, dmix, d_sh_f, d_sc_f, g_gmlp, d_ga_a = _rowwise(
        norm2_bwd, [_row(x1), _row(mix), _row(dh2), _row(dx2), _bc(ga_a), _bc(g_mlp), _bc(sc_f_bwd)],
        [("row", F32), ("row", BF16), ("acc", 1), ("acc", 1), ("acc", 1), ("acc", 1)], nr=s_len // tr, tr=tr, nc=1,
        cw=d, name="norm2_bwd")

    du = _mm(dmix, w_out_f, tb=True, name="out_bwd_x")
    gw_out = _mm(u, dmix, ta=True, out_dtype=BF16, name="out_bwd_w")
    sent["w_out"] = _scatter_start(gw_out, 0, my, "rs_start_w_out")

    def merge_bwd(i, j, duv, ya, yr, gla, glr):
        ga, gr = _sigmoid(gla), _sigmoid(glr)
        return duv * ga, duv * gr, duv * ya * ga * (1.0 - ga), duv * yr * gr * (1.0 - gr)

    dya, dyr, dgla, dglr = _rowwise(
        merge_bwd, [_row(du), _row(y_attn), _row(y_rnn), _row(proj, glb, lb), _row(proj, glb + 2, lb)],
        [("row", BF16)] * 4, nr=s_len // tr, tr=tr, nc=2, cw=hw, name="merge_bwd")
    dattn = _mm(dya, w_oa_f, tb=True, out_dtype=BF16, name="o_attn_bwd_x")
    gw_oa = _mm(attn_o, dya, ta=True, out_dtype=BF16, name="o_attn_bwd_w")
    drg = _mm(dyr, w_or_f, tb=True, out_dtype=BF16, name="o_rnn_bwd_x")
    gw_or = _mm(rg, dyr, ta=True, out_dtype=BF16, name="o_rnn_bwd_w")
    sent["w_o_attn"] = _scatter_start(gw_oa, 0, my, "rs_start_w_o_attn")
    sent["w_o_rnn"] = _scatter_start(gw_or, 0, my, "rs_start_w_o_rnn")
    conv_b_bwd = _tok(conv_b, sent["w_out"][0][4], sent["w_o_attn"][0][4], sent["w_o_rnn"][0][4])

    dq, dk, dv = _attn_bwd(q, k_all, v_all, dattn, tq=tq, name="attn_bwd")
    dxr, dxg, dvec, g_wrg, g_wig = _rglru_bwd(proj, xr_blk, xg_blk, shift_masks, h_fwd, h_bwd, drg, conv_w_f, conv_b_bwd, w_rg[0], w_ig[0], b_rg_f,
                                               b_ig_f, lam_f, l_len=l_len, name="rglru_bwd")
    gate_part = _pack_flat([g_wrg, g_wig], N_DEV * SUBLANES)
    gate_sent = _scatter_start(gate_part, 0, my, "rs_start_gates")
    q_gain_bwd = _tok(q_gain, gate_sent[0][4])

    dq_p, g_qgain = _rowwise(lambda i, j, xv, dov, gain, ce, se: _qk_bwd_heads(xv, dov, gain, ce, se),
                             [_row(proj, 0, lb, aw), _row(dq, 0, 0, aw), _full(q_gain_bwd), _row(cos_e, 0, lb, HEAD_DIM),
                              _row(sin_e, 0, lb, HEAD_DIM)],
                             [("row", BF16, aw), ("acc", 1, HEAD_DIM)], nr=s_len // tr, tr=tr, nc=1, cw=aw, name="q_bwd")

    def kv_bwd(i, j, xv, dkv, dvv, gain, ce, se):
        dk_p, dgain = _qk_bwd_heads(xv, dkv, gain, ce, se)
        return dk_p, dvv, dgain

    dk_p, dv_p, g_kgain = _rowwise(kv_bwd, [_row(proj, kb, 0, kvw), _row(dk, 0, 0, kvw), _row(dv, 0, 0, kvw),
                                            _full(k_gain), _row(cos_e, 0, 0, HEAD_DIM), _row(sin_e, 0, 0, HEAD_DIM)],
                                   [("row", BF16, kvw), ("row", BF16, kvw), ("acc", 1, HEAD_DIM)], nr=t_len // tr, tr=tr,
                                   nc=1, cw=kvw, name="kv_bwd")

    def dproj_fn(i, j, dqv, dkv, dvv, dxrv, dxgv, dgav, dgrv):
        zero_ctx = lambda a: jnp.where(i < lb, jnp.zeros_like(a), a)
        return jnp.concatenate([zero_ctx(dqv), dkv, dvv, dxrv, dxgv, zero_ctx(dgav), zero_ctx(dgrv)], axis=1)

    dproj = _rowwise(dproj_fn, [_row(dq_p, 0, -lb, aw), _row(dk_p, 0, 0, kvw), _row(dv_p, 0, 0, kvw), _row(dxr, 0, 0, d),
                                _row(dxg, 0, 0, d), _row(dgla, 0, -lb, d), _row(dglr, 0, -lb, d)],
                     [("row", BF16, n_in)], nr=t_len // tr, tr=tr, nc=1, cw=n_in, name="dproj_pack")[0]
    dh_all = _mm(dproj, w_in_f, tb=True, name="proj_bwd_x")
    g_mix_bwd = g_mix

    def norm1_bwd(i, j, xv, dhv, dresv, g, sc):
        r = _rms(xv)
        xh = xv * r
        dn = dhv * (1.0 + sc)
        return dresv + _rms_bwd(xh, r, dn * g), _colsum(dhv), _colsum(dhv * (xh * g)), _colsum(dn * xh)

    def norm1_bwd_ctx(i, j, xv, dhv, g, sc):
        xh = xv * _rms(xv)
        return _colsum(dhv), _colsum(dhv * (xh * g)), _colsum(dhv * (1.0 + sc) * xh)

    dx_lat, d_sh_a, d_sc_a, g_gmix_lat = _rowwise(
        norm1_bwd, [_row(xs), _row(dh_all, 0, lb), _row(dx1), _bc(g_mix_bwd), _bc(sc_a)],
        [("row", F32)] + [("acc", 1)] * 3, nr=s_len // tr, tr=tr, nc=1, cw=d, name="norm1_bwd")
    d_csh, d_csc, g_gmix_ctx = _rowwise(
        norm1_bwd_ctx, [_row(ctxs), _row(dh_all), _bc(g_mix_bwd), _bc(csc_a)],
        [("acc", 1)] * 3, nr=lb, tr=tr, nc=1, cw=d, name="norm1_bwd_ctx")
    g_gmix = g_gmix_lat + g_gmix_ctx
    grad_x = dx_lat[None]

    dmod_lat = jnp.concatenate([d_sh_a, d_sc_a, d_ga_a, d_sh_f, d_sc_f, d_ga_f], axis=1)
    dmod_ctx = jnp.concatenate([d_csh, d_csc, jnp.zeros((1, (N_MOD - 2) * d), F32)], axis=1)
    dm_g = _all_gather([jnp.concatenate([dmod_lat, dmod_ctx, jnp.zeros((SUBLANES - 2, N_MOD * d), F32)], axis=0)], [0],
                       "ag_dmod")[0]
    dm16 = jnp.concatenate([dm_g[0::SUBLANES], dm_g[1::SUBLANES]], axis=0)
    dm_sh = lax.dynamic_slice(dm16, (0, my * nmod), (2 * N_DEV, nmod))
    gw_mod = _mm(c16, dm_sh, ta=True, a_fn=_silu, name="mod_bwd_w")
    dc16 = _mm(dm_sh, w_mod[0], tb=True, name="mod_bwd_x")

    def cctx_fn(i, j, pv, cc, dl, dc):
        row = lax.broadcasted_iota(jnp.int32, pv.shape, 0)
        return _colsum(jnp.where(row >= N_DEV, pv, 0.0)) * _dsilu(cc), dl + dc

    g_cctx, g_bmod = _rowwise(cctx_fn, [_full(dc16), _full(c_ctx2), _full(dmod_lat), _full(dmod_ctx)],
                              [("row", F32, d), ("row", F32, N_MOD * d)], nr=1, tr=1, nc=1, cw=d, name="cctx_bwd")

    rep_part = dict(c_ctx=g_cctx, b_mod=g_bmod, g_mix=g_gmix, g_mlp=g_gmlp, q_gain=g_qgain, k_gain=g_kgain,
                    conv_b=dvec[4:5], g_final=g_gfinal)
    rep_small = [n for n in REPLICATED if n in rep_part]
    gates = [n for n in REPLICATED if n not in rep_part]
    part_pack = _pack_flat([rep_part[n] for n in rep_small], PACK_ROWS)
    rep_rows = part_pack.shape[0]
    gate_land = _xfer_wait("scatter", gate_sent[0], 0, g_bmod, "rs_wait_gates")
    gate_sum = _sum_parts(gate_sent[1], gate_land, name="gate_sum")
    (gate_full, parts), order = _all_gather([gate_sum, part_pack], [0, 0], "ag_rep_grads", with_token=True)
    small_g = jnp.concatenate([dvec[0:4], dvec[5:11], jnp.zeros((SMALL_ROWS - 10, d), F32)], axis=0)
    (small_recv,), order = _all_to_all([_tok(small_g, order)], [1], "a2a_small_grads")

    halves = {}
    for part, lo in (("top", 0), ("bot", d // 2)):
        gw_part = _mm(h_all, dproj, ta=True, a_cols=(lo, d // 2), out_dtype=BF16, name="proj_bwd_w_" + part,
                      bias=_tok(jnp.zeros((1, n_in), F32), order))
        halves[part] = _chip_scatter_start(_pair_reduce(gw_part, my, "pair_w_in_" + part), my, "rs_start_w_in_" + part)
        order = halves[part][0][4]
    started = order

    res = {}
    for names, pr, mult, nm in ((rep_small, parts.reshape(N_DEV, rep_rows, LANES), PACK_ROWS, "adamw_rep"),
                                (gates, gate_full[None], N_DEV * SUBLANES, "adamw_gates")):
        rep_out = _adamw(pr, _tok(_pack_flat([wd[n] for n in names], mult), started),
                         _pack_flat([md[n] for n in names], mult), _pack_flat([vd[n] for n in names], mult), name=nm)
        for kind, packed in zip(("g", "d", "m", "v"), rep_out):
            for n, a in zip(names, _unpack_flat(packed, [wd[n].shape for n in names])):
                res[kind, n] = a
    out = _adamw(small_recv, _tok(sp_w, started), _pack_small(md), _pack_small(vd), name="adamw_small")
    for kind, a in zip(("g", "d", "m", "v"), out):
        for n, piece in _unpack_small(a).items():
            res[kind, n] = piece

    after = started
    for n, (handle, own, ax, kind) in sent.items():
        land = _xfer_wait(kind, handle, ax, after, "rs_wait_" + n)
        out = _adamw(land, wd[n][0], md[n][0], vd[n][0], own=own, name="adamw_" + n)
        after = out[0]
        for kind, a in zip(("g", "d", "m", "v"), out):
            res[kind, n] = a[None]
    out = _adamw(gw_mod[None], w_mod[0], m_w_mod[0], v_w_mod[0], name="adamw_w_mod")
    after = out[0]
    for kind, a in zip(("g", "d", "m", "v"), out):
        res[kind, 'w_mod'] = a[None]
    lands, owns = [], []
    for part, (handle, own, ax, kind) in halves.items():
        lands.append(_xfer_wait(kind, handle, ax, after, "rs_wait_w_in_" + part))
        owns.append(own)
        after = lands[-1]
    out = _adamw(jnp.concatenate(lands, axis=1), w_in[0], m_w_in[0], v_w_in[0], own=jnp.concatenate(owns, axis=0),
                 name="adamw_w_in")
    for kind, a in zip(("g", "d", "m", "v"), out):
        res[kind, 'w_in'] = a[None]

    outs = [loss, grad_x]
    for kind in ("g", "d", "m", "v"):
        outs += [res[kind, n].reshape(wd[n].shape) for n in WEIGHTS]
    return tuple(outs)
```

```python
import math

import jax
import jax.numpy as jnp
from jax import lax
from jax.experimental import pallas as pl
from jax.experimental.pallas import tpu as pltpu

F32 = jnp.float32
BF16 = jnp.bfloat16
MESH = pl.DeviceIdType.MESH

N_DEV = 8
HEAD_DIM = 128
Q_GROUP = 4
RNN_BLOCK = 128
GRID_W = 64
CONV_WIDTH = 4
LRU_C = 8.0
ROPE_THETA = 10000.0
NORM_EPS = 1e-6
N_MOD = 6
ADAM_LR = 0.001
ADAM_B1 = 0.9
ADAM_B2 = 0.999
ADAM_EPS = 1e-08
ADAM_WD = 0.01
ADAM_STEP = 10

LANES = 128
SUBLANES = 8
VMEM_LIMIT = 56 * 1024 * 1024
HBM_SPEC = pl.BlockSpec(memory_space=pltpu.HBM)


def _params(sem=None, vmem=VMEM_LIMIT):
    kw = dict(vmem_limit_bytes=vmem)
    if sem is not None:
        kw["dimension_semantics"] = sem
    return pltpu.CompilerParams(**kw)


def _pick(dim, target, mult):
    for t in range(min(dim, target), 0, -1):
        if dim % t == 0 and t % mult == 0:
            return t
    return dim


def _my_pos():
    return lax.axis_index("x"), lax.axis_index("y"), lax.axis_index("c")


def _flip(pos, d):
    x, y, c = pos
    return (1 - x if d & 4 else x, 1 - y if d & 2 else y, 1 - c if d & 1 else c)


def _lin(pos):
    return 4 * pos[0] + 2 * pos[1] + pos[2]


def _slab(ref, axis, idx, size):
    if axis == 0:
        return ref.at[pl.ds(idx * size, size)]
    assert axis == 1
    return ref.at[:, pl.ds(idx * size, size)]


def _all_gather(arrs, axes, name, with_token=False):
    n = len(arrs)
    sizes = [a.shape[ax] for a, ax in zip(arrs, axes)]

    def body(*refs):
        ins, outs, token = refs[:n], refs[n:2 * n], refs[2 * n]
        send_sems, recv_sems, local_sems = refs[2 * n + 1:]
        token[...] = jnp.zeros_like(token)
        me = _my_pos()
        sibling = _flip(me, 1)
        chips = [2, 4, 6]

        def copy(a, k, block_pos, to, src=None):
            dst = _slab(outs[a], axes[a], _lin(block_pos), sizes[a])
            return pltpu.make_async_remote_copy(
                src_ref=dst if src is None else src, dst_ref=dst,
                send_sem=send_sems.at[7 * a + k], recv_sem=recv_sems.at[7 * a + k],
                device_id=to, device_id_type=MESH)

        local, first, passed = [], [], []
        for a in range(n):
            cp = pltpu.make_async_copy(ins[a], _slab(outs[a], axes[a], _lin(me), sizes[a]), local_sems.at[a])
            cp.start()
            local.append(cp)
        for a in range(n):
            f = [copy(a, 0, me, sibling, src=ins[a])]
            f += [copy(a, 1 + j, me, _flip(me, m), src=ins[a]) for j, m in enumerate(chips)]
            for cp in f:
                cp.start()
            first += f
        for a in range(n):
            for j, m in enumerate(chips):
                copy(a, 1 + j, _flip(me, m), me).wait_recv()
                cp = copy(a, 4 + j, _flip(me, m), sibling)
                cp.start()
                passed.append(cp)
        for a in range(n):
            copy(a, 0, sibling, me).wait_recv()
            for j, m in enumerate(chips):
                copy(a, 4 + j, _flip(sibling, m), me).wait_recv()
        for cp in first + passed:
            cp.wait_send()
        for cp in local:
            cp.wait()

    out_shape = []
    for a, ax in zip(arrs, axes):
        shp = list(a.shape)
        shp[ax] *= N_DEV
        out_shape.append(jax.ShapeDtypeStruct(tuple(shp), a.dtype))
    out_shape.append(jax.ShapeDtypeStruct((SUBLANES, LANES), F32))
    res = pl.pallas_call(
        body, name=name, out_shape=tuple(out_shape),
        in_specs=[HBM_SPEC] * n, out_specs=tuple([HBM_SPEC] * n + [pl.BlockSpec(memory_space=pltpu.VMEM)]),
        scratch_shapes=[pltpu.SemaphoreType.DMA((7 * n,)), pltpu.SemaphoreType.DMA((7 * n,)),
                        pltpu.SemaphoreType.DMA((n,))],
    )(*arrs)
    return (list(res[:n]), res[n]) if with_token else list(res[:n])


def _gather_proj(h_all, w_shard, name):
    t_len, kdim = h_all.shape
    sz = w_shard.shape[1]
    tm = _pick(t_len, 1024, 16)
    n_m = t_len // tm

    def body(h_ref, w_ref, wfull_ref, proj_ref, token, a_buf, b_buf, o_buf, send_sems, recv_sems, io_sems, out_sems):
        token[...] = jnp.zeros_like(token)
        me = _my_pos()
        sibling = _flip(me, 1)
        chips = [2, 4, 6]
        slab = lambda pos: _slab(wfull_ref, 1, _lin(pos), sz)

        def copy(k, block_pos, to, src=None):
            return pltpu.make_async_remote_copy(
                src_ref=slab(block_pos) if src is None else src, dst_ref=slab(block_pos),
                send_sem=send_sems.at[k], recv_sem=recv_sems.at[k], device_id=to, device_id_type=MESH)

        pending = []

        def block_product(pos, b_src):
            load = pltpu.make_async_copy(b_src, b_buf, io_sems.at[1])
            load.start()
            load.wait()
            for i in range(n_m):
                slot = len(pending) % 2
                if len(pending) >= 2:
                    pending[-2].wait()
                o_buf[slot] = _dot(a_buf[pl.ds(i * tm, tm), :], b_buf[...], 1, 0)
                out = pltpu.make_async_copy(
                    o_buf.at[slot], proj_ref.at[pl.ds(i * tm, tm), pl.ds(_lin(pos) * sz, sz)], out_sems.at[slot])
                out.start()
                pending.append(out)

        local = pltpu.make_async_copy(w_ref, slab(me), io_sems.at[0])
        local.start()
        first = [copy(0, me, sibling, src=w_ref)] + [copy(1 + j, me, _flip(me, m), src=w_ref) for j, m in enumerate(chips)]
        for cp in first:
            cp.start()
        load_a = pltpu.make_async_copy(h_ref, a_buf, io_sems.at[2])
        load_a.start()
        load_a.wait()
        block_product(me, w_ref)
        copy(0, sibling, me).wait_recv()
        block_product(sibling, slab(sibling))
        passed = []
        for j, m in enumerate(chips):
            src_pos = _flip(me, m)
            copy(1 + j, src_pos, me).wait_recv()
            cp = copy(4 + j, src_pos, sibling)
            cp.start()
            passed.append(cp)
            block_product(src_pos, slab(src_pos))
        for j, m in enumerate(chips):
            src_pos = _flip(sibling, m)
            copy(4 + j, src_pos, me).wait_recv()
            block_product(src_pos, slab(src_pos))
        for out in pending[-2:]:
            out.wait()
        for cp in first + passed:
            cp.wait_send()
        local.wait()

    return pl.pallas_call(
        body, name=name,
        out_shape=(jax.ShapeDtypeStruct((kdim, N_DEV * sz), w_shard.dtype), jax.ShapeDtypeStruct((t_len, N_DEV * sz), F32),
                   jax.ShapeDtypeStruct((SUBLANES, LANES), F32)),
        in_specs=[HBM_SPEC, HBM_SPEC], out_specs=(HBM_SPEC, HBM_SPEC, pl.BlockSpec(memory_space=pltpu.VMEM)),
        scratch_shapes=[pltpu.VMEM((t_len, kdim), h_all.dtype), pltpu.VMEM((kdim, sz), w_shard.dtype),
                        pltpu.VMEM((2, tm, sz), F32), pltpu.SemaphoreType.DMA((7,)), pltpu.SemaphoreType.DMA((7,)),
                        pltpu.SemaphoreType.DMA((3,)), pltpu.SemaphoreType.DMA((2,))],
        compiler_params=_params(),
    )(h_all, w_shard)


def _all_to_all(arrs, axes, name):
    n = len(arrs)
    sizes = [a.shape[ax] // N_DEV for a, ax in zip(arrs, axes)]

    def body(*refs):
        ins, outs, token = refs[:n], refs[n:2 * n], refs[2 * n]
        send_sems, recv_sems, local_sems = refs[2 * n + 1:]
        token[...] = jnp.zeros_like(token)
        me = _my_pos()
        my = _lin(me)
        copies, local = [], []
        for a in range(n):
            cp = pltpu.make_async_copy(_slab(ins[a], axes[a], my, sizes[a]), outs[a].at[my], local_sems.at[a])
            cp.start()
            local.append(cp)
        for d in range(1, N_DEV):
            peer = _flip(me, d)
            for a in range(n):
                cp = pltpu.make_async_remote_copy(
                    src_ref=_slab(ins[a], axes[a], _lin(peer), sizes[a]), dst_ref=outs[a].at[my],
                    send_sem=send_sems.at[7 * a + d - 1], recv_sem=recv_sems.at[7 * a + d - 1],
                    device_id=peer, device_id_type=MESH)
                cp.start()
                copies.append((a, d, cp))
        for a, d, cp in copies:
            peer = _flip(me, d)
            pltpu.make_async_remote_copy(
                src_ref=_slab(ins[a], axes[a], _lin(peer), sizes[a]), dst_ref=outs[a].at[_lin(peer)],
                send_sem=send_sems.at[7 * a + d - 1], recv_sem=recv_sems.at[7 * a + d - 1],
                device_id=peer, device_id_type=MESH).wait_recv()
            cp.wait_send()
        for cp in local:
            cp.wait()

    out_shape = []
    for a, ax, s in zip(arrs, axes, sizes):
        shp = list(a.shape)
        shp[ax] = s
        out_shape.append(jax.ShapeDtypeStruct((N_DEV,) + tuple(shp), a.dtype))
    out_shape.append(jax.ShapeDtypeStruct((SUBLANES, LANES), F32))
    res = pl.pallas_call(
        body, name=name, out_shape=tuple(out_shape),
        in_specs=[HBM_SPEC] * n, out_specs=tuple([HBM_SPEC] * n + [pl.BlockSpec(memory_space=pltpu.VMEM)]),
        scratch_shapes=[pltpu.SemaphoreType.DMA((7 * n,)), pltpu.SemaphoreType.DMA((7 * n,)),
                        pltpu.SemaphoreType.DMA((n,))],
    )(*arrs)
    return list(res[:n]), res[n]


SEM_SPEC = pl.BlockSpec(memory_space=pltpu.SEMAPHORE)
_EFFECT = pltpu.SideEffectType.DATAFLOW_SIDE_EFFECTING
N_PEERS = N_DEV - 1


def _split_copies(kind, src_ref, land_ref, send_sems, recv_sems, axis, size, arriving):
    me = _my_pos()
    out = []
    if kind == "sibling":
        sibling = _flip(me, 1)
        for k in range(N_CHIPS):
            blk = _slab(land_ref, axis, 2 * k + (sibling[2] if arriving else me[2]), size)
            out.append(pltpu.make_async_remote_copy(src_ref=blk, dst_ref=blk, send_sem=send_sems.at[k],
                                                    recv_sem=recv_sems.at[k], device_id=sibling, device_id_type=MESH))
        return out
    for dd in (range(2, N_DEV, 2) if kind in ("chips", "gather_chips") else range(1, N_DEV)):
        peer = _flip(me, dd)
        k = dd - 1
        if kind in ("gather", "gather_chips"):
            src = _slab(land_ref, axis, _lin(me), size)
            dst = _slab(land_ref, axis, _lin(peer if arriving else me), size)
        elif kind == "scatter":
            src, dst = _slab(src_ref, axis, _lin(peer), size), land_ref.at[k]
        else:
            k = dd // 2 - 1
            src, dst = src_ref.at[2 * peer[0] + peer[1]], land_ref.at[k]
        out.append(pltpu.make_async_remote_copy(
            src_ref=src, dst_ref=dst, send_sem=send_sems.at[k], recv_sem=recv_sems.at[k],
            device_id=peer, device_id_type=MESH))
    return out


N_CHIPS = N_DEV // 2


def _split_size(kind, src_shape, land_shape, axis):
    if kind in ("gather", "gather_chips", "sibling"):
        return land_shape[axis] // N_DEV
    return src_shape[axis] // {"scatter": N_DEV, "chips": N_CHIPS}[kind]


def _xfer_start(kind, src, land, axis, name):
    size = _split_size(kind, src.shape, land.shape, axis)

    def body(src_ref, land_ref, send_sems, recv_sems, src_thru, land_thru, token):
        for cp in _split_copies(kind, src_ref, land_ref, send_sems, recv_sems, axis, size, False):
            cp.start()
        token[...] = jnp.zeros_like(token)

    return pl.pallas_call(
        body, name=name,
        out_shape=(pltpu.SemaphoreType.DMA((N_PEERS,)), pltpu.SemaphoreType.DMA((N_PEERS,)),
                   pltpu.HBM(src.shape, src.dtype), pltpu.HBM(land.shape, land.dtype),
                   jax.ShapeDtypeStruct((SUBLANES, LANES), F32)),
        in_specs=(HBM_SPEC, HBM_SPEC),
        out_specs=(SEM_SPEC, SEM_SPEC, HBM_SPEC, HBM_SPEC, pl.BlockSpec(memory_space=pltpu.VMEM)),
        input_output_aliases={0: 2, 1: 3},
        compiler_params=pltpu.CompilerParams(has_side_effects=_EFFECT),
    )(pltpu.with_memory_space_constraint(src, pltpu.HBM), pltpu.with_memory_space_constraint(land, pltpu.HBM))


def _xfer_wait(kind, handle, axis, after, name):
    send_sems, recv_sems, src_thru, land_thru, _ = handle
    size = _split_size(kind, src_thru.shape, land_thru.shape, axis)

    def body(src_ref, land_ref, send_sems, recv_sems, after_ref, src_dead, got_ref):
        for cp in _split_copies(kind, src_ref, land_ref, send_sems, recv_sems, axis, size, True):
            cp.wait_send()
            cp.wait_recv()

    return pl.pallas_call(
        body, name=name,
        out_shape=(pltpu.HBM(src_thru.shape, src_thru.dtype), pltpu.HBM(land_thru.shape, land_thru.dtype)),
        in_specs=(HBM_SPEC, HBM_SPEC, SEM_SPEC, SEM_SPEC, pl.BlockSpec(memory_space=pl.ANY)),
        out_specs=(HBM_SPEC, HBM_SPEC), input_output_aliases={0: 0, 1: 1},
        compiler_params=pltpu.CompilerParams(has_side_effects=_EFFECT),
    )(src_thru, land_thru, send_sems, recv_sems, after)[1]


def _place_block(shard, axis, my, name):
    r, c = shard.shape
    tr = _pick(r, max(16, (1 << 19) // c), 16)
    nr = r // tr
    if axis == 0:
        full, out_map = (r * N_DEV, c), (lambda i, my_ref: (my_ref[0] * nr + i, 0))
    else:
        full, out_map = (r, c * N_DEV), (lambda i, my_ref: (i, my_ref[0]))

    def body(my_ref, x_ref, o_ref):
        o_ref[...] = x_ref[...].astype(BF16)

    return pl.pallas_call(
        body, name=name, out_shape=jax.ShapeDtypeStruct(full, BF16),
        grid_spec=pltpu.PrefetchScalarGridSpec(
            num_scalar_prefetch=1, grid=(nr,), in_specs=[pl.BlockSpec((tr, c), lambda i, my_ref: (i, 0))],
            out_specs=pl.BlockSpec((tr, c), out_map)),
        compiler_params=_params(("arbitrary",)),
    )(jnp.reshape(my, (1,)).astype(jnp.int32), shard)


def _scatter_start(grad, axis, my, name):
    shp = list(grad.shape)
    shp[axis] //= N_DEV
    own = lax.dynamic_slice_in_dim(grad, my * shp[axis], shp[axis], axis)
    return (_xfer_start("scatter", grad, lax.empty((N_PEERS,) + tuple(shp), grad.dtype), axis, name), own, axis,
            "scatter")


def _pair_reduce(grad, my, name):
    r, cols = grad.shape
    sz = cols // N_DEV

    def exchange(g_ref, got_ref, send_sems, recv_sems):
        me = _my_pos()
        sibling = _flip(me, 1)
        cps = []
        for k in range(N_CHIPS):
            cp = pltpu.make_async_remote_copy(
                src_ref=_slab(g_ref, 1, 2 * k + sibling[2], sz), dst_ref=_slab(got_ref, 1, k, sz),
                send_sem=send_sems.at[k], recv_sem=recv_sems.at[k], device_id=sibling, device_id_type=MESH)
            cp.start()
            cps.append(cp)
        for cp in cps:
            cp.wait()

    got = pl.pallas_call(
        exchange, name=name + "_swap", out_shape=jax.ShapeDtypeStruct((r, N_CHIPS * sz), grad.dtype),
        in_specs=[HBM_SPEC], out_specs=HBM_SPEC,
        scratch_shapes=[pltpu.SemaphoreType.DMA((N_CHIPS,)), pltpu.SemaphoreType.DMA((N_CHIPS,))],
    )(grad)

    tr = _pick(r, 512, 16)

    def add(c_ref, g_ref, p_ref, o_ref):
        o_ref[...] = (g_ref[...].astype(F32) + p_ref[...].astype(F32)).astype(o_ref.dtype)

    def add(c_ref, g_ref, p_ref, o_ref):
        o_ref[0] = (g_ref[...].astype(F32) + p_ref[...].astype(F32)).astype(o_ref.dtype)

    return pl.pallas_call(
        add, name=name + "_sum", out_shape=jax.ShapeDtypeStruct((N_CHIPS, r, sz), grad.dtype),
        grid_spec=pltpu.PrefetchScalarGridSpec(
            num_scalar_prefetch=1, grid=(N_CHIPS, r // tr),
            in_specs=[pl.BlockSpec((tr, sz), lambda k, i, c_ref: (i, 2 * k + c_ref[0])),
                      pl.BlockSpec((tr, sz), lambda k, i, c_ref: (i, k))],
            out_specs=pl.BlockSpec((1, tr, sz), lambda k, i, c_ref: (k, i, 0))),
        compiler_params=_params(("parallel", "parallel")),
    )(jnp.reshape(my % 2, (1,)).astype(jnp.int32), grad, got)


def _chip_scatter_start(pair_sum, my, name):
    _, r, sz = pair_sum.shape
    own = lax.dynamic_index_in_dim(pair_sum, my // 2, 0, keepdims=False)
    return (_xfer_start("chips", pair_sum, lax.empty((N_CHIPS - 1, r, sz), pair_sum.dtype), 0, name), own, 0, "chips")


def _tok(a, *tokens):
    for t in tokens:
        a = a + t[0, 0].astype(a.dtype)
    return a


def _mm(a, b, *, name, ta=False, tb=False, out_dtype=F32, a_fn=None, out_fn=None, extras=(), bias=None,
        tm=1024, tn=1024, tk=2816, a_cols=None):
    a_blk0 = 0
    if ta:
        kdim, m = a.shape
        if a_cols is not None:
            m = a_cols[1]
    else:
        m, kdim = a.shape
    if tb:
        n, kb = b.shape
    else:
        kb, n = b.shape
    assert kb == kdim, (a.shape, b.shape, ta, tb)
    tm = _pick(m, tm, LANES if ta else SUBLANES)
    if a_cols is not None:
        assert ta and a_cols[0] % tm == 0
        a_blk0 = a_cols[0] // tm
    tn = _pick(n, tn, LANES)
    tk = _pick(kdim, tk, LANES)
    nk = kdim // tk
    ne = len(extras)
    has_bias = bias is not None

    def body(*refs):
        a_ref, b_ref = refs[0], refs[1]
        pos = 2
        bias_ref = None
        if has_bias:
            bias_ref = refs[pos]
            pos += 1
        extra_refs = refs[pos:pos + ne]
        o_ref = refs[pos + ne]
        av = a_ref[...]
        if a_fn is not None:
            av = a_fn(av)
        dn = (((0 if ta else 1,), (1 if tb else 0,)), ((), ()))
        part = lax.dot_general(av.astype(BF16), b_ref[...].astype(BF16), dn, preferred_element_type=F32)

        def finish(r):
            if has_bias:
                r = r + bias_ref[...]
            if out_fn is not None:
                r = out_fn(r, *[e[...] for e in extra_refs])
            o_ref[...] = r.astype(out_dtype)

        if nk == 1:
            finish(part)
            return
        acc = refs[pos + ne + 1]
        k = pl.program_id(2)

        @pl.when(k == 0)
        def _():
            acc[...] = part

        @pl.when(k > 0)
        def _():
            acc[...] += part

        @pl.when(k == nk - 1)
        def _():
            finish(acc[...])

    in_specs = [
        pl.BlockSpec((tk, tm), lambda i, j, k: (k, i + a_blk0)) if ta else pl.BlockSpec((tm, tk), lambda i, j, k: (i, k)),
        pl.BlockSpec((tn, tk), lambda i, j, k: (j, k)) if tb else pl.BlockSpec((tk, tn), lambda i, j, k: (k, j)),
    ]
    args = [a, b]
    if has_bias:
        in_specs.append(pl.BlockSpec((1, tn), lambda i, j, k: (0, j)))
        args.append(bias)
    for e in extras:
        in_specs.append(pl.BlockSpec((tm, tn), lambda i, j, k: (i, j)))
        args.append(e)
    return pl.pallas_call(
        body, name=name, grid=(m // tm, n // tn, nk),
        in_specs=in_specs, out_specs=pl.BlockSpec((tm, tn), lambda i, j, k: (i, j)),
        out_shape=jax.ShapeDtypeStruct((m, n), out_dtype),
        scratch_shapes=[pltpu.VMEM((tm, tn), F32)] if nk > 1 else [],
        compiler_params=_params(("parallel", "parallel", "arbitrary")),
    )(*args)


def _row(arr, cb=0, rb=0, w=None):
    return ("row", arr, cb, rb, w)


def _bc(arr, cb=0, w=None):
    return ("bc", arr, cb, 0, w)


def _full(arr):
    return ("full", arr, 0, 0, None)


def _rowwise(fn, ins, outs, *, nr, tr, nc, cw, name):
    n_in = len(ins)
    outs = [(o[0], o[1], (o[2] if len(o) > 2 and o[2] else cw)) for o in outs]

    def body(*refs):
        in_refs, out_refs = refs[:n_in], refs[n_in:]
        j, i = pl.program_id(0), pl.program_id(1)
        res = fn(i, j, *[r[...] for r in in_refs])
        if not isinstance(res, (tuple, list)):
            res = (res,)
        for (kind, _, _), o_ref, r in zip(outs, out_refs, res):
            if kind == "row":
                o_ref[...] = r.astype(o_ref.dtype)
            else:
                @pl.when(i == 0)
                def _(o_ref=o_ref):
                    o_ref[...] = jnp.zeros_like(o_ref)

                o_ref[...] += r

    in_specs, args = [], []
    for kind, arr, cb, rb, w in ins:
        w = w or cw
        if kind == "row":
            last = arr.shape[0] // tr - 1
            in_specs.append(pl.BlockSpec((tr, w), lambda j, i, cb=cb, rb=rb, last=last: (jnp.clip(i + rb, 0, last), j + cb)))
        elif kind == "bc":
            in_specs.append(pl.BlockSpec((arr.shape[0], w), lambda j, i, cb=cb: (0, j + cb)))
        else:
            in_specs.append(pl.BlockSpec(arr.shape, lambda j, i, nd=arr.ndim: (0,) * nd))
        args.append(arr)
    out_specs, out_shape = [], []
    for kind, spec, w in outs:
        if kind == "row":
            out_specs.append(pl.BlockSpec((tr, w), lambda j, i: (i, j)))
            out_shape.append(jax.ShapeDtypeStruct((nr * tr, nc * w), spec))
        else:
            out_specs.append(pl.BlockSpec((spec, w), lambda j, i: (0, j)))
            out_shape.append(jax.ShapeDtypeStruct((spec, nc * w), F32))
    res = pl.pallas_call(
        body, name=name, grid=(nc, nr), in_specs=in_specs, out_specs=tuple(out_specs), out_shape=tuple(out_shape),
        compiler_params=_params(("parallel", "arbitrary")),
    )(*args)
    return res


def _sum_parts(own, parts, *, name):
    p, r, c = parts.shape
    tr = _pick(r, max(SUBLANES, (1 << 18) // c), SUBLANES)

    def body(o_ref, p_ref, g_out):
        g = o_ref[...].astype(F32)
        for q in range(p):
            g = g + p_ref[q].astype(F32)
        g_out[...] = g

    blk = pl.BlockSpec((tr, c), lambda i: (i, 0))
    return pl.pallas_call(
        body, name=name, grid=(r // tr,), in_specs=[blk, pl.BlockSpec((p, tr, c), lambda i: (0, i, 0))],
        out_specs=blk, out_shape=jax.ShapeDtypeStruct((r, c), F32), compiler_params=_params(("parallel",)),
    )(own, parts)


def _adamw(parts, w, m, v, *, name, own=None):
    p, r, c = parts.shape
    tr = _pick(r, max(SUBLANES, (1 << 18) // c), 16 if parts.dtype == BF16 else SUBLANES)
    bc1 = 1.0 - ADAM_B1 ** ADAM_STEP
    bc2 = 1.0 - ADAM_B2 ** ADAM_STEP
    has_own = own is not None

    def body(*refs):
        p_ref, w_ref, m_ref, v_ref = refs[:4]
        g_out, d_out, m_out, v_out = refs[4 + has_own:]
        if has_own:
            g = refs[4][...].astype(F32) + p_ref[0].astype(F32)
        else:
            g = p_ref[0].astype(F32)
        for q in range(1, p):
            g = g + p_ref[q].astype(F32)
        m_new = ADAM_B1 * m_ref[...] + (1.0 - ADAM_B1) * g
        v_new = ADAM_B2 * v_ref[...] + (1.0 - ADAM_B2) * (g * g)
        m_hat = m_new / bc1
        v_hat = v_new / bc2
        g_out[...] = g
        d_out[...] = -ADAM_LR * (m_hat / (jnp.sqrt(v_hat) + ADAM_EPS) + ADAM_WD * w_ref[...])
        m_out[...] = m_new
        v_out[...] = v_new

    blk = pl.BlockSpec((tr, c), lambda i: (i, 0))
    return pl.pallas_call(
        body, name=name, grid=(r // tr,),
        in_specs=[pl.BlockSpec((p, tr, c), lambda i: (0, i, 0)), blk, blk, blk] + ([blk] if has_own else []),
        out_specs=(blk, blk, blk, blk), out_shape=tuple(jax.ShapeDtypeStruct((r, c), F32) for _ in range(4)),
        compiler_params=_params(("parallel",)),
    )(*([parts, w, m, v] + ([own] if has_own else [])))


_GELU_C = math.sqrt(2.0 / math.pi)
_GELU_K = 0.044715


def _sigmoid(z):
    return 0.5 * jnp.tanh(0.5 * z) + 0.5


def _silu(z):
    return z * _sigmoid(z)


def _dsilu(z):
    s = _sigmoid(z)
    return s * (1.0 + z * (1.0 - s))


def _gelu(z):
    return 0.5 * z * (1.0 + jnp.tanh(_GELU_C * (z + _GELU_K * z * z * z)))


def _gelu_pair(z):
    zz = z * z
    t = jnp.tanh(_GELU_C * (z * (1.0 + _GELU_K * zz)))
    half = 0.5 * (1.0 + t)
    return z * half, half + 0.5 * z * (1.0 - t * t) * (_GELU_C * (1.0 + 3.0 * _GELU_K * zz))


def _softplus(z):
    return jnp.maximum(z, 0.0) + jnp.log(1.0 + jnp.exp(-jnp.abs(z)))


def _rms(xv):
    return lax.rsqrt(jnp.mean(xv * xv, axis=-1, keepdims=True) + NORM_EPS)


def _rms_bwd(xhat, r, dxhat):
    return r * (dxhat - xhat * jnp.mean(dxhat * xhat, axis=-1, keepdims=True))


def _swap_pairs(y):
    lane = lax.broadcasted_iota(jnp.int32, y.shape, 1)
    return jnp.where(lane % 2 == 0, pltpu.roll(y, LANES - 1, 1), pltpu.roll(y, 1, 1))


def _dot(a, b, ca, cb):
    return lax.dot_general(a, b, (((ca,), (cb,)), ((), ())), preferred_element_type=F32)


_SM_SCALE = HEAD_DIM ** -0.5
_EXP2_SCALE = _SM_SCALE * math.log2(math.e)


def _softmax_parts(qh, kk):
    s = _dot(qh, kk, 1, 1)
    p = jnp.exp2((s - jnp.max(s, axis=-1, keepdims=True)) * _EXP2_SCALE)
    return p, 1.0 / jnp.sum(p, axis=-1, keepdims=True)


def _attn_fwd(q, k, v, *, tq, name):
    s_len, aw = q.shape
    t_len, kvw = k.shape
    gw = Q_GROUP * HEAD_DIM

    def body(q_ref, k_ref, v_ref, o_ref):
        kk, vv = k_ref[...], v_ref[...]
        for g in range(Q_GROUP):
            cols = slice(g * HEAD_DIM, (g + 1) * HEAD_DIM)
            p, inv_l = _softmax_parts(q_ref[:, cols], kk)
            o_ref[:, cols] = (_dot(p.astype(BF16), vv, 1, 0) * inv_l).astype(BF16)

    return pl.pallas_call(
        body, name=name, grid=(kvw // HEAD_DIM, s_len // tq),
        in_specs=[pl.BlockSpec((tq, gw), lambda h, i: (i, h)), pl.BlockSpec((t_len, HEAD_DIM), lambda h, i: (0, h)),
                  pl.BlockSpec((t_len, HEAD_DIM), lambda h, i: (0, h))],
        out_specs=pl.BlockSpec((tq, gw), lambda h, i: (i, h)),
        out_shape=jax.ShapeDtypeStruct((s_len, aw), BF16),
        compiler_params=_params(("parallel", "parallel")),
    )(q, k, v)


def _attn_bwd(q, k, v, do, *, tq, name):
    s_len, aw = q.shape
    t_len, kvw = k.shape
    gw = Q_GROUP * HEAD_DIM

    def body(q_ref, k_ref, v_ref, do_ref, dq_ref, dk_ref, dv_ref):
        @pl.when(pl.program_id(1) == 0)
        def _():
            dk_ref[...] = jnp.zeros_like(dk_ref)
            dv_ref[...] = jnp.zeros_like(dv_ref)

        kk, vv = k_ref[...], v_ref[...]
        for g in range(Q_GROUP):
            cols = slice(g * HEAD_DIM, (g + 1) * HEAD_DIM)
            qh, doh = q_ref[:, cols], do_ref[:, cols]
            p, inv_l = _softmax_parts(qh, kk)
            dp = _dot(doh, vv, 1, 1)
            delta = jnp.sum(p * dp, axis=-1, keepdims=True) * inv_l
            dsb = (p * ((dp - delta) * (inv_l * _SM_SCALE))).astype(BF16)
            dq_ref[:, cols] = _dot(dsb, kk, 1, 0)
            dk_ref[...] += _dot(dsb, qh, 0, 0)
            dv_ref[...] += _dot(p.astype(BF16), (doh * inv_l).astype(BF16), 0, 0)

    qspec = pl.BlockSpec((tq, gw), lambda h, i: (i, h))
    kspec = pl.BlockSpec((t_len, HEAD_DIM), lambda h, i: (0, h))
    return pl.pallas_call(
        body, name=name, grid=(kvw // HEAD_DIM, s_len // tq),
        in_specs=[qspec, kspec, kspec, qspec], out_specs=(qspec, kspec, kspec),
        out_shape=(jax.ShapeDtypeStruct((s_len, aw), F32), jax.ShapeDtypeStruct((t_len, kvw), F32),
                   jax.ShapeDtypeStruct((t_len, kvw), F32)),
        compiler_params=_params(("parallel", "arbitrary")),
    )(q, k, v, do)


SHIFTS = (-2, -1, 1, 2)


def _shift_masks(l_len, t_len):
    t = jnp.arange(t_len)[:, None]
    cols = []
    for d in SHIFTS:
        td = t + d
        ok = ((t < l_len) & (td >= 0) & (td < l_len)) | ((t >= l_len) & (td >= l_len) & (td < t_len))
        cols.append(jnp.broadcast_to(ok.astype(F32), (t_len, RNN_BLOCK)))
    return jnp.concatenate(cols, axis=1)


def _shift_rows(xv, d, m_ref, l_len, t_len):
    if d == 0:
        return xv
    k = SHIFTS.index(d)
    return pltpu.roll(xv, (-d) % t_len, 0) * m_ref[:, k * RNN_BLOCK:(k + 1) * RNN_BLOCK]


def _sublane_scan(a8, b8, rev):
    row = lax.broadcasted_iota(jnp.int32, a8.shape, 0)
    for s in (1, 2, 4):
        if rev:
            ok = row < SUBLANES - s
            sh = SUBLANES - s
        else:
            ok = row >= s
            sh = s
        a_sh = jnp.where(ok, pltpu.roll(a8, sh, 0), 1.0)
        b_sh = jnp.where(ok, pltpu.roll(b8, sh, 0), 0.0)
        b8 = a8 * b_sh + b8
        a8 = a8 * a_sh
    return a8, b8


def _scan_chunks(chains, c0, n, carries):
    def step(kk, hs):
        new = []
        for (a_ref, b_ref, h_ref, rev), hprev in zip(chains, hs):
            c = c0 + (n - 1 - kk if rev else kk)
            r0 = pl.multiple_of(c * SUBLANES, SUBLANES)
            big_a, big_b = _sublane_scan(a_ref[pl.ds(r0, SUBLANES), :], b_ref[pl.ds(r0, SUBLANES), :], rev)
            h8 = big_b + big_a * hprev
            h_ref[pl.ds(r0, SUBLANES), :] = h8
            edge = 0 if rev else SUBLANES - 1
            new.append(jnp.broadcast_to(h8[edge:edge + 1, :], h8.shape))
        return tuple(new)

    return lax.fori_loop(0, n, step, tuple(carries))


def _rglru_conv(x, cw_ref, cb_ref, m_ref, l_len, t_len):
    y = cb_ref[...] + _shift_rows(x, -1, m_ref, l_len, t_len) * cw_ref[0:1, :]
    for kk in range(1, CONV_WIDTH):
        y = y + _shift_rows(x, kk - 1, m_ref, l_len, t_len) * cw_ref[kk:kk + 1, :]
    return y


def _rglru_gates(xc, wr, wi, br, bi, lam):
    xb = xc.astype(BF16)
    r = _sigmoid(_dot(xb, wr.astype(BF16), 1, 0) + br)
    ig = _sigmoid(_dot(xb, wi.astype(BF16), 1, 0) + bi)
    sp = _softplus(-lam)
    log_a = (-LRU_C) * r * sp
    a = jnp.exp(log_a)
    z2 = 2.0 * log_a
    series = z2 * (1.0 + z2 * (0.5 + z2 * (1.0 / 6.0 + z2 * (1.0 / 24.0))))
    one_m = jnp.where(jnp.abs(z2) < 0.03, -series, 1.0 - a * a)
    rinv = lax.rsqrt(jnp.maximum(one_m, 1e-30))
    u = ig * xc
    return r, ig, sp, a, one_m * rinv, u, rinv


def _rglru_specs(proj, xr_blk, xg_blk, t_len):
    col = lambda off: pl.BlockSpec((t_len, RNN_BLOCK), lambda h, off=off: (0, off + h))
    vec = lambda rows: pl.BlockSpec((rows, RNN_BLOCK), lambda h: (0, h))
    wsp = pl.BlockSpec((2, 1, RNN_BLOCK, RNN_BLOCK), lambda h: (0, h, 0, 0))
    msp = pl.BlockSpec((t_len, len(SHIFTS) * RNN_BLOCK), lambda h: (0, 0))
    return col(xr_blk), col(xg_blk), vec, wsp, msp


def _rglru_fwd(proj, xr_blk, xg_blk, masks, conv_w, conv_b, w_rg, w_ig, b_rg, b_ig, lam, *, l_len, name):
    t_len = proj.shape[0]
    s_len = t_len - l_len
    d_rnn = conv_b.shape[1]
    n_l, n_t = l_len // SUBLANES, t_len // SUBLANES

    def body(xr_ref, xg_ref, m_ref, cw_ref, cb_ref, wr_ref, wi_ref, br_ref, bi_ref, lam_ref, rg_ref, hf_ref, hb_ref,
             a_s, b_s):
        xc = _rglru_conv(xr_ref[...], cw_ref, cb_ref, m_ref, l_len, t_len)
        for d in range(2):
            _, _, _, a, mult, u, _ = _rglru_gates(xc, wr_ref[d, 0], wi_ref[d, 0], br_ref[d:d + 1, :],
                                               bi_ref[d:d + 1, :], lam_ref[d:d + 1, :])
            a_s[d] = a
            b_s[d] = mult * u
        chains = [(a_s.at[0], b_s.at[0], hf_ref, False), (a_s.at[1], b_s.at[1], hb_ref, True)]
        zero = jnp.zeros((SUBLANES, RNN_BLOCK), F32)
        carry = _scan_chunks(chains, 0, n_l, (zero, zero))
        _scan_chunks(chains, n_l, n_t - n_l, carry)
        rnn = hf_ref[pl.ds(l_len, s_len), :] + hb_ref[pl.ds(l_len, s_len), :]
        rg_ref[...] = (rnn * _gelu(xg_ref[pl.ds(l_len, s_len), :])).astype(BF16)

    xr_spec, xg_spec, vec, wsp, msp = _rglru_specs(proj, xr_blk, xg_blk, t_len)
    col_t = pl.BlockSpec((t_len, RNN_BLOCK), lambda h: (0, h))
    return pl.pallas_call(
        body, name=name, grid=(d_rnn // RNN_BLOCK,),
        in_specs=[xr_spec, xg_spec, msp, vec(CONV_WIDTH), vec(1), wsp, wsp, vec(2), vec(2), vec(2)],
        out_specs=(pl.BlockSpec((s_len, RNN_BLOCK), lambda h: (0, h)), col_t, col_t),
        out_shape=(jax.ShapeDtypeStruct((s_len, d_rnn), BF16), jax.ShapeDtypeStruct((t_len, d_rnn), F32),
                   jax.ShapeDtypeStruct((t_len, d_rnn), F32)),
        scratch_shapes=[pltpu.VMEM((2, t_len, RNN_BLOCK), F32)] * 2,
        compiler_params=_params(("parallel",)),
    )(proj, proj, masks, conv_w, conv_b, w_rg, w_ig, b_rg, b_ig, lam)


N_RNN_VEC = 16


def _rglru_bwd(proj, xr_blk, xg_blk, masks, h_fwd, h_bwd, drg, conv_w, conv_b, w_rg, w_ig, b_rg, b_ig, lam, *, l_len, name):
    t_len = proj.shape[0]
    s_len = t_len - l_len
    d_rnn = conv_b.shape[1]
    n_l, n_t = l_len // SUBLANES, t_len // SUBLANES

    def body(xr_ref, xg_ref, m_ref, hf_ref, hb_ref, drg_ref, cw_ref, cb_ref, wr_ref, wi_ref, br_ref, bi_ref, lam_ref,
             dxr_ref, dxg_ref, dvec_ref, dwr_ref, dwi_ref, a_s, b_s, h_s):
        t = lax.broadcasted_iota(jnp.int32, (t_len, RNN_BLOCK), 0)
        x = xr_ref[...]
        xc = _rglru_conv(x, cw_ref, cb_ref, m_ref, l_len, t_len)
        xb = xc.astype(BF16)
        gates = []
        for d in range(2):
            gt = _rglru_gates(xc, wr_ref[d, 0], wi_ref[d, 0], br_ref[d:d + 1, :], bi_ref[d:d + 1, :],
                              lam_ref[d:d + 1, :])
            gates.append(gt)
        zero = jnp.zeros((SUBLANES, RNN_BLOCK), F32)
        h0, h1 = hf_ref[...], hb_ref[...]
        xg = xg_ref[...]
        dr = jnp.concatenate([jnp.zeros((l_len, RNN_BLOCK), F32), drg_ref[...].astype(F32)], axis=0)
        gel, dgel = _gelu_pair(xg)
        dxg_ref[...] = (dr * (h0 + h1) * dgel).astype(dxg_ref.dtype)
        drnn = dr * gel
        a_s[0] = pltpu.roll(gates[0][3], t_len - 1, 0)
        a_s[1] = pltpu.roll(gates[1][3], 1, 0)
        b_s[0] = drnn
        b_s[1] = drnn
        hprev = [jnp.where(t == 0, 0.0, pltpu.roll(h0, 1, 0)),
                 jnp.where(t == l_len - 1, 0.0, pltpu.roll(h1, t_len - 1, 0))]
        chains = [(a_s.at[0], b_s.at[0], h_s.at[0], True), (a_s.at[1], b_s.at[1], h_s.at[1], False)]
        carry = _scan_chunks(chains, n_l, n_t - n_l, (zero, zero))
        _scan_chunks(chains, 0, n_l, carry)

        dxc = jnp.zeros((t_len, RNN_BLOCK), F32)
        for d in range(2):
            r, ig, sp, a, mult, u, rinv = gates[d]
            lamb = h_s[d]
            da = lamb * hprev[d]
            dmult = lamb * u
            du = lamb * mult
            dlog_a = da * a - dmult * (a * a) * rinv
            dzr = (dlog_a * ((-LRU_C) * sp)) * r * (1.0 - r)
            dsp = jnp.sum(dlog_a * ((-LRU_C) * r), axis=0, keepdims=True)
            dzi = (du * xc) * ig * (1.0 - ig)
            dzrb, dzib = dzr.astype(BF16), dzi.astype(BF16)
            dxc = dxc + du * ig + _dot(dzrb, wr_ref[d, 0].astype(BF16), 1, 1) + _dot(dzib, wi_ref[d, 0].astype(BF16), 1, 1)
            dwr_ref[d, 0] = _dot(xb, dzrb, 0, 0)
            dwi_ref[d, 0] = _dot(xb, dzib, 0, 0)
            dvec_ref[5 + d:6 + d, :] = jnp.sum(dzr, axis=0, keepdims=True)
            dvec_ref[7 + d:8 + d, :] = jnp.sum(dzi, axis=0, keepdims=True)
            dvec_ref[9 + d:10 + d, :] = dsp * (-_sigmoid(-lam_ref[d:d + 1, :]))
        dvec_ref[4:5, :] = jnp.sum(dxc, axis=0, keepdims=True)
        dvec_ref[11:N_RNN_VEC, :] = jnp.zeros((N_RNN_VEC - 11, RNN_BLOCK), F32)
        dxr = jnp.zeros((t_len, RNN_BLOCK), F32)
        for kk in range(CONV_WIDTH):
            dvec_ref[kk:kk + 1, :] = jnp.sum(dxc * _shift_rows(x, kk - 1, m_ref, l_len, t_len), axis=0, keepdims=True)
            dxr = dxr + _shift_rows(dxc, 1 - kk, m_ref, l_len, t_len) * cw_ref[kk:kk + 1, :]
        dxr_ref[...] = dxr.astype(dxr_ref.dtype)

    xr_spec, xg_spec, vec, wsp, msp = _rglru_specs(proj, xr_blk, xg_blk, t_len)
    col_t = pl.BlockSpec((t_len, RNN_BLOCK), lambda h: (0, h))
    return pl.pallas_call(
        body, name=name, grid=(d_rnn // RNN_BLOCK,),
        in_specs=[xr_spec, xg_spec, msp, col_t, col_t, pl.BlockSpec((s_len, RNN_BLOCK), lambda h: (0, h)), vec(CONV_WIDTH),
                  vec(1),
                  wsp, wsp, vec(2), vec(2), vec(2)],
        out_specs=(col_t, col_t, vec(N_RNN_VEC), wsp, wsp),
        out_shape=(jax.ShapeDtypeStruct((t_len, d_rnn), BF16), jax.ShapeDtypeStruct((t_len, d_rnn), BF16),
                   jax.ShapeDtypeStruct((N_RNN_VEC, d_rnn), F32), jax.ShapeDtypeStruct(w_rg.shape, F32),
                   jax.ShapeDtypeStruct(w_ig.shape, F32)),
        scratch_shapes=[pltpu.VMEM((2, t_len, RNN_BLOCK), F32)] * 3,
        compiler_params=_params(("parallel",)),
    )(proj, proj, masks, h_fwd, h_bwd, drg, conv_w, conv_b, w_rg, w_ig, b_rg, b_ig, lam)


def _colsum(v):
    return jnp.sum(v, axis=0, keepdims=True)


def _qk_fwd_heads(xv, gain, cos_e, sin_e):
    outs = []
    for h in range(xv.shape[1] // HEAD_DIM):
        xh = xv[:, h * HEAD_DIM:(h + 1) * HEAD_DIM]
        y = xh * _rms(xh) * gain
        outs.append(y * cos_e + _swap_pairs(y) * sin_e)
    return outs[0] if len(outs) == 1 else jnp.concatenate(outs, axis=1)


def _qk_bwd_heads(xv, dov, gain, cos_e, sin_e):
    outs = []
    dgain = jnp.zeros((1, HEAD_DIM), F32)
    for h in range(xv.shape[1] // HEAD_DIM):
        cols = slice(h * HEAD_DIM, (h + 1) * HEAD_DIM)
        xh, doh = xv[:, cols], dov[:, cols]
        r = _rms(xh)
        xhat = xh * r
        dy = doh * cos_e + _swap_pairs(doh * sin_e)
        dgain = dgain + _colsum(dy * xhat)
        outs.append(_rms_bwd(xhat, r, dy * gain))
    return (outs[0] if len(outs) == 1 else jnp.concatenate(outs, axis=1)), dgain


WEIGHTS = ['c_ctx', 'w_mod', 'b_mod', 'g_mix', 'g_mlp', 'w_in', 'q_gain', 'k_gain', 'conv_w', 'conv_b', 'w_rg', 'b_rg',
           'w_ig', 'b_ig', 'lru_lambda', 'w_o_attn', 'w_o_rnn', 'w_out', 'w_up', 'w_down', 'g_final']
REPLICATED = ['c_ctx', 'b_mod', 'g_mix', 'g_mlp', 'q_gain', 'k_gain', 'conv_b', 'w_rg', 'w_ig', 'g_final']
SMALL_SHARDED = [('conv_w', CONV_WIDTH), ('b_rg', 2), ('b_ig', 2), ('lru_lambda', 2)]
SMALL_ROWS = 16


PACK_ROWS = 64


def _pack_flat(arrs, rows_mult):
    flat = jnp.concatenate([a.reshape(-1) for a in arrs])
    n = flat.shape[0]
    unit = rows_mult * LANES
    pad = (-n) % unit
    if pad:
        flat = jnp.concatenate([flat, jnp.zeros((pad,), flat.dtype)])
    return flat.reshape(-1, LANES)


def _unpack_flat(packed, shapes):
    flat = packed.reshape(-1)
    out, pos = [], 0
    for shp in shapes:
        n = math.prod(shp)
        out.append(flat[pos:pos + n].reshape(shp))
        pos += n
    return out


def _pack_small(d):
    rows = [d[n][0] for n, _ in SMALL_SHARDED]
    cols = rows[0].shape[1]
    used = sum(k for _, k in SMALL_SHARDED)
    return jnp.concatenate(rows + [jnp.zeros((SMALL_ROWS - used, cols), F32)], axis=0)


def _unpack_small(p):
    out, pos = {}, 0
    for n, k in SMALL_SHARDED:
        out[n] = p[pos:pos + k][None]
        pos += k
    return out


def kernel(x, c, ctx, c_ctx, w_mod, b_mod, g_mix, g_mlp, w_in, q_gain, k_gain, conv_w, conv_b, w_rg, b_rg, w_ig, b_ig, lru_lambda, w_o_attn, w_o_rnn, w_out, w_up, w_down, g_final, loss_target, m_c_ctx, m_w_mod, m_b_mod, m_g_mix, m_g_mlp, m_w_in, m_q_gain, m_k_gain, m_conv_w, m_conv_b, m_w_rg, m_b_rg, m_w_ig, m_b_ig, m_lru_lambda, m_w_o_attn, m_w_o_rnn, m_w_out, m_w_up, m_w_down, m_g_final, v_c_ctx, v_w_mod, v_b_mod, v_g_mix, v_g_mlp, v_w_in, v_q_gain, v_k_gain, v_conv_w, v_conv_b, v_w_rg, v_b_rg, v_w_ig, v_b_ig, v_lru_lambda, v_w_o_attn, v_w_o_rnn, v_w_out, v_w_up, v_w_down, v_g_final):
    wd = dict(c_ctx=c_ctx, w_mod=w_mod, b_mod=b_mod, g_mix=g_mix, g_mlp=g_mlp, w_in=w_in, q_gain=q_gain, k_gain=k_gain,
              conv_w=conv_w, conv_b=conv_b, w_rg=w_rg, b_rg=b_rg, w_ig=w_ig, b_ig=b_ig, lru_lambda=lru_lambda,
              w_o_attn=w_o_attn, w_o_rnn=w_o_rnn, w_out=w_out, w_up=w_up, w_down=w_down, g_final=g_final)
    md = dict(zip(WEIGHTS, (m_c_ctx, m_w_mod, m_b_mod, m_g_mix, m_g_mlp, m_w_in, m_q_gain, m_k_gain, m_conv_w, m_conv_b,
                            m_w_rg, m_b_rg, m_w_ig, m_b_ig, m_lru_lambda, m_w_o_attn, m_w_o_rnn, m_w_out, m_w_up,
                            m_w_down, m_g_final)))
    vd = dict(zip(WEIGHTS, (v_c_ctx, v_w_mod, v_b_mod, v_g_mix, v_g_mlp, v_w_in, v_q_gain, v_k_gain, v_conv_w, v_conv_b,
                            v_w_rg, v_b_rg, v_w_ig, v_b_ig, v_lru_lambda, v_w_o_attn, v_w_o_rnn, v_w_out, v_w_up,
                            v_w_down, v_g_final)))

    xs, tgt, ctxs = x[0], loss_target[0], ctx[0]
    s_len, d = xs.shape
    l_len = ctxs.shape[0]
    t_len = s_len + l_len
    n_in = w_in.shape[2] * N_DEV
    aw = (n_in - 4 * d) * 2 // 3
    kvw = aw // Q_GROUP
    assert aw == d and n_in == aw + 2 * kvw + 4 * d
    nmod = N_MOD * d // N_DEV
    tr = min(256, l_len)
    assert l_len % tr == 0 and s_len % tr == 0
    lb = l_len // tr
    tq = min(256, s_len)
    my = _lin(_my_pos())
    c_ctx2 = c_ctx[None]
    g_final2 = g_final[None]

    c_all = _all_gather([jnp.broadcast_to(c, (SUBLANES, d))], [0], "ag_cond")[0][::SUBLANES]
    c16 = jnp.concatenate([c_all, jnp.broadcast_to(c_ctx2, (N_DEV, d))], axis=0)
    b_mod_sh = lax.dynamic_slice(b_mod, (0, my * nmod), (1, nmod))
    mod_part = _mm(c16, w_mod[0], a_fn=_silu, bias=b_mod_sh, name="mod_fwd")
    sp_w = _pack_small(wd)
    (mod_g, sp_f), order = _all_gather([mod_part, sp_w], [1, 1], "ag_mod", with_token=True)
    conv_w_f, b_rg_f, b_ig_f, lam_f = sp_f[0:4], sp_f[4:6], sp_f[6:8], sp_f[8:10]
    mod_lat = lax.dynamic_slice(mod_g, (my, 0), (1, N_MOD * d))
    sh_a, sc_a, ga_a, sh_f, sc_f, ga_f = [mod_lat[:, k * d:(k + 1) * d] for k in range(N_MOD)]
    csh_a, csc_a = mod_g[N_DEV:N_DEV + 1, :d], mod_g[N_DEV:N_DEV + 1, d:2 * d]

    def nm1_fwd(i, j, cv, lv, g, csh, csc, sh, sc):
        is_ctx = i < lb
        xv = jnp.where(is_ctx, cv, lv)
        n = xv * _rms(xv) * g
        return n * (1.0 + jnp.where(is_ctx, csc, sc)) + jnp.where(is_ctx, csh, sh)

    h_all = _rowwise(nm1_fwd, [_row(ctxs), _row(xs, 0, -lb), _bc(g_mix), _bc(csh_a), _bc(csc_a), _bc(sh_a), _bc(sc_a)],
                     [("row", BF16)], nr=t_len // tr, tr=tr, nc=1, cw=d, name="norm1_fwd")[0]
    w_in_f, proj, order = _gather_proj(h_all, _tok(w_in[0], order).astype(BF16), "ag_proj")
    late_axis = dict(w_o_attn=0, w_o_rnn=0, w_out=0, w_up=1, w_down=0)
    placed = [_place_block(wd[n][0], ax, my, "place_" + n) for n, ax in late_axis.items()]
    late_kind = lambda n: "gather_chips" if n in ("w_up", "w_down") else "gather"
    late = {}
    for (n, ax), full in zip(late_axis.items(), placed):
        late[n] = _xfer_start(late_kind(n), order, full, ax, "ag_start_" + n)
        order = late[n][4]
    q_gain_fwd = _tok(q_gain, order)

    late_wait = lambda n, after: _xfer_wait(late_kind(n), late[n], late_axis[n], after, "ag_wait_" + n)

    def sibling_start(n, after):
        arrived = late_wait(n, after)
        return _xfer_start("sibling", jnp.zeros((SUBLANES, LANES), F32), arrived, late_axis[n], "ag_sib_start_" + n)

    sibling_wait = lambda n, handle, after: _xfer_wait("sibling", handle, late_axis[n], after, "ag_sib_wait_" + n)

    pos = jnp.arange(s_len)
    n_freq = HEAD_DIM // 4
    inv_freq = ROPE_THETA ** (-jnp.arange(n_freq, dtype=F32) / n_freq)
    ang = jnp.concatenate([(pos // GRID_W).astype(F32)[:, None] * inv_freq,
                           (pos % GRID_W).astype(F32)[:, None] * inv_freq], axis=-1)
    sign = jnp.where(jnp.arange(HEAD_DIM) % 2 == 0, -1.0, 1.0).astype(F32)
    cos_e = jnp.concatenate([jnp.ones((l_len, HEAD_DIM), F32), jnp.repeat(jnp.cos(ang), 2, axis=-1)], axis=0)
    sin_e = jnp.concatenate([jnp.zeros((l_len, HEAD_DIM), F32), jnp.repeat(jnp.sin(ang), 2, axis=-1) * sign], axis=0)


    kb = aw // kvw
    q = _rowwise(lambda i, j, xv, gain, ce, se: _qk_fwd_heads(xv, gain, ce, se),
                 [_row(proj, 0, lb, aw), _full(q_gain_fwd), _row(cos_e, 0, lb, HEAD_DIM), _row(sin_e, 0, lb, HEAD_DIM)],
                 [("row", BF16, aw)], nr=s_len // tr, tr=tr, nc=1, cw=aw, name="q_fwd")[0]
    k_all, v_all = _rowwise(lambda i, j, kv_, vv_, gain, ce, se: (_qk_fwd_heads(kv_, gain, ce, se), vv_),
                            [_row(proj, kb, 0, kvw), _row(proj, kb + 1, 0, kvw), _full(k_gain),
                             _row(cos_e, 0, 0, HEAD_DIM), _row(sin_e, 0, 0, HEAD_DIM)],
                            [("row", BF16, kvw), ("row", BF16, kvw)], nr=t_len // tr, tr=tr, nc=1, cw=kvw, name="kv_fwd")
    attn_o = _attn_fwd(q, k_all, v_all, tq=tq, name="attn_fwd")

    xr_blk = (aw + 2 * kvw) // RNN_BLOCK
    xg_blk = xr_blk + d // RNN_BLOCK
    shift_masks = _shift_masks(l_len, t_len)
    rg, h_fwd, h_bwd = _rglru_fwd(proj, xr_blk, xg_blk, shift_masks, conv_w_f, conv_b, w_rg[0], w_ig[0], b_rg_f, b_ig_f, lam_f, l_len=l_len,
                    name="rglru_fwd")

    w_oa_f = late_wait("w_o_attn", attn_o)
    y_attn = _mm(attn_o, w_oa_f, name="o_attn_fwd")
    w_or_f = late_wait("w_o_rnn", rg)
    y_rnn = _mm(rg, w_or_f, name="o_rnn_fwd")
    hw = d // 2
    glb = (aw + 2 * kvw + 2 * d) // hw

    def merge_fwd(i, j, ya, yr, gla, glr):
        return _sigmoid(gla) * ya + _sigmoid(glr) * yr

    u = _rowwise(merge_fwd, [_row(y_attn), _row(y_rnn), _row(proj, glb, lb), _row(proj, glb + 2, lb)],
                 [("row", BF16)], nr=s_len // tr, tr=tr, nc=2, cw=hw, name="merge_fwd")[0]
    w_out_f = late_wait("w_out", u)
    sib_up = sibling_start("w_up", u)
    mix = _mm(u, w_out_f, name="out_fwd")

    def res1_fwd(i, j, xv, mixv, gaa, gm, shf, scf):
        x1v = xv + gaa * mixv
        n = x1v * _rms(x1v) * gm
        return x1v, n * (1.0 + scf) + shf

    x1, h2 = _rowwise(res1_fwd, [_row(xs), _row(mix), _bc(_tok(ga_a, sib_up[4])), _bc(g_mlp), _bc(sh_f), _bc(sc_f)],
                      [("row", F32), ("row", BF16)], nr=s_len // tr, tr=tr, nc=1, cw=d, name="norm2_fwd")
    relu2 = lambda z: jnp.square(jnp.maximum(z, 0.0))
    w_up_f = sibling_wait("w_up", sib_up, h2)
    up = _mm(h2, w_up_f, name="up_fwd")
    w_down_f = sibling_wait("w_down", sibling_start("w_down", up), up)
    down = _mm(up, w_down_f, a_fn=relu2, name="down_fwd")

    def final_fn(i, j, x1v, dn, tg, gaf, gf):
        x2 = x1v + gaf * dn
        r = _rms(x2)
        xh = x2 * r
        e = xh * gf - tg
        dy = e * (1.0 / d)
        dx2 = _rms_bwd(xh, r, dy * gf)
        return dx2, dx2 * gaf, _colsum(e * e) * (0.5 / d), _colsum(dy * xh), _colsum(dx2 * dn)

    dx2, ddown, loss_cols, g_gfinal, d_ga_f = _rowwise(
        final_fn, [_row(x1), _row(down), _row(tgt), _bc(ga_f), _bc(g_final2)],
        [("row", F32), ("row", BF16), ("acc", 1), ("acc", 1), ("acc", 1)], nr=s_len // tr, tr=tr, nc=1, cw=d,
        name="final_bwd")
    loss = lax.psum(jnp.sum(loss_cols), ("x", "y", "c"))

    dup = _mm(ddown, w_down_f, tb=True, out_fn=lambda r, upv: r * (2.0 * jnp.maximum(upv, 0.0)), extras=[up],
              out_dtype=BF16, name="down_bwd_x")
    gw_down = _mm(up, ddown, ta=True, a_fn=relu2, out_dtype=BF16, name="down_bwd_w")
    sent = dict(w_down=_scatter_start(gw_down, 0, my, "rs_start_w_down"))
    dh2 = _mm(dup, w_up_f, tb=True, name="up_bwd_x")
    gw_up = _mm(h2, dup, ta=True, out_dtype=BF16, name="up_bwd_w")
    sent["w_up"] = _scatter_start(gw_up, 1, my, "rs_start_w_up")
    sc_f_bwd = _tok(sc_f, sent["w_down"][0][4], sent["w_up"][0][4])

    def norm2_bwd(i, j, x1v, mixv, dh2v, dx2v, gaa, gm, scf):
        r = _rms(x1v)
        xh = x1v * r
        n = xh * gm
        dn = dh2v * (1.0 + scf)
        dx1 = dx2v + _rms_bwd(xh, r, dn * gm)
        return dx1, dx1 * gaa, _colsum(dh2v), _colsum(dh2v * n), _colsum(dn * xh), _colsum(dx1 * mixv)

    dx1, dmix, d_sh_f, d_sc_f, g_gmlp, d_ga_a = _rowwise(
        norm2_bwd, [_row(x1), _row(mix), _row(dh2), _row(dx2), _bc(ga_a), _bc(g_mlp), _bc(sc_f_bwd)],
        [("row", F32), ("row", BF16), ("acc", 1), ("acc", 1), ("acc", 1), ("acc", 1)], nr=s_len // tr, tr=tr, nc=1,
        cw=d, name="norm2_bwd")

    du = _mm(dmix, w_out_f, tb=True, name="out_bwd_x")
    gw_out = _mm(u, dmix, ta=True, out_dtype=BF16, name="out_bwd_w")
    sent["w_out"] = _scatter_start(gw_out, 0, my, "rs_start_w_out")

    def merge_bwd(i, j, duv, ya, yr, gla, glr):
        ga, gr = _sigmoid(gla), _sigmoid(glr)
        return duv * ga, duv * gr, duv * ya * ga * (1.0 - ga), duv * yr * gr * (1.0 - gr)

    dya, dyr, dgla, dglr = _rowwise(
        merge_bwd, [_row(du), _row(y_attn), _row(y_rnn), _row(proj, glb, lb), _row(proj, glb + 2, lb)],
        [("row", BF16)] * 4, nr=s_len // tr, tr=tr, nc=2, cw=hw, name="merge_bwd")
    dattn = _mm(dya, w_oa_f, tb=True, out_dtype=BF16, name="o_attn_bwd_x")
    gw_oa = _mm(attn_o, dya, ta=True, out_dtype=BF16, name="o_attn_bwd_w")
    drg = _mm(dyr, w_or_f, tb=True, out_dtype=BF16, name="o_rnn_bwd_x")
    gw_or = _mm(rg, dyr, ta=True, out_dtype=BF16, name="o_rnn_bwd_w")
    sent["w_o_attn"] = _scatter_start(gw_oa, 0, my, "rs_start_w_o_attn")
    sent["w_o_rnn"] = _scatter_start(gw_or, 0, my, "rs_start_w_o_rnn")
    conv_b_bwd = _tok(conv_b, sent["w_out"][0][4], sent["w_o_attn"][0][4], sent["w_o_rnn"][0][4])

    dq, dk, dv = _attn_bwd(q, k_all, v_all, dattn, tq=tq, name="attn_bwd")
    dxr, dxg, dvec, g_wrg, g_wig = _rglru_bwd(proj, xr_blk, xg_blk, shift_masks, h_fwd, h_bwd, drg, conv_w_f, conv_b_bwd, w_rg[0], w_ig[0], b_rg_f,
                                               b_ig_f, lam_f, l_len=l_len, name="rglru_bwd")
    gate_part = _pack_flat([g_wrg, g_wig], N_DEV * SUBLANES)
    gate_sent = _scatter_start(gate_part, 0, my, "rs_start_gates")
    q_gain_bwd = _tok(q_gain, gate_sent[0][4])

    dq_p, g_qgain = _rowwise(lambda i, j, xv, dov, gain, ce, se: _qk_bwd_heads(xv, dov, gain, ce, se),
                             [_row(proj, 0, lb, aw), _row(dq, 0, 0, aw), _full(q_gain_bwd), _row(cos_e, 0, lb, HEAD_DIM),
                              _row(sin_e, 0, lb, HEAD_DIM)],
                             [("row", BF16, aw), ("acc", 1, HEAD_DIM)], nr=s_len // tr, tr=tr, nc=1, cw=aw, name="q_bwd")

    def kv_bwd(i, j, xv, dkv, dvv, gain, ce, se):
        dk_p, dgain = _qk_bwd_heads(xv, dkv, gain, ce, se)
        return dk_p, dvv, dgain

    dk_p, dv_p, g_kgain = _rowwise(kv_bwd, [_row(proj, kb, 0, kvw), _row(dk, 0, 0, kvw), _row(dv, 0, 0, kvw),
                                            _full(k_gain), _row(cos_e, 0, 0, HEAD_DIM), _row(sin_e, 0, 0, HEAD_DIM)],
                                   [("row", BF16, kvw), ("row", BF16, kvw), ("acc", 1, HEAD_DIM)], nr=t_len // tr, tr=tr,
                                   nc=1, cw=kvw, name="kv_bwd")

    def dproj_fn(i, j, dqv, dkv, dvv, dxrv, dxgv, dgav, dgrv):
        zero_ctx = lambda a: jnp.where(i < lb, jnp.zeros_like(a), a)
        return jnp.concatenate([zero_ctx(dqv), dkv, dvv, dxrv, dxgv, zero_ctx(dgav), zero_ctx(dgrv)], axis=1)

    dproj = _rowwise(dproj_fn, [_row(dq_p, 0, -lb, aw), _row(dk_p, 0, 0, kvw), _row(dv_p, 0, 0, kvw), _row(dxr, 0, 0, d),
                                _row(dxg, 0, 0, d), _row(dgla, 0, -lb, d), _row(dglr, 0, -lb, d)],
                     [("row", BF16, n_in)], nr=t_len // tr, tr=tr, nc=1, cw=n_in, name="dproj_pack")[0]
    dh_all = _mm(dproj, w_in_f, tb=True, name="proj_bwd_x")
    g_mix_bwd = g_mix

    def norm1_bwd(i, j, xv, dhv, dresv, g, sc):
        r = _rms(xv)
        xh = xv * r
        dn = dhv * (1.0 + sc)
        return dresv + _rms_bwd(xh, r, dn * g), _colsum(dhv), _colsum(dhv * (xh * g)), _colsum(dn * xh)

    def norm1_bwd_ctx(i, j, xv, dhv, g, sc):
        xh = xv * _rms(xv)
        return _colsum(dhv), _colsum(dhv * (xh * g)), _colsum(dhv * (1.0 + sc) * xh)

    dx_lat, d_sh_a, d_sc_a, g_gmix_lat = _rowwise(
        norm1_bwd, [_row(xs), _row(dh_all, 0, lb), _row(dx1), _bc(g_mix_bwd), _bc(sc_a)],
        [("row", F32)] + [("acc", 1)] * 3, nr=s_len // tr, tr=tr, nc=1, cw=d, name="norm1_bwd")
    d_csh, d_csc, g_gmix_ctx = _rowwise(
        norm1_bwd_ctx, [_row(ctxs), _row(dh_all), _bc(g_mix_bwd), _bc(csc_a)],
        [("acc", 1)] * 3, nr=lb, tr=tr, nc=1, cw=d, name="norm1_bwd_ctx")
    g_gmix = g_gmix_lat + g_gmix_ctx
    grad_x = dx_lat[None]

    dmod_lat = jnp.concatenate([d_sh_a, d_sc_a, d_ga_a, d_sh_f, d_sc_f, d_ga_f], axis=1)
    dmod_ctx = jnp.concatenate([d_csh, d_csc, jnp.zeros((1, (N_MOD - 2) * d), F32)], axis=1)
    dm_g = _all_gather([jnp.concatenate([dmod_lat, dmod_ctx, jnp.zeros((SUBLANES - 2, N_MOD * d), F32)], axis=0)], [0],
                       "ag_dmod")[0]
    dm16 = jnp.concatenate([dm_g[0::SUBLANES], dm_g[1::SUBLANES]], axis=0)
    dm_sh = lax.dynamic_slice(dm16, (0, my * nmod), (2 * N_DEV, nmod))
    gw_mod = _mm(c16, dm_sh, ta=True, a_fn=_silu, name="mod_bwd_w")
    dc16 = _mm(dm_sh, w_mod[0], tb=True, name="mod_bwd_x")

    def cctx_fn(i, j, pv, cc, dl, dc):
        row = lax.broadcasted_iota(jnp.int32, pv.shape, 0)
        return _colsum(jnp.where(row >= N_DEV, pv, 0.0)) * _dsilu(cc), dl + dc

    g_cctx, g_bmod = _rowwise(cctx_fn, [_full(dc16), _full(c_ctx2), _full(dmod_lat), _full(dmod_ctx)],
                              [("row", F32, d), ("row", F32, N_MOD * d)], nr=1, tr=1, nc=1, cw=d, name="cctx_bwd")

    rep_part = dict(c_ctx=g_cctx, b_mod=g_bmod, g_mix=g_gmix, g_mlp=g_gmlp, q_gain=g_qgain, k_gain=g_kgain,
                    conv_b=dvec[4:5], g_final=g_gfinal)
    rep_small = [n for n in REPLICATED if n in rep_part]
    gates = [n for n in REPLICATED if n not in rep_part]
    part_pack = _pack_flat([rep_part[n] for n in rep_small], PACK_ROWS)
    rep_rows = part_pack.shape[0]
    gate_land = _xfer_wait("scatter", gate_sent[0], 0, g_bmod, "rs_wait_gates")
    gate_sum = _sum_parts(gate_sent[1], gate_land, name="gate_sum")
    (gate_full, parts), order = _all_gather([gate_sum, part_pack], [0, 0], "ag_rep_grads", with_token=True)
    small_g = jnp.concatenate([dvec[0:4], dvec[5:11], jnp.zeros((SMALL_ROWS - 10, d), F32)], axis=0)
    (small_recv,), order = _all_to_all([_tok(small_g, order)], [1], "a2a_small_grads")

    halves = {}
    for part, lo in (("top", 0), ("bot", d // 2)):
        gw_part = _mm(h_all, dproj, ta=True, a_cols=(lo, d // 2), out_dtype=BF16, name="proj_bwd_w_" + part,
                      bias=_tok(jnp.zeros((1, n_in), F32), order))
        halves[part] = _chip_scatter_start(_pair_reduce(gw_part, my, "pair_w_in_" + part), my, "rs_start_w_in_" + part)
        order = halves[part][0][4]
    started = order

    res = {}
    for names, pr, mult, nm in ((rep_small, parts.reshape(N_DEV, rep_rows, LANES), PACK_ROWS, "adamw_rep"),
                                (gates, gate_full[None], N_DEV * SUBLANES, "adamw_gates")):
        rep_out = _adamw(pr, _tok(_pack_flat([wd[n] for n in names], mult), started),
                         _pack_flat([md[n] for n in names], mult), _pack_flat([vd[n] for n in names], mult), name=nm)
        for kind, packed in zip(("g", "d", "m", "v"), rep_out):
            for n, a in zip(names, _unpack_flat(packed, [wd[n].shape for n in names])):
                res[kind, n] = a
    out = _adamw(small_recv, _tok(sp_w, started), _pack_small(md), _pack_small(vd), name="adamw_small")
    for kind, a in zip(("g", "d", "m", "v"), out):
        for n, piece in _unpack_small(a).items():
            res[kind, n] = piece

    after = started
    for n, (handle, own, ax, kind) in sent.items():
        land = _xfer_wait(kind, handle, ax, after, "rs_wait_" + n)
        out = _adamw(land, wd[n][0], md[n][0], vd[n][0], own=own, name="adamw_" + n)
        after = out[0]
        for kind, a in zip(("g", "d", "m", "v"), out):
            res[kind, n] = a[None]
    out = _adamw(gw_mod[None], w_mod[0], m_w_mod[0], v_w_mod[0], name="adamw_w_mod")
    after = out[0]
    for kind, a in zip(("g", "d", "m", "v"), out):
        res[kind, 'w_mod'] = a[None]
    lands, owns = [], []
    for part, (handle, own, ax, kind) in halves.items():
        lands.append(_xfer_wait(kind, handle, ax, after, "rs_wait_w_in_" + part))
        owns.append(own)
        after = lands[-1]
    out = _adamw(jnp.concatenate(lands, axis=1), w_in[0], m_w_in[0], v_w_in[0], own=jnp.concatenate(owns, axis=0),
                 name="adamw_w_in")
    for kind, a in zip(("g", "d", "m", "v"), out):
        res[kind, 'w_in'] = a[None]

    outs = [loss, grad_x]
    for kind in ("g", "d", "m", "v"):
        outs += [res[kind, n].reshape(wd[n].shape) for n in WEIGHTS]
    return tuple(outs)
```

```python
import math

import jax
import jax.numpy as jnp
from jax import lax
from jax.experimental import pallas as pl
from jax.experimental.pallas import tpu as pltpu

F32 = jnp.float32
BF16 = jnp.bfloat16
MESH = pl.DeviceIdType.MESH

N_DEV = 8
HEAD_DIM = 128
Q_GROUP = 4
RNN_BLOCK = 128
GRID_W = 64
CONV_WIDTH = 4
LRU_C = 8.0
ROPE_THETA = 10000.0
NORM_EPS = 1e-6
N_MOD = 6
ADAM_LR = 0.001
ADAM_B1 = 0.9
ADAM_B2 = 0.999
ADAM_EPS = 1e-08
ADAM_WD = 0.01
ADAM_STEP = 10

LANES = 128
SUBLANES = 8
VMEM_LIMIT = 56 * 1024 * 1024
HBM_SPEC = pl.BlockSpec(memory_space=pltpu.HBM)


def _params(sem=None, vmem=VMEM_LIMIT):
    kw = dict(vmem_limit_bytes=vmem)
    if sem is not None:
        kw["dimension_semantics"] = sem
    return pltpu.CompilerParams(**kw)


def _pick(dim, target, mult):
    for t in range(min(dim, target), 0, -1):
        if dim % t == 0 and t % mult == 0:
            return t
    return dim


def _my_pos():
    return lax.axis_index("x"), lax.axis_index("y"), lax.axis_index("c")


def _flip(pos, d):
    x, y, c = pos
    return (1 - x if d & 4 else x, 1 - y if d & 2 else y, 1 - c if d & 1 else c)


def _lin(pos):
    return 4 * pos[0] + 2 * pos[1] + pos[2]


def _slab(ref, axis, idx, size):
    if axis == 0:
        return ref.at[pl.ds(idx * size, size)]
    assert axis == 1
    return ref.at[:, pl.ds(idx * size, size)]


def _all_gather(arrs, axes, name, with_token=False):
    n = len(arrs)
    sizes = [a.shape[ax] for a, ax in zip(arrs, axes)]

    def body(*refs):
        ins, outs, token = refs[:n], refs[n:2 * n], refs[2 * n]
        send_sems, recv_sems, local_sems = refs[2 * n + 1:]
        token[...] = jnp.zeros_like(token)
        me = _my_pos()
        sibling = _flip(me, 1)
        chips = [2, 4, 6]

        def copy(a, k, block_pos, to, src=None):
            dst = _slab(outs[a], axes[a], _lin(block_pos), sizes[a])
            return pltpu.make_async_remote_copy(
                src_ref=dst if src is None else src, dst_ref=dst,
                send_sem=send_sems.at[7 * a + k], recv_sem=recv_sems.at[7 * a + k],
                device_id=to, device_id_type=MESH)

        local, first, passed = [], [], []
        for a in range(n):
            cp = pltpu.make_async_copy(ins[a], _slab(outs[a], axes[a], _lin(me), sizes[a]), local_sems.at[a])
            cp.start()
            local.append(cp)
        for a in range(n):
            f = [copy(a, 0, me, sibling, src=ins[a])]
            f += [copy(a, 1 + j, me, _flip(me, m), src=ins[a]) for j, m in enumerate(chips)]
            for cp in f:
                cp.start()
            first += f
        for a in range(n):
            for j, m in enumerate(chips):
                copy(a, 1 + j, _flip(me, m), me).wait_recv()
                cp = copy(a, 4 + j, _flip(me, m), sibling)
                cp.start()
                passed.append(cp)
        for a in range(n):
            copy(a, 0, sibling, me).wait_recv()
            for j, m in enumerate(chips):
                copy(a, 4 + j, _flip(sibling, m), me).wait_recv()
        for cp in first + passed:
            cp.wait_send()
        for cp in local:
            cp.wait()

    out_shape = []
    for a, ax in zip(arrs, axes):
        shp = list(a.shape)
        shp[ax] *= N_DEV
        out_shape.append(jax.ShapeDtypeStruct(tuple(shp), a.dtype))
    out_shape.append(jax.ShapeDtypeStruct((SUBLANES, LANES), F32))
    res = pl.pallas_call(
        body, name=name, out_shape=tuple(out_shape),
        in_specs=[HBM_SPEC] * n, out_specs=tuple([HBM_SPEC] * n + [pl.BlockSpec(memory_space=pltpu.VMEM)]),
        scratch_shapes=[pltpu.SemaphoreType.DMA((7 * n,)), pltpu.SemaphoreType.DMA((7 * n,)),
                        pltpu.SemaphoreType.DMA((n,))],
    )(*arrs)
    return (list(res[:n]), res[n]) if with_token else list(res[:n])


def _gather_proj(h_all, w_shard, name):
    t_len, kdim = h_all.shape
    sz = w_shard.shape[1]
    tm = _pick(t_len, 1024, 16)
    n_m = t_len // tm

    def body(h_ref, w_ref, wfull_ref, proj_ref, token, a_buf, b_buf, o_buf, send_sems, recv_sems, io_sems, out_sems):
        token[...] = jnp.zeros_like(token)
        me = _my_pos()
        sibling = _flip(me, 1)
        chips = [2, 4, 6]
        slab = lambda pos: _slab(wfull_ref, 1, _lin(pos), sz)

        def copy(k, block_pos, to, src=None):
            return pltpu.make_async_remote_copy(
                src_ref=slab(block_pos) if src is None else src, dst_ref=slab(block_pos),
                send_sem=send_sems.at[k], recv_sem=recv_sems.at[k], device_id=to, device_id_type=MESH)

        pending = []

        def block_product(pos, b_src):
            load = pltpu.make_async_copy(b_src, b_buf, io_sems.at[1])
            load.start()
            load.wait()
            for i in range(n_m):
                slot = len(pending) % 2
                if len(pending) >= 2:
                    pending[-2].wait()
                o_buf[slot] = _dot(a_buf[pl.ds(i * tm, tm), :], b_buf[...], 1, 0)
                out = pltpu.make_async_copy(
                    o_buf.at[slot], proj_ref.at[pl.ds(i * tm, tm), pl.ds(_lin(pos) * sz, sz)], out_sems.at[slot])
                out.start()
                pending.append(out)

        local = pltpu.make_async_copy(w_ref, slab(me), io_sems.at[0])
        local.start()
        first = [copy(0, me, sibling, src=w_ref)] + [copy(1 + j, me, _flip(me, m), src=w_ref) for j, m in enumerate(chips)]
        for cp in first:
            cp.start()
        load_a = pltpu.make_async_copy(h_ref, a_buf, io_sems.at[2])
        load_a.start()
        load_a.wait()
        block_product(me, w_ref)
        copy(0, sibling, me).wait_recv()
        block_product(sibling, slab(sibling))
        passed = []
        for j, m in enumerate(chips):
            src_pos = _flip(me, m)
            copy(1 + j, src_pos, me).wait_recv()
            cp = copy(4 + j, src_pos, sibling)
            cp.start()
            passed.append(cp)
            block_product(src_pos, slab(src_pos))
        for j, m in enumerate(chips):
            src_pos = _flip(sibling, m)
            copy(4 + j, src_pos, me).wait_recv()
            block_product(src_pos, slab(src_pos))
        for out in pending[-2:]:
            out.wait()
        for cp in first + passed:
            cp.wait_send()
        local.wait()

    return pl.pallas_call(
        body, name=name,
        out_shape=(jax.ShapeDtypeStruct((kdim, N_DEV * sz), w_shard.dtype), jax.ShapeDtypeStruct((t_len, N_DEV * sz), F32),
                   jax.ShapeDtypeStruct((SUBLANES, LANES), F32)),
        in_specs=[HBM_SPEC, HBM_SPEC], out_specs=(HBM_SPEC, HBM_SPEC, pl.BlockSpec(memory_space=pltpu.VMEM)),
        scratch_shapes=[pltpu.VMEM((t_len, kdim), h_all.dtype), pltpu.VMEM((kdim, sz), w_shard.dtype),
                        pltpu.VMEM((2, tm, sz), F32), pltpu.SemaphoreType.DMA((7,)), pltpu.SemaphoreType.DMA((7,)),
                        pltpu.SemaphoreType.DMA((3,)), pltpu.SemaphoreType.DMA((2,))],
        compiler_params=_params(),
    )(h_all, w_shard)


def _all_to_all(arrs, axes, name):
    n = len(arrs)
    sizes = [a.shape[ax] // N_DEV for a, ax in zip(arrs, axes)]

    def body(*refs):
        ins, outs, token = refs[:n], refs[n:2 * n], refs[2 * n]
        send_sems, recv_sems, local_sems = refs[2 * n + 1:]
        token[...] = jnp.zeros_like(token)
        me = _my_pos()
        my = _lin(me)
        copies, local = [], []
        for a in range(n):
            cp = pltpu.make_async_copy(_slab(ins[a], axes[a], my, sizes[a]), outs[a].at[my], local_sems.at[a])
            cp.start()
            local.append(cp)
        for d in range(1, N_DEV):
            peer = _flip(me, d)
            for a in range(n):
                cp = pltpu.make_async_remote_copy(
                    src_ref=_slab(ins[a], axes[a], _lin(peer), sizes[a]), dst_ref=outs[a].at[my],
                    send_sem=send_sems.at[7 * a + d - 1], recv_sem=recv_sems.at[7 * a + d - 1],
                    device_id=peer, device_id_type=MESH)
                cp.start()
                copies.append((a, d, cp))
        for a, d, cp in copies:
            peer = _flip(me, d)
            pltpu.make_async_remote_copy(
                src_ref=_slab(ins[a], axes[a], _lin(peer), sizes[a]), dst_ref=outs[a].at[_lin(peer)],
                send_sem=send_sems.at[7 * a + d - 1], recv_sem=recv_sems.at[7 * a + d - 1],
                device_id=peer, device_id_type=MESH).wait_recv()
            cp.wait_send()
        for cp in local:
            cp.wait()

    out_shape = []
    for a, ax, s in zip(arrs, axes, sizes):
        shp = list(a.shape)
        shp[ax] = s
        out_shape.append(jax.ShapeDtypeStruct((N_DEV,) + tuple(shp), a.dtype))
    out_shape.append(jax.ShapeDtypeStruct((SUBLANES, LANES), F32))
    res = pl.pallas_call(
        body, name=name, out_shape=tuple(out_shape),
        in_specs=[HBM_SPEC] * n, out_specs=tuple([HBM_SPEC] * n + [pl.BlockSpec(memory_space=pltpu.VMEM)]),
        scratch_shapes=[pltpu.SemaphoreType.DMA((7 * n,)), pltpu.SemaphoreType.DMA((7 * n,)),
                        pltpu.SemaphoreType.DMA((n,))],
    )(*arrs)
    return list(res[:n]), res[n]


SEM_SPEC = pl.BlockSpec(memory_space=pltpu.SEMAPHORE)
_EFFECT = pltpu.SideEffectType.DATAFLOW_SIDE_EFFECTING
N_PEERS = N_DEV - 1


def _split_copies(kind, src_ref, land_ref, send_sems, recv_sems, axis, size, arriving):
    me = _my_pos()
    out = []
    if kind == "sibling":
        sibling = _flip(me, 1)
        for k in range(N_CHIPS):
            blk = _slab(land_ref, axis, 2 * k + (sibling[2] if arriving else me[2]), size)
            out.append(pltpu.make_async_remote_copy(src_ref=blk, dst_ref=blk, send_sem=send_sems.at[k],
                                                    recv_sem=recv_sems.at[k], device_id=sibling, device_id_type=MESH))
        return out
    for dd in (range(2, N_DEV, 2) if kind in ("chips", "gather_chips") else range(1, N_DEV)):
        peer = _flip(me, dd)
        k = dd - 1
        if kind in ("gather", "gather_chips"):
            src = _slab(land_ref, axis, _lin(me), size)
            dst = _slab(land_ref, axis, _lin(peer if arriving else me), size)
        elif kind == "scatter":
            src, dst = _slab(src_ref, axis, _lin(peer), size), land_ref.at[k]
        else:
            k = dd // 2 - 1
            src, dst = src_ref.at[2 * peer[0] + peer[1]], land_ref.at[k]
        out.append(pltpu.make_async_remote_copy(
            src_ref=src, dst_ref=dst, send_sem=send_sems.at[k], recv_sem=recv_sems.at[k],
            device_id=peer, device_id_type=MESH))
    return out


N_CHIPS = N_DEV // 2


def _split_size(kind, src_shape, land_shape, axis):
    if kind in ("gather", "gather_chips", "sibling"):
        return land_shape[axis] // N_DEV
    return src_shape[axis] // {"scatter": N_DEV, "chips": N_CHIPS}[kind]


def _xfer_start(kind, src, land, axis, name):
    size = _split_size(kind, src.shape, land.shape, axis)

    def body(src_ref, land_ref, send_sems, recv_sems, src_thru, land_thru, token):
        for cp in _split_copies(kind, src_ref, land_ref, send_sems, recv_sems, axis, size, False):
            cp.start()
        token[...] = jnp.zeros_like(token)

    return pl.pallas_call(
        body, name=name,
        out_shape=(pltpu.SemaphoreType.DMA((N_PEERS,)), pltpu.SemaphoreType.DMA((N_PEERS,)),
                   pltpu.HBM(src.shape, src.dtype), pltpu.HBM(land.shape, land.dtype),
                   jax.ShapeDtypeStruct((SUBLANES, LANES), F32)),
        in_specs=(HBM_SPEC, HBM_SPEC),
        out_specs=(SEM_SPEC, SEM_SPEC, HBM_SPEC, HBM_SPEC, pl.BlockSpec(memory_space=pltpu.VMEM)),
        input_output_aliases={0: 2, 1: 3},
        compiler_params=pltpu.CompilerParams(has_side_effects=_EFFECT),
    )(pltpu.with_memory_space_constraint(src, pltpu.HBM), pltpu.with_memory_space_constraint(land, pltpu.HBM))


def _xfer_wait(kind, handle, axis, after, name):
    send_sems, recv_sems, src_thru, land_thru, _ = handle
    size = _split_size(kind, src_thru.shape, land_thru.shape, axis)

    def body(src_ref, land_ref, send_sems, recv_sems, after_ref, src_dead, got_ref):
        for cp in _split_copies(kind, src_ref, land_ref, send_sems, recv_sems, axis, size, True):
            cp.wait_send()
            cp.wait_recv()

    return pl.pallas_call(
        body, name=name,
        out_shape=(pltpu.HBM(src_thru.shape, src_thru.dtype), pltpu.HBM(land_thru.shape, land_thru.dtype)),
        in_specs=(HBM_SPEC, HBM_SPEC, SEM_SPEC, SEM_SPEC, pl.BlockSpec(memory_space=pl.ANY)),
        out_specs=(HBM_SPEC, HBM_SPEC), input_output_aliases={0: 0, 1: 1},
        compiler_params=pltpu.CompilerParams(has_side_effects=_EFFECT),
    )(src_thru, land_thru, send_sems, recv_sems, after)[1]


def _place_block(shard, axis, my, name):
    r, c = shard.shape
    tr = _pick(r, max(16, (1 << 19) // c), 16)
    nr = r // tr
    if axis == 0:
        full, out_map = (r * N_DEV, c), (lambda i, my_ref: (my_ref[0] * nr + i, 0))
    else:
        full, out_map = (r, c * N_DEV), (lambda i, my_ref: (i, my_ref[0]))

    def body(my_ref, x_ref, o_ref):
        o_ref[...] = x_ref[...].astype(BF16)

    return pl.pallas_call(
        body, name=name, out_shape=jax.ShapeDtypeStruct(full, BF16),
        grid_spec=pltpu.PrefetchScalarGridSpec(
            num_scalar_prefetch=1, grid=(nr,), in_specs=[pl.BlockSpec((tr, c), lambda i, my_ref: (i, 0))],
            out_specs=pl.BlockSpec((tr, c), out_map)),
        compiler_params=_params(("arbitrary",)),
    )(jnp.reshape(my, (1,)).astype(jnp.int32), shard)


def _scatter_start(grad, axis, my, name):
    shp = list(grad.shape)
    shp[axis] //= N_DEV
    own = lax.dynamic_slice_in_dim(grad, my * shp[axis], shp[axis], axis)
    return (_xfer_start("scatter", grad, lax.empty((N_PEERS,) + tuple(shp), grad.dtype), axis, name), own, axis,
            "scatter")


def _pair_reduce(grad, my, name):
    r, cols = grad.shape
    sz = cols // N_DEV

    def exchange(g_ref, got_ref, send_sems, recv_sems):
        me = _my_pos()
        sibling = _flip(me, 1)
        cps = []
        for k in range(N_CHIPS):
            cp = pltpu.make_async_remote_copy(
                src_ref=_slab(g_ref, 1, 2 * k + sibling[2], sz), dst_ref=_slab(got_ref, 1, k, sz),
                send_sem=send_sems.at[k], recv_sem=recv_sems.at[k], device_id=sibling, device_id_type=MESH)
            cp.start()
            cps.append(cp)
        for cp in cps:
            cp.wait()

    got = pl.pallas_call(
        exchange, name=name + "_swap", out_shape=jax.ShapeDtypeStruct((r, N_CHIPS * sz), grad.dtype),
        in_specs=[HBM_SPEC], out_specs=HBM_SPEC,
        scratch_shapes=[pltpu.SemaphoreType.DMA((N_CHIPS,)), pltpu.SemaphoreType.DMA((N_CHIPS,))],
    )(grad)

    tr = _pick(r, 512, 16)

    def add(c_ref, g_ref, p_ref, o_ref):
        o_ref[...] = (g_ref[...].astype(F32) + p_ref[...].astype(F32)).astype(o_ref.dtype)

    def add(c_ref, g_ref, p_ref, o_ref):
        o_ref[0] = (g_ref[...].astype(F32) + p_ref[...].astype(F32)).astype(o_ref.dtype)

    return pl.pallas_call(
        add, name=name + "_sum", out_shape=jax.ShapeDtypeStruct((N_CHIPS, r, sz), grad.dtype),
        grid_spec=pltpu.PrefetchScalarGridSpec(
            num_scalar_prefetch=1, grid=(N_CHIPS, r // tr),
            in_specs=[pl.BlockSpec((tr, sz), lambda k, i, c_ref: (i, 2 * k + c_ref[0])),
                      pl.BlockSpec((tr, sz), lambda k, i, c_ref: (i, k))],
            out_specs=pl.BlockSpec((1, tr, sz), lambda k, i, c_ref: (k, i, 0))),
        compiler_params=_params(("parallel", "parallel")),
    )(jnp.reshape(my % 2, (1,)).astype(jnp.int32), grad, got)


def _chip_scatter_start(pair_sum, my, name):
    _, r, sz = pair_sum.shape
    own = lax.dynamic_index_in_dim(pair_sum, my // 2, 0, keepdims=False)
    return (_xfer_start("chips", pair_sum, lax.empty((N_CHIPS - 1, r, sz), pair_sum.dtype), 0, name), own, 0, "chips")


def _tok(a, *tokens):
    for t in tokens:
        a = a + t[0, 0].astype(a.dtype)
    return a


def _mm(a, b, *, name, ta=False, tb=False, out_dtype=F32, a_fn=None, out_fn=None, extras=(), bias=None,
        tm=1024, tn=1024, tk=2816, a_cols=None):
    a_blk0 = 0
    if ta:
        kdim, m = a.shape
        if a_cols is not None:
            m = a_cols[1]
    else:
        m, kdim = a.shape
    if tb:
        n, kb = b.shape
    else:
        kb, n = b.shape
    assert kb == kdim, (a.shape, b.shape, ta, tb)
    tm = _pick(m, tm, LANES if ta else SUBLANES)
    if a_cols is not None:
        assert ta and a_cols[0] % tm == 0
        a_blk0 = a_cols[0] // tm
    tn = _pick(n, tn, LANES)
    tk = _pick(kdim, tk, LANES)
    nk = kdim // tk
    ne = len(extras)
    has_bias = bias is not None

    def body(*refs):
        a_ref, b_ref = refs[0], refs[1]
        pos = 2
        bias_ref = None
        if has_bias:
            bias_ref = refs[pos]
            pos += 1
        extra_refs = refs[pos:pos + ne]
        o_ref = refs[pos + ne]
        av = a_ref[...]
        if a_fn is not None:
            av = a_fn(av)
        dn = (((0 if ta else 1,), (1 if tb else 0,)), ((), ()))
        part = lax.dot_general(av.astype(BF16), b_ref[...].astype(BF16), dn, preferred_element_type=F32)

        def finish(r):
            if has_bias:
                r = r + bias_ref[...]
            if out_fn is not None:
                r = out_fn(r, *[e[...] for e in extra_refs])
            o_ref[...] = r.astype(out_dtype)

        if nk == 1:
            finish(part)
            return
        acc = refs[pos + ne + 1]
        k = pl.program_id(2)

        @pl.when(k == 0)
        def _():
            acc[...] = part

        @pl.when(k > 0)
        def _():
            acc[...] += part

        @pl.when(k == nk - 1)
        def _():
            finish(acc[...])

    in_specs = [
        pl.BlockSpec((tk, tm), lambda i, j, k: (k, i + a_blk0)) if ta else pl.BlockSpec((tm, tk), lambda i, j, k: (i, k)),
        pl.BlockSpec((tn, tk), lambda i, j, k: (j, k)) if tb else pl.BlockSpec((tk, tn), lambda i, j, k: (k, j)),
    ]
    args = [a, b]
    if has_bias:
        in_specs.append(pl.BlockSpec((1, tn), lambda i, j, k: (0, j)))
        args.append(bias)
    for e in extras:
        in_specs.append(pl.BlockSpec((tm, tn), lambda i, j, k: (i, j)))
        args.append(e)
    return pl.pallas_call(
        body, name=name, grid=(m // tm, n // tn, nk),
        in_specs=in_specs, out_specs=pl.BlockSpec((tm, tn), lambda i, j, k: (i, j)),
        out_shape=jax.ShapeDtypeStruct((m, n), out_dtype),
        scratch_shapes=[pltpu.VMEM((tm, tn), F32)] if nk > 1 else [],
        compiler_params=_params(("parallel", "parallel", "arbitrary")),
    )(*args)


def _row(arr, cb=0, rb=0, w=None):
    return ("row", arr, cb, rb, w)


def _bc(arr, cb=0, w=None):
    return ("bc", arr, cb, 0, w)


def _full(arr):
    return ("full", arr, 0, 0, None)


def _rowwise(fn, ins, outs, *, nr, tr, nc, cw, name):
    n_in = len(ins)
    outs = [(o[0], o[1], (o[2] if len(o) > 2 and o[2] else cw)) for o in outs]

    def body(*refs):
        in_refs, out_refs = refs[:n_in], refs[n_in:]
        j, i = pl.program_id(0), pl.program_id(1)
        res = fn(i, j, *[r[...] for r in in_refs])
        if not isinstance(res, (tuple, list)):
            res = (res,)
        for (kind, _, _), o_ref, r in zip(outs, out_refs, res):
            if kind == "row":
                o_ref[...] = r.astype(o_ref.dtype)
            else:
                @pl.when(i == 0)
                def _(o_ref=o_ref):
                    o_ref[...] = jnp.zeros_like(o_ref)

                o_ref[...] += r

    in_specs, args = [], []
    for kind, arr, cb, rb, w in ins:
        w = w or cw
        if kind == "row":
            last = arr.shape[0] // tr - 1
            in_specs.append(pl.BlockSpec((tr, w), lambda j, i, cb=cb, rb=rb, last=last: (jnp.clip(i + rb, 0, last), j + cb)))
        elif kind == "bc":
            in_specs.append(pl.BlockSpec((arr.shape[0], w), lambda j, i, cb=cb: (0, j + cb)))
        else:
            in_specs.append(pl.BlockSpec(arr.shape, lambda j, i, nd=arr.ndim: (0,) * nd))
        args.append(arr)
    out_specs, out_shape = [], []
    for kind, spec, w in outs:
        if kind == "row":
            out_specs.append(pl.BlockSpec((tr, w), lambda j, i: (i, j)))
            out_shape.append(jax.ShapeDtypeStruct((nr * tr, nc * w), spec))
        else:
            out_specs.append(pl.BlockSpec((spec, w), lambda j, i: (0, j)))
            out_shape.append(jax.ShapeDtypeStruct((spec, nc * w), F32))
    res = pl.pallas_call(
        body, name=name, grid=(nc, nr), in_specs=in_specs, out_specs=tuple(out_specs), out_shape=tuple(out_shape),
        compiler_params=_params(("parallel", "arbitrary")),
    )(*args)
    return res


def _sum_parts(own, parts, *, name):
    p, r, c = parts.shape
    tr = _pick(r, max(SUBLANES, (1 << 19) // c), SUBLANES)

    def body(o_ref, p_ref, g_out):
        g = o_ref[...].astype(F32)
        for q in range(p):
            g = g + p_ref[q].astype(F32)
        g_out[...] = g

    blk = pl.BlockSpec((tr, c), lambda i: (i, 0))
    return pl.pallas_call(
        body, name=name, grid=(r // tr,), in_specs=[blk, pl.BlockSpec((p, tr, c), lambda i: (0, i, 0))],
        out_specs=blk, out_shape=jax.ShapeDtypeStruct((r, c), F32), compiler_params=_params(("parallel",)),
    )(own, parts)


def _adamw(parts, w, m, v, *, name, own=None):
    p, r, c = parts.shape
    tr = _pick(r, max(SUBLANES, (1 << 19) // c), 16 if parts.dtype == BF16 else SUBLANES)
    bc1 = 1.0 - ADAM_B1 ** ADAM_STEP
    bc2 = 1.0 - ADAM_B2 ** ADAM_STEP
    has_own = own is not None

    def body(*refs):
        p_ref, w_ref, m_ref, v_ref = refs[:4]
        g_out, d_out, m_out, v_out = refs[4 + has_own:]
        if has_own:
            g = refs[4][...].astype(F32) + p_ref[0].astype(F32)
        else:
            g = p_ref[0].astype(F32)
        for q in range(1, p):
            g = g + p_ref[q].astype(F32)
        m_new = ADAM_B1 * m_ref[...] + (1.0 - ADAM_B1) * g
        v_new = ADAM_B2 * v_ref[...] + (1.0 - ADAM_B2) * (g * g)
        m_hat = m_new / bc1
        v_hat = v_new / bc2
        g_out[...] = g
        d_out[...] = -ADAM_LR * (m_hat / (jnp.sqrt(v_hat) + ADAM_EPS) + ADAM_WD * w_ref[...])
        m_out[...] = m_new
        v_out[...] = v_new

    blk = pl.BlockSpec((tr, c), lambda i: (i, 0))
    return pl.pallas_call(
        body, name=name, grid=(r // tr,),
        in_specs=[pl.BlockSpec((p, tr, c), lambda i: (0, i, 0)), blk, blk, blk] + ([blk] if has_own else []),
        out_specs=(blk, blk, blk, blk), out_shape=tuple(jax.ShapeDtypeStruct((r, c), F32) for _ in range(4)),
        compiler_params=_params(("parallel",)),
    )(*([parts, w, m, v] + ([own] if has_own else [])))


_GELU_C = math.sqrt(2.0 / math.pi)
_GELU_K = 0.044715


def _sigmoid(z):
    return 0.5 * jnp.tanh(0.5 * z) + 0.5


def _silu(z):
    return z * _sigmoid(z)


def _dsilu(z):
    s = _sigmoid(z)
    return s * (1.0 + z * (1.0 - s))


def _gelu(z):
    return 0.5 * z * (1.0 + jnp.tanh(_GELU_C * (z + _GELU_K * z * z * z)))


def _gelu_pair(z):
    zz = z * z
    t = jnp.tanh(_GELU_C * (z * (1.0 + _GELU_K * zz)))
    half = 0.5 * (1.0 + t)
    return z * half, half + 0.5 * z * (1.0 - t * t) * (_GELU_C * (1.0 + 3.0 * _GELU_K * zz))


def _softplus(z):
    return jnp.maximum(z, 0.0) + jnp.log(1.0 + jnp.exp(-jnp.abs(z)))


def _rms(xv):
    return lax.rsqrt(jnp.mean(xv * xv, axis=-1, keepdims=True) + NORM_EPS)


def _rms_bwd(xhat, r, dxhat):
    return r * (dxhat - xhat * jnp.mean(dxhat * xhat, axis=-1, keepdims=True))


def _swap_pairs(y):
    lane = lax.broadcasted_iota(jnp.int32, y.shape, 1)
    return jnp.where(lane % 2 == 0, pltpu.roll(y, LANES - 1, 1), pltpu.roll(y, 1, 1))


def _dot(a, b, ca, cb):
    return lax.dot_general(a, b, (((ca,), (cb,)), ((), ())), preferred_element_type=F32)


_SM_SCALE = HEAD_DIM ** -0.5
_EXP2_SCALE = _SM_SCALE * math.log2(math.e)


def _softmax_parts(qh, kk):
    s = _dot(qh, kk, 1, 1)
    p = jnp.exp2((s - jnp.max(s, axis=-1, keepdims=True)) * _EXP2_SCALE)
    return p, 1.0 / jnp.sum(p, axis=-1, keepdims=True)


def _attn_fwd(q, k, v, *, tq, name):
    s_len, aw = q.shape
    t_len, kvw = k.shape
    gw = Q_GROUP * HEAD_DIM

    def body(q_ref, k_ref, v_ref, o_ref):
        kk, vv = k_ref[...], v_ref[...]
        for g in range(Q_GROUP):
            cols = slice(g * HEAD_DIM, (g + 1) * HEAD_DIM)
            p, inv_l = _softmax_parts(q_ref[:, cols], kk)
            o_ref[:, cols] = (_dot(p.astype(BF16), vv, 1, 0) * inv_l).astype(BF16)

    return pl.pallas_call(
        body, name=name, grid=(kvw // HEAD_DIM, s_len // tq),
        in_specs=[pl.BlockSpec((tq, gw), lambda h, i: (i, h)), pl.BlockSpec((t_len, HEAD_DIM), lambda h, i: (0, h)),
                  pl.BlockSpec((t_len, HEAD_DIM), lambda h, i: (0, h))],
        out_specs=pl.BlockSpec((tq, gw), lambda h, i: (i, h)),
        out_shape=jax.ShapeDtypeStruct((s_len, aw), BF16),
        compiler_params=_params(("parallel", "parallel")),
    )(q, k, v)


def _attn_bwd(q, k, v, do, *, tq, name):
    s_len, aw = q.shape
    t_len, kvw = k.shape
    gw = Q_GROUP * HEAD_DIM

    def body(q_ref, k_ref, v_ref, do_ref, dq_ref, dk_ref, dv_ref):
        @pl.when(pl.program_id(1) == 0)
        def _():
            dk_ref[...] = jnp.zeros_like(dk_ref)
            dv_ref[...] = jnp.zeros_like(dv_ref)

        kk, vv = k_ref[...], v_ref[...]
        for g in range(Q_GROUP):
            cols = slice(g * HEAD_DIM, (g + 1) * HEAD_DIM)
            qh, doh = q_ref[:, cols], do_ref[:, cols]
            p, inv_l = _softmax_parts(qh, kk)
            dp = _dot(doh, vv, 1, 1)
            delta = jnp.sum(p * dp, axis=-1, keepdims=True) * inv_l
            dsb = (p * ((dp - delta) * (inv_l * _SM_SCALE))).astype(BF16)
            dq_ref[:, cols] = _dot(dsb, kk, 1, 0)
            dk_ref[...] += _dot(dsb, qh, 0, 0)
            dv_ref[...] += _dot(p.astype(BF16), (doh * inv_l).astype(BF16), 0, 0)

    qspec = pl.BlockSpec((tq, gw), lambda h, i: (i, h))
    kspec = pl.BlockSpec((t_len, HEAD_DIM), lambda h, i: (0, h))
    return pl.pallas_call(
        body, name=name, grid=(kvw // HEAD_DIM, s_len // tq),
        in_specs=[qspec, kspec, kspec, qspec], out_specs=(qspec, kspec, kspec),
        out_shape=(jax.ShapeDtypeStruct((s_len, aw), F32), jax.ShapeDtypeStruct((t_len, kvw), F32),
                   jax.ShapeDtypeStruct((t_len, kvw), F32)),
        compiler_params=_params(("parallel", "arbitrary")),
    )(q, k, v, do)


SHIFTS = (-2, -1, 1, 2)


def _shift_masks(l_len, t_len):
    t = jnp.arange(t_len)[:, None]
    cols = []
    for d in SHIFTS:
        td = t + d
        ok = ((t < l_len) & (td >= 0) & (td < l_len)) | ((t >= l_len) & (td >= l_len) & (td < t_len))
        cols.append(jnp.broadcast_to(ok.astype(F32), (t_len, RNN_BLOCK)))
    return jnp.concatenate(cols, axis=1)


def _shift_rows(xv, d, m_ref, l_len, t_len):
    if d == 0:
        return xv
    k = SHIFTS.index(d)
    return pltpu.roll(xv, (-d) % t_len, 0) * m_ref[:, k * RNN_BLOCK:(k + 1) * RNN_BLOCK]


def _sublane_scan(a8, b8, rev):
    row = lax.broadcasted_iota(jnp.int32, a8.shape, 0)
    for s in (1, 2, 4):
        if rev:
            ok = row < SUBLANES - s
            sh = SUBLANES - s
        else:
            ok = row >= s
            sh = s
        a_sh = jnp.where(ok, pltpu.roll(a8, sh, 0), 1.0)
        b_sh = jnp.where(ok, pltpu.roll(b8, sh, 0), 0.0)
        b8 = a8 * b_sh + b8
        a8 = a8 * a_sh
    return a8, b8


def _scan_chunks(chains, c0, n, carries):
    def step(kk, hs):
        new = []
        for (a_ref, b_ref, h_ref, rev), hprev in zip(chains, hs):
            c = c0 + (n - 1 - kk if rev else kk)
            r0 = pl.multiple_of(c * SUBLANES, SUBLANES)
            big_a, big_b = _sublane_scan(a_ref[pl.ds(r0, SUBLANES), :], b_ref[pl.ds(r0, SUBLANES), :], rev)
            h8 = big_b + big_a * hprev
            h_ref[pl.ds(r0, SUBLANES), :] = h8
            edge = 0 if rev else SUBLANES - 1
            new.append(jnp.broadcast_to(h8[edge:edge + 1, :], h8.shape))
        return tuple(new)

    return lax.fori_loop(0, n, step, tuple(carries))


def _rglru_conv(x, cw_ref, cb_ref, m_ref, l_len, t_len):
    y = cb_ref[...] + _shift_rows(x, -1, m_ref, l_len, t_len) * cw_ref[0:1, :]
    for kk in range(1, CONV_WIDTH):
        y = y + _shift_rows(x, kk - 1, m_ref, l_len, t_len) * cw_ref[kk:kk + 1, :]
    return y


def _rglru_gates(xc, wr, wi, br, bi, lam):
    xb = xc.astype(BF16)
    r = _sigmoid(_dot(xb, wr.astype(BF16), 1, 0) + br)
    ig = _sigmoid(_dot(xb, wi.astype(BF16), 1, 0) + bi)
    sp = _softplus(-lam)
    log_a = (-LRU_C) * r * sp
    a = jnp.exp(log_a)
    z2 = 2.0 * log_a
    series = z2 * (1.0 + z2 * (0.5 + z2 * (1.0 / 6.0 + z2 * (1.0 / 24.0))))
    one_m = jnp.where(jnp.abs(z2) < 0.03, -series, 1.0 - a * a)
    rinv = lax.rsqrt(jnp.maximum(one_m, 1e-30))
    u = ig * xc
    return r, ig, sp, a, one_m * rinv, u, rinv


def _rglru_specs(proj, xr_blk, xg_blk, t_len):
    col = lambda off: pl.BlockSpec((t_len, RNN_BLOCK), lambda h, off=off: (0, off + h))
    vec = lambda rows: pl.BlockSpec((rows, RNN_BLOCK), lambda h: (0, h))
    wsp = pl.BlockSpec((2, 1, RNN_BLOCK, RNN_BLOCK), lambda h: (0, h, 0, 0))
    msp = pl.BlockSpec((t_len, len(SHIFTS) * RNN_BLOCK), lambda h: (0, 0))
    return col(xr_blk), col(xg_blk), vec, wsp, msp


def _rglru_fwd(proj, xr_blk, xg_blk, masks, conv_w, conv_b, w_rg, w_ig, b_rg, b_ig, lam, *, l_len, name):
    t_len = proj.shape[0]
    s_len = t_len - l_len
    d_rnn = conv_b.shape[1]
    n_l, n_t = l_len // SUBLANES, t_len // SUBLANES

    def body(xr_ref, xg_ref, m_ref, cw_ref, cb_ref, wr_ref, wi_ref, br_ref, bi_ref, lam_ref, rg_ref, hf_ref, hb_ref,
             a_s, b_s):
        xc = _rglru_conv(xr_ref[...], cw_ref, cb_ref, m_ref, l_len, t_len)
        for d in range(2):
            _, _, _, a, mult, u, _ = _rglru_gates(xc, wr_ref[d, 0], wi_ref[d, 0], br_ref[d:d + 1, :],
                                               bi_ref[d:d + 1, :], lam_ref[d:d + 1, :])
            a_s[d] = a
            b_s[d] = mult * u
        chains = [(a_s.at[0], b_s.at[0], hf_ref, False), (a_s.at[1], b_s.at[1], hb_ref, True)]
        zero = jnp.zeros((SUBLANES, RNN_BLOCK), F32)
        carry = _scan_chunks(chains, 0, n_l, (zero, zero))
        _scan_chunks(chains, n_l, n_t - n_l, carry)
        rnn = hf_ref[pl.ds(l_len, s_len), :] + hb_ref[pl.ds(l_len, s_len), :]
        rg_ref[...] = (rnn * _gelu(xg_ref[pl.ds(l_len, s_len), :])).astype(BF16)

    xr_spec, xg_spec, vec, wsp, msp = _rglru_specs(proj, xr_blk, xg_blk, t_len)
    col_t = pl.BlockSpec((t_len, RNN_BLOCK), lambda h: (0, h))
    return pl.pallas_call(
        body, name=name, grid=(d_rnn // RNN_BLOCK,),
        in_specs=[xr_spec, xg_spec, msp, vec(CONV_WIDTH), vec(1), wsp, wsp, vec(2), vec(2), vec(2)],
        out_specs=(pl.BlockSpec((s_len, RNN_BLOCK), lambda h: (0, h)), col_t, col_t),
        out_shape=(jax.ShapeDtypeStruct((s_len, d_rnn), BF16), jax.ShapeDtypeStruct((t_len, d_rnn), F32),
                   jax.ShapeDtypeStruct((t_len, d_rnn), F32)),
        scratch_shapes=[pltpu.VMEM((2, t_len, RNN_BLOCK), F32)] * 2,
        compiler_params=_params(("parallel",)),
    )(proj, proj, masks, conv_w, conv_b, w_rg, w_ig, b_rg, b_ig, lam)


N_RNN_VEC = 16


def _rglru_bwd(proj, xr_blk, xg_blk, masks, h_fwd, h_bwd, drg, conv_w, conv_b, w_rg, w_ig, b_rg, b_ig, lam, *, l_len, name):
    t_len = proj.shape[0]
    s_len = t_len - l_len
    d_rnn = conv_b.shape[1]
    n_l, n_t = l_len // SUBLANES, t_len // SUBLANES

    def body(xr_ref, xg_ref, m_ref, hf_ref, hb_ref, drg_ref, cw_ref, cb_ref, wr_ref, wi_ref, br_ref, bi_ref, lam_ref,
             dxr_ref, dxg_ref, dvec_ref, dwr_ref, dwi_ref, a_s, b_s, h_s):
        t = lax.broadcasted_iota(jnp.int32, (t_len, RNN_BLOCK), 0)
        x = xr_ref[...]
        xc = _rglru_conv(x, cw_ref, cb_ref, m_ref, l_len, t_len)
        xb = xc.astype(BF16)
        gates = []
        for d in range(2):
            gt = _rglru_gates(xc, wr_ref[d, 0], wi_ref[d, 0], br_ref[d:d + 1, :], bi_ref[d:d + 1, :],
                              lam_ref[d:d + 1, :])
            gates.append(gt)
        zero = jnp.zeros((SUBLANES, RNN_BLOCK), F32)
        h0, h1 = hf_ref[...], hb_ref[...]
        xg = xg_ref[...]
        dr = jnp.concatenate([jnp.zeros((l_len, RNN_BLOCK), F32), drg_ref[...].astype(F32)], axis=0)
        gel, dgel = _gelu_pair(xg)
        dxg_ref[...] = (dr * (h0 + h1) * dgel).astype(dxg_ref.dtype)
        drnn = dr * gel
        a_s[0] = pltpu.roll(gates[0][3], t_len - 1, 0)
        a_s[1] = pltpu.roll(gates[1][3], 1, 0)
        b_s[0] = drnn
        b_s[1] = drnn
        hprev = [jnp.where(t == 0, 0.0, pltpu.roll(h0, 1, 0)),
                 jnp.where(t == l_len - 1, 0.0, pltpu.roll(h1, t_len - 1, 0))]
        chains = [(a_s.at[0], b_s.at[0], h_s.at[0], True), (a_s.at[1], b_s.at[1], h_s.at[1], False)]
        carry = _scan_chunks(chains, n_l, n_t - n_l, (zero, zero))
        _scan_chunks(chains, 0, n_l, carry)

        dxc = jnp.zeros((t_len, RNN_BLOCK), F32)
        for d in range(2):
            r, ig, sp, a, mult, u, rinv = gates[d]
            lamb = h_s[d]
            da = lamb * hprev[d]
            dmult = lamb * u
            du = lamb * mult
            dlog_a = da * a - dmult * (a * a) * rinv
            dzr = (dlog_a * ((-LRU_C) * sp)) * r * (1.0 - r)
            dsp = jnp.sum(dlog_a * ((-LRU_C) * r), axis=0, keepdims=True)
            dzi = (du * xc) * ig * (1.0 - ig)
            dzrb, dzib = dzr.astype(BF16), dzi.astype(BF16)
            dxc = dxc + du * ig + _dot(dzrb, wr_ref[d, 0].astype(BF16), 1, 1) + _dot(dzib, wi_ref[d, 0].astype(BF16), 1, 1)
            dwr_ref[d, 0] = _dot(xb, dzrb, 0, 0)
            dwi_ref[d, 0] = _dot(xb, dzib, 0, 0)
            dvec_ref[5 + d:6 + d, :] = jnp.sum(dzr, axis=0, keepdims=True)
            dvec_ref[7 + d:8 + d, :] = jnp.sum(dzi, axis=0, keepdims=True)
            dvec_ref[9 + d:10 + d, :] = dsp * (-_sigmoid(-lam_ref[d:d + 1, :]))
        dvec_ref[4:5, :] = jnp.sum(dxc, axis=0, keepdims=True)
        dvec_ref[11:N_RNN_VEC, :] = jnp.zeros((N_RNN_VEC - 11, RNN_BLOCK), F32)
        dxr = jnp.zeros((t_len, RNN_BLOCK), F32)
        for kk in range(CONV_WIDTH):
            dvec_ref[kk:kk + 1, :] = jnp.sum(dxc * _shift_rows(x, kk - 1, m_ref, l_len, t_len), axis=0, keepdims=True)
            dxr = dxr + _shift_rows(dxc, 1 - kk, m_ref, l_len, t_len) * cw_ref[kk:kk + 1, :]
        dxr_ref[...] = dxr.astype(dxr_ref.dtype)

    xr_spec, xg_spec, vec, wsp, msp = _rglru_specs(proj, xr_blk, xg_blk, t_len)
    col_t = pl.BlockSpec((t_len, RNN_BLOCK), lambda h: (0, h))
    return pl.pallas_call(
        body, name=name, grid=(d_rnn // RNN_BLOCK,),
        in_specs=[xr_spec, xg_spec, msp, col_t, col_t, pl.BlockSpec((s_len, RNN_BLOCK), lambda h: (0, h)), vec(CONV_WIDTH),
                  vec(1),
                  wsp, wsp, vec(2), vec(2), vec(2)],
        out_specs=(col_t, col_t, vec(N_RNN_VEC), wsp, wsp),
        out_shape=(jax.ShapeDtypeStruct((t_len, d_rnn), BF16), jax.ShapeDtypeStruct((t_len, d_rnn), BF16),
                   jax.ShapeDtypeStruct((N_RNN_VEC, d_rnn), F32), jax.ShapeDtypeStruct(w_rg.shape, F32),
                   jax.ShapeDtypeStruct(w_ig.shape, F32)),
        scratch_shapes=[pltpu.VMEM((2, t_len, RNN_BLOCK), F32)] * 3,
        compiler_params=_params(("parallel",)),
    )(proj, proj, masks, h_fwd, h_bwd, drg, conv_w, conv_b, w_rg, w_ig, b_rg, b_ig, lam)


def _colsum(v):
    return jnp.sum(v, axis=0, keepdims=True)


def _qk_fwd_heads(xv, gain, cos_e, sin_e):
    outs = []
    for h in range(xv.shape[1] // HEAD_DIM):
        xh = xv[:, h * HEAD_DIM:(h + 1) * HEAD_DIM]
        y = xh * _rms(xh) * gain
        outs.append(y * cos_e + _swap_pairs(y) * sin_e)
    return outs[0] if len(outs) == 1 else jnp.concatenate(outs, axis=1)


def _qk_bwd_heads(xv, dov, gain, cos_e, sin_e):
    outs = []
    dgain = jnp.zeros((1, HEAD_DIM), F32)
    for h in range(xv.shape[1] // HEAD_DIM):
        cols = slice(h * HEAD_DIM, (h + 1) * HEAD_DIM)
        xh, doh = xv[:, cols], dov[:, cols]
        r = _rms(xh)
        xhat = xh * r
        dy = doh * cos_e + _swap_pairs(doh * sin_e)
        dgain = dgain + _colsum(dy * xhat)
        outs.append(_rms_bwd(xhat, r, dy * gain))
    return (outs[0] if len(outs) == 1 else jnp.concatenate(outs, axis=1)), dgain


WEIGHTS = ['c_ctx', 'w_mod', 'b_mod', 'g_mix', 'g_mlp', 'w_in', 'q_gain', 'k_gain', 'conv_w', 'conv_b', 'w_rg', 'b_rg',
           'w_ig', 'b_ig', 'lru_lambda', 'w_o_attn', 'w_o_rnn', 'w_out', 'w_up', 'w_down', 'g_final']
REPLICATED = ['c_ctx', 'b_mod', 'g_mix', 'g_mlp', 'q_gain', 'k_gain', 'conv_b', 'w_rg', 'w_ig', 'g_final']
SMALL_SHARDED = [('conv_w', CONV_WIDTH), ('b_rg', 2), ('b_ig', 2), ('lru_lambda', 2)]
SMALL_ROWS = 16


PACK_ROWS = 64


def _pack_flat(arrs, rows_mult):
    flat = jnp.concatenate([a.reshape(-1) for a in arrs])
    n = flat.shape[0]
    unit = rows_mult * LANES
    pad = (-n) % unit
    if pad:
        flat = jnp.concatenate([flat, jnp.zeros((pad,), flat.dtype)])
    return flat.reshape(-1, LANES)


def _unpack_flat(packed, shapes):
    flat = packed.reshape(-1)
    out, pos = [], 0
    for shp in shapes:
        n = math.prod(shp)
        out.append(flat[pos:pos + n].reshape(shp))
        pos += n
    return out


def _pack_small(d):
    rows = [d[n][0] for n, _ in SMALL_SHARDED]
    cols = rows[0].shape[1]
    used = sum(k for _, k in SMALL_SHARDED)
    return jnp.concatenate(rows + [jnp.zeros((SMALL_ROWS - used, cols), F32)], axis=0)


def _unpack_small(p):
    out, pos = {}, 0
    for n, k in SMALL_SHARDED:
        out[n] = p[pos:pos + k][None]
        pos += k
    return out


def kernel(x, c, ctx, c_ctx, w_mod, b_mod, g_mix, g_mlp, w_in, q_gain, k_gain, conv_w, conv_b, w_rg, b_rg, w_ig, b_ig, lru_lambda, w_o_attn, w_o_rnn, w_out, w_up, w_down, g_final, loss_target, m_c_ctx, m_w_mod, m_b_mod, m_g_mix, m_g_mlp, m_w_in, m_q_gain, m_k_gain, m_conv_w, m_conv_b, m_w_rg, m_b_rg, m_w_ig, m_b_ig, m_lru_lambda, m_w_o_attn, m_w_o_rnn, m_w_out, m_w_up, m_w_down, m_g_final, v_c_ctx, v_w_mod, v_b_mod, v_g_mix, v_g_mlp, v_w_in, v_q_gain, v_k_gain, v_conv_w, v_conv_b, v_w_rg, v_b_rg, v_w_ig, v_b_ig, v_lru_lambda, v_w_o_attn, v_w_o_rnn, v_w_out, v_w_up, v_w_down, v_g_final):
    wd = dict(c_ctx=c_ctx, w_mod=w_mod, b_mod=b_mod, g_mix=g_mix, g_mlp=g_mlp, w_in=w_in, q_gain=q_gain, k_gain=k_gain,
              conv_w=conv_w, conv_b=conv_b, w_rg=w_rg, b_rg=b_rg, w_ig=w_ig, b_ig=b_ig, lru_lambda=lru_lambda,
              w_o_attn=w_o_attn, w_o_rnn=w_o_rnn, w_out=w_out, w_up=w_up, w_down=w_down, g_final=g_final)
    md = dict(zip(WEIGHTS, (m_c_ctx, m_w_mod, m_b_mod, m_g_mix, m_g_mlp, m_w_in, m_q_gain, m_k_gain, m_conv_w, m_conv_b,
                            m_w_rg, m_b_rg, m_w_ig, m_b_ig, m_lru_lambda, m_w_o_attn, m_w_o_rnn, m_w_out, m_w_up,
                            m_w_down, m_g_final)))
    vd = dict(zip(WEIGHTS, (v_c_ctx, v_w_mod, v_b_mod, v_g_mix, v_g_mlp, v_w_in, v_q_gain, v_k_gain, v_conv_w, v_conv_b,
                            v_w_rg, v_b_rg, v_w_ig, v_b_ig, v_lru_lambda, v_w_o_attn, v_w_o_rnn, v_w_out, v_w_up,
                            v_w_down, v_g_final)))

    xs, tgt, ctxs = x[0], loss_target[0], ctx[0]
    s_len, d = xs.shape
    l_len = ctxs.shape[0]
    t_len = s_len + l_len
    n_in = w_in.shape[2] * N_DEV
    aw = (n_in - 4 * d) * 2 // 3
    kvw = aw // Q_GROUP
    assert aw == d and n_in == aw + 2 * kvw + 4 * d
    nmod = N_MOD * d // N_DEV
    tr = min(256, l_len)
    assert l_len % tr == 0 and s_len % tr == 0
    lb = l_len // tr
    tq = min(256, s_len)
    my = _lin(_my_pos())
    c_ctx2 = c_ctx[None]
    g_final2 = g_final[None]

    c_all = _all_gather([jnp.broadcast_to(c, (SUBLANES, d))], [0], "ag_cond")[0][::SUBLANES]
    c16 = jnp.concatenate([c_all, jnp.broadcast_to(c_ctx2, (N_DEV, d))], axis=0)
    b_mod_sh = lax.dynamic_slice(b_mod, (0, my * nmod), (1, nmod))
    mod_part = _mm(c16, w_mod[0], a_fn=_silu, bias=b_mod_sh, name="mod_fwd")
    sp_w = _pack_small(wd)
    (mod_g, sp_f), order = _all_gather([mod_part, sp_w], [1, 1], "ag_mod", with_token=True)
    conv_w_f, b_rg_f, b_ig_f, lam_f = sp_f[0:4], sp_f[4:6], sp_f[6:8], sp_f[8:10]
    mod_lat = lax.dynamic_slice(mod_g, (my, 0), (1, N_MOD * d))
    sh_a, sc_a, ga_a, sh_f, sc_f, ga_f = [mod_lat[:, k * d:(k + 1) * d] for k in range(N_MOD)]
    csh_a, csc_a = mod_g[N_DEV:N_DEV + 1, :d], mod_g[N_DEV:N_DEV + 1, d:2 * d]

    def nm1_fwd(i, j, cv, lv, g, csh, csc, sh, sc):
        is_ctx = i < lb
        xv = jnp.where(is_ctx, cv, lv)
        n = xv * _rms(xv) * g
        return n * (1.0 + jnp.where(is_ctx, csc, sc)) + jnp.where(is_ctx, csh, sh)

    h_all = _rowwise(nm1_fwd, [_row(ctxs), _row(xs, 0, -lb), _bc(g_mix), _bc(csh_a), _bc(csc_a), _bc(sh_a), _bc(sc_a)],
                     [("row", BF16)], nr=t_len // tr, tr=tr, nc=1, cw=d, name="norm1_fwd")[0]
    w_in_f, proj, order = _gather_proj(h_all, _tok(w_in[0], order).astype(BF16), "ag_proj")
    late_axis = dict(w_o_attn=0, w_o_rnn=0, w_out=0, w_up=1, w_down=0)
    placed = [_place_block(wd[n][0], ax, my, "place_" + n) for n, ax in late_axis.items()]
    late_kind = lambda n: "gather_chips" if n in ("w_up", "w_down") else "gather"
    late = {}
    for (n, ax), full in zip(late_axis.items(), placed):
        late[n] = _xfer_start(late_kind(n), order, full, ax, "ag_start_" + n)
        order = late[n][4]
    q_gain_fwd = _tok(q_gain, order)

    late_wait = lambda n, after: _xfer_wait(late_kind(n), late[n], late_axis[n], after, "ag_wait_" + n)

    def sibling_start(n, after):
        arrived = late_wait(n, after)
        return _xfer_start("sibling", jnp.zeros((SUBLANES, LANES), F32), arrived, late_axis[n], "ag_sib_start_" + n)

    sibling_wait = lambda n, handle, after: _xfer_wait("sibling", handle, late_axis[n], after, "ag_sib_wait_" + n)

    pos = jnp.arange(s_len)
    n_freq = HEAD_DIM // 4
    inv_freq = ROPE_THETA ** (-jnp.arange(n_freq, dtype=F32) / n_freq)
    ang = jnp.concatenate([(pos // GRID_W).astype(F32)[:, None] * inv_freq,
                           (pos % GRID_W).astype(F32)[:, None] * inv_freq], axis=-1)
    sign = jnp.where(jnp.arange(HEAD_DIM) % 2 == 0, -1.0, 1.0).astype(F32)
    cos_e = jnp.concatenate([jnp.ones((l_len, HEAD_DIM), F32), jnp.repeat(jnp.cos(ang), 2, axis=-1)], axis=0)
    sin_e = jnp.concatenate([jnp.zeros((l_len, HEAD_DIM), F32), jnp.repeat(jnp.sin(ang), 2, axis=-1) * sign], axis=0)


    kb = aw // kvw
    q = _rowwise(lambda i, j, xv, gain, ce, se: _qk_fwd_heads(xv, gain, ce, se),
                 [_row(proj, 0, lb, aw), _full(q_gain_fwd), _row(cos_e, 0, lb, HEAD_DIM), _row(sin_e, 0, lb, HEAD_DIM)],
                 [("row", BF16, aw)], nr=s_len // tr, tr=tr, nc=1, cw=aw, name="q_fwd")[0]
    k_all, v_all = _rowwise(lambda i, j, kv_, vv_, gain, ce, se: (_qk_fwd_heads(kv_, gain, ce, se), vv_),
                            [_row(proj, kb, 0, kvw), _row(proj, kb + 1, 0, kvw), _full(k_gain),
                             _row(cos_e, 0, 0, HEAD_DIM), _row(sin_e, 0, 0, HEAD_DIM)],
                            [("row", BF16, kvw), ("row", BF16, kvw)], nr=t_len // tr, tr=tr, nc=1, cw=kvw, name="kv_fwd")
    attn_o = _attn_fwd(q, k_all, v_all, tq=tq, name="attn_fwd")

    xr_blk = (aw + 2 * kvw) // RNN_BLOCK
    xg_blk = xr_blk + d // RNN_BLOCK
    shift_masks = _shift_masks(l_len, t_len)
    rg, h_fwd, h_bwd = _rglru_fwd(proj, xr_blk, xg_blk, shift_masks, conv_w_f, conv_b, w_rg[0], w_ig[0], b_rg_f, b_ig_f, lam_f, l_len=l_len,
                    name="rglru_fwd")

    w_oa_f = late_wait("w_o_attn", attn_o)
    y_attn = _mm(attn_o, w_oa_f, name="o_attn_fwd")
    w_or_f = late_wait("w_o_rnn", rg)
    y_rnn = _mm(rg, w_or_f, name="o_rnn_fwd")
    hw = d // 2
    glb = (aw + 2 * kvw + 2 * d) // hw

    def merge_fwd(i, j, ya, yr, gla, glr):
        return _sigmoid(gla) * ya + _sigmoid(glr) * yr

    u = _rowwise(merge_fwd, [_row(y_attn), _row(y_rnn), _row(proj, glb, lb), _row(proj, glb + 2, lb)],
                 [("row", BF16)], nr=s_len // tr, tr=tr, nc=2, cw=hw, name="merge_fwd")[0]
    w_out_f = late_wait("w_out", u)
    sib_up = sibling_start("w_up", u)
    mix = _mm(u, w_out_f, name="out_fwd")

    def res1_fwd(i, j, xv, mixv, gaa, gm, shf, scf):
        x1v = xv + gaa * mixv
        n = x1v * _rms(x1v) * gm
        return x1v, n * (1.0 + scf) + shf

    x1, h2 = _rowwise(res1_fwd, [_row(xs), _row(mix), _bc(_tok(ga_a, sib_up[4])), _bc(g_mlp), _bc(sh_f), _bc(sc_f)],
                      [("row", F32), ("row", BF16)], nr=s_len // tr, tr=tr, nc=1, cw=d, name="norm2_fwd")
    relu2 = lambda z: jnp.square(jnp.maximum(z, 0.0))
    w_up_f = sibling_wait("w_up", sib_up, h2)
    up = _mm(h2, w_up_f, name="up_fwd")
    w_down_f = sibling_wait("w_down", sibling_start("w_down", up), up)
    down = _mm(up, w_down_f, a_fn=relu2, name="down_fwd")

    def final_fn(i, j, x1v, dn, tg, gaf, gf):
        x2 = x1v + gaf * dn
        r = _rms(x2)
        xh = x2 * r
        e = xh * gf - tg
        dy = e * (1.0 / d)
        dx2 = _rms_bwd(xh, r, dy * gf)
        return dx2, dx2 * gaf, _colsum(e * e) * (0.5 / d), _colsum(dy * xh), _colsum(dx2 * dn)

    dx2, ddown, loss_cols, g_gfinal, d_ga_f = _rowwise(
        final_fn, [_row(x1), _row(down), _row(tgt), _bc(ga_f), _bc(g_final2)],
        [("row", F32), ("row", BF16), ("acc", 1), ("acc", 1), ("acc", 1)], nr=s_len // tr, tr=tr, nc=1, cw=d,
        name="final_bwd")
    loss = lax.psum(jnp.sum(loss_cols), ("x", "y", "c"))

    dup = _mm(ddown, w_down_f, tb=True, out_fn=lambda r, upv: r * (2.0 * jnp.maximum(upv, 0.0)), extras=[up],
              out_dtype=BF16, name="down_bwd_x")
    gw_down = _mm(up, ddown, ta=True, a_fn=relu2, out_dtype=BF16, name="down_bwd_w")
    sent = dict(w_down=_scatter_start(gw_down, 0, my, "rs_start_w_down"))
    dh2 = _mm(dup, w_up_f, tb=True, name="up_bwd_x")
    gw_up = _mm(h2, dup, ta=True, out_dtype=BF16, name="up_bwd_w")
    sent["w_up"] = _scatter_start(gw_up, 1, my, "rs_start_w_up")
    sc_f_bwd = _tok(sc_f, sent["w_down"][0][4], sent["w_up"][0][4])

    def norm2_bwd(i, j, x1v, mixv, dh2v, dx2v, gaa, gm, scf):
        r = _rms(x1v)
        xh = x1v * r
        n = xh * gm
        dn = dh2v * (1.0 + scf)
        dx1 = dx2v + _rms_bwd(xh, r, dn * gm)
        return dx1, dx1 * gaa, _colsum(dh2v), _colsum(dh2v * n), _colsum(dn * xh), _colsum(dx1 * mixv)

    dx1, dmix, d_sh_f, d_sc_f, g_gmlp, d_ga_a = _rowwise(
        norm2_bwd, [_row(x1), _row(mix), _row(dh2), _row(dx2), _bc(ga_a), _bc(g_mlp), _bc(sc_f_bwd)],
        [("row", F32), ("row", BF16), ("acc", 1), ("acc", 1), ("acc", 1), ("acc", 1)], nr=s_len // tr, tr=tr, nc=1,
        cw=d, name="norm2_bwd")

    du = _mm(dmix, w_out_f, tb=True, name="out_bwd_x")
    gw_out = _mm(u, dmix, ta=True, out_dtype=BF16, name="out_bwd_w")
    sent["w_out"] = _scatter_start(gw_out, 0, my, "rs_start_w_out")

    def merge_bwd(i, j, duv, ya, yr, gla, glr):
        ga, gr = _sigmoid(gla), _sigmoid(glr)
        return duv * ga, duv * gr, duv * ya * ga * (1.0 - ga), duv * yr * gr * (1.0 - gr)

    dya, dyr, dgla, dglr = _rowwise(
        merge_bwd, [_row(du), _row(y_attn), _row(y_rnn), _row(proj, glb, lb), _row(proj, glb + 2, lb)],
        [("row", BF16)] * 4, nr=s_len // tr, tr=tr, nc=2, cw=hw, name="merge_bwd")
    dattn = _mm(dya, w_oa_f, tb=True, out_dtype=BF16, name="o_attn_bwd_x")
    gw_oa = _mm(attn_o, dya, ta=True, out_dtype=BF16, name="o_attn_bwd_w")
    drg = _mm(dyr, w_or_f, tb=True, out_dtype=BF16, name="o_rnn_bwd_x")
    gw_or = _mm(rg, dyr, ta=True, out_dtype=BF16, name="o_rnn_bwd_w")
    sent["w_o_attn"] = _scatter_start(gw_oa, 0, my, "rs_start_w_o_attn")
    sent["w_o_rnn"] = _scatter_start(gw_or, 0, my, "rs_start_w_o_rnn")
    conv_b_bwd = _tok(conv_b, sent["w_out"][0][4], sent["w_o_attn"][0][4], sent["w_o_rnn"][0][4])

    dq, dk, dv = _attn_bwd(q, k_all, v_all, dattn, tq=tq, name="attn_bwd")
    dxr, dxg, dvec, g_wrg, g_wig = _rglru_bwd(proj, xr_blk, xg_blk, shift_masks, h_fwd, h_bwd, drg, conv_w_f, conv_b_bwd, w_rg[0], w_ig[0], b_rg_f,
                                               b_ig_f, lam_f, l_len=l_len, name="rglru_bwd")
    gate_part = _pack_flat([g_wrg, g_wig], N_DEV * SUBLANES)
    gate_sent = _scatter_start(gate_part, 0, my, "rs_start_gates")
    q_gain_bwd = _tok(q_gain, gate_sent[0][4])

    dq_p, g_qgain = _rowwise(lambda i, j, xv, dov, gain, ce, se: _qk_bwd_heads(xv, dov, gain, ce, se),
                             [_row(proj, 0, lb, aw), _row(dq, 0, 0, aw), _full(q_gain_bwd), _row(cos_e, 0, lb, HEAD_DIM),
                              _row(sin_e, 0, lb, HEAD_DIM)],
                             [("row", BF16, aw), ("acc", 1, HEAD_DIM)], nr=s_len // tr, tr=tr, nc=1, cw=aw, name="q_bwd")

    def kv_bwd(i, j, xv, dkv, dvv, gain, ce, se):
        dk_p, dgain = _qk_bwd_heads(xv, dkv, gain, ce, se)
        return dk_p, dvv, dgain

    dk_p, dv_p, g_kgain = _rowwise(kv_bwd, [_row(proj, kb, 0, kvw), _row(dk, 0, 0, kvw), _row(dv, 0, 0, kvw),
                                            _full(k_gain), _row(cos_e, 0, 0, HEAD_DIM), _row(sin_e, 0, 0, HEAD_DIM)],
                                   [("row", BF16, kvw), ("row", BF16, kvw), ("acc", 1, HEAD_DIM)], nr=t_len // tr, tr=tr,
                                   nc=1, cw=kvw, name="kv_bwd")

    def dproj_fn(i, j, dqv, dkv, dvv, dxrv, dxgv, dgav, dgrv):
        zero_ctx = lambda a: jnp.where(i < lb, jnp.zeros_like(a), a)
        return jnp.concatenate([zero_ctx(dqv), dkv, dvv, dxrv, dxgv, zero_ctx(dgav), zero_ctx(dgrv)], axis=1)

    dproj = _rowwise(dproj_fn, [_row(dq_p, 0, -lb, aw), _row(dk_p, 0, 0, kvw), _row(dv_p, 0, 0, kvw), _row(dxr, 0, 0, d),
                                _row(dxg, 0, 0, d), _row(dgla, 0, -lb, d), _row(dglr, 0, -lb, d)],
                     [("row", BF16, n_in)], nr=t_len // tr, tr=tr, nc=1, cw=n_in, name="dproj_pack")[0]
    dh_all = _mm(dproj, w_in_f, tb=True, name="proj_bwd_x")
    g_mix_bwd = g_mix

    def norm1_bwd(i, j, xv, dhv, dresv, g, sc):
        r = _rms(xv)
        xh = xv * r
        dn = dhv * (1.0 + sc)
        return dresv + _rms_bwd(xh, r, dn * g), _colsum(dhv), _colsum(dhv * (xh * g)), _colsum(dn * xh)

    def norm1_bwd_ctx(i, j, xv, dhv, g, sc):
        xh = xv * _rms(xv)
        return _colsum(dhv), _colsum(dhv * (xh * g)), _colsum(dhv * (1.0 + sc) * xh)

    dx_lat, d_sh_a, d_sc_a, g_gmix_lat = _rowwise(
        norm1_bwd, [_row(xs), _row(dh_all, 0, lb), _row(dx1), _bc(g_mix_bwd), _bc(sc_a)],
        [("row", F32)] + [("acc", 1)] * 3, nr=s_len // tr, tr=tr, nc=1, cw=d, name="norm1_bwd")
    d_csh, d_csc, g_gmix_ctx = _rowwise(
        norm1_bwd_ctx, [_row(ctxs), _row(dh_all), _bc(g_mix_bwd), _bc(csc_a)],
        [("acc", 1)] * 3, nr=lb, tr=tr, nc=1, cw=d, name="norm1_bwd_ctx")
    g_gmix = g_gmix_lat + g_gmix_ctx
    grad_x = dx_lat[None]

    dmod_lat = jnp.concatenate([d_sh_a, d_sc_a, d_ga_a, d_sh_f, d_sc_f, d_ga_f], axis=1)
    dmod_ctx = jnp.concatenate([d_csh, d_csc, jnp.zeros((1, (N_MOD - 2) * d), F32)], axis=1)
    dm_g = _all_gather([jnp.concatenate([dmod_lat, dmod_ctx, jnp.zeros((SUBLANES - 2, N_MOD * d), F32)], axis=0)], [0],
                       "ag_dmod")[0]
    dm16 = jnp.concatenate([dm_g[0::SUBLANES], dm_g[1::SUBLANES]], axis=0)
    dm_sh = lax.dynamic_slice(dm16, (0, my * nmod), (2 * N_DEV, nmod))
    gw_mod = _mm(c16, dm_sh, ta=True, a_fn=_silu, name="mod_bwd_w")
    dc16 = _mm(dm_sh, w_mod[0], tb=True, name="mod_bwd_x")

    def cctx_fn(i, j, pv, cc, dl, dc):
        row = lax.broadcasted_iota(jnp.int32, pv.shape, 0)
        return _colsum(jnp.where(row >= N_DEV, pv, 0.0)) * _dsilu(cc), dl + dc

    g_cctx, g_bmod = _rowwise(cctx_fn, [_full(dc16), _full(c_ctx2), _full(dmod_lat), _full(dmod_ctx)],
                              [("row", F32, d), ("row", F32, N_MOD * d)], nr=1, tr=1, nc=1, cw=d, name="cctx_bwd")

    rep_part = dict(c_ctx=g_cctx, b_mod=g_bmod, g_mix=g_gmix, g_mlp=g_gmlp, q_gain=g_qgain, k_gain=g_kgain,
                    conv_b=dvec[4:5], g_final=g_gfinal)
    rep_small = [n for n in REPLICATED if n in rep_part]
    gates = [n for n in REPLICATED if n not in rep_part]
    part_pack = _pack_flat([rep_part[n] for n in rep_small], PACK_ROWS)
    rep_rows = part_pack.shape[0]
    gate_land = _xfer_wait("scatter", gate_sent[0], 0, g_bmod, "rs_wait_gates")
    gate_sum = _sum_parts(gate_sent[1], gate_land, name="gate_sum")
    (gate_full, parts), order = _all_gather([gate_sum, part_pack], [0, 0], "ag_rep_grads", with_token=True)
    small_g = jnp.concatenate([dvec[0:4], dvec[5:11], jnp.zeros((SMALL_ROWS - 10, d), F32)], axis=0)
    (small_recv,), order = _all_to_all([_tok(small_g, order)], [1], "a2a_small_grads")

    halves = {}
    for part, lo in (("top", 0), ("bot", d // 2)):
        gw_part = _mm(h_all, dproj, ta=True, a_cols=(lo, d // 2), out_dtype=BF16, name="proj_bwd_w_" + part,
                      bias=_tok(jnp.zeros((1, n_in), F32), order))
        halves[part] = _chip_scatter_start(_pair_reduce(gw_part, my, "pair_w_in_" + part), my, "rs_start_w_in_" + part)
        order = halves[part][0][4]
    started = order

    res = {}
    for names, pr, mult, nm in ((rep_small, parts.reshape(N_DEV, rep_rows, LANES), PACK_ROWS, "adamw_rep"),
                                (gates, gate_full[None], N_DEV * SUBLANES, "adamw_gates")):
        rep_out = _adamw(pr, _tok(_pack_flat([wd[n] for n in names], mult), started),
                         _pack_flat([md[n] for n in names], mult), _pack_flat([vd[n] for n in names], mult), name=nm)
        for kind, packed in zip(("g", "d", "m", "v"), rep_out):
            for n, a in zip(names, _unpack_flat(packed, [wd[n].shape for n in names])):
                res[kind, n] = a
    out = _adamw(small_recv, _tok(sp_w, started), _pack_small(md), _pack_small(vd), name="adamw_small")
    for kind, a in zip(("g", "d", "m", "v"), out):
        for n, piece in _unpack_small(a).items():
            res[kind, n] = piece

    after = started
    for n, (handle, own, ax, kind) in sent.items():
        land = _xfer_wait(kind, handle, ax, after, "rs_wait_" + n)
        out = _adamw(land, wd[n][0], md[n][0], vd[n][0], own=own, name="adamw_" + n)
        after = out[0]
        for kind, a in zip(("g", "d", "m", "v"), out):
            res[kind, n] = a[None]
    out = _adamw(gw_mod[None], w_mod[0], m_w_mod[0], v_w_mod[0], name="adamw_w_mod")
    after = out[0]
    for kind, a in zip(("g", "d", "m", "v"), out):
        res[kind, 'w_mod'] = a[None]
    lands, owns = [], []
    for part, (handle, own, ax, kind) in halves.items():
        lands.append(_xfer_wait(kind, handle, ax, after, "rs_wait_w_in_" + part))
        owns.append(own)
        after = lands[-1]
    out = _adamw(jnp.concatenate(lands, axis=1), w_in[0], m_w_in[0], v_w_in[0], own=jnp.concatenate(owns, axis=0),
                 name="adamw_w_in")
    for kind, a in zip(("g", "d", "m", "v"), out):
        res[kind, 'w_in'] = a[None]

    outs = [loss, grad_x]
    for kind in ("g", "d", "m", "v"):
        outs += [res[kind, n].reshape(wd[n].shape) for n in WEIGHTS]
    return tuple(outs)
```

```python
import math

import jax
import jax.numpy as jnp
from jax import lax
from jax.experimental import pallas as pl
from jax.experimental.pallas import tpu as pltpu

F32 = jnp.float32
BF16 = jnp.bfloat16
MESH = pl.DeviceIdType.MESH

N_DEV = 8
HEAD_DIM = 128
Q_GROUP = 4
RNN_BLOCK = 128
GRID_W = 64
CONV_WIDTH = 4
LRU_C = 8.0
ROPE_THETA = 10000.0
NORM_EPS = 1e-6
N_MOD = 6
ADAM_LR = 0.001
ADAM_B1 = 0.9
ADAM_B2 = 0.999
ADAM_EPS = 1e-08
ADAM_WD = 0.01
ADAM_STEP = 10

LANES = 128
SUBLANES = 8
VMEM_LIMIT = 56 * 1024 * 1024
HBM_SPEC = pl.BlockSpec(memory_space=pltpu.HBM)


def _params(sem=None, vmem=VMEM_LIMIT):
    kw = dict(vmem_limit_bytes=vmem)
    if sem is not None:
        kw["dimension_semantics"] = sem
    return pltpu.CompilerParams(**kw)


def _pick(dim, target, mult):
    for t in range(min(dim, target), 0, -1):
        if dim % t == 0 and t % mult == 0:
            return t
    return dim


def _my_pos():
    return lax.axis_index("x"), lax.axis_index("y"), lax.axis_index("c")


def _flip(pos, d):
    x, y, c = pos
    return (1 - x if d & 4 else x, 1 - y if d & 2 else y, 1 - c if d & 1 else c)


def _lin(pos):
    return 4 * pos[0] + 2 * pos[1] + pos[2]


def _slab(ref, axis, idx, size):
    if axis == 0:
        return ref.at[pl.ds(idx * size, size)]
    assert axis == 1
    return ref.at[:, pl.ds(idx * size, size)]


def _all_gather(arrs, axes, name, with_token=False):
    n = len(arrs)
    sizes = [a.shape[ax] for a, ax in zip(arrs, axes)]

    def body(*refs):
        ins, outs, token = refs[:n], refs[n:2 * n], refs[2 * n]
        send_sems, recv_sems, local_sems = refs[2 * n + 1:]
        token[...] = jnp.zeros_like(token)
        me = _my_pos()
        sibling = _flip(me, 1)
        chips = [2, 4, 6]

        def copy(a, k, block_pos, to, src=None):
            dst = _slab(outs[a], axes[a], _lin(block_pos), sizes[a])
            return pltpu.make_async_remote_copy(
                src_ref=dst if src is None else src, dst_ref=dst,
                send_sem=send_sems.at[7 * a + k], recv_sem=recv_sems.at[7 * a + k],
                device_id=to, device_id_type=MESH)

        local, first, passed = [], [], []
        for a in range(n):
            cp = pltpu.make_async_copy(ins[a], _slab(outs[a], axes[a], _lin(me), sizes[a]), local_sems.at[a])
            cp.start()
            local.append(cp)
        for a in range(n):
            f = [copy(a, 0, me, sibling, src=ins[a])]
            f += [copy(a, 1 + j, me, _flip(me, m), src=ins[a]) for j, m in enumerate(chips)]
            for cp in f:
                cp.start()
            first += f
        for a in range(n):
            for j, m in enumerate(chips):
                copy(a, 1 + j, _flip(me, m), me).wait_recv()
                cp = copy(a, 4 + j, _flip(me, m), sibling)
                cp.start()
                passed.append(cp)
        for a in range(n):
            copy(a, 0, sibling, me).wait_recv()
            for j, m in enumerate(chips):
                copy(a, 4 + j, _flip(sibling, m), me).wait_recv()
        for cp in first + passed:
            cp.wait_send()
        for cp in local:
            cp.wait()

    out_shape = []
    for a, ax in zip(arrs, axes):
        shp = list(a.shape)
        shp[ax] *= N_DEV
        out_shape.append(jax.ShapeDtypeStruct(tuple(shp), a.dtype))
    out_shape.append(jax.ShapeDtypeStruct((SUBLANES, LANES), F32))
    res = pl.pallas_call(
        body, name=name, out_shape=tuple(out_shape),
        in_specs=[HBM_SPEC] * n, out_specs=tuple([HBM_SPEC] * n + [pl.BlockSpec(memory_space=pltpu.VMEM)]),
        scratch_shapes=[pltpu.SemaphoreType.DMA((7 * n,)), pltpu.SemaphoreType.DMA((7 * n,)),
                        pltpu.SemaphoreType.DMA((n,))],
    )(*arrs)
    return (list(res[:n]), res[n]) if with_token else list(res[:n])


def _gather_proj(h_all, w_shard, name):
    t_len, kdim = h_all.shape
    sz = w_shard.shape[1]
    tm = _pick(t_len, 1024, 16)
    n_m = t_len // tm

    def body(h_ref, w_ref, wfull_ref, proj_ref, token, a_buf, b_buf, o_buf, send_sems, recv_sems, io_sems, out_sems):
        token[...] = jnp.zeros_like(token)
        me = _my_pos()
        sibling = _flip(me, 1)
        chips = [2, 4, 6]
        slab = lambda pos: _slab(wfull_ref, 1, _lin(pos), sz)

        def copy(k, block_pos, to, src=None):
            return pltpu.make_async_remote_copy(
                src_ref=slab(block_pos) if src is None else src, dst_ref=slab(block_pos),
                send_sem=send_sems.at[k], recv_sem=recv_sems.at[k], device_id=to, device_id_type=MESH)

        pending = []

        def block_product(pos, b_src):
            load = pltpu.make_async_copy(b_src, b_buf, io_sems.at[1])
            load.start()
            load.wait()
            for i in range(n_m):
                slot = len(pending) % 2
                if len(pending) >= 2:
                    pending[-2].wait()
                o_buf[slot] = _dot(a_buf[pl.ds(i * tm, tm), :], b_buf[...], 1, 0)
                out = pltpu.make_async_copy(
                    o_buf.at[slot], proj_ref.at[pl.ds(i * tm, tm), pl.ds(_lin(pos) * sz, sz)], out_sems.at[slot])
                out.start()
                pending.append(out)

        local = pltpu.make_async_copy(w_ref, slab(me), io_sems.at[0])
        local.start()
        first = [copy(0, me, sibling, src=w_ref)] + [copy(1 + j, me, _flip(me, m), src=w_ref) for j, m in enumerate(chips)]
        for cp in first:
            cp.start()
        load_a = pltpu.make_async_copy(h_ref, a_buf, io_sems.at[2])
        load_a.start()
        load_a.wait()
        block_product(me, w_ref)
        copy(0, sibling, me).wait_recv()
        block_product(sibling, slab(sibling))
        passed = []
        for j, m in enumerate(chips):
            src_pos = _flip(me, m)
            copy(1 + j, src_pos, me).wait_recv()
            cp = copy(4 + j, src_pos, sibling)
            cp.start()
            passed.append(cp)
            block_product(src_pos, slab(src_pos))
        for j, m in enumerate(chips):
            src_pos = _flip(sibling, m)
            copy(4 + j, src_pos, me).wait_recv()
            block_product(src_pos, slab(src_pos))
        for out in pending[-2:]:
            out.wait()
        for cp in first + passed:
            cp.wait_send()
        local.wait()

    return pl.pallas_call(
        body, name=name,
        out_shape=(jax.ShapeDtypeStruct((kdim, N_DEV * sz), w_shard.dtype), jax.ShapeDtypeStruct((t_len, N_DEV * sz), F32),
                   jax.ShapeDtypeStruct((SUBLANES, LANES), F32)),
        in_specs=[HBM_SPEC, HBM_SPEC], out_specs=(HBM_SPEC, HBM_SPEC, pl.BlockSpec(memory_space=pltpu.VMEM)),
        scratch_shapes=[pltpu.VMEM((t_len, kdim), h_all.dtype), pltpu.VMEM((kdim, sz), w_shard.dtype),
                        pltpu.VMEM((2, tm, sz), F32), pltpu.SemaphoreType.DMA((7,)), pltpu.SemaphoreType.DMA((7,)),
                        pltpu.SemaphoreType.DMA((3,)), pltpu.SemaphoreType.DMA((2,))],
        compiler_params=_params(),
    )(h_all, w_shard)


def _all_to_all(arrs, axes, name):
    n = len(arrs)
    sizes = [a.shape[ax] // N_DEV for a, ax in zip(arrs, axes)]

    def body(*refs):
        ins, outs, token = refs[:n], refs[n:2 * n], refs[2 * n]
        send_sems, recv_sems, local_sems = refs[2 * n + 1:]
        token[...] = jnp.zeros_like(token)
        me = _my_pos()
        my = _lin(me)
        copies, local = [], []
        for a in range(n):
            cp = pltpu.make_async_copy(_slab(ins[a], axes[a], my, sizes[a]), outs[a].at[my], local_sems.at[a])
            cp.start()
            local.append(cp)
        for d in range(1, N_DEV):
            peer = _flip(me, d)
            for a in range(n):
                cp = pltpu.make_async_remote_copy(
                    src_ref=_slab(ins[a], axes[a], _lin(peer), sizes[a]), dst_ref=outs[a].at[my],
                    send_sem=send_sems.at[7 * a + d - 1], recv_sem=recv_sems.at[7 * a + d - 1],
                    device_id=peer, device_id_type=MESH)
                cp.start()
                copies.append((a, d, cp))
        for a, d, cp in copies:
            peer = _flip(me, d)
            pltpu.make_async_remote_copy(
                src_ref=_slab(ins[a], axes[a], _lin(peer), sizes[a]), dst_ref=outs[a].at[_lin(peer)],
                send_sem=send_sems.at[7 * a + d - 1], recv_sem=recv_sems.at[7 * a + d - 1],
                device_id=peer, device_id_type=MESH).wait_recv()
            cp.wait_send()
        for cp in local:
            cp.wait()

    out_shape = []
    for a, ax, s in zip(arrs, axes, sizes):
        shp = list(a.shape)
        shp[ax] = s
        out_shape.append(jax.ShapeDtypeStruct((N_DEV,) + tuple(shp), a.dtype))
    out_shape.append(jax.ShapeDtypeStruct((SUBLANES, LANES), F32))
    res = pl.pallas_call(
        body, name=name, out_shape=tuple(out_shape),
        in_specs=[HBM_SPEC] * n, out_specs=tuple([HBM_SPEC] * n + [pl.BlockSpec(memory_space=pltpu.VMEM)]),
        scratch_shapes=[pltpu.SemaphoreType.DMA((7 * n,)), pltpu.SemaphoreType.DMA((7 * n,)),
                        pltpu.SemaphoreType.DMA((n,))],
    )(*arrs)
    return list(res[:n]), res[n]


SEM_SPEC = pl.BlockSpec(memory_space=pltpu.SEMAPHORE)
_EFFECT = pltpu.SideEffectType.DATAFLOW_SIDE_EFFECTING
N_PEERS = N_DEV - 1


def _split_copies(kind, src_ref, land_ref, send_sems, recv_sems, axis, size, arriving):
    me = _my_pos()
    out = []
    if kind == "sibling":
        sibling = _flip(me, 1)
        for k in range(N_CHIPS):
            blk = _slab(land_ref, axis, 2 * k + (sibling[2] if arriving else me[2]), size)
            out.append(pltpu.make_async_remote_copy(src_ref=blk, dst_ref=blk, send_sem=send_sems.at[k],
                                                    recv_sem=recv_sems.at[k], device_id=sibling, device_id_type=MESH))
        return out
    for dd in (range(2, N_DEV, 2) if kind in ("chips", "gather_chips") else range(1, N_DEV)):
        peer = _flip(me, dd)
        k = dd - 1
        if kind in ("gather", "gather_chips"):
            src = _slab(land_ref, axis, _lin(me), size)
            dst = _slab(land_ref, axis, _lin(peer if arriving else me), size)
        elif kind == "scatter":
            src, dst = _slab(src_ref, axis, _lin(peer), size), land_ref.at[k]
        else:
            k = dd // 2 - 1
            src, dst = src_ref.at[2 * peer[0] + peer[1]], land_ref.at[k]
        out.append(pltpu.make_async_remote_copy(
            src_ref=src, dst_ref=dst, send_sem=send_sems.at[k], recv_sem=recv_sems.at[k],
            device_id=peer, device_id_type=MESH))
    return out


N_CHIPS = N_DEV // 2


def _split_size(kind, src_shape, land_shape, axis):
    if kind in ("gather", "gather_chips", "sibling"):
        return land_shape[axis] // N_DEV
    return src_shape[axis] // {"scatter": N_DEV, "chips": N_CHIPS}[kind]


def _xfer_start(kind, src, land, axis, name):
    size = _split_size(kind, src.shape, land.shape, axis)

    def body(src_ref, land_ref, send_sems, recv_sems, src_thru, land_thru, token):
        for cp in _split_copies(kind, src_ref, land_ref, send_sems, recv_sems, axis, size, False):
            cp.start()
        token[...] = jnp.zeros_like(token)

    return pl.pallas_call(
        body, name=name,
        out_shape=(pltpu.SemaphoreType.DMA((N_PEERS,)), pltpu.SemaphoreType.DMA((N_PEERS,)),
                   pltpu.HBM(src.shape, src.dtype), pltpu.HBM(land.shape, land.dtype),
                   jax.ShapeDtypeStruct((SUBLANES, LANES), F32)),
        in_specs=(HBM_SPEC, HBM_SPEC),
        out_specs=(SEM_SPEC, SEM_SPEC, HBM_SPEC, HBM_SPEC, pl.BlockSpec(memory_space=pltpu.VMEM)),
        input_output_aliases={0: 2, 1: 3},
        compiler_params=pltpu.CompilerParams(has_side_effects=_EFFECT),
    )(pltpu.with_memory_space_constraint(src, pltpu.HBM), pltpu.with_memory_space_constraint(land, pltpu.HBM))


def _xfer_wait(kind, handle, axis, after, name):
    send_sems, recv_sems, src_thru, land_thru, _ = handle
    size = _split_size(kind, src_thru.shape, land_thru.shape, axis)

    def body(src_ref, land_ref, send_sems, recv_sems, after_ref, src_dead, got_ref):
        for cp in _split_copies(kind, src_ref, land_ref, send_sems, recv_sems, axis, size, True):
            cp.wait_send()
            cp.wait_recv()

    return pl.pallas_call(
        body, name=name,
        out_shape=(pltpu.HBM(src_thru.shape, src_thru.dtype), pltpu.HBM(land_thru.shape, land_thru.dtype)),
        in_specs=(HBM_SPEC, HBM_SPEC, SEM_SPEC, SEM_SPEC, pl.BlockSpec(memory_space=pl.ANY)),
        out_specs=(HBM_SPEC, HBM_SPEC), input_output_aliases={0: 0, 1: 1},
        compiler_params=pltpu.CompilerParams(has_side_effects=_EFFECT),
    )(src_thru, land_thru, send_sems, recv_sems, after)[1]


def _place_block(shard, axis, my, name):
    r, c = shard.shape
    tr = _pick(r, max(16, (1 << 19) // c), 16)
    nr = r // tr
    if axis == 0:
        full, out_map = (r * N_DEV, c), (lambda i, my_ref: (my_ref[0] * nr + i, 0))
    else:
        full, out_map = (r, c * N_DEV), (lambda i, my_ref: (i, my_ref[0]))

    def body(my_ref, x_ref, o_ref):
        o_ref[...] = x_ref[...].astype(BF16)

    return pl.pallas_call(
        body, name=name, out_shape=jax.ShapeDtypeStruct(full, BF16),
        grid_spec=pltpu.PrefetchScalarGridSpec(
            num_scalar_prefetch=1, grid=(nr,), in_specs=[pl.BlockSpec((tr, c), lambda i, my_ref: (i, 0))],
            out_specs=pl.BlockSpec((tr, c), out_map)),
        compiler_params=_params(("arbitrary",)),
    )(jnp.reshape(my, (1,)).astype(jnp.int32), shard)


def _scatter_start(grad, axis, my, name):
    shp = list(grad.shape)
    shp[axis] //= N_DEV
    own = lax.dynamic_slice_in_dim(grad, my * shp[axis], shp[axis], axis)
    return (_xfer_start("scatter", grad, lax.empty((N_PEERS,) + tuple(shp), grad.dtype), axis, name), own, axis,
            "scatter")


def _pair_reduce(grad, my, name):
    r, cols = grad.shape
    sz = cols // N_DEV

    def exchange(g_ref, got_ref, send_sems, recv_sems):
        me = _my_pos()
        sibling = _flip(me, 1)
        cps = []
        for k in range(N_CHIPS):
            cp = pltpu.make_async_remote_copy(
                src_ref=_slab(g_ref, 1, 2 * k + sibling[2], sz), dst_ref=_slab(got_ref, 1, k, sz),
                send_sem=send_sems.at[k], recv_sem=recv_sems.at[k], device_id=sibling, device_id_type=MESH)
            cp.start()
            cps.append(cp)
        for cp in cps:
            cp.wait()

    got = pl.pallas_call(
        exchange, name=name + "_swap", out_shape=jax.ShapeDtypeStruct((r, N_CHIPS * sz), grad.dtype),
        in_specs=[HBM_SPEC], out_specs=HBM_SPEC,
        scratch_shapes=[pltpu.SemaphoreType.DMA((N_CHIPS,)), pltpu.SemaphoreType.DMA((N_CHIPS,))],
    )(grad)

    tr = _pick(r, 512, 16)

    def add(c_ref, g_ref, p_ref, o_ref):
        o_ref[...] = (g_ref[...].astype(F32) + p_ref[...].astype(F32)).astype(o_ref.dtype)

    def add(c_ref, g_ref, p_ref, o_ref):
        o_ref[0] = (g_ref[...].astype(F32) + p_ref[...].astype(F32)).astype(o_ref.dtype)

    return pl.pallas_call(
        add, name=name + "_sum", out_shape=jax.ShapeDtypeStruct((N_CHIPS, r, sz), grad.dtype),
        grid_spec=pltpu.PrefetchScalarGridSpec(
            num_scalar_prefetch=1, grid=(N_CHIPS, r // tr),
            in_specs=[pl.BlockSpec((tr, sz), lambda k, i, c_ref: (i, 2 * k + c_ref[0])),
                      pl.BlockSpec((tr, sz), lambda k, i, c_ref: (i, k))],
            out_specs=pl.BlockSpec((1, tr, sz), lambda k, i, c_ref: (k, i, 0))),
        compiler_params=_params(("parallel", "parallel")),
    )(jnp.reshape(my % 2, (1,)).astype(jnp.int32), grad, got)


def _chip_scatter_start(pair_sum, my, name):
    _, r, sz = pair_sum.shape
    own = lax.dynamic_index_in_dim(pair_sum, my // 2, 0, keepdims=False)
    return (_xfer_start("chips", pair_sum, lax.empty((N_CHIPS - 1, r, sz), pair_sum.dtype), 0, name), own, 0, "chips")


def _tok(a, *tokens):
    for t in tokens:
        a = a + t[0, 0].astype(a.dtype)
    return a


def _mm(a, b, *, name, ta=False, tb=False, out_dtype=F32, a_fn=None, out_fn=None, extras=(), bias=None,
        tm=1024, tn=1024, tk=2816, a_cols=None):
    a_blk0 = 0
    if ta:
        kdim, m = a.shape
        if a_cols is not None:
            m = a_cols[1]
    else:
        m, kdim = a.shape
    if tb:
        n, kb = b.shape
    else:
        kb, n = b.shape
    assert kb == kdim, (a.shape, b.shape, ta, tb)
    tm = _pick(m, tm, LANES if ta else SUBLANES)
    if a_cols is not None:
        assert ta and a_cols[0] % tm == 0
        a_blk0 = a_cols[0] // tm
    tn = _pick(n, tn, LANES)
    tk = _pick(kdim, tk, LANES)
    nk = kdim // tk
    ne = len(extras)
    has_bias = bias is not None

    def body(*refs):
        a_ref, b_ref = refs[0], refs[1]
        pos = 2
        bias_ref = None
        if has_bias:
            bias_ref = refs[pos]
            pos += 1
        extra_refs = refs[pos:pos + ne]
        o_ref = refs[pos + ne]
        av = a_ref[...]
        if a_fn is not None:
            av = a_fn(av)
        dn = (((0 if ta else 1,), (1 if tb else 0,)), ((), ()))
        part = lax.dot_general(av.astype(BF16), b_ref[...].astype(BF16), dn, preferred_element_type=F32)

        def finish(r):
            if has_bias:
                r = r + bias_ref[...]
            if out_fn is not None:
                r = out_fn(r, *[e[...] for e in extra_refs])
            o_ref[...] = r.astype(out_dtype)

        if nk == 1:
            finish(part)
            return
        acc = refs[pos + ne + 1]
        k = pl.program_id(2)

        @pl.when(k == 0)
        def _():
            acc[...] = part

        @pl.when(k > 0)
        def _():
            acc[...] += part

        @pl.when(k == nk - 1)
        def _():
            finish(acc[...])

    in_specs = [
        pl.BlockSpec((tk, tm), lambda i, j, k: (k, i + a_blk0)) if ta else pl.BlockSpec((tm, tk), lambda i, j, k: (i, k)),
        pl.BlockSpec((tn, tk), lambda i, j, k: (j, k)) if tb else pl.BlockSpec((tk, tn), lambda i, j, k: (k, j)),
    ]
    args = [a, b]
    if has_bias:
        in_specs.append(pl.BlockSpec((1, tn), lambda i, j, k: (0, j)))
        args.append(bias)
    for e in extras:
        in_specs.append(pl.BlockSpec((tm, tn), lambda i, j, k: (i, j)))
        args.append(e)
    return pl.pallas_call(
        body, name=name, grid=(m // tm, n // tn, nk),
        in_specs=in_specs, out_specs=pl.BlockSpec((tm, tn), lambda i, j, k: (i, j)),
        out_shape=jax.ShapeDtypeStruct((m, n), out_dtype),
        scratch_shapes=[pltpu.VMEM((tm, tn), F32)] if nk > 1 else [],
        compiler_params=_params(("parallel", "parallel", "arbitrary")),
    )(*args)


def _row(arr, cb=0, rb=0, w=None):
    return ("row", arr, cb, rb, w)


def _bc(arr, cb=0, w=None):
    return ("bc", arr, cb, 0, w)


def _full(arr):
    return ("full", arr, 0, 0, None)


def _rowwise(fn, ins, outs, *, nr, tr, nc, cw, name):
    n_in = len(ins)
    outs = [(o[0], o[1], (o[2] if len(o) > 2 and o[2] else cw)) for o in outs]

    def body(*refs):
        in_refs, out_refs = refs[:n_in], refs[n_in:]
        j, i = pl.program_id(0), pl.program_id(1)
        res = fn(i, j, *[r[...] for r in in_refs])
        if not isinstance(res, (tuple, list)):
            res = (res,)
        for (kind, _, _), o_ref, r in zip(outs, out_refs, res):
            if kind == "row":
                o_ref[...] = r.astype(o_ref.dtype)
            else:
                @pl.when(i == 0)
                def _(o_ref=o_ref):
                    o_ref[...] = jnp.zeros_like(o_ref)

                o_ref[...] += r

    in_specs, args = [], []
    for kind, arr, cb, rb, w in ins:
        w = w or cw
        if kind == "row":
            last = arr.shape[0] // tr - 1
            in_specs.append(pl.BlockSpec((tr, w), lambda j, i, cb=cb, rb=rb, last=last: (jnp.clip(i + rb, 0, last), j + cb)))
        elif kind == "bc":
            in_specs.append(pl.BlockSpec((arr.shape[0], w), lambda j, i, cb=cb: (0, j + cb)))
        else:
            in_specs.append(pl.BlockSpec(arr.shape, lambda j, i, nd=arr.ndim: (0,) * nd))
        args.append(arr)
    out_specs, out_shape = [], []
    for kind, spec, w in outs:
        if kind == "row":
            out_specs.append(pl.BlockSpec((tr, w), lambda j, i: (i, j)))
            out_shape.append(jax.ShapeDtypeStruct((nr * tr, nc * w), spec))
        else:
            out_specs.append(pl.BlockSpec((spec, w), lambda j, i: (0, j)))
            out_shape.append(jax.ShapeDtypeStruct((spec, nc * w), F32))
    res = pl.pallas_call(
        body, name=name, grid=(nc, nr), in_specs=in_specs, out_specs=tuple(out_specs), out_shape=tuple(out_shape),
        compiler_params=_params(("parallel", "arbitrary")),
    )(*args)
    return res


def _sum_parts(own, parts, *, name):
    p, r, c = parts.shape
    tr = _pick(r, max(SUBLANES, (1 << 19) // c), SUBLANES)

    def body(o_ref, p_ref, g_out):
        g = o_ref[...].astype(F32)
        for q in range(p):
            g = g + p_ref[q].astype(F32)
        g_out[...] = g

    blk = pl.BlockSpec((tr, c), lambda i: (i, 0))
    return pl.pallas_call(
        body, name=name, grid=(r // tr,), in_specs=[blk, pl.BlockSpec((p, tr, c), lambda i: (0, i, 0))],
        out_specs=blk, out_shape=jax.ShapeDtypeStruct((r, c), F32), compiler_params=_params(("parallel",)),
    )(own, parts)


def _adamw(parts, w, m, v, *, name, own=None):
    p, r, c = parts.shape
    tr = _pick(r, max(SUBLANES, (1 << 19) // c), 16 if parts.dtype == BF16 else SUBLANES)
    bc1 = 1.0 - ADAM_B1 ** ADAM_STEP
    bc2 = 1.0 - ADAM_B2 ** ADAM_STEP
    has_own = own is not None

    def body(*refs):
        p_ref, w_ref, m_ref, v_ref = refs[:4]
        g_out, d_out, m_out, v_out = refs[4 + has_own:]
        if has_own:
            g = refs[4][...].astype(F32) + p_ref[0].astype(F32)
        else:
            g = p_ref[0].astype(F32)
        for q in range(1, p):
            g = g + p_ref[q].astype(F32)
        m_new = ADAM_B1 * m_ref[...] + (1.0 - ADAM_B1) * g
        v_new = ADAM_B2 * v_ref[...] + (1.0 - ADAM_B2) * (g * g)
        m_hat = m_new / bc1
        v_hat = v_new / bc2
        g_out[...] = g
        d_out[...] = -ADAM_LR * (m_hat / (jnp.sqrt(v_hat) + ADAM_EPS) + ADAM_WD * w_ref[...])
        m_out[...] = m_new
        v_out[...] = v_new

    blk = pl.BlockSpec((tr, c), lambda i: (i, 0))
    return pl.pallas_call(
        body, name=name, grid=(r // tr,),
        in_specs=[pl.BlockSpec((p, tr, c), lambda i: (0, i, 0)), blk, blk, blk] + ([blk] if has_own else []),
        out_specs=(blk, blk, blk, blk), out_shape=tuple(jax.ShapeDtypeStruct((r, c), F32) for _ in range(4)),
        compiler_params=_params(("parallel",)),
    )(*([parts, w, m, v] + ([own] if has_own else [])))


_GELU_C = math.sqrt(2.0 / math.pi)
_GELU_K = 0.044715


def _sigmoid(z):
    return 0.5 * jnp.tanh(0.5 * z) + 0.5


def _silu(z):
    return z * _sigmoid(z)


def _dsilu(z):
    s = _sigmoid(z)
    return s * (1.0 + z * (1.0 - s))


def _gelu(z):
    return 0.5 * z * (1.0 + jnp.tanh(_GELU_C * (z + _GELU_K * z * z * z)))


def _gelu_pair(z):
    zz = z * z
    t = jnp.tanh(_GELU_C * (z * (1.0 + _GELU_K * zz)))
    half = 0.5 * (1.0 + t)
    return z * half, half + 0.5 * z * (1.0 - t * t) * (_GELU_C * (1.0 + 3.0 * _GELU_K * zz))


def _softplus(z):
    return jnp.maximum(z, 0.0) + jnp.log(1.0 + jnp.exp(-jnp.abs(z)))


def _rms(xv):
    return lax.rsqrt(jnp.mean(xv * xv, axis=-1, keepdims=True) + NORM_EPS)


def _rms_bwd(xhat, r, dxhat):
    return r * (dxhat - xhat * jnp.mean(dxhat * xhat, axis=-1, keepdims=True))


def _swap_pairs(y):
    lane = lax.broadcasted_iota(jnp.int32, y.shape, 1)
    return jnp.where(lane % 2 == 0, pltpu.roll(y, LANES - 1, 1), pltpu.roll(y, 1, 1))


def _dot(a, b, ca, cb):
    return lax.dot_general(a, b, (((ca,), (cb,)), ((), ())), preferred_element_type=F32)


_SM_SCALE = HEAD_DIM ** -0.5
_EXP2_SCALE = _SM_SCALE * math.log2(math.e)


def _softmax_parts(qh, kk):
    s = _dot(qh, kk, 1, 1)
    p = jnp.exp2((s - jnp.max(s, axis=-1, keepdims=True)) * _EXP2_SCALE)
    return p, 1.0 / jnp.sum(p, axis=-1, keepdims=True)


def _attn_fwd(q, k, v, *, tq, name):
    s_len, aw = q.shape
    t_len, kvw = k.shape
    gw = Q_GROUP * HEAD_DIM

    def body(q_ref, k_ref, v_ref, o_ref):
        kk, vv = k_ref[...], v_ref[...]
        for g in range(Q_GROUP):
            cols = slice(g * HEAD_DIM, (g + 1) * HEAD_DIM)
            p, inv_l = _softmax_parts(q_ref[:, cols], kk)
            o_ref[:, cols] = (_dot(p.astype(BF16), vv, 1, 0) * inv_l).astype(BF16)

    return pl.pallas_call(
        body, name=name, grid=(kvw // HEAD_DIM, s_len // tq),
        in_specs=[pl.BlockSpec((tq, gw), lambda h, i: (i, h)), pl.BlockSpec((t_len, HEAD_DIM), lambda h, i: (0, h)),
                  pl.BlockSpec((t_len, HEAD_DIM), lambda h, i: (0, h))],
        out_specs=pl.BlockSpec((tq, gw), lambda h, i: (i, h)),
        out_shape=jax.ShapeDtypeStruct((s_len, aw), BF16),
        compiler_params=_params(("parallel", "parallel")),
    )(q, k, v)


def _attn_bwd(q, k, v, do, *, tq, name):
    s_len, aw = q.shape
    t_len, kvw = k.shape
    gw = Q_GROUP * HEAD_DIM

    def body(q_ref, k_ref, v_ref, do_ref, dq_ref, dk_ref, dv_ref):
        @pl.when(pl.program_id(1) == 0)
        def _():
            dk_ref[...] = jnp.zeros_like(dk_ref)
            dv_ref[...] = jnp.zeros_like(dv_ref)

        kk, vv = k_ref[...], v_ref[...]
        for g in range(Q_GROUP):
            cols = slice(g * HEAD_DIM, (g + 1) * HEAD_DIM)
            qh, doh = q_ref[:, cols], do_ref[:, cols]
            p, inv_l = _softmax_parts(qh, kk)
            dp = _dot(doh, vv, 1, 1)
            delta = jnp.sum(p * dp, axis=-1, keepdims=True) * inv_l
            dsb = (p * ((dp - delta) * (inv_l * _SM_SCALE))).astype(BF16)
            dq_ref[:, cols] = _dot(dsb, kk, 1, 0)
            dk_ref[...] += _dot(dsb, qh, 0, 0)
            dv_ref[...] += _dot(p.astype(BF16), (doh * inv_l).astype(BF16), 0, 0)

    qspec = pl.BlockSpec((tq, gw), lambda h, i: (i, h))
    kspec = pl.BlockSpec((t_len, HEAD_DIM), lambda h, i: (0, h))
    return pl.pallas_call(
        body, name=name, grid=(kvw // HEAD_DIM, s_len // tq),
        in_specs=[qspec, kspec, kspec, qspec], out_specs=(qspec, kspec, kspec),
        out_shape=(jax.ShapeDtypeStruct((s_len, aw), F32), jax.ShapeDtypeStruct((t_len, kvw), F32),
                   jax.ShapeDtypeStruct((t_len, kvw), F32)),
        compiler_params=_params(("parallel", "arbitrary")),
    )(q, k, v, do)


SHIFTS = (-2, -1, 1, 2)


def _shift_masks(l_len, t_len):
    t = jnp.arange(t_len)[:, None]
    cols = []
    for d in SHIFTS:
        td = t + d
        ok = ((t < l_len) & (td >= 0) & (td < l_len)) | ((t >= l_len) & (td >= l_len) & (td < t_len))
        cols.append(jnp.broadcast_to(ok.astype(F32), (t_len, RNN_BLOCK)))
    return jnp.concatenate(cols, axis=1)


def _shift_rows(xv, d, m_ref, l_len, t_len):
    if d == 0:
        return xv
    k = SHIFTS.index(d)
    return pltpu.roll(xv, (-d) % t_len, 0) * m_ref[:, k * RNN_BLOCK:(k + 1) * RNN_BLOCK]


def _sublane_scan(a8, b8, rev):
    row = lax.broadcasted_iota(jnp.int32, a8.shape, 0)
    for s in (1, 2, 4):
        if rev:
            ok = row < SUBLANES - s
            sh = SUBLANES - s
        else:
            ok = row >= s
            sh = s
        a_sh = jnp.where(ok, pltpu.roll(a8, sh, 0), 1.0)
        b_sh = jnp.where(ok, pltpu.roll(b8, sh, 0), 0.0)
        b8 = a8 * b_sh + b8
        a8 = a8 * a_sh
    return a8, b8


def _scan_chunks(chains, c0, n, carries):
    def step(kk, hs):
        new = []
        for (a_ref, b_ref, h_ref, rev), hprev in zip(chains, hs):
            c = c0 + (n - 1 - kk if rev else kk)
            r0 = pl.multiple_of(c * SUBLANES, SUBLANES)
            big_a, big_b = _sublane_scan(a_ref[pl.ds(r0, SUBLANES), :], b_ref[pl.ds(r0, SUBLANES), :], rev)
            h8 = big_b + big_a * hprev
            h_ref[pl.ds(r0, SUBLANES), :] = h8
            edge = 0 if rev else SUBLANES - 1
            new.append(jnp.broadcast_to(h8[edge:edge + 1, :], h8.shape))
        return tuple(new)

    return lax.fori_loop(0, n, step, tuple(carries))


def _rglru_conv(x, cw_ref, cb_ref, m_ref, l_len, t_len):
    y = cb_ref[...] + _shift_rows(x, -1, m_ref, l_len, t_len) * cw_ref[0:1, :]
    for kk in range(1, CONV_WIDTH):
        y = y + _shift_rows(x, kk - 1, m_ref, l_len, t_len) * cw_ref[kk:kk + 1, :]
    return y


def _rglru_gates(xc, wr, wi, br, bi, lam):
    xb = xc.astype(BF16)
    r = _sigmoid(_dot(xb, wr.astype(BF16), 1, 0) + br)
    ig = _sigmoid(_dot(xb, wi.astype(BF16), 1, 0) + bi)
    sp = _softplus(-lam)
    log_a = (-LRU_C) * r * sp
    a = jnp.exp(log_a)
    z2 = 2.0 * log_a
    series = z2 * (1.0 + z2 * (0.5 + z2 * (1.0 / 6.0 + z2 * (1.0 / 24.0))))
    one_m = jnp.where(jnp.abs(z2) < 0.03, -series, 1.0 - a * a)
    rinv = lax.rsqrt(jnp.maximum(one_m, 1e-30))
    u = ig * xc
    return r, ig, sp, a, one_m * rinv, u, rinv


def _rglru_specs(proj, xr_blk, xg_blk, t_len):
    col = lambda off: pl.BlockSpec((t_len, RNN_BLOCK), lambda h, off=off: (0, off + h))
    vec = lambda rows: pl.BlockSpec((rows, RNN_BLOCK), lambda h: (0, h))
    wsp = pl.BlockSpec((2, 1, RNN_BLOCK, RNN_BLOCK), lambda h: (0, h, 0, 0))
    msp = pl.BlockSpec((t_len, len(SHIFTS) * RNN_BLOCK), lambda h: (0, 0))
    return col(xr_blk), col(xg_blk), vec, wsp, msp


def _rglru_fwd(proj, xr_blk, xg_blk, masks, conv_w, conv_b, w_rg, w_ig, b_rg, b_ig, lam, *, l_len, name):
    t_len = proj.shape[0]
    s_len = t_len - l_len
    d_rnn = conv_b.shape[1]
    n_l, n_t = l_len // SUBLANES, t_len // SUBLANES

    def body(xr_ref, xg_ref, m_ref, cw_ref, cb_ref, wr_ref, wi_ref, br_ref, bi_ref, lam_ref, rg_ref, hf_ref, hb_ref,
             a_s, b_s):
        xc = _rglru_conv(xr_ref[...], cw_ref, cb_ref, m_ref, l_len, t_len)
        for d in range(2):
            _, _, _, a, mult, u, _ = _rglru_gates(xc, wr_ref[d, 0], wi_ref[d, 0], br_ref[d:d + 1, :],
                                               bi_ref[d:d + 1, :], lam_ref[d:d + 1, :])
            a_s[d] = a
            b_s[d] = mult * u
        chains = [(a_s.at[0], b_s.at[0], hf_ref, False), (a_s.at[1], b_s.at[1], hb_ref, True)]
        zero = jnp.zeros((SUBLANES, RNN_BLOCK), F32)
        carry = _scan_chunks(chains, 0, n_l, (zero, zero))
        _scan_chunks(chains, n_l, n_t - n_l, carry)
        rnn = hf_ref[pl.ds(l_len, s_len), :] + hb_ref[pl.ds(l_len, s_len), :]
        rg_ref[...] = (rnn * _gelu(xg_ref[pl.ds(l_len, s_len), :])).astype(BF16)

    xr_spec, xg_spec, vec, wsp, msp = _rglru_specs(proj, xr_blk, xg_blk, t_len)
    col_t = pl.BlockSpec((t_len, RNN_BLOCK), lambda h: (0, h))
    return pl.pallas_call(
        body, name=name, grid=(d_rnn // RNN_BLOCK,),
        in_specs=[xr_spec, xg_spec, msp, vec(CONV_WIDTH), vec(1), wsp, wsp, vec(2), vec(2), vec(2)],
        out_specs=(pl.BlockSpec((s_len, RNN_BLOCK), lambda h: (0, h)), col_t, col_t),
        out_shape=(jax.ShapeDtypeStruct((s_len, d_rnn), BF16), jax.ShapeDtypeStruct((t_len, d_rnn), F32),
                   jax.ShapeDtypeStruct((t_len, d_rnn), F32)),
        scratch_shapes=[pltpu.VMEM((2, t_len, RNN_BLOCK), F32)] * 2,
        compiler_params=_params(("parallel",)),
    )(proj, proj, masks, conv_w, conv_b, w_rg, w_ig, b_rg, b_ig, lam)


N_RNN_VEC = 16


def _rglru_bwd(proj, xr_blk, xg_blk, masks, h_fwd, h_bwd, drg, conv_w, conv_b, w_rg, w_ig, b_rg, b_ig, lam, *, l_len, name):
    t_len = proj.shape[0]
    s_len = t_len - l_len
    d_rnn = conv_b.shape[1]
    n_l, n_t = l_len // SUBLANES, t_len // SUBLANES

    def body(xr_ref, xg_ref, m_ref, hf_ref, hb_ref, drg_ref, cw_ref, cb_ref, wr_ref, wi_ref, br_ref, bi_ref, lam_ref,
             dxr_ref, dxg_ref, dvec_ref, dwr_ref, dwi_ref, a_s, b_s, h_s):
        t = lax.broadcasted_iota(jnp.int32, (t_len, RNN_BLOCK), 0)
        x = xr_ref[...]
        xc = _rglru_conv(x, cw_ref, cb_ref, m_ref, l_len, t_len)
        xb = xc.astype(BF16)
        gates = []
        for d in range(2):
            gt = _rglru_gates(xc, wr_ref[d, 0], wi_ref[d, 0], br_ref[d:d + 1, :], bi_ref[d:d + 1, :],
                              lam_ref[d:d + 1, :])
            gates.append(gt)
        zero = jnp.zeros((SUBLANES, RNN_BLOCK), F32)
        h0, h1 = hf_ref[...], hb_ref[...]
        xg = xg_ref[...]
        dr = jnp.concatenate([jnp.zeros((l_len, RNN_BLOCK), F32), drg_ref[...].astype(F32)], axis=0)
        gel, dgel = _gelu_pair(xg)
        dxg_ref[...] = (dr * (h0 + h1) * dgel).astype(dxg_ref.dtype)
        drnn = dr * gel
        a_s[0] = pltpu.roll(gates[0][3], t_len - 1, 0)
        a_s[1] = pltpu.roll(gates[1][3], 1, 0)
        b_s[0] = drnn
        b_s[1] = drnn
        hprev = [jnp.where(t == 0, 0.0, pltpu.roll(h0, 1, 0)),
                 jnp.where(t == l_len - 1, 0.0, pltpu.roll(h1, t_len - 1, 0))]
        chains = [(a_s.at[0], b_s.at[0], h_s.at[0], True), (a_s.at[1], b_s.at[1], h_s.at[1], False)]
        carry = _scan_chunks(chains, n_l, n_t - n_l, (zero, zero))
        _scan_chunks(chains, 0, n_l, carry)

        dxc = jnp.zeros((t_len, RNN_BLOCK), F32)
        for d in range(2):
            r, ig, sp, a, mult, u, rinv = gates[d]
            lamb = h_s[d]
            da = lamb * hprev[d]
            dmult = lamb * u
            du = lamb * mult
            dlog_a = da * a - dmult * (a * a) * rinv
            dzr = (dlog_a * ((-LRU_C) * sp)) * r * (1.0 - r)
            dsp = jnp.sum(dlog_a * ((-LRU_C) * r), axis=0, keepdims=True)
            dzi = (du * xc) * ig * (1.0 - ig)
            dzrb, dzib = dzr.astype(BF16), dzi.astype(BF16)
            dxc = dxc + du * ig + _dot(dzrb, wr_ref[d, 0].astype(BF16), 1, 1) + _dot(dzib, wi_ref[d, 0].astype(BF16), 1, 1)
            dwr_ref[d, 0] = _dot(xb, dzrb, 0, 0)
            dwi_ref[d, 0] = _dot(xb, dzib, 0, 0)
            dvec_ref[5 + d:6 + d, :] = jnp.sum(dzr, axis=0, keepdims=True)
            dvec_ref[7 + d:8 + d, :] = jnp.sum(dzi, axis=0, keepdims=True)
            dvec_ref[9 + d:10 + d, :] = dsp * (-_sigmoid(-lam_ref[d:d + 1, :]))
        dvec_ref[4:5, :] = jnp.sum(dxc, axis=0, keepdims=True)
        dvec_ref[11:N_RNN_VEC, :] = jnp.zeros((N_RNN_VEC - 11, RNN_BLOCK), F32)
        dxr = jnp.zeros((t_len, RNN_BLOCK), F32)
        for kk in range(CONV_WIDTH):
            dvec_ref[kk:kk + 1, :] = jnp.sum(dxc * _shift_rows(x, kk - 1, m_ref, l_len, t_len), axis=0, keepdims=True)
            dxr = dxr + _shift_rows(dxc, 1 - kk, m_ref, l_len, t_len) * cw_ref[kk:kk + 1, :]
        dxr_ref[...] = dxr.astype(dxr_ref.dtype)

    xr_spec, xg_spec, vec, wsp, msp = _rglru_specs(proj, xr_blk, xg_blk, t_len)
    col_t = pl.BlockSpec((t_len, RNN_BLOCK), lambda h: (0, h))
    return pl.pallas_call(
        body, name=name, grid=(d_rnn // RNN_BLOCK,),
        in_specs=[xr_spec, xg_spec, msp, col_t, col_t, pl.BlockSpec((s_len, RNN_BLOCK), lambda h: (0, h)), vec(CONV_WIDTH),
                  vec(1),
                  wsp, wsp, vec(2), vec(2), vec(2)],
        out_specs=(col_t, col_t, vec(N_RNN_VEC), wsp, wsp),
        out_shape=(jax.ShapeDtypeStruct((t_len, d_rnn), BF16), jax.ShapeDtypeStruct((t_len, d_rnn), BF16),
                   jax.ShapeDtypeStruct((N_RNN_VEC, d_rnn), F32), jax.ShapeDtypeStruct(w_rg.shape, F32),
                   jax.ShapeDtypeStruct(w_ig.shape, F32)),
        scratch_shapes=[pltpu.VMEM((2, t_len, RNN_BLOCK), F32)] * 3,
        compiler_params=_params(("parallel",)),
    )(proj, proj, masks, h_fwd, h_bwd, drg, conv_w, conv_b, w_rg, w_ig, b_rg, b_ig, lam)


def _colsum(v):
    return jnp.sum(v, axis=0, keepdims=True)


def _qk_fwd_heads(xv, gain, cos_e, sin_e):
    outs = []
    for h in range(xv.shape[1] // HEAD_DIM):
        xh = xv[:, h * HEAD_DIM:(h + 1) * HEAD_DIM]
        y = xh * _rms(xh) * gain
        outs.append(y * cos_e + _swap_pairs(y) * sin_e)
    return outs[0] if len(outs) == 1 else jnp.concatenate(outs, axis=1)


def _qk_bwd_heads(xv, dov, gain, cos_e, sin_e):
    outs = []
    dgain = jnp.zeros((1, HEAD_DIM), F32)
    for h in range(xv.shape[1] // HEAD_DIM):
        cols = slice(h * HEAD_DIM, (h + 1) * HEAD_DIM)
        xh, doh = xv[:, cols], dov[:, cols]
        r = _rms(xh)
        xhat = xh * r
        dy = doh * cos_e + _swap_pairs(doh * sin_e)
        dgain = dgain + _colsum(dy * xhat)
        outs.append(_rms_bwd(xhat, r, dy * gain))
    return (outs[0] if len(outs) == 1 else jnp.concatenate(outs, axis=1)), dgain


WEIGHTS = ['c_ctx', 'w_mod', 'b_mod', 'g_mix', 'g_mlp', 'w_in', 'q_gain', 'k_gain', 'conv_w', 'conv_b', 'w_rg', 'b_rg',
           'w_ig', 'b_ig', 'lru_lambda', 'w_o_attn', 'w_o_rnn', 'w_out', 'w_up', 'w_down', 'g_final']
REPLICATED = ['c_ctx', 'b_mod', 'g_mix', 'g_mlp', 'q_gain', 'k_gain', 'conv_b', 'w_rg', 'w_ig', 'g_final']
SMALL_SHARDED = [('conv_w', CONV_WIDTH), ('b_rg', 2), ('b_ig', 2), ('lru_lambda', 2)]
SMALL_ROWS = 16


PACK_ROWS = 64


def _pack_flat(arrs, rows_mult):
    flat = jnp.concatenate([a.reshape(-1) for a in arrs])
    n = flat.shape[0]
    unit = rows_mult * LANES
    pad = (-n) % unit
    if pad:
        flat = jnp.concatenate([flat, jnp.zeros((pad,), flat.dtype)])
    return flat.reshape(-1, LANES)


def _unpack_flat(packed, shapes):
    flat = packed.reshape(-1)
    out, pos = [], 0
    for shp in shapes:
        n = math.prod(shp)
        out.append(flat[pos:pos + n].reshape(shp))
        pos += n
    return out


def _pack_small(d):
    rows = [d[n][0] for n, _ in SMALL_SHARDED]
    cols = rows[0].shape[1]
    used = sum(k for _, k in SMALL_SHARDED)
    return jnp.concatenate(rows + [jnp.zeros((SMALL_ROWS - used, cols), F32)], axis=0)


def _unpack_small(p):
    out, pos = {}, 0
    for n, k in SMALL_SHARDED:
        out[n] = p[pos:pos + k][None]
        pos += k
    return out


def kernel(x, c, ctx, c_ctx, w_mod, b_mod, g_mix, g_mlp, w_in, q_gain, k_gain, conv_w, conv_b, w_rg, b_rg, w_ig, b_ig, lru_lambda, w_o_attn, w_o_rnn, w_out, w_up, w_down, g_final, loss_target, m_c_ctx, m_w_mod, m_b_mod, m_g_mix, m_g_mlp, m_w_in, m_q_gain, m_k_gain, m_conv_w, m_conv_b, m_w_rg, m_b_rg, m_w_ig, m_b_ig, m_lru_lambda, m_w_o_attn, m_w_o_rnn, m_w_out, m_w_up, m_w_down, m_g_final, v_c_ctx, v_w_mod, v_b_mod, v_g_mix, v_g_mlp, v_w_in, v_q_gain, v_k_gain, v_conv_w, v_conv_b, v_w_rg, v_b_rg, v_w_ig, v_b_ig, v_lru_lambda, v_w_o_attn, v_w_o_rnn, v_w_out, v_w_up, v_w_down, v_g_final):
    wd = dict(c_ctx=c_ctx, w_mod=w_mod, b_mod=b_mod, g_mix=g_mix, g_mlp=g_mlp, w_in=w_in, q_gain=q_gain, k_gain=k_gain,
              conv_w=conv_w, conv_b=conv_b, w_rg=w_rg, b_rg=b_rg, w_ig=w_ig, b_ig=b_ig, lru_lambda=lru_lambda,
              w_o_attn=w_o_attn, w_o_rnn=w_o_rnn, w_out=w_out, w_up=w_up, w_down=w_down, g_final=g_final)
    md = dict(zip(WEIGHTS, (m_c_ctx, m_w_mod, m_b_mod, m_g_mix, m_g_mlp, m_w_in, m_q_gain, m_k_gain, m_conv_w, m_conv_b,
                            m_w_rg, m_b_rg, m_w_ig, m_b_ig, m_lru_lambda, m_w_o_attn, m_w_o_rnn, m_w_out, m_w_up,
                            m_w_down, m_g_final)))
    vd = dict(zip(WEIGHTS, (v_c_ctx, v_w_mod, v_b_mod, v_g_mix, v_g_mlp, v_w_in, v_q_gain, v_k_gain, v_conv_w, v_conv_b,
                            v_w_rg, v_b_rg, v_w_ig, v_b_ig, v_lru_lambda, v_w_o_attn, v_w_o_rnn, v_w_out, v_w_up,
                            v_w_down, v_g_final)))

    xs, tgt, ctxs = x[0], loss_target[0], ctx[0]
    s_len, d = xs.shape
    l_len = ctxs.shape[0]
    t_len = s_len + l_len
    n_in = w_in.shape[2] * N_DEV
    aw = (n_in - 4 * d) * 2 // 3
    kvw = aw // Q_GROUP
    assert aw == d and n_in == aw + 2 * kvw + 4 * d
    nmod = N_MOD * d // N_DEV
    tr = min(256, l_len)
    assert l_len % tr == 0 and s_len % tr == 0
    lb = l_len // tr
    tq = min(256, s_len)
    my = _lin(_my_pos())
    c_ctx2 = c_ctx[None]
    g_final2 = g_final[None]

    c_all = _all_gather([jnp.broadcast_to(c, (SUBLANES, d))], [0], "ag_cond")[0][::SUBLANES]
    c16 = jnp.concatenate([c_all, jnp.broadcast_to(c_ctx2, (N_DEV, d))], axis=0)
    b_mod_sh = lax.dynamic_slice(b_mod, (0, my * nmod), (1, nmod))
    mod_part = _mm(c16, w_mod[0], a_fn=_silu, bias=b_mod_sh, name="mod_fwd")
    sp_w = _pack_small(wd)
    (mod_g, sp_f), order = _all_gather([mod_part, sp_w], [1, 1], "ag_mod", with_token=True)
    conv_w_f, b_rg_f, b_ig_f, lam_f = sp_f[0:4], sp_f[4:6], sp_f[6:8], sp_f[8:10]
    mod_lat = lax.dynamic_slice(mod_g, (my, 0), (1, N_MOD * d))
    sh_a, sc_a, ga_a, sh_f, sc_f, ga_f = [mod_lat[:, k * d:(k + 1) * d] for k in range(N_MOD)]
    csh_a, csc_a = mod_g[N_DEV:N_DEV + 1, :d], mod_g[N_DEV:N_DEV + 1, d:2 * d]

    def nm1_fwd(i, j, cv, lv, g, csh, csc, sh, sc):
        is_ctx = i < lb
        xv = jnp.where(is_ctx, cv, lv)
        n = xv * _rms(xv) * g
        return n * (1.0 + jnp.where(is_ctx, csc, sc)) + jnp.where(is_ctx, csh, sh)

    h_all = _rowwise(nm1_fwd, [_row(ctxs), _row(xs, 0, -lb), _bc(g_mix), _bc(csh_a), _bc(csc_a), _bc(sh_a), _bc(sc_a)],
                     [("row", BF16)], nr=t_len // tr, tr=tr, nc=1, cw=d, name="norm1_fwd")[0]
    w_in_f, proj, order = _gather_proj(h_all, _tok(w_in[0], order).astype(BF16), "ag_proj")
    late_axis = dict(w_o_attn=0, w_o_rnn=0, w_out=0, w_up=1, w_down=0)
    placed = [_place_block(wd[n][0], ax, my, "place_" + n) for n, ax in late_axis.items()]
    late_kind = lambda n: "gather_chips" if n in ("w_up", "w_down") else "gather"
    late = {}
    for (n, ax), full in zip(late_axis.items(), placed):
        late[n] = _xfer_start(late_kind(n), order, full, ax, "ag_start_" + n)
        order = late[n][4]
    q_gain_fwd = _tok(q_gain, order)

    late_wait = lambda n, after: _xfer_wait(late_kind(n), late[n], late_axis[n], after, "ag_wait_" + n)

    def sibling_start(n, after):
        arrived = late_wait(n, after)
        return _xfer_start("sibling", jnp.zeros((SUBLANES, LANES), F32), arrived, late_axis[n], "ag_sib_start_" + n)

    sibling_wait = lambda n, handle, after: _xfer_wait("sibling", handle, late_axis[n], after, "ag_sib_wait_" + n)

    pos = jnp.arange(s_len)
    n_freq = HEAD_DIM // 4
    inv_freq = ROPE_THETA ** (-jnp.arange(n_freq, dtype=F32) / n_freq)
    ang = jnp.concatenate([(pos // GRID_W).astype(F32)[:, None] * inv_freq,
                           (pos % GRID_W).astype(F32)[:, None] * inv_freq], axis=-1)
    sign = jnp.where(jnp.arange(HEAD_DIM) % 2 == 0, -1.0, 1.0).astype(F32)
    cos_e = jnp.concatenate([jnp.ones((l_len, HEAD_DIM), F32), jnp.repeat(jnp.cos(ang), 2, axis=-1)], axis=0)
    sin_e = jnp.concatenate([jnp.zeros((l_len, HEAD_DIM), F32), jnp.repeat(jnp.sin(ang), 2, axis=-1) * sign], axis=0)


    kb = aw // kvw
    q = _rowwise(lambda i, j, xv, gain, ce, se: _qk_fwd_heads(xv, gain, ce, se),
                 [_row(proj, 0, lb, aw), _full(q_gain_fwd), _row(cos_e, 0, lb, HEAD_DIM), _row(sin_e, 0, lb, HEAD_DIM)],
                 [("row", BF16, aw)], nr=s_len // tr, tr=tr, nc=1, cw=aw, name="q_fwd")[0]
    k_all, v_all = _rowwise(lambda i, j, kv_, vv_, gain, ce, se: (_qk_fwd_heads(kv_, gain, ce, se), vv_),
                            [_row(proj, kb, 0, kvw), _row(proj, kb + 1, 0, kvw), _full(k_gain),
                             _row(cos_e, 0, 0, HEAD_DIM), _row(sin_e, 0, 0, HEAD_DIM)],
                            [("row", BF16, kvw), ("row", BF16, kvw)], nr=t_len // tr, tr=tr, nc=1, cw=kvw, name="kv_fwd")
    attn_o = _attn_fwd(q, k_all, v_all, tq=tq, name="attn_fwd")

    xr_blk = (aw + 2 * kvw) // RNN_BLOCK
    xg_blk = xr_blk + d // RNN_BLOCK
    shift_masks = _shift_masks(l_len, t_len)
    rg, h_fwd, h_bwd = _rglru_fwd(proj, xr_blk, xg_blk, shift_masks, conv_w_f, conv_b, w_rg[0], w_ig[0], b_rg_f, b_ig_f, lam_f, l_len=l_len,
                    name="rglru_fwd")

    w_oa_f = late_wait("w_o_attn", attn_o)
    y_attn = _mm(attn_o, w_oa_f, name="o_attn_fwd")
    w_or_f = late_wait("w_o_rnn", rg)
    y_rnn = _mm(rg, w_or_f, name="o_rnn_fwd")
    hw = d // 2
    glb = (aw + 2 * kvw + 2 * d) // hw

    def merge_fwd(i, j, ya, yr, gla, glr):
        return _sigmoid(gla) * ya + _sigmoid(glr) * yr

    u = _rowwise(merge_fwd, [_row(y_attn), _row(y_rnn), _row(proj, glb, lb), _row(proj, glb + 2, lb)],
                 [("row", BF16)], nr=s_len // tr, tr=tr, nc=2, cw=hw, name="merge_fwd")[0]
    w_out_f = late_wait("w_out", u)
    sib_up = sibling_start("w_up", u)
    mix = _mm(u, w_out_f, name="out_fwd")

    def res1_fwd(i, j, xv, mixv, gaa, gm, shf, scf):
        x1v = xv + gaa * mixv
        n = x1v * _rms(x1v) * gm
        return x1v, n * (1.0 + scf) + shf

    x1, h2 = _rowwise(res1_fwd, [_row(xs), _row(mix), _bc(_tok(ga_a, sib_up[4])), _bc(g_mlp), _bc(sh_f), _bc(sc_f)],
                      [("row", F32), ("row", BF16)], nr=s_len // tr, tr=tr, nc=1, cw=d, name="norm2_fwd")
    relu2 = lambda z: jnp.square(jnp.maximum(z.astype(F32), 0.0))
    w_up_f = sibling_wait("w_up", sib_up, h2)
    up = _mm(h2, w_up_f, out_dtype=BF16, name="up_fwd")
    w_down_f = sibling_wait("w_down", sibling_start("w_down", up), up)
    down = _mm(up, w_down_f, a_fn=relu2, name="down_fwd")

    def final_fn(i, j, x1v, dn, tg, gaf, gf):
        x2 = x1v + gaf * dn
        r = _rms(x2)
        xh = x2 * r
        e = xh * gf - tg
        dy = e * (1.0 / d)
        dx2 = _rms_bwd(xh, r, dy * gf)
        return dx2, dx2 * gaf, _colsum(e * e) * (0.5 / d), _colsum(dy * xh), _colsum(dx2 * dn)

    dx2, ddown, loss_cols, g_gfinal, d_ga_f = _rowwise(
        final_fn, [_row(x1), _row(down), _row(tgt), _bc(ga_f), _bc(g_final2)],
        [("row", F32), ("row", BF16), ("acc", 1), ("acc", 1), ("acc", 1)], nr=s_len // tr, tr=tr, nc=1, cw=d,
        name="final_bwd")
    loss = lax.psum(jnp.sum(loss_cols), ("x", "y", "c"))

    dup = _mm(ddown, w_down_f, tb=True, out_fn=lambda r, upv: r * (2.0 * jnp.maximum(upv.astype(F32), 0.0)), extras=[up],
              out_dtype=BF16, name="down_bwd_x")
    gw_down = _mm(up, ddown, ta=True, a_fn=relu2, out_dtype=BF16, name="down_bwd_w")
    sent = dict(w_down=_scatter_start(gw_down, 0, my, "rs_start_w_down"))
    dh2 = _mm(dup, w_up_f, tb=True, name="up_bwd_x")
    gw_up = _mm(h2, dup, ta=True, out_dtype=BF16, name="up_bwd_w")
    sent["w_up"] = _scatter_start(gw_up, 1, my, "rs_start_w_up")
    sc_f_bwd = _tok(sc_f, sent["w_down"][0][4], sent["w_up"][0][4])

    def norm2_bwd(i, j, x1v, mixv, dh2v, dx2v, gaa, gm, scf):
        r = _rms(x1v)
        xh = x1v * r
        n = xh * gm
        dn = dh2v * (1.0 + scf)
        dx1 = dx2v + _rms_bwd(xh, r, dn * gm)
        return dx1, dx1 * gaa, _colsum(dh2v), _colsum(dh2v * n), _colsum(dn * xh), _colsum(dx1 * mixv)

    dx1, dmix, d_sh_f, d_sc_f, g_gmlp, d_ga_a = _rowwise(
        norm2_bwd, [_row(x1), _row(mix), _row(dh2), _row(dx2), _bc(ga_a), _bc(g_mlp), _bc(sc_f_bwd)],
        [("row", F32), ("row", BF16), ("acc", 1), ("acc", 1), ("acc", 1), ("acc", 1)], nr=s_len // tr, tr=tr, nc=1,
        cw=d, name="norm2_bwd")

    du = _mm(dmix, w_out_f, tb=True, name="out_bwd_x")
    gw_out = _mm(u, dmix, ta=True, out_dtype=BF16, name="out_bwd_w")
    sent["w_out"] = _scatter_start(gw_out, 0, my, "rs_start_w_out")

    def merge_bwd(i, j, duv, ya, yr, gla, glr):
        ga, gr = _sigmoid(gla), _sigmoid(glr)
        return duv * ga, duv * gr, duv * ya * ga * (1.0 - ga), duv * yr * gr * (1.0 - gr)

    dya, dyr, dgla, dglr = _rowwise(
        merge_bwd, [_row(du), _row(y_attn), _row(y_rnn), _row(proj, glb, lb), _row(proj, glb + 2, lb)],
        [("row", BF16)] * 4, nr=s_len // tr, tr=tr, nc=2, cw=hw, name="merge_bwd")
    dattn = _mm(dya, w_oa_f, tb=True, out_dtype=BF16, name="o_attn_bwd_x")
    gw_oa = _mm(attn_o, dya, ta=True, out_dtype=BF16, name="o_attn_bwd_w")
    drg = _mm(dyr, w_or_f, tb=True, out_dtype=BF16, name="o_rnn_bwd_x")
    gw_or = _mm(rg, dyr, ta=True, out_dtype=BF16, name="o_rnn_bwd_w")
    sent["w_o_attn"] = _scatter_start(gw_oa, 0, my, "rs_start_w_o_attn")
    sent["w_o_rnn"] = _scatter_start(gw_or, 0, my, "rs_start_w_o_rnn")
    conv_b_bwd = _tok(conv_b, sent["w_out"][0][4], sent["w_o_attn"][0][4], sent["w_o_rnn"][0][4])

    dq, dk, dv = _attn_bwd(q, k_all, v_all, dattn, tq=tq, name="attn_bwd")
    dxr, dxg, dvec, g_wrg, g_wig = _rglru_bwd(proj, xr_blk, xg_blk, shift_masks, h_fwd, h_bwd, drg, conv_w_f, conv_b_bwd, w_rg[0], w_ig[0], b_rg_f,
                                               b_ig_f, lam_f, l_len=l_len, name="rglru_bwd")
    gate_part = _pack_flat([g_wrg, g_wig], N_DEV * SUBLANES)
    gate_sent = _scatter_start(gate_part, 0, my, "rs_start_gates")
    q_gain_bwd = _tok(q_gain, gate_sent[0][4])

    dq_p, g_qgain = _rowwise(lambda i, j, xv, dov, gain, ce, se: _qk_bwd_heads(xv, dov, gain, ce, se),
                             [_row(proj, 0, lb, aw), _row(dq, 0, 0, aw), _full(q_gain_bwd), _row(cos_e, 0, lb, HEAD_DIM),
                              _row(sin_e, 0, lb, HEAD_DIM)],
                             [("row", BF16, aw), ("acc", 1, HEAD_DIM)], nr=s_len // tr, tr=tr, nc=1, cw=aw, name="q_bwd")

    def kv_bwd(i, j, xv, dkv, dvv, gain, ce, se):
        dk_p, dgain = _qk_bwd_heads(xv, dkv, gain, ce, se)
        return dk_p, dvv, dgain

    dk_p, dv_p, g_kgain = _rowwise(kv_bwd, [_row(proj, kb, 0, kvw), _row(dk, 0, 0, kvw), _row(dv, 0, 0, kvw),
                                            _full(k_gain), _row(cos_e, 0, 0, HEAD_DIM), _row(sin_e, 0, 0, HEAD_DIM)],
                                   [("row", BF16, kvw), ("row", BF16, kvw), ("acc", 1, HEAD_DIM)], nr=t_len // tr, tr=tr,
                                   nc=1, cw=kvw, name="kv_bwd")

    def dproj_fn(i, j, dqv, dkv, dvv, dxrv, dxgv, dgav, dgrv):
        zero_ctx = lambda a: jnp.where(i < lb, jnp.zeros_like(a), a)
        return jnp.concatenate([zero_ctx(dqv), dkv, dvv, dxrv, dxgv, zero_ctx(dgav), zero_ctx(dgrv)], axis=1)

    dproj = _rowwise(dproj_fn, [_row(dq_p, 0, -lb, aw), _row(dk_p, 0, 0, kvw), _row(dv_p, 0, 0, kvw), _row(dxr, 0, 0, d),
                                _row(dxg, 0, 0, d), _row(dgla, 0, -lb, d), _row(dglr, 0, -lb, d)],
                     [("row", BF16, n_in)], nr=t_len // tr, tr=tr, nc=1, cw=n_in, name="dproj_pack")[0]
    dh_all = _mm(dproj, w_in_f, tb=True, name="proj_bwd_x")
    g_mix_bwd = g_mix

    def norm1_bwd(i, j, xv, dhv, dresv, g, sc):
        r = _rms(xv)
        xh = xv * r
        dn = dhv * (1.0 + sc)
        return dresv + _rms_bwd(xh, r, dn * g), _colsum(dhv), _colsum(dhv * (xh * g)), _colsum(dn * xh)

    def norm1_bwd_ctx(i, j, xv, dhv, g, sc):
        xh = xv * _rms(xv)
        return _colsum(dhv), _colsum(dhv * (xh * g)), _colsum(dhv * (1.0 + sc) * xh)

    dx_lat, d_sh_a, d_sc_a, g_gmix_lat = _rowwise(
        norm1_bwd, [_row(xs), _row(dh_all, 0, lb), _row(dx1), _bc(g_mix_bwd), _bc(sc_a)],
        [("row", F32)] + [("acc", 1)] * 3, nr=s_len // tr, tr=tr, nc=1, cw=d, name="norm1_bwd")
    d_csh, d_csc, g_gmix_ctx = _rowwise(
        norm1_bwd_ctx, [_row(ctxs), _row(dh_all), _bc(g_mix_bwd), _bc(csc_a)],
        [("acc", 1)] * 3, nr=lb, tr=tr, nc=1, cw=d, name="norm1_bwd_ctx")
    g_gmix = g_gmix_lat + g_gmix_ctx
    grad_x = dx_lat[None]

    dmod_lat = jnp.concatenate([d_sh_a, d_sc_a, d_ga_a, d_sh_f, d_sc_f, d_ga_f], axis=1)
    dmod_ctx = jnp.concatenate([d_csh, d_csc, jnp.zeros((1, (N_MOD - 2) * d), F32)], axis=1)
    dm_g = _all_gather([jnp.concatenate([dmod_lat, dmod_ctx, jnp.zeros((SUBLANES - 2, N_MOD * d), F32)], axis=0)], [0],
                       "ag_dmod")[0]
    dm16 = jnp.concatenate([dm_g[0::SUBLANES], dm_g[1::SUBLANES]], axis=0)
    dm_sh = lax.dynamic_slice(dm16, (0, my * nmod), (2 * N_DEV, nmod))
    gw_mod = _mm(c16, dm_sh, ta=True, a_fn=_silu, name="mod_bwd_w")
    dc16 = _mm(dm_sh, w_mod[0], tb=True, name="mod_bwd_x")

    def cctx_fn(i, j, pv, cc, dl, dc):
        row = lax.broadcasted_iota(jnp.int32, pv.shape, 0)
        return _colsum(jnp.where(row >= N_DEV, pv, 0.0)) * _dsilu(cc), dl + dc

    g_cctx, g_bmod = _rowwise(cctx_fn, [_full(dc16), _full(c_ctx2), _full(dmod_lat), _full(dmod_ctx)],
                              [("row", F32, d), ("row", F32, N_MOD * d)], nr=1, tr=1, nc=1, cw=d, name="cctx_bwd")

    rep_part = dict(c_ctx=g_cctx, b_mod=g_bmod, g_mix=g_gmix, g_mlp=g_gmlp, q_gain=g_qgain, k_gain=g_kgain,
                    conv_b=dvec[4:5], g_final=g_gfinal)
    rep_small = [n for n in REPLICATED if n in rep_part]
    gates = [n for n in REPLICATED if n not in rep_part]
    part_pack = _pack_flat([rep_part[n] for n in rep_small], PACK_ROWS)
    rep_rows = part_pack.shape[0]
    gate_land = _xfer_wait("scatter", gate_sent[0], 0, g_bmod, "rs_wait_gates")
    gate_sum = _sum_parts(gate_sent[1], gate_land, name="gate_sum")
    (gate_full, parts), order = _all_gather([gate_sum, part_pack], [0, 0], "ag_rep_grads", with_token=True)
    small_g = jnp.concatenate([dvec[0:4], dvec[5:11], jnp.zeros((SMALL_ROWS - 10, d), F32)], axis=0)
    (small_recv,), order = _all_to_all([_tok(small_g, order)], [1], "a2a_small_grads")

    halves = {}
    for part, lo in (("top", 0), ("bot", d // 2)):
        gw_part = _mm(h_all, dproj, ta=True, a_cols=(lo, d // 2), out_dtype=BF16, name="proj_bwd_w_" + part,
                      bias=_tok(jnp.zeros((1, n_in), F32), order))
        halves[part] = _chip_scatter_start(_pair_reduce(gw_part, my, "pair_w_in_" + part), my, "rs_start_w_in_" + part)
        order = halves[part][0][4]
    started = order

    res = {}
    for names, pr, mult, nm in ((rep_small, parts.reshape(N_DEV, rep_rows, LANES), PACK_ROWS, "adamw_rep"),
                                (gates, gate_full[None], N_DEV * SUBLANES, "adamw_gates")):
        rep_out = _adamw(pr, _tok(_pack_flat([wd[n] for n in names], mult), started),
                         _pack_flat([md[n] for n in names], mult), _pack_flat([vd[n] for n in names], mult), name=nm)
        for kind, packed in zip(("g", "d", "m", "v"), rep_out):
            for n, a in zip(names, _unpack_flat(packed, [wd[n].shape for n in names])):
                res[kind, n] = a
    out = _adamw(small_recv, _tok(sp_w, started), _pack_small(md), _pack_small(vd), name="adamw_small")
    for kind, a in zip(("g", "d", "m", "v"), out):
        for n, piece in _unpack_small(a).items():
            res[kind, n] = piece

    after = started
    for n, (handle, own, ax, kind) in sent.items():
        land = _xfer_wait(kind, handle, ax, after, "rs_wait_" + n)
        out = _adamw(land, wd[n][0], md[n][0], vd[n][0], own=own, name="adamw_" + n)
        after = out[0]
        for kind, a in zip(("g", "d", "m", "v"), out):
            res[kind, n] = a[None]
    out = _adamw(gw_mod[None], w_mod[0], m_w_mod[0], v_w_mod[0], name="adamw_w_mod")
    after = out[0]
    for kind, a in zip(("g", "d", "m", "v"), out):
        res[kind, 'w_mod'] = a[None]
    lands, owns = [], []
    for part, (handle, own, ax, kind) in halves.items():
        lands.append(_xfer_wait(kind, handle, ax, after, "rs_wait_w_in_" + part))
        owns.append(own)
        after = lands[-1]
    out = _adamw(jnp.concatenate(lands, axis=1), w_in[0], m_w_in[0], v_w_in[0], own=jnp.concatenate(owns, axis=0),
                 name="adamw_w_in")
    for kind, a in zip(("g", "d", "m", "v"), out):
        res[kind, 'w_in'] = a[None]

    outs = [loss, grad_x]
    for kind in ("g", "d", "m", "v"):
        outs += [res[kind, n].reshape(wd[n].shape) for n in WEIGHTS]
    return tuple(outs)
```
